```python
import math
import jax, jax.numpy as jnp
from jax import lax
import numpy as np

D_MODEL = 2048
BATCH = 8
SEQ = 8192
DEPTH = 2

CHUNK = 64
N_MIXERS = 2
EPS = 1e-6

A_HEADS = 16
A_HEAD_DIM = 128
A_INNER = A_HEADS * A_HEAD_DIM
CONV_K = 4

B_HEADS = 16
B_HEAD_DIM = 128
B_INNER = B_HEADS * B_HEAD_DIM
LEFT_CHUNKS = 8
BAND = (LEFT_CHUNKS + 1) * CHUNK
REL_CLIP = 256

kernel_name = "hybrid_gdn_chunkband_stream"


def rms_norm(x, w):
    xf = x.astype(jnp.float32)
    y = xf * lax.rsqrt(jnp.mean(xf * xf, axis=-1, keepdims=True) + EPS)
    return (y * w.astype(jnp.float32)).astype(x.dtype)


def l2_norm(x):
    xf = x.astype(jnp.float32)
    return xf * lax.rsqrt(jnp.sum(xf * xf, axis=-1, keepdims=True) + EPS)


def causal_depthwise_conv(x, w):
    c = x.shape[-1]
    return lax.conv_general_dilated(
        x, w[:, None, :].astype(x.dtype), window_strides=(1,),
        padding=[(CONV_K - 1, 0)], dimension_numbers=("NWC", "WIO", "NWC"),
        feature_group_count=c)


def gated_deltanet(h, w_in, conv_w, a_log, dt_bias, out_norm_w, w_out):
    bsz, t, _ = h.shape
    nc = t // CHUNK
    proj = h @ w_in.astype(h.dtype)
    qkv = proj[..., :3 * A_INNER]
    z = proj[..., 3 * A_INNER:4 * A_INNER]
    a_in = proj[..., 4 * A_INNER:4 * A_INNER + A_HEADS]
    b_in = proj[..., 4 * A_INNER + A_HEADS:]
    qkv = jax.nn.silu(causal_depthwise_conv(qkv, conv_w))
    q = qkv[..., :A_INNER].reshape(bsz, t, A_HEADS, A_HEAD_DIM)
    k = qkv[..., A_INNER:2 * A_INNER].reshape(bsz, t, A_HEADS, A_HEAD_DIM)
    v = qkv[..., 2 * A_INNER:].reshape(bsz, t, A_HEADS, A_HEAD_DIM).astype(jnp.float32)
    q = l2_norm(q) * (A_HEAD_DIM ** -0.5)
    k = l2_norm(k)
    beta = jax.nn.sigmoid(b_in.astype(jnp.float32))
    g = -jnp.exp(a_log.astype(jnp.float32)) * jax.nn.softplus(
        a_in.astype(jnp.float32) + dt_bias.astype(jnp.float32))

    def to_chunks(u):
        u = u.reshape((bsz, nc, CHUNK) + u.shape[2:])
        return jnp.moveaxis(u, 3, 1)

    q, k, v = to_chunks(q), to_chunks(k), to_chunks(v)
    beta, g = to_chunks(beta), to_chunks(g)
    gc = jnp.cumsum(g, axis=-1)
    tri_incl = jnp.tril(jnp.ones((CHUNK, CHUNK), dtype=bool))
    tri_strict = jnp.tril(jnp.ones((CHUNK, CHUNK), dtype=bool), k=-1)
    diff = gc[..., :, None] - gc[..., None, :]
    decay = jnp.exp(jnp.where(tri_incl, diff, -jnp.inf))

    k_beta = k * beta[..., None]
    v_beta = v * beta[..., None]
    lower = jnp.where(tri_strict, jnp.einsum('bhncd,bhnsd->bhncs', k_beta, k) * decay, 0.0)
    eye = jnp.eye(CHUNK, dtype=jnp.float32)
    rhs = jnp.concatenate([v_beta, k_beta * jnp.exp(gc)[..., None]], axis=-1)
    sol = lax.linalg.triangular_solve(eye + lower, rhs, left_side=True, lower=True,
                                      unit_diagonal=True)
    u = sol[..., :A_HEAD_DIM]
    w = sol[..., A_HEAD_DIM:]
    qk = jnp.einsum('bhncd,bhnsd->bhncs', q, k) * decay
    q_dec = q * jnp.exp(gc)[..., None]
    k_dec = k * jnp.exp(gc[..., -1:] - gc)[..., None]
    g_last = jnp.exp(gc[..., -1])

    def step(state, inp):
        qk_n, q_dec_n, k_dec_n, u_n, w_n, gl_n = inp
        v_new = u_n - jnp.einsum('bhcd,bhdv->bhcv', w_n, state)
        o_n = (jnp.einsum('bhcd,bhdv->bhcv', q_dec_n, state)
               + jnp.einsum('bhcs,bhsv->bhcv', qk_n, v_new))
        state = state * gl_n[..., None, None] + jnp.einsum('bhcd,bhcv->bhdv', k_dec_n, v_new)
        return state, o_n

    xs = tuple(jnp.moveaxis(a, 2, 0) for a in (qk, q_dec, k_dec, u, w, g_last))
    s0 = jnp.zeros((bsz, A_HEADS, A_HEAD_DIM, A_HEAD_DIM), jnp.float32)
    _, o = lax.scan(step, s0, xs)
    o = jnp.transpose(o, (1, 0, 3, 2, 4)).reshape(bsz, t, A_HEADS, A_HEAD_DIM)
    zg = jax.nn.silu(z.astype(jnp.float32)).reshape(bsz, t, A_HEADS, A_HEAD_DIM)
    o = rms_norm(o, out_norm_w) * zg
    return o.reshape(bsz, t, A_INNER).astype(h.dtype) @ w_out.astype(h.dtype)


def chunk_band_attention(h, w_in, q_norm_w, k_norm_w, rel_bias, w_out):
    bsz, t, _ = h.shape
    nc = t // CHUNK
    pad = LEFT_CHUNKS * CHUNK
    proj = h @ w_in.astype(h.dtype)
    q = rms_norm(proj[..., :B_INNER].reshape(bsz, t, B_HEADS, B_HEAD_DIM), q_norm_w)
    k = rms_norm(proj[..., B_INNER:2 * B_INNER].reshape(bsz, t, B_HEADS, B_HEAD_DIM), k_norm_w)
    v = proj[..., 2 * B_INNER:3 * B_INNER].reshape(bsz, t, B_HEADS, B_HEAD_DIM)
    z = proj[..., 3 * B_INNER:]
    k_pad = jnp.pad(k, ((0, 0), (pad, 0), (0, 0), (0, 0)))
    v_pad = jnp.pad(v, ((0, 0), (pad, 0), (0, 0), (0, 0)))
    q_chunks = jnp.moveaxis(q.reshape(bsz, nc, CHUNK, B_HEADS, B_HEAD_DIM), 1, 0)

    r = jnp.arange(CHUNK)
    m = jnp.arange(BAND)
    rel = (pad + r[:, None]) - m[None, :]
    idx = jnp.clip(rel, -REL_CLIP, REL_CLIP) + REL_CLIP
    bias = rel_bias.astype(jnp.float32)[:, idx]
    scale = B_HEAD_DIM ** -0.5

    def one_chunk(args):
        n, q_n = args
        start = n * CHUNK
        k_n = lax.dynamic_slice_in_dim(k_pad, start, BAND, axis=1)
        v_n = lax.dynamic_slice_in_dim(v_pad, start, BAND, axis=1)
        s = jnp.einsum('bchd,bmhd->bhcm', q_n, k_n).astype(jnp.float32) * scale + bias
        valid = (start - pad + m) >= 0
        s = jnp.where(valid[None, None, None, :], s, -jnp.inf)
        p = jax.nn.softmax(s, axis=-1).astype(v_n.dtype)
        return jnp.einsum('bhcm,bmhd->bchd', p, v_n)

    o = lax.map(one_chunk, (jnp.arange(nc, dtype=jnp.int32), q_chunks))
    o = jnp.moveaxis(o, 0, 1).reshape(bsz, t, B_INNER)
    o = (o.astype(jnp.float32) * jax.nn.silu(z.astype(jnp.float32))).astype(h.dtype)
    return o @ w_out.astype(h.dtype)


def _fwd_setup_inputs(seed: int = 0) -> dict:
    key = jax.random.key(seed)
    ks = jax.random.split(key, 16)
    n_a = (DEPTH + 1) // N_MIXERS
    n_b = DEPTH // N_MIXERS
    f32 = jnp.float32
    x = jax.random.normal(ks[0], (BATCH, SEQ, D_MODEL), f32)
    norm_w = 1.0 + 0.02 * jax.random.normal(ks[1], (DEPTH, D_MODEL), f32)
    a_w_in = jax.random.normal(ks[2], (n_a, D_MODEL, 4 * A_INNER + 2 * A_HEADS), f32) * D_MODEL ** -0.5
    a_conv_w = jax.random.normal(ks[3], (n_a, CONV_K, 3 * A_INNER), f32) * CONV_K ** -0.5
    a_a_log = jnp.log(jax.random.uniform(ks[4], (n_a, A_HEADS), f32, 1.0, 16.0))
    dt = jnp.exp(jax.random.uniform(ks[5], (n_a, A_HEADS), f32, math.log(1e-3), math.log(1e-1)))
    a_dt_bias = dt + jnp.log(-jnp.expm1(-dt))
    a_out_norm_w = 1.0 + 0.02 * jax.random.normal(ks[6], (n_a, A_HEAD_DIM), f32)
    a_w_out = jax.random.normal(ks[7], (n_a, A_INNER, D_MODEL), f32) * A_INNER ** -0.5
    b_w_in = jax.random.normal(ks[8], (n_b, D_MODEL, 4 * B_INNER), f32) * D_MODEL ** -0.5
    b_q_norm_w = 1.0 + 0.02 * jax.random.normal(ks[9], (n_b, B_HEAD_DIM), f32)
    b_k_norm_w = 1.0 + 0.02 * jax.random.normal(ks[10], (n_b, B_HEAD_DIM), f32)
    b_rel_bias = 0.5 * jax.random.normal(ks[11], (n_b, B_HEADS, 2 * REL_CLIP + 1), f32)
    b_w_out = jax.random.normal(ks[12], (n_b, B_INNER, D_MODEL), f32) * B_INNER ** -0.5
    return {"x": x, "norm_w": norm_w, "a_w_in": a_w_in, "a_conv_w": a_conv_w,
            "a_a_log": a_a_log, "a_dt_bias": a_dt_bias, "a_out_norm_w": a_out_norm_w,
            "a_w_out": a_w_out, "b_w_in": b_w_in, "b_q_norm_w": b_q_norm_w,
            "b_k_norm_w": b_k_norm_w, "b_rel_bias": b_rel_bias, "b_w_out": b_w_out}


def _fwd_reference(x, norm_w, a_w_in, a_conv_w, a_a_log, a_dt_bias, a_out_norm_w, a_w_out,
              b_w_in, b_q_norm_w, b_k_norm_w, b_rel_bias, b_w_out):
    h = x
    for i in range(DEPTH):
        j = i // N_MIXERS
        hn = rms_norm(h, norm_w[i])
        if i % N_MIXERS == 0:
            y = gated_deltanet(hn, a_w_in[j], a_conv_w[j], a_a_log[j], a_dt_bias[j],
                               a_out_norm_w[j], a_w_out[j])
        else:
            y = chunk_band_attention(hn, b_w_in[j], b_q_norm_w[j], b_k_norm_w[j],
                                     b_rel_bias[j], b_w_out[j])
        h = h + y
    return h


import jax as _jax
import jax.numpy as _jnp

TWIN_FORMAT = 'train_step'
FWD_PARAMS = ['x', 'norm_w', 'a_w_in', 'a_conv_w', 'a_a_log', 'a_dt_bias', 'a_out_norm_w', 'a_w_out', 'b_w_in', 'b_q_norm_w', 'b_k_norm_w', 'b_rel_bias', 'b_w_out']
TWIN_WEIGHTS = ['norm_w', 'a_w_in', 'a_conv_w', 'a_a_log', 'a_dt_bias', 'a_out_norm_w', 'a_w_out', 'b_w_in', 'b_q_norm_w', 'b_k_norm_w', 'b_rel_bias', 'b_w_out']
TWIN_DIFF_INPUT = 'x'
TWIN_INPUTS = ['x', 'norm_w', 'a_w_in', 'a_conv_w', 'a_a_log', 'a_dt_bias', 'a_out_norm_w', 'a_w_out', 'b_w_in', 'b_q_norm_w', 'b_k_norm_w', 'b_rel_bias', 'b_w_out', 'loss_target', 'm_norm_w', 'm_a_w_in', 'm_a_conv_w', 'm_a_a_log', 'm_a_dt_bias', 'm_a_out_norm_w', 'm_a_w_out', 'm_b_w_in', 'm_b_q_norm_w', 'm_b_k_norm_w', 'm_b_rel_bias', 'm_b_w_out', 'v_norm_w', 'v_a_w_in', 'v_a_conv_w', 'v_a_a_log', 'v_a_dt_bias', 'v_a_out_norm_w', 'v_a_w_out', 'v_b_w_in', 'v_b_q_norm_w', 'v_b_k_norm_w', 'v_b_rel_bias', 'v_b_w_out']
TWIN_OUTPUTS = ['loss', 'grad_x', 'grad_norm_w', 'grad_a_w_in', 'grad_a_conv_w', 'grad_a_a_log', 'grad_a_dt_bias', 'grad_a_out_norm_w', 'grad_a_w_out', 'grad_b_w_in', 'grad_b_q_norm_w', 'grad_b_k_norm_w', 'grad_b_rel_bias', 'grad_b_w_out', 'delta_norm_w', 'delta_a_w_in', 'delta_a_conv_w', 'delta_a_a_log', 'delta_a_dt_bias', 'delta_a_out_norm_w', 'delta_a_w_out', 'delta_b_w_in', 'delta_b_q_norm_w', 'delta_b_k_norm_w', 'delta_b_rel_bias', 'delta_b_w_out', 'new_m_norm_w', 'new_m_a_w_in', 'new_m_a_conv_w', 'new_m_a_a_log', 'new_m_a_dt_bias', 'new_m_a_out_norm_w', 'new_m_a_w_out', 'new_m_b_w_in', 'new_m_b_q_norm_w', 'new_m_b_k_norm_w', 'new_m_b_rel_bias', 'new_m_b_w_out', 'new_v_norm_w', 'new_v_a_w_in', 'new_v_a_conv_w', 'new_v_a_a_log', 'new_v_a_dt_bias', 'new_v_a_out_norm_w', 'new_v_a_w_out', 'new_v_b_w_in', 'new_v_b_q_norm_w', 'new_v_b_k_norm_w', 'new_v_b_rel_bias', 'new_v_b_w_out']
TWIN_LEAF_KINDS = {'loss': 'loss', 'grad_x': 'grad_x', 'grad_norm_w': 'grad_w', 'grad_a_w_in': 'grad_w', 'grad_a_conv_w': 'grad_w', 'grad_a_a_log': 'grad_w', 'grad_a_dt_bias': 'grad_w', 'grad_a_out_norm_w': 'grad_w', 'grad_a_w_out': 'grad_w', 'grad_b_w_in': 'grad_w', 'grad_b_q_norm_w': 'grad_w', 'grad_b_k_norm_w': 'grad_w', 'grad_b_rel_bias': 'grad_w', 'grad_b_w_out': 'grad_w', 'delta_norm_w': 'delta_w', 'delta_a_w_in': 'delta_w', 'delta_a_conv_w': 'delta_w', 'delta_a_a_log': 'delta_w', 'delta_a_dt_bias': 'delta_w', 'delta_a_out_norm_w': 'delta_w', 'delta_a_w_out': 'delta_w', 'delta_b_w_in': 'delta_w', 'delta_b_q_norm_w': 'delta_w', 'delta_b_k_norm_w': 'delta_w', 'delta_b_rel_bias': 'delta_w', 'delta_b_w_out': 'delta_w', 'new_m_norm_w': 'new_m', 'new_m_a_w_in': 'new_m', 'new_m_a_conv_w': 'new_m', 'new_m_a_a_log': 'new_m', 'new_m_a_dt_bias': 'new_m', 'new_m_a_out_norm_w': 'new_m', 'new_m_a_w_out': 'new_m', 'new_m_b_w_in': 'new_m', 'new_m_b_q_norm_w': 'new_m', 'new_m_b_k_norm_w': 'new_m', 'new_m_b_rel_bias': 'new_m', 'new_m_b_w_out': 'new_m', 'new_v_norm_w': 'new_v', 'new_v_a_w_in': 'new_v', 'new_v_a_conv_w': 'new_v', 'new_v_a_a_log': 'new_v', 'new_v_a_dt_bias': 'new_v', 'new_v_a_out_norm_w': 'new_v', 'new_v_a_w_out': 'new_v', 'new_v_b_w_in': 'new_v', 'new_v_b_q_norm_w': 'new_v', 'new_v_b_k_norm_w': 'new_v', 'new_v_b_rel_bias': 'new_v', 'new_v_b_w_out': 'new_v'}


def _forward(args):
    return _fwd_reference(*[args[k] for k in FWD_PARAMS])


def _output_shape():
    def fwd():
        inp = _fwd_setup_inputs(0)
        return _fwd_reference(*[inp[k] for k in FWD_PARAMS])
    out = _jax.eval_shape(fwd)
    return out.shape, out.dtype

N_MICROBATCH = 1
ADAM_LR = 0.001
ADAM_B1 = 0.9
ADAM_B2 = 0.999
ADAM_EPS = 1e-08
ADAM_WD = 0.01
ADAM_STEP = 10
PER_EXAMPLE_BATCH_AXIS = {'x': 0, 'loss_target': 0}
SHARED_INPUTS = []
_WEIGHT_DTYPES = {'norm_w': _jnp.float32, 'a_w_in': _jnp.float32, 'a_conv_w': _jnp.float32, 'a_a_log': _jnp.float32, 'a_dt_bias': _jnp.float32, 'a_out_norm_w': _jnp.float32, 'a_w_out': _jnp.float32, 'b_w_in': _jnp.float32, 'b_q_norm_w': _jnp.float32, 'b_k_norm_w': _jnp.float32, 'b_rel_bias': _jnp.float32, 'b_w_out': _jnp.float32}
MOMENT_SCALE = {'norm_w': 9.559373e+00, 'a_w_in': 2.187801e-01, 'a_conv_w': 3.634645e-01, 'a_a_log': 4.362012e+01, 'a_dt_bias': 4.160705e+01, 'a_out_norm_w': 1.782110e+02, 'a_w_out': 5.700664e-01, 'b_w_in': 4.869747e-02, 'b_q_norm_w': 6.246243e-01, 'b_k_norm_w': 6.257527e-01, 'b_rel_bias': 1.650130e-02, 'b_w_out': 6.198494e-02}


def _to_microbatches(a, axis):
    t = _jnp.moveaxis(a, axis, 0)
    t = t.reshape((N_MICROBATCH, t.shape[0] // N_MICROBATCH) + t.shape[1:])
    return _jnp.moveaxis(t, 1, axis + 1)


def setup_inputs(seed: int = 0) -> dict:
    inp = _fwd_setup_inputs(seed)
    key = _jax.random.fold_in(_jax.random.key(seed), 7919)
    shape, _ = _output_shape()
    out = dict(inp)
    out["loss_target"] = _jax.random.normal(_jax.random.fold_in(key, 0), shape, _jnp.float32)
    for i, name in enumerate(TWIN_WEIGHTS):
        w = inp[name].astype(_jnp.float32)
        if MOMENT_SCALE is None:
            s = _jnp.sqrt(_jnp.mean(_jnp.square(w)) + 1e-30)
        else:
            s = MOMENT_SCALE[name]
        km, kv = _jax.random.split(_jax.random.fold_in(key, i + 1))
        out[name] = w
        out["m_" + name] = s * _jax.random.normal(km, w.shape, _jnp.float32)
        out["v_" + name] = (s * s) * _jax.random.uniform(kv, w.shape, _jnp.float32, 0.5, 1.5)
    if N_MICROBATCH > 1:
        for name, axis in PER_EXAMPLE_BATCH_AXIS.items():
            out[name] = _to_microbatches(out[name], axis)
    return {'x': out['x'], 'norm_w': out['norm_w'], 'a_w_in': out['a_w_in'], 'a_conv_w': out['a_conv_w'], 'a_a_log': out['a_a_log'], 'a_dt_bias': out['a_dt_bias'], 'a_out_norm_w': out['a_out_norm_w'], 'a_w_out': out['a_w_out'], 'b_w_in': out['b_w_in'], 'b_q_norm_w': out['b_q_norm_w'], 'b_k_norm_w': out['b_k_norm_w'], 'b_rel_bias': out['b_rel_bias'], 'b_w_out': out['b_w_out'], 'loss_target': out['loss_target'], 'm_norm_w': out['m_norm_w'], 'm_a_w_in': out['m_a_w_in'], 'm_a_conv_w': out['m_a_conv_w'], 'm_a_a_log': out['m_a_a_log'], 'm_a_dt_bias': out['m_a_dt_bias'], 'm_a_out_norm_w': out['m_a_out_norm_w'], 'm_a_w_out': out['m_a_w_out'], 'm_b_w_in': out['m_b_w_in'], 'm_b_q_norm_w': out['m_b_q_norm_w'], 'm_b_k_norm_w': out['m_b_k_norm_w'], 'm_b_rel_bias': out['m_b_rel_bias'], 'm_b_w_out': out['m_b_w_out'], 'v_norm_w': out['v_norm_w'], 'v_a_w_in': out['v_a_w_in'], 'v_a_conv_w': out['v_a_conv_w'], 'v_a_a_log': out['v_a_a_log'], 'v_a_dt_bias': out['v_a_dt_bias'], 'v_a_out_norm_w': out['v_a_out_norm_w'], 'v_a_w_out': out['v_a_w_out'], 'v_b_w_in': out['v_b_w_in'], 'v_b_q_norm_w': out['v_b_q_norm_w'], 'v_b_k_norm_w': out['v_b_k_norm_w'], 'v_b_rel_bias': out['v_b_rel_bias'], 'v_b_w_out': out['v_b_w_out']}


def _loss(weights, diff, rest, loss_target):
    with _jax.named_scope("forward"):
        args = {**rest, TWIN_DIFF_INPUT: diff, **{k: w.astype(_WEIGHT_DTYPES[k]) for k, w in weights.items()}}
        y = _forward(args)
    with _jax.named_scope("loss_head"):
        err = _jnp.square(y.astype(_jnp.float32) - loss_target)
        return 0.5 * _jnp.sum(_jnp.mean(err, axis=-1)) if err.ndim else 0.5 * err


def _adamw(w, g, m, v):
    m = ADAM_B1 * m + (1.0 - ADAM_B1) * g
    v = ADAM_B2 * v + (1.0 - ADAM_B2) * _jnp.square(g)
    m_hat = m / (1.0 - ADAM_B1 ** ADAM_STEP)
    v_hat = v / (1.0 - ADAM_B2 ** ADAM_STEP)
    delta = -ADAM_LR * (m_hat / (_jnp.sqrt(v_hat) + ADAM_EPS) + ADAM_WD * w)
    return delta, m, v


def reference(x, norm_w, a_w_in, a_conv_w, a_a_log, a_dt_bias, a_out_norm_w, a_w_out, b_w_in, b_q_norm_w, b_k_norm_w, b_rel_bias, b_w_out, loss_target, m_norm_w, m_a_w_in, m_a_conv_w, m_a_a_log, m_a_dt_bias, m_a_out_norm_w, m_a_w_out, m_b_w_in, m_b_q_norm_w, m_b_k_norm_w, m_b_rel_bias, m_b_w_out, v_norm_w, v_a_w_in, v_a_conv_w, v_a_a_log, v_a_dt_bias, v_a_out_norm_w, v_a_w_out, v_b_w_in, v_b_q_norm_w, v_b_k_norm_w, v_b_rel_bias, v_b_w_out):
    given = dict(x=x, norm_w=norm_w, a_w_in=a_w_in, a_conv_w=a_conv_w, a_a_log=a_a_log, a_dt_bias=a_dt_bias, a_out_norm_w=a_out_norm_w, a_w_out=a_w_out, b_w_in=b_w_in, b_q_norm_w=b_q_norm_w, b_k_norm_w=b_k_norm_w, b_rel_bias=b_rel_bias, b_w_out=b_w_out, loss_target=loss_target, m_norm_w=m_norm_w, m_a_w_in=m_a_w_in, m_a_conv_w=m_a_conv_w, m_a_a_log=m_a_a_log, m_a_dt_bias=m_a_dt_bias, m_a_out_norm_w=m_a_out_norm_w, m_a_w_out=m_a_w_out, m_b_w_in=m_b_w_in, m_b_q_norm_w=m_b_q_norm_w, m_b_k_norm_w=m_b_k_norm_w, m_b_rel_bias=m_b_rel_bias, m_b_w_out=m_b_w_out, v_norm_w=v_norm_w, v_a_w_in=v_a_w_in, v_a_conv_w=v_a_conv_w, v_a_a_log=v_a_a_log, v_a_dt_bias=v_a_dt_bias, v_a_out_norm_w=v_a_out_norm_w, v_a_w_out=v_a_w_out, v_b_w_in=v_b_w_in, v_b_q_norm_w=v_b_q_norm_w, v_b_k_norm_w=v_b_k_norm_w, v_b_rel_bias=v_b_rel_bias, v_b_w_out=v_b_w_out)
    weights = {n: given[n] for n in TWIN_WEIGHTS}
    shared = {n: given[n] for n in SHARED_INPUTS}
    per_example = {n: given[n] for n in ['x']}
    grad_fn = _jax.value_and_grad(_loss, argnums=(0, 1))

    def one_microbatch(ex, loss_target):
        ex = dict(ex)
        diff = ex.pop(TWIN_DIFF_INPUT)
        return grad_fn(weights, diff, {**shared, **ex}, loss_target)

    if N_MICROBATCH == 1:
        loss, (grad_w, grad_x) = one_microbatch(per_example, given["loss_target"])
    else:
        def body(carry, xs):
            loss_sum, grad_sum = carry
            l_k, (gw_k, gx_k) = one_microbatch(xs[0], xs[1])
            with _jax.named_scope("update"):
                return (loss_sum + l_k, _jax.tree.map(_jnp.add, grad_sum, gw_k)), gx_k

        init = (_jnp.zeros((), _jnp.float32), _jax.tree.map(_jnp.zeros_like, weights))
        (loss, grad_w), grad_x = _jax.lax.scan(body, init, (per_example, given["loss_target"]))
    with _jax.named_scope("update"):
        delta_w, new_m, new_v = {}, {}, {}
        for n in TWIN_WEIGHTS:
            delta_w[n], new_m[n], new_v[n] = _adamw(weights[n], grad_w[n], given["m_" + n], given["v_" + n])
    return (loss, grad_x, *[grad_w[n] for n in TWIN_WEIGHTS], *[delta_w[n] for n in TWIN_WEIGHTS],
            *[new_m[n] for n in TWIN_WEIGHTS], *[new_v[n] for n in TWIN_WEIGHTS])
```

```python
import functools

import jax
import jax.numpy as jnp
from jax import lax
from jax.experimental import pallas as pl
from jax.experimental.pallas import tpu as pltpu

F32 = jnp.float32
BF16 = jnp.bfloat16

CHUNK = 64
HEAD_DIM = 128
LEFT_CHUNKS = 8
REL_CLIP = 256
CONV_K = 4
EPS = 1e-6
HALO = 8

ADAM_LR = 0.001
ADAM_B1 = 0.9
ADAM_B2 = 0.999
ADAM_EPS = 1e-08
ADAM_WD = 0.01
ADAM_STEP = 10

LANES = 128
N_CHIPS = 4
N_DEV = 8
VMEM_LIMIT_BYTES = 56 * 1024 * 1024
MESH = pl.DeviceIdType.MESH
HIGHEST = lax.Precision.HIGHEST


def _params(*sem):
    return pltpu.CompilerParams(dimension_semantics=sem, vmem_limit_bytes=VMEM_LIMIT_BYTES)


def _dot(a, b, dims=(((1,), (0,)), ((), ())), precision=None):
    return lax.dot_general(a, b, dims, precision=precision, preferred_element_type=F32)


_NT = (((1,), (1,)), ((), ()))
_TN = (((0,), (0,)), ((), ()))


def _bdot(a, b, dims=(((1,), (0,)), ((), ()))):
    return _dot(a.astype(BF16), b.astype(BF16), dims)


def _fdot(a, b, dims=(((1,), (0,)), ((), ()))):
    return _dot(a, b, dims, precision=HIGHEST)


def _silu(x):
    return x * jax.nn.sigmoid(x)


def _matmul(a, b, *, name, trans_b=False, residual=None, out_dtype=F32, tm=1024, tn=1024, tk=1024):
    m, k = a.shape
    n = b.shape[0] if trans_b else b.shape[1]
    tm, tn, tk = min(tm, m), min(tn, n), min(tk, k)
    assert m % tm == 0 and n % tn == 0 and k % tk == 0, (a.shape, b.shape, tm, tn, tk)
    nk = k // tk
    dims = _NT if trans_b else (((1,), (0,)), ((), ()))

    def body(*refs):
        if residual is None:
            a_ref, b_ref, o_ref, acc_ref = refs
            r_ref = None
        else:
            a_ref, b_ref, r_ref, o_ref, acc_ref = refs
        kk = pl.program_id(2)

        @pl.when(kk == 0)
        def _():
            acc_ref[...] = jnp.zeros_like(acc_ref)

        acc_ref[...] += _dot(a_ref[...], b_ref[...], dims)

        @pl.when(kk == nk - 1)
        def _():
            r = acc_ref[...]
            if r_ref is not None:
                r = r + r_ref[...]
            o_ref[...] = r.astype(o_ref.dtype)

    in_specs = [
        pl.BlockSpec((tm, tk), lambda i, j, kk: (i, kk)),
        pl.BlockSpec((tn, tk), lambda i, j, kk: (j, kk)) if trans_b else pl.BlockSpec((tk, tn), lambda i, j, kk: (kk, j)),
    ]
    args = [a, b]
    if residual is not None:
        in_specs.append(pl.BlockSpec((tm, tn), lambda i, j, kk: (i, j)))
        args.append(residual)
    return pl.pallas_call(
        body,
        name=name,
        grid=(m // tm, n // tn, nk),
        in_specs=in_specs,
        out_specs=pl.BlockSpec((tm, tn), lambda i, j, kk: (i, j)),
        out_shape=jax.ShapeDtypeStruct((m, n), out_dtype),
        scratch_shapes=[pltpu.VMEM((tm, tn), F32)],
        compiler_params=_params("parallel", "parallel", "arbitrary"),
    )(*args)


def _rms(x, w):
    return x * lax.rsqrt(jnp.mean(x * x, axis=-1, keepdims=True) + EPS) * w


def _rmsnorm_fwd(x, w_row, *, name, tr=512):
    t, d = x.shape
    tr = min(tr, t)

    def body(x_ref, w_ref, o_ref):
        o_ref[...] = _rms(x_ref[...], w_ref[...]).astype(BF16)

    return pl.pallas_call(
        body,
        name=name,
        grid=(t // tr,),
        in_specs=[pl.BlockSpec((tr, d), lambda i: (i, 0)), pl.BlockSpec((1, d), lambda i: (0, 0))],
        out_specs=pl.BlockSpec((tr, d), lambda i: (i, 0)),
        out_shape=jax.ShapeDtypeStruct((t, d), BF16),
        compiler_params=_params("parallel"),
    )(x, w_row)


def _rmsnorm_bwd(x, w_row, dy, dres, *, name, tr=256):
    t, d = x.shape
    tr = min(tr, t)

    def body(x_ref, w_ref, dy_ref, dres_ref, dx_ref, dxb_ref, dw_ref):
        @pl.when(pl.program_id(0) == 0)
        def _():
            dw_ref[...] = jnp.zeros_like(dw_ref)

        _, vjp = jax.vjp(_rms, x_ref[...], w_ref[...])
        dx, dw = vjp(dy_ref[...])
        dx = dx + dres_ref[...]
        dx_ref[...] = dx
        dxb_ref[...] = dx.astype(BF16)
        dw_ref[...] += dw

    row = pl.BlockSpec((tr, d), lambda i: (i, 0))
    vec = pl.BlockSpec((1, d), lambda i: (0, 0))
    return pl.pallas_call(
        body,
        name=name,
        grid=(t // tr,),
        in_specs=[row, vec, row, row],
        out_specs=[row, row, vec],
        out_shape=[jax.ShapeDtypeStruct((t, d), F32), jax.ShapeDtypeStruct((t, d), BF16), jax.ShapeDtypeStruct((1, d), F32)],
        compiler_params=_params("arbitrary"),
    )(x, w_row, dy, dres)


def _loss_head(h, target, *, name, tr=512):
    t, d = h.shape
    tr = min(tr, t)

    def body(h_ref, t_ref, dh_ref, dhb_ref, part_ref):
        @pl.when(pl.program_id(0) == 0)
        def _():
            part_ref[...] = jnp.zeros_like(part_ref)

        err = h_ref[...] - t_ref[...]
        dh = err * (1.0 / d)
        dh_ref[...] = dh
        dhb_ref[...] = dh.astype(BF16)
        part_ref[...] += jnp.sum(err * err, axis=0, keepdims=True)

    row = pl.BlockSpec((tr, d), lambda i: (i, 0))
    vec = pl.BlockSpec((1, d), lambda i: (0, 0))
    dh, dhb, part = pl.pallas_call(
        body,
        name=name,
        grid=(t // tr,),
        in_specs=[row, row],
        out_specs=[row, row, vec],
        out_shape=[jax.ShapeDtypeStruct((t, d), F32), jax.ShapeDtypeStruct((t, d), BF16), jax.ShapeDtypeStruct((1, d), F32)],
        compiler_params=_params("arbitrary"),
    )(h, target)
    return 0.5 / d * jnp.sum(part), dh, dhb


def _conv_silu(xe, w):
    first = HALO - (CONV_K - 1)
    c = w[0:1, :] * xe[first:first + CHUNK, :]
    for j in range(1, CONV_K):
        c = c + w[j:j + 1, :] * xe[first + j:first + j + CHUNK, :]
    return _silu(c)


def _gdn_chunk(qx, kx, vx, z, a, b, state, wq, wk, wv, alog, dtb, onw):
    qt, kt, v = _conv_silu(qx, wq), _conv_silu(kx, wk), _conv_silu(vx, wv)
    q = qt * lax.rsqrt(jnp.sum(qt * qt, axis=-1, keepdims=True) + EPS) * (HEAD_DIM ** -0.5)
    k = kt * lax.rsqrt(jnp.sum(kt * kt, axis=-1, keepdims=True) + EPS)
    beta = jax.nn.sigmoid(b)
    sp = a + dtb
    g = -jnp.exp(alog) * (jnp.maximum(sp, 0.0) + jnp.log(1.0 + jnp.exp(-jnp.abs(sp))))

    row = lax.broadcasted_iota(jnp.int32, (CHUNK, CHUNK), 0)
    col = lax.broadcasted_iota(jnp.int32, (CHUNK, CHUNK), 1)
    tri_incl = row >= col
    tri_strict = row > col
    ones_incl = tri_incl.astype(F32)
    gc = _fdot(ones_incl, g * jnp.ones((1, HEAD_DIM), F32))
    gc_row = _fdot(g * jnp.ones((1, CHUNK), F32), (row <= col).astype(F32), _TN)
    decay = jnp.exp(jnp.where(tri_incl, gc[:, :CHUNK] - gc_row, -1e30))

    kb = k * beta
    vb = v * beta
    neg_l = jnp.where(tri_strict, -(_bdot(kb, k, _NT) * decay), 0.0)
    inv = (row == col).astype(F32) + neg_l
    power = neg_l
    for _ in range(5):
        power = _fdot(power, power)
        inv = inv + _fdot(inv, power)
    e = jnp.exp(gc)
    u = _fdot(inv, vb)
    w = _fdot(inv, kb * e)
    qk = jnp.where(tri_incl, _bdot(q, k, _NT) * decay, 0.0)
    g_last = gc[CHUNK - 1:CHUNK, :]
    k_dec = k * jnp.exp(g_last - gc)
    state_b = state.astype(BF16)
    v_new = u - _bdot(w, state_b)
    o = _bdot(q * e, state_b) + _bdot(qk, v_new)
    new_state = state * jnp.exp(g_last) + _bdot(k_dec, v_new, _TN)
    og = _rms(o, onw) * _silu(z)
    return og, new_state


def _head_lane(h, offset=0):
    return lax.broadcasted_iota(jnp.int32, (1, LANES), 1) == h + offset


def _pick(mask, x):
    return jnp.sum(jnp.where(mask, x, 0.0), axis=1, keepdims=True)


def _gdn_specs(heads, tb, rev, nb):
    blk = (lambda i: nb - 1 - i) if rev else (lambda i: i)
    hb = tb // HALO

    def col(group):
        return pl.BlockSpec((tb, HEAD_DIM), lambda i, h: (blk(i), group * heads + h))

    def halo(group):
        return pl.BlockSpec((HALO, HEAD_DIM), lambda i, h: (jnp.maximum(blk(i) * hb - 1, 0), group * heads + h))

    def convw(group):
        return pl.BlockSpec((CONV_K, HEAD_DIM), lambda i, h: (0, group * heads + h))

    vec = pl.BlockSpec((1, LANES), lambda i, h: (0, 0))
    ab = pl.BlockSpec((tb, LANES), lambda i, h: (blk(i), 0))
    states = pl.BlockSpec((1, tb // CHUNK, HEAD_DIM, HEAD_DIM), lambda i, h: (h, blk(i), 0, 0))
    return blk, col, halo, convw, vec, ab, states


def _gdn_fwd(proj, ab, conv_w, alog_row, dtb_row, onw_row, *, heads, name, tb=512):
    t = proj.shape[0]
    tb = min(tb, t)
    nb, cpb = t // tb, tb // CHUNK
    _, col, halo, convw, vec, abspec, states = _gdn_specs(heads, tb, False, nb)

    def body(q_ref, k_ref, v_ref, qh_ref, kh_ref, vh_ref, z_ref, ab_ref, wq_ref, wk_ref, wv_ref, alog_ref, dtb_ref, onw_ref,
             og_ref, st_ref, state_scr, x_scr):
        i, h = pl.program_id(0), pl.program_id(1)
        for n, (ref, href) in enumerate(((q_ref, qh_ref), (k_ref, kh_ref), (v_ref, vh_ref))):
            x_scr[n, 0:HALO, :] = jnp.where(i > 0, href[...], 0.0)
            x_scr[n, HALO:HALO + tb, :] = ref[...]

        @pl.when(i == 0)
        def _():
            state_scr[h] = jnp.zeros((HEAD_DIM, HEAD_DIM), F32)

        sel_a, sel_b = _head_lane(h), _head_lane(h, heads)
        alog, dtb = _pick(sel_a, alog_ref[...]), _pick(sel_a, dtb_ref[...])
        wq, wk, wv, onw = wq_ref[...], wk_ref[...], wv_ref[...], onw_ref[...]

        def chunk(c, state):
            r0 = pl.multiple_of(c * CHUNK, CHUNK)
            abc = ab_ref[pl.ds(r0, CHUNK), :]
            st_ref[0, c] = state
            og, new_state = _gdn_chunk(
                x_scr[0, pl.ds(r0, HALO + CHUNK), :], x_scr[1, pl.ds(r0, HALO + CHUNK), :], x_scr[2, pl.ds(r0, HALO + CHUNK), :],
                z_ref[pl.ds(r0, CHUNK), :], _pick(sel_a, abc), _pick(sel_b, abc), state, wq, wk, wv, alog, dtb, onw)
            og_ref[pl.ds(r0, CHUNK), :] = og.astype(BF16)
            return new_state

        state_scr[h] = lax.fori_loop(0, cpb, chunk, state_scr[h])

    return pl.pallas_call(
        body,
        name=name,
        grid=(nb, heads),
        in_specs=[col(0), col(1), col(2), halo(0), halo(1), halo(2), col(3), abspec, convw(0), convw(1), convw(2), vec, vec, vec],
        out_specs=[pl.BlockSpec((tb, HEAD_DIM), lambda i, h: (i, h)), states],
        out_shape=[jax.ShapeDtypeStruct((t, heads * HEAD_DIM), BF16),
                   jax.ShapeDtypeStruct((heads, t // CHUNK, HEAD_DIM, HEAD_DIM), F32)],
        scratch_shapes=[pltpu.VMEM((heads, HEAD_DIM, HEAD_DIM), F32), pltpu.VMEM((3, HALO + tb, HEAD_DIM), F32)],
        compiler_params=_params("arbitrary", "arbitrary"),
    )(proj, proj, proj, proj, proj, proj, proj, ab, conv_w, conv_w, conv_w, alog_row, dtb_row, onw_row)


def _gdn_bwd(proj, ab, conv_w, alog_row, dtb_row, onw_row, states, dog, *, heads, name, tb=512):
    t = proj.shape[0]
    tb = min(tb, t)
    nb, cpb = t // tb, tb // CHUNK
    _, col, halo, convw, vec, abspec, states_spec = _gdn_specs(heads, tb, True, nb)
    n_conv = conv_w.shape[1]

    def body(q_ref, k_ref, v_ref, qh_ref, kh_ref, vh_ref, z_ref, ab_ref, wq_ref, wk_ref, wv_ref, alog_ref, dtb_ref, onw_ref,
             st_ref, dog_ref, dq_ref, dk_ref, dv_ref, dz_ref, dab_ref, dconv_ref, dalog_ref, ddtb_ref, donw_ref,
             dstate_scr, x_scr, dx_scr, carry_scr):
        i, h = pl.program_id(0), pl.program_id(1)
        first_block = i == nb - 1
        for n, (ref, href) in enumerate(((q_ref, qh_ref), (k_ref, kh_ref), (v_ref, vh_ref))):
            x_scr[n, 0:HALO, :] = jnp.where(first_block, 0.0, href[...])
            x_scr[n, HALO:HALO + tb, :] = ref[...]
        dx_scr[...] = jnp.zeros_like(dx_scr)

        @pl.when(jnp.logical_and(i == 0, h == 0))
        def _():
            dconv_ref[...] = jnp.zeros_like(dconv_ref)
            dalog_ref[...] = jnp.zeros_like(dalog_ref)
            ddtb_ref[...] = jnp.zeros_like(ddtb_ref)
            donw_ref[...] = jnp.zeros_like(donw_ref)

        @pl.when(h == 0)
        def _():
            dab_ref[...] = jnp.zeros_like(dab_ref)

        @pl.when(i == 0)
        def _():
            dstate_scr[h] = jnp.zeros((HEAD_DIM, HEAD_DIM), F32)
            carry_scr[h] = jnp.zeros((3, HALO, HEAD_DIM), F32)

        sel_a, sel_b = _head_lane(h), _head_lane(h, heads)
        alog, dtb = _pick(sel_a, alog_ref[...]), _pick(sel_a, dtb_ref[...])
        wq, wk, wv, onw = wq_ref[...], wk_ref[...], wv_ref[...], onw_ref[...]

        def chunk(step, carry):
            dstate, dwq, dwk, dwv, dalog, ddtb, donw = carry
            c = cpb - 1 - step
            r0 = pl.multiple_of(c * CHUNK, CHUNK)
            rows = pl.ds(r0, CHUNK)
            ext = pl.ds(r0, HALO + CHUNK)
            abc = ab_ref[rows, :]
            _, vjp = jax.vjp(
                _gdn_chunk, x_scr[0, ext, :], x_scr[1, ext, :], x_scr[2, ext, :], z_ref[rows, :], _pick(sel_a, abc),
                _pick(sel_b, abc), st_ref[0, c], wq, wk, wv, alog, dtb, onw)
            dqx, dkx, dvx, dz, da, db, dstate, gwq, gwk, gwv, galog, gdtb, gonw = vjp((dog_ref[rows, :], dstate))
            dx_scr[0, ext, :] += dqx
            dx_scr[1, ext, :] += dkx
            dx_scr[2, ext, :] += dvx
            dz_ref[rows, :] = dz.astype(BF16)
            dab_ref[rows, :] += jnp.where(sel_a, da, 0.0) + jnp.where(sel_b, db, 0.0)
            return dstate, dwq + gwq, dwk + gwk, dwv + gwv, dalog + galog, ddtb + gdtb, donw + gonw

        zw = jnp.zeros((CONV_K, HEAD_DIM), F32)
        z1 = jnp.zeros((1, 1), F32)
        dstate, dwq, dwk, dwv, dalog, ddtb, donw = lax.fori_loop(
            0, cpb, chunk, (dstate_scr[h], zw, zw, zw, z1, z1, jnp.zeros((1, HEAD_DIM), F32)))
        dstate_scr[h] = dstate
        for n, (dref, dw) in enumerate(((dq_ref, dwq), (dk_ref, dwk), (dv_ref, dwv))):
            dx_scr[n, tb:tb + HALO, :] += carry_scr[h, n]
            carry_scr[h, n] = dx_scr[n, 0:HALO, :]
            dref[...] = dx_scr[n, HALO:HALO + tb, :].astype(BF16)
            lanes = pl.ds(pl.multiple_of((n * heads + h) * HEAD_DIM, HEAD_DIM), HEAD_DIM)
            dconv_ref[:, lanes] += dw
        dalog_ref[...] += jnp.where(sel_a, dalog, 0.0)
        ddtb_ref[...] += jnp.where(sel_a, ddtb, 0.0)
        donw_ref[...] += donw

    out_col = pl.BlockSpec((tb, HEAD_DIM), lambda i, h: (nb - 1 - i, h))
    dog_spec = pl.BlockSpec((tb, HEAD_DIM), lambda i, h: (nb - 1 - i, h))
    col_shape = jax.ShapeDtypeStruct((t, heads * HEAD_DIM), BF16)
    row_shape = jax.ShapeDtypeStruct((1, LANES), F32)
    return pl.pallas_call(
        body,
        name=name,
        grid=(nb, heads),
        in_specs=[col(0), col(1), col(2), halo(0), halo(1), halo(2), col(3), abspec, convw(0), convw(1), convw(2), vec, vec, vec,
                  states_spec, dog_spec],
        out_specs=[out_col, out_col, out_col, out_col, abspec,
                   pl.BlockSpec((CONV_K, n_conv), lambda i, h: (0, 0)), vec, vec, vec],
        out_shape=[col_shape, col_shape, col_shape, col_shape, jax.ShapeDtypeStruct((t, LANES), F32),
                   jax.ShapeDtypeStruct((CONV_K, n_conv), F32), row_shape, row_shape, row_shape],
        scratch_shapes=[pltpu.VMEM((heads, HEAD_DIM, HEAD_DIM), F32), pltpu.VMEM((3, HALO + tb, HEAD_DIM), F32),
                        pltpu.VMEM((3, HALO + tb, HEAD_DIM), F32), pltpu.VMEM((heads, 3, HALO, HEAD_DIM), F32)],
        compiler_params=_params("arbitrary", "arbitrary"),
    )(proj, proj, proj, proj, proj, proj, proj, ab, conv_w, conv_w, conv_w, alog_row, dtb_row, onw_row, states, dog)


BAND = (LEFT_CHUNKS + 1) * CHUNK
PAD = LEFT_CHUNKS * CHUNK


DIAGS = BAND + CHUNK - 1
NEAR = PAD + CHUNK - 1 - REL_CLIP
assert 0 < NEAR < DIAGS


def _band_bias(rel_bias):
    heads = rel_bias.shape[0]
    far = jnp.broadcast_to(rel_bias[:, 2 * REL_CLIP:], (heads, NEAR + 1))
    near = rel_bias[:, 2 * REL_CLIP + NEAR + 1 - DIAGS:2 * REL_CLIP][:, ::-1]
    diag = jnp.concatenate([far, near], axis=1)
    return jnp.stack([diag[:, CHUNK - 1 - r:CHUNK - 1 - r + BAND] for r in range(CHUNK)], axis=1)


def _band_bias_grad(dbias):
    heads = dbias.shape[0]
    diag = sum(jnp.pad(dbias[:, r, :], ((0, 0), (CHUNK - 1 - r, r))) for r in range(CHUNK))
    far = jnp.sum(diag[:, :NEAR + 1], axis=1, keepdims=True)
    near = diag[:, NEAR + 1:][:, ::-1]
    unused = jnp.zeros((heads, 2 * REL_CLIP - near.shape[1]), F32)
    return jnp.concatenate([unused, near, far], axis=1)


def _attn_chunk(q_pre, z, kn, v, bias, qnw, first_valid):
    q = _rms(q_pre, qnw)
    s = _bdot(q, kn, _NT) * (HEAD_DIM ** -0.5) + bias
    valid = lax.broadcasted_iota(jnp.int32, (CHUNK, BAND), 1) >= first_valid
    s = jnp.where(valid, s, -1e30)
    p = jnp.exp(s - jnp.max(s, axis=-1, keepdims=True))
    p = p / jnp.sum(p, axis=-1, keepdims=True)
    return _bdot(p, v) * _silu(z)


def _attn_specs(heads, tb, t):
    def col(group):
        return pl.BlockSpec((tb, HEAD_DIM), lambda h, i: (i, group * heads + h))

    def full(group):
        return pl.BlockSpec((t, HEAD_DIM), lambda h, i: (0, group * heads + h))

    bias = pl.BlockSpec((1, CHUNK, BAND), lambda h, i: (h, 0, 0))
    vec = pl.BlockSpec((1, HEAD_DIM), lambda h, i: (0, 0))
    return col, full, bias, vec


def _attn_fill(k_ref, v_ref, knw_ref, kn_scr, v_scr, t):
    kn_scr[0:PAD, :] = jnp.zeros((PAD, HEAD_DIM), BF16)
    v_scr[0:PAD, :] = jnp.zeros((PAD, HEAD_DIM), BF16)
    step = min(512, t)

    def fill(j, _):
        rows = pl.ds(pl.multiple_of(j * step, step), step)
        prows = pl.ds(pl.multiple_of(PAD + j * step, CHUNK), step)
        kn_scr[prows, :] = _rms(k_ref[rows, :], knw_ref[...]).astype(BF16)
        v_scr[prows, :] = v_ref[rows, :].astype(BF16)
        return 0

    lax.fori_loop(0, t // step, fill, 0)


def _attn_fwd(proj, bias, qnw_row, knw_row, *, heads, name, tb=512):
    t = proj.shape[0]
    tb = min(tb, t)
    nb, cpb = t // tb, tb // CHUNK
    col, full, bias_spec, vec = _attn_specs(heads, tb, t)

    def body(q_ref, k_ref, v_ref, z_ref, bias_ref, qnw_ref, knw_ref, og_ref, kn_scr, v_scr):
        i = pl.program_id(1)

        @pl.when(i == 0)
        def _():
            _attn_fill(k_ref, v_ref, knw_ref, kn_scr, v_scr, t)

        b = bias_ref[0]
        qnw = qnw_ref[...]

        def chunk(c, _):
            r0 = pl.multiple_of(c * CHUNK, CHUNK)
            rows = pl.ds(r0, CHUNK)
            start = pl.multiple_of(i * tb + r0, CHUNK)
            band = pl.ds(start, BAND)
            og = _attn_chunk(q_ref[rows, :], z_ref[rows, :], kn_scr[band, :], v_scr[band, :], b, qnw, PAD - start)
            og_ref[rows, :] = og.astype(BF16)
            return 0

        lax.fori_loop(0, cpb, chunk, 0)

    return pl.pallas_call(
        body,
        name=name,
        grid=(heads, nb),
        in_specs=[col(0), full(1), full(2), col(3), bias_spec, vec, vec],
        out_specs=pl.BlockSpec((tb, HEAD_DIM), lambda h, i: (i, h)),
        out_shape=jax.ShapeDtypeStruct((t, heads * HEAD_DIM), BF16),
        scratch_shapes=[pltpu.VMEM((PAD + t, HEAD_DIM), BF16), pltpu.VMEM((PAD + t, HEAD_DIM), BF16)],
        compiler_params=_params("arbitrary", "arbitrary"),
    )(proj, proj, proj, proj, bias, qnw_row, knw_row)


def _attn_bwd(proj, bias, qnw_row, knw_row, dog, *, heads, name, tb=512):
    t = proj.shape[0]
    tb = min(tb, t)
    nb, cpb = t // tb, tb // CHUNK
    col, full, bias_spec, vec = _attn_specs(heads, tb, t)

    def body(q_ref, k_ref, v_ref, z_ref, bias_ref, qnw_ref, knw_ref, dog_ref,
             dq_ref, dk_ref, dv_ref, dz_ref, dbias_ref, dqnw_ref, dknw_ref, kn_scr, v_scr, dkn_scr, dv_scr):
        i = pl.program_id(1)

        @pl.when(i == 0)
        def _():
            _attn_fill(k_ref, v_ref, knw_ref, kn_scr, v_scr, t)
            dkn_scr[...] = jnp.zeros_like(dkn_scr)
            dv_scr[...] = jnp.zeros_like(dv_scr)
            dbias_ref[...] = jnp.zeros_like(dbias_ref)
            dqnw_ref[...] = jnp.zeros_like(dqnw_ref)

        b = bias_ref[0]
        qnw = qnw_ref[...]

        def chunk(c, carry):
            dbias, dqnw = carry
            r0 = pl.multiple_of(c * CHUNK, CHUNK)
            rows = pl.ds(r0, CHUNK)
            start = pl.multiple_of(i * tb + r0, CHUNK)
            band = pl.ds(start, BAND)
            first_valid = PAD - start
            _, vjp = jax.vjp(
                lambda q_pre, z, kn, v, bb, w: _attn_chunk(q_pre, z, kn, v, bb, w, first_valid),
                q_ref[rows, :], z_ref[rows, :], kn_scr[band, :].astype(F32), v_scr[band, :].astype(F32), b, qnw)
            dq, dz, dkn, dv, db, dw = vjp(dog_ref[rows, :])
            dq_ref[rows, :] = dq.astype(BF16)
            dz_ref[rows, :] = dz.astype(BF16)
            dkn_scr[band, :] += dkn
            dv_scr[band, :] += dv
            return dbias + db, dqnw + dw

        dbias, dqnw = lax.fori_loop(0, cpb, chunk, (jnp.zeros((CHUNK, BAND), F32), jnp.zeros((1, HEAD_DIM), F32)))
        dbias_ref[0] += dbias
        dqnw_ref[0] += dqnw

        @pl.when(i == nb - 1)
        def _():
            step = min(512, t)

            def finish(j, dknw):
                rows = pl.ds(pl.multiple_of(j * step, step), step)
                prows = pl.ds(pl.multiple_of(PAD + j * step, CHUNK), step)
                _, vjp = jax.vjp(_rms, k_ref[rows, :], knw_ref[...])
                dk, dw = vjp(dkn_scr[prows, :])
                dk_ref[rows, :] = dk.astype(BF16)
                dv_ref[rows, :] = dv_scr[prows, :].astype(BF16)
                return dknw + dw

            dknw_ref[0] = lax.fori_loop(0, t // step, finish, jnp.zeros((1, HEAD_DIM), F32))

    out_col = pl.BlockSpec((tb, HEAD_DIM), lambda h, i: (i, h))
    out_full = pl.BlockSpec((t, HEAD_DIM), lambda h, i: (0, h))
    head_vec = pl.BlockSpec((1, 1, HEAD_DIM), lambda h, i: (h, 0, 0))
    col_shape = jax.ShapeDtypeStruct((t, heads * HEAD_DIM), BF16)
    vec_shape = jax.ShapeDtypeStruct((heads, 1, HEAD_DIM), F32)
    return pl.pallas_call(
        body,
        name=name,
        grid=(heads, nb),
        in_specs=[col(0), full(1), full(2), col(3), bias_spec, vec, vec, pl.BlockSpec((tb, HEAD_DIM), lambda h, i: (i, h))],
        out_specs=[out_col, out_full, out_full, out_col, bias_spec, head_vec, head_vec],
        out_shape=[col_shape, col_shape, col_shape, col_shape, jax.ShapeDtypeStruct((heads, CHUNK, BAND), F32),
                   vec_shape, vec_shape],
        scratch_shapes=[pltpu.VMEM((PAD + t, HEAD_DIM), BF16), pltpu.VMEM((PAD + t, HEAD_DIM), BF16),
                        pltpu.VMEM((PAD + t, HEAD_DIM), F32), pltpu.VMEM((PAD + t, HEAD_DIM), F32)],
        compiler_params=_params("arbitrary", "arbitrary"),
    )(proj, proj, proj, proj, bias, qnw_row, knw_row, dog)


def _lane_row(v):
    v = v.reshape(1, -1)
    return jnp.pad(v, ((0, 0), (0, LANES - v.shape[1])))


def _local_step(x, target, norm_w, wa_in, conv_w, a_log, dt_bias, onw, wa_out, wb_in, qnw, knw, rel_bias, wb_out):
    ha, hb = a_log.shape[-1], rel_bias.shape[-2]
    na = 4 * ha * HEAD_DIM
    wa_main = wa_in[:, :na]
    wa_ab = jnp.pad(wa_in[:, na:], ((0, 0), (0, LANES - 2 * ha)))
    alog_row, dtb_row, onw_row = _lane_row(a_log), _lane_row(dt_bias), _lane_row(onw)
    qnw_row, knw_row = _lane_row(qnw), _lane_row(knw)
    bias = _band_bias(rel_bias.reshape(hb, -1))

    hn0 = _rmsnorm_fwd(x, norm_w[0:1], name="norm0")
    proj_a = _matmul(hn0, wa_main, name="a_in")
    ab_a = _matmul(hn0, wa_ab, name="a_in_ab")
    og_a, states = _gdn_fwd(proj_a, ab_a, conv_w, alog_row, dtb_row, onw_row, heads=ha, name="gdn_fwd")
    h1 = _matmul(og_a, wa_out, residual=x, name="a_out")
    hn1 = _rmsnorm_fwd(h1, norm_w[1:2], name="norm1")
    proj_b = _matmul(hn1, wb_in, name="b_in")
    og_b = _attn_fwd(proj_b, bias, qnw_row, knw_row, heads=hb, name="attn_fwd")
    h2 = _matmul(og_b, wb_out, residual=h1, name="b_out")
    loss, dh2, dh2_b = _loss_head(h2, target, name="loss_head")

    dog_b = _matmul(dh2_b, wb_out, trans_b=True, name="d_b_out_x")
    dwb_out = _matmul(og_b.T, dh2_b, name="d_b_out_w")
    dq, dk, dv, dz, dbias, dqnw, dknw = _attn_bwd(proj_b, bias, qnw_row, knw_row, dog_b, heads=hb, name="attn_bwd")
    dproj_b = jnp.concatenate([dq, dk, dv, dz], axis=1)
    dhn1 = _matmul(dproj_b, wb_in, trans_b=True, name="d_b_in_x")
    dwb_in = _matmul(hn1.T, dproj_b, name="d_b_in_w")
    dh1, dh1_b, dnw1 = _rmsnorm_bwd(h1, norm_w[1:2], dhn1, dh2, name="d_norm1")

    dog_a = _matmul(dh1_b, wa_out, trans_b=True, name="d_a_out_x")
    dwa_out = _matmul(og_a.T, dh1_b, name="d_a_out_w")
    dq, dk, dv, dz, dab, dconv, dalog, ddtb, donw = _gdn_bwd(
        proj_a, ab_a, conv_w, alog_row, dtb_row, onw_row, states, dog_a, heads=ha, name="gdn_bwd")
    dproj_a = jnp.concatenate([dq, dk, dv, dz], axis=1)
    dab_b = dab.astype(BF16)
    dhn0 = _matmul(dproj_a, wa_main, trans_b=True, name="d_a_in_x")
    dhn0 = _matmul(dab_b, wa_ab, trans_b=True, residual=dhn0, name="d_a_in_ab_x")
    hn0_t = hn0.T
    dwa_in = jnp.concatenate(
        [_matmul(hn0_t, dproj_a, name="d_a_in_w"), _matmul(hn0_t, dab_b, name="d_a_in_ab_w")[:, :2 * ha]], axis=1)
    dx, _, dnw0 = _rmsnorm_bwd(x, norm_w[0:1], dhn0, dh1, name="d_norm0")

    drel = _band_bias_grad(dbias)
    grads = dict(
        norm_w=jnp.concatenate([dnw0, dnw1], axis=0), a_w_in=dwa_in, a_conv_w=dconv, a_a_log=dalog[:, :ha],
        a_dt_bias=ddtb[:, :ha], a_out_norm_w=donw, a_w_out=dwa_out, b_w_in=dwb_in, b_q_norm_w=jnp.sum(dqnw, axis=0),
        b_k_norm_w=jnp.sum(dknw, axis=0), b_rel_bias=drel[None], b_w_out=dwb_out)
    return loss, dx, grads


_ANY = pl.BlockSpec(memory_space=pl.ANY)
_CHIP_FLIPS = ((1, 0), (0, 1), (1, 1))


def _place():
    x, y, c = lax.axis_index("x"), lax.axis_index("y"), lax.axis_index("c")
    return x, y, c


def _flip(v, bit):
    return 1 - v if bit else v


def _remote(src, dst, send_sem, recv_sem, peer):
    return pltpu.make_async_remote_copy(src_ref=src, dst_ref=dst, send_sem=send_sem, recv_sem=recv_sem, device_id=peer,
                                        device_id_type=MESH)


def _comm_call(body, arrays, out_shapes, n_remote, n_local, name):
    scratch = [pltpu.SemaphoreType.DMA((n_remote,)), pltpu.SemaphoreType.DMA((n_remote,))]
    if n_local:
        scratch.append(pltpu.SemaphoreType.DMA((n_local,)))
    return pl.pallas_call(
        body, name=name, in_specs=[_ANY] * len(arrays), out_specs=[_ANY] * len(out_shapes), out_shape=out_shapes,
        scratch_shapes=scratch)(*arrays)


def _gather_chips(shards, *, name):
    n = len(shards)

    def body(*refs):
        ins, outs, (send_sems, recv_sems, local_sems) = refs[:n], refs[n:2 * n], refs[2 * n:]
        x, y, c = _place()
        mine = 2 * x + y
        local, remote, landing = [], [], []
        for a in range(n):
            local.append(pltpu.make_async_copy(ins[a], outs[a].at[mine], local_sems.at[a]))
            for k, (fx, fy) in enumerate(_CHIP_FLIPS):
                peer = (_flip(x, fx), _flip(y, fy), c)
                sems = send_sems.at[3 * a + k], recv_sems.at[3 * a + k]
                remote.append(_remote(ins[a], outs[a].at[mine], *sems, peer))
                landing.append(_remote(ins[a], outs[a].at[2 * peer[0] + peer[1]], *sems, peer))
        for cp in local + remote:
            cp.start()
        for cp in local:
            cp.wait()
        for cp in landing:
            cp.wait_recv()
        for cp in remote:
            cp.wait_send()

    shapes = [jax.ShapeDtypeStruct((N_CHIPS,) + s.shape, s.dtype) for s in shards]
    return _comm_call(body, shards, shapes, 3 * n, n, name)


def _exchange_chips(slabs, *, name):
    n = len(slabs)

    def body(*refs):
        ins, outs, (send_sems, recv_sems, local_sems) = refs[:n], refs[n:2 * n], refs[2 * n:]
        x, y, c = _place()
        mine = 2 * x + y
        local, remote, landing = [], [], []
        for a in range(n):
            local.append(pltpu.make_async_copy(ins[a].at[mine], outs[a].at[mine], local_sems.at[a]))
            for k, (fx, fy) in enumerate(_CHIP_FLIPS):
                peer = (_flip(x, fx), _flip(y, fy), c)
                theirs = 2 * peer[0] + peer[1]
                sems = send_sems.at[3 * a + k], recv_sems.at[3 * a + k]
                remote.append(_remote(ins[a].at[theirs], outs[a].at[mine], *sems, peer))
                landing.append(_remote(ins[a].at[theirs], outs[a].at[theirs], *sems, peer))
        for cp in local + remote:
            cp.start()
        for cp in local:
            cp.wait()
        for cp in landing:
            cp.wait_recv()
        for cp in remote:
            cp.wait_send()

    shapes = [jax.ShapeDtypeStruct(s.shape, s.dtype) for s in slabs]
    return _comm_call(body, slabs, shapes, 3 * n, n, name)


def _swap_pair(arrays, *, name):
    n = len(arrays)

    def body(*refs):
        ins, outs, (send_sems, recv_sems) = refs[:n], refs[n:2 * n], refs[2 * n:]
        x, y, c = _place()
        copies = [_remote(ins[a], outs[a], send_sems.at[a], recv_sems.at[a], (x, y, 1 - c)) for a in range(n)]
        for cp in copies:
            cp.start()
        for cp in copies:
            cp.wait_recv()
        for cp in copies:
            cp.wait_send()

    shapes = [jax.ShapeDtypeStruct(s.shape, s.dtype) for s in arrays]
    return _comm_call(body, arrays, shapes, n, 0, name)


def _gather_all(tile, *, name):
    def body(in_ref, out_ref, send_sems, recv_sems, local_sems):
        x, y, c = _place()
        mine = 4 * x + 2 * y + c
        local = pltpu.make_async_copy(in_ref, out_ref.at[mine], local_sems.at[0])
        remote, landing = [], []
        for k in range(1, N_DEV):
            peer = (_flip(x, k & 4), _flip(y, k & 2), _flip(c, k & 1))
            sems = send_sems.at[k - 1], recv_sems.at[k - 1]
            remote.append(_remote(in_ref, out_ref.at[mine], *sems, peer))
            landing.append(_remote(in_ref, out_ref.at[4 * peer[0] + 2 * peer[1] + peer[2]], *sems, peer))
        for cp in [local] + remote:
            cp.start()
        local.wait()
        for cp in landing:
            cp.wait_recv()
        for cp in remote:
            cp.wait_send()

    return _comm_call(body, [tile], [jax.ShapeDtypeStruct((N_DEV,) + tile.shape, tile.dtype)], N_DEV - 1, 1, name)[0]


def _sum_slots(slabs, *, name, tr=128):
    s, r, c = slabs.shape
    tr = min(tr, r)

    def body(in_ref, o_ref):
        acc = in_ref[0]
        for j in range(1, s):
            acc = acc + in_ref[j]
        o_ref[...] = acc

    return pl.pallas_call(
        body, name=name, grid=(r // tr,),
        in_specs=[pl.BlockSpec((s, tr, c), lambda i: (0, i, 0))], out_specs=pl.BlockSpec((tr, c), lambda i: (i, 0)),
        out_shape=jax.ShapeDtypeStruct((r, c), F32), compiler_params=_params("parallel"))(slabs)


def _adamw_math(w, g, m, v):
    m = ADAM_B1 * m + (1.0 - ADAM_B1) * g
    v = ADAM_B2 * v + (1.0 - ADAM_B2) * (g * g)
    m_hat = m / (1.0 - ADAM_B1 ** ADAM_STEP)
    v_hat = v / (1.0 - ADAM_B2 ** ADAM_STEP)
    delta = -ADAM_LR * (m_hat / (jnp.sqrt(v_hat) + ADAM_EPS) + ADAM_WD * w)
    return delta, m, v


def _adamw(w, m, v, parts, *, name, tr=128):
    r, c = w.shape
    tr = min(tr, r)
    s = len(parts)

    def body(w_ref, m_ref, v_ref, *refs):
        g_ref, d_ref, nm_ref, nv_ref = refs[s:]
        g = refs[0][...]
        for p_ref in refs[1:s]:
            g = g + p_ref[...]
        g_ref[...] = g
        d_ref[...], nm_ref[...], nv_ref[...] = _adamw_math(w_ref[...], g, m_ref[...], v_ref[...])

    blk = pl.BlockSpec((tr, c), lambda i: (i, 0))
    shape = jax.ShapeDtypeStruct((r, c), F32)
    return pl.pallas_call(
        body, name=name, grid=(r // tr,), in_specs=[blk] * (3 + s), out_specs=[blk] * 4, out_shape=[shape] * 4,
        compiler_params=_params("parallel"))(w, m, v, *parts)


_BIG = ("a_w_in", "b_w_in", "a_w_out", "b_w_out", "a_conv_w")
_BIG_BY_COLS = {"a_w_in": True, "b_w_in": True, "a_w_out": False, "b_w_out": False, "a_conv_w": True}
_SMALL = ("norm_w", "a_a_log", "a_dt_bias", "a_out_norm_w", "b_q_norm_w", "b_k_norm_w", "b_rel_bias")
_ORDER = ("norm_w", "a_w_in", "a_conv_w", "a_a_log", "a_dt_bias", "a_out_norm_w", "a_w_out", "b_w_in", "b_q_norm_w",
          "b_k_norm_w", "b_rel_bias", "b_w_out")


def _join_cols(g):
    return jnp.transpose(g, (1, 0, 2)).reshape(g.shape[1], -1)


def _split_cols(g):
    return jnp.transpose(g.reshape(g.shape[0], N_CHIPS, -1), (1, 0, 2))


def _pack(d):
    flat = jnp.concatenate([d[n].reshape(-1) for n in _SMALL])
    return jnp.pad(flat, (0, -flat.shape[0] % LANES)).reshape(1, -1)


def _unpack(row, like):
    out, at = {}, 0
    for n in _SMALL:
        size = like[n].size
        out[n] = row[0, at:at + size].reshape(like[n].shape)
        at += size
    return out


def kernel(x, norm_w, a_w_in, a_conv_w, a_a_log, a_dt_bias, a_out_norm_w, a_w_out, b_w_in, b_q_norm_w, b_k_norm_w, b_rel_bias, b_w_out, loss_target, m_norm_w, m_a_w_in, m_a_conv_w, m_a_a_log, m_a_dt_bias, m_a_out_norm_w, m_a_w_out, m_b_w_in, m_b_q_norm_w, m_b_k_norm_w, m_b_rel_bias, m_b_w_out, v_norm_w, v_a_w_in, v_a_conv_w, v_a_a_log, v_a_dt_bias, v_a_out_norm_w, v_a_w_out, v_b_w_in, v_b_q_norm_w, v_b_k_norm_w, v_b_rel_bias, v_b_w_out):
    w = dict(norm_w=norm_w, a_w_in=a_w_in, a_conv_w=a_conv_w, a_a_log=a_a_log, a_dt_bias=a_dt_bias,
             a_out_norm_w=a_out_norm_w, a_w_out=a_w_out, b_w_in=b_w_in, b_q_norm_w=b_q_norm_w, b_k_norm_w=b_k_norm_w,
             b_rel_bias=b_rel_bias, b_w_out=b_w_out)
    m = dict(norm_w=m_norm_w, a_w_in=m_a_w_in, a_conv_w=m_a_conv_w, a_a_log=m_a_a_log, a_dt_bias=m_a_dt_bias,
             a_out_norm_w=m_a_out_norm_w, a_w_out=m_a_w_out, b_w_in=m_b_w_in, b_q_norm_w=m_b_q_norm_w,
             b_k_norm_w=m_b_k_norm_w, b_rel_bias=m_b_rel_bias, b_w_out=m_b_w_out)
    v = dict(norm_w=v_norm_w, a_w_in=v_a_w_in, a_conv_w=v_a_conv_w, a_a_log=v_a_a_log, a_dt_bias=v_a_dt_bias,
             a_out_norm_w=v_a_out_norm_w, a_w_out=v_a_w_out, b_w_in=v_b_w_in, b_q_norm_w=v_b_q_norm_w,
             b_k_norm_w=v_b_k_norm_w, b_rel_bias=v_b_rel_bias, b_w_out=v_b_w_out)

    shards = [w[n][0].astype(F32 if n == "a_conv_w" else BF16) for n in _BIG]
    full = {n: (_join_cols(g) if _BIG_BY_COLS[n] else g.reshape(-1, g.shape[-1]))
            for n, g in zip(_BIG, _gather_chips(shards, name="gather_weights"))}

    loss, dx, grads = _local_step(
        x[0], loss_target[0], norm_w, full["a_w_in"], full["a_conv_w"], a_a_log, a_dt_bias, a_out_norm_w, full["a_w_out"],
        full["b_w_in"], b_q_norm_w, b_k_norm_w, b_rel_bias, full["b_w_out"])
    loss = lax.psum(loss, ("x", "y", "c"))

    slabs = [_split_cols(grads[n]) if _BIG_BY_COLS[n] else grads[n].reshape(N_CHIPS, -1, grads[n].shape[-1]) for n in _BIG]
    landed = _exchange_chips(slabs, name="scatter_grads")
    mine = [_sum_slots(s, name=f"chip_sum_{n}") for n, s in zip(_BIG, landed)]
    theirs = _swap_pair(mine, name="pair_grads")
    out = {}
    for n, p, q in zip(_BIG, mine, theirs):
        out[n] = [r[None] for r in _adamw(w[n][0], m[n][0], v[n][0], [p, q], name=f"adamw_{n}")]

    row = _pack(grads)
    tiles = _gather_all(jnp.broadcast_to(row, (8, row.shape[1])), name="gather_small_grads")
    res = _adamw(_pack(w), _pack(m), _pack(v), [tiles[d, 0:1, :] for d in range(N_DEV)], name="adamw_small")
    unpacked = [_unpack(r, w) for r in res]
    for n in _SMALL:
        out[n] = [u[n] for u in unpacked]

    return (loss, dx[None], *[out[n][0] for n in _ORDER], *[out[n][1] for n in _ORDER], *[out[n][2] for n in _ORDER],
            *[out[n][3] for n in _ORDER])
```

```python
import functools

import jax
import jax.numpy as jnp
from jax import lax
from jax.experimental import pallas as pl
from jax.experimental.pallas import tpu as pltpu

F32 = jnp.float32
BF16 = jnp.bfloat16

CHUNK = 64
HEAD_DIM = 128
LEFT_CHUNKS = 8
REL_CLIP = 256
CONV_K = 4
EPS = 1e-6
HALO = 8

ADAM_LR = 0.001
ADAM_B1 = 0.9
ADAM_B2 = 0.999
ADAM_EPS = 1e-08
ADAM_WD = 0.01
ADAM_STEP = 10

LANES = 128
N_CHIPS = 4
N_DEV = 8
VMEM_LIMIT_BYTES = 56 * 1024 * 1024
MESH = pl.DeviceIdType.MESH
HIGHEST = lax.Precision.HIGHEST


def _params(*sem):
    return pltpu.CompilerParams(dimension_semantics=sem, vmem_limit_bytes=VMEM_LIMIT_BYTES)


def _dot(a, b, dims=(((1,), (0,)), ((), ())), precision=None):
    return lax.dot_general(a, b, dims, precision=precision, preferred_element_type=F32)


_NT = (((1,), (1,)), ((), ()))
_TN = (((0,), (0,)), ((), ()))


def _bdot(a, b, dims=(((1,), (0,)), ((), ()))):
    return _dot(a.astype(BF16), b.astype(BF16), dims)


def _fdot(a, b, dims=(((1,), (0,)), ((), ()))):
    return _dot(a, b, dims, precision=lax.Precision.HIGH)


def _silu(x):
    return x * jax.nn.sigmoid(x)


def _matmul(a, b, *, name, trans_b=False, residual=None, out_dtype=F32, tm=1024, tn=1024, tk=1024):
    m, k = a.shape
    n = b.shape[0] if trans_b else b.shape[1]
    tm, tn, tk = min(tm, m), min(tn, n), min(tk, k)
    assert m % tm == 0 and n % tn == 0 and k % tk == 0, (a.shape, b.shape, tm, tn, tk)
    nk = k // tk
    dims = _NT if trans_b else (((1,), (0,)), ((), ()))

    def body(*refs):
        if residual is None:
            a_ref, b_ref, o_ref, acc_ref = refs
            r_ref = None
        else:
            a_ref, b_ref, r_ref, o_ref, acc_ref = refs
        kk = pl.program_id(2)

        @pl.when(kk == 0)
        def _():
            acc_ref[...] = jnp.zeros_like(acc_ref)

        acc_ref[...] += _dot(a_ref[...], b_ref[...], dims)

        @pl.when(kk == nk - 1)
        def _():
            r = acc_ref[...]
            if r_ref is not None:
                r = r + r_ref[...]
            o_ref[...] = r.astype(o_ref.dtype)

    in_specs = [
        pl.BlockSpec((tm, tk), lambda i, j, kk: (i, kk)),
        pl.BlockSpec((tn, tk), lambda i, j, kk: (j, kk)) if trans_b else pl.BlockSpec((tk, tn), lambda i, j, kk: (kk, j)),
    ]
    args = [a, b]
    if residual is not None:
        in_specs.append(pl.BlockSpec((tm, tn), lambda i, j, kk: (i, j)))
        args.append(residual)
    return pl.pallas_call(
        body,
        name=name,
        grid=(m // tm, n // tn, nk),
        in_specs=in_specs,
        out_specs=pl.BlockSpec((tm, tn), lambda i, j, kk: (i, j)),
        out_shape=jax.ShapeDtypeStruct((m, n), out_dtype),
        scratch_shapes=[pltpu.VMEM((tm, tn), F32)],
        compiler_params=_params("parallel", "parallel", "arbitrary"),
    )(*args)


def _rms(x, w):
    return x * lax.rsqrt(jnp.mean(x * x, axis=-1, keepdims=True) + EPS) * w


def _rmsnorm_fwd(x, w_row, *, name, tr=512):
    t, d = x.shape
    tr = min(tr, t)

    def body(x_ref, w_ref, o_ref):
        o_ref[...] = _rms(x_ref[...], w_ref[...]).astype(BF16)

    return pl.pallas_call(
        body,
        name=name,
        grid=(t // tr,),
        in_specs=[pl.BlockSpec((tr, d), lambda i: (i, 0)), pl.BlockSpec((1, d), lambda i: (0, 0))],
        out_specs=pl.BlockSpec((tr, d), lambda i: (i, 0)),
        out_shape=jax.ShapeDtypeStruct((t, d), BF16),
        compiler_params=_params("parallel"),
    )(x, w_row)


def _rmsnorm_bwd(x, w_row, dy, dres, *, name, tr=256):
    t, d = x.shape
    tr = min(tr, t)

    def body(x_ref, w_ref, dy_ref, dres_ref, dx_ref, dxb_ref, dw_ref):
        @pl.when(pl.program_id(0) == 0)
        def _():
            dw_ref[...] = jnp.zeros_like(dw_ref)

        _, vjp = jax.vjp(_rms, x_ref[...], w_ref[...])
        dx, dw = vjp(dy_ref[...])
        dx = dx + dres_ref[...]
        dx_ref[...] = dx
        dxb_ref[...] = dx.astype(BF16)
        dw_ref[...] += dw

    row = pl.BlockSpec((tr, d), lambda i: (i, 0))
    vec = pl.BlockSpec((1, d), lambda i: (0, 0))
    return pl.pallas_call(
        body,
        name=name,
        grid=(t // tr,),
        in_specs=[row, vec, row, row],
        out_specs=[row, row, vec],
        out_shape=[jax.ShapeDtypeStruct((t, d), F32), jax.ShapeDtypeStruct((t, d), BF16), jax.ShapeDtypeStruct((1, d), F32)],
        compiler_params=_params("arbitrary"),
    )(x, w_row, dy, dres)


def _loss_head(h, target, *, name, tr=512):
    t, d = h.shape
    tr = min(tr, t)

    def body(h_ref, t_ref, dh_ref, dhb_ref, part_ref):
        @pl.when(pl.program_id(0) == 0)
        def _():
            part_ref[...] = jnp.zeros_like(part_ref)

        err = h_ref[...] - t_ref[...]
        dh = err * (1.0 / d)
        dh_ref[...] = dh
        dhb_ref[...] = dh.astype(BF16)
        part_ref[...] += jnp.sum(err * err, axis=0, keepdims=True)

    row = pl.BlockSpec((tr, d), lambda i: (i, 0))
    vec = pl.BlockSpec((1, d), lambda i: (0, 0))
    dh, dhb, part = pl.pallas_call(
        body,
        name=name,
        grid=(t // tr,),
        in_specs=[row, row],
        out_specs=[row, row, vec],
        out_shape=[jax.ShapeDtypeStruct((t, d), F32), jax.ShapeDtypeStruct((t, d), BF16), jax.ShapeDtypeStruct((1, d), F32)],
        compiler_params=_params("arbitrary"),
    )(h, target)
    return 0.5 / d * jnp.sum(part), dh, dhb


_BNN = (((2,), (1,)), ((0,), (0,)))
_BNT = (((2,), (2,)), ((0,), (0,)))
_BTN = (((1,), (1,)), ((0,), (0,)))


def _conv_silu(xe, w):
    rows = xe.shape[0] - HALO
    first = HALO - (CONV_K - 1)
    c = w[0:1, :] * xe[first:first + rows, :]
    for j in range(1, CONV_K):
        c = c + w[j:j + 1, :] * xe[first + j:first + j + rows, :]
    return _silu(c)


def _gdn_intra(qx, kx, vx, a, b, wq, wk, wv, alog, dtb):
    n = a.shape[0] // CHUNK
    qt, kt, v = _conv_silu(qx, wq), _conv_silu(kx, wk), _conv_silu(vx, wv)
    q = qt * lax.rsqrt(jnp.sum(qt * qt, axis=-1, keepdims=True) + EPS) * (HEAD_DIM ** -0.5)
    k = kt * lax.rsqrt(jnp.sum(kt * kt, axis=-1, keepdims=True) + EPS)
    lanes = jnp.ones((1, HEAD_DIM), F32)
    beta = jax.nn.sigmoid(b) * lanes
    sp = a + dtb
    g = (-jnp.exp(alog) * (jnp.maximum(sp, 0.0) + jnp.log(1.0 + jnp.exp(-jnp.abs(sp))))) * lanes
    q, k, v, beta, g = (t.reshape(n, CHUNK, HEAD_DIM) for t in (q, k, v, beta, g))

    row = lax.broadcasted_iota(jnp.int32, (n, CHUNK, CHUNK), 1)
    col = lax.broadcasted_iota(jnp.int32, (n, CHUNK, CHUNK), 2)
    tri_incl = row >= col
    tri_strict = row > col
    gc = _fdot(tri_incl.astype(F32), g, _BNN)
    gc_row = _fdot(g[:, :, :CHUNK], (row <= col).astype(F32), _BTN)
    decay = jnp.exp(jnp.where(tri_incl, gc[:, :, :CHUNK] - gc_row, -1e30))
    kb = k * beta
    vb = v * beta
    neg_l = jnp.where(tri_strict, -(_bdot(kb, k, _BNT) * decay), 0.0)
    inv = (row == col).astype(F32) + neg_l
    power = neg_l
    for _ in range(5):
        power = _fdot(power, power, _BNN)
        inv = inv + _fdot(inv, power, _BNN)
    e = jnp.exp(gc)
    u = _fdot(inv, vb, _BNN)
    w = _fdot(inv, kb * e, _BNN)
    qk = jnp.where(tri_incl, _bdot(q, k, _BNT) * decay, 0.0)
    g_last = gc[:, CHUNK - 1:CHUNK, :]
    return u, w, qk, q * e, k * jnp.exp(g_last - gc), jnp.exp(g_last)


def _gdn_inter(state, u, w, qk, q_dec, k_dec, decay_last, z, onw):
    state_b = state.astype(BF16)
    v_new = u - _bdot(w, state_b)
    o = _bdot(q_dec, state_b) + _bdot(qk, v_new)
    new_state = state * decay_last + _bdot(k_dec, v_new, _TN)
    return _rms(o, onw) * _silu(z), new_state


def _intra_scratch(n, dtype):
    return [pltpu.VMEM((n, CHUNK, HEAD_DIM), F32), pltpu.VMEM((n, CHUNK, HEAD_DIM), dtype), pltpu.VMEM((n, CHUNK, CHUNK), dtype),
            pltpu.VMEM((n, CHUNK, HEAD_DIM), dtype), pltpu.VMEM((n, CHUNK, HEAD_DIM), dtype), pltpu.VMEM((n, 1, HEAD_DIM), F32)]


def _head_lane(h, offset=0):
    return lax.broadcasted_iota(jnp.int32, (1, LANES), 1) == h + offset


def _pick(mask, x):
    return jnp.sum(jnp.where(mask, x, 0.0), axis=1, keepdims=True)


def _gdn_specs(heads, tb, rev, nb):
    blk = (lambda i: nb - 1 - i) if rev else (lambda i: i)
    hb = tb // HALO

    def col(group):
        return pl.BlockSpec((tb, HEAD_DIM), lambda i, h: (blk(i), group * heads + h))

    def halo(group):
        return pl.BlockSpec((HALO, HEAD_DIM), lambda i, h: (jnp.maximum(blk(i) * hb - 1, 0), group * heads + h))

    def convw(group):
        return pl.BlockSpec((CONV_K, HEAD_DIM), lambda i, h: (0, group * heads + h))

    vec = pl.BlockSpec((1, LANES), lambda i, h: (0, 0))
    ab = pl.BlockSpec((tb, LANES), lambda i, h: (blk(i), 0))
    states = pl.BlockSpec((1, tb // CHUNK, HEAD_DIM, HEAD_DIM), lambda i, h: (h, blk(i), 0, 0))
    return blk, col, halo, convw, vec, ab, states


def _gdn_fwd(proj, ab, conv_w, alog_row, dtb_row, onw_row, *, heads, name, tb=512):
    t = proj.shape[0]
    tb = min(tb, t)
    nb, cpb = t // tb, tb // CHUNK
    _, col, halo, convw, vec, abspec, states = _gdn_specs(heads, tb, False, nb)

    def body(q_ref, k_ref, v_ref, qh_ref, kh_ref, vh_ref, z_ref, ab_ref, wq_ref, wk_ref, wv_ref, alog_ref, dtb_ref, onw_ref,
             og_ref, st_ref, state_scr, x_scr, *op_scr):
        i, h = pl.program_id(0), pl.program_id(1)
        for n, (ref, href) in enumerate(((q_ref, qh_ref), (k_ref, kh_ref), (v_ref, vh_ref))):
            x_scr[n, 0:HALO, :] = jnp.where(i > 0, href[...], 0.0)
            x_scr[n, HALO:HALO + tb, :] = ref[...]

        @pl.when(i == 0)
        def _():
            state_scr[h] = jnp.zeros((HEAD_DIM, HEAD_DIM), F32)

        sel_a, sel_b = _head_lane(h), _head_lane(h, heads)
        alog, dtb = _pick(sel_a, alog_ref[...]), _pick(sel_a, dtb_ref[...])
        abv = ab_ref[...]
        ops = _gdn_intra(x_scr[0], x_scr[1], x_scr[2], _pick(sel_a, abv), _pick(sel_b, abv), wq_ref[...], wk_ref[...],
                         wv_ref[...], alog, dtb)
        for scr, val in zip(op_scr, ops):
            scr[...] = val.astype(scr.dtype)
        onw = onw_ref[...]

        def chunk(c, state):
            rows = pl.ds(pl.multiple_of(c * CHUNK, CHUNK), CHUNK)
            st_ref[0, c] = state
            og, new_state = _gdn_inter(state, *[scr[c] for scr in op_scr], z_ref[rows, :], onw)
            og_ref[rows, :] = og.astype(BF16)
            return new_state

        state_scr[h] = lax.fori_loop(0, cpb, chunk, state_scr[h])

    return pl.pallas_call(
        body,
        name=name,
        grid=(nb, heads),
        in_specs=[col(0), col(1), col(2), halo(0), halo(1), halo(2), col(3), abspec, convw(0), convw(1), convw(2), vec, vec, vec],
        out_specs=[pl.BlockSpec((tb, HEAD_DIM), lambda i, h: (i, h)), states],
        out_shape=[jax.ShapeDtypeStruct((t, heads * HEAD_DIM), BF16),
                   jax.ShapeDtypeStruct((heads, t // CHUNK, HEAD_DIM, HEAD_DIM), F32)],
        scratch_shapes=[pltpu.VMEM((heads, HEAD_DIM, HEAD_DIM), F32), pltpu.VMEM((3, HALO + tb, HEAD_DIM), F32)]
        + _intra_scratch(cpb, BF16),
        compiler_params=_params("arbitrary", "arbitrary"),
    )(proj, proj, proj, proj, proj, proj, proj, ab, conv_w, conv_w, conv_w, alog_row, dtb_row, onw_row)


def _gdn_bwd(proj, ab, conv_w, alog_row, dtb_row, onw_row, states, dog, *, heads, name, tb=512):
    t = proj.shape[0]
    tb = min(tb, t)
    nb, cpb = t // tb, tb // CHUNK
    _, col, halo, convw, vec, abspec, states_spec = _gdn_specs(heads, tb, True, nb)
    n_conv = conv_w.shape[1]

    def body(q_ref, k_ref, v_ref, qh_ref, kh_ref, vh_ref, z_ref, ab_ref, wq_ref, wk_ref, wv_ref, alog_ref, dtb_ref, onw_ref,
             st_ref, dog_ref, dq_ref, dk_ref, dv_ref, dz_ref, dab_ref, dconv_ref, dalog_ref, ddtb_ref, donw_ref,
             dstate_scr, x_scr, carry_scr, *scr):
        op_scr, dop_scr = scr[:6], scr[6:]
        i, h = pl.program_id(0), pl.program_id(1)
        first_block = i == nb - 1
        for n, (ref, href) in enumerate(((q_ref, qh_ref), (k_ref, kh_ref), (v_ref, vh_ref))):
            x_scr[n, 0:HALO, :] = jnp.where(first_block, 0.0, href[...])
            x_scr[n, HALO:HALO + tb, :] = ref[...]

        @pl.when(jnp.logical_and(i == 0, h == 0))
        def _():
            dconv_ref[...] = jnp.zeros_like(dconv_ref)
            dalog_ref[...] = jnp.zeros_like(dalog_ref)
            ddtb_ref[...] = jnp.zeros_like(ddtb_ref)
            donw_ref[...] = jnp.zeros_like(donw_ref)

        @pl.when(h == 0)
        def _():
            dab_ref[...] = jnp.zeros_like(dab_ref)

        @pl.when(i == 0)
        def _():
            dstate_scr[h] = jnp.zeros((HEAD_DIM, HEAD_DIM), F32)
            carry_scr[h] = jnp.zeros((3, HALO, HEAD_DIM), F32)

        sel_a, sel_b = _head_lane(h), _head_lane(h, heads)
        alog, dtb = _pick(sel_a, alog_ref[...]), _pick(sel_a, dtb_ref[...])
        abv = ab_ref[...]
        ops, vjp_intra = jax.vjp(_gdn_intra, x_scr[0], x_scr[1], x_scr[2], _pick(sel_a, abv), _pick(sel_b, abv), wq_ref[...],
                                 wk_ref[...], wv_ref[...], alog, dtb)
        for s, val in zip(op_scr, ops):
            s[...] = val.astype(s.dtype)
        onw = onw_ref[...]

        def chunk(step, carry):
            dstate, donw = carry
            c = cpb - 1 - step
            rows = pl.ds(pl.multiple_of(c * CHUNK, CHUNK), CHUNK)
            _, vjp = jax.vjp(_gdn_inter, st_ref[0, c], *[s[c].astype(F32) for s in op_scr], z_ref[rows, :], onw)
            grads = vjp((dog_ref[rows, :], dstate))
            for s, val in zip(dop_scr, grads[1:7]):
                s[c] = val
            dz_ref[rows, :] = grads[7].astype(BF16)
            return grads[0], donw + grads[8]

        dstate, donw = lax.fori_loop(0, cpb, chunk, (dstate_scr[h], jnp.zeros((1, HEAD_DIM), F32)))
        dstate_scr[h] = dstate
        dqx, dkx, dvx, da, db, dwq, dwk, dwv, dalog, ddtb = vjp_intra(tuple(s[...] for s in dop_scr))
        dab_ref[...] += jnp.where(sel_a, da, 0.0) + jnp.where(sel_b, db, 0.0)
        for n, (dref, dx, dw) in enumerate(((dq_ref, dqx, dwq), (dk_ref, dkx, dwk), (dv_ref, dvx, dwv))):
            x_scr[n] = dx
            x_scr[n, tb:tb + HALO, :] += carry_scr[h, n]
            carry_scr[h, n] = x_scr[n, 0:HALO, :]
            dref[...] = x_scr[n, HALO:HALO + tb, :].astype(BF16)
            lanes = pl.ds(pl.multiple_of((n * heads + h) * HEAD_DIM, HEAD_DIM), HEAD_DIM)
            dconv_ref[:, lanes] += dw
        dalog_ref[...] += jnp.where(sel_a, dalog, 0.0)
        ddtb_ref[...] += jnp.where(sel_a, ddtb, 0.0)
        donw_ref[...] += donw

    out_col = pl.BlockSpec((tb, HEAD_DIM), lambda i, h: (nb - 1 - i, h))
    dog_spec = pl.BlockSpec((tb, HEAD_DIM), lambda i, h: (nb - 1 - i, h))
    col_shape = jax.ShapeDtypeStruct((t, heads * HEAD_DIM), BF16)
    row_shape = jax.ShapeDtypeStruct((1, LANES), F32)
    return pl.pallas_call(
        body,
        name=name,
        grid=(nb, heads),
        in_specs=[col(0), col(1), col(2), halo(0), halo(1), halo(2), col(3), abspec, convw(0), convw(1), convw(2), vec, vec, vec,
                  states_spec, dog_spec],
        out_specs=[out_col, out_col, out_col, out_col, abspec,
                   pl.BlockSpec((CONV_K, n_conv), lambda i, h: (0, 0)), vec, vec, vec],
        out_shape=[col_shape, col_shape, col_shape, col_shape, jax.ShapeDtypeStruct((t, LANES), F32),
                   jax.ShapeDtypeStruct((CONV_K, n_conv), F32), row_shape, row_shape, row_shape],
        scratch_shapes=[pltpu.VMEM((heads, HEAD_DIM, HEAD_DIM), F32), pltpu.VMEM((3, HALO + tb, HEAD_DIM), F32),
                        pltpu.VMEM((heads, 3, HALO, HEAD_DIM), F32)] + _intra_scratch(cpb, BF16) + _intra_scratch(cpb, F32),
        compiler_params=_params("arbitrary", "arbitrary"),
    )(proj, proj, proj, proj, proj, proj, proj, ab, conv_w, conv_w, conv_w, alog_row, dtb_row, onw_row, states, dog)


BAND = (LEFT_CHUNKS + 1) * CHUNK
PAD = LEFT_CHUNKS * CHUNK


DIAGS = BAND + CHUNK - 1
NEAR = PAD + CHUNK - 1 - REL_CLIP
assert 0 < NEAR < DIAGS


def _band_bias(rel_bias):
    heads = rel_bias.shape[0]
    far = jnp.broadcast_to(rel_bias[:, 2 * REL_CLIP:], (heads, NEAR + 1))
    near = rel_bias[:, 2 * REL_CLIP + NEAR + 1 - DIAGS:2 * REL_CLIP][:, ::-1]
    diag = jnp.concatenate([far, near], axis=1)
    return jnp.stack([diag[:, CHUNK - 1 - r:CHUNK - 1 - r + BAND] for r in range(CHUNK)], axis=1)


def _band_bias_grad(dbias):
    heads = dbias.shape[0]
    diag = sum(jnp.pad(dbias[:, r, :], ((0, 0), (CHUNK - 1 - r, r))) for r in range(CHUNK))
    far = jnp.sum(diag[:, :NEAR + 1], axis=1, keepdims=True)
    near = diag[:, NEAR + 1:][:, ::-1]
    unused = jnp.zeros((heads, 2 * REL_CLIP - near.shape[1]), F32)
    return jnp.concatenate([unused, near, far], axis=1)


def _attn_chunk(q_pre, z, kn, v, bias, qnw, first_valid):
    q = _rms(q_pre, qnw)
    s = _bdot(q, kn, _NT) * (HEAD_DIM ** -0.5) + bias
    valid = lax.broadcasted_iota(jnp.int32, (CHUNK, BAND), 1) >= first_valid
    s = jnp.where(valid, s, -1e30)
    p = jnp.exp(s - jnp.max(s, axis=-1, keepdims=True))
    p = p / jnp.sum(p, axis=-1, keepdims=True)
    return _bdot(p, v) * _silu(z)


def _attn_specs(heads, tb, t):
    def col(group):
        return pl.BlockSpec((tb, HEAD_DIM), lambda h, i: (i, group * heads + h))

    def full(group):
        return pl.BlockSpec((t, HEAD_DIM), lambda h, i: (0, group * heads + h))

    bias = pl.BlockSpec((1, CHUNK, BAND), lambda h, i: (h, 0, 0))
    vec = pl.BlockSpec((1, HEAD_DIM), lambda h, i: (0, 0))
    return col, full, bias, vec


def _attn_fill(k_ref, v_ref, knw_ref, kn_scr, v_scr, t):
    kn_scr[0:PAD, :] = jnp.zeros((PAD, HEAD_DIM), BF16)
    v_scr[0:PAD, :] = jnp.zeros((PAD, HEAD_DIM), BF16)
    step = min(512, t)

    def fill(j, _):
        rows = pl.ds(pl.multiple_of(j * step, step), step)
        prows = pl.ds(pl.multiple_of(PAD + j * step, CHUNK), step)
        kn_scr[prows, :] = _rms(k_ref[rows, :], knw_ref[...]).astype(BF16)
        v_scr[prows, :] = v_ref[rows, :].astype(BF16)
        return 0

    lax.fori_loop(0, t // step, fill, 0)


def _attn_fwd(proj, bias, qnw_row, knw_row, *, heads, name, tb=512):
    t = proj.shape[0]
    tb = min(tb, t)
    nb, cpb = t // tb, tb // CHUNK
    col, full, bias_spec, vec = _attn_specs(heads, tb, t)

    def body(q_ref, k_ref, v_ref, z_ref, bias_ref, qnw_ref, knw_ref, og_ref, kn_scr, v_scr):
        i = pl.program_id(1)

        @pl.when(i == 0)
        def _():
            _attn_fill(k_ref, v_ref, knw_ref, kn_scr, v_scr, t)

        b = bias_ref[0]
        qnw = qnw_ref[...]

        def chunk(c, _):
            r0 = pl.multiple_of(c * CHUNK, CHUNK)
            rows = pl.ds(r0, CHUNK)
            start = pl.multiple_of(i * tb + r0, CHUNK)
            band = pl.ds(start, BAND)
            og = _attn_chunk(q_ref[rows, :], z_ref[rows, :], kn_scr[band, :], v_scr[band, :], b, qnw, PAD - start)
            og_ref[rows, :] = og.astype(BF16)
            return 0

        lax.fori_loop(0, cpb, chunk, 0)

    return pl.pallas_call(
        body,
        name=name,
        grid=(heads, nb),
        in_specs=[col(0), full(1), full(2), col(3), bias_spec, vec, vec],
        out_specs=pl.BlockSpec((tb, HEAD_DIM), lambda h, i: (i, h)),
        out_shape=jax.ShapeDtypeStruct((t, heads * HEAD_DIM), BF16),
        scratch_shapes=[pltpu.VMEM((PAD + t, HEAD_DIM), BF16), pltpu.VMEM((PAD + t, HEAD_DIM), BF16)],
        compiler_params=_params("arbitrary", "arbitrary"),
    )(proj, proj, proj, proj, bias, qnw_row, knw_row)


def _attn_bwd(proj, bias, qnw_row, knw_row, dog, *, heads, name, tb=512):
    t = proj.shape[0]
    tb = min(tb, t)
    nb, cpb = t // tb, tb // CHUNK
    col, full, bias_spec, vec = _attn_specs(heads, tb, t)

    def body(q_ref, k_ref, v_ref, z_ref, bias_ref, qnw_ref, knw_ref, dog_ref,
             dq_ref, dk_ref, dv_ref, dz_ref, dbias_ref, dqnw_ref, dknw_ref, kn_scr, v_scr, dkn_scr, dv_scr):
        i = pl.program_id(1)

        @pl.when(i == 0)
        def _():
            _attn_fill(k_ref, v_ref, knw_ref, kn_scr, v_scr, t)
            dkn_scr[...] = jnp.zeros_like(dkn_scr)
            dv_scr[...] = jnp.zeros_like(dv_scr)
            dbias_ref[...] = jnp.zeros_like(dbias_ref)
            dqnw_ref[...] = jnp.zeros_like(dqnw_ref)

        b = bias_ref[0]
        qnw = qnw_ref[...]

        def chunk(c, carry):
            dbias, dqnw = carry
            r0 = pl.multiple_of(c * CHUNK, CHUNK)
            rows = pl.ds(r0, CHUNK)
            start = pl.multiple_of(i * tb + r0, CHUNK)
            band = pl.ds(start, BAND)
            first_valid = PAD - start
            _, vjp = jax.vjp(
                lambda q_pre, z, kn, v, bb, w: _attn_chunk(q_pre, z, kn, v, bb, w, first_valid),
                q_ref[rows, :], z_ref[rows, :], kn_scr[band, :].astype(F32), v_scr[band, :].astype(F32), b, qnw)
            dq, dz, dkn, dv, db, dw = vjp(dog_ref[rows, :])
            dq_ref[rows, :] = dq.astype(BF16)
            dz_ref[rows, :] = dz.astype(BF16)
            dkn_scr[band, :] += dkn
            dv_scr[band, :] += dv
            return dbias + db, dqnw + dw

        dbias, dqnw = lax.fori_loop(0, cpb, chunk, (jnp.zeros((CHUNK, BAND), F32), jnp.zeros((1, HEAD_DIM), F32)))
        dbias_ref[0] += dbias
        dqnw_ref[0] += dqnw

        @pl.when(i == nb - 1)
        def _():
            step = min(512, t)

            def finish(j, dknw):
                rows = pl.ds(pl.multiple_of(j * step, step), step)
                prows = pl.ds(pl.multiple_of(PAD + j * step, CHUNK), step)
                _, vjp = jax.vjp(_rms, k_ref[rows, :], knw_ref[...])
                dk, dw = vjp(dkn_scr[prows, :])
                dk_ref[rows, :] = dk.astype(BF16)
                dv_ref[rows, :] = dv_scr[prows, :].astype(BF16)
                return dknw + dw

            dknw_ref[0] = lax.fori_loop(0, t // step, finish, jnp.zeros((1, HEAD_DIM), F32))

    out_col = pl.BlockSpec((tb, HEAD_DIM), lambda h, i: (i, h))
    out_full = pl.BlockSpec((t, HEAD_DIM), lambda h, i: (0, h))
    head_vec = pl.BlockSpec((1, 1, HEAD_DIM), lambda h, i: (h, 0, 0))
    col_shape = jax.ShapeDtypeStruct((t, heads * HEAD_DIM), BF16)
    vec_shape = jax.ShapeDtypeStruct((heads, 1, HEAD_DIM), F32)
    return pl.pallas_call(
        body,
        name=name,
        grid=(heads, nb),
        in_specs=[col(0), full(1), full(2), col(3), bias_spec, vec, vec, pl.BlockSpec((tb, HEAD_DIM), lambda h, i: (i, h))],
        out_specs=[out_col, out_full, out_full, out_col, bias_spec, head_vec, head_vec],
        out_shape=[col_shape, col_shape, col_shape, col_shape, jax.ShapeDtypeStruct((heads, CHUNK, BAND), F32),
                   vec_shape, vec_shape],
        scratch_shapes=[pltpu.VMEM((PAD + t, HEAD_DIM), BF16), pltpu.VMEM((PAD + t, HEAD_DIM), BF16),
                        pltpu.VMEM((PAD + t, HEAD_DIM), F32), pltpu.VMEM((PAD + t, HEAD_DIM), F32)],
        compiler_params=_params("arbitrary", "arbitrary"),
    )(proj, proj, proj, proj, bias, qnw_row, knw_row, dog)


def _lane_row(v):
    v = v.reshape(1, -1)
    return jnp.pad(v, ((0, 0), (0, LANES - v.shape[1])))


def _local_step(x, target, norm_w, wa_in, conv_w, a_log, dt_bias, onw, wa_out, wb_in, qnw, knw, rel_bias, wb_out):
    ha, hb = a_log.shape[-1], rel_bias.shape[-2]
    na = 4 * ha * HEAD_DIM
    wa_main = wa_in[:, :na]
    wa_ab = jnp.pad(wa_in[:, na:], ((0, 0), (0, LANES - 2 * ha)))
    alog_row, dtb_row, onw_row = _lane_row(a_log), _lane_row(dt_bias), _lane_row(onw)
    qnw_row, knw_row = _lane_row(qnw), _lane_row(knw)
    bias = _band_bias(rel_bias.reshape(hb, -1))

    hn0 = _rmsnorm_fwd(x, norm_w[0:1], name="norm0")
    proj_a = _matmul(hn0, wa_main, name="a_in")
    ab_a = _matmul(hn0, wa_ab, name="a_in_ab")
    og_a, states = _gdn_fwd(proj_a, ab_a, conv_w, alog_row, dtb_row, onw_row, heads=ha, name="gdn_fwd")
    h1 = _matmul(og_a, wa_out, residual=x, name="a_out")
    hn1 = _rmsnorm_fwd(h1, norm_w[1:2], name="norm1")
    proj_b = _matmul(hn1, wb_in, name="b_in")
    og_b = _attn_fwd(proj_b, bias, qnw_row, knw_row, heads=hb, name="attn_fwd")
    h2 = _matmul(og_b, wb_out, residual=h1, name="b_out")
    loss, dh2, dh2_b = _loss_head(h2, target, name="loss_head")

    dog_b = _matmul(dh2_b, wb_out, trans_b=True, name="d_b_out_x")
    dwb_out = _matmul(og_b.T, dh2_b, name="d_b_out_w")
    dq, dk, dv, dz, dbias, dqnw, dknw = _attn_bwd(proj_b, bias, qnw_row, knw_row, dog_b, heads=hb, name="attn_bwd")
    dproj_b = jnp.concatenate([dq, dk, dv, dz], axis=1)
    dhn1 = _matmul(dproj_b, wb_in, trans_b=True, name="d_b_in_x")
    dwb_in = _matmul(hn1.T, dproj_b, name="d_b_in_w")
    dh1, dh1_b, dnw1 = _rmsnorm_bwd(h1, norm_w[1:2], dhn1, dh2, name="d_norm1")

    dog_a = _matmul(dh1_b, wa_out, trans_b=True, name="d_a_out_x")
    dwa_out = _matmul(og_a.T, dh1_b, name="d_a_out_w")
    dq, dk, dv, dz, dab, dconv, dalog, ddtb, donw = _gdn_bwd(
        proj_a, ab_a, conv_w, alog_row, dtb_row, onw_row, states, dog_a, heads=ha, name="gdn_bwd")
    dproj_a = jnp.concatenate([dq, dk, dv, dz], axis=1)
    dab_b = dab.astype(BF16)
    dhn0 = _matmul(dproj_a, wa_main, trans_b=True, name="d_a_in_x")
    dhn0 = _matmul(dab_b, wa_ab, trans_b=True, residual=dhn0, name="d_a_in_ab_x")
    hn0_t = hn0.T
    dwa_in = jnp.concatenate(
        [_matmul(hn0_t, dproj_a, name="d_a_in_w"), _matmul(hn0_t, dab_b, name="d_a_in_ab_w")[:, :2 * ha]], axis=1)
    dx, _, dnw0 = _rmsnorm_bwd(x, norm_w[0:1], dhn0, dh1, name="d_norm0")

    drel = _band_bias_grad(dbias)
    grads = dict(
        norm_w=jnp.concatenate([dnw0, dnw1], axis=0), a_w_in=dwa_in, a_conv_w=dconv, a_a_log=dalog[:, :ha],
        a_dt_bias=ddtb[:, :ha], a_out_norm_w=donw, a_w_out=dwa_out, b_w_in=dwb_in, b_q_norm_w=jnp.sum(dqnw, axis=0),
        b_k_norm_w=jnp.sum(dknw, axis=0), b_rel_bias=drel[None], b_w_out=dwb_out)
    return loss, dx, grads


_ANY = pl.BlockSpec(memory_space=pl.ANY)
_CHIP_FLIPS = ((1, 0), (0, 1), (1, 1))


def _place():
    x, y, c = lax.axis_index("x"), lax.axis_index("y"), lax.axis_index("c")
    return x, y, c


def _flip(v, bit):
    return 1 - v if bit else v


def _remote(src, dst, send_sem, recv_sem, peer):
    return pltpu.make_async_remote_copy(src_ref=src, dst_ref=dst, send_sem=send_sem, recv_sem=recv_sem, device_id=peer,
                                        device_id_type=MESH)


def _comm_call(body, arrays, out_shapes, n_remote, n_local, name):
    scratch = [pltpu.SemaphoreType.DMA((n_remote,)), pltpu.SemaphoreType.DMA((n_remote,))]
    if n_local:
        scratch.append(pltpu.SemaphoreType.DMA((n_local,)))
    return pl.pallas_call(
        body, name=name, in_specs=[_ANY] * len(arrays), out_specs=[_ANY] * len(out_shapes), out_shape=out_shapes,
        scratch_shapes=scratch)(*arrays)


def _gather_chips(shards, *, name):
    n = len(shards)

    def body(*refs):
        ins, outs, (send_sems, recv_sems, local_sems) = refs[:n], refs[n:2 * n], refs[2 * n:]
        x, y, c = _place()
        mine = 2 * x + y
        local, remote, landing = [], [], []
        for a in range(n):
            local.append(pltpu.make_async_copy(ins[a], outs[a].at[mine], local_sems.at[a]))
            for k, (fx, fy) in enumerate(_CHIP_FLIPS):
                peer = (_flip(x, fx), _flip(y, fy), c)
                sems = send_sems.at[3 * a + k], recv_sems.at[3 * a + k]
                remote.append(_remote(ins[a], outs[a].at[mine], *sems, peer))
                landing.append(_remote(ins[a], outs[a].at[2 * peer[0] + peer[1]], *sems, peer))
        for cp in local + remote:
            cp.start()
        for cp in local:
            cp.wait()
        for cp in landing:
            cp.wait_recv()
        for cp in remote:
            cp.wait_send()

    shapes = [jax.ShapeDtypeStruct((N_CHIPS,) + s.shape, s.dtype) for s in shards]
    return _comm_call(body, shards, shapes, 3 * n, n, name)


def _exchange_chips(slabs, *, name):
    n = len(slabs)

    def body(*refs):
        ins, outs, (send_sems, recv_sems, local_sems) = refs[:n], refs[n:2 * n], refs[2 * n:]
        x, y, c = _place()
        mine = 2 * x + y
        local, remote, landing = [], [], []
        for a in range(n):
            local.append(pltpu.make_async_copy(ins[a].at[mine], outs[a].at[mine], local_sems.at[a]))
            for k, (fx, fy) in enumerate(_CHIP_FLIPS):
                peer = (_flip(x, fx), _flip(y, fy), c)
                theirs = 2 * peer[0] + peer[1]
                sems = send_sems.at[3 * a + k], recv_sems.at[3 * a + k]
                remote.append(_remote(ins[a].at[theirs], outs[a].at[mine], *sems, peer))
                landing.append(_remote(ins[a].at[theirs], outs[a].at[theirs], *sems, peer))
        for cp in local + remote:
            cp.start()
        for cp in local:
            cp.wait()
        for cp in landing:
            cp.wait_recv()
        for cp in remote:
            cp.wait_send()

    shapes = [jax.ShapeDtypeStruct(s.shape, s.dtype) for s in slabs]
    return _comm_call(body, slabs, shapes, 3 * n, n, name)


def _swap_pair(arrays, *, name):
    n = len(arrays)

    def body(*refs):
        ins, outs, (send_sems, recv_sems) = refs[:n], refs[n:2 * n], refs[2 * n:]
        x, y, c = _place()
        copies = [_remote(ins[a], outs[a], send_sems.at[a], recv_sems.at[a], (x, y, 1 - c)) for a in range(n)]
        for cp in copies:
            cp.start()
        for cp in copies:
            cp.wait_recv()
        for cp in copies:
            cp.wait_send()

    shapes = [jax.ShapeDtypeStruct(s.shape, s.dtype) for s in arrays]
    return _comm_call(body, arrays, shapes, n, 0, name)


def _gather_all(tile, *, name):
    def body(in_ref, out_ref, send_sems, recv_sems, local_sems):
        x, y, c = _place()
        mine = 4 * x + 2 * y + c
        local = pltpu.make_async_copy(in_ref, out_ref.at[mine], local_sems.at[0])
        remote, landing = [], []
        for k in range(1, N_DEV):
            peer = (_flip(x, k & 4), _flip(y, k & 2), _flip(c, k & 1))
            sems = send_sems.at[k - 1], recv_sems.at[k - 1]
            remote.append(_remote(in_ref, out_ref.at[mine], *sems, peer))
            landing.append(_remote(in_ref, out_ref.at[4 * peer[0] + 2 * peer[1] + peer[2]], *sems, peer))
        for cp in [local] + remote:
            cp.start()
        local.wait()
        for cp in landing:
            cp.wait_recv()
        for cp in remote:
            cp.wait_send()

    return _comm_call(body, [tile], [jax.ShapeDtypeStruct((N_DEV,) + tile.shape, tile.dtype)], N_DEV - 1, 1, name)[0]


def _sum_slots(slabs, *, name, tr=128):
    s, r, c = slabs.shape
    tr = min(tr, r)

    def body(in_ref, o_ref):
        acc = in_ref[0]
        for j in range(1, s):
            acc = acc + in_ref[j]
        o_ref[...] = acc

    return pl.pallas_call(
        body, name=name, grid=(r // tr,),
        in_specs=[pl.BlockSpec((s, tr, c), lambda i: (0, i, 0))], out_specs=pl.BlockSpec((tr, c), lambda i: (i, 0)),
        out_shape=jax.ShapeDtypeStruct((r, c), F32), compiler_params=_params("parallel"))(slabs)


def _adamw_math(w, g, m, v):
    m = ADAM_B1 * m + (1.0 - ADAM_B1) * g
    v = ADAM_B2 * v + (1.0 - ADAM_B2) * (g * g)
    m_hat = m / (1.0 - ADAM_B1 ** ADAM_STEP)
    v_hat = v / (1.0 - ADAM_B2 ** ADAM_STEP)
    delta = -ADAM_LR * (m_hat / (jnp.sqrt(v_hat) + ADAM_EPS) + ADAM_WD * w)
    return delta, m, v


def _adamw(w, m, v, parts, *, name, tr=128):
    r, c = w.shape
    tr = min(tr, r)
    s = len(parts)

    def body(w_ref, m_ref, v_ref, *refs):
        g_ref, d_ref, nm_ref, nv_ref = refs[s:]
        g = refs[0][...]
        for p_ref in refs[1:s]:
            g = g + p_ref[...]
        g_ref[...] = g
        d_ref[...], nm_ref[...], nv_ref[...] = _adamw_math(w_ref[...], g, m_ref[...], v_ref[...])

    blk = pl.BlockSpec((tr, c), lambda i: (i, 0))
    shape = jax.ShapeDtypeStruct((r, c), F32)
    return pl.pallas_call(
        body, name=name, grid=(r // tr,), in_specs=[blk] * (3 + s), out_specs=[blk] * 4, out_shape=[shape] * 4,
        compiler_params=_params("parallel"))(w, m, v, *parts)


_BIG = ("a_w_in", "b_w_in", "a_w_out", "b_w_out", "a_conv_w")
_BIG_BY_COLS = {"a_w_in": True, "b_w_in": True, "a_w_out": False, "b_w_out": False, "a_conv_w": True}
_SMALL = ("norm_w", "a_a_log", "a_dt_bias", "a_out_norm_w", "b_q_norm_w", "b_k_norm_w", "b_rel_bias")
_ORDER = ("norm_w", "a_w_in", "a_conv_w", "a_a_log", "a_dt_bias", "a_out_norm_w", "a_w_out", "b_w_in", "b_q_norm_w",
          "b_k_norm_w", "b_rel_bias", "b_w_out")


def _join_cols(g):
    return jnp.transpose(g, (1, 0, 2)).reshape(g.shape[1], -1)


def _split_cols(g):
    return jnp.transpose(g.reshape(g.shape[0], N_CHIPS, -1), (1, 0, 2))


def _pack(d):
    flat = jnp.concatenate([d[n].reshape(-1) for n in _SMALL])
    return jnp.pad(flat, (0, -flat.shape[0] % LANES)).reshape(1, -1)


def _unpack(row, like):
    out, at = {}, 0
    for n in _SMALL:
        size = like[n].size
        out[n] = row[0, at:at + size].reshape(like[n].shape)
        at += size
    return out


def kernel(x, norm_w, a_w_in, a_conv_w, a_a_log, a_dt_bias, a_out_norm_w, a_w_out, b_w_in, b_q_norm_w, b_k_norm_w, b_rel_bias, b_w_out, loss_target, m_norm_w, m_a_w_in, m_a_conv_w, m_a_a_log, m_a_dt_bias, m_a_out_norm_w, m_a_w_out, m_b_w_in, m_b_q_norm_w, m_b_k_norm_w, m_b_rel_bias, m_b_w_out, v_norm_w, v_a_w_in, v_a_conv_w, v_a_a_log, v_a_dt_bias, v_a_out_norm_w, v_a_w_out, v_b_w_in, v_b_q_norm_w, v_b_k_norm_w, v_b_rel_bias, v_b_w_out):
    w = dict(norm_w=norm_w, a_w_in=a_w_in, a_conv_w=a_conv_w, a_a_log=a_a_log, a_dt_bias=a_dt_bias,
             a_out_norm_w=a_out_norm_w, a_w_out=a_w_out, b_w_in=b_w_in, b_q_norm_w=b_q_norm_w, b_k_norm_w=b_k_norm_w,
             b_rel_bias=b_rel_bias, b_w_out=b_w_out)
    m = dict(norm_w=m_norm_w, a_w_in=m_a_w_in, a_conv_w=m_a_conv_w, a_a_log=m_a_a_log, a_dt_bias=m_a_dt_bias,
             a_out_norm_w=m_a_out_norm_w, a_w_out=m_a_w_out, b_w_in=m_b_w_in, b_q_norm_w=m_b_q_norm_w,
             b_k_norm_w=m_b_k_norm_w, b_rel_bias=m_b_rel_bias, b_w_out=m_b_w_out)
    v = dict(norm_w=v_norm_w, a_w_in=v_a_w_in, a_conv_w=v_a_conv_w, a_a_log=v_a_a_log, a_dt_bias=v_a_dt_bias,
             a_out_norm_w=v_a_out_norm_w, a_w_out=v_a_w_out, b_w_in=v_b_w_in, b_q_norm_w=v_b_q_norm_w,
             b_k_norm_w=v_b_k_norm_w, b_rel_bias=v_b_rel_bias, b_w_out=v_b_w_out)

    shards = [w[n][0].astype(F32 if n == "a_conv_w" else BF16) for n in _BIG]
    full = {n: (_join_cols(g) if _BIG_BY_COLS[n] else g.reshape(-1, g.shape[-1]))
            for n, g in zip(_BIG, _gather_chips(shards, name="gather_weights"))}

    loss, dx, grads = _local_step(
        x[0], loss_target[0], norm_w, full["a_w_in"], full["a_conv_w"], a_a_log, a_dt_bias, a_out_norm_w, full["a_w_out"],
        full["b_w_in"], b_q_norm_w, b_k_norm_w, b_rel_bias, full["b_w_out"])
    loss = lax.psum(loss, ("x", "y", "c"))

    slabs = [_split_cols(grads[n]) if _BIG_BY_COLS[n] else grads[n].reshape(N_CHIPS, -1, grads[n].shape[-1]) for n in _BIG]
    landed = _exchange_chips(slabs, name="scatter_grads")
    mine = [_sum_slots(s, name=f"chip_sum_{n}") for n, s in zip(_BIG, landed)]
    theirs = _swap_pair(mine, name="pair_grads")
    out = {}
    for n, p, q in zip(_BIG, mine, theirs):
        out[n] = [r[None] for r in _adamw(w[n][0], m[n][0], v[n][0], [p, q], name=f"adamw_{n}")]

    row = _pack(grads)
    tiles = _gather_all(jnp.broadcast_to(row, (8, row.shape[1])), name="gather_small_grads")
    res = _adamw(_pack(w), _pack(m), _pack(v), [tiles[d, 0:1, :] for d in range(N_DEV)], name="adamw_small")
    unpacked = [_unpack(r, w) for r in res]
    for n in _SMALL:
        out[n] = [u[n] for u in unpacked]

    return (loss, dx[None], *[out[n][0] for n in _ORDER], *[out[n][1] for n in _ORDER], *[out[n][2] for n in _ORDER],
            *[out[n][3] for n in _ORDER])
```

```python
import functools

import jax
import jax.numpy as jnp
from jax import lax
from jax.experimental import pallas as pl
from jax.experimental.pallas import tpu as pltpu

F32 = jnp.float32
BF16 = jnp.bfloat16

CHUNK = 64
HEAD_DIM = 128
LEFT_CHUNKS = 8
REL_CLIP = 256
CONV_K = 4
EPS = 1e-6
HALO = 8

ADAM_LR = 0.001
ADAM_B1 = 0.9
ADAM_B2 = 0.999
ADAM_EPS = 1e-08
ADAM_WD = 0.01
ADAM_STEP = 10

LANES = 128
N_CHIPS = 4
N_DEV = 8
VMEM_LIMIT_BYTES = 56 * 1024 * 1024
MESH = pl.DeviceIdType.MESH
HIGHEST = lax.Precision.HIGHEST


def _params(*sem):
    return pltpu.CompilerParams(dimension_semantics=sem, vmem_limit_bytes=VMEM_LIMIT_BYTES)


def _dot(a, b, dims=(((1,), (0,)), ((), ())), precision=None):
    return lax.dot_general(a, b, dims, precision=precision, preferred_element_type=F32)


_NT = (((1,), (1,)), ((), ()))
_TN = (((0,), (0,)), ((), ()))


def _bdot(a, b, dims=(((1,), (0,)), ((), ()))):
    return _dot(a.astype(BF16), b.astype(BF16), dims)


def _fdot(a, b, dims=(((1,), (0,)), ((), ()))):
    return _dot(a, b, dims, precision=lax.Precision.HIGH)


def _silu(x):
    return x * jax.nn.sigmoid(x)


def _matmul(a, b, *, name, trans_b=False, residual=None, out_dtype=F32, tm=1024, tn=1024, tk=2048):
    m, k = a.shape
    n = b.shape[0] if trans_b else b.shape[1]
    tm, tn, tk = min(tm, m), min(tn, n), min(tk, k)
    assert m % tm == 0 and n % tn == 0 and k % tk == 0, (a.shape, b.shape, tm, tn, tk)
    nk = k // tk
    dims = _NT if trans_b else (((1,), (0,)), ((), ()))

    def body(*refs):
        if residual is None:
            a_ref, b_ref, o_ref, acc_ref = refs
            r_ref = None
        else:
            a_ref, b_ref, r_ref, o_ref, acc_ref = refs
        kk = pl.program_id(2)

        @pl.when(kk == 0)
        def _():
            acc_ref[...] = jnp.zeros_like(acc_ref)

        acc_ref[...] += _dot(a_ref[...], b_ref[...], dims)

        @pl.when(kk == nk - 1)
        def _():
            r = acc_ref[...]
            if r_ref is not None:
                r = r + r_ref[...]
            o_ref[...] = r.astype(o_ref.dtype)

    in_specs = [
        pl.BlockSpec((tm, tk), lambda i, j, kk: (i, kk)),
        pl.BlockSpec((tn, tk), lambda i, j, kk: (j, kk)) if trans_b else pl.BlockSpec((tk, tn), lambda i, j, kk: (kk, j)),
    ]
    args = [a, b]
    if residual is not None:
        in_specs.append(pl.BlockSpec((tm, tn), lambda i, j, kk: (i, j)))
        args.append(residual)
    return pl.pallas_call(
        body,
        name=name,
        grid=(m // tm, n // tn, nk),
        in_specs=in_specs,
        out_specs=pl.BlockSpec((tm, tn), lambda i, j, kk: (i, j)),
        out_shape=jax.ShapeDtypeStruct((m, n), out_dtype),
        scratch_shapes=[pltpu.VMEM((tm, tn), F32)],
        compiler_params=_params("parallel", "parallel", "arbitrary"),
    )(*args)


def _rms(x, w):
    return x * lax.rsqrt(jnp.mean(x * x, axis=-1, keepdims=True) + EPS) * w


def _rmsnorm_fwd(x, w_row, *, name, tr=512):
    t, d = x.shape
    tr = min(tr, t)

    def body(x_ref, w_ref, o_ref):
        o_ref[...] = _rms(x_ref[...], w_ref[...]).astype(BF16)

    return pl.pallas_call(
        body,
        name=name,
        grid=(t // tr,),
        in_specs=[pl.BlockSpec((tr, d), lambda i: (i, 0)), pl.BlockSpec((1, d), lambda i: (0, 0))],
        out_specs=pl.BlockSpec((tr, d), lambda i: (i, 0)),
        out_shape=jax.ShapeDtypeStruct((t, d), BF16),
        compiler_params=_params("parallel"),
    )(x, w_row)


def _rmsnorm_bwd(x, w_row, dy, dres, *, name, tr=256):
    t, d = x.shape
    tr = min(tr, t)

    def body(x_ref, w_ref, dy_ref, dres_ref, dx_ref, dxb_ref, dw_ref):
        @pl.when(pl.program_id(0) == 0)
        def _():
            dw_ref[...] = jnp.zeros_like(dw_ref)

        _, vjp = jax.vjp(_rms, x_ref[...], w_ref[...])
        dx, dw = vjp(dy_ref[...])
        dx = dx + dres_ref[...]
        dx_ref[...] = dx
        dxb_ref[...] = dx.astype(BF16)
        dw_ref[...] += dw

    row = pl.BlockSpec((tr, d), lambda i: (i, 0))
    vec = pl.BlockSpec((1, d), lambda i: (0, 0))
    return pl.pallas_call(
        body,
        name=name,
        grid=(t // tr,),
        in_specs=[row, vec, row, row],
        out_specs=[row, row, vec],
        out_shape=[jax.ShapeDtypeStruct((t, d), F32), jax.ShapeDtypeStruct((t, d), BF16), jax.ShapeDtypeStruct((1, d), F32)],
        compiler_params=_params("arbitrary"),
    )(x, w_row, dy, dres)


def _loss_head(h, target, *, name, tr=512):
    t, d = h.shape
    tr = min(tr, t)

    def body(h_ref, t_ref, dh_ref, dhb_ref, part_ref):
        @pl.when(pl.program_id(0) == 0)
        def _():
            part_ref[...] = jnp.zeros_like(part_ref)

        err = h_ref[...] - t_ref[...]
        dh = err * (1.0 / d)
        dh_ref[...] = dh
        dhb_ref[...] = dh.astype(BF16)
        part_ref[...] += jnp.sum(err * err, axis=0, keepdims=True)

    row = pl.BlockSpec((tr, d), lambda i: (i, 0))
    vec = pl.BlockSpec((1, d), lambda i: (0, 0))
    dh, dhb, part = pl.pallas_call(
        body,
        name=name,
        grid=(t // tr,),
        in_specs=[row, row],
        out_specs=[row, row, vec],
        out_shape=[jax.ShapeDtypeStruct((t, d), F32), jax.ShapeDtypeStruct((t, d), BF16), jax.ShapeDtypeStruct((1, d), F32)],
        compiler_params=_params("arbitrary"),
    )(h, target)
    return 0.5 / d * jnp.sum(part), dh, dhb


_BNN = (((2,), (1,)), ((0,), (0,)))
_BNT = (((2,), (2,)), ((0,), (0,)))
_BTN = (((1,), (1,)), ((0,), (0,)))


def _conv_silu(xe, w):
    rows = xe.shape[0] - HALO
    first = HALO - (CONV_K - 1)
    c = w[0:1, :] * xe[first:first + rows, :]
    for j in range(1, CONV_K):
        c = c + w[j:j + 1, :] * xe[first + j:first + j + rows, :]
    return _silu(c)


def _gdn_intra(qx, kx, vx, a, b, wq, wk, wv, alog, dtb):
    n = a.shape[0] // CHUNK
    qt, kt, v = _conv_silu(qx, wq), _conv_silu(kx, wk), _conv_silu(vx, wv)
    q = qt * lax.rsqrt(jnp.sum(qt * qt, axis=-1, keepdims=True) + EPS) * (HEAD_DIM ** -0.5)
    k = kt * lax.rsqrt(jnp.sum(kt * kt, axis=-1, keepdims=True) + EPS)
    lanes = jnp.ones((1, HEAD_DIM), F32)
    beta = jax.nn.sigmoid(b) * lanes
    sp = a + dtb
    g = (-jnp.exp(alog) * (jnp.maximum(sp, 0.0) + jnp.log(1.0 + jnp.exp(-jnp.abs(sp))))) * lanes
    q, k, v, beta, g = (t.reshape(n, CHUNK, HEAD_DIM) for t in (q, k, v, beta, g))

    row = lax.broadcasted_iota(jnp.int32, (n, CHUNK, CHUNK), 1)
    col = lax.broadcasted_iota(jnp.int32, (n, CHUNK, CHUNK), 2)
    tri_incl = row >= col
    tri_strict = row > col
    gc = _fdot(tri_incl.astype(F32), g, _BNN)
    gc_row = _fdot(g[:, :, :CHUNK], (row <= col).astype(F32), _BTN)
    decay = jnp.exp(jnp.where(tri_incl, gc[:, :, :CHUNK] - gc_row, -1e30))
    kb = k * beta
    vb = v * beta
    neg_l = jnp.where(tri_strict, -(_bdot(kb, k, _BNT) * decay), 0.0)
    inv = (row == col).astype(F32) + neg_l
    power = neg_l
    for _ in range(5):
        power = _fdot(power, power, _BNN)
        inv = inv + _fdot(inv, power, _BNN)
    e = jnp.exp(gc)
    u = _fdot(inv, vb, _BNN)
    w = _fdot(inv, kb * e, _BNN)
    qk = jnp.where(tri_incl, _bdot(q, k, _BNT) * decay, 0.0)
    g_last = gc[:, CHUNK - 1:CHUNK, :]
    return u, w, qk, q * e, k * jnp.exp(g_last - gc), jnp.exp(g_last)


def _gdn_inter(state, u, w, qk, q_dec, k_dec, decay_last, z, onw):
    state_b = state.astype(BF16)
    v_new = u - _bdot(w, state_b)
    o = _bdot(q_dec, state_b) + _bdot(qk, v_new)
    new_state = state * decay_last + _bdot(k_dec, v_new, _TN)
    return _rms(o, onw) * _silu(z), new_state


def _intra_scratch(n, dtype):
    return [pltpu.VMEM((n, CHUNK, HEAD_DIM), F32), pltpu.VMEM((n, CHUNK, HEAD_DIM), dtype), pltpu.VMEM((n, CHUNK, CHUNK), dtype),
            pltpu.VMEM((n, CHUNK, HEAD_DIM), dtype), pltpu.VMEM((n, CHUNK, HEAD_DIM), dtype), pltpu.VMEM((n, 1, HEAD_DIM), F32)]


def _head_lane(h, offset=0):
    return lax.broadcasted_iota(jnp.int32, (1, LANES), 1) == h + offset


def _pick(mask, x):
    return jnp.sum(jnp.where(mask, x, 0.0), axis=1, keepdims=True)


def _gdn_specs(heads, tb, rev, nb):
    blk = (lambda i: nb - 1 - i) if rev else (lambda i: i)
    hb = tb // HALO

    def col(group):
        return pl.BlockSpec((tb, HEAD_DIM), lambda i, h: (blk(i), group * heads + h))

    def halo(group):
        return pl.BlockSpec((HALO, HEAD_DIM), lambda i, h: (jnp.maximum(blk(i) * hb - 1, 0), group * heads + h))

    def convw(group):
        return pl.BlockSpec((CONV_K, HEAD_DIM), lambda i, h: (0, group * heads + h))

    vec = pl.BlockSpec((1, LANES), lambda i, h: (0, 0))
    ab = pl.BlockSpec((tb, LANES), lambda i, h: (blk(i), 0))
    states = pl.BlockSpec((1, tb // CHUNK, HEAD_DIM, HEAD_DIM), lambda i, h: (h, blk(i), 0, 0))
    return blk, col, halo, convw, vec, ab, states


def _gdn_fwd(proj, ab, conv_w, alog_row, dtb_row, onw_row, *, heads, name, tb=512):
    t = proj.shape[0]
    tb = min(tb, t)
    nb, cpb = t // tb, tb // CHUNK
    _, col, halo, convw, vec, abspec, states = _gdn_specs(heads, tb, False, nb)

    def body(q_ref, k_ref, v_ref, qh_ref, kh_ref, vh_ref, z_ref, ab_ref, wq_ref, wk_ref, wv_ref, alog_ref, dtb_ref, onw_ref,
             og_ref, st_ref, state_scr, x_scr, *op_scr):
        i, h = pl.program_id(0), pl.program_id(1)
        for n, (ref, href) in enumerate(((q_ref, qh_ref), (k_ref, kh_ref), (v_ref, vh_ref))):
            x_scr[n, 0:HALO, :] = jnp.where(i > 0, href[...], 0.0)
            x_scr[n, HALO:HALO + tb, :] = ref[...]

        @pl.when(i == 0)
        def _():
            state_scr[h] = jnp.zeros((HEAD_DIM, HEAD_DIM), F32)

        sel_a, sel_b = _head_lane(h), _head_lane(h, heads)
        alog, dtb = _pick(sel_a, alog_ref[...]), _pick(sel_a, dtb_ref[...])
        abv = ab_ref[...]
        ops = _gdn_intra(x_scr[0], x_scr[1], x_scr[2], _pick(sel_a, abv), _pick(sel_b, abv), wq_ref[...], wk_ref[...],
                         wv_ref[...], alog, dtb)
        for scr, val in zip(op_scr, ops):
            scr[...] = val.astype(scr.dtype)
        onw = onw_ref[...]

        def chunk(c, state):
            rows = pl.ds(pl.multiple_of(c * CHUNK, CHUNK), CHUNK)
            st_ref[0, c] = state
            og, new_state = _gdn_inter(state, *[scr[c] for scr in op_scr], z_ref[rows, :], onw)
            og_ref[rows, :] = og.astype(BF16)
            return new_state

        state_scr[h] = lax.fori_loop(0, cpb, chunk, state_scr[h])

    return pl.pallas_call(
        body,
        name=name,
        grid=(nb, heads),
        in_specs=[col(0), col(1), col(2), halo(0), halo(1), halo(2), col(3), abspec, convw(0), convw(1), convw(2), vec, vec, vec],
        out_specs=[pl.BlockSpec((tb, HEAD_DIM), lambda i, h: (i, h)), states],
        out_shape=[jax.ShapeDtypeStruct((t, heads * HEAD_DIM), BF16),
                   jax.ShapeDtypeStruct((heads, t // CHUNK, HEAD_DIM, HEAD_DIM), F32)],
        scratch_shapes=[pltpu.VMEM((heads, HEAD_DIM, HEAD_DIM), F32), pltpu.VMEM((3, HALO + tb, HEAD_DIM), F32)]
        + _intra_scratch(cpb, BF16),
        compiler_params=_params("arbitrary", "arbitrary"),
    )(proj, proj, proj, proj, proj, proj, proj, ab, conv_w, conv_w, conv_w, alog_row, dtb_row, onw_row)


def _gdn_bwd(proj, ab, conv_w, alog_row, dtb_row, onw_row, states, dog, *, heads, name, tb=512):
    t = proj.shape[0]
    tb = min(tb, t)
    nb, cpb = t // tb, tb // CHUNK
    _, col, halo, convw, vec, abspec, states_spec = _gdn_specs(heads, tb, True, nb)
    n_conv = conv_w.shape[1]

    def body(q_ref, k_ref, v_ref, qh_ref, kh_ref, vh_ref, z_ref, ab_ref, wq_ref, wk_ref, wv_ref, alog_ref, dtb_ref, onw_ref,
             st_ref, dog_ref, dq_ref, dk_ref, dv_ref, dz_ref, dab_ref, dconv_ref, dalog_ref, ddtb_ref, donw_ref,
             dstate_scr, x_scr, carry_scr, *scr):
        op_scr, dop_scr = scr[:6], scr[6:]
        i, h = pl.program_id(0), pl.program_id(1)
        first_block = i == nb - 1
        for n, (ref, href) in enumerate(((q_ref, qh_ref), (k_ref, kh_ref), (v_ref, vh_ref))):
            x_scr[n, 0:HALO, :] = jnp.where(first_block, 0.0, href[...])
            x_scr[n, HALO:HALO + tb, :] = ref[...]

        @pl.when(jnp.logical_and(i == 0, h == 0))
        def _():
            dconv_ref[...] = jnp.zeros_like(dconv_ref)
            dalog_ref[...] = jnp.zeros_like(dalog_ref)
            ddtb_ref[...] = jnp.zeros_like(ddtb_ref)
            donw_ref[...] = jnp.zeros_like(donw_ref)

        @pl.when(h == 0)
        def _():
            dab_ref[...] = jnp.zeros_like(dab_ref)

        @pl.when(i == 0)
        def _():
            dstate_scr[h] = jnp.zeros((HEAD_DIM, HEAD_DIM), F32)
            carry_scr[h] = jnp.zeros((3, HALO, HEAD_DIM), F32)

        sel_a, sel_b = _head_lane(h), _head_lane(h, heads)
        alog, dtb = _pick(sel_a, alog_ref[...]), _pick(sel_a, dtb_ref[...])
        abv = ab_ref[...]
        ops, vjp_intra = jax.vjp(_gdn_intra, x_scr[0], x_scr[1], x_scr[2], _pick(sel_a, abv), _pick(sel_b, abv), wq_ref[...],
                                 wk_ref[...], wv_ref[...], alog, dtb)
        for s, val in zip(op_scr, ops):
            s[...] = val.astype(s.dtype)
        onw = onw_ref[...]

        def chunk(step, carry):
            dstate, donw = carry
            c = cpb - 1 - step
            rows = pl.ds(pl.multiple_of(c * CHUNK, CHUNK), CHUNK)
            _, vjp = jax.vjp(_gdn_inter, st_ref[0, c], *[s[c].astype(F32) for s in op_scr], z_ref[rows, :], onw)
            grads = vjp((dog_ref[rows, :], dstate))
            for s, val in zip(dop_scr, grads[1:7]):
                s[c] = val
            dz_ref[rows, :] = grads[7].astype(BF16)
            return grads[0], donw + grads[8]

        dstate, donw = lax.fori_loop(0, cpb, chunk, (dstate_scr[h], jnp.zeros((1, HEAD_DIM), F32)))
        dstate_scr[h] = dstate
        dqx, dkx, dvx, da, db, dwq, dwk, dwv, dalog, ddtb = vjp_intra(tuple(s[...] for s in dop_scr))
        dab_ref[...] += jnp.where(sel_a, da, 0.0) + jnp.where(sel_b, db, 0.0)
        for n, (dref, dx, dw) in enumerate(((dq_ref, dqx, dwq), (dk_ref, dkx, dwk), (dv_ref, dvx, dwv))):
            x_scr[n] = dx
            x_scr[n, tb:tb + HALO, :] += carry_scr[h, n]
            carry_scr[h, n] = x_scr[n, 0:HALO, :]
            dref[...] = x_scr[n, HALO:HALO + tb, :].astype(BF16)
            lanes = pl.ds(pl.multiple_of((n * heads + h) * HEAD_DIM, HEAD_DIM), HEAD_DIM)
            dconv_ref[:, lanes] += dw
        dalog_ref[...] += jnp.where(sel_a, dalog, 0.0)
        ddtb_ref[...] += jnp.where(sel_a, ddtb, 0.0)
        donw_ref[...] += donw

    out_col = pl.BlockSpec((tb, HEAD_DIM), lambda i, h: (nb - 1 - i, h))
    dog_spec = pl.BlockSpec((tb, HEAD_DIM), lambda i, h: (nb - 1 - i, h))
    col_shape = jax.ShapeDtypeStruct((t, heads * HEAD_DIM), BF16)
    row_shape = jax.ShapeDtypeStruct((1, LANES), F32)
    return pl.pallas_call(
        body,
        name=name,
        grid=(nb, heads),
        in_specs=[col(0), col(1), col(2), halo(0), halo(1), halo(2), col(3), abspec, convw(0), convw(1), convw(2), vec, vec, vec,
                  states_spec, dog_spec],
        out_specs=[out_col, out_col, out_col, out_col, abspec,
                   pl.BlockSpec((CONV_K, n_conv), lambda i, h: (0, 0)), vec, vec, vec],
        out_shape=[col_shape, col_shape, col_shape, col_shape, jax.ShapeDtypeStruct((t, LANES), F32),
                   jax.ShapeDtypeStruct((CONV_K, n_conv), F32), row_shape, row_shape, row_shape],
        scratch_shapes=[pltpu.VMEM((heads, HEAD_DIM, HEAD_DIM), F32), pltpu.VMEM((3, HALO + tb, HEAD_DIM), F32),
                        pltpu.VMEM((heads, 3, HALO, HEAD_DIM), F32)] + _intra_scratch(cpb, BF16) + _intra_scratch(cpb, F32),
        compiler_params=_params("arbitrary", "arbitrary"),
    )(proj, proj, proj, proj, proj, proj, proj, ab, conv_w, conv_w, conv_w, alog_row, dtb_row, onw_row, states, dog)


BAND = (LEFT_CHUNKS + 1) * CHUNK
PAD = LEFT_CHUNKS * CHUNK
GROUP = 2
ROWS = GROUP * CHUNK
WIN = (LEFT_CHUNKS + GROUP) * CHUNK
DIAGS = WIN + ROWS - 1
NEAR = PAD + ROWS - 1 - REL_CLIP
assert 0 < NEAR < DIAGS and WIN - PAD - 1 <= REL_CLIP and WIN % LANES == 0


def _band_bias(rel_bias):
    heads = rel_bias.shape[0]
    far = jnp.broadcast_to(rel_bias[:, 2 * REL_CLIP:], (heads, NEAR + 1))
    near = rel_bias[:, 2 * REL_CLIP + NEAR + 1 - DIAGS:2 * REL_CLIP][:, ::-1]
    diag = jnp.concatenate([far, near], axis=1)
    return jnp.stack([diag[:, ROWS - 1 - r:ROWS - 1 - r + WIN] for r in range(ROWS)], axis=1)


def _band_bias_grad(dbias):
    heads = dbias.shape[0]
    diag = sum(jnp.pad(dbias[:, r, :], ((0, 0), (ROWS - 1 - r, r))) for r in range(ROWS))
    far = jnp.sum(diag[:, :NEAR + 1], axis=1, keepdims=True)
    near = diag[:, NEAR + 1:][:, ::-1]
    unused = jnp.zeros((heads, 2 * REL_CLIP - near.shape[1]), F32)
    return jnp.concatenate([unused, near, far], axis=1)


def _attn_groups(q_pre, z, kn, v, bias, qnw, start):
    n = q_pre.shape[0]
    q = _rms(q_pre, qnw)
    s = _bdot(q, kn, _BNT) * (HEAD_DIM ** -0.5) + bias
    group = lax.broadcasted_iota(jnp.int32, (n, ROWS, WIN), 0)
    row = lax.broadcasted_iota(jnp.int32, (n, ROWS, WIN), 1)
    key = lax.broadcasted_iota(jnp.int32, (n, ROWS, WIN), 2)
    band_start = jnp.bitwise_and(row, -CHUNK)
    in_sequence = key >= PAD - (start + group * ROWS)
    valid = jnp.logical_and(jnp.logical_and(key >= band_start, key < band_start + BAND), in_sequence)
    s = jnp.where(valid, s, -1e30)
    p = jnp.exp(s - jnp.max(s, axis=-1, keepdims=True))
    p = p / jnp.sum(p, axis=-1, keepdims=True)
    return _bdot(p, v, _BNN) * _silu(z)


def _attn_specs(heads, tb, t):
    def col(group):
        return pl.BlockSpec((tb, HEAD_DIM), lambda h, i: (i, group * heads + h))

    def full(group):
        return pl.BlockSpec((t, HEAD_DIM), lambda h, i: (0, group * heads + h))

    bias = pl.BlockSpec((1, ROWS, WIN), lambda h, i: (h, 0, 0))
    vec = pl.BlockSpec((1, HEAD_DIM), lambda h, i: (0, 0))
    return col, full, bias, vec


def _attn_windows(scr, block_start, n):
    return jnp.stack([scr[pl.ds(pl.multiple_of(block_start + g * ROWS, ROWS), WIN), :] for g in range(n)])


def _attn_fill(k_ref, v_ref, knw_ref, kn_scr, v_scr, t):
    kn_scr[0:PAD, :] = jnp.zeros((PAD, HEAD_DIM), BF16)
    v_scr[0:PAD, :] = jnp.zeros((PAD, HEAD_DIM), BF16)
    step = min(512, t)

    def fill(j, _):
        rows = pl.ds(pl.multiple_of(j * step, step), step)
        prows = pl.ds(pl.multiple_of(PAD + j * step, CHUNK), step)
        kn_scr[prows, :] = _rms(k_ref[rows, :], knw_ref[...]).astype(BF16)
        v_scr[prows, :] = v_ref[rows, :].astype(BF16)
        return 0

    lax.fori_loop(0, t // step, fill, 0)


def _attn_fwd(proj, bias, qnw_row, knw_row, *, heads, name, tb=512):
    t = proj.shape[0]
    tb = min(tb, t)
    nb, ng = t // tb, tb // ROWS
    col, full, bias_spec, vec = _attn_specs(heads, tb, t)

    def body(q_ref, k_ref, v_ref, z_ref, bias_ref, qnw_ref, knw_ref, og_ref, kn_scr, v_scr):
        i = pl.program_id(1)

        @pl.when(i == 0)
        def _():
            _attn_fill(k_ref, v_ref, knw_ref, kn_scr, v_scr, t)

        start = i * tb
        og = _attn_groups(q_ref[...].reshape(ng, ROWS, HEAD_DIM), z_ref[...].reshape(ng, ROWS, HEAD_DIM),
                          _attn_windows(kn_scr, start, ng), _attn_windows(v_scr, start, ng), bias_ref[0], qnw_ref[...], start)
        og_ref[...] = og.reshape(tb, HEAD_DIM).astype(BF16)

    return pl.pallas_call(
        body,
        name=name,
        grid=(heads, nb),
        in_specs=[col(0), full(1), full(2), col(3), bias_spec, vec, vec],
        out_specs=pl.BlockSpec((tb, HEAD_DIM), lambda h, i: (i, h)),
        out_shape=jax.ShapeDtypeStruct((t, heads * HEAD_DIM), BF16),
        scratch_shapes=[pltpu.VMEM((PAD + t, HEAD_DIM), BF16), pltpu.VMEM((PAD + t, HEAD_DIM), BF16)],
        compiler_params=_params("arbitrary", "arbitrary"),
    )(proj, proj, proj, proj, bias, qnw_row, knw_row)


def _attn_bwd(proj, bias, qnw_row, knw_row, dog, *, heads, name, tb=512):
    t = proj.shape[0]
    tb = min(tb, t)
    nb, ng = t // tb, tb // ROWS
    col, full, bias_spec, vec = _attn_specs(heads, tb, t)

    def body(q_ref, k_ref, v_ref, z_ref, bias_ref, qnw_ref, knw_ref, dog_ref,
             dq_ref, dk_ref, dv_ref, dz_ref, dbias_ref, dqnw_ref, dknw_ref, kn_scr, v_scr, dkn_scr, dv_scr):
        i = pl.program_id(1)

        @pl.when(i == 0)
        def _():
            _attn_fill(k_ref, v_ref, knw_ref, kn_scr, v_scr, t)
            dkn_scr[...] = jnp.zeros_like(dkn_scr)
            dv_scr[...] = jnp.zeros_like(dv_scr)
            dbias_ref[...] = jnp.zeros_like(dbias_ref)
            dqnw_ref[...] = jnp.zeros_like(dqnw_ref)

        start = i * tb
        _, vjp = jax.vjp(
            lambda q_pre, z, kn, v, bias, qnw: _attn_groups(q_pre, z, kn, v, bias, qnw, start),
            q_ref[...].reshape(ng, ROWS, HEAD_DIM), z_ref[...].reshape(ng, ROWS, HEAD_DIM),
            _attn_windows(kn_scr, start, ng).astype(F32), _attn_windows(v_scr, start, ng).astype(F32), bias_ref[0], qnw_ref[...])
        dq, dz, dkn, dv, dbias, dqnw = vjp(dog_ref[...].reshape(ng, ROWS, HEAD_DIM))
        dq_ref[...] = dq.reshape(tb, HEAD_DIM).astype(BF16)
        dz_ref[...] = dz.reshape(tb, HEAD_DIM).astype(BF16)
        for g in range(ng):
            window = pl.ds(pl.multiple_of(start + g * ROWS, ROWS), WIN)
            dkn_scr[window, :] += dkn[g]
            dv_scr[window, :] += dv[g]
        dbias_ref[0] += dbias
        dqnw_ref[0] += dqnw

        @pl.when(i == nb - 1)
        def _():
            step = min(512, t)

            def finish(j, dknw):
                rows = pl.ds(pl.multiple_of(j * step, step), step)
                prows = pl.ds(pl.multiple_of(PAD + j * step, CHUNK), step)
                _, vjp = jax.vjp(_rms, k_ref[rows, :], knw_ref[...])
                dk, dw = vjp(dkn_scr[prows, :])
                dk_ref[rows, :] = dk.astype(BF16)
                dv_ref[rows, :] = dv_scr[prows, :].astype(BF16)
                return dknw + dw

            dknw_ref[0] = lax.fori_loop(0, t // step, finish, jnp.zeros((1, HEAD_DIM), F32))

    out_col = pl.BlockSpec((tb, HEAD_DIM), lambda h, i: (i, h))
    out_full = pl.BlockSpec((t, HEAD_DIM), lambda h, i: (0, h))
    head_vec = pl.BlockSpec((1, 1, HEAD_DIM), lambda h, i: (h, 0, 0))
    col_shape = jax.ShapeDtypeStruct((t, heads * HEAD_DIM), BF16)
    vec_shape = jax.ShapeDtypeStruct((heads, 1, HEAD_DIM), F32)
    return pl.pallas_call(
        body,
        name=name,
        grid=(heads, nb),
        in_specs=[col(0), full(1), full(2), col(3), bias_spec, vec, vec, pl.BlockSpec((tb, HEAD_DIM), lambda h, i: (i, h))],
        out_specs=[out_col, out_full, out_full, out_col, bias_spec, head_vec, head_vec],
        out_shape=[col_shape, col_shape, col_shape, col_shape, jax.ShapeDtypeStruct((heads, ROWS, WIN), F32),
                   vec_shape, vec_shape],
        scratch_shapes=[pltpu.VMEM((PAD + t, HEAD_DIM), BF16), pltpu.VMEM((PAD + t, HEAD_DIM), BF16),
                        pltpu.VMEM((PAD + t, HEAD_DIM), F32), pltpu.VMEM((PAD + t, HEAD_DIM), F32)],
        compiler_params=_params("arbitrary", "arbitrary"),
    )(proj, proj, proj, proj, bias, qnw_row, knw_row, dog)


def _lane_row(v):
    v = v.reshape(1, -1)
    return jnp.pad(v, ((0, 0), (0, LANES - v.shape[1])))


def _local_step(x, target, norm_w, wa_in, conv_w, a_log, dt_bias, onw, wa_out, wb_in, qnw, knw, rel_bias, wb_out):
    ha, hb = a_log.shape[-1], rel_bias.shape[-2]
    na = 4 * ha * HEAD_DIM
    wa_main = wa_in[:, :na]
    wa_ab = jnp.pad(wa_in[:, na:], ((0, 0), (0, LANES - 2 * ha)))
    alog_row, dtb_row, onw_row = _lane_row(a_log), _lane_row(dt_bias), _lane_row(onw)
    qnw_row, knw_row = _lane_row(qnw), _lane_row(knw)
    bias = _band_bias(rel_bias.reshape(hb, -1))

    hn0 = _rmsnorm_fwd(x, norm_w[0:1], name="norm0")
    proj_a = _matmul(hn0, wa_main, name="a_in")
    ab_a = _matmul(hn0, wa_ab, name="a_in_ab")
    og_a, states = _gdn_fwd(proj_a, ab_a, conv_w, alog_row, dtb_row, onw_row, heads=ha, name="gdn_fwd")
    h1 = _matmul(og_a, wa_out, residual=x, name="a_out")
    hn1 = _rmsnorm_fwd(h1, norm_w[1:2], name="norm1")
    proj_b = _matmul(hn1, wb_in, name="b_in")
    og_b = _attn_fwd(proj_b, bias, qnw_row, knw_row, heads=hb, name="attn_fwd")
    h2 = _matmul(og_b, wb_out, residual=h1, name="b_out")
    loss, dh2, dh2_b = _loss_head(h2, target, name="loss_head")

    dog_b = _matmul(dh2_b, wb_out, trans_b=True, name="d_b_out_x")
    dwb_out = _matmul(og_b.T, dh2_b, name="d_b_out_w")
    dq, dk, dv, dz, dbias, dqnw, dknw = _attn_bwd(proj_b, bias, qnw_row, knw_row, dog_b, heads=hb, name="attn_bwd")
    dproj_b = jnp.concatenate([dq, dk, dv, dz], axis=1)
    dhn1 = _matmul(dproj_b, wb_in, trans_b=True, name="d_b_in_x")
    dwb_in = _matmul(hn1.T, dproj_b, name="d_b_in_w")
    dh1, dh1_b, dnw1 = _rmsnorm_bwd(h1, norm_w[1:2], dhn1, dh2, name="d_norm1")

    dog_a = _matmul(dh1_b, wa_out, trans_b=True, name="d_a_out_x")
    dwa_out = _matmul(og_a.T, dh1_b, name="d_a_out_w")
    dq, dk, dv, dz, dab, dconv, dalog, ddtb, donw = _gdn_bwd(
        proj_a, ab_a, conv_w, alog_row, dtb_row, onw_row, states, dog_a, heads=ha, name="gdn_bwd")
    dproj_a = jnp.concatenate([dq, dk, dv, dz], axis=1)
    dab_b = dab.astype(BF16)
    dhn0 = _matmul(dproj_a, wa_main, trans_b=True, name="d_a_in_x")
    dhn0 = _matmul(dab_b, wa_ab, trans_b=True, residual=dhn0, name="d_a_in_ab_x")
    hn0_t = hn0.T
    dwa_in = jnp.concatenate(
        [_matmul(hn0_t, dproj_a, name="d_a_in_w"), _matmul(hn0_t, dab_b, name="d_a_in_ab_w")[:, :2 * ha]], axis=1)
    dx, _, dnw0 = _rmsnorm_bwd(x, norm_w[0:1], dhn0, dh1, name="d_norm0")

    drel = _band_bias_grad(dbias)
    grads = dict(
        norm_w=jnp.concatenate([dnw0, dnw1], axis=0), a_w_in=dwa_in, a_conv_w=dconv, a_a_log=dalog[:, :ha],
        a_dt_bias=ddtb[:, :ha], a_out_norm_w=donw, a_w_out=dwa_out, b_w_in=dwb_in, b_q_norm_w=jnp.sum(dqnw, axis=0),
        b_k_norm_w=jnp.sum(dknw, axis=0), b_rel_bias=drel[None], b_w_out=dwb_out)
    return loss, dx, grads


_ANY = pl.BlockSpec(memory_space=pl.ANY)
_CHIP_FLIPS = ((1, 0), (0, 1), (1, 1))


def _place():
    x, y, c = lax.axis_index("x"), lax.axis_index("y"), lax.axis_index("c")
    return x, y, c


def _flip(v, bit):
    return 1 - v if bit else v


def _remote(src, dst, send_sem, recv_sem, peer):
    return pltpu.make_async_remote_copy(src_ref=src, dst_ref=dst, send_sem=send_sem, recv_sem=recv_sem, device_id=peer,
                                        device_id_type=MESH)


def _comm_call(body, arrays, out_shapes, n_remote, n_local, name):
    scratch = [pltpu.SemaphoreType.DMA((n_remote,)), pltpu.SemaphoreType.DMA((n_remote,))]
    if n_local:
        scratch.append(pltpu.SemaphoreType.DMA((n_local,)))
    return pl.pallas_call(
        body, name=name, in_specs=[_ANY] * len(arrays), out_specs=[_ANY] * len(out_shapes), out_shape=out_shapes,
        scratch_shapes=scratch)(*arrays)


def _gather_chips(shards, *, name):
    n = len(shards)

    def body(*refs):
        ins, outs, (send_sems, recv_sems, local_sems) = refs[:n], refs[n:2 * n], refs[2 * n:]
        x, y, c = _place()
        mine = 2 * x + y
        local, remote, landing = [], [], []
        for a in range(n):
            local.append(pltpu.make_async_copy(ins[a], outs[a].at[mine], local_sems.at[a]))
            for k, (fx, fy) in enumerate(_CHIP_FLIPS):
                peer = (_flip(x, fx), _flip(y, fy), c)
                sems = send_sems.at[3 * a + k], recv_sems.at[3 * a + k]
                remote.append(_remote(ins[a], outs[a].at[mine], *sems, peer))
                landing.append(_remote(ins[a], outs[a].at[2 * peer[0] + peer[1]], *sems, peer))
        for cp in local + remote:
            cp.start()
        for cp in local:
            cp.wait()
        for cp in landing:
            cp.wait_recv()
        for cp in remote:
            cp.wait_send()

    shapes = [jax.ShapeDtypeStruct((N_CHIPS,) + s.shape, s.dtype) for s in shards]
    return _comm_call(body, shards, shapes, 3 * n, n, name)


def _exchange_chips(slabs, *, name):
    n = len(slabs)

    def body(*refs):
        ins, outs, (send_sems, recv_sems, local_sems) = refs[:n], refs[n:2 * n], refs[2 * n:]
        x, y, c = _place()
        mine = 2 * x + y
        local, remote, landing = [], [], []
        for a in range(n):
            local.append(pltpu.make_async_copy(ins[a].at[mine], outs[a].at[mine], local_sems.at[a]))
            for k, (fx, fy) in enumerate(_CHIP_FLIPS):
                peer = (_flip(x, fx), _flip(y, fy), c)
                theirs = 2 * peer[0] + peer[1]
                sems = send_sems.at[3 * a + k], recv_sems.at[3 * a + k]
                remote.append(_remote(ins[a].at[theirs], outs[a].at[mine], *sems, peer))
                landing.append(_remote(ins[a].at[theirs], outs[a].at[theirs], *sems, peer))
        for cp in local + remote:
            cp.start()
        for cp in local:
            cp.wait()
        for cp in landing:
            cp.wait_recv()
        for cp in remote:
            cp.wait_send()

    shapes = [jax.ShapeDtypeStruct(s.shape, s.dtype) for s in slabs]
    return _comm_call(body, slabs, shapes, 3 * n, n, name)


def _swap_pair(arrays, *, name):
    n = len(arrays)

    def body(*refs):
        ins, outs, (send_sems, recv_sems) = refs[:n], refs[n:2 * n], refs[2 * n:]
        x, y, c = _place()
        copies = [_remote(ins[a], outs[a], send_sems.at[a], recv_sems.at[a], (x, y, 1 - c)) for a in range(n)]
        for cp in copies:
            cp.start()
        for cp in copies:
            cp.wait_recv()
        for cp in copies:
            cp.wait_send()

    shapes = [jax.ShapeDtypeStruct(s.shape, s.dtype) for s in arrays]
    return _comm_call(body, arrays, shapes, n, 0, name)


def _gather_all(tile, *, name):
    def body(in_ref, out_ref, send_sems, recv_sems, local_sems):
        x, y, c = _place()
        mine = 4 * x + 2 * y + c
        local = pltpu.make_async_copy(in_ref, out_ref.at[mine], local_sems.at[0])
        remote, landing = [], []
        for k in range(1, N_DEV):
            peer = (_flip(x, k & 4), _flip(y, k & 2), _flip(c, k & 1))
            sems = send_sems.at[k - 1], recv_sems.at[k - 1]
            remote.append(_remote(in_ref, out_ref.at[mine], *sems, peer))
            landing.append(_remote(in_ref, out_ref.at[4 * peer[0] + 2 * peer[1] + peer[2]], *sems, peer))
        for cp in [local] + remote:
            cp.start()
        local.wait()
        for cp in landing:
            cp.wait_recv()
        for cp in remote:
            cp.wait_send()

    return _comm_call(body, [tile], [jax.ShapeDtypeStruct((N_DEV,) + tile.shape, tile.dtype)], N_DEV - 1, 1, name)[0]


def _sum_slots(slabs, *, name, tr=128):
    s, r, c = slabs.shape
    tr = min(tr, r)

    def body(in_ref, o_ref):
        acc = in_ref[0]
        for j in range(1, s):
            acc = acc + in_ref[j]
        o_ref[...] = acc

    return pl.pallas_call(
        body, name=name, grid=(r // tr,),
        in_specs=[pl.BlockSpec((s, tr, c), lambda i: (0, i, 0))], out_specs=pl.BlockSpec((tr, c), lambda i: (i, 0)),
        out_shape=jax.ShapeDtypeStruct((r, c), F32), compiler_params=_params("parallel"))(slabs)


def _adamw_math(w, g, m, v):
    m = ADAM_B1 * m + (1.0 - ADAM_B1) * g
    v = ADAM_B2 * v + (1.0 - ADAM_B2) * (g * g)
    m_hat = m / (1.0 - ADAM_B1 ** ADAM_STEP)
    v_hat = v / (1.0 - ADAM_B2 ** ADAM_STEP)
    delta = -ADAM_LR * (m_hat / (jnp.sqrt(v_hat) + ADAM_EPS) + ADAM_WD * w)
    return delta, m, v


def _adamw(w, m, v, parts, *, name, tr=128):
    r, c = w.shape
    tr = min(tr, r)
    s = len(parts)

    def body(w_ref, m_ref, v_ref, *refs):
        g_ref, d_ref, nm_ref, nv_ref = refs[s:]
        g = refs[0][...]
        for p_ref in refs[1:s]:
            g = g + p_ref[...]
        g_ref[...] = g
        d_ref[...], nm_ref[...], nv_ref[...] = _adamw_math(w_ref[...], g, m_ref[...], v_ref[...])

    blk = pl.BlockSpec((tr, c), lambda i: (i, 0))
    shape = jax.ShapeDtypeStruct((r, c), F32)
    return pl.pallas_call(
        body, name=name, grid=(r // tr,), in_specs=[blk] * (3 + s), out_specs=[blk] * 4, out_shape=[shape] * 4,
        compiler_params=_params("parallel"))(w, m, v, *parts)


_BIG = ("a_w_in", "b_w_in", "a_w_out", "b_w_out", "a_conv_w")
_BIG_BY_COLS = {"a_w_in": True, "b_w_in": True, "a_w_out": False, "b_w_out": False, "a_conv_w": True}
_SMALL = ("norm_w", "a_a_log", "a_dt_bias", "a_out_norm_w", "b_q_norm_w", "b_k_norm_w", "b_rel_bias")
_ORDER = ("norm_w", "a_w_in", "a_conv_w", "a_a_log", "a_dt_bias", "a_out_norm_w", "a_w_out", "b_w_in", "b_q_norm_w",
          "b_k_norm_w", "b_rel_bias", "b_w_out")


def _join_cols(g):
    return jnp.transpose(g, (1, 0, 2)).reshape(g.shape[1], -1)


def _split_cols(g):
    return jnp.transpose(g.reshape(g.shape[0], N_CHIPS, -1), (1, 0, 2))


def _pack(d):
    flat = jnp.concatenate([d[n].reshape(-1) for n in _SMALL])
    return jnp.pad(flat, (0, -flat.shape[0] % LANES)).reshape(1, -1)


def _unpack(row, like):
    out, at = {}, 0
    for n in _SMALL:
        size = like[n].size
        out[n] = row[0, at:at + size].reshape(like[n].shape)
        at += size
    return out


def kernel(x, norm_w, a_w_in, a_conv_w, a_a_log, a_dt_bias, a_out_norm_w, a_w_out, b_w_in, b_q_norm_w, b_k_norm_w, b_rel_bias, b_w_out, loss_target, m_norm_w, m_a_w_in, m_a_conv_w, m_a_a_log, m_a_dt_bias, m_a_out_norm_w, m_a_w_out, m_b_w_in, m_b_q_norm_w, m_b_k_norm_w, m_b_rel_bias, m_b_w_out, v_norm_w, v_a_w_in, v_a_conv_w, v_a_a_log, v_a_dt_bias, v_a_out_norm_w, v_a_w_out, v_b_w_in, v_b_q_norm_w, v_b_k_norm_w, v_b_rel_bias, v_b_w_out):
    w = dict(norm_w=norm_w, a_w_in=a_w_in, a_conv_w=a_conv_w, a_a_log=a_a_log, a_dt_bias=a_dt_bias,
             a_out_norm_w=a_out_norm_w, a_w_out=a_w_out, b_w_in=b_w_in, b_q_norm_w=b_q_norm_w, b_k_norm_w=b_k_norm_w,
             b_rel_bias=b_rel_bias, b_w_out=b_w_out)
    m = dict(norm_w=m_norm_w, a_w_in=m_a_w_in, a_conv_w=m_a_conv_w, a_a_log=m_a_a_log, a_dt_bias=m_a_dt_bias,
             a_out_norm_w=m_a_out_norm_w, a_w_out=m_a_w_out, b_w_in=m_b_w_in, b_q_norm_w=m_b_q_norm_w,
             b_k_norm_w=m_b_k_norm_w, b_rel_bias=m_b_rel_bias, b_w_out=m_b_w_out)
    v = dict(norm_w=v_norm_w, a_w_in=v_a_w_in, a_conv_w=v_a_conv_w, a_a_log=v_a_a_log, a_dt_bias=v_a_dt_bias,
             a_out_norm_w=v_a_out_norm_w, a_w_out=v_a_w_out, b_w_in=v_b_w_in, b_q_norm_w=v_b_q_norm_w,
             b_k_norm_w=v_b_k_norm_w, b_rel_bias=v_b_rel_bias, b_w_out=v_b_w_out)

    shards = [w[n][0].astype(F32 if n == "a_conv_w" else BF16) for n in _BIG]
    full = {n: (_join_cols(g) if _BIG_BY_COLS[n] else g.reshape(-1, g.shape[-1]))
            for n, g in zip(_BIG, _gather_chips(shards, name="gather_weights"))}

    loss, dx, grads = _local_step(
        x[0], loss_target[0], norm_w, full["a_w_in"], full["a_conv_w"], a_a_log, a_dt_bias, a_out_norm_w, full["a_w_out"],
        full["b_w_in"], b_q_norm_w, b_k_norm_w, b_rel_bias, full["b_w_out"])
    loss = lax.psum(loss, ("x", "y", "c"))

    slabs = [_split_cols(grads[n]) if _BIG_BY_COLS[n] else grads[n].reshape(N_CHIPS, -1, grads[n].shape[-1]) for n in _BIG]
    landed = _exchange_chips(slabs, name="scatter_grads")
    mine = [_sum_slots(s, name=f"chip_sum_{n}") for n, s in zip(_BIG, landed)]
    theirs = _swap_pair(mine, name="pair_grads")
    out = {}
    for n, p, q in zip(_BIG, mine, theirs):
        out[n] = [r[None] for r in _adamw(w[n][0], m[n][0], v[n][0], [p, q], name=f"adamw_{n}")]

    row = _pack(grads)
    tiles = _gather_all(jnp.broadcast_to(row, (8, row.shape[1])), name="gather_small_grads")
    res = _adamw(_pack(w), _pack(m), _pack(v), [tiles[d, 0:1, :] for d in range(N_DEV)], name="adamw_small")
    unpacked = [_unpack(r, w) for r in res]
    for n in _SMALL:
        out[n] = [u[n] for u in unpacked]

    return (loss, dx[None], *[out[n][0] for n in _ORDER], *[out[n][1] for n in _ORDER], *[out[n][2] for n in _ORDER],
            *[out[n][3] for n in _ORDER])
```

```python
import functools

import jax
import jax.numpy as jnp
from jax import lax
from jax.experimental import pallas as pl
from jax.experimental.pallas import tpu as pltpu

F32 = jnp.float32
BF16 = jnp.bfloat16

CHUNK = 64
HEAD_DIM = 128
LEFT_CHUNKS = 8
REL_CLIP = 256
CONV_K = 4
EPS = 1e-6
HALO = 8

ADAM_LR = 0.001
ADAM_B1 = 0.9
ADAM_B2 = 0.999
ADAM_EPS = 1e-08
ADAM_WD = 0.01
ADAM_STEP = 10

LANES = 128
N_CHIPS = 4
N_DEV = 8
VMEM_LIMIT_BYTES = 56 * 1024 * 1024
MESH = pl.DeviceIdType.MESH
HIGHEST = lax.Precision.HIGHEST


def _params(*sem):
    return pltpu.CompilerParams(dimension_semantics=sem, vmem_limit_bytes=VMEM_LIMIT_BYTES)


def _dot(a, b, dims=(((1,), (0,)), ((), ())), precision=None):
    return lax.dot_general(a, b, dims, precision=precision, preferred_element_type=F32)


_NT = (((1,), (1,)), ((), ()))
_TN = (((0,), (0,)), ((), ()))


def _bdot(a, b, dims=(((1,), (0,)), ((), ()))):
    return _dot(a.astype(BF16), b.astype(BF16), dims)


def _fdot(a, b, dims=(((1,), (0,)), ((), ()))):
    return _dot(a, b, dims, precision=lax.Precision.HIGH)


def _silu(x):
    return x * jax.nn.sigmoid(x)


def _matmul(a, b, *, name, trans_b=False, residual=None, out_dtype=F32, tm=1024, tn=1024, tk=2048):
    m, k = a.shape
    n = b.shape[0] if trans_b else b.shape[1]
    tm, tn, tk = min(tm, m), min(tn, n), min(tk, k)
    assert m % tm == 0 and n % tn == 0 and k % tk == 0, (a.shape, b.shape, tm, tn, tk)
    nk = k // tk
    dims = _NT if trans_b else (((1,), (0,)), ((), ()))

    def body(*refs):
        if residual is None:
            a_ref, b_ref, o_ref, acc_ref = refs
            r_ref = None
        else:
            a_ref, b_ref, r_ref, o_ref, acc_ref = refs
        kk = pl.program_id(2)

        @pl.when(kk == 0)
        def _():
            acc_ref[...] = jnp.zeros_like(acc_ref)

        acc_ref[...] += _dot(a_ref[...], b_ref[...], dims)

        @pl.when(kk == nk - 1)
        def _():
            r = acc_ref[...]
            if r_ref is not None:
                r = r + r_ref[...]
            o_ref[...] = r.astype(o_ref.dtype)

    in_specs = [
        pl.BlockSpec((tm, tk), lambda i, j, kk: (i, kk)),
        pl.BlockSpec((tn, tk), lambda i, j, kk: (j, kk)) if trans_b else pl.BlockSpec((tk, tn), lambda i, j, kk: (kk, j)),
    ]
    args = [a, b]
    if residual is not None:
        in_specs.append(pl.BlockSpec((tm, tn), lambda i, j, kk: (i, j)))
        args.append(residual)
    return pl.pallas_call(
        body,
        name=name,
        grid=(m // tm, n // tn, nk),
        in_specs=in_specs,
        out_specs=pl.BlockSpec((tm, tn), lambda i, j, kk: (i, j)),
        out_shape=jax.ShapeDtypeStruct((m, n), out_dtype),
        scratch_shapes=[pltpu.VMEM((tm, tn), F32)],
        compiler_params=_params("parallel", "parallel", "arbitrary"),
    )(*args)


def _rms(x, w):
    return x * lax.rsqrt(jnp.mean(x * x, axis=-1, keepdims=True) + EPS) * w


def _rmsnorm_fwd(x, w_row, *, name, tr=512):
    t, d = x.shape
    tr = min(tr, t)

    def body(x_ref, w_ref, o_ref):
        o_ref[...] = _rms(x_ref[...], w_ref[...]).astype(BF16)

    return pl.pallas_call(
        body,
        name=name,
        grid=(t // tr,),
        in_specs=[pl.BlockSpec((tr, d), lambda i: (i, 0)), pl.BlockSpec((1, d), lambda i: (0, 0))],
        out_specs=pl.BlockSpec((tr, d), lambda i: (i, 0)),
        out_shape=jax.ShapeDtypeStruct((t, d), BF16),
        compiler_params=_params("parallel"),
    )(x, w_row)


def _rmsnorm_bwd(x, w_row, dy, dres, *, name, tr=256):
    t, d = x.shape
    tr = min(tr, t)

    def body(x_ref, w_ref, dy_ref, dres_ref, dx_ref, dxb_ref, dw_ref):
        @pl.when(pl.program_id(0) == 0)
        def _():
            dw_ref[...] = jnp.zeros_like(dw_ref)

        _, vjp = jax.vjp(_rms, x_ref[...], w_ref[...])
        dx, dw = vjp(dy_ref[...])
        dx = dx + dres_ref[...]
        dx_ref[...] = dx
        dxb_ref[...] = dx.astype(BF16)
        dw_ref[...] += dw

    row = pl.BlockSpec((tr, d), lambda i: (i, 0))
    vec = pl.BlockSpec((1, d), lambda i: (0, 0))
    return pl.pallas_call(
        body,
        name=name,
        grid=(t // tr,),
        in_specs=[row, vec, row, row],
        out_specs=[row, row, vec],
        out_shape=[jax.ShapeDtypeStruct((t, d), F32), jax.ShapeDtypeStruct((t, d), BF16), jax.ShapeDtypeStruct((1, d), F32)],
        compiler_params=_params("arbitrary"),
    )(x, w_row, dy, dres)


def _loss_head(h, target, *, name, tr=512):
    t, d = h.shape
    tr = min(tr, t)

    def body(h_ref, t_ref, dh_ref, dhb_ref, part_ref):
        @pl.when(pl.program_id(0) == 0)
        def _():
            part_ref[...] = jnp.zeros_like(part_ref)

        err = h_ref[...] - t_ref[...]
        dh = err * (1.0 / d)
        dh_ref[...] = dh
        dhb_ref[...] = dh.astype(BF16)
        part_ref[...] += jnp.sum(err * err, axis=0, keepdims=True)

    row = pl.BlockSpec((tr, d), lambda i: (i, 0))
    vec = pl.BlockSpec((1, d), lambda i: (0, 0))
    dh, dhb, part = pl.pallas_call(
        body,
        name=name,
        grid=(t // tr,),
        in_specs=[row, row],
        out_specs=[row, row, vec],
        out_shape=[jax.ShapeDtypeStruct((t, d), F32), jax.ShapeDtypeStruct((t, d), BF16), jax.ShapeDtypeStruct((1, d), F32)],
        compiler_params=_params("arbitrary"),
    )(h, target)
    return 0.5 / d * jnp.sum(part), dh, dhb


_BNN = (((2,), (1,)), ((0,), (0,)))
_BNT = (((2,), (2,)), ((0,), (0,)))
_BTN = (((1,), (1,)), ((0,), (0,)))


def _conv_silu(xe, w):
    rows = xe.shape[0] - HALO
    first = HALO - (CONV_K - 1)
    c = w[0:1, :] * xe[first:first + rows, :]
    for j in range(1, CONV_K):
        c = c + w[j:j + 1, :] * xe[first + j:first + j + rows, :]
    return _silu(c)


def _gdn_intra(qx, kx, vx, a, b, wq, wk, wv, alog, dtb):
    n = a.shape[0] // CHUNK
    qt, kt, v = _conv_silu(qx, wq), _conv_silu(kx, wk), _conv_silu(vx, wv)
    q = qt * lax.rsqrt(jnp.sum(qt * qt, axis=-1, keepdims=True) + EPS) * (HEAD_DIM ** -0.5)
    k = kt * lax.rsqrt(jnp.sum(kt * kt, axis=-1, keepdims=True) + EPS)
    lanes = jnp.ones((1, HEAD_DIM), F32)
    beta = jax.nn.sigmoid(b) * lanes
    sp = a + dtb
    g = (-jnp.exp(alog) * (jnp.maximum(sp, 0.0) + jnp.log(1.0 + jnp.exp(-jnp.abs(sp))))) * lanes
    q, k, v, beta, g = (t.reshape(n, CHUNK, HEAD_DIM) for t in (q, k, v, beta, g))

    row = lax.broadcasted_iota(jnp.int32, (n, CHUNK, CHUNK), 1)
    col = lax.broadcasted_iota(jnp.int32, (n, CHUNK, CHUNK), 2)
    tri_incl = row >= col
    tri_strict = row > col
    gc = _fdot(tri_incl.astype(F32), g, _BNN)
    gc_row = _fdot(g[:, :, :CHUNK], (row <= col).astype(F32), _BTN)
    decay = jnp.exp(jnp.where(tri_incl, gc[:, :, :CHUNK] - gc_row, -1e30))
    kb = k * beta
    vb = v * beta
    neg_l = jnp.where(tri_strict, -(_bdot(kb, k, _BNT) * decay), 0.0)
    inv = (row == col).astype(F32) + neg_l
    power = neg_l
    for _ in range(5):
        power = _fdot(power, power, _BNN)
        inv = inv + _fdot(inv, power, _BNN)
    e = jnp.exp(gc)
    u = _fdot(inv, vb, _BNN)
    w = _fdot(inv, kb * e, _BNN)
    qk = jnp.where(tri_incl, _bdot(q, k, _BNT) * decay, 0.0)
    g_last = gc[:, CHUNK - 1:CHUNK, :]
    return u, w, qk, q * e, k * jnp.exp(g_last - gc), jnp.exp(g_last)


def _gdn_inter(state, u, w, qk, q_dec, k_dec, decay_last, z, onw):
    state_b = state.astype(BF16)
    v_new = u - _bdot(w, state_b)
    o = _bdot(q_dec, state_b) + _bdot(qk, v_new)
    new_state = state * decay_last + _bdot(k_dec, v_new, _TN)
    return _rms(o, onw) * _silu(z), new_state


def _intra_scratch(n, dtype):
    return [pltpu.VMEM((n, CHUNK, HEAD_DIM), F32), pltpu.VMEM((n, CHUNK, HEAD_DIM), dtype), pltpu.VMEM((n, CHUNK, CHUNK), dtype),
            pltpu.VMEM((n, CHUNK, HEAD_DIM), dtype), pltpu.VMEM((n, CHUNK, HEAD_DIM), dtype), pltpu.VMEM((n, 1, HEAD_DIM), F32)]


def _head_lane(h, offset=0):
    return lax.broadcasted_iota(jnp.int32, (1, LANES), 1) == h + offset


def _pick(mask, x):
    return jnp.sum(jnp.where(mask, x, 0.0), axis=1, keepdims=True)


def _gdn_specs(heads, tb, rev, nb):
    blk = (lambda i: nb - 1 - i) if rev else (lambda i: i)
    hb = tb // HALO

    def col(group):
        return pl.BlockSpec((tb, HEAD_DIM), lambda i, h: (blk(i), group * heads + h))

    def halo(group):
        return pl.BlockSpec((HALO, HEAD_DIM), lambda i, h: (jnp.maximum(blk(i) * hb - 1, 0), group * heads + h))

    def convw(group):
        return pl.BlockSpec((CONV_K, HEAD_DIM), lambda i, h: (0, group * heads + h))

    vec = pl.BlockSpec((1, LANES), lambda i, h: (0, 0))
    ab = pl.BlockSpec((tb, LANES), lambda i, h: (blk(i), 0))
    states = pl.BlockSpec((1, tb // CHUNK, HEAD_DIM, HEAD_DIM), lambda i, h: (h, blk(i), 0, 0))
    return blk, col, halo, convw, vec, ab, states


def _gdn_fwd(proj, ab, conv_w, alog_row, dtb_row, onw_row, *, heads, name, tb=512, gather=()):
    t = proj.shape[0]
    tb = min(tb, t)
    nb, cpb = t // tb, tb // CHUNK
    _, col, halo, convw, vec, abspec, states = _gdn_specs(heads, tb, False, nb)

    def body(q_ref, k_ref, v_ref, qh_ref, kh_ref, vh_ref, z_ref, ab_ref, wq_ref, wk_ref, wv_ref, alog_ref, dtb_ref, onw_ref,
             og_ref, st_ref, state_scr, x_scr, *op_scr):
        i, h = pl.program_id(0), pl.program_id(1)
        for n, (ref, href) in enumerate(((q_ref, qh_ref), (k_ref, kh_ref), (v_ref, vh_ref))):
            x_scr[n, 0:HALO, :] = jnp.where(i > 0, href[...], 0.0)
            x_scr[n, HALO:HALO + tb, :] = ref[...]

        @pl.when(i == 0)
        def _():
            state_scr[h] = jnp.zeros((HEAD_DIM, HEAD_DIM), F32)

        sel_a, sel_b = _head_lane(h), _head_lane(h, heads)
        alog, dtb = _pick(sel_a, alog_ref[...]), _pick(sel_a, dtb_ref[...])
        abv = ab_ref[...]
        ops = _gdn_intra(x_scr[0], x_scr[1], x_scr[2], _pick(sel_a, abv), _pick(sel_b, abv), wq_ref[...], wk_ref[...],
                         wv_ref[...], alog, dtb)
        for scr, val in zip(op_scr, ops):
            scr[...] = val.astype(scr.dtype)
        onw = onw_ref[...]

        def chunk(c, state):
            rows = pl.ds(pl.multiple_of(c * CHUNK, CHUNK), CHUNK)
            st_ref[0, c] = state
            og, new_state = _gdn_inter(state, *[scr[c] for scr in op_scr], z_ref[rows, :], onw)
            og_ref[rows, :] = og.astype(BF16)
            return new_state

        state_scr[h] = lax.fori_loop(0, cpb, chunk, state_scr[h])

    n_x = len(gather)
    og, st, *gathered = pl.pallas_call(
        _with_exchange(body, 14, 2, True, n_x, (nb, heads)),
        name=name,
        grid=(nb, heads),
        in_specs=[col(0), col(1), col(2), halo(0), halo(1), halo(2), col(3), abspec, convw(0), convw(1), convw(2), vec, vec, vec]
        + [_ANY] * n_x,
        out_specs=[pl.BlockSpec((tb, HEAD_DIM), lambda i, h: (i, h)), states] + [_ANY] * n_x,
        out_shape=[jax.ShapeDtypeStruct((t, heads * HEAD_DIM), BF16),
                   jax.ShapeDtypeStruct((heads, t // CHUNK, HEAD_DIM, HEAD_DIM), F32)] + _chip_shapes(True, gather),
        scratch_shapes=[pltpu.VMEM((heads, HEAD_DIM, HEAD_DIM), F32), pltpu.VMEM((3, HALO + tb, HEAD_DIM), F32)]
        + _intra_scratch(cpb, BF16) + (_chip_scratch(n_x) if n_x else []),
        compiler_params=_params("arbitrary", "arbitrary"),
    )(proj, proj, proj, proj, proj, proj, proj, ab, conv_w, conv_w, conv_w, alog_row, dtb_row, onw_row, *gather)
    return og, st, gathered


def _gdn_bwd(proj, ab, conv_w, alog_row, dtb_row, onw_row, states, dog, *, heads, name, tb=512, exchange=()):
    t = proj.shape[0]
    tb = min(tb, t)
    nb, cpb = t // tb, tb // CHUNK
    _, col, halo, convw, vec, abspec, states_spec = _gdn_specs(heads, tb, True, nb)
    n_conv = conv_w.shape[1]

    def body(q_ref, k_ref, v_ref, qh_ref, kh_ref, vh_ref, z_ref, ab_ref, wq_ref, wk_ref, wv_ref, alog_ref, dtb_ref, onw_ref,
             st_ref, dog_ref, dq_ref, dk_ref, dv_ref, dz_ref, dab_ref, dconv_ref, dalog_ref, ddtb_ref, donw_ref,
             dstate_scr, x_scr, carry_scr, *scr):
        op_scr, dop_scr = scr[:6], scr[6:]
        i, h = pl.program_id(0), pl.program_id(1)
        first_block = i == nb - 1
        for n, (ref, href) in enumerate(((q_ref, qh_ref), (k_ref, kh_ref), (v_ref, vh_ref))):
            x_scr[n, 0:HALO, :] = jnp.where(first_block, 0.0, href[...])
            x_scr[n, HALO:HALO + tb, :] = ref[...]

        @pl.when(jnp.logical_and(i == 0, h == 0))
        def _():
            dconv_ref[...] = jnp.zeros_like(dconv_ref)
            dalog_ref[...] = jnp.zeros_like(dalog_ref)
            ddtb_ref[...] = jnp.zeros_like(ddtb_ref)
            donw_ref[...] = jnp.zeros_like(donw_ref)

        @pl.when(h == 0)
        def _():
            dab_ref[...] = jnp.zeros_like(dab_ref)

        @pl.when(i == 0)
        def _():
            dstate_scr[h] = jnp.zeros((HEAD_DIM, HEAD_DIM), F32)
            carry_scr[h] = jnp.zeros((3, HALO, HEAD_DIM), F32)

        sel_a, sel_b = _head_lane(h), _head_lane(h, heads)
        alog, dtb = _pick(sel_a, alog_ref[...]), _pick(sel_a, dtb_ref[...])
        abv = ab_ref[...]
        ops, vjp_intra = jax.vjp(_gdn_intra, x_scr[0], x_scr[1], x_scr[2], _pick(sel_a, abv), _pick(sel_b, abv), wq_ref[...],
                                 wk_ref[...], wv_ref[...], alog, dtb)
        for s, val in zip(op_scr, ops):
            s[...] = val.astype(s.dtype)
        onw = onw_ref[...]

        def chunk(step, carry):
            dstate, donw = carry
            c = cpb - 1 - step
            rows = pl.ds(pl.multiple_of(c * CHUNK, CHUNK), CHUNK)
            _, vjp = jax.vjp(_gdn_inter, st_ref[0, c], *[s[c].astype(F32) for s in op_scr], z_ref[rows, :], onw)
            grads = vjp((dog_ref[rows, :], dstate))
            for s, val in zip(dop_scr, grads[1:7]):
                s[c] = val
            dz_ref[rows, :] = grads[7].astype(BF16)
            return grads[0], donw + grads[8]

        dstate, donw = lax.fori_loop(0, cpb, chunk, (dstate_scr[h], jnp.zeros((1, HEAD_DIM), F32)))
        dstate_scr[h] = dstate
        dqx, dkx, dvx, da, db, dwq, dwk, dwv, dalog, ddtb = vjp_intra(tuple(s[...] for s in dop_scr))
        dab_ref[...] += jnp.where(sel_a, da, 0.0) + jnp.where(sel_b, db, 0.0)
        for n, (dref, dx, dw) in enumerate(((dq_ref, dqx, dwq), (dk_ref, dkx, dwk), (dv_ref, dvx, dwv))):
            x_scr[n] = dx
            x_scr[n, tb:tb + HALO, :] += carry_scr[h, n]
            carry_scr[h, n] = x_scr[n, 0:HALO, :]
            dref[...] = x_scr[n, HALO:HALO + tb, :].astype(BF16)
            lanes = pl.ds(pl.multiple_of((n * heads + h) * HEAD_DIM, HEAD_DIM), HEAD_DIM)
            dconv_ref[:, lanes] += dw
        dalog_ref[...] += jnp.where(sel_a, dalog, 0.0)
        ddtb_ref[...] += jnp.where(sel_a, ddtb, 0.0)
        donw_ref[...] += donw

    out_col = pl.BlockSpec((tb, HEAD_DIM), lambda i, h: (nb - 1 - i, h))
    dog_spec = pl.BlockSpec((tb, HEAD_DIM), lambda i, h: (nb - 1 - i, h))
    col_shape = jax.ShapeDtypeStruct((t, heads * HEAD_DIM), BF16)
    row_shape = jax.ShapeDtypeStruct((1, LANES), F32)
    n_x = len(exchange)
    outs = pl.pallas_call(
        _with_exchange(body, 16, 9, False, n_x, (nb, heads)),
        name=name,
        grid=(nb, heads),
        in_specs=[col(0), col(1), col(2), halo(0), halo(1), halo(2), col(3), abspec, convw(0), convw(1), convw(2), vec, vec, vec,
                  states_spec, dog_spec] + [_ANY] * n_x,
        out_specs=[out_col, out_col, out_col, out_col, abspec,
                   pl.BlockSpec((CONV_K, n_conv), lambda i, h: (0, 0)), vec, vec, vec] + [_ANY] * n_x,
        out_shape=[col_shape, col_shape, col_shape, col_shape, jax.ShapeDtypeStruct((t, LANES), F32),
                   jax.ShapeDtypeStruct((CONV_K, n_conv), F32), row_shape, row_shape, row_shape] + _chip_shapes(False, exchange),
        scratch_shapes=[pltpu.VMEM((heads, HEAD_DIM, HEAD_DIM), F32), pltpu.VMEM((3, HALO + tb, HEAD_DIM), F32),
                        pltpu.VMEM((heads, 3, HALO, HEAD_DIM), F32)] + _intra_scratch(cpb, BF16) + _intra_scratch(cpb, F32)
        + (_chip_scratch(n_x) if n_x else []),
        compiler_params=_params("arbitrary", "arbitrary"),
    )(proj, proj, proj, proj, proj, proj, proj, ab, conv_w, conv_w, conv_w, alog_row, dtb_row, onw_row, states, dog, *exchange)
    return (*outs[:9], outs[9:])


BAND = (LEFT_CHUNKS + 1) * CHUNK
PAD = LEFT_CHUNKS * CHUNK
GROUP = 2
ROWS = GROUP * CHUNK
WIN = (LEFT_CHUNKS + GROUP) * CHUNK
DIAGS = WIN + ROWS - 1
NEAR = PAD + ROWS - 1 - REL_CLIP
assert 0 < NEAR < DIAGS and WIN - PAD - 1 <= REL_CLIP and WIN % LANES == 0


def _band_bias(rel_bias):
    heads = rel_bias.shape[0]
    far = jnp.broadcast_to(rel_bias[:, 2 * REL_CLIP:], (heads, NEAR + 1))
    near = rel_bias[:, 2 * REL_CLIP + NEAR + 1 - DIAGS:2 * REL_CLIP][:, ::-1]
    diag = jnp.concatenate([far, near], axis=1)
    return jnp.stack([diag[:, ROWS - 1 - r:ROWS - 1 - r + WIN] for r in range(ROWS)], axis=1)


def _band_bias_grad(dbias):
    heads = dbias.shape[0]
    diag = sum(jnp.pad(dbias[:, r, :], ((0, 0), (ROWS - 1 - r, r))) for r in range(ROWS))
    far = jnp.sum(diag[:, :NEAR + 1], axis=1, keepdims=True)
    near = diag[:, NEAR + 1:][:, ::-1]
    unused = jnp.zeros((heads, 2 * REL_CLIP - near.shape[1]), F32)
    return jnp.concatenate([unused, near, far], axis=1)


def _attn_groups(q_pre, z, kn, v, bias, qnw, start):
    n = q_pre.shape[0]
    q = _rms(q_pre, qnw)
    s = _bdot(q, kn, _BNT) * (HEAD_DIM ** -0.5) + bias
    group = lax.broadcasted_iota(jnp.int32, (n, ROWS, WIN), 0)
    row = lax.broadcasted_iota(jnp.int32, (n, ROWS, WIN), 1)
    key = lax.broadcasted_iota(jnp.int32, (n, ROWS, WIN), 2)
    band_start = jnp.bitwise_and(row, -CHUNK)
    in_sequence = key >= PAD - (start + group * ROWS)
    valid = jnp.logical_and(jnp.logical_and(key >= band_start, key < band_start + BAND), in_sequence)
    s = jnp.where(valid, s, -1e30)
    p = jnp.exp(s - jnp.max(s, axis=-1, keepdims=True))
    p = p / jnp.sum(p, axis=-1, keepdims=True)
    return _bdot(p, v, _BNN) * _silu(z)


def _attn_specs(heads, tb, t):
    def col(group):
        return pl.BlockSpec((tb, HEAD_DIM), lambda h, i: (i, group * heads + h))

    def full(group):
        return pl.BlockSpec((t, HEAD_DIM), lambda h, i: (0, group * heads + h))

    bias = pl.BlockSpec((1, ROWS, WIN), lambda h, i: (h, 0, 0))
    vec = pl.BlockSpec((1, HEAD_DIM), lambda h, i: (0, 0))
    return col, full, bias, vec


def _attn_windows(scr, block_start, n):
    return jnp.stack([scr[pl.ds(pl.multiple_of(block_start + g * ROWS, ROWS), WIN), :] for g in range(n)])


def _attn_fill(k_ref, v_ref, knw_ref, kn_scr, v_scr, t):
    kn_scr[0:PAD, :] = jnp.zeros((PAD, HEAD_DIM), BF16)
    v_scr[0:PAD, :] = jnp.zeros((PAD, HEAD_DIM), BF16)
    step = min(512, t)

    def fill(j, _):
        rows = pl.ds(pl.multiple_of(j * step, step), step)
        prows = pl.ds(pl.multiple_of(PAD + j * step, CHUNK), step)
        kn_scr[prows, :] = _rms(k_ref[rows, :], knw_ref[...]).astype(BF16)
        v_scr[prows, :] = v_ref[rows, :].astype(BF16)
        return 0

    lax.fori_loop(0, t // step, fill, 0)


def _attn_fwd(proj, bias, qnw_row, knw_row, *, heads, name, tb=512):
    t = proj.shape[0]
    tb = min(tb, t)
    nb, ng = t // tb, tb // ROWS
    col, full, bias_spec, vec = _attn_specs(heads, tb, t)

    def body(q_ref, k_ref, v_ref, z_ref, bias_ref, qnw_ref, knw_ref, og_ref, kn_scr, v_scr):
        i = pl.program_id(1)

        @pl.when(i == 0)
        def _():
            _attn_fill(k_ref, v_ref, knw_ref, kn_scr, v_scr, t)

        start = i * tb
        og = _attn_groups(q_ref[...].reshape(ng, ROWS, HEAD_DIM), z_ref[...].reshape(ng, ROWS, HEAD_DIM),
                          _attn_windows(kn_scr, start, ng), _attn_windows(v_scr, start, ng), bias_ref[0], qnw_ref[...], start)
        og_ref[...] = og.reshape(tb, HEAD_DIM).astype(BF16)

    return pl.pallas_call(
        body,
        name=name,
        grid=(heads, nb),
        in_specs=[col(0), full(1), full(2), col(3), bias_spec, vec, vec],
        out_specs=pl.BlockSpec((tb, HEAD_DIM), lambda h, i: (i, h)),
        out_shape=jax.ShapeDtypeStruct((t, heads * HEAD_DIM), BF16),
        scratch_shapes=[pltpu.VMEM((PAD + t, HEAD_DIM), BF16), pltpu.VMEM((PAD + t, HEAD_DIM), BF16)],
        compiler_params=_params("arbitrary", "arbitrary"),
    )(proj, proj, proj, proj, bias, qnw_row, knw_row)


def _attn_bwd(proj, bias, qnw_row, knw_row, dog, *, heads, name, tb=512):
    t = proj.shape[0]
    tb = min(tb, t)
    nb, ng = t // tb, tb // ROWS
    col, full, bias_spec, vec = _attn_specs(heads, tb, t)

    def body(q_ref, k_ref, v_ref, z_ref, bias_ref, qnw_ref, knw_ref, dog_ref,
             dq_ref, dk_ref, dv_ref, dz_ref, dbias_ref, dqnw_ref, dknw_ref, kn_scr, v_scr, dkn_scr, dv_scr):
        i = pl.program_id(1)

        @pl.when(i == 0)
        def _():
            _attn_fill(k_ref, v_ref, knw_ref, kn_scr, v_scr, t)
            dkn_scr[...] = jnp.zeros_like(dkn_scr)
            dv_scr[...] = jnp.zeros_like(dv_scr)
            dbias_ref[...] = jnp.zeros_like(dbias_ref)
            dqnw_ref[...] = jnp.zeros_like(dqnw_ref)

        start = i * tb
        _, vjp = jax.vjp(
            lambda q_pre, z, kn, v, bias, qnw: _attn_groups(q_pre, z, kn, v, bias, qnw, start),
            q_ref[...].reshape(ng, ROWS, HEAD_DIM), z_ref[...].reshape(ng, ROWS, HEAD_DIM),
            _attn_windows(kn_scr, start, ng).astype(F32), _attn_windows(v_scr, start, ng).astype(F32), bias_ref[0], qnw_ref[...])
        dq, dz, dkn, dv, dbias, dqnw = vjp(dog_ref[...].reshape(ng, ROWS, HEAD_DIM))
        dq_ref[...] = dq.reshape(tb, HEAD_DIM).astype(BF16)
        dz_ref[...] = dz.reshape(tb, HEAD_DIM).astype(BF16)
        for g in range(ng):
            window = pl.ds(pl.multiple_of(start + g * ROWS, ROWS), WIN)
            dkn_scr[window, :] += dkn[g]
            dv_scr[window, :] += dv[g]
        dbias_ref[0] += dbias
        dqnw_ref[0] += dqnw

        @pl.when(i == nb - 1)
        def _():
            step = min(512, t)

            def finish(j, dknw):
                rows = pl.ds(pl.multiple_of(j * step, step), step)
                prows = pl.ds(pl.multiple_of(PAD + j * step, CHUNK), step)
                _, vjp = jax.vjp(_rms, k_ref[rows, :], knw_ref[...])
                dk, dw = vjp(dkn_scr[prows, :])
                dk_ref[rows, :] = dk.astype(BF16)
                dv_ref[rows, :] = dv_scr[prows, :].astype(BF16)
                return dknw + dw

            dknw_ref[0] = lax.fori_loop(0, t // step, finish, jnp.zeros((1, HEAD_DIM), F32))

    out_col = pl.BlockSpec((tb, HEAD_DIM), lambda h, i: (i, h))
    out_full = pl.BlockSpec((t, HEAD_DIM), lambda h, i: (0, h))
    head_vec = pl.BlockSpec((1, 1, HEAD_DIM), lambda h, i: (h, 0, 0))
    col_shape = jax.ShapeDtypeStruct((t, heads * HEAD_DIM), BF16)
    vec_shape = jax.ShapeDtypeStruct((heads, 1, HEAD_DIM), F32)
    return pl.pallas_call(
        body,
        name=name,
        grid=(heads, nb),
        in_specs=[col(0), full(1), full(2), col(3), bias_spec, vec, vec, pl.BlockSpec((tb, HEAD_DIM), lambda h, i: (i, h))],
        out_specs=[out_col, out_full, out_full, out_col, bias_spec, head_vec, head_vec],
        out_shape=[col_shape, col_shape, col_shape, col_shape, jax.ShapeDtypeStruct((heads, ROWS, WIN), F32),
                   vec_shape, vec_shape],
        scratch_shapes=[pltpu.VMEM((PAD + t, HEAD_DIM), BF16), pltpu.VMEM((PAD + t, HEAD_DIM), BF16),
                        pltpu.VMEM((PAD + t, HEAD_DIM), F32), pltpu.VMEM((PAD + t, HEAD_DIM), F32)],
        compiler_params=_params("arbitrary", "arbitrary"),
    )(proj, proj, proj, proj, bias, qnw_row, knw_row, dog)


def _lane_row(v):
    v = v.reshape(1, -1)
    return jnp.pad(v, ((0, 0), (0, LANES - v.shape[1])))


def _local_step(x, target, norm_w, wa_in, conv_w, a_log, dt_bias, onw, wa_out, wb_in, qnw, knw, rel_bias, wb_out, *,
                sharded=False):
    ha, hb = a_log.shape[-1], rel_bias.shape[-2]
    na = 4 * ha * HEAD_DIM
    wa_main = wa_in[:, :na]
    wa_ab = jnp.pad(wa_in[:, na:], ((0, 0), (0, LANES - 2 * ha)))
    alog_row, dtb_row, onw_row = _lane_row(a_log), _lane_row(dt_bias), _lane_row(onw)
    qnw_row, knw_row = _lane_row(qnw), _lane_row(knw)
    bias = _band_bias(rel_bias.reshape(hb, -1))

    hn0 = _rmsnorm_fwd(x, norm_w[0:1], name="norm0")
    proj_a = _matmul(hn0, wa_main, name="a_in")
    ab_a = _matmul(hn0, wa_ab, name="a_in_ab")
    og_a, states, got = _gdn_fwd(proj_a, ab_a, conv_w, alog_row, dtb_row, onw_row, heads=ha, name="gdn_fwd",
                                 gather=[wb_in, wa_out, wb_out] if sharded else [])
    if sharded:
        wb_in, wa_out, wb_out = _join_cols(got[0]), got[1].reshape(-1, got[1].shape[-1]), got[2].reshape(-1, got[2].shape[-1])
    h1 = _matmul(og_a, wa_out, residual=x, name="a_out")
    hn1 = _rmsnorm_fwd(h1, norm_w[1:2], name="norm1")
    proj_b = _matmul(hn1, wb_in, name="b_in")
    og_b = _attn_fwd(proj_b, bias, qnw_row, knw_row, heads=hb, name="attn_fwd")
    h2 = _matmul(og_b, wb_out, residual=h1, name="b_out")
    loss, dh2, dh2_b = _loss_head(h2, target, name="loss_head")

    dog_b = _matmul(dh2_b, wb_out, trans_b=True, name="d_b_out_x")
    dwb_out = _matmul(og_b.T, dh2_b, name="d_b_out_w")
    dq, dk, dv, dz, dbias, dqnw, dknw = _attn_bwd(proj_b, bias, qnw_row, knw_row, dog_b, heads=hb, name="attn_bwd")
    dproj_b = jnp.concatenate([dq, dk, dv, dz], axis=1)
    dhn1 = _matmul(dproj_b, wb_in, trans_b=True, name="d_b_in_x")
    dwb_in = _matmul(hn1.T, dproj_b, name="d_b_in_w")
    dh1, dh1_b, dnw1 = _rmsnorm_bwd(h1, norm_w[1:2], dhn1, dh2, name="d_norm1")

    dog_a = _matmul(dh1_b, wa_out, trans_b=True, name="d_a_out_x")
    dwa_out = _matmul(og_a.T, dh1_b, name="d_a_out_w")
    early = [_split_cols(dwb_in), _split_rows(dwa_out), _split_rows(dwb_out)] if sharded else []
    dq, dk, dv, dz, dab, dconv, dalog, ddtb, donw, landed = _gdn_bwd(
        proj_a, ab_a, conv_w, alog_row, dtb_row, onw_row, states, dog_a, heads=ha, name="gdn_bwd",
        exchange=[s.astype(BF16) for s in early])
    if sharded:
        dwb_in, dwa_out, dwb_out = landed
    dproj_a = jnp.concatenate([dq, dk, dv, dz], axis=1)
    dab_b = dab.astype(BF16)
    dhn0 = _matmul(dproj_a, wa_main, trans_b=True, name="d_a_in_x")
    dhn0 = _matmul(dab_b, wa_ab, trans_b=True, residual=dhn0, name="d_a_in_ab_x")
    hn0_t = hn0.T
    dwa_in = jnp.concatenate(
        [_matmul(hn0_t, dproj_a, name="d_a_in_w"), _matmul(hn0_t, dab_b, name="d_a_in_ab_w")[:, :2 * ha]], axis=1)
    dx, _, dnw0 = _rmsnorm_bwd(x, norm_w[0:1], dhn0, dh1, name="d_norm0")

    drel = _band_bias_grad(dbias)
    grads = dict(
        norm_w=jnp.concatenate([dnw0, dnw1], axis=0), a_w_in=dwa_in, a_conv_w=dconv, a_a_log=dalog[:, :ha],
        a_dt_bias=ddtb[:, :ha], a_out_norm_w=donw, a_w_out=dwa_out, b_w_in=dwb_in, b_q_norm_w=jnp.sum(dqnw, axis=0),
        b_k_norm_w=jnp.sum(dknw, axis=0), b_rel_bias=drel[None], b_w_out=dwb_out)
    return loss, dx, grads


_ANY = pl.BlockSpec(memory_space=pl.ANY)
_CHIP_FLIPS = ((1, 0), (0, 1), (1, 1))


def _place():
    x, y, c = lax.axis_index("x"), lax.axis_index("y"), lax.axis_index("c")
    return x, y, c


def _flip(v, bit):
    return 1 - v if bit else v


def _remote(src, dst, send_sem, recv_sem, peer):
    return pltpu.make_async_remote_copy(src_ref=src, dst_ref=dst, send_sem=send_sem, recv_sem=recv_sem, device_id=peer,
                                        device_id_type=MESH)


def _comm_call(body, arrays, out_shapes, n_remote, n_local, name):
    scratch = [pltpu.SemaphoreType.DMA((n_remote,)), pltpu.SemaphoreType.DMA((n_remote,))]
    if n_local:
        scratch.append(pltpu.SemaphoreType.DMA((n_local,)))
    return pl.pallas_call(
        body, name=name, in_specs=[_ANY] * len(arrays), out_specs=[_ANY] * len(out_shapes), out_shape=out_shapes,
        scratch_shapes=scratch)(*arrays)


def _chip_scratch(n):
    return [pltpu.SemaphoreType.DMA((3 * n,)), pltpu.SemaphoreType.DMA((3 * n,)), pltpu.SemaphoreType.DMA((n,))]


def _chip_shapes(gather, arrays):
    return [jax.ShapeDtypeStruct(((N_CHIPS,) + s.shape) if gather else s.shape, s.dtype) for s in arrays]


def _chip_traffic(gather, ins, outs, sems):
    send_sems, recv_sems, local_sems = sems
    x, y, c = _place()
    mine = 2 * x + y
    local, remote, landing = [], [], []
    for a in range(len(ins)):
        local.append(pltpu.make_async_copy(ins[a] if gather else ins[a].at[mine], outs[a].at[mine], local_sems.at[a]))
        for k, (fx, fy) in enumerate(_CHIP_FLIPS):
            peer = (_flip(x, fx), _flip(y, fy), c)
            theirs = 2 * peer[0] + peer[1]
            src = ins[a] if gather else ins[a].at[theirs]
            pair = send_sems.at[3 * a + k], recv_sems.at[3 * a + k]
            remote.append(_remote(src, outs[a].at[mine], *pair, peer))
            landing.append(_remote(src, outs[a].at[theirs], *pair, peer))
    return local + remote, (local, landing, remote)


def _start(traffic):
    for cp in traffic[0]:
        cp.start()


def _finish(traffic):
    local, landing, remote = traffic[1]
    for cp in local:
        cp.wait()
    for cp in landing:
        cp.wait_recv()
    for cp in remote:
        cp.wait_send()


def _with_exchange(compute, n_in, n_out, gather, n_x, grid):
    if not n_x:
        return compute

    def body(*refs):
        ins, x_in = refs[:n_in], refs[n_in:n_in + n_x]
        outs, x_out = refs[n_in + n_x:n_in + n_x + n_out], refs[n_in + n_x + n_out:n_in + 2 * n_x + n_out]
        scratch, sems = refs[n_in + 2 * n_x + n_out:-3], refs[-3:]
        traffic = _chip_traffic(gather, x_in, x_out, sems)
        first = functools.reduce(jnp.logical_and, [pl.program_id(d) == 0 for d in range(len(grid))])
        last = functools.reduce(jnp.logical_and, [pl.program_id(d) == grid[d] - 1 for d in range(len(grid))])

        @pl.when(first)
        def _():
            _start(traffic)

        compute(*ins, *outs, *scratch)

        @pl.when(last)
        def _():
            _finish(traffic)

    return body


def _chip_call(gather, arrays, *, name):
    n = len(arrays)

    def body(*refs):
        traffic = _chip_traffic(gather, refs[:n], refs[n:2 * n], refs[2 * n:])
        _start(traffic)
        _finish(traffic)

    return pl.pallas_call(
        body, name=name, in_specs=[_ANY] * n, out_specs=[_ANY] * n, out_shape=_chip_shapes(gather, arrays),
        scratch_shapes=_chip_scratch(n))(*arrays)


def _swap_pair(arrays, *, name):
    n = len(arrays)

    def body(*refs):
        ins, outs, (send_sems, recv_sems) = refs[:n], refs[n:2 * n], refs[2 * n:]
        x, y, c = _place()
        copies = [_remote(ins[a], outs[a], send_sems.at[a], recv_sems.at[a], (x, y, 1 - c)) for a in range(n)]
        for cp in copies:
            cp.start()
        for cp in copies:
            cp.wait_recv()
        for cp in copies:
            cp.wait_send()

    shapes = [jax.ShapeDtypeStruct(s.shape, s.dtype) for s in arrays]
    return _comm_call(body, arrays, shapes, n, 0, name)


def _gather_all(tile, *, name):
    def body(in_ref, out_ref, send_sems, recv_sems, local_sems):
        x, y, c = _place()
        mine = 4 * x + 2 * y + c
        local = pltpu.make_async_copy(in_ref, out_ref.at[mine], local_sems.at[0])
        remote, landing = [], []
        for k in range(1, N_DEV):
            peer = (_flip(x, k & 4), _flip(y, k & 2), _flip(c, k & 1))
            sems = send_sems.at[k - 1], recv_sems.at[k - 1]
            remote.append(_remote(in_ref, out_ref.at[mine], *sems, peer))
            landing.append(_remote(in_ref, out_ref.at[4 * peer[0] + 2 * peer[1] + peer[2]], *sems, peer))
        for cp in [local] + remote:
            cp.start()
        local.wait()
        for cp in landing:
            cp.wait_recv()
        for cp in remote:
            cp.wait_send()

    return _comm_call(body, [tile], [jax.ShapeDtypeStruct((N_DEV,) + tile.shape, tile.dtype)], N_DEV - 1, 1, name)[0]


def _sum_slots(slabs, *, name, tr=128):
    s, r, c = slabs.shape
    tr = min(tr, r)

    def body(in_ref, o_ref):
        acc = in_ref[0].astype(F32)
        for j in range(1, s):
            acc = acc + in_ref[j].astype(F32)
        o_ref[...] = acc

    return pl.pallas_call(
        body, name=name, grid=(r // tr,),
        in_specs=[pl.BlockSpec((s, tr, c), lambda i: (0, i, 0))], out_specs=pl.BlockSpec((tr, c), lambda i: (i, 0)),
        out_shape=jax.ShapeDtypeStruct((r, c), F32), compiler_params=_params("parallel"))(slabs)


def _adamw_math(w, g, m, v):
    m = ADAM_B1 * m + (1.0 - ADAM_B1) * g
    v = ADAM_B2 * v + (1.0 - ADAM_B2) * (g * g)
    m_hat = m / (1.0 - ADAM_B1 ** ADAM_STEP)
    v_hat = v / (1.0 - ADAM_B2 ** ADAM_STEP)
    delta = -ADAM_LR * (m_hat / (jnp.sqrt(v_hat) + ADAM_EPS) + ADAM_WD * w)
    return delta, m, v


def _adamw(w, m, v, parts, *, name, tr=128):
    r, c = w.shape
    tr = min(tr, r)
    s = len(parts)

    def body(w_ref, m_ref, v_ref, *refs):
        g_ref, d_ref, nm_ref, nv_ref = refs[s:]
        g = refs[0][...]
        for p_ref in refs[1:s]:
            g = g + p_ref[...]
        g_ref[...] = g
        d_ref[...], nm_ref[...], nv_ref[...] = _adamw_math(w_ref[...], g, m_ref[...], v_ref[...])

    blk = pl.BlockSpec((tr, c), lambda i: (i, 0))
    shape = jax.ShapeDtypeStruct((r, c), F32)
    return pl.pallas_call(
        body, name=name, grid=(r // tr,), in_specs=[blk] * (3 + s), out_specs=[blk] * 4, out_shape=[shape] * 4,
        compiler_params=_params("parallel"))(w, m, v, *parts)


_BIG = ("a_w_in", "b_w_in", "a_w_out", "b_w_out", "a_conv_w")
_SMALL = ("norm_w", "a_a_log", "a_dt_bias", "a_out_norm_w", "b_q_norm_w", "b_k_norm_w", "b_rel_bias")
_ORDER = ("norm_w", "a_w_in", "a_conv_w", "a_a_log", "a_dt_bias", "a_out_norm_w", "a_w_out", "b_w_in", "b_q_norm_w",
          "b_k_norm_w", "b_rel_bias", "b_w_out")


def _join_cols(g):
    return jnp.transpose(g, (1, 0, 2)).reshape(g.shape[1], -1)


def _split_cols(g):
    return jnp.transpose(g.reshape(g.shape[0], N_CHIPS, -1), (1, 0, 2))


def _split_rows(g):
    return g.reshape(N_CHIPS, -1, g.shape[-1])


def _pack(d):
    flat = jnp.concatenate([d[n].reshape(-1) for n in _SMALL])
    return jnp.pad(flat, (0, -flat.shape[0] % LANES)).reshape(1, -1)


def _unpack(row, like):
    out, at = {}, 0
    for n in _SMALL:
        size = like[n].size
        out[n] = row[0, at:at + size].reshape(like[n].shape)
        at += size
    return out


def kernel(x, norm_w, a_w_in, a_conv_w, a_a_log, a_dt_bias, a_out_norm_w, a_w_out, b_w_in, b_q_norm_w, b_k_norm_w, b_rel_bias, b_w_out, loss_target, m_norm_w, m_a_w_in, m_a_conv_w, m_a_a_log, m_a_dt_bias, m_a_out_norm_w, m_a_w_out, m_b_w_in, m_b_q_norm_w, m_b_k_norm_w, m_b_rel_bias, m_b_w_out, v_norm_w, v_a_w_in, v_a_conv_w, v_a_a_log, v_a_dt_bias, v_a_out_norm_w, v_a_w_out, v_b_w_in, v_b_q_norm_w, v_b_k_norm_w, v_b_rel_bias, v_b_w_out):
    w = dict(norm_w=norm_w, a_w_in=a_w_in, a_conv_w=a_conv_w, a_a_log=a_a_log, a_dt_bias=a_dt_bias,
             a_out_norm_w=a_out_norm_w, a_w_out=a_w_out, b_w_in=b_w_in, b_q_norm_w=b_q_norm_w, b_k_norm_w=b_k_norm_w,
             b_rel_bias=b_rel_bias, b_w_out=b_w_out)
    m = dict(norm_w=m_norm_w, a_w_in=m_a_w_in, a_conv_w=m_a_conv_w, a_a_log=m_a_a_log, a_dt_bias=m_a_dt_bias,
             a_out_norm_w=m_a_out_norm_w, a_w_out=m_a_w_out, b_w_in=m_b_w_in, b_q_norm_w=m_b_q_norm_w,
             b_k_norm_w=m_b_k_norm_w, b_rel_bias=m_b_rel_bias, b_w_out=m_b_w_out)
    v = dict(norm_w=v_norm_w, a_w_in=v_a_w_in, a_conv_w=v_a_conv_w, a_a_log=v_a_a_log, a_dt_bias=v_a_dt_bias,
             a_out_norm_w=v_a_out_norm_w, a_w_out=v_a_w_out, b_w_in=v_b_w_in, b_q_norm_w=v_b_q_norm_w,
             b_k_norm_w=v_b_k_norm_w, b_rel_bias=v_b_rel_bias, b_w_out=v_b_w_out)

    wa_in, conv = _chip_call(True, [a_w_in[0].astype(BF16), a_conv_w[0]], name="gather_a_in")
    loss, dx, grads = _local_step(
        x[0], loss_target[0], norm_w, _join_cols(wa_in), _join_cols(conv), a_a_log, a_dt_bias, a_out_norm_w,
        a_w_out[0].astype(BF16), b_w_in[0].astype(BF16), b_q_norm_w, b_k_norm_w, b_rel_bias, b_w_out[0].astype(BF16),
        sharded=True)
    loss = lax.psum(loss, ("x", "y", "c"))

    landed = dict(grads)
    landed["a_w_in"], landed["a_conv_w"] = _chip_call(
        False, [_split_cols(grads["a_w_in"]).astype(BF16), _split_cols(grads["a_conv_w"])], name="scatter_a_in")
    mine = [_sum_slots(landed[n], name=f"chip_sum_{n}") for n in _BIG]
    theirs = _swap_pair(mine, name="pair_grads")
    out = {}
    for n, p, q in zip(_BIG, mine, theirs):
        out[n] = [r[None] for r in _adamw(w[n][0], m[n][0], v[n][0], [p, q], name=f"adamw_{n}")]

    row = _pack(grads)
    tiles = _gather_all(jnp.broadcast_to(row, (8, row.shape[1])), name="gather_small_grads")
    res = _adamw(_pack(w), _pack(m), _pack(v), [tiles[d, 0:1, :] for d in range(N_DEV)], name="adamw_small")
    unpacked = [_unpack(r, w) for r in res]
    for n in _SMALL:
        out[n] = [u[n] for u in unpacked]

    return (loss, dx[None], *[out[n][0] for n in _ORDER], *[out[n][1] for n in _ORDER], *[out[n][2] for n in _ORDER],
            *[out[n][3] for n in _ORDER])
```

```python
import functools

import jax
import jax.numpy as jnp
from jax import lax
from jax.experimental import pallas as pl
from jax.experimental.pallas import tpu as pltpu

F32 = jnp.float32
BF16 = jnp.bfloat16

CHUNK = 64
HEAD_DIM = 128
LEFT_CHUNKS = 8
REL_CLIP = 256
CONV_K = 4
EPS = 1e-6
HALO = 8

ADAM_LR = 0.001
ADAM_B1 = 0.9
ADAM_B2 = 0.999
ADAM_EPS = 1e-08
ADAM_WD = 0.01
ADAM_STEP = 10

LANES = 128
N_CHIPS = 4
N_DEV = 8
VMEM_LIMIT_BYTES = 56 * 1024 * 1024
MESH = pl.DeviceIdType.MESH
HIGHEST = lax.Precision.HIGHEST


def _params(*sem):
    return pltpu.CompilerParams(dimension_semantics=sem, vmem_limit_bytes=VMEM_LIMIT_BYTES)


def _dot(a, b, dims=(((1,), (0,)), ((), ())), precision=None):
    return lax.dot_general(a, b, dims, precision=precision, preferred_element_type=F32)


_NT = (((1,), (1,)), ((), ()))
_TN = (((0,), (0,)), ((), ()))


def _bdot(a, b, dims=(((1,), (0,)), ((), ()))):
    return _dot(a.astype(BF16), b.astype(BF16), dims)


def _fdot(a, b, dims=(((1,), (0,)), ((), ()))):
    return _dot(a, b, dims, precision=lax.Precision.HIGH)


def _silu(x):
    return x * jax.nn.sigmoid(x)


def _matmul(a, b, *, name, trans_b=False, residual=None, out_dtype=F32, tm=1024, tn=1024, tk=2048):
    m, k = a.shape
    n = b.shape[0] if trans_b else b.shape[1]
    tm, tn, tk = min(tm, m), min(tn, n), min(tk, k)
    assert m % tm == 0 and n % tn == 0 and k % tk == 0, (a.shape, b.shape, tm, tn, tk)
    nk = k // tk
    dims = _NT if trans_b else (((1,), (0,)), ((), ()))

    def body(*refs):
        if residual is None:
            a_ref, b_ref, o_ref, acc_ref = refs
            r_ref = None
        else:
            a_ref, b_ref, r_ref, o_ref, acc_ref = refs
        kk = pl.program_id(2)

        @pl.when(kk == 0)
        def _():
            acc_ref[...] = jnp.zeros_like(acc_ref)

        acc_ref[...] += _dot(a_ref[...], b_ref[...], dims)

        @pl.when(kk == nk - 1)
        def _():
            r = acc_ref[...]
            if r_ref is not None:
                r = r + r_ref[...]
            o_ref[...] = r.astype(o_ref.dtype)

    in_specs = [
        pl.BlockSpec((tm, tk), lambda i, j, kk: (i, kk)),
        pl.BlockSpec((tn, tk), lambda i, j, kk: (j, kk)) if trans_b else pl.BlockSpec((tk, tn), lambda i, j, kk: (kk, j)),
    ]
    args = [a, b]
    if residual is not None:
        in_specs.append(pl.BlockSpec((tm, tn), lambda i, j, kk: (i, j)))
        args.append(residual)
    return pl.pallas_call(
        body,
        name=name,
        grid=(m // tm, n // tn, nk),
        in_specs=in_specs,
        out_specs=pl.BlockSpec((tm, tn), lambda i, j, kk: (i, j)),
        out_shape=jax.ShapeDtypeStruct((m, n), out_dtype),
        scratch_shapes=[pltpu.VMEM((tm, tn), F32)],
        compiler_params=_params("parallel", "parallel", "arbitrary"),
    )(*args)


def _rms(x, w):
    return x * lax.rsqrt(jnp.mean(x * x, axis=-1, keepdims=True) + EPS) * w


def _rmsnorm_fwd(x, w_row, *, name, tr=512):
    t, d = x.shape
    tr = min(tr, t)

    def body(x_ref, w_ref, o_ref):
        o_ref[...] = _rms(x_ref[...], w_ref[...]).astype(BF16)

    return pl.pallas_call(
        body,
        name=name,
        grid=(t // tr,),
        in_specs=[pl.BlockSpec((tr, d), lambda i: (i, 0)), pl.BlockSpec((1, d), lambda i: (0, 0))],
        out_specs=pl.BlockSpec((tr, d), lambda i: (i, 0)),
        out_shape=jax.ShapeDtypeStruct((t, d), BF16),
        compiler_params=_params("parallel"),
    )(x, w_row)


def _rmsnorm_bwd(x, w_row, dy, dres, *, name, tr=256):
    t, d = x.shape
    tr = min(tr, t)

    def body(x_ref, w_ref, dy_ref, dres_ref, dx_ref, dxb_ref, dw_ref):
        @pl.when(pl.program_id(0) == 0)
        def _():
            dw_ref[...] = jnp.zeros_like(dw_ref)

        _, vjp = jax.vjp(_rms, x_ref[...], w_ref[...])
        dx, dw = vjp(dy_ref[...])
        dx = dx + dres_ref[...]
        dx_ref[...] = dx
        dxb_ref[...] = dx.astype(BF16)
        dw_ref[...] += dw

    row = pl.BlockSpec((tr, d), lambda i: (i, 0))
    vec = pl.BlockSpec((1, d), lambda i: (0, 0))
    return pl.pallas_call(
        body,
        name=name,
        grid=(t // tr,),
        in_specs=[row, vec, row, row],
        out_specs=[row, row, vec],
        out_shape=[jax.ShapeDtypeStruct((t, d), F32), jax.ShapeDtypeStruct((t, d), BF16), jax.ShapeDtypeStruct((1, d), F32)],
        compiler_params=_params("arbitrary"),
    )(x, w_row, dy, dres)


def _loss_head(h, target, *, name, tr=512):
    t, d = h.shape
    tr = min(tr, t)

    def body(h_ref, t_ref, dh_ref, dhb_ref, part_ref):
        @pl.when(pl.program_id(0) == 0)
        def _():
            part_ref[...] = jnp.zeros_like(part_ref)

        err = h_ref[...] - t_ref[...]
        dh = err * (1.0 / d)
        dh_ref[...] = dh
        dhb_ref[...] = dh.astype(BF16)
        part_ref[...] += jnp.sum(err * err, axis=0, keepdims=True)

    row = pl.BlockSpec((tr, d), lambda i: (i, 0))
    vec = pl.BlockSpec((1, d), lambda i: (0, 0))
    dh, dhb, part = pl.pallas_call(
        body,
        name=name,
        grid=(t // tr,),
        in_specs=[row, row],
        out_specs=[row, row, vec],
        out_shape=[jax.ShapeDtypeStruct((t, d), F32), jax.ShapeDtypeStruct((t, d), BF16), jax.ShapeDtypeStruct((1, d), F32)],
        compiler_params=_params("arbitrary"),
    )(h, target)
    return 0.5 / d * jnp.sum(part), dh, dhb


_BNN = (((2,), (1,)), ((0,), (0,)))
_BNT = (((2,), (2,)), ((0,), (0,)))
_BTN = (((1,), (1,)), ((0,), (0,)))


def _conv_silu(xe, w):
    rows = xe.shape[0] - HALO
    first = HALO - (CONV_K - 1)
    c = w[0:1, :] * xe[first:first + rows, :]
    for j in range(1, CONV_K):
        c = c + w[j:j + 1, :] * xe[first + j:first + j + rows, :]
    return _silu(c)


def _gdn_intra(qx, kx, vx, a, b, wq, wk, wv, alog, dtb):
    n = a.shape[0] // CHUNK
    qt, kt, v = _conv_silu(qx, wq), _conv_silu(kx, wk), _conv_silu(vx, wv)
    q = qt * lax.rsqrt(jnp.sum(qt * qt, axis=-1, keepdims=True) + EPS) * (HEAD_DIM ** -0.5)
    k = kt * lax.rsqrt(jnp.sum(kt * kt, axis=-1, keepdims=True) + EPS)
    lanes = jnp.ones((1, HEAD_DIM), F32)
    beta = jax.nn.sigmoid(b) * lanes
    sp = a + dtb
    g = (-jnp.exp(alog) * (jnp.maximum(sp, 0.0) + jnp.log(1.0 + jnp.exp(-jnp.abs(sp))))) * lanes
    q, k, v, beta, g = (t.reshape(n, CHUNK, HEAD_DIM) for t in (q, k, v, beta, g))

    row = lax.broadcasted_iota(jnp.int32, (n, CHUNK, CHUNK), 1)
    col = lax.broadcasted_iota(jnp.int32, (n, CHUNK, CHUNK), 2)
    tri_incl = row >= col
    tri_strict = row > col
    gc = _fdot(tri_incl.astype(F32), g, _BNN)
    gc_row = _fdot(g[:, :, :CHUNK], (row <= col).astype(F32), _BTN)
    decay = jnp.exp(jnp.where(tri_incl, gc[:, :, :CHUNK] - gc_row, -1e30))
    kb = k * beta
    vb = v * beta
    neg_l = jnp.where(tri_strict, -(_bdot(kb, k, _BNT) * decay), 0.0)
    inv = (row == col).astype(F32) + neg_l
    power = neg_l
    for _ in range(5):
        power = _bdot(power, power, _BNN)
        inv = inv + _bdot(inv, power, _BNN)
    e = jnp.exp(gc)
    u = _bdot(inv, vb, _BNN)
    w = _bdot(inv, kb * e, _BNN)
    qk = jnp.where(tri_incl, _bdot(q, k, _BNT) * decay, 0.0)
    g_last = gc[:, CHUNK - 1:CHUNK, :]
    k_dec = k * jnp.exp(g_last - gc)
    step = -_bdot(k_dec, w, _BTN)
    add = _bdot(k_dec, u, _BTN)
    read = q * e - _bdot(qk, w, _BNN)
    out = _bdot(qk, u, _BNN)
    return step, add, jnp.exp(g_last), read, out


def _gdn_scan_step(state, step, add, decay_last):
    return state * decay_last + _bdot(step, state) + add


def _gdn_outputs(states, read, out, z, onw):
    return _rms(_bdot(read, states, _BNN) + out, onw) * _silu(z)


def _scan_scratch(n, dtype):
    return [pltpu.VMEM((n, HEAD_DIM, HEAD_DIM), dtype), pltpu.VMEM((n, HEAD_DIM, HEAD_DIM), F32), pltpu.VMEM((n, 1, HEAD_DIM), F32)]


def _head_lane(h, offset=0):
    return lax.broadcasted_iota(jnp.int32, (1, LANES), 1) == h + offset


def _pick(mask, x):
    return jnp.sum(jnp.where(mask, x, 0.0), axis=1, keepdims=True)


def _gdn_specs(heads, tb, rev, nb):
    blk = (lambda i: nb - 1 - i) if rev else (lambda i: i)
    hb = tb // HALO

    def col(group):
        return pl.BlockSpec((tb, HEAD_DIM), lambda i, h: (blk(i), group * heads + h))

    def halo(group):
        return pl.BlockSpec((HALO, HEAD_DIM), lambda i, h: (jnp.maximum(blk(i) * hb - 1, 0), group * heads + h))

    def convw(group):
        return pl.BlockSpec((CONV_K, HEAD_DIM), lambda i, h: (0, group * heads + h))

    vec = pl.BlockSpec((1, LANES), lambda i, h: (0, 0))
    ab = pl.BlockSpec((tb, LANES), lambda i, h: (blk(i), 0))
    states = pl.BlockSpec((1, tb // CHUNK, HEAD_DIM, HEAD_DIM), lambda i, h: (h, blk(i), 0, 0))
    return blk, col, halo, convw, vec, ab, states


def _gdn_fwd(proj, ab, conv_w, alog_row, dtb_row, onw_row, *, heads, name, tb=512, gather=()):
    t = proj.shape[0]
    tb = min(tb, t)
    nb, cpb = t // tb, tb // CHUNK
    _, col, halo, convw, vec, abspec, states = _gdn_specs(heads, tb, False, nb)

    def body(q_ref, k_ref, v_ref, qh_ref, kh_ref, vh_ref, z_ref, ab_ref, wq_ref, wk_ref, wv_ref, alog_ref, dtb_ref, onw_ref,
             og_ref, st_ref, state_scr, x_scr, *op_scr):
        i, h = pl.program_id(0), pl.program_id(1)
        for n, (ref, href) in enumerate(((q_ref, qh_ref), (k_ref, kh_ref), (v_ref, vh_ref))):
            x_scr[n, 0:HALO, :] = jnp.where(i > 0, href[...], 0.0)
            x_scr[n, HALO:HALO + tb, :] = ref[...]

        @pl.when(i == 0)
        def _():
            state_scr[h] = jnp.zeros((HEAD_DIM, HEAD_DIM), F32)

        sel_a, sel_b = _head_lane(h), _head_lane(h, heads)
        alog, dtb = _pick(sel_a, alog_ref[...]), _pick(sel_a, dtb_ref[...])
        abv = ab_ref[...]
        *scan, read, out = _gdn_intra(x_scr[0], x_scr[1], x_scr[2], _pick(sel_a, abv), _pick(sel_b, abv), wq_ref[...],
                                      wk_ref[...], wv_ref[...], alog, dtb)
        for scr, val in zip(op_scr, scan):
            scr[...] = val.astype(scr.dtype)

        def chunk(c, state):
            st_ref[0, c] = state
            return _gdn_scan_step(state, *[scr[c] for scr in op_scr])

        state_scr[h] = lax.fori_loop(0, cpb, chunk, state_scr[h])
        og = _gdn_outputs(st_ref[0], read, out, z_ref[...].reshape(cpb, CHUNK, HEAD_DIM), onw_ref[...])
        og_ref[...] = og.reshape(tb, HEAD_DIM).astype(BF16)

    n_x = len(gather)
    og, st, *gathered = pl.pallas_call(
        _with_exchange(body, 14, 2, True, n_x, (nb, heads)),
        name=name,
        grid=(nb, heads),
        in_specs=[col(0), col(1), col(2), halo(0), halo(1), halo(2), col(3), abspec, convw(0), convw(1), convw(2), vec, vec, vec]
        + [_ANY] * n_x,
        out_specs=[pl.BlockSpec((tb, HEAD_DIM), lambda i, h: (i, h)), states] + [_ANY] * n_x,
        out_shape=[jax.ShapeDtypeStruct((t, heads * HEAD_DIM), BF16),
                   jax.ShapeDtypeStruct((heads, t // CHUNK, HEAD_DIM, HEAD_DIM), F32)] + _chip_shapes(True, gather),
        scratch_shapes=[pltpu.VMEM((heads, HEAD_DIM, HEAD_DIM), F32), pltpu.VMEM((3, HALO + tb, HEAD_DIM), F32)]
        + _scan_scratch(cpb, BF16) + (_chip_scratch(n_x) if n_x else []),
        compiler_params=_params("arbitrary", "arbitrary"),
    )(proj, proj, proj, proj, proj, proj, proj, ab, conv_w, conv_w, conv_w, alog_row, dtb_row, onw_row, *gather)
    return og, st, gathered


def _gdn_bwd(proj, ab, conv_w, alog_row, dtb_row, onw_row, states, dog, *, heads, name, tb=512, exchange=()):
    t = proj.shape[0]
    tb = min(tb, t)
    nb, cpb = t // tb, tb // CHUNK
    _, col, halo, convw, vec, abspec, states_spec = _gdn_specs(heads, tb, True, nb)
    n_conv = conv_w.shape[1]

    def body(q_ref, k_ref, v_ref, qh_ref, kh_ref, vh_ref, z_ref, ab_ref, wq_ref, wk_ref, wv_ref, alog_ref, dtb_ref, onw_ref,
             st_ref, dog_ref, dq_ref, dk_ref, dv_ref, dz_ref, dab_ref, dconv_ref, dalog_ref, ddtb_ref, donw_ref,
             dstate_scr, x_scr, carry_scr, *scr):
        op_scr, dop_scr, dstates_scr = scr[:3], scr[3:6], scr[6]
        i, h = pl.program_id(0), pl.program_id(1)
        first_block = i == nb - 1
        for n, (ref, href) in enumerate(((q_ref, qh_ref), (k_ref, kh_ref), (v_ref, vh_ref))):
            x_scr[n, 0:HALO, :] = jnp.where(first_block, 0.0, href[...])
            x_scr[n, HALO:HALO + tb, :] = ref[...]

        @pl.when(jnp.logical_and(i == 0, h == 0))
        def _():
            dconv_ref[...] = jnp.zeros_like(dconv_ref)
            dalog_ref[...] = jnp.zeros_like(dalog_ref)
            ddtb_ref[...] = jnp.zeros_like(ddtb_ref)
            donw_ref[...] = jnp.zeros_like(donw_ref)

        @pl.when(h == 0)
        def _():
            dab_ref[...] = jnp.zeros_like(dab_ref)

        @pl.when(i == 0)
        def _():
            dstate_scr[h] = jnp.zeros((HEAD_DIM, HEAD_DIM), F32)
            carry_scr[h] = jnp.zeros((3, HALO, HEAD_DIM), F32)

        sel_a, sel_b = _head_lane(h), _head_lane(h, heads)
        alog, dtb = _pick(sel_a, alog_ref[...]), _pick(sel_a, dtb_ref[...])
        abv = ab_ref[...]
        (*scan, read, out), vjp_intra = jax.vjp(
            _gdn_intra, x_scr[0], x_scr[1], x_scr[2], _pick(sel_a, abv), _pick(sel_b, abv), wq_ref[...], wk_ref[...],
            wv_ref[...], alog, dtb)
        for s, val in zip(op_scr, scan):
            s[...] = val.astype(s.dtype)
        blocked = lambda ref: ref[...].reshape(cpb, CHUNK, HEAD_DIM)
        _, vjp_outputs = jax.vjp(_gdn_outputs, st_ref[0], read, out, blocked(z_ref), onw_ref[...])
        dstates_scr[...], dread, dout, dz, donw = vjp_outputs(blocked(dog_ref))
        dz_ref[...] = dz.reshape(tb, HEAD_DIM).astype(BF16)

        def chunk(i_rev, dstate):
            c = cpb - 1 - i_rev
            _, vjp = jax.vjp(_gdn_scan_step, st_ref[0, c], *[s[c].astype(F32) for s in op_scr])
            dstate, *grads = vjp(dstate)
            for s, val in zip(dop_scr, grads):
                s[c] = val
            return dstate + dstates_scr[c]

        dstate_scr[h] = lax.fori_loop(0, cpb, chunk, dstate_scr[h])
        dqx, dkx, dvx, da, db, dwq, dwk, dwv, dalog, ddtb = vjp_intra((*[s[...] for s in dop_scr], dread, dout))
        dab_ref[...] += jnp.where(sel_a, da, 0.0) + jnp.where(sel_b, db, 0.0)
        for n, (dref, dx, dw) in enumerate(((dq_ref, dqx, dwq), (dk_ref, dkx, dwk), (dv_ref, dvx, dwv))):
            x_scr[n] = dx
            x_scr[n, tb:tb + HALO, :] += carry_scr[h, n]
            carry_scr[h, n] = x_scr[n, 0:HALO, :]
            dref[...] = x_scr[n, HALO:HALO + tb, :].astype(BF16)
            lanes = pl.ds(pl.multiple_of((n * heads + h) * HEAD_DIM, HEAD_DIM), HEAD_DIM)
            dconv_ref[:, lanes] += dw
        dalog_ref[...] += jnp.where(sel_a, dalog, 0.0)
        ddtb_ref[...] += jnp.where(sel_a, ddtb, 0.0)
        donw_ref[...] += donw

    out_col = pl.BlockSpec((tb, HEAD_DIM), lambda i, h: (nb - 1 - i, h))
    dog_spec = pl.BlockSpec((tb, HEAD_DIM), lambda i, h: (nb - 1 - i, h))
    col_shape = jax.ShapeDtypeStruct((t, heads * HEAD_DIM), BF16)
    row_shape = jax.ShapeDtypeStruct((1, LANES), F32)
    n_x = len(exchange)
    outs = pl.pallas_call(
        _with_exchange(body, 16, 9, False, n_x, (nb, heads)),
        name=name,
        grid=(nb, heads),
        in_specs=[col(0), col(1), col(2), halo(0), halo(1), halo(2), col(3), abspec, convw(0), convw(1), convw(2), vec, vec, vec,
                  states_spec, dog_spec] + [_ANY] * n_x,
        out_specs=[out_col, out_col, out_col, out_col, abspec,
                   pl.BlockSpec((CONV_K, n_conv), lambda i, h: (0, 0)), vec, vec, vec] + [_ANY] * n_x,
        out_shape=[col_shape, col_shape, col_shape, col_shape, jax.ShapeDtypeStruct((t, LANES), F32),
                   jax.ShapeDtypeStruct((CONV_K, n_conv), F32), row_shape, row_shape, row_shape] + _chip_shapes(False, exchange),
        scratch_shapes=[pltpu.VMEM((heads, HEAD_DIM, HEAD_DIM), F32), pltpu.VMEM((3, HALO + tb, HEAD_DIM), F32),
                        pltpu.VMEM((heads, 3, HALO, HEAD_DIM), F32)] + _scan_scratch(cpb, BF16) + _scan_scratch(cpb, F32)
        + [pltpu.VMEM((cpb, HEAD_DIM, HEAD_DIM), F32)]
        + (_chip_scratch(n_x) if n_x else []),
        compiler_params=_params("arbitrary", "arbitrary"),
    )(proj, proj, proj, proj, proj, proj, proj, ab, conv_w, conv_w, conv_w, alog_row, dtb_row, onw_row, states, dog, *exchange)
    return (*outs[:9], outs[9:])


BAND = (LEFT_CHUNKS + 1) * CHUNK
PAD = LEFT_CHUNKS * CHUNK
GROUP = 2
ROWS = GROUP * CHUNK
WIN = (LEFT_CHUNKS + GROUP) * CHUNK
DIAGS = WIN + ROWS - 1
NEAR = PAD + ROWS - 1 - REL_CLIP
assert 0 < NEAR < DIAGS and WIN - PAD - 1 <= REL_CLIP and WIN % LANES == 0


def _band_bias(rel_bias):
    heads = rel_bias.shape[0]
    far = jnp.broadcast_to(rel_bias[:, 2 * REL_CLIP:], (heads, NEAR + 1))
    near = rel_bias[:, 2 * REL_CLIP + NEAR + 1 - DIAGS:2 * REL_CLIP][:, ::-1]
    diag = jnp.concatenate([far, near], axis=1)
    return jnp.stack([diag[:, ROWS - 1 - r:ROWS - 1 - r + WIN] for r in range(ROWS)], axis=1)


def _band_bias_grad(dbias):
    heads = dbias.shape[0]
    diag = sum(jnp.pad(dbias[:, r, :], ((0, 0), (ROWS - 1 - r, r))) for r in range(ROWS))
    far = jnp.sum(diag[:, :NEAR + 1], axis=1, keepdims=True)
    near = diag[:, NEAR + 1:][:, ::-1]
    unused = jnp.zeros((heads, 2 * REL_CLIP - near.shape[1]), F32)
    return jnp.concatenate([unused, near, far], axis=1)


def _attn_groups(q_pre, z, kn, v, bias, qnw, start):
    n = q_pre.shape[0]
    q = _rms(q_pre, qnw)
    s = _bdot(q, kn, _BNT) * (HEAD_DIM ** -0.5) + bias
    group = lax.broadcasted_iota(jnp.int32, (n, ROWS, WIN), 0)
    row = lax.broadcasted_iota(jnp.int32, (n, ROWS, WIN), 1)
    key = lax.broadcasted_iota(jnp.int32, (n, ROWS, WIN), 2)
    band_start = jnp.bitwise_and(row, -CHUNK)
    in_sequence = key >= PAD - (start + group * ROWS)
    valid = jnp.logical_and(jnp.logical_and(key >= band_start, key < band_start + BAND), in_sequence)
    s = jnp.where(valid, s, -1e30)
    p = jnp.exp(s - jnp.max(s, axis=-1, keepdims=True))
    p = p / jnp.sum(p, axis=-1, keepdims=True)
    return _bdot(p, v, _BNN) * _silu(z)


def _attn_specs(heads, tb, t):
    def col(group):
        return pl.BlockSpec((tb, HEAD_DIM), lambda h, i: (i, group * heads + h))

    def full(group):
        return pl.BlockSpec((t, HEAD_DIM), lambda h, i: (0, group * heads + h))

    bias = pl.BlockSpec((1, ROWS, WIN), lambda h, i: (h, 0, 0))
    vec = pl.BlockSpec((1, HEAD_DIM), lambda h, i: (0, 0))
    return col, full, bias, vec


def _attn_windows(scr, block_start, n):
    return jnp.stack([scr[pl.ds(pl.multiple_of(block_start + g * ROWS, ROWS), WIN), :] for g in range(n)])


def _attn_fill(k_ref, v_ref, knw_ref, kn_scr, v_scr, t):
    kn_scr[0:PAD, :] = jnp.zeros((PAD, HEAD_DIM), BF16)
    v_scr[0:PAD, :] = jnp.zeros((PAD, HEAD_DIM), BF16)
    step = min(512, t)

    def fill(j, _):
        rows = pl.ds(pl.multiple_of(j * step, step), step)
        prows = pl.ds(pl.multiple_of(PAD + j * step, CHUNK), step)
        kn_scr[prows, :] = _rms(k_ref[rows, :], knw_ref[...]).astype(BF16)
        v_scr[prows, :] = v_ref[rows, :].astype(BF16)
        return 0

    lax.fori_loop(0, t // step, fill, 0)


def _attn_fwd(proj, bias, qnw_row, knw_row, *, heads, name, tb=512):
    t = proj.shape[0]
    tb = min(tb, t)
    nb, ng = t // tb, tb // ROWS
    col, full, bias_spec, vec = _attn_specs(heads, tb, t)

    def body(q_ref, k_ref, v_ref, z_ref, bias_ref, qnw_ref, knw_ref, og_ref, kn_scr, v_scr):
        i = pl.program_id(1)

        @pl.when(i == 0)
        def _():
            _attn_fill(k_ref, v_ref, knw_ref, kn_scr, v_scr, t)

        start = i * tb
        og = _attn_groups(q_ref[...].reshape(ng, ROWS, HEAD_DIM), z_ref[...].reshape(ng, ROWS, HEAD_DIM),
                          _attn_windows(kn_scr, start, ng), _attn_windows(v_scr, start, ng), bias_ref[0], qnw_ref[...], start)
        og_ref[...] = og.reshape(tb, HEAD_DIM).astype(BF16)

    return pl.pallas_call(
        body,
        name=name,
        grid=(heads, nb),
        in_specs=[col(0), full(1), full(2), col(3), bias_spec, vec, vec],
        out_specs=pl.BlockSpec((tb, HEAD_DIM), lambda h, i: (i, h)),
        out_shape=jax.ShapeDtypeStruct((t, heads * HEAD_DIM), BF16),
        scratch_shapes=[pltpu.VMEM((PAD + t, HEAD_DIM), BF16), pltpu.VMEM((PAD + t, HEAD_DIM), BF16)],
        compiler_params=_params("arbitrary", "arbitrary"),
    )(proj, proj, proj, proj, bias, qnw_row, knw_row)


def _attn_bwd(proj, bias, qnw_row, knw_row, dog, *, heads, name, tb=512):
    t = proj.shape[0]
    tb = min(tb, t)
    nb, ng = t // tb, tb // ROWS
    col, full, bias_spec, vec = _attn_specs(heads, tb, t)

    def body(q_ref, k_ref, v_ref, z_ref, bias_ref, qnw_ref, knw_ref, dog_ref,
             dq_ref, dk_ref, dv_ref, dz_ref, dbias_ref, dqnw_ref, dknw_ref, kn_scr, v_scr, dkn_scr, dv_scr):
        i = pl.program_id(1)

        @pl.when(i == 0)
        def _():
            _attn_fill(k_ref, v_ref, knw_ref, kn_scr, v_scr, t)
            dkn_scr[...] = jnp.zeros_like(dkn_scr)
            dv_scr[...] = jnp.zeros_like(dv_scr)
            dbias_ref[...] = jnp.zeros_like(dbias_ref)
            dqnw_ref[...] = jnp.zeros_like(dqnw_ref)

        start = i * tb
        _, vjp = jax.vjp(
            lambda q_pre, z, kn, v, bias, qnw: _attn_groups(q_pre, z, kn, v, bias, qnw, start),
            q_ref[...].reshape(ng, ROWS, HEAD_DIM), z_ref[...].reshape(ng, ROWS, HEAD_DIM),
            _attn_windows(kn_scr, start, ng).astype(F32), _attn_windows(v_scr, start, ng).astype(F32), bias_ref[0], qnw_ref[...])
        dq, dz, dkn, dv, dbias, dqnw = vjp(dog_ref[...].reshape(ng, ROWS, HEAD_DIM))
        dq_ref[...] = dq.reshape(tb, HEAD_DIM).astype(BF16)
        dz_ref[...] = dz.reshape(tb, HEAD_DIM).astype(BF16)
        for g in range(ng):
            window = pl.ds(pl.multiple_of(start + g * ROWS, ROWS), WIN)
            dkn_scr[window, :] += dkn[g]
            dv_scr[window, :] += dv[g]
        dbias_ref[0] += dbias
        dqnw_ref[0] += dqnw

        @pl.when(i == nb - 1)
        def _():
            step = min(512, t)

            def finish(j, dknw):
                rows = pl.ds(pl.multiple_of(j * step, step), step)
                prows = pl.ds(pl.multiple_of(PAD + j * step, CHUNK), step)
                _, vjp = jax.vjp(_rms, k_ref[rows, :], knw_ref[...])
                dk, dw = vjp(dkn_scr[prows, :])
                dk_ref[rows, :] = dk.astype(BF16)
                dv_ref[rows, :] = dv_scr[prows, :].astype(BF16)
                return dknw + dw

            dknw_ref[0] = lax.fori_loop(0, t // step, finish, jnp.zeros((1, HEAD_DIM), F32))

    out_col = pl.BlockSpec((tb, HEAD_DIM), lambda h, i: (i, h))
    out_full = pl.BlockSpec((t, HEAD_DIM), lambda h, i: (0, h))
    head_vec = pl.BlockSpec((1, 1, HEAD_DIM), lambda h, i: (h, 0, 0))
    col_shape = jax.ShapeDtypeStruct((t, heads * HEAD_DIM), BF16)
    vec_shape = jax.ShapeDtypeStruct((heads, 1, HEAD_DIM), F32)
    return pl.pallas_call(
        body,
        name=name,
        grid=(heads, nb),
        in_specs=[col(0), full(1), full(2), col(3), bias_spec, vec, vec, pl.BlockSpec((tb, HEAD_DIM), lambda h, i: (i, h))],
        out_specs=[out_col, out_full, out_full, out_col, bias_spec, head_vec, head_vec],
        out_shape=[col_shape, col_shape, col_shape, col_shape, jax.ShapeDtypeStruct((heads, ROWS, WIN), F32),
                   vec_shape, vec_shape],
        scratch_shapes=[pltpu.VMEM((PAD + t, HEAD_DIM), BF16), pltpu.VMEM((PAD + t, HEAD_DIM), BF16),
                        pltpu.VMEM((PAD + t, HEAD_DIM), F32), pltpu.VMEM((PAD + t, HEAD_DIM), F32)],
        compiler_params=_params("arbitrary", "arbitrary"),
    )(proj, proj, proj, proj, bias, qnw_row, knw_row, dog)


def _lane_row(v):
    v = v.reshape(1, -1)
    return jnp.pad(v, ((0, 0), (0, LANES - v.shape[1])))


def _local_step(x, target, norm_w, wa_in, conv_w, a_log, dt_bias, onw, wa_out, wb_in, qnw, knw, rel_bias, wb_out, *,
                sharded=False):
    ha, hb = a_log.shape[-1], rel_bias.shape[-2]
    na = 4 * ha * HEAD_DIM
    wa_main = wa_in[:, :na]
    wa_ab = jnp.pad(wa_in[:, na:], ((0, 0), (0, LANES - 2 * ha)))
    alog_row, dtb_row, onw_row = _lane_row(a_log), _lane_row(dt_bias), _lane_row(onw)
    qnw_row, knw_row = _lane_row(qnw), _lane_row(knw)
    bias = _band_bias(rel_bias.reshape(hb, -1))

    hn0 = _rmsnorm_fwd(x, norm_w[0:1], name="norm0")
    proj_a = _matmul(hn0, wa_main, name="a_in")
    ab_a = _matmul(hn0, wa_ab, name="a_in_ab")
    og_a, states, got = _gdn_fwd(proj_a, ab_a, conv_w, alog_row, dtb_row, onw_row, heads=ha, name="gdn_fwd",
                                 gather=[wb_in, wa_out, wb_out] if sharded else [])
    if sharded:
        wb_in, wa_out, wb_out = _join_cols(got[0]), got[1].reshape(-1, got[1].shape[-1]), got[2].reshape(-1, got[2].shape[-1])
    h1 = _matmul(og_a, wa_out, residual=x, name="a_out")
    hn1 = _rmsnorm_fwd(h1, norm_w[1:2], name="norm1")
    proj_b = _matmul(hn1, wb_in, name="b_in")
    og_b = _attn_fwd(proj_b, bias, qnw_row, knw_row, heads=hb, name="attn_fwd")
    h2 = _matmul(og_b, wb_out, residual=h1, name="b_out")
    loss, dh2, dh2_b = _loss_head(h2, target, name="loss_head")

    dog_b = _matmul(dh2_b, wb_out, trans_b=True, name="d_b_out_x")
    dwb_out = _matmul(og_b.T, dh2_b, name="d_b_out_w")
    dq, dk, dv, dz, dbias, dqnw, dknw = _attn_bwd(proj_b, bias, qnw_row, knw_row, dog_b, heads=hb, name="attn_bwd")
    dproj_b = jnp.concatenate([dq, dk, dv, dz], axis=1)
    dhn1 = _matmul(dproj_b, wb_in, trans_b=True, name="d_b_in_x")
    dwb_in = _matmul(hn1.T, dproj_b, name="d_b_in_w")
    dh1, dh1_b, dnw1 = _rmsnorm_bwd(h1, norm_w[1:2], dhn1, dh2, name="d_norm1")

    dog_a = _matmul(dh1_b, wa_out, trans_b=True, name="d_a_out_x")
    dwa_out = _matmul(og_a.T, dh1_b, name="d_a_out_w")
    early = [_split_cols(dwb_in), _split_rows(dwa_out), _split_rows(dwb_out)] if sharded else []
    dq, dk, dv, dz, dab, dconv, dalog, ddtb, donw, landed = _gdn_bwd(
        proj_a, ab_a, conv_w, alog_row, dtb_row, onw_row, states, dog_a, heads=ha, name="gdn_bwd",
        exchange=[s.astype(BF16) for s in early])
    if sharded:
        dwb_in, dwa_out, dwb_out = landed
    dproj_a = jnp.concatenate([dq, dk, dv, dz], axis=1)
    dab_b = dab.astype(BF16)
    dhn0 = _matmul(dproj_a, wa_main, trans_b=True, name="d_a_in_x")
    dhn0 = _matmul(dab_b, wa_ab, trans_b=True, residual=dhn0, name="d_a_in_ab_x")
    hn0_t = hn0.T
    dwa_in = jnp.concatenate(
        [_matmul(hn0_t, dproj_a, name="d_a_in_w"), _matmul(hn0_t, dab_b, name="d_a_in_ab_w")[:, :2 * ha]], axis=1)
    dx, _, dnw0 = _rmsnorm_bwd(x, norm_w[0:1], dhn0, dh1, name="d_norm0")

    drel = _band_bias_grad(dbias)
    grads = dict(
        norm_w=jnp.concatenate([dnw0, dnw1], axis=0), a_w_in=dwa_in, a_conv_w=dconv, a_a_log=dalog[:, :ha],
        a_dt_bias=ddtb[:, :ha], a_out_norm_w=donw, a_w_out=dwa_out, b_w_in=dwb_in, b_q_norm_w=jnp.sum(dqnw, axis=0),
        b_k_norm_w=jnp.sum(dknw, axis=0), b_rel_bias=drel[None], b_w_out=dwb_out)
    return loss, dx, grads


_ANY = pl.BlockSpec(memory_space=pl.ANY)
_CHIP_FLIPS = ((1, 0), (0, 1), (1, 1))


def _place():
    x, y, c = lax.axis_index("x"), lax.axis_index("y"), lax.axis_index("c")
    return x, y, c


def _flip(v, bit):
    return 1 - v if bit else v


def _remote(src, dst, send_sem, recv_sem, peer):
    return pltpu.make_async_remote_copy(src_ref=src, dst_ref=dst, send_sem=send_sem, recv_sem=recv_sem, device_id=peer,
                                        device_id_type=MESH)


def _comm_call(body, arrays, out_shapes, n_remote, n_local, name):
    scratch = [pltpu.SemaphoreType.DMA((n_remote,)), pltpu.SemaphoreType.DMA((n_remote,))]
    if n_local:
        scratch.append(pltpu.SemaphoreType.DMA((n_local,)))
    return pl.pallas_call(
        body, name=name, in_specs=[_ANY] * len(arrays), out_specs=[_ANY] * len(out_shapes), out_shape=out_shapes,
        scratch_shapes=scratch)(*arrays)


def _chip_scratch(n):
    return [pltpu.SemaphoreType.DMA((3 * n,)), pltpu.SemaphoreType.DMA((3 * n,)), pltpu.SemaphoreType.DMA((n,))]


def _chip_shapes(gather, arrays):
    return [jax.ShapeDtypeStruct(((N_CHIPS,) + s.shape) if gather else s.shape, s.dtype) for s in arrays]


def _chip_traffic(gather, ins, outs, sems):
    send_sems, recv_sems, local_sems = sems
    x, y, c = _place()
    mine = 2 * x + y
    local, remote, landing = [], [], []
    for a in range(len(ins)):
        local.append(pltpu.make_async_copy(ins[a] if gather else ins[a].at[mine], outs[a].at[mine], local_sems.at[a]))
        for k, (fx, fy) in enumerate(_CHIP_FLIPS):
            peer = (_flip(x, fx), _flip(y, fy), c)
            theirs = 2 * peer[0] + peer[1]
            src = ins[a] if gather else ins[a].at[theirs]
            pair = send_sems.at[3 * a + k], recv_sems.at[3 * a + k]
            remote.append(_remote(src, outs[a].at[mine], *pair, peer))
            landing.append(_remote(src, outs[a].at[theirs], *pair, peer))
    return local + remote, (local, landing, remote)


def _start(traffic):
    for cp in traffic[0]:
        cp.start()


def _finish(traffic):
    local, landing, remote = traffic[1]
    for cp in local:
        cp.wait()
    for cp in landing:
        cp.wait_recv()
    for cp in remote:
        cp.wait_send()


def _with_exchange(compute, n_in, n_out, gather, n_x, grid):
    if not n_x:
        return compute

    def body(*refs):
        ins, x_in = refs[:n_in], refs[n_in:n_in + n_x]
        outs, x_out = refs[n_in + n_x:n_in + n_x + n_out], refs[n_in + n_x + n_out:n_in + 2 * n_x + n_out]
        scratch, sems = refs[n_in + 2 * n_x + n_out:-3], refs[-3:]
        traffic = _chip_traffic(gather, x_in, x_out, sems)
        first = functools.reduce(jnp.logical_and, [pl.program_id(d) == 0 for d in range(len(grid))])
        last = functools.reduce(jnp.logical_and, [pl.program_id(d) == grid[d] - 1 for d in range(len(grid))])

        @pl.when(first)
        def _():
            _start(traffic)

        compute(*ins, *outs, *scratch)

        @pl.when(last)
        def _():
            _finish(traffic)

    return body


def _chip_call(gather, arrays, *, name):
    n = len(arrays)

    def body(*refs):
        traffic = _chip_traffic(gather, refs[:n], refs[n:2 * n], refs[2 * n:])
        _start(traffic)
        _finish(traffic)

    return pl.pallas_call(
        body, name=name, in_specs=[_ANY] * n, out_specs=[_ANY] * n, out_shape=_chip_shapes(gather, arrays),
        scratch_shapes=_chip_scratch(n))(*arrays)


def _swap_pair(arrays, *, name):
    n = len(arrays)

    def body(*refs):
        ins, outs, (send_sems, recv_sems) = refs[:n], refs[n:2 * n], refs[2 * n:]
        x, y, c = _place()
        copies = [_remote(ins[a], outs[a], send_sems.at[a], recv_sems.at[a], (x, y, 1 - c)) for a in range(n)]
        for cp in copies:
            cp.start()
        for cp in copies:
            cp.wait_recv()
        for cp in copies:
            cp.wait_send()

    shapes = [jax.ShapeDtypeStruct(s.shape, s.dtype) for s in arrays]
    return _comm_call(body, arrays, shapes, n, 0, name)


def _gather_all(tile, *, name):
    def body(in_ref, out_ref, send_sems, recv_sems, local_sems):
        x, y, c = _place()
        mine = 4 * x + 2 * y + c
        local = pltpu.make_async_copy(in_ref, out_ref.at[mine], local_sems.at[0])
        remote, landing = [], []
        for k in range(1, N_DEV):
            peer = (_flip(x, k & 4), _flip(y, k & 2), _flip(c, k & 1))
            sems = send_sems.at[k - 1], recv_sems.at[k - 1]
            remote.append(_remote(in_ref, out_ref.at[mine], *sems, peer))
            landing.append(_remote(in_ref, out_ref.at[4 * peer[0] + 2 * peer[1] + peer[2]], *sems, peer))
        for cp in [local] + remote:
            cp.start()
        local.wait()
        for cp in landing:
            cp.wait_recv()
        for cp in remote:
            cp.wait_send()

    return _comm_call(body, [tile], [jax.ShapeDtypeStruct((N_DEV,) + tile.shape, tile.dtype)], N_DEV - 1, 1, name)[0]


def _sum_slots(slabs, *, name, tr=128):
    s, r, c = slabs.shape
    tr = min(tr, r)

    def body(in_ref, o_ref):
        acc = in_ref[0].astype(F32)
        for j in range(1, s):
            acc = acc + in_ref[j].astype(F32)
        o_ref[...] = acc

    return pl.pallas_call(
        body, name=name, grid=(r // tr,),
        in_specs=[pl.BlockSpec((s, tr, c), lambda i: (0, i, 0))], out_specs=pl.BlockSpec((tr, c), lambda i: (i, 0)),
        out_shape=jax.ShapeDtypeStruct((r, c), F32), compiler_params=_params("parallel"))(slabs)


def _adamw_math(w, g, m, v):
    m = ADAM_B1 * m + (1.0 - ADAM_B1) * g
    v = ADAM_B2 * v + (1.0 - ADAM_B2) * (g * g)
    m_hat = m / (1.0 - ADAM_B1 ** ADAM_STEP)
    v_hat = v / (1.0 - ADAM_B2 ** ADAM_STEP)
    delta = -ADAM_LR * (m_hat / (jnp.sqrt(v_hat) + ADAM_EPS) + ADAM_WD * w)
    return delta, m, v


def _adamw(w, m, v, parts, *, name, tr=128):
    r, c = w.shape
    tr = min(tr, r)
    s = len(parts)

    def body(w_ref, m_ref, v_ref, *refs):
        g_ref, d_ref, nm_ref, nv_ref = refs[s:]
        g = refs[0][...]
        for p_ref in refs[1:s]:
            g = g + p_ref[...]
        g_ref[...] = g
        d_ref[...], nm_ref[...], nv_ref[...] = _adamw_math(w_ref[...], g, m_ref[...], v_ref[...])

    blk = pl.BlockSpec((tr, c), lambda i: (i, 0))
    shape = jax.ShapeDtypeStruct((r, c), F32)
    return pl.pallas_call(
        body, name=name, grid=(r // tr,), in_specs=[blk] * (3 + s), out_specs=[blk] * 4, out_shape=[shape] * 4,
        compiler_params=_params("parallel"))(w, m, v, *parts)


_BIG = ("a_w_in", "b_w_in", "a_w_out", "b_w_out", "a_conv_w")
_SMALL = ("norm_w", "a_a_log", "a_dt_bias", "a_out_norm_w", "b_q_norm_w", "b_k_norm_w", "b_rel_bias")
_ORDER = ("norm_w", "a_w_in", "a_conv_w", "a_a_log", "a_dt_bias", "a_out_norm_w", "a_w_out", "b_w_in", "b_q_norm_w",
          "b_k_norm_w", "b_rel_bias", "b_w_out")


def _join_cols(g):
    return jnp.transpose(g, (1, 0, 2)).reshape(g.shape[1], -1)


def _split_cols(g):
    return jnp.transpose(g.reshape(g.shape[0], N_CHIPS, -1), (1, 0, 2))


def _split_rows(g):
    return g.reshape(N_CHIPS, -1, g.shape[-1])


def _pack(d):
    flat = jnp.concatenate([d[n].reshape(-1) for n in _SMALL])
    return jnp.pad(flat, (0, -flat.shape[0] % LANES)).reshape(1, -1)


def _unpack(row, like):
    out, at = {}, 0
    for n in _SMALL:
        size = like[n].size
        out[n] = row[0, at:at + size].reshape(like[n].shape)
        at += size
    return out


def kernel(x, norm_w, a_w_in, a_conv_w, a_a_log, a_dt_bias, a_out_norm_w, a_w_out, b_w_in, b_q_norm_w, b_k_norm_w, b_rel_bias, b_w_out, loss_target, m_norm_w, m_a_w_in, m_a_conv_w, m_a_a_log, m_a_dt_bias, m_a_out_norm_w, m_a_w_out, m_b_w_in, m_b_q_norm_w, m_b_k_norm_w, m_b_rel_bias, m_b_w_out, v_norm_w, v_a_w_in, v_a_conv_w, v_a_a_log, v_a_dt_bias, v_a_out_norm_w, v_a_w_out, v_b_w_in, v_b_q_norm_w, v_b_k_norm_w, v_b_rel_bias, v_b_w_out):
    w = dict(norm_w=norm_w, a_w_in=a_w_in, a_conv_w=a_conv_w, a_a_log=a_a_log, a_dt_bias=a_dt_bias,
             a_out_norm_w=a_out_norm_w, a_w_out=a_w_out, b_w_in=b_w_in, b_q_norm_w=b_q_norm_w, b_k_norm_w=b_k_norm_w,
             b_rel_bias=b_rel_bias, b_w_out=b_w_out)
    m = dict(norm_w=m_norm_w, a_w_in=m_a_w_in, a_conv_w=m_a_conv_w, a_a_log=m_a_a_log, a_dt_bias=m_a_dt_bias,
             a_out_norm_w=m_a_out_norm_w, a_w_out=m_a_w_out, b_w_in=m_b_w_in, b_q_norm_w=m_b_q_norm_w,
             b_k_norm_w=m_b_k_norm_w, b_rel_bias=m_b_rel_bias, b_w_out=m_b_w_out)
    v = dict(norm_w=v_norm_w, a_w_in=v_a_w_in, a_conv_w=v_a_conv_w, a_a_log=v_a_a_log, a_dt_bias=v_a_dt_bias,
             a_out_norm_w=v_a_out_norm_w, a_w_out=v_a_w_out, b_w_in=v_b_w_in, b_q_norm_w=v_b_q_norm_w,
             b_k_norm_w=v_b_k_norm_w, b_rel_bias=v_b_rel_bias, b_w_out=v_b_w_out)

    wa_in, conv = _chip_call(True, [a_w_in[0].astype(BF16), a_conv_w[0]], name="gather_a_in")
    loss, dx, grads = _local_step(
        x[0], loss_target[0], norm_w, _join_cols(wa_in), _join_cols(conv), a_a_log, a_dt_bias, a_out_norm_w,
        a_w_out[0].astype(BF16), b_w_in[0].astype(BF16), b_q_norm_w, b_k_norm_w, b_rel_bias, b_w_out[0].astype(BF16),
        sharded=True)
    loss = lax.psum(loss, ("x", "y", "c"))

    landed = dict(grads)
    landed["a_w_in"], landed["a_conv_w"] = _chip_call(
        False, [_split_cols(grads["a_w_in"]).astype(BF16), _split_cols(grads["a_conv_w"])], name="scatter_a_in")
    mine = [_sum_slots(landed[n], name=f"chip_sum_{n}") for n in _BIG]
    theirs = _swap_pair(mine, name="pair_grads")
    out = {}
    for n, p, q in zip(_BIG, mine, theirs):
        out[n] = [r[None] for r in _adamw(w[n][0], m[n][0], v[n][0], [p, q], name=f"adamw_{n}")]

    row = _pack(grads)
    tiles = _gather_all(jnp.broadcast_to(row, (8, row.shape[1])), name="gather_small_grads")
    res = _adamw(_pack(w), _pack(m), _pack(v), [tiles[d, 0:1, :] for d in range(N_DEV)], name="adamw_small")
    unpacked = [_unpack(r, w) for r in res]
    for n in _SMALL:
        out[n] = [u[n] for u in unpacked]

    return (loss, dx[None], *[out[n][0] for n in _ORDER], *[out[n][1] for n in _ORDER], *[out[n][2] for n in _ORDER],
            *[out[n][3] for n in _ORDER])
```

```python
import functools

import numpy as np
import jax
import jax.numpy as jnp
from jax import lax
from jax.experimental import pallas as pl
from jax.experimental.pallas import tpu as pltpu

F32 = jnp.float32
BF16 = jnp.bfloat16

CHUNK = 64
HEAD_DIM = 128
LEFT_CHUNKS = 8
REL_CLIP = 256
CONV_K = 4
EPS = 1e-6
HALO = 8

ADAM_LR = 0.001
ADAM_B1 = 0.9
ADAM_B2 = 0.999
ADAM_EPS = 1e-08
ADAM_WD = 0.01
ADAM_STEP = 10

LANES = 128
N_CHIPS = 4
N_DEV = 8
VMEM_LIMIT_BYTES = 56 * 1024 * 1024
MESH = pl.DeviceIdType.MESH
HIGHEST = lax.Precision.HIGHEST


def _params(*sem):
    return pltpu.CompilerParams(dimension_semantics=sem, vmem_limit_bytes=VMEM_LIMIT_BYTES)


def _dot(a, b, dims=(((1,), (0,)), ((), ())), precision=None):
    return lax.dot_general(a, b, dims, precision=precision, preferred_element_type=F32)


_NT = (((1,), (1,)), ((), ()))
_TN = (((0,), (0,)), ((), ()))


def _bdot(a, b, dims=(((1,), (0,)), ((), ()))):
    return _dot(a.astype(BF16), b.astype(BF16), dims)


def _fdot(a, b, dims=(((1,), (0,)), ((), ()))):
    return _dot(a, b, dims, precision=lax.Precision.HIGH)


def _silu(x):
    return x * jax.nn.sigmoid(x)


def _matmul(a, b, *, name, trans_b=False, residual=None, out_dtype=F32, tm=1024, tn=1024, tk=2048, exchange=()):
    m, k = a.shape
    n = b.shape[0] if trans_b else b.shape[1]
    tm, tn, tk = min(tm, m), min(tn, n), min(tk, k)
    assert m % tm == 0 and n % tn == 0 and k % tk == 0, (a.shape, b.shape, tm, tn, tk)
    nk = k // tk
    dims = _NT if trans_b else (((1,), (0,)), ((), ()))

    def body(*refs):
        if residual is None:
            a_ref, b_ref, o_ref, acc_ref = refs
            r_ref = None
        else:
            a_ref, b_ref, r_ref, o_ref, acc_ref = refs
        kk = pl.program_id(2)

        @pl.when(kk == 0)
        def _():
            acc_ref[...] = jnp.zeros_like(acc_ref)

        acc_ref[...] += _dot(a_ref[...], b_ref[...], dims)

        @pl.when(kk == nk - 1)
        def _():
            r = acc_ref[...]
            if r_ref is not None:
                r = r + r_ref[...]
            o_ref[...] = r.astype(o_ref.dtype)

    in_specs = [
        pl.BlockSpec((tm, tk), lambda i, j, kk: (i, kk)),
        pl.BlockSpec((tn, tk), lambda i, j, kk: (j, kk)) if trans_b else pl.BlockSpec((tk, tn), lambda i, j, kk: (kk, j)),
    ]
    args = [a, b]
    if residual is not None:
        in_specs.append(pl.BlockSpec((tm, tn), lambda i, j, kk: (i, j)))
        args.append(residual)
    grid = (m // tm, n // tn, nk)
    n_x = len(exchange)
    out, *landed = pl.pallas_call(
        _with_exchange(body, len(args), 1, False, n_x, grid),
        name=name,
        grid=grid,
        in_specs=in_specs + [_ANY] * n_x,
        out_specs=[pl.BlockSpec((tm, tn), lambda i, j, kk: (i, j))] + [_ANY] * n_x,
        out_shape=[jax.ShapeDtypeStruct((m, n), out_dtype)] + _chip_shapes(False, exchange),
        scratch_shapes=[pltpu.VMEM((tm, tn), F32)] + (_chip_scratch(n_x) if n_x else []),
        compiler_params=_params(*(("arbitrary",) * 3 if n_x else ("parallel", "parallel", "arbitrary"))),
    )(*args, *exchange)
    return (out, landed) if n_x else out


def _rms(x, w):
    return x * lax.rsqrt(jnp.mean(x * x, axis=-1, keepdims=True) + EPS) * w


def _rmsnorm_fwd(x, w_row, *, name, tr=512):
    t, d = x.shape
    tr = min(tr, t)

    def body(x_ref, w_ref, o_ref):
        o_ref[...] = _rms(x_ref[...], w_ref[...]).astype(BF16)

    return pl.pallas_call(
        body,
        name=name,
        grid=(t // tr,),
        in_specs=[pl.BlockSpec((tr, d), lambda i: (i, 0)), pl.BlockSpec((1, d), lambda i: (0, 0))],
        out_specs=pl.BlockSpec((tr, d), lambda i: (i, 0)),
        out_shape=jax.ShapeDtypeStruct((t, d), BF16),
        compiler_params=_params("parallel"),
    )(x, w_row)


def _rmsnorm_bwd(x, w_row, dy, dres, *, name, tr=256):
    t, d = x.shape
    tr = min(tr, t)

    def body(x_ref, w_ref, dy_ref, dres_ref, dx_ref, dxb_ref, dw_ref):
        @pl.when(pl.program_id(0) == 0)
        def _():
            dw_ref[...] = jnp.zeros_like(dw_ref)

        _, vjp = jax.vjp(_rms, x_ref[...], w_ref[...])
        dx, dw = vjp(dy_ref[...])
        dx = dx + dres_ref[...]
        dx_ref[...] = dx
        dxb_ref[...] = dx.astype(BF16)
        dw_ref[...] += dw

    row = pl.BlockSpec((tr, d), lambda i: (i, 0))
    vec = pl.BlockSpec((1, d), lambda i: (0, 0))
    return pl.pallas_call(
        body,
        name=name,
        grid=(t // tr,),
        in_specs=[row, vec, row, row],
        out_specs=[row, row, vec],
        out_shape=[jax.ShapeDtypeStruct((t, d), F32), jax.ShapeDtypeStruct((t, d), BF16), jax.ShapeDtypeStruct((1, d), F32)],
        compiler_params=_params("arbitrary"),
    )(x, w_row, dy, dres)


def _loss_head(h, target, *, name, tr=512):
    t, d = h.shape
    tr = min(tr, t)

    def body(h_ref, t_ref, dh_ref, dhb_ref, part_ref):
        @pl.when(pl.program_id(0) == 0)
        def _():
            part_ref[...] = jnp.zeros_like(part_ref)

        err = h_ref[...] - t_ref[...]
        dh = err * (1.0 / d)
        dh_ref[...] = dh
        dhb_ref[...] = dh.astype(BF16)
        part_ref[...] += jnp.sum(err * err, axis=0, keepdims=True)

    row = pl.BlockSpec((tr, d), lambda i: (i, 0))
    vec = pl.BlockSpec((1, d), lambda i: (0, 0))
    dh, dhb, part = pl.pallas_call(
        body,
        name=name,
        grid=(t // tr,),
        in_specs=[row, row],
        out_specs=[row, row, vec],
        out_shape=[jax.ShapeDtypeStruct((t, d), F32), jax.ShapeDtypeStruct((t, d), BF16), jax.ShapeDtypeStruct((1, d), F32)],
        compiler_params=_params("arbitrary"),
    )(h, target)
    return 0.5 / d * jnp.sum(part), dh, dhb


_BNN = (((2,), (1,)), ((0,), (0,)))
_BNT = (((2,), (2,)), ((0,), (0,)))
_BTN = (((1,), (1,)), ((0,), (0,)))


def _conv_silu(xe, w):
    rows = xe.shape[0] - HALO
    first = HALO - (CONV_K - 1)
    c = w[0:1, :] * xe[first:first + rows, :]
    for j in range(1, CONV_K):
        c = c + w[j:j + 1, :] * xe[first + j:first + j + rows, :]
    return _silu(c)


def _gdn_intra(qx, kx, vx, a, b, wq, wk, wv, alog, dtb):
    n = a.shape[0] // CHUNK
    qt, kt, v = _conv_silu(qx, wq), _conv_silu(kx, wk), _conv_silu(vx, wv)
    q = qt * lax.rsqrt(jnp.sum(qt * qt, axis=-1, keepdims=True) + EPS) * (HEAD_DIM ** -0.5)
    k = kt * lax.rsqrt(jnp.sum(kt * kt, axis=-1, keepdims=True) + EPS)
    lanes = jnp.ones((1, HEAD_DIM), F32)
    beta = jax.nn.sigmoid(b) * lanes
    sp = a + dtb
    g = (-jnp.exp(alog) * (jnp.maximum(sp, 0.0) + jnp.log(1.0 + jnp.exp(-jnp.abs(sp))))) * lanes
    q, k, v, beta, g = (t.reshape(n, CHUNK, HEAD_DIM) for t in (q, k, v, beta, g))

    row = lax.broadcasted_iota(jnp.int32, (n, CHUNK, CHUNK), 1)
    col = lax.broadcasted_iota(jnp.int32, (n, CHUNK, CHUNK), 2)
    tri_incl = row >= col
    tri_strict = row > col
    gc = _fdot(tri_incl.astype(F32), g, _BNN)
    gc_row = _fdot(g[:, :, :CHUNK], (row <= col).astype(F32), _BTN)
    decay = jnp.exp(jnp.where(tri_incl, gc[:, :, :CHUNK] - gc_row, -1e30))
    kb = k * beta
    vb = v * beta
    with_k = _bdot(jnp.concatenate([kb, q], axis=1), k, _BNT)
    neg_l = jnp.where(tri_strict, -(with_k[:, :CHUNK] * decay), 0.0)
    qk = jnp.where(tri_incl, with_k[:, CHUNK:] * decay, 0.0)
    inv = (row == col).astype(F32) + neg_l
    power = _bdot(neg_l, neg_l, _BNN)
    for _ in range(4):
        both = _bdot(jnp.concatenate([inv, power], axis=1), power, _BNN)
        inv, power = inv + both[:, :CHUNK], both[:, CHUNK:]
    inv = inv + _bdot(inv, power, _BNN)
    e = jnp.exp(gc)
    solved = _bdot(inv, jnp.concatenate([kb * e, vb], axis=2), _BNN)
    g_last = gc[:, CHUNK - 1:CHUNK, :]
    k_dec = k * jnp.exp(g_last - gc)
    from_k = _bdot(k_dec, solved, _BTN)
    from_qk = _bdot(qk, solved, _BNN)
    step, add = -from_k[:, :, :HEAD_DIM], from_k[:, :, HEAD_DIM:]
    read, out = q * e - from_qk[:, :, :HEAD_DIM], from_qk[:, :, HEAD_DIM:]
    return step, add, jnp.exp(g_last), read, out


def _gdn_scan_step(state, step, add, decay_last):
    return state * decay_last + _bdot(step, state) + add


def _gdn_outputs(states, read, out, z, onw):
    return _rms(_bdot(read, states, _BNN) + out, onw) * _silu(z)


def _scan_scratch(n, dtype):
    return [pltpu.VMEM((n, HEAD_DIM, HEAD_DIM), dtype), pltpu.VMEM((n, HEAD_DIM, HEAD_DIM), F32), pltpu.VMEM((n, 1, HEAD_DIM), F32)]


def _head_lane(h, offset=0):
    return lax.broadcasted_iota(jnp.int32, (1, LANES), 1) == h + offset


def _pick(mask, x):
    return jnp.sum(jnp.where(mask, x, 0.0), axis=1, keepdims=True)


def _gdn_specs(heads, tb, rev, nb):
    blk = (lambda i: nb - 1 - i) if rev else (lambda i: i)
    hb = tb // HALO

    def col(group):
        return pl.BlockSpec((tb, HEAD_DIM), lambda i, h: (blk(i), group * heads + h))

    def halo(group):
        return pl.BlockSpec((HALO, HEAD_DIM), lambda i, h: (jnp.maximum(blk(i) * hb - 1, 0), group * heads + h))

    def convw(group):
        return pl.BlockSpec((CONV_K, HEAD_DIM), lambda i, h: (0, group * heads + h))

    vec = pl.BlockSpec((1, LANES), lambda i, h: (0, 0))
    ab = pl.BlockSpec((tb, LANES), lambda i, h: (blk(i), 0))
    states = pl.BlockSpec((1, tb // CHUNK, HEAD_DIM, HEAD_DIM), lambda i, h: (h, blk(i), 0, 0))
    return blk, col, halo, convw, vec, ab, states


def _gdn_fwd(proj, ab, conv_w, alog_row, dtb_row, onw_row, *, heads, name, tb=1024, gather=()):
    t = proj.shape[0]
    tb = min(tb, t)
    nb, cpb = t // tb, tb // CHUNK
    _, col, halo, convw, vec, abspec, states = _gdn_specs(heads, tb, False, nb)

    def body(q_ref, k_ref, v_ref, qh_ref, kh_ref, vh_ref, z_ref, ab_ref, wq_ref, wk_ref, wv_ref, alog_ref, dtb_ref, onw_ref,
             og_ref, st_ref, state_scr, x_scr, *op_scr):
        i, h = pl.program_id(0), pl.program_id(1)
        for n, (ref, href) in enumerate(((q_ref, qh_ref), (k_ref, kh_ref), (v_ref, vh_ref))):
            x_scr[n, 0:HALO, :] = jnp.where(i > 0, href[...], 0.0)
            x_scr[n, HALO:HALO + tb, :] = ref[...]

        @pl.when(i == 0)
        def _():
            state_scr[h] = jnp.zeros((HEAD_DIM, HEAD_DIM), F32)

        sel_a, sel_b = _head_lane(h), _head_lane(h, heads)
        alog, dtb = _pick(sel_a, alog_ref[...]), _pick(sel_a, dtb_ref[...])
        abv = ab_ref[...]
        *scan, read, out = _gdn_intra(x_scr[0], x_scr[1], x_scr[2], _pick(sel_a, abv), _pick(sel_b, abv), wq_ref[...],
                                      wk_ref[...], wv_ref[...], alog, dtb)
        for scr, val in zip(op_scr, scan):
            scr[...] = val.astype(scr.dtype)

        def chunk(c, state):
            st_ref[0, c] = state
            return _gdn_scan_step(state, *[scr[c] for scr in op_scr])

        state_scr[h] = lax.fori_loop(0, cpb, chunk, state_scr[h])
        og = _gdn_outputs(st_ref[0], read, out, z_ref[...].reshape(cpb, CHUNK, HEAD_DIM), onw_ref[...])
        og_ref[...] = og.reshape(tb, HEAD_DIM).astype(BF16)

    n_x = len(gather)
    og, st, *gathered = pl.pallas_call(
        _with_exchange(body, 14, 2, True, n_x, (nb, heads)),
        name=name,
        grid=(nb, heads),
        in_specs=[col(0), col(1), col(2), halo(0), halo(1), halo(2), col(3), abspec, convw(0), convw(1), convw(2), vec, vec, vec]
        + [_ANY] * n_x,
        out_specs=[pl.BlockSpec((tb, HEAD_DIM), lambda i, h: (i, h)), states] + [_ANY] * n_x,
        out_shape=[jax.ShapeDtypeStruct((t, heads * HEAD_DIM), BF16),
                   jax.ShapeDtypeStruct((heads, t // CHUNK, HEAD_DIM, HEAD_DIM), F32)] + _chip_shapes(True, gather),
        scratch_shapes=[pltpu.VMEM((heads, HEAD_DIM, HEAD_DIM), F32), pltpu.VMEM((3, HALO + tb, HEAD_DIM), F32)]
        + _scan_scratch(cpb, BF16) + (_chip_scratch(n_x) if n_x else []),
        compiler_params=_params("arbitrary", "arbitrary"),
    )(proj, proj, proj, proj, proj, proj, proj, ab, conv_w, conv_w, conv_w, alog_row, dtb_row, onw_row, *gather)
    return og, st, gathered


def _gdn_bwd(proj, ab, conv_w, alog_row, dtb_row, onw_row, states, dog, *, heads, name, tb=1024, exchange=()):
    t = proj.shape[0]
    tb = min(tb, t)
    nb, cpb = t // tb, tb // CHUNK
    _, col, halo, convw, vec, abspec, states_spec = _gdn_specs(heads, tb, True, nb)
    n_conv = conv_w.shape[1]

    def body(q_ref, k_ref, v_ref, qh_ref, kh_ref, vh_ref, z_ref, ab_ref, wq_ref, wk_ref, wv_ref, alog_ref, dtb_ref, onw_ref,
             st_ref, dog_ref, dq_ref, dk_ref, dv_ref, dz_ref, dab_ref, dconv_ref, dalog_ref, ddtb_ref, donw_ref,
             dstate_scr, x_scr, carry_scr, *scr):
        op_scr, dop_scr, dstates_scr = scr[:3], scr[3:6], scr[6]
        i, h = pl.program_id(0), pl.program_id(1)
        first_block = i == nb - 1
        for n, (ref, href) in enumerate(((q_ref, qh_ref), (k_ref, kh_ref), (v_ref, vh_ref))):
            x_scr[n, 0:HALO, :] = jnp.where(first_block, 0.0, href[...])
            x_scr[n, HALO:HALO + tb, :] = ref[...]

        @pl.when(jnp.logical_and(i == 0, h == 0))
        def _():
            dconv_ref[...] = jnp.zeros_like(dconv_ref)
            dalog_ref[...] = jnp.zeros_like(dalog_ref)
            ddtb_ref[...] = jnp.zeros_like(ddtb_ref)
            donw_ref[...] = jnp.zeros_like(donw_ref)

        @pl.when(h == 0)
        def _():
            dab_ref[...] = jnp.zeros_like(dab_ref)

        @pl.when(i == 0)
        def _():
            dstate_scr[h] = jnp.zeros((HEAD_DIM, HEAD_DIM), F32)
            carry_scr[h] = jnp.zeros((3, HALO, HEAD_DIM), F32)

        sel_a, sel_b = _head_lane(h), _head_lane(h, heads)
        alog, dtb = _pick(sel_a, alog_ref[...]), _pick(sel_a, dtb_ref[...])
        abv = ab_ref[...]
        (*scan, read, out), vjp_intra = jax.vjp(
            _gdn_intra, x_scr[0], x_scr[1], x_scr[2], _pick(sel_a, abv), _pick(sel_b, abv), wq_ref[...], wk_ref[...],
            wv_ref[...], alog, dtb)
        for s, val in zip(op_scr, scan):
            s[...] = val.astype(s.dtype)
        blocked = lambda ref: ref[...].reshape(cpb, CHUNK, HEAD_DIM)
        _, vjp_outputs = jax.vjp(_gdn_outputs, st_ref[0], read, out, blocked(z_ref), onw_ref[...])
        dstates_scr[...], dread, dout, dz, donw = vjp_outputs(blocked(dog_ref))
        dz_ref[...] = dz.reshape(tb, HEAD_DIM).astype(BF16)

        def chunk(i_rev, dstate):
            c = cpb - 1 - i_rev
            _, vjp = jax.vjp(_gdn_scan_step, st_ref[0, c], *[s[c].astype(F32) for s in op_scr])
            dstate, *grads = vjp(dstate)
            for s, val in zip(dop_scr, grads):
                s[c] = val
            return dstate + dstates_scr[c]

        dstate_scr[h] = lax.fori_loop(0, cpb, chunk, dstate_scr[h])
        dqx, dkx, dvx, da, db, dwq, dwk, dwv, dalog, ddtb = vjp_intra((*[s[...] for s in dop_scr], dread, dout))
        dab_ref[...] += jnp.where(sel_a, da, 0.0) + jnp.where(sel_b, db, 0.0)
        for n, (dref, dx, dw) in enumerate(((dq_ref, dqx, dwq), (dk_ref, dkx, dwk), (dv_ref, dvx, dwv))):
            x_scr[n] = dx
            x_scr[n, tb:tb + HALO, :] += carry_scr[h, n]
            carry_scr[h, n] = x_scr[n, 0:HALO, :]
            dref[...] = x_scr[n, HALO:HALO + tb, :].astype(BF16)
            lanes = pl.ds(pl.multiple_of((n * heads + h) * HEAD_DIM, HEAD_DIM), HEAD_DIM)
            dconv_ref[:, lanes] += dw
        dalog_ref[...] += jnp.where(sel_a, dalog, 0.0)
        ddtb_ref[...] += jnp.where(sel_a, ddtb, 0.0)
        donw_ref[...] += donw

    out_col = pl.BlockSpec((tb, HEAD_DIM), lambda i, h: (nb - 1 - i, h))
    dog_spec = pl.BlockSpec((tb, HEAD_DIM), lambda i, h: (nb - 1 - i, h))
    col_shape = jax.ShapeDtypeStruct((t, heads * HEAD_DIM), BF16)
    row_shape = jax.ShapeDtypeStruct((1, LANES), F32)
    n_x = len(exchange)
    outs = pl.pallas_call(
        _with_exchange(body, 16, 9, False, n_x, (nb, heads)),
        name=name,
        grid=(nb, heads),
        in_specs=[col(0), col(1), col(2), halo(0), halo(1), halo(2), col(3), abspec, convw(0), convw(1), convw(2), vec, vec, vec,
                  states_spec, dog_spec] + [_ANY] * n_x,
        out_specs=[out_col, out_col, out_col, out_col, abspec,
                   pl.BlockSpec((CONV_K, n_conv), lambda i, h: (0, 0)), vec, vec, vec] + [_ANY] * n_x,
        out_shape=[col_shape, col_shape, col_shape, col_shape, jax.ShapeDtypeStruct((t, LANES), F32),
                   jax.ShapeDtypeStruct((CONV_K, n_conv), F32), row_shape, row_shape, row_shape] + _chip_shapes(False, exchange),
        scratch_shapes=[pltpu.VMEM((heads, HEAD_DIM, HEAD_DIM), F32), pltpu.VMEM((3, HALO + tb, HEAD_DIM), F32),
                        pltpu.VMEM((heads, 3, HALO, HEAD_DIM), F32)] + _scan_scratch(cpb, BF16) + _scan_scratch(cpb, F32)
        + [pltpu.VMEM((cpb, HEAD_DIM, HEAD_DIM), F32)]
        + (_chip_scratch(n_x) if n_x else []),
        compiler_params=_params("arbitrary", "arbitrary"),
    )(proj, proj, proj, proj, proj, proj, proj, ab, conv_w, conv_w, conv_w, alog_row, dtb_row, onw_row, states, dog, *exchange)
    return (*outs[:9], outs[9:])


BAND = (LEFT_CHUNKS + 1) * CHUNK
PAD = LEFT_CHUNKS * CHUNK
GROUP = 2
ROWS = GROUP * CHUNK
WIN = (LEFT_CHUNKS + GROUP) * CHUNK
DIAGS = WIN + ROWS - 1
NEAR = PAD + ROWS - 1 - REL_CLIP
assert 0 < NEAR < DIAGS and WIN - PAD - 1 <= REL_CLIP and WIN % LANES == 0
ATTN_BLOCK = 512


def _band_bias(rel_bias):
    heads = rel_bias.shape[0]
    far = jnp.broadcast_to(rel_bias[:, 2 * REL_CLIP:], (heads, NEAR + 1))
    near = rel_bias[:, 2 * REL_CLIP + NEAR + 1 - DIAGS:2 * REL_CLIP][:, ::-1]
    diag = jnp.concatenate([far, near], axis=1)
    return jnp.stack([diag[:, ROWS - 1 - r:ROWS - 1 - r + WIN] for r in range(ROWS)], axis=1)


def _band_bias_grad(dbias):
    heads = dbias.shape[0]
    diag = sum(jnp.pad(dbias[:, r, :], ((0, 0), (ROWS - 1 - r, r))) for r in range(ROWS))
    far = jnp.sum(diag[:, :NEAR + 1], axis=1, keepdims=True)
    near = diag[:, NEAR + 1:][:, ::-1]
    unused = jnp.zeros((heads, 2 * REL_CLIP - near.shape[1]), F32)
    return jnp.concatenate([unused, near, far], axis=1)


def _masked_bias(bias, n):
    r = np.arange(ROWS)[:, None]
    key = np.arange(WIN)[None, :]
    band_start = (r // CHUNK) * CHUNK
    in_band = np.logical_and(key >= band_start, key < band_start + BAND)
    in_sequence = key[None] >= PAD - np.arange(n)[:, None, None] * ROWS
    seen = np.stack([np.logical_and(in_band[None], in_sequence), np.broadcast_to(in_band[None], (n, ROWS, WIN))])
    return jnp.where(seen[None], bias[:, None, None], -1e30)


def _attn_groups(q_pre, z, kn, v, bias, qnw):
    q = _rms(q_pre, qnw)
    s = _bdot(q, kn, _BNT) * (HEAD_DIM ** -0.5) + bias
    p = jnp.exp(s - jnp.max(s, axis=-1, keepdims=True))
    p = p / jnp.sum(p, axis=-1, keepdims=True)
    return _bdot(p, v, _BNN) * _silu(z)


def _attn_specs(heads, tb, t):
    def col(group):
        return pl.BlockSpec((tb, HEAD_DIM), lambda h, i: (i, group * heads + h))

    def full(group):
        return pl.BlockSpec((t, HEAD_DIM), lambda h, i: (0, group * heads + h))

    bias = pl.BlockSpec((1, 1, tb // ROWS, ROWS, WIN), lambda h, i: (h, jnp.minimum(i, 1), 0, 0, 0))
    vec = pl.BlockSpec((1, HEAD_DIM), lambda h, i: (0, 0))
    return col, full, bias, vec


def _attn_windows(scr, block_start, n):
    return jnp.stack([scr[pl.ds(pl.multiple_of(block_start + g * ROWS, ROWS), WIN), :] for g in range(n)])


def _attn_fill(k_ref, v_ref, knw_ref, kn_scr, v_scr, t):
    kn_scr[0:PAD, :] = jnp.zeros((PAD, HEAD_DIM), BF16)
    v_scr[0:PAD, :] = jnp.zeros((PAD, HEAD_DIM), BF16)
    step = min(512, t)

    def fill(j, _):
        rows = pl.ds(pl.multiple_of(j * step, step), step)
        prows = pl.ds(pl.multiple_of(PAD + j * step, CHUNK), step)
        kn_scr[prows, :] = _rms(k_ref[rows, :], knw_ref[...]).astype(BF16)
        v_scr[prows, :] = v_ref[rows, :].astype(BF16)
        return 0

    lax.fori_loop(0, t // step, fill, 0)


def _attn_fwd(proj, bias, qnw_row, knw_row, *, heads, name, tb=ATTN_BLOCK):
    t = proj.shape[0]
    tb = min(tb, t)
    nb, ng = t // tb, tb // ROWS
    col, full, bias_spec, vec = _attn_specs(heads, tb, t)

    def body(q_ref, k_ref, v_ref, z_ref, bias_ref, qnw_ref, knw_ref, og_ref, kn_scr, v_scr):
        i = pl.program_id(1)

        @pl.when(i == 0)
        def _():
            _attn_fill(k_ref, v_ref, knw_ref, kn_scr, v_scr, t)

        start = i * tb
        og = _attn_groups(q_ref[...].reshape(ng, ROWS, HEAD_DIM), z_ref[...].reshape(ng, ROWS, HEAD_DIM),
                          _attn_windows(kn_scr, start, ng), _attn_windows(v_scr, start, ng), bias_ref[0, 0], qnw_ref[...])
        og_ref[...] = og.reshape(tb, HEAD_DIM).astype(BF16)

    return pl.pallas_call(
        body,
        name=name,
        grid=(heads, nb),
        in_specs=[col(0), full(1), full(2), col(3), bias_spec, vec, vec],
        out_specs=pl.BlockSpec((tb, HEAD_DIM), lambda h, i: (i, h)),
        out_shape=jax.ShapeDtypeStruct((t, heads * HEAD_DIM), BF16),
        scratch_shapes=[pltpu.VMEM((PAD + t, HEAD_DIM), BF16), pltpu.VMEM((PAD + t, HEAD_DIM), BF16)],
        compiler_params=_params("arbitrary", "arbitrary"),
    )(proj, proj, proj, proj, bias, qnw_row, knw_row)


def _attn_bwd(proj, bias, qnw_row, knw_row, dog, *, heads, name, tb=ATTN_BLOCK):
    t = proj.shape[0]
    tb = min(tb, t)
    nb, ng = t // tb, tb // ROWS
    col, full, bias_spec, vec = _attn_specs(heads, tb, t)

    def body(q_ref, k_ref, v_ref, z_ref, bias_ref, qnw_ref, knw_ref, dog_ref,
             dq_ref, dk_ref, dv_ref, dz_ref, dbias_ref, dqnw_ref, dknw_ref, kn_scr, v_scr, dkn_scr, dv_scr):
        i = pl.program_id(1)

        @pl.when(i == 0)
        def _():
            _attn_fill(k_ref, v_ref, knw_ref, kn_scr, v_scr, t)
            dkn_scr[...] = jnp.zeros_like(dkn_scr)
            dv_scr[...] = jnp.zeros_like(dv_scr)
            dbias_ref[...] = jnp.zeros_like(dbias_ref)
            dqnw_ref[...] = jnp.zeros_like(dqnw_ref)

        start = i * tb
        _, vjp = jax.vjp(
            _attn_groups, q_ref[...].reshape(ng, ROWS, HEAD_DIM), z_ref[...].reshape(ng, ROWS, HEAD_DIM),
            _attn_windows(kn_scr, start, ng).astype(F32), _attn_windows(v_scr, start, ng).astype(F32), bias_ref[0, 0],
            qnw_ref[...])
        dq, dz, dkn, dv, dbias, dqnw = vjp(dog_ref[...].reshape(ng, ROWS, HEAD_DIM))
        dq_ref[...] = dq.reshape(tb, HEAD_DIM).astype(BF16)
        dz_ref[...] = dz.reshape(tb, HEAD_DIM).astype(BF16)
        for g in range(ng):
            window = pl.ds(pl.multiple_of(start + g * ROWS, ROWS), WIN)
            dkn_scr[window, :] += dkn[g]
            dv_scr[window, :] += dv[g]
        dbias_ref[0] += jnp.sum(dbias, axis=0)
        dqnw_ref[0] += dqnw

        @pl.when(i == nb - 1)
        def _():
            step = min(512, t)

            def finish(j, dknw):
                rows = pl.ds(pl.multiple_of(j * step, step), step)
                prows = pl.ds(pl.multiple_of(PAD + j * step, CHUNK), step)
                _, vjp = jax.vjp(_rms, k_ref[rows, :], knw_ref[...])
                dk, dw = vjp(dkn_scr[prows, :])
                dk_ref[rows, :] = dk.astype(BF16)
                dv_ref[rows, :] = dv_scr[prows, :].astype(BF16)
                return dknw + dw

            dknw_ref[0] = lax.fori_loop(0, t // step, finish, jnp.zeros((1, HEAD_DIM), F32))

    out_col = pl.BlockSpec((tb, HEAD_DIM), lambda h, i: (i, h))
    out_full = pl.BlockSpec((t, HEAD_DIM), lambda h, i: (0, h))
    head_vec = pl.BlockSpec((1, 1, HEAD_DIM), lambda h, i: (h, 0, 0))
    col_shape = jax.ShapeDtypeStruct((t, heads * HEAD_DIM), BF16)
    vec_shape = jax.ShapeDtypeStruct((heads, 1, HEAD_DIM), F32)
    return pl.pallas_call(
        body,
        name=name,
        grid=(heads, nb),
        in_specs=[col(0), full(1), full(2), col(3), bias_spec, vec, vec, pl.BlockSpec((tb, HEAD_DIM), lambda h, i: (i, h))],
        out_specs=[out_col, out_full, out_full, out_col, pl.BlockSpec((1, ROWS, WIN), lambda h, i: (h, 0, 0)), head_vec,
                   head_vec],
        out_shape=[col_shape, col_shape, col_shape, col_shape, jax.ShapeDtypeStruct((heads, ROWS, WIN), F32),
                   vec_shape, vec_shape],
        scratch_shapes=[pltpu.VMEM((PAD + t, HEAD_DIM), BF16), pltpu.VMEM((PAD + t, HEAD_DIM), BF16),
                        pltpu.VMEM((PAD + t, HEAD_DIM), F32), pltpu.VMEM((PAD + t, HEAD_DIM), F32)],
        compiler_params=_params("arbitrary", "arbitrary"),
    )(proj, proj, proj, proj, bias, qnw_row, knw_row, dog)


def _lane_row(v):
    v = v.reshape(1, -1)
    return jnp.pad(v, ((0, 0), (0, LANES - v.shape[1])))


def _local_step(x, target, norm_w, wa_in, conv_w, a_log, dt_bias, onw, wa_out, wb_in, qnw, knw, rel_bias, wb_out, *,
                sharded=False):
    ha, hb = a_log.shape[-1], rel_bias.shape[-2]
    na = 4 * ha * HEAD_DIM
    wa_main = wa_in[:, :na]
    wa_ab = jnp.pad(wa_in[:, na:], ((0, 0), (0, LANES - 2 * ha)))
    alog_row, dtb_row, onw_row = _lane_row(a_log), _lane_row(dt_bias), _lane_row(onw)
    qnw_row, knw_row = _lane_row(qnw), _lane_row(knw)
    bias = _masked_bias(_band_bias(rel_bias.reshape(hb, -1)), min(ATTN_BLOCK, x.shape[0]) // ROWS)

    hn0 = _rmsnorm_fwd(x, norm_w[0:1], name="norm0")
    proj_a = _matmul(hn0, wa_main, name="a_in")
    ab_a = _matmul(hn0, wa_ab, name="a_in_ab")
    og_a, states, got = _gdn_fwd(proj_a, ab_a, conv_w, alog_row, dtb_row, onw_row, heads=ha, name="gdn_fwd",
                                 gather=[wb_in, wa_out, wb_out] if sharded else [])
    if sharded:
        wb_in, wa_out, wb_out = _join_cols(got[0]), got[1].reshape(-1, got[1].shape[-1]), got[2].reshape(-1, got[2].shape[-1])
    h1 = _matmul(og_a, wa_out, residual=x, name="a_out")
    hn1 = _rmsnorm_fwd(h1, norm_w[1:2], name="norm1")
    proj_b = _matmul(hn1, wb_in, name="b_in")
    og_b = _attn_fwd(proj_b, bias, qnw_row, knw_row, heads=hb, name="attn_fwd")
    h2 = _matmul(og_b, wb_out, residual=h1, name="b_out")
    loss, dh2, dh2_b = _loss_head(h2, target, name="loss_head")

    dog_b = _matmul(dh2_b, wb_out, trans_b=True, name="d_b_out_x")
    dwb_out = _matmul(og_b.T, dh2_b, name="d_b_out_w")
    dq, dk, dv, dz, dbias, dqnw, dknw = _attn_bwd(proj_b, bias, qnw_row, knw_row, dog_b, heads=hb, name="attn_bwd")
    dproj_b = jnp.concatenate([dq, dk, dv, dz], axis=1)
    dhn1 = _matmul(dproj_b, wb_in, trans_b=True, name="d_b_in_x")
    dwb_in = _matmul(hn1.T, dproj_b, name="d_b_in_w")
    dh1, dh1_b, dnw1 = _rmsnorm_bwd(h1, norm_w[1:2], dhn1, dh2, name="d_norm1")

    dog_a = _matmul(dh1_b, wa_out, trans_b=True, name="d_a_out_x")
    dwa_out = _matmul(og_a.T, dh1_b, name="d_a_out_w")
    early = [_split_cols(dwb_in), _split_rows(dwa_out), _split_rows(dwb_out)] if sharded else []
    dq, dk, dv, dz, dab, dconv, dalog, ddtb, donw, landed = _gdn_bwd(
        proj_a, ab_a, conv_w, alog_row, dtb_row, onw_row, states, dog_a, heads=ha, name="gdn_bwd",
        exchange=[s.astype(BF16) for s in early])
    if sharded:
        dwb_in, dwa_out, dwb_out = landed
    dproj_a = jnp.concatenate([dq, dk, dv, dz], axis=1)
    dab_b = dab.astype(BF16)
    hn0_t = hn0.T
    dwa_in = jnp.concatenate(
        [_matmul(hn0_t, dproj_a, name="d_a_in_w"), _matmul(hn0_t, dab_b, name="d_a_in_ab_w")[:, :2 * ha]], axis=1)
    if sharded:
        dhn0, (dwa_in, dconv) = _matmul(dproj_a, wa_main, trans_b=True, name="d_a_in_x",
                                        exchange=[_split_cols(dwa_in).astype(BF16), _split_cols(dconv)])
    else:
        dhn0 = _matmul(dproj_a, wa_main, trans_b=True, name="d_a_in_x")
    dhn0 = _matmul(dab_b, wa_ab, trans_b=True, residual=dhn0, name="d_a_in_ab_x")
    dx, _, dnw0 = _rmsnorm_bwd(x, norm_w[0:1], dhn0, dh1, name="d_norm0")

    drel = _band_bias_grad(dbias)
    grads = dict(
        norm_w=jnp.concatenate([dnw0, dnw1], axis=0), a_w_in=dwa_in, a_conv_w=dconv, a_a_log=dalog[:, :ha],
        a_dt_bias=ddtb[:, :ha], a_out_norm_w=donw, a_w_out=dwa_out, b_w_in=dwb_in, b_q_norm_w=jnp.sum(dqnw, axis=0),
        b_k_norm_w=jnp.sum(dknw, axis=0), b_rel_bias=drel[None], b_w_out=dwb_out)
    return loss, dx, grads


_ANY = pl.BlockSpec(memory_space=pl.ANY)
_CHIP_FLIPS = ((1, 0), (0, 1), (1, 1))


def _place():
    x, y, c = lax.axis_index("x"), lax.axis_index("y"), lax.axis_index("c")
    return x, y, c


def _flip(v, bit):
    return 1 - v if bit else v


def _remote(src, dst, send_sem, recv_sem, peer):
    return pltpu.make_async_remote_copy(src_ref=src, dst_ref=dst, send_sem=send_sem, recv_sem=recv_sem, device_id=peer,
                                        device_id_type=MESH)


def _comm_call(body, arrays, out_shapes, n_remote, n_local, name):
    scratch = [pltpu.SemaphoreType.DMA((n_remote,)), pltpu.SemaphoreType.DMA((n_remote,))]
    if n_local:
        scratch.append(pltpu.SemaphoreType.DMA((n_local,)))
    return pl.pallas_call(
        body, name=name, in_specs=[_ANY] * len(arrays), out_specs=[_ANY] * len(out_shapes), out_shape=out_shapes,
        scratch_shapes=scratch)(*arrays)


def _chip_scratch(n):
    return [pltpu.SemaphoreType.DMA((3 * n,)), pltpu.SemaphoreType.DMA((3 * n,)), pltpu.SemaphoreType.DMA((n,))]


def _chip_shapes(gather, arrays):
    return [jax.ShapeDtypeStruct(((N_CHIPS,) + s.shape) if gather else s.shape, s.dtype) for s in arrays]


def _chip_traffic(gather, ins, outs, sems):
    send_sems, recv_sems, local_sems = sems
    x, y, c = _place()
    mine = 2 * x + y
    local, remote, landing = [], [], []
    for a in range(len(ins)):
        local.append(pltpu.make_async_copy(ins[a] if gather else ins[a].at[mine], outs[a].at[mine], local_sems.at[a]))
        for k, (fx, fy) in enumerate(_CHIP_FLIPS):
            peer = (_flip(x, fx), _flip(y, fy), c)
            theirs = 2 * peer[0] + peer[1]
            src = ins[a] if gather else ins[a].at[theirs]
            pair = send_sems.at[3 * a + k], recv_sems.at[3 * a + k]
            remote.append(_remote(src, outs[a].at[mine], *pair, peer))
            landing.append(_remote(src, outs[a].at[theirs], *pair, peer))
    return local + remote, (local, landing, remote)


def _start(traffic):
    for cp in traffic[0]:
        cp.start()


def _finish(traffic):
    local, landing, remote = traffic[1]
    for cp in local:
        cp.wait()
    for cp in landing:
        cp.wait_recv()
    for cp in remote:
        cp.wait_send()


def _with_exchange(compute, n_in, n_out, gather, n_x, grid):
    if not n_x:
        return compute

    def body(*refs):
        ins, x_in = refs[:n_in], refs[n_in:n_in + n_x]
        outs, x_out = refs[n_in + n_x:n_in + n_x + n_out], refs[n_in + n_x + n_out:n_in + 2 * n_x + n_out]
        scratch, sems = refs[n_in + 2 * n_x + n_out:-3], refs[-3:]
        traffic = _chip_traffic(gather, x_in, x_out, sems)
        first = functools.reduce(jnp.logical_and, [pl.program_id(d) == 0 for d in range(len(grid))])
        last = functools.reduce(jnp.logical_and, [pl.program_id(d) == grid[d] - 1 for d in range(len(grid))])

        @pl.when(first)
        def _():
            _start(traffic)

        compute(*ins, *outs, *scratch)

        @pl.when(last)
        def _():
            _finish(traffic)

    return body


def _chip_call(gather, arrays, *, name):
    n = len(arrays)

    def body(*refs):
        traffic = _chip_traffic(gather, refs[:n], refs[n:2 * n], refs[2 * n:])
        _start(traffic)
        _finish(traffic)

    return pl.pallas_call(
        body, name=name, in_specs=[_ANY] * n, out_specs=[_ANY] * n, out_shape=_chip_shapes(gather, arrays),
        scratch_shapes=_chip_scratch(n))(*arrays)


def _swap_pair(arrays, *, name):
    n = len(arrays)

    def body(*refs):
        ins, outs, (send_sems, recv_sems) = refs[:n], refs[n:2 * n], refs[2 * n:]
        x, y, c = _place()
        copies = [_remote(ins[a], outs[a], send_sems.at[a], recv_sems.at[a], (x, y, 1 - c)) for a in range(n)]
        for cp in copies:
            cp.start()
        for cp in copies:
            cp.wait_recv()
        for cp in copies:
            cp.wait_send()

    shapes = [jax.ShapeDtypeStruct(s.shape, s.dtype) for s in arrays]
    return _comm_call(body, arrays, shapes, n, 0, name)


def _gather_all(tile, *, name):
    def body(in_ref, out_ref, send_sems, recv_sems, local_sems):
        x, y, c = _place()
        mine = 4 * x + 2 * y + c
        local = pltpu.make_async_copy(in_ref, out_ref.at[mine], local_sems.at[0])
        remote, landing = [], []
        for k in range(1, N_DEV):
            peer = (_flip(x, k & 4), _flip(y, k & 2), _flip(c, k & 1))
            sems = send_sems.at[k - 1], recv_sems.at[k - 1]
            remote.append(_remote(in_ref, out_ref.at[mine], *sems, peer))
            landing.append(_remote(in_ref, out_ref.at[4 * peer[0] + 2 * peer[1] + peer[2]], *sems, peer))
        for cp in [local] + remote:
            cp.start()
        local.wait()
        for cp in landing:
            cp.wait_recv()
        for cp in remote:
            cp.wait_send()

    return _comm_call(body, [tile], [jax.ShapeDtypeStruct((N_DEV,) + tile.shape, tile.dtype)], N_DEV - 1, 1, name)[0]


def _sum_slots(slabs, *, name, tr=128):
    s, r, c = slabs.shape
    tr = min(tr, r)

    def body(in_ref, o_ref):
        acc = in_ref[0].astype(F32)
        for j in range(1, s):
            acc = acc + in_ref[j].astype(F32)
        o_ref[...] = acc

    return pl.pallas_call(
        body, name=name, grid=(r // tr,),
        in_specs=[pl.BlockSpec((s, tr, c), lambda i: (0, i, 0))], out_specs=pl.BlockSpec((tr, c), lambda i: (i, 0)),
        out_shape=jax.ShapeDtypeStruct((r, c), F32), compiler_params=_params("parallel"))(slabs)


def _adamw_math(w, g, m, v):
    m = ADAM_B1 * m + (1.0 - ADAM_B1) * g
    v = ADAM_B2 * v + (1.0 - ADAM_B2) * (g * g)
    m_hat = m / (1.0 - ADAM_B1 ** ADAM_STEP)
    v_hat = v / (1.0 - ADAM_B2 ** ADAM_STEP)
    delta = -ADAM_LR * (m_hat / (jnp.sqrt(v_hat) + ADAM_EPS) + ADAM_WD * w)
    return delta, m, v


def _adamw(w, m, v, parts, *, name, tr=128):
    r, c = w.shape
    tr = min(tr, r)
    s = len(parts)

    def body(w_ref, m_ref, v_ref, *refs):
        g_ref, d_ref, nm_ref, nv_ref = refs[s:]
        g = refs[0][...]
        for p_ref in refs[1:s]:
            g = g + p_ref[...]
        g_ref[...] = g
        d_ref[...], nm_ref[...], nv_ref[...] = _adamw_math(w_ref[...], g, m_ref[...], v_ref[...])

    blk = pl.BlockSpec((tr, c), lambda i: (i, 0))
    shape = jax.ShapeDtypeStruct((r, c), F32)
    return pl.pallas_call(
        body, name=name, grid=(r // tr,), in_specs=[blk] * (3 + s), out_specs=[blk] * 4, out_shape=[shape] * 4,
        compiler_params=_params("parallel"))(w, m, v, *parts)


_BIG = ("a_w_in", "b_w_in", "a_w_out", "b_w_out", "a_conv_w")
_SMALL = ("norm_w", "a_a_log", "a_dt_bias", "a_out_norm_w", "b_q_norm_w", "b_k_norm_w", "b_rel_bias")
_ORDER = ("norm_w", "a_w_in", "a_conv_w", "a_a_log", "a_dt_bias", "a_out_norm_w", "a_w_out", "b_w_in", "b_q_norm_w",
          "b_k_norm_w", "b_rel_bias", "b_w_out")


def _join_cols(g):
    return jnp.transpose(g, (1, 0, 2)).reshape(g.shape[1], -1)


def _split_cols(g):
    return jnp.transpose(g.reshape(g.shape[0], N_CHIPS, -1), (1, 0, 2))


def _split_rows(g):
    return g.reshape(N_CHIPS, -1, g.shape[-1])


def _pack(d):
    flat = jnp.concatenate([d[n].reshape(-1) for n in _SMALL])
    return jnp.pad(flat, (0, -flat.shape[0] % LANES)).reshape(1, -1)


def _unpack(row, like):
    out, at = {}, 0
    for n in _SMALL:
        size = like[n].size
        out[n] = row[0, at:at + size].reshape(like[n].shape)
        at += size
    return out


def kernel(x, norm_w, a_w_in, a_conv_w, a_a_log, a_dt_bias, a_out_norm_w, a_w_out, b_w_in, b_q_norm_w, b_k_norm_w, b_rel_bias, b_w_out, loss_target, m_norm_w, m_a_w_in, m_a_conv_w, m_a_a_log, m_a_dt_bias, m_a_out_norm_w, m_a_w_out, m_b_w_in, m_b_q_norm_w, m_b_k_norm_w, m_b_rel_bias, m_b_w_out, v_norm_w, v_a_w_in, v_a_conv_w, v_a_a_log, v_a_dt_bias, v_a_out_norm_w, v_a_w_out, v_b_w_in, v_b_q_norm_w, v_b_k_norm_w, v_b_rel_bias, v_b_w_out):
    w = dict(norm_w=norm_w, a_w_in=a_w_in, a_conv_w=a_conv_w, a_a_log=a_a_log, a_dt_bias=a_dt_bias,
             a_out_norm_w=a_out_norm_w, a_w_out=a_w_out, b_w_in=b_w_in, b_q_norm_w=b_q_norm_w, b_k_norm_w=b_k_norm_w,
             b_rel_bias=b_rel_bias, b_w_out=b_w_out)
    m = dict(norm_w=m_norm_w, a_w_in=m_a_w_in, a_conv_w=m_a_conv_w, a_a_log=m_a_a_log, a_dt_bias=m_a_dt_bias,
             a_out_norm_w=m_a_out_norm_w, a_w_out=m_a_w_out, b_w_in=m_b_w_in, b_q_norm_w=m_b_q_norm_w,
             b_k_norm_w=m_b_k_norm_w, b_rel_bias=m_b_rel_bias, b_w_out=m_b_w_out)
    v = dict(norm_w=v_norm_w, a_w_in=v_a_w_in, a_conv_w=v_a_conv_w, a_a_log=v_a_a_log, a_dt_bias=v_a_dt_bias,
             a_out_norm_w=v_a_out_norm_w, a_w_out=v_a_w_out, b_w_in=v_b_w_in, b_q_norm_w=v_b_q_norm_w,
             b_k_norm_w=v_b_k_norm_w, b_rel_bias=v_b_rel_bias, b_w_out=v_b_w_out)

    wa_in, conv = _chip_call(True, [a_w_in[0].astype(BF16), a_conv_w[0]], name="gather_a_in")
    loss, dx, grads = _local_step(
        x[0], loss_target[0], norm_w, _join_cols(wa_in), _join_cols(conv), a_a_log, a_dt_bias, a_out_norm_w,
        a_w_out[0].astype(BF16), b_w_in[0].astype(BF16), b_q_norm_w, b_k_norm_w, b_rel_bias, b_w_out[0].astype(BF16),
        sharded=True)
    loss = lax.psum(loss, ("x", "y", "c"))

    mine = [_sum_slots(grads[n], name=f"chip_sum_{n}") for n in _BIG]
    theirs = _swap_pair(mine, name="pair_grads")
    out = {}
    for n, p, q in zip(_BIG, mine, theirs):
        out[n] = [r[None] for r in _adamw(w[n][0], m[n][0], v[n][0], [p, q], name=f"adamw_{n}")]

    row = _pack(grads)
    tiles = _gather_all(jnp.broadcast_to(row, (8, row.shape[1])), name="gather_small_grads")
    res = _adamw(_pack(w), _pack(m), _pack(v), [tiles[d, 0:1, :] for d in range(N_DEV)], name="adamw_small")
    unpacked = [_unpack(r, w) for r in res]
    for n in _SMALL:
        out[n] = [u[n] for u in unpacked]

    return (loss, dx[None], *[out[n][0] for n in _ORDER], *[out[n][1] for n in _ORDER], *[out[n][2] for n in _ORDER],
            *[out[n][3] for n in _ORDER])
```

```python
import functools

import numpy as np
import jax
import jax.numpy as jnp
from jax import lax
from jax.experimental import pallas as pl
from jax.experimental.pallas import tpu as pltpu

F32 = jnp.float32
BF16 = jnp.bfloat16

CHUNK = 64
HEAD_DIM = 128
LEFT_CHUNKS = 8
REL_CLIP = 256
CONV_K = 4
EPS = 1e-6
HALO = 8

ADAM_LR = 0.001
ADAM_B1 = 0.9
ADAM_B2 = 0.999
ADAM_EPS = 1e-08
ADAM_WD = 0.01
ADAM_STEP = 10

LANES = 128
N_CHIPS = 4
N_DEV = 8
VMEM_LIMIT_BYTES = 56 * 1024 * 1024
MESH = pl.DeviceIdType.MESH
HIGHEST = lax.Precision.HIGHEST


def _params(*sem):
    return pltpu.CompilerParams(dimension_semantics=sem, vmem_limit_bytes=VMEM_LIMIT_BYTES)


def _dot(a, b, dims=(((1,), (0,)), ((), ())), precision=None):
    return lax.dot_general(a, b, dims, precision=precision, preferred_element_type=F32)


_NT = (((1,), (1,)), ((), ()))
_TN = (((0,), (0,)), ((), ()))


def _bdot(a, b, dims=(((1,), (0,)), ((), ()))):
    return _dot(a.astype(BF16), b.astype(BF16), dims)


def _fdot(a, b, dims=(((1,), (0,)), ((), ()))):
    return _dot(a, b, dims, precision=lax.Precision.HIGH)


def _silu(x):
    return x * jax.nn.sigmoid(x)


def _matmul(a, b, *, name, trans_b=False, residual=None, out_dtype=F32, tm=1024, tn=1024, tk=2048, exchange=()):
    m, k = a.shape
    n = b.shape[0] if trans_b else b.shape[1]
    tm, tn, tk = min(tm, m), min(tn, n), min(tk, k)
    assert m % tm == 0 and n % tn == 0 and k % tk == 0, (a.shape, b.shape, tm, tn, tk)
    nk = k // tk
    dims = _NT if trans_b else (((1,), (0,)), ((), ()))

    def body(*refs):
        if residual is None:
            a_ref, b_ref, o_ref, acc_ref = refs
            r_ref = None
        else:
            a_ref, b_ref, r_ref, o_ref, acc_ref = refs
        kk = pl.program_id(2)

        @pl.when(kk == 0)
        def _():
            acc_ref[...] = jnp.zeros_like(acc_ref)

        acc_ref[...] += _dot(a_ref[...], b_ref[...], dims)

        @pl.when(kk == nk - 1)
        def _():
            r = acc_ref[...]
            if r_ref is not None:
                r = r + r_ref[...]
            o_ref[...] = r.astype(o_ref.dtype)

    in_specs = [
        pl.BlockSpec((tm, tk), lambda i, j, kk: (i, kk)),
        pl.BlockSpec((tn, tk), lambda i, j, kk: (j, kk)) if trans_b else pl.BlockSpec((tk, tn), lambda i, j, kk: (kk, j)),
    ]
    args = [a, b]
    if residual is not None:
        in_specs.append(pl.BlockSpec((tm, tn), lambda i, j, kk: (i, j)))
        args.append(residual)
    grid = (m // tm, n // tn, nk)
    n_x = len(exchange)
    out, *landed = pl.pallas_call(
        _with_exchange(body, len(args), 1, False, n_x, grid),
        name=name,
        grid=grid,
        in_specs=in_specs + [_ANY] * n_x,
        out_specs=[pl.BlockSpec((tm, tn), lambda i, j, kk: (i, j))] + [_ANY] * n_x,
        out_shape=[jax.ShapeDtypeStruct((m, n), out_dtype)] + _chip_shapes(False, exchange),
        scratch_shapes=[pltpu.VMEM((tm, tn), F32)] + (_chip_scratch(n_x) if n_x else []),
        compiler_params=_params(*(("arbitrary",) * 3 if n_x else ("parallel", "parallel", "arbitrary"))),
    )(*args, *exchange)
    return (out, landed) if n_x else out


def _rms(x, w):
    return x * lax.rsqrt(jnp.mean(x * x, axis=-1, keepdims=True) + EPS) * w


def _rmsnorm_fwd(x, w_row, *, name, tr=512):
    t, d = x.shape
    tr = min(tr, t)

    def body(x_ref, w_ref, o_ref):
        o_ref[...] = _rms(x_ref[...], w_ref[...]).astype(BF16)

    return pl.pallas_call(
        body,
        name=name,
        grid=(t // tr,),
        in_specs=[pl.BlockSpec((tr, d), lambda i: (i, 0)), pl.BlockSpec((1, d), lambda i: (0, 0))],
        out_specs=pl.BlockSpec((tr, d), lambda i: (i, 0)),
        out_shape=jax.ShapeDtypeStruct((t, d), BF16),
        compiler_params=_params("parallel"),
    )(x, w_row)


def _rmsnorm_bwd(x, w_row, dy, dres, *, name, tr=256):
    t, d = x.shape
    tr = min(tr, t)

    def body(x_ref, w_ref, dy_ref, dres_ref, dx_ref, dxb_ref, dw_ref):
        @pl.when(pl.program_id(0) == 0)
        def _():
            dw_ref[...] = jnp.zeros_like(dw_ref)

        _, vjp = jax.vjp(_rms, x_ref[...], w_ref[...])
        dx, dw = vjp(dy_ref[...])
        dx = dx + dres_ref[...]
        dx_ref[...] = dx
        dxb_ref[...] = dx.astype(BF16)
        dw_ref[...] += dw

    row = pl.BlockSpec((tr, d), lambda i: (i, 0))
    vec = pl.BlockSpec((1, d), lambda i: (0, 0))
    return pl.pallas_call(
        body,
        name=name,
        grid=(t // tr,),
        in_specs=[row, vec, row, row],
        out_specs=[row, row, vec],
        out_shape=[jax.ShapeDtypeStruct((t, d), F32), jax.ShapeDtypeStruct((t, d), BF16), jax.ShapeDtypeStruct((1, d), F32)],
        compiler_params=_params("arbitrary"),
    )(x, w_row, dy, dres)


def _loss_head(h, target, *, name, tr=512):
    t, d = h.shape
    tr = min(tr, t)

    def body(h_ref, t_ref, dh_ref, dhb_ref, part_ref):
        @pl.when(pl.program_id(0) == 0)
        def _():
            part_ref[...] = jnp.zeros_like(part_ref)

        err = h_ref[...] - t_ref[...]
        dh = err * (1.0 / d)
        dh_ref[...] = dh
        dhb_ref[...] = dh.astype(BF16)
        part_ref[...] += jnp.sum(err * err, axis=0, keepdims=True)

    row = pl.BlockSpec((tr, d), lambda i: (i, 0))
    vec = pl.BlockSpec((1, d), lambda i: (0, 0))
    dh, dhb, part = pl.pallas_call(
        body,
        name=name,
        grid=(t // tr,),
        in_specs=[row, row],
        out_specs=[row, row, vec],
        out_shape=[jax.ShapeDtypeStruct((t, d), F32), jax.ShapeDtypeStruct((t, d), BF16), jax.ShapeDtypeStruct((1, d), F32)],
        compiler_params=_params("arbitrary"),
    )(h, target)
    return 0.5 / d * jnp.sum(part), dh, dhb


_BNN = (((2,), (1,)), ((0,), (0,)))
_BNT = (((2,), (2,)), ((0,), (0,)))
_BTN = (((1,), (1,)), ((0,), (0,)))


def _conv_silu(xe, w):
    rows = xe.shape[0] - HALO
    first = HALO - (CONV_K - 1)
    c = w[0:1, :] * xe[first:first + rows, :]
    for j in range(1, CONV_K):
        c = c + w[j:j + 1, :] * xe[first + j:first + j + rows, :]
    return _silu(c)


def _gdn_intra(qx, kx, vx, a, b, wq, wk, wv, alog, dtb):
    n = a.shape[0] // CHUNK
    qt, kt, v = _conv_silu(qx, wq), _conv_silu(kx, wk), _conv_silu(vx, wv)
    q = qt * lax.rsqrt(jnp.sum(qt * qt, axis=-1, keepdims=True) + EPS) * (HEAD_DIM ** -0.5)
    k = kt * lax.rsqrt(jnp.sum(kt * kt, axis=-1, keepdims=True) + EPS)
    lanes = jnp.ones((1, HEAD_DIM), F32)
    beta = jax.nn.sigmoid(b) * lanes
    sp = a + dtb
    g = (-jnp.exp(alog) * (jnp.maximum(sp, 0.0) + jnp.log(1.0 + jnp.exp(-jnp.abs(sp))))) * lanes
    q, k, v, beta, g = (t.reshape(n, CHUNK, HEAD_DIM) for t in (q, k, v, beta, g))

    row = lax.broadcasted_iota(jnp.int32, (n, CHUNK, CHUNK), 1)
    col = lax.broadcasted_iota(jnp.int32, (n, CHUNK, CHUNK), 2)
    tri_incl = row >= col
    tri_strict = row > col
    gc = _fdot(tri_incl.astype(F32), g, _BNN)
    gc_row = _fdot(g[:, :, :CHUNK], (row <= col).astype(F32), _BTN)
    decay = jnp.exp(jnp.where(tri_incl, gc[:, :, :CHUNK] - gc_row, -1e30))
    kb = k * beta
    vb = v * beta
    with_k = _bdot(jnp.concatenate([kb, q], axis=1), k, _BNT)
    neg_l = jnp.where(tri_strict, -(with_k[:, :CHUNK] * decay), 0.0)
    qk = jnp.where(tri_incl, with_k[:, CHUNK:] * decay, 0.0)
    inv = (row == col).astype(F32) + neg_l
    power = _bdot(neg_l, neg_l, _BNN)
    for _ in range(4):
        both = _bdot(jnp.concatenate([inv, power], axis=1), power, _BNN)
        inv, power = inv + both[:, :CHUNK], both[:, CHUNK:]
    inv = inv + _bdot(inv, power, _BNN)
    e = jnp.exp(gc)
    solved = _bdot(inv, jnp.concatenate([kb * e, vb], axis=2), _BNN)
    g_last = gc[:, CHUNK - 1:CHUNK, :]
    k_dec = k * jnp.exp(g_last - gc)
    from_k = _bdot(k_dec, solved, _BTN)
    from_qk = _bdot(qk, solved, _BNN)
    step, add = -from_k[:, :, :HEAD_DIM], from_k[:, :, HEAD_DIM:]
    read, out = q * e - from_qk[:, :, :HEAD_DIM], from_qk[:, :, HEAD_DIM:]
    return step, add, jnp.exp(g_last), read, out


def _gdn_scan_step(state, step, add, decay_last):
    return state * decay_last + _bdot(step, state) + add


def _gdn_outputs(states, read, out, z, onw):
    return _rms(_bdot(read, states, _BNN) + out, onw) * _silu(z)


def _scan_scratch(n, dtype):
    return [pltpu.VMEM((n, HEAD_DIM, HEAD_DIM), dtype), pltpu.VMEM((n, HEAD_DIM, HEAD_DIM), F32), pltpu.VMEM((n, 1, HEAD_DIM), F32)]


def _head_lane(h, offset=0):
    return lax.broadcasted_iota(jnp.int32, (1, LANES), 1) == h + offset


def _pick(mask, x):
    return jnp.sum(jnp.where(mask, x, 0.0), axis=1, keepdims=True)


def _gdn_specs(heads, tb, rev, nb):
    blk = (lambda i: nb - 1 - i) if rev else (lambda i: i)
    hb = tb // HALO

    def col(group):
        return pl.BlockSpec((tb, HEAD_DIM), lambda i, h: (blk(i), group * heads + h))

    def halo(group):
        return pl.BlockSpec((HALO, HEAD_DIM), lambda i, h: (jnp.maximum(blk(i) * hb - 1, 0), group * heads + h))

    def convw(group):
        return pl.BlockSpec((CONV_K, HEAD_DIM), lambda i, h: (0, group * heads + h))

    vec = pl.BlockSpec((1, LANES), lambda i, h: (0, 0))
    ab = pl.BlockSpec((tb, LANES), lambda i, h: (blk(i), 0))
    states = pl.BlockSpec((1, tb // CHUNK, HEAD_DIM, HEAD_DIM), lambda i, h: (h, blk(i), 0, 0))
    return blk, col, halo, convw, vec, ab, states


def _gdn_fwd(proj, ab, conv_w, alog_row, dtb_row, onw_row, *, heads, name, tb=1024, gather=()):
    t = proj.shape[0]
    tb = min(tb, t)
    nb, cpb = t // tb, tb // CHUNK
    _, col, halo, convw, vec, abspec, states = _gdn_specs(heads, tb, False, nb)

    def body(q_ref, k_ref, v_ref, qh_ref, kh_ref, vh_ref, z_ref, ab_ref, wq_ref, wk_ref, wv_ref, alog_ref, dtb_ref, onw_ref,
             og_ref, st_ref, state_scr, x_scr, *op_scr):
        i, h = pl.program_id(0), pl.program_id(1)
        for n, (ref, href) in enumerate(((q_ref, qh_ref), (k_ref, kh_ref), (v_ref, vh_ref))):
            x_scr[n, 0:HALO, :] = jnp.where(i > 0, href[...], 0.0)
            x_scr[n, HALO:HALO + tb, :] = ref[...]

        @pl.when(i == 0)
        def _():
            state_scr[h] = jnp.zeros((HEAD_DIM, HEAD_DIM), F32)

        sel_a, sel_b = _head_lane(h), _head_lane(h, heads)
        alog, dtb = _pick(sel_a, alog_ref[...]), _pick(sel_a, dtb_ref[...])
        abv = ab_ref[...]
        *scan, read, out = _gdn_intra(x_scr[0], x_scr[1], x_scr[2], _pick(sel_a, abv), _pick(sel_b, abv), wq_ref[...],
                                      wk_ref[...], wv_ref[...], alog, dtb)
        for scr, val in zip(op_scr, scan):
            scr[...] = val.astype(scr.dtype)

        def chunk(c, state):
            st_ref[0, c] = state
            return _gdn_scan_step(state, *[scr[c] for scr in op_scr])

        state_scr[h] = lax.fori_loop(0, cpb, chunk, state_scr[h])
        og = _gdn_outputs(st_ref[0], read, out, z_ref[...].reshape(cpb, CHUNK, HEAD_DIM), onw_ref[...])
        og_ref[...] = og.reshape(tb, HEAD_DIM).astype(BF16)

    n_x = len(gather)
    og, st, *gathered = pl.pallas_call(
        _with_exchange(body, 14, 2, True, n_x, (nb, heads)),
        name=name,
        grid=(nb, heads),
        in_specs=[col(0), col(1), col(2), halo(0), halo(1), halo(2), col(3), abspec, convw(0), convw(1), convw(2), vec, vec, vec]
        + [_ANY] * n_x,
        out_specs=[pl.BlockSpec((tb, HEAD_DIM), lambda i, h: (i, h)), states] + [_ANY] * n_x,
        out_shape=[jax.ShapeDtypeStruct((t, heads * HEAD_DIM), BF16),
                   jax.ShapeDtypeStruct((heads, t // CHUNK, HEAD_DIM, HEAD_DIM), F32)] + _chip_shapes(True, gather),
        scratch_shapes=[pltpu.VMEM((heads, HEAD_DIM, HEAD_DIM), F32), pltpu.VMEM((3, HALO + tb, HEAD_DIM), F32)]
        + _scan_scratch(cpb, BF16) + (_chip_scratch(n_x) if n_x else []),
        compiler_params=_params("arbitrary", "arbitrary"),
    )(proj, proj, proj, proj, proj, proj, proj, ab, conv_w, conv_w, conv_w, alog_row, dtb_row, onw_row, *gather)
    return og, st, gathered


def _gdn_bwd(proj, ab, conv_w, alog_row, dtb_row, onw_row, states, dog, *, heads, name, tb=1024, exchange=()):
    t = proj.shape[0]
    tb = min(tb, t)
    nb, cpb = t // tb, tb // CHUNK
    _, col, halo, convw, vec, abspec, states_spec = _gdn_specs(heads, tb, True, nb)
    n_conv = conv_w.shape[1]

    def body(q_ref, k_ref, v_ref, qh_ref, kh_ref, vh_ref, z_ref, ab_ref, wq_ref, wk_ref, wv_ref, alog_ref, dtb_ref, onw_ref,
             st_ref, dog_ref, dq_ref, dk_ref, dv_ref, dz_ref, dab_ref, dconv_ref, dalog_ref, ddtb_ref, donw_ref,
             dstate_scr, x_scr, carry_scr, *scr):
        op_scr, dop_scr, dstates_scr = scr[:3], scr[3:6], scr[6]
        i, h = pl.program_id(0), pl.program_id(1)
        first_block = i == nb - 1
        for n, (ref, href) in enumerate(((q_ref, qh_ref), (k_ref, kh_ref), (v_ref, vh_ref))):
            x_scr[n, 0:HALO, :] = jnp.where(first_block, 0.0, href[...])
            x_scr[n, HALO:HALO + tb, :] = ref[...]

        @pl.when(jnp.logical_and(i == 0, h == 0))
        def _():
            dconv_ref[...] = jnp.zeros_like(dconv_ref)
            dalog_ref[...] = jnp.zeros_like(dalog_ref)
            ddtb_ref[...] = jnp.zeros_like(ddtb_ref)
            donw_ref[...] = jnp.zeros_like(donw_ref)

        @pl.when(h == 0)
        def _():
            dab_ref[...] = jnp.zeros_like(dab_ref)

        @pl.when(i == 0)
        def _():
            dstate_scr[h] = jnp.zeros((HEAD_DIM, HEAD_DIM), F32)
            carry_scr[h] = jnp.zeros((3, HALO, HEAD_DIM), F32)

        sel_a, sel_b = _head_lane(h), _head_lane(h, heads)
        alog, dtb = _pick(sel_a, alog_ref[...]), _pick(sel_a, dtb_ref[...])
        abv = ab_ref[...]
        (*scan, read, out), vjp_intra = jax.vjp(
            _gdn_intra, x_scr[0], x_scr[1], x_scr[2], _pick(sel_a, abv), _pick(sel_b, abv), wq_ref[...], wk_ref[...],
            wv_ref[...], alog, dtb)
        for s, val in zip(op_scr, scan):
            s[...] = val.astype(s.dtype)
        blocked = lambda ref: ref[...].reshape(cpb, CHUNK, HEAD_DIM)
        _, vjp_outputs = jax.vjp(_gdn_outputs, st_ref[0], read, out, blocked(z_ref), onw_ref[...])
        dstates_scr[...], dread, dout, dz, donw = vjp_outputs(blocked(dog_ref))
        dz_ref[...] = dz.reshape(tb, HEAD_DIM).astype(BF16)

        def chunk(i_rev, dstate):
            c = cpb - 1 - i_rev
            _, vjp = jax.vjp(_gdn_scan_step, st_ref[0, c], *[s[c].astype(F32) for s in op_scr])
            dstate, *grads = vjp(dstate)
            for s, val in zip(dop_scr, grads):
                s[c] = val
            return dstate + dstates_scr[c]

        dstate_scr[h] = lax.fori_loop(0, cpb, chunk, dstate_scr[h])
        dqx, dkx, dvx, da, db, dwq, dwk, dwv, dalog, ddtb = vjp_intra((*[s[...] for s in dop_scr], dread, dout))
        dab_ref[...] += jnp.where(sel_a, da, 0.0) + jnp.where(sel_b, db, 0.0)
        for n, (dref, dx, dw) in enumerate(((dq_ref, dqx, dwq), (dk_ref, dkx, dwk), (dv_ref, dvx, dwv))):
            x_scr[n] = dx
            x_scr[n, tb:tb + HALO, :] += carry_scr[h, n]
            carry_scr[h, n] = x_scr[n, 0:HALO, :]
            dref[...] = x_scr[n, HALO:HALO + tb, :].astype(BF16)
            lanes = pl.ds(pl.multiple_of((n * heads + h) * HEAD_DIM, HEAD_DIM), HEAD_DIM)
            dconv_ref[:, lanes] += dw
        dalog_ref[...] += jnp.where(sel_a, dalog, 0.0)
        ddtb_ref[...] += jnp.where(sel_a, ddtb, 0.0)
        donw_ref[...] += donw

    out_col = pl.BlockSpec((tb, HEAD_DIM), lambda i, h: (nb - 1 - i, h))
    dog_spec = pl.BlockSpec((tb, HEAD_DIM), lambda i, h: (nb - 1 - i, h))
    col_shape = jax.ShapeDtypeStruct((t, heads * HEAD_DIM), BF16)
    row_shape = jax.ShapeDtypeStruct((1, LANES), F32)
    n_x = len(exchange)
    outs = pl.pallas_call(
        _with_exchange(body, 16, 9, False, n_x, (nb, heads)),
        name=name,
        grid=(nb, heads),
        in_specs=[col(0), col(1), col(2), halo(0), halo(1), halo(2), col(3), abspec, convw(0), convw(1), convw(2), vec, vec, vec,
                  states_spec, dog_spec] + [_ANY] * n_x,
        out_specs=[out_col, out_col, out_col, out_col, abspec,
                   pl.BlockSpec((CONV_K, n_conv), lambda i, h: (0, 0)), vec, vec, vec] + [_ANY] * n_x,
        out_shape=[col_shape, col_shape, col_shape, col_shape, jax.ShapeDtypeStruct((t, LANES), F32),
                   jax.ShapeDtypeStruct((CONV_K, n_conv), F32), row_shape, row_shape, row_shape] + _chip_shapes(False, exchange),
        scratch_shapes=[pltpu.VMEM((heads, HEAD_DIM, HEAD_DIM), F32), pltpu.VMEM((3, HALO + tb, HEAD_DIM), F32),
                        pltpu.VMEM((heads, 3, HALO, HEAD_DIM), F32)] + _scan_scratch(cpb, BF16) + _scan_scratch(cpb, F32)
        + [pltpu.VMEM((cpb, HEAD_DIM, HEAD_DIM), F32)]
        + (_chip_scratch(n_x) if n_x else []),
        compiler_params=_params("arbitrary", "arbitrary"),
    )(proj, proj, proj, proj, proj, proj, proj, ab, conv_w, conv_w, conv_w, alog_row, dtb_row, onw_row, states, dog, *exchange)
    return (*outs[:9], outs[9:])


BAND = (LEFT_CHUNKS + 1) * CHUNK
PAD = LEFT_CHUNKS * CHUNK
GROUP = 2
ROWS = GROUP * CHUNK
WIN = (LEFT_CHUNKS + GROUP) * CHUNK
DIAGS = WIN + ROWS - 1
NEAR = PAD + ROWS - 1 - REL_CLIP
assert 0 < NEAR < DIAGS and WIN - PAD - 1 <= REL_CLIP and WIN % LANES == 0
ATTN_BLOCK = 1024


def _band_bias(rel_bias):
    heads = rel_bias.shape[0]
    far = jnp.broadcast_to(rel_bias[:, 2 * REL_CLIP:], (heads, NEAR + 1))
    near = rel_bias[:, 2 * REL_CLIP + NEAR + 1 - DIAGS:2 * REL_CLIP][:, ::-1]
    diag = jnp.concatenate([far, near], axis=1)
    return jnp.stack([diag[:, ROWS - 1 - r:ROWS - 1 - r + WIN] for r in range(ROWS)], axis=1)


def _band_bias_grad(dbias):
    heads = dbias.shape[0]
    diag = sum(jnp.pad(dbias[:, r, :], ((0, 0), (ROWS - 1 - r, r))) for r in range(ROWS))
    far = jnp.sum(diag[:, :NEAR + 1], axis=1, keepdims=True)
    near = diag[:, NEAR + 1:][:, ::-1]
    unused = jnp.zeros((heads, 2 * REL_CLIP - near.shape[1]), F32)
    return jnp.concatenate([unused, near, far], axis=1)


def _masked_bias(bias, n):
    r = np.arange(ROWS)[:, None]
    key = np.arange(WIN)[None, :]
    band_start = (r // CHUNK) * CHUNK
    in_band = np.logical_and(key >= band_start, key < band_start + BAND)
    in_sequence = key[None] >= PAD - np.arange(n)[:, None, None] * ROWS
    first = jnp.where(np.logical_and(in_band[None], in_sequence)[None], bias[:, None], -1e30)
    return first, jnp.where(in_band[None, None], bias[:, None], -1e30)


def _attn_groups(q_pre, z, kn, v, bias, qnw):
    q = _rms(q_pre, qnw)
    s = _bdot(q, kn, _BNT) * (HEAD_DIM ** -0.5) + bias
    p = jnp.exp(s - jnp.max(s, axis=-1, keepdims=True))
    p = p / jnp.sum(p, axis=-1, keepdims=True)
    return _bdot(p, v, _BNN) * _silu(z)


def _attn_groups_bwd(q_pre, z, kn, v, bias, qnw, dog):
    scale = HEAD_DIM ** -0.5
    inv_rms = lax.rsqrt(jnp.mean(q_pre * q_pre, axis=-1, keepdims=True) + EPS)
    q_hat = q_pre * inv_rms
    q_b = (q_hat * qnw).astype(BF16)
    s = _dot(q_b, kn, _BNT) * scale + bias
    e = jnp.exp(s - jnp.max(s, axis=-1, keepdims=True))
    p = e * (1.0 / jnp.sum(e, axis=-1, keepdims=True))
    p_b = p.astype(BF16)
    o = _dot(p_b, v, _BNN)
    sig = jax.nn.sigmoid(z)
    do = dog * (z * sig)
    dz = dog * o * (sig * (1.0 + z * (1.0 - sig)))
    do_b = do.astype(BF16)
    dv = _dot(p_b, do_b, _BTN)
    dp = _dot(do_b, v, _BNT)
    ds = p * (dp - jnp.sum(do * o, axis=-1, keepdims=True))
    ds_b = (ds * scale).astype(BF16)
    dq = _dot(ds_b, kn, _BNN)
    dkn = _dot(ds_b, q_b, _BTN)
    dqnw = jnp.sum(jnp.sum(dq * q_hat, axis=0), axis=0, keepdims=True)
    dq_hat = dq * qnw
    dq_pre = inv_rms * (dq_hat - q_hat * jnp.mean(dq_hat * q_hat, axis=-1, keepdims=True))
    return dq_pre, dz, dkn, dv, jnp.sum(ds, axis=0), dqnw


def _attn_specs(heads, tb, t):
    def col(group):
        return pl.BlockSpec((tb, HEAD_DIM), lambda h, i: (i, group * heads + h))

    def full(group):
        return pl.BlockSpec((t, HEAD_DIM), lambda h, i: (0, group * heads + h))

    bias = [pl.BlockSpec((1, tb // ROWS, ROWS, WIN), lambda h, i: (h, 0, 0, 0)),
            pl.BlockSpec((1, 1, ROWS, WIN), lambda h, i: (h, 0, 0, 0))]
    vec = pl.BlockSpec((1, HEAD_DIM), lambda h, i: (0, 0))
    return col, full, bias, vec


def _attn_windows(scr, block_start, n):
    return jnp.stack([scr[pl.ds(pl.multiple_of(block_start + g * ROWS, ROWS), WIN), :] for g in range(n)])


def _attn_fill(k_ref, v_ref, knw_ref, kn_scr, v_scr, t):
    kn_scr[0:PAD, :] = jnp.zeros((PAD, HEAD_DIM), BF16)
    v_scr[0:PAD, :] = jnp.zeros((PAD, HEAD_DIM), BF16)
    step = min(512, t)

    def fill(j, _):
        rows = pl.ds(pl.multiple_of(j * step, step), step)
        prows = pl.ds(pl.multiple_of(PAD + j * step, CHUNK), step)
        kn_scr[prows, :] = _rms(k_ref[rows, :], knw_ref[...]).astype(BF16)
        v_scr[prows, :] = v_ref[rows, :].astype(BF16)
        return 0

    lax.fori_loop(0, t // step, fill, 0)


def _attn_fwd(proj, bias, qnw_row, knw_row, *, heads, name, tb=ATTN_BLOCK):
    t = proj.shape[0]
    tb = min(tb, t)
    nb, ng = t // tb, tb // ROWS
    col, full, bias_spec, vec = _attn_specs(heads, tb, t)

    def body(q_ref, k_ref, v_ref, z_ref, first_ref, rest_ref, qnw_ref, knw_ref, og_ref, kn_scr, v_scr):
        i = pl.program_id(1)

        @pl.when(i == 0)
        def _():
            _attn_fill(k_ref, v_ref, knw_ref, kn_scr, v_scr, t)

        def run(block_bias):
            start = i * tb
            og = _attn_groups(q_ref[...].reshape(ng, ROWS, HEAD_DIM), z_ref[...].reshape(ng, ROWS, HEAD_DIM),
                              _attn_windows(kn_scr, start, ng), _attn_windows(v_scr, start, ng), block_bias, qnw_ref[...])
            og_ref[...] = og.reshape(tb, HEAD_DIM).astype(BF16)

        pl.when(i == 0)(lambda: run(first_ref[0]))
        pl.when(i > 0)(lambda: run(rest_ref[0]))

    return pl.pallas_call(
        body,
        name=name,
        grid=(heads, nb),
        in_specs=[col(0), full(1), full(2), col(3), *bias_spec, vec, vec],
        out_specs=pl.BlockSpec((tb, HEAD_DIM), lambda h, i: (i, h)),
        out_shape=jax.ShapeDtypeStruct((t, heads * HEAD_DIM), BF16),
        scratch_shapes=[pltpu.VMEM((PAD + t, HEAD_DIM), BF16), pltpu.VMEM((PAD + t, HEAD_DIM), BF16)],
        compiler_params=_params("arbitrary", "arbitrary"),
    )(proj, proj, proj, proj, *bias, qnw_row, knw_row)


def _attn_bwd(proj, bias, qnw_row, knw_row, dog, *, heads, name, tb=ATTN_BLOCK, sub=4):
    t = proj.shape[0]
    tb = min(tb, t)
    nb, ng = t // tb, tb // ROWS
    sub = min(sub, ng)
    col, full, bias_spec, vec = _attn_specs(heads, tb, t)

    def body(q_ref, k_ref, v_ref, z_ref, first_ref, rest_ref, qnw_ref, knw_ref, dog_ref,
             dq_ref, dk_ref, dv_ref, dz_ref, dbias_ref, dqnw_ref, dknw_ref, kn_scr, v_scr, dkn_scr, dv_scr):
        i = pl.program_id(1)

        @pl.when(i == 0)
        def _():
            _attn_fill(k_ref, v_ref, knw_ref, kn_scr, v_scr, t)
            dkn_scr[...] = jnp.zeros_like(dkn_scr)
            dv_scr[...] = jnp.zeros_like(dv_scr)
            dbias_ref[...] = jnp.zeros_like(dbias_ref)
            dqnw_ref[...] = jnp.zeros_like(dqnw_ref)

        def run(block_bias):
            for g0 in range(0, ng, sub):
                rows = pl.ds(g0 * ROWS, sub * ROWS)
                at = i * tb + g0 * ROWS
                blocked = lambda ref: ref[rows, :].reshape(sub, ROWS, HEAD_DIM)
                dq, dz, dkn, dv, dbias, dqnw = _attn_groups_bwd(
                    blocked(q_ref), blocked(z_ref), _attn_windows(kn_scr, at, sub), _attn_windows(v_scr, at, sub),
                    block_bias(g0), qnw_ref[...], blocked(dog_ref))
                dq_ref[rows, :] = dq.reshape(sub * ROWS, HEAD_DIM).astype(BF16)
                dz_ref[rows, :] = dz.reshape(sub * ROWS, HEAD_DIM).astype(BF16)
                for g in range(sub):
                    window = pl.ds(pl.multiple_of(at + g * ROWS, ROWS), WIN)
                    dkn_scr[window, :] += dkn[g]
                    dv_scr[window, :] += dv[g]
                dbias_ref[0] += dbias
                dqnw_ref[0] += dqnw

        pl.when(i == 0)(lambda: run(lambda g0: first_ref[0, g0:g0 + sub]))
        pl.when(i > 0)(lambda: run(lambda g0: rest_ref[0]))

        @pl.when(i == nb - 1)
        def _():
            step = min(512, t)

            def finish(j, dknw):
                rows = pl.ds(pl.multiple_of(j * step, step), step)
                prows = pl.ds(pl.multiple_of(PAD + j * step, CHUNK), step)
                _, vjp = jax.vjp(_rms, k_ref[rows, :], knw_ref[...])
                dk, dw = vjp(dkn_scr[prows, :])
                dk_ref[rows, :] = dk.astype(BF16)
                dv_ref[rows, :] = dv_scr[prows, :].astype(BF16)
                return dknw + dw

            dknw_ref[0] = lax.fori_loop(0, t // step, finish, jnp.zeros((1, HEAD_DIM), F32))

    out_col = pl.BlockSpec((tb, HEAD_DIM), lambda h, i: (i, h))
    out_full = pl.BlockSpec((t, HEAD_DIM), lambda h, i: (0, h))
    head_vec = pl.BlockSpec((1, 1, HEAD_DIM), lambda h, i: (h, 0, 0))
    col_shape = jax.ShapeDtypeStruct((t, heads * HEAD_DIM), BF16)
    vec_shape = jax.ShapeDtypeStruct((heads, 1, HEAD_DIM), F32)
    return pl.pallas_call(
        body,
        name=name,
        grid=(heads, nb),
        in_specs=[col(0), full(1), full(2), col(3), *bias_spec, vec, vec, pl.BlockSpec((tb, HEAD_DIM), lambda h, i: (i, h))],
        out_specs=[out_col, out_full, out_full, out_col, pl.BlockSpec((1, ROWS, WIN), lambda h, i: (h, 0, 0)), head_vec,
                   head_vec],
        out_shape=[col_shape, col_shape, col_shape, col_shape, jax.ShapeDtypeStruct((heads, ROWS, WIN), F32),
                   vec_shape, vec_shape],
        scratch_shapes=[pltpu.VMEM((PAD + t, HEAD_DIM), BF16), pltpu.VMEM((PAD + t, HEAD_DIM), BF16),
                        pltpu.VMEM((PAD + t, HEAD_DIM), F32), pltpu.VMEM((PAD + t, HEAD_DIM), F32)],
        compiler_params=_params("arbitrary", "arbitrary"),
    )(proj, proj, proj, proj, *bias, qnw_row, knw_row, dog)


def _lane_row(v):
    v = v.reshape(1, -1)
    return jnp.pad(v, ((0, 0), (0, LANES - v.shape[1])))


def _local_step(x, target, norm_w, wa_in, conv_w, a_log, dt_bias, onw, wa_out, wb_in, qnw, knw, rel_bias, wb_out, *,
                sharded=False):
    ha, hb = a_log.shape[-1], rel_bias.shape[-2]
    na = 4 * ha * HEAD_DIM
    wa_main = wa_in[:, :na]
    wa_ab = jnp.pad(wa_in[:, na:], ((0, 0), (0, LANES - 2 * ha)))
    alog_row, dtb_row, onw_row = _lane_row(a_log), _lane_row(dt_bias), _lane_row(onw)
    qnw_row, knw_row = _lane_row(qnw), _lane_row(knw)
    bias = _masked_bias(_band_bias(rel_bias.reshape(hb, -1)), min(ATTN_BLOCK, x.shape[0]) // ROWS)

    hn0 = _rmsnorm_fwd(x, norm_w[0:1], name="norm0")
    proj_a = _matmul(hn0, wa_main, name="a_in")
    ab_a = _matmul(hn0, wa_ab, name="a_in_ab")
    og_a, states, got = _gdn_fwd(proj_a, ab_a, conv_w, alog_row, dtb_row, onw_row, heads=ha, name="gdn_fwd",
                                 gather=[wb_in, wa_out, wb_out] if sharded else [])
    if sharded:
        wb_in, wa_out, wb_out = _join_cols(got[0]), got[1].reshape(-1, got[1].shape[-1]), got[2].reshape(-1, got[2].shape[-1])
    h1 = _matmul(og_a, wa_out, residual=x, name="a_out")
    hn1 = _rmsnorm_fwd(h1, norm_w[1:2], name="norm1")
    proj_b = _matmul(hn1, wb_in, name="b_in")
    og_b = _attn_fwd(proj_b, bias, qnw_row, knw_row, heads=hb, name="attn_fwd")
    h2 = _matmul(og_b, wb_out, residual=h1, name="b_out")
    loss, dh2, dh2_b = _loss_head(h2, target, name="loss_head")

    dog_b = _matmul(dh2_b, wb_out, trans_b=True, name="d_b_out_x")
    dwb_out = _matmul(og_b.T, dh2_b, name="d_b_out_w")
    dq, dk, dv, dz, dbias, dqnw, dknw = _attn_bwd(proj_b, bias, qnw_row, knw_row, dog_b, heads=hb, name="attn_bwd")
    dproj_b = jnp.concatenate([dq, dk, dv, dz], axis=1)
    dhn1 = _matmul(dproj_b, wb_in, trans_b=True, name="d_b_in_x")
    dwb_in = _matmul(hn1.T, dproj_b, name="d_b_in_w")
    dh1, dh1_b, dnw1 = _rmsnorm_bwd(h1, norm_w[1:2], dhn1, dh2, name="d_norm1")

    dog_a = _matmul(dh1_b, wa_out, trans_b=True, name="d_a_out_x")
    dwa_out = _matmul(og_a.T, dh1_b, name="d_a_out_w")
    early = [_split_cols(dwb_in), _split_rows(dwa_out), _split_rows(dwb_out)] if sharded else []
    dq, dk, dv, dz, dab, dconv, dalog, ddtb, donw, landed = _gdn_bwd(
        proj_a, ab_a, conv_w, alog_row, dtb_row, onw_row, states, dog_a, heads=ha, name="gdn_bwd",
        exchange=[s.astype(BF16) for s in early])
    if sharded:
        dwb_in, dwa_out, dwb_out = landed
    dproj_a = jnp.concatenate([dq, dk, dv, dz], axis=1)
    dab_b = dab.astype(BF16)
    hn0_t = hn0.T
    dwa_in = jnp.concatenate(
        [_matmul(hn0_t, dproj_a, name="d_a_in_w"), _matmul(hn0_t, dab_b, name="d_a_in_ab_w")[:, :2 * ha]], axis=1)
    if sharded:
        dhn0, (dwa_in, dconv) = _matmul(dproj_a, wa_main, trans_b=True, name="d_a_in_x",
                                        exchange=[_split_cols(dwa_in).astype(BF16), _split_cols(dconv)])
    else:
        dhn0 = _matmul(dproj_a, wa_main, trans_b=True, name="d_a_in_x")
    dhn0 = _matmul(dab_b, wa_ab, trans_b=True, residual=dhn0, name="d_a_in_ab_x")
    dx, _, dnw0 = _rmsnorm_bwd(x, norm_w[0:1], dhn0, dh1, name="d_norm0")

    drel = _band_bias_grad(dbias)
    grads = dict(
        norm_w=jnp.concatenate([dnw0, dnw1], axis=0), a_w_in=dwa_in, a_conv_w=dconv, a_a_log=dalog[:, :ha],
        a_dt_bias=ddtb[:, :ha], a_out_norm_w=donw, a_w_out=dwa_out, b_w_in=dwb_in, b_q_norm_w=jnp.sum(dqnw, axis=0),
        b_k_norm_w=jnp.sum(dknw, axis=0), b_rel_bias=drel[None], b_w_out=dwb_out)
    return loss, dx, grads


_ANY = pl.BlockSpec(memory_space=pl.ANY)
_CHIP_FLIPS = ((1, 0), (0, 1), (1, 1))


def _place():
    x, y, c = lax.axis_index("x"), lax.axis_index("y"), lax.axis_index("c")
    return x, y, c


def _flip(v, bit):
    return 1 - v if bit else v


def _remote(src, dst, send_sem, recv_sem, peer):
    return pltpu.make_async_remote_copy(src_ref=src, dst_ref=dst, send_sem=send_sem, recv_sem=recv_sem, device_id=peer,
                                        device_id_type=MESH)


def _comm_call(body, arrays, out_shapes, n_remote, n_local, name):
    scratch = [pltpu.SemaphoreType.DMA((n_remote,)), pltpu.SemaphoreType.DMA((n_remote,))]
    if n_local:
        scratch.append(pltpu.SemaphoreType.DMA((n_local,)))
    return pl.pallas_call(
        body, name=name, in_specs=[_ANY] * len(arrays), out_specs=[_ANY] * len(out_shapes), out_shape=out_shapes,
        scratch_shapes=scratch)(*arrays)


def _chip_scratch(n):
    return [pltpu.SemaphoreType.DMA((3 * n,)), pltpu.SemaphoreType.DMA((3 * n,)), pltpu.SemaphoreType.DMA((n,))]


def _chip_shapes(gather, arrays):
    return [jax.ShapeDtypeStruct(((N_CHIPS,) + s.shape) if gather else s.shape, s.dtype) for s in arrays]


def _chip_traffic(gather, ins, outs, sems):
    send_sems, recv_sems, local_sems = sems
    x, y, c = _place()
    mine = 2 * x + y
    local, remote, landing = [], [], []
    for a in range(len(ins)):
        local.append(pltpu.make_async_copy(ins[a] if gather else ins[a].at[mine], outs[a].at[mine], local_sems.at[a]))
        for k, (fx, fy) in enumerate(_CHIP_FLIPS):
            peer = (_flip(x, fx), _flip(y, fy), c)
            theirs = 2 * peer[0] + peer[1]
            src = ins[a] if gather else ins[a].at[theirs]
            pair = send_sems.at[3 * a + k], recv_sems.at[3 * a + k]
            remote.append(_remote(src, outs[a].at[mine], *pair, peer))
            landing.append(_remote(src, outs[a].at[theirs], *pair, peer))
    return local + remote, (local, landing, remote)


def _start(traffic):
    for cp in traffic[0]:
        cp.start()


def _finish(traffic):
    local, landing, remote = traffic[1]
    for cp in local:
        cp.wait()
    for cp in landing:
        cp.wait_recv()
    for cp in remote:
        cp.wait_send()


def _with_exchange(compute, n_in, n_out, gather, n_x, grid):
    if not n_x:
        return compute

    def body(*refs):
        ins, x_in = refs[:n_in], refs[n_in:n_in + n_x]
        outs, x_out = refs[n_in + n_x:n_in + n_x + n_out], refs[n_in + n_x + n_out:n_in + 2 * n_x + n_out]
        scratch, sems = refs[n_in + 2 * n_x + n_out:-3], refs[-3:]
        traffic = _chip_traffic(gather, x_in, x_out, sems)
        first = functools.reduce(jnp.logical_and, [pl.program_id(d) == 0 for d in range(len(grid))])
        last = functools.reduce(jnp.logical_and, [pl.program_id(d) == grid[d] - 1 for d in range(len(grid))])

        @pl.when(first)
        def _():
            _start(traffic)

        compute(*ins, *outs, *scratch)

        @pl.when(last)
        def _():
            _finish(traffic)

    return body


def _chip_call(gather, arrays, *, name):
    n = len(arrays)

    def body(*refs):
        traffic = _chip_traffic(gather, refs[:n], refs[n:2 * n], refs[2 * n:])
        _start(traffic)
        _finish(traffic)

    return pl.pallas_call(
        body, name=name, in_specs=[_ANY] * n, out_specs=[_ANY] * n, out_shape=_chip_shapes(gather, arrays),
        scratch_shapes=_chip_scratch(n))(*arrays)


def _swap_pair(arrays, *, name):
    n = len(arrays)

    def body(*refs):
        ins, outs, (send_sems, recv_sems) = refs[:n], refs[n:2 * n], refs[2 * n:]
        x, y, c = _place()
        copies = [_remote(ins[a], outs[a], send_sems.at[a], recv_sems.at[a], (x, y, 1 - c)) for a in range(n)]
        for cp in copies:
            cp.start()
        for cp in copies:
            cp.wait_recv()
        for cp in copies:
            cp.wait_send()

    shapes = [jax.ShapeDtypeStruct(s.shape, s.dtype) for s in arrays]
    return _comm_call(body, arrays, shapes, n, 0, name)


def _gather_all(tile, *, name):
    def body(in_ref, out_ref, send_sems, recv_sems, local_sems):
        x, y, c = _place()
        mine = 4 * x + 2 * y + c
        local = pltpu.make_async_copy(in_ref, out_ref.at[mine], local_sems.at[0])
        remote, landing = [], []
        for k in range(1, N_DEV):
            peer = (_flip(x, k & 4), _flip(y, k & 2), _flip(c, k & 1))
            sems = send_sems.at[k - 1], recv_sems.at[k - 1]
            remote.append(_remote(in_ref, out_ref.at[mine], *sems, peer))
            landing.append(_remote(in_ref, out_ref.at[4 * peer[0] + 2 * peer[1] + peer[2]], *sems, peer))
        for cp in [local] + remote:
            cp.start()
        local.wait()
        for cp in landing:
            cp.wait_recv()
        for cp in remote:
            cp.wait_send()

    return _comm_call(body, [tile], [jax.ShapeDtypeStruct((N_DEV,) + tile.shape, tile.dtype)], N_DEV - 1, 1, name)[0]


def _sum_slots(slabs, *, name, tr=128):
    s, r, c = slabs.shape
    tr = min(tr, r)

    def body(in_ref, o_ref):
        acc = in_ref[0].astype(F32)
        for j in range(1, s):
            acc = acc + in_ref[j].astype(F32)
        o_ref[...] = acc

    return pl.pallas_call(
        body, name=name, grid=(r // tr,),
        in_specs=[pl.BlockSpec((s, tr, c), lambda i: (0, i, 0))], out_specs=pl.BlockSpec((tr, c), lambda i: (i, 0)),
        out_shape=jax.ShapeDtypeStruct((r, c), F32), compiler_params=_params("parallel"))(slabs)


def _adamw_math(w, g, m, v):
    m = ADAM_B1 * m + (1.0 - ADAM_B1) * g
    v = ADAM_B2 * v + (1.0 - ADAM_B2) * (g * g)
    m_hat = m / (1.0 - ADAM_B1 ** ADAM_STEP)
    v_hat = v / (1.0 - ADAM_B2 ** ADAM_STEP)
    delta = -ADAM_LR * (m_hat / (jnp.sqrt(v_hat) + ADAM_EPS) + ADAM_WD * w)
    return delta, m, v


def _adamw(w, m, v, parts, *, name, tr=128):
    r, c = w.shape
    tr = min(tr, r)
    s = len(parts)

    def body(w_ref, m_ref, v_ref, *refs):
        g_ref, d_ref, nm_ref, nv_ref = refs[s:]
        g = refs[0][...]
        for p_ref in refs[1:s]:
            g = g + p_ref[...]
        g_ref[...] = g
        d_ref[...], nm_ref[...], nv_ref[...] = _adamw_math(w_ref[...], g, m_ref[...], v_ref[...])

    blk = pl.BlockSpec((tr, c), lambda i: (i, 0))
    shape = jax.ShapeDtypeStruct((r, c), F32)
    return pl.pallas_call(
        body, name=name, grid=(r // tr,), in_specs=[blk] * (3 + s), out_specs=[blk] * 4, out_shape=[shape] * 4,
        compiler_params=_params("parallel"))(w, m, v, *parts)


_BIG = ("a_w_in", "b_w_in", "a_w_out", "b_w_out", "a_conv_w")
_SMALL = ("norm_w", "a_a_log", "a_dt_bias", "a_out_norm_w", "b_q_norm_w", "b_k_norm_w", "b_rel_bias")
_ORDER = ("norm_w", "a_w_in", "a_conv_w", "a_a_log", "a_dt_bias", "a_out_norm_w", "a_w_out", "b_w_in", "b_q_norm_w",
          "b_k_norm_w", "b_rel_bias", "b_w_out")


def _join_cols(g):
    return jnp.transpose(g, (1, 0, 2)).reshape(g.shape[1], -1)


def _split_cols(g):
    return jnp.transpose(g.reshape(g.shape[0], N_CHIPS, -1), (1, 0, 2))


def _split_rows(g):
    return g.reshape(N_CHIPS, -1, g.shape[-1])


def _pack(d):
    flat = jnp.concatenate([d[n].reshape(-1) for n in _SMALL])
    return jnp.pad(flat, (0, -flat.shape[0] % LANES)).reshape(1, -1)


def _unpack(row, like):
    out, at = {}, 0
    for n in _SMALL:
        size = like[n].size
        out[n] = row[0, at:at + size].reshape(like[n].shape)
        at += size
    return out


def kernel(x, norm_w, a_w_in, a_conv_w, a_a_log, a_dt_bias, a_out_norm_w, a_w_out, b_w_in, b_q_norm_w, b_k_norm_w, b_rel_bias, b_w_out, loss_target, m_norm_w, m_a_w_in, m_a_conv_w, m_a_a_log, m_a_dt_bias, m_a_out_norm_w, m_a_w_out, m_b_w_in, m_b_q_norm_w, m_b_k_norm_w, m_b_rel_bias, m_b_w_out, v_norm_w, v_a_w_in, v_a_conv_w, v_a_a_log, v_a_dt_bias, v_a_out_norm_w, v_a_w_out, v_b_w_in, v_b_q_norm_w, v_b_k_norm_w, v_b_rel_bias, v_b_w_out):
    w = dict(norm_w=norm_w, a_w_in=a_w_in, a_conv_w=a_conv_w, a_a_log=a_a_log, a_dt_bias=a_dt_bias,
             a_out_norm_w=a_out_norm_w, a_w_out=a_w_out, b_w_in=b_w_in, b_q_norm_w=b_q_norm_w, b_k_norm_w=b_k_norm_w,
             b_rel_bias=b_rel_bias, b_w_out=b_w_out)
    m = dict(norm_w=m_norm_w, a_w_in=m_a_w_in, a_conv_w=m_a_conv_w, a_a_log=m_a_a_log, a_dt_bias=m_a_dt_bias,
             a_out_norm_w=m_a_out_norm_w, a_w_out=m_a_w_out, b_w_in=m_b_w_in, b_q_norm_w=m_b_q_norm_w,
             b_k_norm_w=m_b_k_norm_w, b_rel_bias=m_b_rel_bias, b_w_out=m_b_w_out)
    v = dict(norm_w=v_norm_w, a_w_in=v_a_w_in, a_conv_w=v_a_conv_w, a_a_log=v_a_a_log, a_dt_bias=v_a_dt_bias,
             a_out_norm_w=v_a_out_norm_w, a_w_out=v_a_w_out, b_w_in=v_b_w_in, b_q_norm_w=v_b_q_norm_w,
             b_k_norm_w=v_b_k_norm_w, b_rel_bias=v_b_rel_bias, b_w_out=v_b_w_out)

    wa_in, conv = _chip_call(True, [a_w_in[0].astype(BF16), a_conv_w[0]], name="gather_a_in")
    loss, dx, grads = _local_step(
        x[0], loss_target[0], norm_w, _join_cols(wa_in), _join_cols(conv), a_a_log, a_dt_bias, a_out_norm_w,
        a_w_out[0].astype(BF16), b_w_in[0].astype(BF16), b_q_norm_w, b_k_norm_w, b_rel_bias, b_w_out[0].astype(BF16),
        sharded=True)
    loss = lax.psum(loss, ("x", "y", "c"))

    mine = [_sum_slots(grads[n], name=f"chip_sum_{n}") for n in _BIG]
    theirs = _swap_pair(mine, name="pair_grads")
    out = {}
    for n, p, q in zip(_BIG, mine, theirs):
        out[n] = [r[None] for r in _adamw(w[n][0], m[n][0], v[n][0], [p, q], name=f"adamw_{n}")]

    row = _pack(grads)
    tiles = _gather_all(jnp.broadcast_to(row, (8, row.shape[1])), name="gather_small_grads")
    res = _adamw(_pack(w), _pack(m), _pack(v), [tiles[d, 0:1, :] for d in range(N_DEV)], name="adamw_small")
    unpacked = [_unpack(r, w) for r in res]
    for n in _SMALL:
        out[n] = [u[n] for u in unpacked]

    return (loss, dx[None], *[out[n][0] for n in _ORDER], *[out[n][1] for n in _ORDER], *[out[n][2] for n in _ORDER],
            *[out[n][3] for n in _ORDER])
```

```python
import functools

import numpy as np
import jax
import jax.numpy as jnp
from jax import lax
from jax.experimental import pallas as pl
from jax.experimental.pallas import tpu as pltpu

F32 = jnp.float32
BF16 = jnp.bfloat16

CHUNK = 64
HEAD_DIM = 128
LEFT_CHUNKS = 8
REL_CLIP = 256
CONV_K = 4
EPS = 1e-6
HALO = 8

ADAM_LR = 0.001
ADAM_B1 = 0.9
ADAM_B2 = 0.999
ADAM_EPS = 1e-08
ADAM_WD = 0.01
ADAM_STEP = 10

LANES = 128
N_CHIPS = 4
N_DEV = 8
VMEM_LIMIT_BYTES = 56 * 1024 * 1024
MESH = pl.DeviceIdType.MESH
HIGHEST = lax.Precision.HIGHEST


def _params(*sem):
    return pltpu.CompilerParams(dimension_semantics=sem, vmem_limit_bytes=VMEM_LIMIT_BYTES)


def _dot(a, b, dims=(((1,), (0,)), ((), ())), precision=None):
    return lax.dot_general(a, b, dims, precision=precision, preferred_element_type=F32)


_NT = (((1,), (1,)), ((), ()))
_TN = (((0,), (0,)), ((), ()))


def _bdot(a, b, dims=(((1,), (0,)), ((), ()))):
    return _dot(a.astype(BF16), b.astype(BF16), dims)


def _fdot(a, b, dims=(((1,), (0,)), ((), ()))):
    return _dot(a, b, dims, precision=lax.Precision.HIGH)


def _silu(x):
    return x * jax.nn.sigmoid(x)


def _matmul(a, b, *, name, trans_b=False, residual=None, out_dtype=F32, tm=1024, tn=1024, tk=2048, exchange=()):
    m, k = a.shape
    n = b.shape[0] if trans_b else b.shape[1]
    tm, tn, tk = min(tm, m), min(tn, n), min(tk, k)
    assert m % tm == 0 and n % tn == 0 and k % tk == 0, (a.shape, b.shape, tm, tn, tk)
    nk = k // tk
    dims = _NT if trans_b else (((1,), (0,)), ((), ()))

    def body(*refs):
        if residual is None:
            a_ref, b_ref, o_ref, acc_ref = refs
            r_ref = None
        else:
            a_ref, b_ref, r_ref, o_ref, acc_ref = refs
        kk = pl.program_id(2)

        @pl.when(kk == 0)
        def _():
            acc_ref[...] = jnp.zeros_like(acc_ref)

        acc_ref[...] += _dot(a_ref[...], b_ref[...], dims)

        @pl.when(kk == nk - 1)
        def _():
            r = acc_ref[...]
            if r_ref is not None:
                r = r + r_ref[...]
            o_ref[...] = r.astype(o_ref.dtype)

    in_specs = [
        pl.BlockSpec((tm, tk), lambda i, j, kk: (i, kk)),
        pl.BlockSpec((tn, tk), lambda i, j, kk: (j, kk)) if trans_b else pl.BlockSpec((tk, tn), lambda i, j, kk: (kk, j)),
    ]
    args = [a, b]
    if residual is not None:
        in_specs.append(pl.BlockSpec((tm, tn), lambda i, j, kk: (i, j)))
        args.append(residual)
    grid = (m // tm, n // tn, nk)
    n_x = len(exchange)
    out, *landed = pl.pallas_call(
        _with_exchange(body, len(args), 1, False, n_x, grid),
        name=name,
        grid=grid,
        in_specs=in_specs + [_ANY] * n_x,
        out_specs=[pl.BlockSpec((tm, tn), lambda i, j, kk: (i, j))] + [_ANY] * n_x,
        out_shape=[jax.ShapeDtypeStruct((m, n), out_dtype)] + _chip_shapes(False, exchange),
        scratch_shapes=[pltpu.VMEM((tm, tn), F32)] + (_chip_scratch(n_x) if n_x else []),
        compiler_params=_params(*(("arbitrary",) * 3 if n_x else ("parallel", "parallel", "arbitrary"))),
    )(*args, *exchange)
    return (out, landed) if n_x else out


def _rms(x, w):
    return x * lax.rsqrt(jnp.mean(x * x, axis=-1, keepdims=True) + EPS) * w


def _rmsnorm_fwd(x, w_row, *, name, tr=512):
    t, d = x.shape
    tr = min(tr, t)

    def body(x_ref, w_ref, o_ref):
        o_ref[...] = _rms(x_ref[...], w_ref[...]).astype(BF16)

    return pl.pallas_call(
        body,
        name=name,
        grid=(t // tr,),
        in_specs=[pl.BlockSpec((tr, d), lambda i: (i, 0)), pl.BlockSpec((1, d), lambda i: (0, 0))],
        out_specs=pl.BlockSpec((tr, d), lambda i: (i, 0)),
        out_shape=jax.ShapeDtypeStruct((t, d), BF16),
        compiler_params=_params("parallel"),
    )(x, w_row)


def _rmsnorm_bwd(x, w_row, dy, dres, *, name, tr=256):
    t, d = x.shape
    tr = min(tr, t)

    def body(x_ref, w_ref, dy_ref, dres_ref, dx_ref, dxb_ref, dw_ref):
        @pl.when(pl.program_id(0) == 0)
        def _():
            dw_ref[...] = jnp.zeros_like(dw_ref)

        _, vjp = jax.vjp(_rms, x_ref[...], w_ref[...])
        dx, dw = vjp(dy_ref[...])
        dx = dx + dres_ref[...]
        dx_ref[...] = dx
        dxb_ref[...] = dx.astype(BF16)
        dw_ref[...] += dw

    row = pl.BlockSpec((tr, d), lambda i: (i, 0))
    vec = pl.BlockSpec((1, d), lambda i: (0, 0))
    return pl.pallas_call(
        body,
        name=name,
        grid=(t // tr,),
        in_specs=[row, vec, row, row],
        out_specs=[row, row, vec],
        out_shape=[jax.ShapeDtypeStruct((t, d), F32), jax.ShapeDtypeStruct((t, d), BF16), jax.ShapeDtypeStruct((1, d), F32)],
        compiler_params=_params("arbitrary"),
    )(x, w_row, dy, dres)


def _loss_head(h, target, *, name, tr=512):
    t, d = h.shape
    tr = min(tr, t)

    def body(h_ref, t_ref, dh_ref, dhb_ref, part_ref):
        @pl.when(pl.program_id(0) == 0)
        def _():
            part_ref[...] = jnp.zeros_like(part_ref)

        err = h_ref[...] - t_ref[...]
        dh = err * (1.0 / d)
        dh_ref[...] = dh
        dhb_ref[...] = dh.astype(BF16)
        part_ref[...] += jnp.sum(err * err, axis=0, keepdims=True)

    row = pl.BlockSpec((tr, d), lambda i: (i, 0))
    vec = pl.BlockSpec((1, d), lambda i: (0, 0))
    dh, dhb, part = pl.pallas_call(
        body,
        name=name,
        grid=(t // tr,),
        in_specs=[row, row],
        out_specs=[row, row, vec],
        out_shape=[jax.ShapeDtypeStruct((t, d), F32), jax.ShapeDtypeStruct((t, d), BF16), jax.ShapeDtypeStruct((1, d), F32)],
        compiler_params=_params("arbitrary"),
    )(h, target)
    return 0.5 / d * jnp.sum(part), dh, dhb


_BNN = (((2,), (1,)), ((0,), (0,)))
_BNT = (((2,), (2,)), ((0,), (0,)))
_BTN = (((1,), (1,)), ((0,), (0,)))


_TAP0 = HALO - (CONV_K - 1)


def _conv(x_ref, w, rows):
    c = w[0:1, :] * x_ref[_TAP0:_TAP0 + rows, :]
    for j in range(1, CONV_K):
        c = c + w[j:j + 1, :] * x_ref[_TAP0 + j:_TAP0 + j + rows, :]
    return c


def _conv_silu_bwd(x_ref, w, dact, dc_ref, rows):
    c = _conv(x_ref, w, rows)
    sig = jax.nn.sigmoid(c)
    dc = dact * (sig * (1.0 + c * (1.0 - sig)))
    dw = [jnp.sum(dc * x_ref[_TAP0 + j:_TAP0 + j + rows, :], axis=0, keepdims=True) for j in range(CONV_K)]
    dc_ref[0:HALO, :] = jnp.zeros((HALO, HEAD_DIM), F32)
    dc_ref[HALO:HALO + rows, :] = dc
    dc_ref[HALO + rows:HALO + rows + HALO, :] = jnp.zeros((HALO, HEAD_DIM), F32)
    first = HALO - _TAP0
    dx = w[0:1, :] * dc_ref[first:first + HALO + rows, :]
    for j in range(1, CONV_K):
        dx = dx + w[j:j + 1, :] * dc_ref[first - j:first - j + HALO + rows, :]
    return dx, dw


def _gdn_intra(qt, kt, v, a, b, alog, dtb):
    n = a.shape[0] // CHUNK
    q = qt * lax.rsqrt(jnp.sum(qt * qt, axis=-1, keepdims=True) + EPS) * (HEAD_DIM ** -0.5)
    k = kt * lax.rsqrt(jnp.sum(kt * kt, axis=-1, keepdims=True) + EPS)
    lanes = jnp.ones((1, HEAD_DIM), F32)
    beta = jax.nn.sigmoid(b) * lanes
    sp = a + dtb
    g = (-jnp.exp(alog) * (jnp.maximum(sp, 0.0) + jnp.log(1.0 + jnp.exp(-jnp.abs(sp))))) * lanes
    q, k, v, beta, g = (t.reshape(n, CHUNK, HEAD_DIM) for t in (q, k, v, beta, g))

    row = lax.broadcasted_iota(jnp.int32, (n, CHUNK, CHUNK), 1)
    col = lax.broadcasted_iota(jnp.int32, (n, CHUNK, CHUNK), 2)
    tri_incl = row >= col
    tri_strict = row > col
    gc = _fdot(tri_incl.astype(F32), g, _BNN)
    gc_row = _fdot(g[:, :, :CHUNK], (row <= col).astype(F32), _BTN)
    decay = jnp.exp(jnp.where(tri_incl, gc[:, :, :CHUNK] - gc_row, -1e30))
    kb = k * beta
    vb = v * beta
    with_k = _bdot(jnp.concatenate([kb, q], axis=1), k, _BNT)
    neg_l = jnp.where(tri_strict, -(with_k[:, :CHUNK] * decay), 0.0)
    qk = jnp.where(tri_incl, with_k[:, CHUNK:] * decay, 0.0)
    inv = (row == col).astype(F32) + neg_l
    power = _bdot(neg_l, neg_l, _BNN)
    for _ in range(4):
        both = _bdot(jnp.concatenate([inv, power], axis=1), power, _BNN)
        inv, power = inv + both[:, :CHUNK], both[:, CHUNK:]
    inv = inv + _bdot(inv, power, _BNN)
    e = jnp.exp(gc)
    solved = _bdot(inv, jnp.concatenate([kb * e, vb], axis=2), _BNN)
    g_last = gc[:, CHUNK - 1:CHUNK, :]
    k_dec = k * jnp.exp(g_last - gc)
    from_k = _bdot(k_dec, solved, _BTN)
    from_qk = _bdot(qk, solved, _BNN)
    step, add = -from_k[:, :, :HEAD_DIM], from_k[:, :, HEAD_DIM:]
    read, out = q * e - from_qk[:, :, :HEAD_DIM], from_qk[:, :, HEAD_DIM:]
    return step, add, jnp.exp(g_last), read, out


def _gdn_scan_step(state, step, add, decay_last):
    return state * decay_last + _bdot(step, state) + add


def _gdn_outputs(states, read, out, z, onw):
    return _rms(_bdot(read, states, _BNN) + out, onw) * _silu(z)


def _scan_scratch(n, dtype):
    return [pltpu.VMEM((n, HEAD_DIM, HEAD_DIM), dtype), pltpu.VMEM((n, HEAD_DIM, HEAD_DIM), F32), pltpu.VMEM((n, 1, HEAD_DIM), F32)]


def _head_lane(h, offset=0):
    return lax.broadcasted_iota(jnp.int32, (1, LANES), 1) == h + offset


def _pick(mask, x):
    return jnp.sum(jnp.where(mask, x, 0.0), axis=1, keepdims=True)


def _gdn_specs(heads, tb, rev, nb, PAIR):
    assert heads % PAIR == 0
    blk = (lambda i: nb - 1 - i) if rev else (lambda i: i)
    hb = tb // HALO
    width, pairs = PAIR * HEAD_DIM, heads // PAIR

    def col(group):
        return pl.BlockSpec((tb, width), lambda i, h: (blk(i), group * pairs + h))

    def halo(group):
        return pl.BlockSpec((HALO, width), lambda i, h: (jnp.maximum(blk(i) * hb - 1, 0), group * pairs + h))

    def convw(group):
        return pl.BlockSpec((CONV_K, width), lambda i, h: (0, group * pairs + h))

    vec = pl.BlockSpec((1, LANES), lambda i, h: (0, 0))
    ab = pl.BlockSpec((tb, LANES), lambda i, h: (blk(i), 0))
    states = pl.BlockSpec((PAIR, tb // CHUNK, HEAD_DIM, HEAD_DIM), lambda i, h: (h, blk(i), 0, 0))
    return blk, col, halo, convw, vec, ab, states


def _head_cols(p):
    return slice(p * HEAD_DIM, (p + 1) * HEAD_DIM)


def _gdn_fwd(proj, ab, conv_w, alog_row, dtb_row, onw_row, *, heads, name, tb=1024, pair=2, gather=()):
    t = proj.shape[0]
    tb = min(tb, t)
    nb, cpb = t // tb, tb // CHUNK
    PAIR = pair
    _, col, halo, convw, vec, abspec, states = _gdn_specs(heads, tb, False, nb, PAIR)

    def body(q_ref, k_ref, v_ref, qh_ref, kh_ref, vh_ref, z_ref, ab_ref, wq_ref, wk_ref, wv_ref, alog_ref, dtb_ref, onw_ref,
             og_ref, st_ref, state_scr, x_scr, *op_scr):
        i, pair = pl.program_id(0), pl.program_id(1)
        abv = ab_ref[...]
        heads_here, later = [pair * PAIR + p for p in range(PAIR)], []
        for p, h in enumerate(heads_here):
            cols = _head_cols(p)
            for n, (ref, href) in enumerate(((q_ref, qh_ref), (k_ref, kh_ref), (v_ref, vh_ref))):
                x_scr[p, n, 0:HALO, :] = jnp.where(i > 0, href[:, cols], 0.0)
                x_scr[p, n, HALO:HALO + tb, :] = ref[:, cols]
            sel_a, sel_b = _head_lane(h), _head_lane(h, heads)
            alog, dtb = _pick(sel_a, alog_ref[...]), _pick(sel_a, dtb_ref[...])
            acts = [_silu(_conv(x_scr.at[p, n], w_ref[:, cols], tb)) for n, w_ref in enumerate((wq_ref, wk_ref, wv_ref))]
            *scan, read, out = _gdn_intra(*acts, _pick(sel_a, abv), _pick(sel_b, abv), alog, dtb)
            for scr, val in zip(op_scr[3 * p:3 * p + 3], scan):
                scr[...] = val.astype(scr.dtype)
            later.append((read, out))

        def chunk(c, states):
            for p in range(PAIR):
                st_ref[p, c] = states[p]
            return tuple(_gdn_scan_step(states[p], *[scr[c] for scr in op_scr[3 * p:3 * p + 3]]) for p in range(PAIR))

        @pl.when(i == 0)
        def _():
            for h in heads_here:
                state_scr[h] = jnp.zeros((HEAD_DIM, HEAD_DIM), F32)

        last = lax.fori_loop(0, cpb, chunk, tuple(state_scr[h] for h in heads_here))
        for p, h in enumerate(heads_here):
            cols = _head_cols(p)
            state_scr[h] = last[p]
            og = _gdn_outputs(st_ref[p], *later[p], z_ref[:, cols].reshape(cpb, CHUNK, HEAD_DIM), onw_ref[...])
            og_ref[:, cols] = og.reshape(tb, HEAD_DIM).astype(BF16)

    n_x = len(gather)
    grid = (nb, heads // PAIR)
    og, st, *gathered = pl.pallas_call(
        _with_exchange(body, 14, 2, True, n_x, grid),
        name=name,
        grid=grid,
        in_specs=[col(0), col(1), col(2), halo(0), halo(1), halo(2), col(3), abspec, convw(0), convw(1), convw(2), vec, vec, vec]
        + [_ANY] * n_x,
        out_specs=[pl.BlockSpec((tb, PAIR * HEAD_DIM), lambda i, h: (i, h)), states] + [_ANY] * n_x,
        out_shape=[jax.ShapeDtypeStruct((t, heads * HEAD_DIM), BF16),
                   jax.ShapeDtypeStruct((heads, t // CHUNK, HEAD_DIM, HEAD_DIM), F32)] + _chip_shapes(True, gather),
        scratch_shapes=[pltpu.VMEM((heads, HEAD_DIM, HEAD_DIM), F32), pltpu.VMEM((PAIR, 3, HALO + tb, HEAD_DIM), F32)]
        + _scan_scratch(cpb, BF16) * PAIR + (_chip_scratch(n_x) if n_x else []),
        compiler_params=_params("arbitrary", "arbitrary"),
    )(proj, proj, proj, proj, proj, proj, proj, ab, conv_w, conv_w, conv_w, alog_row, dtb_row, onw_row, *gather)
    return og, st, gathered


def _gdn_bwd(proj, ab, conv_w, alog_row, dtb_row, onw_row, states, dog, *, heads, name, tb=1024, pair=1, exchange=()):
    t = proj.shape[0]
    tb = min(tb, t)
    nb, cpb = t // tb, tb // CHUNK
    PAIR = pair
    _, col, halo, convw, vec, abspec, states_spec = _gdn_specs(heads, tb, True, nb, PAIR)
    n_conv = conv_w.shape[1]

    def body(q_ref, k_ref, v_ref, qh_ref, kh_ref, vh_ref, z_ref, ab_ref, wq_ref, wk_ref, wv_ref, alog_ref, dtb_ref, onw_ref,
             st_ref, dog_ref, dq_ref, dk_ref, dv_ref, dz_ref, dab_ref, dconv_ref, dalog_ref, ddtb_ref, donw_ref,
             dstate_scr, x_scr, carry_scr, *scr):
        op_scr, dop_scr, dstates_scr, dc_scr = scr[:3 * PAIR], scr[3 * PAIR:6 * PAIR], scr[6 * PAIR:7 * PAIR], scr[7 * PAIR]
        i, pair = pl.program_id(0), pl.program_id(1)
        first_block = i == nb - 1
        heads_here, later = [pair * PAIR + p for p in range(PAIR)], []

        @pl.when(jnp.logical_and(i == 0, pair == 0))
        def _():
            dconv_ref[...] = jnp.zeros_like(dconv_ref)
            dalog_ref[...] = jnp.zeros_like(dalog_ref)
            ddtb_ref[...] = jnp.zeros_like(ddtb_ref)
            donw_ref[...] = jnp.zeros_like(donw_ref)

        @pl.when(pair == 0)
        def _():
            dab_ref[...] = jnp.zeros_like(dab_ref)

        @pl.when(i == 0)
        def _():
            for h in heads_here:
                dstate_scr[h] = jnp.zeros((HEAD_DIM, HEAD_DIM), F32)
                carry_scr[h] = jnp.zeros((3, HALO, HEAD_DIM), F32)

        abv = ab_ref[...]
        w_refs = (wq_ref, wk_ref, wv_ref)
        for p, h in enumerate(heads_here):
            cols = _head_cols(p)
            for n, (ref, href) in enumerate(((q_ref, qh_ref), (k_ref, kh_ref), (v_ref, vh_ref))):
                x_scr[p, n, 0:HALO, :] = jnp.where(first_block, 0.0, href[:, cols])
                x_scr[p, n, HALO:HALO + tb, :] = ref[:, cols]
            sel_a, sel_b = _head_lane(h), _head_lane(h, heads)
            alog, dtb = _pick(sel_a, alog_ref[...]), _pick(sel_a, dtb_ref[...])
            acts = [_silu(_conv(x_scr.at[p, n], w_ref[:, cols], tb)) for n, w_ref in enumerate(w_refs)]
            (*scan, read, out), vjp_intra = jax.vjp(_gdn_intra, *acts, _pick(sel_a, abv), _pick(sel_b, abv), alog, dtb)
            for s, val in zip(op_scr[3 * p:3 * p + 3], scan):
                s[...] = val.astype(s.dtype)
            blocked = lambda ref: ref[:, cols].reshape(cpb, CHUNK, HEAD_DIM)
            _, vjp_outputs = jax.vjp(_gdn_outputs, st_ref[p], read, out, blocked(z_ref), onw_ref[...])
            dstates_scr[p][...], dread, dout, dz, donw = vjp_outputs(blocked(dog_ref))
            dz_ref[:, cols] = dz.reshape(tb, HEAD_DIM).astype(BF16)
            donw_ref[...] += donw
            later.append((vjp_intra, dread, dout, sel_a, sel_b))

        def chunk(i_rev, dstates):
            c = cpb - 1 - i_rev
            new = []
            for p in range(PAIR):
                _, vjp = jax.vjp(_gdn_scan_step, st_ref[p, c], *[s[c].astype(F32) for s in op_scr[3 * p:3 * p + 3]])
                dstate, *grads = vjp(dstates[p])
                for s, val in zip(dop_scr[3 * p:3 * p + 3], grads):
                    s[c] = val
                new.append(dstate + dstates_scr[p][c])
            return tuple(new)

        last = lax.fori_loop(0, cpb, chunk, tuple(dstate_scr[h] for h in heads_here))
        for p, h in enumerate(heads_here):
            cols = _head_cols(p)
            vjp_intra, dread, dout, sel_a, sel_b = later[p]
            dstate_scr[h] = last[p]
            *dacts, da, db, dalog, ddtb = vjp_intra((*[s[...] for s in dop_scr[3 * p:3 * p + 3]], dread, dout))
            dab_ref[...] += jnp.where(sel_a, da, 0.0) + jnp.where(sel_b, db, 0.0)
            for n, (dref, dact, w_ref) in enumerate(zip((dq_ref, dk_ref, dv_ref), dacts, w_refs)):
                dx, dw = _conv_silu_bwd(x_scr.at[p, n], w_ref[:, cols], dact, dc_scr, tb)
                x_scr[p, n] = dx
                x_scr[p, n, tb:tb + HALO, :] += carry_scr[h, n]
                carry_scr[h, n] = x_scr[p, n, 0:HALO, :]
                dref[:, cols] = x_scr[p, n, HALO:HALO + tb, :].astype(BF16)
                lanes = pl.ds(pl.multiple_of((n * heads + h) * HEAD_DIM, HEAD_DIM), HEAD_DIM)
                for j in range(CONV_K):
                    dconv_ref[j:j + 1, lanes] += dw[j]
            dalog_ref[...] += jnp.where(sel_a, dalog, 0.0)
            ddtb_ref[...] += jnp.where(sel_a, ddtb, 0.0)

    out_col = pl.BlockSpec((tb, PAIR * HEAD_DIM), lambda i, h: (nb - 1 - i, h))
    dog_spec = out_col
    col_shape = jax.ShapeDtypeStruct((t, heads * HEAD_DIM), BF16)
    row_shape = jax.ShapeDtypeStruct((1, LANES), F32)
    n_x = len(exchange)
    grid = (nb, heads // PAIR)
    outs = pl.pallas_call(
        _with_exchange(body, 16, 9, False, n_x, grid),
        name=name,
        grid=grid,
        in_specs=[col(0), col(1), col(2), halo(0), halo(1), halo(2), col(3), abspec, convw(0), convw(1), convw(2), vec, vec, vec,
                  states_spec, dog_spec] + [_ANY] * n_x,
        out_specs=[out_col, out_col, out_col, out_col, abspec,
                   pl.BlockSpec((CONV_K, n_conv), lambda i, h: (0, 0)), vec, vec, vec] + [_ANY] * n_x,
        out_shape=[col_shape, col_shape, col_shape, col_shape, jax.ShapeDtypeStruct((t, LANES), F32),
                   jax.ShapeDtypeStruct((CONV_K, n_conv), F32), row_shape, row_shape, row_shape] + _chip_shapes(False, exchange),
        scratch_shapes=[pltpu.VMEM((heads, HEAD_DIM, HEAD_DIM), F32), pltpu.VMEM((PAIR, 3, HALO + tb, HEAD_DIM), F32),
                        pltpu.VMEM((heads, 3, HALO, HEAD_DIM), F32)] + _scan_scratch(cpb, BF16) * PAIR
        + _scan_scratch(cpb, F32) * PAIR + [pltpu.VMEM((cpb, HEAD_DIM, HEAD_DIM), F32)] * PAIR
        + [pltpu.VMEM((HALO + tb + HALO, HEAD_DIM), F32)]
        + (_chip_scratch(n_x) if n_x else []),
        compiler_params=_params("arbitrary", "arbitrary"),
    )(proj, proj, proj, proj, proj, proj, proj, ab, conv_w, conv_w, conv_w, alog_row, dtb_row, onw_row, states, dog, *exchange)
    return (*outs[:9], outs[9:])


BAND = (LEFT_CHUNKS + 1) * CHUNK
PAD = LEFT_CHUNKS * CHUNK
GROUP = 2
ROWS = GROUP * CHUNK
WIN = (LEFT_CHUNKS + GROUP) * CHUNK
DIAGS = WIN + ROWS - 1
NEAR = PAD + ROWS - 1 - REL_CLIP
assert 0 < NEAR < DIAGS and WIN - PAD - 1 <= REL_CLIP and WIN % LANES == 0
ATTN_BLOCK = 1024


def _band_bias(rel_bias):
    heads = rel_bias.shape[0]
    far = jnp.broadcast_to(rel_bias[:, 2 * REL_CLIP:], (heads, NEAR + 1))
    near = rel_bias[:, 2 * REL_CLIP + NEAR + 1 - DIAGS:2 * REL_CLIP][:, ::-1]
    diag = jnp.concatenate([far, near], axis=1)
    return jnp.stack([diag[:, ROWS - 1 - r:ROWS - 1 - r + WIN] for r in range(ROWS)], axis=1)


def _band_bias_grad(dbias):
    heads = dbias.shape[0]
    diag = sum(jnp.pad(dbias[:, r, :], ((0, 0), (ROWS - 1 - r, r))) for r in range(ROWS))
    far = jnp.sum(diag[:, :NEAR + 1], axis=1, keepdims=True)
    near = diag[:, NEAR + 1:][:, ::-1]
    unused = jnp.zeros((heads, 2 * REL_CLIP - near.shape[1]), F32)
    return jnp.concatenate([unused, near, far], axis=1)


def _masked_bias(bias, n):
    r = np.arange(ROWS)[:, None]
    key = np.arange(WIN)[None, :]
    band_start = (r // CHUNK) * CHUNK
    in_band = np.logical_and(key >= band_start, key < band_start + BAND)
    in_sequence = key[None] >= PAD - np.arange(n)[:, None, None] * ROWS
    first = jnp.where(np.logical_and(in_band[None], in_sequence)[None], bias[:, None], -1e30)
    return first, jnp.where(in_band[None, None], bias[:, None], -1e30)


def _attn_groups(q_pre, z, kn, v, bias, qnw):
    q = _rms(q_pre, qnw)
    s = _bdot(q, kn, _BNT) * (HEAD_DIM ** -0.5) + bias
    p = jnp.exp(s - jnp.max(s, axis=-1, keepdims=True))
    p = p / jnp.sum(p, axis=-1, keepdims=True)
    return _bdot(p, v, _BNN) * _silu(z)


def _attn_groups_bwd(q_pre, z, kn, v, bias, qnw, dog):
    scale = HEAD_DIM ** -0.5
    inv_rms = lax.rsqrt(jnp.mean(q_pre * q_pre, axis=-1, keepdims=True) + EPS)
    q_hat = q_pre * inv_rms
    q_b = (q_hat * qnw).astype(BF16)
    s = _dot(q_b, kn, _BNT) * scale + bias
    e = jnp.exp(s - jnp.max(s, axis=-1, keepdims=True))
    p = e * (1.0 / jnp.sum(e, axis=-1, keepdims=True))
    p_b = p.astype(BF16)
    o = _dot(p_b, v, _BNN)
    sig = jax.nn.sigmoid(z)
    do = dog * (z * sig)
    dz = dog * o * (sig * (1.0 + z * (1.0 - sig)))
    do_b = do.astype(BF16)
    dv = _dot(p_b, do_b, _BTN)
    dp = _dot(do_b, v, _BNT)
    ds = p * (dp - jnp.sum(do * o, axis=-1, keepdims=True))
    ds_b = (ds * scale).astype(BF16)
    dq = _dot(ds_b, kn, _BNN)
    dkn = _dot(ds_b, q_b, _BTN)
    dqnw = jnp.sum(jnp.sum(dq * q_hat, axis=0), axis=0, keepdims=True)
    dq_hat = dq * qnw
    dq_pre = inv_rms * (dq_hat - q_hat * jnp.mean(dq_hat * q_hat, axis=-1, keepdims=True))
    return dq_pre, dz, dkn, dv, jnp.sum(ds, axis=0), dqnw


def _attn_specs(heads, tb, t):
    def col(group):
        return pl.BlockSpec((tb, HEAD_DIM), lambda h, i: (i, group * heads + h))

    def full(group):
        return pl.BlockSpec((t, HEAD_DIM), lambda h, i: (0, group * heads + h))

    bias = [pl.BlockSpec((1, tb // ROWS, ROWS, WIN), lambda h, i: (h, 0, 0, 0)),
            pl.BlockSpec((1, 1, ROWS, WIN), lambda h, i: (h, 0, 0, 0))]
    vec = pl.BlockSpec((1, HEAD_DIM), lambda h, i: (0, 0))
    return col, full, bias, vec


def _attn_windows(scr, block_start, n):
    return jnp.stack([scr[pl.ds(pl.multiple_of(block_start + g * ROWS, ROWS), WIN), :] for g in range(n)])


def _attn_fill(k_ref, v_ref, knw_ref, kn_scr, v_scr, t):
    kn_scr[0:PAD, :] = jnp.zeros((PAD, HEAD_DIM), BF16)
    v_scr[0:PAD, :] = jnp.zeros((PAD, HEAD_DIM), BF16)
    step = min(512, t)

    def fill(j, _):
        rows = pl.ds(pl.multiple_of(j * step, step), step)
        prows = pl.ds(pl.multiple_of(PAD + j * step, CHUNK), step)
        kn_scr[prows, :] = _rms(k_ref[rows, :], knw_ref[...]).astype(BF16)
        v_scr[prows, :] = v_ref[rows, :].astype(BF16)
        return 0

    lax.fori_loop(0, t // step, fill, 0)


def _attn_fwd(proj, bias, qnw_row, knw_row, *, heads, name, tb=ATTN_BLOCK):
    t = proj.shape[0]
    tb = min(tb, t)
    nb, ng = t // tb, tb // ROWS
    col, full, bias_spec, vec = _attn_specs(heads, tb, t)

    def body(q_ref, k_ref, v_ref, z_ref, first_ref, rest_ref, qnw_ref, knw_ref, og_ref, kn_scr, v_scr):
        i = pl.program_id(1)

        @pl.when(i == 0)
        def _():
            _attn_fill(k_ref, v_ref, knw_ref, kn_scr, v_scr, t)

        def run(block_bias):
            start = i * tb
            og = _attn_groups(q_ref[...].reshape(ng, ROWS, HEAD_DIM), z_ref[...].reshape(ng, ROWS, HEAD_DIM),
                              _attn_windows(kn_scr, start, ng), _attn_windows(v_scr, start, ng), block_bias, qnw_ref[...])
            og_ref[...] = og.reshape(tb, HEAD_DIM).astype(BF16)

        pl.when(i == 0)(lambda: run(first_ref[0]))
        pl.when(i > 0)(lambda: run(rest_ref[0]))

    return pl.pallas_call(
        body,
        name=name,
        grid=(heads, nb),
        in_specs=[col(0), full(1), full(2), col(3), *bias_spec, vec, vec],
        out_specs=pl.BlockSpec((tb, HEAD_DIM), lambda h, i: (i, h)),
        out_shape=jax.ShapeDtypeStruct((t, heads * HEAD_DIM), BF16),
        scratch_shapes=[pltpu.VMEM((PAD + t, HEAD_DIM), BF16), pltpu.VMEM((PAD + t, HEAD_DIM), BF16)],
        compiler_params=_params("arbitrary", "arbitrary"),
    )(proj, proj, proj, proj, *bias, qnw_row, knw_row)


def _attn_bwd(proj, bias, qnw_row, knw_row, dog, *, heads, name, tb=ATTN_BLOCK, sub=4):
    t = proj.shape[0]
    tb = min(tb, t)
    nb, ng = t // tb, tb // ROWS
    sub = min(sub, ng)
    col, full, bias_spec, vec = _attn_specs(heads, tb, t)

    def body(q_ref, k_ref, v_ref, z_ref, first_ref, rest_ref, qnw_ref, knw_ref, dog_ref,
             dq_ref, dk_ref, dv_ref, dz_ref, dbias_ref, dqnw_ref, dknw_ref, kn_scr, v_scr, dkn_scr, dv_scr):
        i = pl.program_id(1)

        @pl.when(i == 0)
        def _():
            _attn_fill(k_ref, v_ref, knw_ref, kn_scr, v_scr, t)
            dkn_scr[...] = jnp.zeros_like(dkn_scr)
            dv_scr[...] = jnp.zeros_like(dv_scr)
            dbias_ref[...] = jnp.zeros_like(dbias_ref)
            dqnw_ref[...] = jnp.zeros_like(dqnw_ref)

        def run(block_bias):
            for g0 in range(0, ng, sub):
                rows = pl.ds(g0 * ROWS, sub * ROWS)
                at = i * tb + g0 * ROWS
                blocked = lambda ref: ref[rows, :].reshape(sub, ROWS, HEAD_DIM)
                dq, dz, dkn, dv, dbias, dqnw = _attn_groups_bwd(
                    blocked(q_ref), blocked(z_ref), _attn_windows(kn_scr, at, sub), _attn_windows(v_scr, at, sub),
                    block_bias(g0), qnw_ref[...], blocked(dog_ref))
                dq_ref[rows, :] = dq.reshape(sub * ROWS, HEAD_DIM).astype(BF16)
                dz_ref[rows, :] = dz.reshape(sub * ROWS, HEAD_DIM).astype(BF16)
                for g in range(sub):
                    window = pl.ds(pl.multiple_of(at + g * ROWS, ROWS), WIN)
                    dkn_scr[window, :] += dkn[g]
                    dv_scr[window, :] += dv[g]
                dbias_ref[0] += dbias
                dqnw_ref[0] += dqnw

        pl.when(i == 0)(lambda: run(lambda g0: first_ref[0, g0:g0 + sub]))
        pl.when(i > 0)(lambda: run(lambda g0: rest_ref[0]))

        @pl.when(i == nb - 1)
        def _():
            step = min(512, t)

            def finish(j, dknw):
                rows = pl.ds(pl.multiple_of(j * step, step), step)
                prows = pl.ds(pl.multiple_of(PAD + j * step, CHUNK), step)
                _, vjp = jax.vjp(_rms, k_ref[rows, :], knw_ref[...])
                dk, dw = vjp(dkn_scr[prows, :])
                dk_ref[rows, :] = dk.astype(BF16)
                dv_ref[rows, :] = dv_scr[prows, :].astype(BF16)
                return dknw + dw

            dknw_ref[0] = lax.fori_loop(0, t // step, finish, jnp.zeros((1, HEAD_DIM), F32))

    out_col = pl.BlockSpec((tb, HEAD_DIM), lambda h, i: (i, h))
    out_full = pl.BlockSpec((t, HEAD_DIM), lambda h, i: (0, h))
    head_vec = pl.BlockSpec((1, 1, HEAD_DIM), lambda h, i: (h, 0, 0))
    col_shape = jax.ShapeDtypeStruct((t, heads * HEAD_DIM), BF16)
    vec_shape = jax.ShapeDtypeStruct((heads, 1, HEAD_DIM), F32)
    return pl.pallas_call(
        body,
        name=name,
        grid=(heads, nb),
        in_specs=[col(0), full(1), full(2), col(3), *bias_spec, vec, vec, pl.BlockSpec((tb, HEAD_DIM), lambda h, i: (i, h))],
        out_specs=[out_col, out_full, out_full, out_col, pl.BlockSpec((1, ROWS, WIN), lambda h, i: (h, 0, 0)), head_vec,
                   head_vec],
        out_shape=[col_shape, col_shape, col_shape, col_shape, jax.ShapeDtypeStruct((heads, ROWS, WIN), F32),
                   vec_shape, vec_shape],
        scratch_shapes=[pltpu.VMEM((PAD + t, HEAD_DIM), BF16), pltpu.VMEM((PAD + t, HEAD_DIM), BF16),
                        pltpu.VMEM((PAD + t, HEAD_DIM), F32), pltpu.VMEM((PAD + t, HEAD_DIM), F32)],
        compiler_params=_params("arbitrary", "arbitrary"),
    )(proj, proj, proj, proj, *bias, qnw_row, knw_row, dog)


def _lane_row(v):
    v = v.reshape(1, -1)
    return jnp.pad(v, ((0, 0), (0, LANES - v.shape[1])))


def _local_step(x, target, norm_w, wa_in, conv_w, a_log, dt_bias, onw, wa_out, wb_in, qnw, knw, rel_bias, wb_out, *,
                sharded=False):
    ha, hb = a_log.shape[-1], rel_bias.shape[-2]
    na = 4 * ha * HEAD_DIM
    wa_main = wa_in[:, :na]
    wa_ab = jnp.pad(wa_in[:, na:], ((0, 0), (0, LANES - 2 * ha)))
    alog_row, dtb_row, onw_row = _lane_row(a_log), _lane_row(dt_bias), _lane_row(onw)
    qnw_row, knw_row = _lane_row(qnw), _lane_row(knw)
    bias = _masked_bias(_band_bias(rel_bias.reshape(hb, -1)), min(ATTN_BLOCK, x.shape[0]) // ROWS)

    hn0 = _rmsnorm_fwd(x, norm_w[0:1], name="norm0")
    proj_a = _matmul(hn0, wa_main, name="a_in")
    ab_a = _matmul(hn0, wa_ab, name="a_in_ab")
    og_a, states, got = _gdn_fwd(proj_a, ab_a, conv_w, alog_row, dtb_row, onw_row, heads=ha, name="gdn_fwd",
                                 gather=[wb_in, wa_out, wb_out] if sharded else [])
    if sharded:
        wb_in, wa_out, wb_out = _join_cols(got[0]), got[1].reshape(-1, got[1].shape[-1]), got[2].reshape(-1, got[2].shape[-1])
    h1 = _matmul(og_a, wa_out, residual=x, name="a_out")
    hn1 = _rmsnorm_fwd(h1, norm_w[1:2], name="norm1")
    proj_b = _matmul(hn1, wb_in, name="b_in")
    og_b = _attn_fwd(proj_b, bias, qnw_row, knw_row, heads=hb, name="attn_fwd")
    h2 = _matmul(og_b, wb_out, residual=h1, name="b_out")
    loss, dh2, dh2_b = _loss_head(h2, target, name="loss_head")

    dog_b = _matmul(dh2_b, wb_out, trans_b=True, name="d_b_out_x")
    dwb_out = _matmul(og_b.T, dh2_b, name="d_b_out_w")
    dq, dk, dv, dz, dbias, dqnw, dknw = _attn_bwd(proj_b, bias, qnw_row, knw_row, dog_b, heads=hb, name="attn_bwd")
    dproj_b = jnp.concatenate([dq, dk, dv, dz], axis=1)
    dhn1 = _matmul(dproj_b, wb_in, trans_b=True, name="d_b_in_x")
    dwb_in = _matmul(hn1.T, dproj_b, name="d_b_in_w")
    dh1, dh1_b, dnw1 = _rmsnorm_bwd(h1, norm_w[1:2], dhn1, dh2, name="d_norm1")

    dog_a = _matmul(dh1_b, wa_out, trans_b=True, name="d_a_out_x")
    dwa_out = _matmul(og_a.T, dh1_b, name="d_a_out_w")
    early = [_split_cols(dwb_in), _split_rows(dwa_out), _split_rows(dwb_out)] if sharded else []
    dq, dk, dv, dz, dab, dconv, dalog, ddtb, donw, landed = _gdn_bwd(
        proj_a, ab_a, conv_w, alog_row, dtb_row, onw_row, states, dog_a, heads=ha, name="gdn_bwd",
        exchange=[s.astype(BF16) for s in early])
    if sharded:
        dwb_in, dwa_out, dwb_out = landed
    dproj_a = jnp.concatenate([dq, dk, dv, dz], axis=1)
    dab_b = dab.astype(BF16)
    hn0_t = hn0.T
    dwa_in = jnp.concatenate(
        [_matmul(hn0_t, dproj_a, name="d_a_in_w"), _matmul(hn0_t, dab_b, name="d_a_in_ab_w")[:, :2 * ha]], axis=1)
    if sharded:
        dhn0, (dwa_in, dconv) = _matmul(dproj_a, wa_main, trans_b=True, name="d_a_in_x",
                                        exchange=[_split_cols(dwa_in).astype(BF16), _split_cols(dconv)])
    else:
        dhn0 = _matmul(dproj_a, wa_main, trans_b=True, name="d_a_in_x")
    dhn0 = _matmul(dab_b, wa_ab, trans_b=True, residual=dhn0, name="d_a_in_ab_x")
    dx, _, dnw0 = _rmsnorm_bwd(x, norm_w[0:1], dhn0, dh1, name="d_norm0")

    drel = _band_bias_grad(dbias)
    grads = dict(
        norm_w=jnp.concatenate([dnw0, dnw1], axis=0), a_w_in=dwa_in, a_conv_w=dconv, a_a_log=dalog[:, :ha],
        a_dt_bias=ddtb[:, :ha], a_out_norm_w=donw, a_w_out=dwa_out, b_w_in=dwb_in, b_q_norm_w=jnp.sum(dqnw, axis=0),
        b_k_norm_w=jnp.sum(dknw, axis=0), b_rel_bias=drel[None], b_w_out=dwb_out)
    return loss, dx, grads


_ANY = pl.BlockSpec(memory_space=pl.ANY)
_CHIP_FLIPS = ((1, 0), (0, 1), (1, 1))


def _place():
    x, y, c = lax.axis_index("x"), lax.axis_index("y"), lax.axis_index("c")
    return x, y, c


def _flip(v, bit):
    return 1 - v if bit else v


def _remote(src, dst, send_sem, recv_sem, peer):
    return pltpu.make_async_remote_copy(src_ref=src, dst_ref=dst, send_sem=send_sem, recv_sem=recv_sem, device_id=peer,
                                        device_id_type=MESH)


def _comm_call(body, arrays, out_shapes, n_remote, n_local, name):
    scratch = [pltpu.SemaphoreType.DMA((n_remote,)), pltpu.SemaphoreType.DMA((n_remote,))]
    if n_local:
        scratch.append(pltpu.SemaphoreType.DMA((n_local,)))
    return pl.pallas_call(
        body, name=name, in_specs=[_ANY] * len(arrays), out_specs=[_ANY] * len(out_shapes), out_shape=out_shapes,
        scratch_shapes=scratch)(*arrays)


def _chip_scratch(n):
    return [pltpu.SemaphoreType.DMA((3 * n,)), pltpu.SemaphoreType.DMA((3 * n,)), pltpu.SemaphoreType.DMA((n,))]


def _chip_shapes(gather, arrays):
    return [jax.ShapeDtypeStruct(((N_CHIPS,) + s.shape) if gather else s.shape, s.dtype) for s in arrays]


def _chip_traffic(gather, ins, outs, sems):
    send_sems, recv_sems, local_sems = sems
    x, y, c = _place()
    mine = 2 * x + y
    local, remote, landing = [], [], []
    for a in range(len(ins)):
        local.append(pltpu.make_async_copy(ins[a] if gather else ins[a].at[mine], outs[a].at[mine], local_sems.at[a]))
        for k, (fx, fy) in enumerate(_CHIP_FLIPS):
            peer = (_flip(x, fx), _flip(y, fy), c)
            theirs = 2 * peer[0] + peer[1]
            src = ins[a] if gather else ins[a].at[theirs]
            pair = send_sems.at[3 * a + k], recv_sems.at[3 * a + k]
            remote.append(_remote(src, outs[a].at[mine], *pair, peer))
            landing.append(_remote(src, outs[a].at[theirs], *pair, peer))
    return local + remote, (local, landing, remote)


def _start(traffic):
    for cp in traffic[0]:
        cp.start()


def _finish(traffic):
    local, landing, remote = traffic[1]
    for cp in local:
        cp.wait()
    for cp in landing:
        cp.wait_recv()
    for cp in remote:
        cp.wait_send()


def _with_exchange(compute, n_in, n_out, gather, n_x, grid):
    if not n_x:
        return compute

    def body(*refs):
        ins, x_in = refs[:n_in], refs[n_in:n_in + n_x]
        outs, x_out = refs[n_in + n_x:n_in + n_x + n_out], refs[n_in + n_x + n_out:n_in + 2 * n_x + n_out]
        scratch, sems = refs[n_in + 2 * n_x + n_out:-3], refs[-3:]
        traffic = _chip_traffic(gather, x_in, x_out, sems)
        first = functools.reduce(jnp.logical_and, [pl.program_id(d) == 0 for d in range(len(grid))])
        last = functools.reduce(jnp.logical_and, [pl.program_id(d) == grid[d] - 1 for d in range(len(grid))])

        @pl.when(first)
        def _():
            _start(traffic)

        compute(*ins, *outs, *scratch)

        @pl.when(last)
        def _():
            _finish(traffic)

    return body


def _chip_call(gather, arrays, *, name):
    n = len(arrays)

    def body(*refs):
        traffic = _chip_traffic(gather, refs[:n], refs[n:2 * n], refs[2 * n:])
        _start(traffic)
        _finish(traffic)

    return pl.pallas_call(
        body, name=name, in_specs=[_ANY] * n, out_specs=[_ANY] * n, out_shape=_chip_shapes(gather, arrays),
        scratch_shapes=_chip_scratch(n))(*arrays)


def _swap_pair(arrays, *, name):
    n = len(arrays)

    def body(*refs):
        ins, outs, (send_sems, recv_sems) = refs[:n], refs[n:2 * n], refs[2 * n:]
        x, y, c = _place()
        copies = [_remote(ins[a], outs[a], send_sems.at[a], recv_sems.at[a], (x, y, 1 - c)) for a in range(n)]
        for cp in copies:
            cp.start()
        for cp in copies:
            cp.wait_recv()
        for cp in copies:
            cp.wait_send()

    shapes = [jax.ShapeDtypeStruct(s.shape, s.dtype) for s in arrays]
    return _comm_call(body, arrays, shapes, n, 0, name)


def _gather_all(tile, *, name):
    def body(in_ref, out_ref, send_sems, recv_sems, local_sems):
        x, y, c = _place()
        mine = 4 * x + 2 * y + c
        local = pltpu.make_async_copy(in_ref, out_ref.at[mine], local_sems.at[0])
        remote, landing = [], []
        for k in range(1, N_DEV):
            peer = (_flip(x, k & 4), _flip(y, k & 2), _flip(c, k & 1))
            sems = send_sems.at[k - 1], recv_sems.at[k - 1]
            remote.append(_remote(in_ref, out_ref.at[mine], *sems, peer))
            landing.append(_remote(in_ref, out_ref.at[4 * peer[0] + 2 * peer[1] + peer[2]], *sems, peer))
        for cp in [local] + remote:
            cp.start()
        local.wait()
        for cp in landing:
            cp.wait_recv()
        for cp in remote:
            cp.wait_send()

    return _comm_call(body, [tile], [jax.ShapeDtypeStruct((N_DEV,) + tile.shape, tile.dtype)], N_DEV - 1, 1, name)[0]


def _sum_slots(slabs, *, name, tr=128):
    s, r, c = slabs.shape
    tr = min(tr, r)

    def body(in_ref, o_ref):
        acc = in_ref[0].astype(F32)
        for j in range(1, s):
            acc = acc + in_ref[j].astype(F32)
        o_ref[...] = acc

    return pl.pallas_call(
        body, name=name, grid=(r // tr,),
        in_specs=[pl.BlockSpec((s, tr, c), lambda i: (0, i, 0))], out_specs=pl.BlockSpec((tr, c), lambda i: (i, 0)),
        out_shape=jax.ShapeDtypeStruct((r, c), F32), compiler_params=_params("parallel"))(slabs)


def _adamw_math(w, g, m, v):
    m = ADAM_B1 * m + (1.0 - ADAM_B1) * g
    v = ADAM_B2 * v + (1.0 - ADAM_B2) * (g * g)
    m_hat = m / (1.0 - ADAM_B1 ** ADAM_STEP)
    v_hat = v / (1.0 - ADAM_B2 ** ADAM_STEP)
    delta = -ADAM_LR * (m_hat / (jnp.sqrt(v_hat) + ADAM_EPS) + ADAM_WD * w)
    return delta, m, v


def _adamw(w, m, v, parts, *, name, tr=128):
    r, c = w.shape
    tr = min(tr, r)
    s = len(parts)

    def body(w_ref, m_ref, v_ref, *refs):
        g_ref, d_ref, nm_ref, nv_ref = refs[s:]
        g = refs[0][...]
        for p_ref in refs[1:s]:
            g = g + p_ref[...]
        g_ref[...] = g
        d_ref[...], nm_ref[...], nv_ref[...] = _adamw_math(w_ref[...], g, m_ref[...], v_ref[...])

    blk = pl.BlockSpec((tr, c), lambda i: (i, 0))
    shape = jax.ShapeDtypeStruct((r, c), F32)
    return pl.pallas_call(
        body, name=name, grid=(r // tr,), in_specs=[blk] * (3 + s), out_specs=[blk] * 4, out_shape=[shape] * 4,
        compiler_params=_params("parallel"))(w, m, v, *parts)


_BIG = ("a_w_in", "b_w_in", "a_w_out", "b_w_out", "a_conv_w")
_SMALL = ("norm_w", "a_a_log", "a_dt_bias", "a_out_norm_w", "b_q_norm_w", "b_k_norm_w", "b_rel_bias")
_ORDER = ("norm_w", "a_w_in", "a_conv_w", "a_a_log", "a_dt_bias", "a_out_norm_w", "a_w_out", "b_w_in", "b_q_norm_w",
          "b_k_norm_w", "b_rel_bias", "b_w_out")


def _join_cols(g):
    return jnp.transpose(g, (1, 0, 2)).reshape(g.shape[1], -1)


def _split_cols(g):
    return jnp.transpose(g.reshape(g.shape[0], N_CHIPS, -1), (1, 0, 2))


def _split_rows(g):
    return g.reshape(N_CHIPS, -1, g.shape[-1])


def _pack(d):
    flat = jnp.concatenate([d[n].reshape(-1) for n in _SMALL])
    return jnp.pad(flat, (0, -flat.shape[0] % LANES)).reshape(1, -1)


def _unpack(row, like):
    out, at = {}, 0
    for n in _SMALL:
        size = like[n].size
        out[n] = row[0, at:at + size].reshape(like[n].shape)
        at += size
    return out


def kernel(x, norm_w, a_w_in, a_conv_w, a_a_log, a_dt_bias, a_out_norm_w, a_w_out, b_w_in, b_q_norm_w, b_k_norm_w, b_rel_bias, b_w_out, loss_target, m_norm_w, m_a_w_in, m_a_conv_w, m_a_a_log, m_a_dt_bias, m_a_out_norm_w, m_a_w_out, m_b_w_in, m_b_q_norm_w, m_b_k_norm_w, m_b_rel_bias, m_b_w_out, v_norm_w, v_a_w_in, v_a_conv_w, v_a_a_log, v_a_dt_bias, v_a_out_norm_w, v_a_w_out, v_b_w_in, v_b_q_norm_w, v_b_k_norm_w, v_b_rel_bias, v_b_w_out):
    w = dict(norm_w=norm_w, a_w_in=a_w_in, a_conv_w=a_conv_w, a_a_log=a_a_log, a_dt_bias=a_dt_bias,
             a_out_norm_w=a_out_norm_w, a_w_out=a_w_out, b_w_in=b_w_in, b_q_norm_w=b_q_norm_w, b_k_norm_w=b_k_norm_w,
             b_rel_bias=b_rel_bias, b_w_out=b_w_out)
    m = dict(norm_w=m_norm_w, a_w_in=m_a_w_in, a_conv_w=m_a_conv_w, a_a_log=m_a_a_log, a_dt_bias=m_a_dt_bias,
             a_out_norm_w=m_a_out_norm_w, a_w_out=m_a_w_out, b_w_in=m_b_w_in, b_q_norm_w=m_b_q_norm_w,
             b_k_norm_w=m_b_k_norm_w, b_rel_bias=m_b_rel_bias, b_w_out=m_b_w_out)
    v = dict(norm_w=v_norm_w, a_w_in=v_a_w_in, a_conv_w=v_a_conv_w, a_a_log=v_a_a_log, a_dt_bias=v_a_dt_bias,
             a_out_norm_w=v_a_out_norm_w, a_w_out=v_a_w_out, b_w_in=v_b_w_in, b_q_norm_w=v_b_q_norm_w,
             b_k_norm_w=v_b_k_norm_w, b_rel_bias=v_b_rel_bias, b_w_out=v_b_w_out)

    wa_in, conv = _chip_call(True, [a_w_in[0].astype(BF16), a_conv_w[0]], name="gather_a_in")
    loss, dx, grads = _local_step(
        x[0], loss_target[0], norm_w, _join_cols(wa_in), _join_cols(conv), a_a_log, a_dt_bias, a_out_norm_w,
        a_w_out[0].astype(BF16), b_w_in[0].astype(BF16), b_q_norm_w, b_k_norm_w, b_rel_bias, b_w_out[0].astype(BF16),
        sharded=True)
    loss = lax.psum(loss, ("x", "y", "c"))

    mine = [_sum_slots(grads[n], name=f"chip_sum_{n}") for n in _BIG]
    theirs = _swap_pair(mine, name="pair_grads")
    out = {}
    for n, p, q in zip(_BIG, mine, theirs):
        out[n] = [r[None] for r in _adamw(w[n][0], m[n][0], v[n][0], [p, q], name=f"adamw_{n}")]

    row = _pack(grads)
    tiles = _gather_all(jnp.broadcast_to(row, (8, row.shape[1])), name="gather_small_grads")
    res = _adamw(_pack(w), _pack(m), _pack(v), [tiles[d, 0:1, :] for d in range(N_DEV)], name="adamw_small")
    unpacked = [_unpack(r, w) for r in res]
    for n in _SMALL:
        out[n] = [u[n] for u in unpacked]

    return (loss, dx[None], *[out[n][0] for n in _ORDER], *[out[n][1] for n in _ORDER], *[out[n][2] for n in _ORDER],
            *[out[n][3] for n in _ORDER])
```

```python
import functools

import numpy as np
import jax
import jax.numpy as jnp
from jax import lax
from jax.experimental import pallas as pl
from jax.experimental.pallas import tpu as pltpu

F32 = jnp.float32
BF16 = jnp.bfloat16

CHUNK = 64
HEAD_DIM = 128
LEFT_CHUNKS = 8
REL_CLIP = 256
CONV_K = 4
EPS = 1e-6
HALO = 8

ADAM_LR = 0.001
ADAM_B1 = 0.9
ADAM_B2 = 0.999
ADAM_EPS = 1e-08
ADAM_WD = 0.01
ADAM_STEP = 10

LANES = 128
N_CHIPS = 4
N_DEV = 8
VMEM_LIMIT_BYTES = 56 * 1024 * 1024
MESH = pl.DeviceIdType.MESH
HIGHEST = lax.Precision.HIGHEST


def _params(*sem):
    return pltpu.CompilerParams(dimension_semantics=sem, vmem_limit_bytes=VMEM_LIMIT_BYTES)


def _dot(a, b, dims=(((1,), (0,)), ((), ())), precision=None):
    return lax.dot_general(a, b, dims, precision=precision, preferred_element_type=F32)


_NT = (((1,), (1,)), ((), ()))
_TN = (((0,), (0,)), ((), ()))


def _bdot(a, b, dims=(((1,), (0,)), ((), ()))):
    return _dot(a.astype(BF16), b.astype(BF16), dims)


def _fdot(a, b, dims=(((1,), (0,)), ((), ()))):
    return _dot(a, b, dims, precision=lax.Precision.HIGH)


def _silu(x):
    return x * jax.nn.sigmoid(x)


def _matmul(a, b, *, name, trans_b=False, residual=None, out_dtype=F32, tm=1024, tn=1024, tk=2048, exchange=()):
    m, k = a.shape
    n = b.shape[0] if trans_b else b.shape[1]
    tm, tn, tk = min(tm, m), min(tn, n), min(tk, k)
    assert m % tm == 0 and n % tn == 0 and k % tk == 0, (a.shape, b.shape, tm, tn, tk)
    nk = k // tk
    dims = _NT if trans_b else (((1,), (0,)), ((), ()))

    def body(*refs):
        if residual is None:
            a_ref, b_ref, o_ref, acc_ref = refs
            r_ref = None
        else:
            a_ref, b_ref, r_ref, o_ref, acc_ref = refs
        kk = pl.program_id(2)

        @pl.when(kk == 0)
        def _():
            acc_ref[...] = jnp.zeros_like(acc_ref)

        acc_ref[...] += _dot(a_ref[...], b_ref[...], dims)

        @pl.when(kk == nk - 1)
        def _():
            r = acc_ref[...]
            if r_ref is not None:
                r = r + r_ref[...]
            o_ref[...] = r.astype(o_ref.dtype)

    in_specs = [
        pl.BlockSpec((tm, tk), lambda i, j, kk: (i, kk)),
        pl.BlockSpec((tn, tk), lambda i, j, kk: (j, kk)) if trans_b else pl.BlockSpec((tk, tn), lambda i, j, kk: (kk, j)),
    ]
    args = [a, b]
    if residual is not None:
        in_specs.append(pl.BlockSpec((tm, tn), lambda i, j, kk: (i, j)))
        args.append(residual)
    grid = (m // tm, n // tn, nk)
    n_x = len(exchange)
    out, *landed = pl.pallas_call(
        _with_exchange(body, len(args), 1, False, n_x, grid),
        name=name,
        grid=grid,
        in_specs=in_specs + [_ANY] * n_x,
        out_specs=[pl.BlockSpec((tm, tn), lambda i, j, kk: (i, j))] + [_ANY] * n_x,
        out_shape=[jax.ShapeDtypeStruct((m, n), out_dtype)] + _chip_shapes(False, exchange),
        scratch_shapes=[pltpu.VMEM((tm, tn), F32)] + (_chip_scratch(n_x) if n_x else []),
        compiler_params=_params(*(("arbitrary",) * 3 if n_x else ("parallel", "parallel", "arbitrary"))),
    )(*args, *exchange)
    return (out, landed) if n_x else out


def _rms(x, w):
    return x * lax.rsqrt(jnp.mean(x * x, axis=-1, keepdims=True) + EPS) * w


def _rmsnorm_fwd(x, w_row, *, name, tr=512):
    t, d = x.shape
    tr = min(tr, t)

    def body(x_ref, w_ref, o_ref):
        o_ref[...] = _rms(x_ref[...], w_ref[...]).astype(BF16)

    return pl.pallas_call(
        body,
        name=name,
        grid=(t // tr,),
        in_specs=[pl.BlockSpec((tr, d), lambda i: (i, 0)), pl.BlockSpec((1, d), lambda i: (0, 0))],
        out_specs=pl.BlockSpec((tr, d), lambda i: (i, 0)),
        out_shape=jax.ShapeDtypeStruct((t, d), BF16),
        compiler_params=_params("parallel"),
    )(x, w_row)


def _rmsnorm_bwd(x, w_row, dy, dres, *, name, tr=256):
    t, d = x.shape
    tr = min(tr, t)

    def body(x_ref, w_ref, dy_ref, dres_ref, dx_ref, dxb_ref, dw_ref):
        @pl.when(pl.program_id(0) == 0)
        def _():
            dw_ref[...] = jnp.zeros_like(dw_ref)

        _, vjp = jax.vjp(_rms, x_ref[...], w_ref[...])
        dx, dw = vjp(dy_ref[...])
        dx = dx + dres_ref[...]
        dx_ref[...] = dx
        dxb_ref[...] = dx.astype(BF16)
        dw_ref[...] += dw

    row = pl.BlockSpec((tr, d), lambda i: (i, 0))
    vec = pl.BlockSpec((1, d), lambda i: (0, 0))
    return pl.pallas_call(
        body,
        name=name,
        grid=(t // tr,),
        in_specs=[row, vec, row, row],
        out_specs=[row, row, vec],
        out_shape=[jax.ShapeDtypeStruct((t, d), F32), jax.ShapeDtypeStruct((t, d), BF16), jax.ShapeDtypeStruct((1, d), F32)],
        compiler_params=_params("arbitrary"),
    )(x, w_row, dy, dres)


def _loss_head(h, target, *, name, tr=512):
    t, d = h.shape
    tr = min(tr, t)

    def body(h_ref, t_ref, dh_ref, dhb_ref, part_ref):
        @pl.when(pl.program_id(0) == 0)
        def _():
            part_ref[...] = jnp.zeros_like(part_ref)

        err = h_ref[...] - t_ref[...]
        dh = err * (1.0 / d)
        dh_ref[...] = dh
        dhb_ref[...] = dh.astype(BF16)
        part_ref[...] += jnp.sum(err * err, axis=0, keepdims=True)

    row = pl.BlockSpec((tr, d), lambda i: (i, 0))
    vec = pl.BlockSpec((1, d), lambda i: (0, 0))
    dh, dhb, part = pl.pallas_call(
        body,
        name=name,
        grid=(t // tr,),
        in_specs=[row, row],
        out_specs=[row, row, vec],
        out_shape=[jax.ShapeDtypeStruct((t, d), F32), jax.ShapeDtypeStruct((t, d), BF16), jax.ShapeDtypeStruct((1, d), F32)],
        compiler_params=_params("arbitrary"),
    )(h, target)
    return 0.5 / d * jnp.sum(part), dh, dhb


_BNN = (((2,), (1,)), ((0,), (0,)))
_BNT = (((2,), (2,)), ((0,), (0,)))
_BTN = (((1,), (1,)), ((0,), (0,)))


_TAP0 = HALO - (CONV_K - 1)


def _conv(x_ref, w, rows):
    c = w[0:1, :] * x_ref[_TAP0:_TAP0 + rows, :]
    for j in range(1, CONV_K):
        c = c + w[j:j + 1, :] * x_ref[_TAP0 + j:_TAP0 + j + rows, :]
    return c


def _conv_silu_bwd(x_ref, w, dact, dc_ref, rows):
    c = _conv(x_ref, w, rows)
    sig = jax.nn.sigmoid(c)
    dc = dact * (sig * (1.0 + c * (1.0 - sig)))
    dw = [jnp.sum(dc * x_ref[_TAP0 + j:_TAP0 + j + rows, :], axis=0, keepdims=True) for j in range(CONV_K)]
    dc_ref[0:HALO, :] = jnp.zeros((HALO, HEAD_DIM), F32)
    dc_ref[HALO:HALO + rows, :] = dc
    dc_ref[HALO + rows:HALO + rows + HALO, :] = jnp.zeros((HALO, HEAD_DIM), F32)
    first = HALO - _TAP0
    dx = w[0:1, :] * dc_ref[first:first + HALO + rows, :]
    for j in range(1, CONV_K):
        dx = dx + w[j:j + 1, :] * dc_ref[first - j:first - j + HALO + rows, :]
    return dx, dw


def _gdn_intra(qt, kt, v, a, b, alog, dtb):
    n = a.shape[0] // CHUNK
    q = qt * lax.rsqrt(jnp.sum(qt * qt, axis=-1, keepdims=True) + EPS) * (HEAD_DIM ** -0.5)
    k = kt * lax.rsqrt(jnp.sum(kt * kt, axis=-1, keepdims=True) + EPS)
    lanes = jnp.ones((1, HEAD_DIM), F32)
    beta = jax.nn.sigmoid(b) * lanes
    sp = a + dtb
    g = (-jnp.exp(alog) * (jnp.maximum(sp, 0.0) + jnp.log(1.0 + jnp.exp(-jnp.abs(sp))))) * lanes
    q, k, v, beta, g = (t.reshape(n, CHUNK, HEAD_DIM) for t in (q, k, v, beta, g))

    row = lax.broadcasted_iota(jnp.int32, (n, CHUNK, CHUNK), 1)
    col = lax.broadcasted_iota(jnp.int32, (n, CHUNK, CHUNK), 2)
    tri_incl = row >= col
    tri_strict = row > col
    gc = _fdot(tri_incl.astype(F32), g, _BNN)
    gc_row = _fdot(g[:, :, :CHUNK], (row <= col).astype(F32), _BTN)
    decay = jnp.exp(jnp.where(tri_incl, gc[:, :, :CHUNK] - gc_row, -1e30))
    kb = k * beta
    vb = v * beta
    with_k = _bdot(jnp.concatenate([kb, q], axis=1), k, _BNT)
    neg_l = jnp.where(tri_strict, -(with_k[:, :CHUNK] * decay), 0.0)
    qk = jnp.where(tri_incl, with_k[:, CHUNK:] * decay, 0.0)
    inv = (row == col).astype(F32) + neg_l
    power = _bdot(neg_l, neg_l, _BNN)
    for _ in range(4):
        both = _bdot(jnp.concatenate([inv, power], axis=1), power, _BNN)
        inv, power = inv + both[:, :CHUNK], both[:, CHUNK:]
    inv = inv + _bdot(inv, power, _BNN)
    e = jnp.exp(gc)
    solved = _bdot(inv, jnp.concatenate([kb * e, vb], axis=2), _BNN)
    g_last = gc[:, CHUNK - 1:CHUNK, :]
    k_dec = k * jnp.exp(g_last - gc)
    from_k = _bdot(k_dec, solved, _BTN)
    from_qk = _bdot(qk, solved, _BNN)
    step, add = -from_k[:, :, :HEAD_DIM], from_k[:, :, HEAD_DIM:]
    read, out = q * e - from_qk[:, :, :HEAD_DIM], from_qk[:, :, HEAD_DIM:]
    return step, add, jnp.exp(g_last), read, out


def _gdn_scan_step(state, step, add, decay_last):
    return state * decay_last + _bdot(step, state) + add


def _gdn_outputs(states, read, out, z, onw):
    return _rms(_bdot(read, states, _BNN) + out, onw) * _silu(z)


def _scan_scratch(n, dtype):
    return [pltpu.VMEM((n, HEAD_DIM, HEAD_DIM), dtype), pltpu.VMEM((n, HEAD_DIM, HEAD_DIM), F32), pltpu.VMEM((n, 1, HEAD_DIM), F32)]


def _head_lane(h, offset=0):
    return lax.broadcasted_iota(jnp.int32, (1, LANES), 1) == h + offset


def _pick(mask, x):
    return jnp.sum(jnp.where(mask, x, 0.0), axis=1, keepdims=True)


def _gdn_specs(heads, tb, rev, nb, PAIR):
    assert heads % PAIR == 0
    blk = (lambda i: nb - 1 - i) if rev else (lambda i: i)
    hb = tb // HALO
    width, pairs = PAIR * HEAD_DIM, heads // PAIR

    def col(group):
        return pl.BlockSpec((tb, width), lambda i, h: (blk(i), group * pairs + h))

    def halo(group):
        return pl.BlockSpec((HALO, width), lambda i, h: (jnp.maximum(blk(i) * hb - 1, 0), group * pairs + h))

    def convw(group):
        return pl.BlockSpec((CONV_K, width), lambda i, h: (0, group * pairs + h))

    vec = pl.BlockSpec((1, LANES), lambda i, h: (0, 0))
    ab = pl.BlockSpec((tb, LANES), lambda i, h: (blk(i), 0))
    states = pl.BlockSpec((PAIR, tb // CHUNK, HEAD_DIM, HEAD_DIM), lambda i, h: (h, blk(i), 0, 0))
    return blk, col, halo, convw, vec, ab, states


def _head_cols(p):
    return slice(p * HEAD_DIM, (p + 1) * HEAD_DIM)


def _gdn_fwd(proj, ab, conv_w, alog_row, dtb_row, onw_row, *, heads, name, tb=1024, pair=2, gather=()):
    t = proj.shape[0]
    tb = min(tb, t)
    nb, cpb = t // tb, tb // CHUNK
    PAIR = pair
    _, col, halo, convw, vec, abspec, states = _gdn_specs(heads, tb, False, nb, PAIR)

    def body(q_ref, k_ref, v_ref, qh_ref, kh_ref, vh_ref, z_ref, ab_ref, wq_ref, wk_ref, wv_ref, alog_ref, dtb_ref, onw_ref,
             og_ref, st_ref, state_scr, x_scr, *op_scr):
        i, pair = pl.program_id(0), pl.program_id(1)
        abv = ab_ref[...]
        heads_here, later = [pair * PAIR + p for p in range(PAIR)], []
        for p, h in enumerate(heads_here):
            cols = _head_cols(p)
            for n, (ref, href) in enumerate(((q_ref, qh_ref), (k_ref, kh_ref), (v_ref, vh_ref))):
                x_scr[p, n, 0:HALO, :] = jnp.where(i > 0, href[:, cols], 0.0)
                x_scr[p, n, HALO:HALO + tb, :] = ref[:, cols]
            sel_a, sel_b = _head_lane(h), _head_lane(h, heads)
            alog, dtb = _pick(sel_a, alog_ref[...]), _pick(sel_a, dtb_ref[...])
            acts = [_silu(_conv(x_scr.at[p, n], w_ref[:, cols], tb)) for n, w_ref in enumerate((wq_ref, wk_ref, wv_ref))]
            *scan, read, out = _gdn_intra(*acts, _pick(sel_a, abv), _pick(sel_b, abv), alog, dtb)
            for scr, val in zip(op_scr[3 * p:3 * p + 3], scan):
                scr[...] = val.astype(scr.dtype)
            later.append((read, out))

        def chunk(c, states):
            for p in range(PAIR):
                st_ref[p, c] = states[p]
            return tuple(_gdn_scan_step(states[p], *[scr[c] for scr in op_scr[3 * p:3 * p + 3]]) for p in range(PAIR))

        @pl.when(i == 0)
        def _():
            for h in heads_here:
                state_scr[h] = jnp.zeros((HEAD_DIM, HEAD_DIM), F32)

        last = lax.fori_loop(0, cpb, chunk, tuple(state_scr[h] for h in heads_here))
        for p, h in enumerate(heads_here):
            cols = _head_cols(p)
            state_scr[h] = last[p]
            og = _gdn_outputs(st_ref[p], *later[p], z_ref[:, cols].reshape(cpb, CHUNK, HEAD_DIM), onw_ref[...])
            og_ref[:, cols] = og.reshape(tb, HEAD_DIM).astype(BF16)

    n_x = len(gather)
    grid = (nb, heads // PAIR)
    og, st, *gathered = pl.pallas_call(
        _with_exchange(body, 14, 2, True, n_x, grid),
        name=name,
        grid=grid,
        in_specs=[col(0), col(1), col(2), halo(0), halo(1), halo(2), col(3), abspec, convw(0), convw(1), convw(2), vec, vec, vec]
        + [_ANY] * n_x,
        out_specs=[pl.BlockSpec((tb, PAIR * HEAD_DIM), lambda i, h: (i, h)), states] + [_ANY] * n_x,
        out_shape=[jax.ShapeDtypeStruct((t, heads * HEAD_DIM), BF16),
                   jax.ShapeDtypeStruct((heads, t // CHUNK, HEAD_DIM, HEAD_DIM), F32)] + _chip_shapes(True, gather),
        scratch_shapes=[pltpu.VMEM((heads, HEAD_DIM, HEAD_DIM), F32), pltpu.VMEM((PAIR, 3, HALO + tb, HEAD_DIM), F32)]
        + _scan_scratch(cpb, BF16) * PAIR + (_chip_scratch(n_x) if n_x else []),
        compiler_params=_params("arbitrary", "arbitrary"),
    )(proj, proj, proj, proj, proj, proj, proj, ab, conv_w, conv_w, conv_w, alog_row, dtb_row, onw_row, *gather)
    return og, st, gathered


def _gdn_bwd(proj, ab, conv_w, alog_row, dtb_row, onw_row, states, dog, *, heads, name, tb=1024, pair=1, exchange=()):
    t = proj.shape[0]
    tb = min(tb, t)
    nb, cpb = t // tb, tb // CHUNK
    PAIR = pair
    _, col, halo, convw, vec, abspec, states_spec = _gdn_specs(heads, tb, True, nb, PAIR)
    n_conv = conv_w.shape[1]

    def body(q_ref, k_ref, v_ref, qh_ref, kh_ref, vh_ref, z_ref, ab_ref, wq_ref, wk_ref, wv_ref, alog_ref, dtb_ref, onw_ref,
             st_ref, dog_ref, dq_ref, dk_ref, dv_ref, dz_ref, dab_ref, dconv_ref, dalog_ref, ddtb_ref, donw_ref,
             dstate_scr, x_scr, carry_scr, *scr):
        op_scr, dop_scr, dstates_scr, dc_scr = scr[:3 * PAIR], scr[3 * PAIR:6 * PAIR], scr[6 * PAIR:7 * PAIR], scr[7 * PAIR]
        i, pair = pl.program_id(0), pl.program_id(1)
        first_block = i == nb - 1
        heads_here, later = [pair * PAIR + p for p in range(PAIR)], []

        @pl.when(jnp.logical_and(i == 0, pair == 0))
        def _():
            dconv_ref[...] = jnp.zeros_like(dconv_ref)
            dalog_ref[...] = jnp.zeros_like(dalog_ref)
            ddtb_ref[...] = jnp.zeros_like(ddtb_ref)
            donw_ref[...] = jnp.zeros_like(donw_ref)

        @pl.when(pair == 0)
        def _():
            dab_ref[...] = jnp.zeros_like(dab_ref)

        @pl.when(i == 0)
        def _():
            for h in heads_here:
                dstate_scr[h] = jnp.zeros((HEAD_DIM, HEAD_DIM), F32)
                carry_scr[h] = jnp.zeros((3, HALO, HEAD_DIM), F32)

        abv = ab_ref[...]
        w_refs = (wq_ref, wk_ref, wv_ref)
        for p, h in enumerate(heads_here):
            cols = _head_cols(p)
            for n, (ref, href) in enumerate(((q_ref, qh_ref), (k_ref, kh_ref), (v_ref, vh_ref))):
                x_scr[p, n, 0:HALO, :] = jnp.where(first_block, 0.0, href[:, cols])
                x_scr[p, n, HALO:HALO + tb, :] = ref[:, cols]
            sel_a, sel_b = _head_lane(h), _head_lane(h, heads)
            alog, dtb = _pick(sel_a, alog_ref[...]), _pick(sel_a, dtb_ref[...])
            acts = [_silu(_conv(x_scr.at[p, n], w_ref[:, cols], tb)) for n, w_ref in enumerate(w_refs)]
            (*scan, read, out), vjp_intra = jax.vjp(_gdn_intra, *acts, _pick(sel_a, abv), _pick(sel_b, abv), alog, dtb)
            for s, val in zip(op_scr[3 * p:3 * p + 3], scan):
                s[...] = val.astype(s.dtype)
            blocked = lambda ref: ref[:, cols].reshape(cpb, CHUNK, HEAD_DIM)
            _, vjp_outputs = jax.vjp(_gdn_outputs, st_ref[p], read, out, blocked(z_ref), onw_ref[...])
            dstates_scr[p][...], dread, dout, dz, donw = vjp_outputs(blocked(dog_ref))
            dz_ref[:, cols] = dz.reshape(tb, HEAD_DIM).astype(BF16)
            donw_ref[...] += donw
            later.append((vjp_intra, dread, dout, sel_a, sel_b))

        def chunk(i_rev, dstates):
            c = cpb - 1 - i_rev
            new = []
            for p in range(PAIR):
                _, vjp = jax.vjp(_gdn_scan_step, st_ref[p, c], *[s[c].astype(F32) for s in op_scr[3 * p:3 * p + 3]])
                dstate, *grads = vjp(dstates[p])
                for s, val in zip(dop_scr[3 * p:3 * p + 3], grads):
                    s[c] = val
                new.append(dstate + dstates_scr[p][c])
            return tuple(new)

        last = lax.fori_loop(0, cpb, chunk, tuple(dstate_scr[h] for h in heads_here))
        for p, h in enumerate(heads_here):
            cols = _head_cols(p)
            vjp_intra, dread, dout, sel_a, sel_b = later[p]
            dstate_scr[h] = last[p]
            *dacts, da, db, dalog, ddtb = vjp_intra((*[s[...] for s in dop_scr[3 * p:3 * p + 3]], dread, dout))
            dab_ref[...] += jnp.where(sel_a, da, 0.0) + jnp.where(sel_b, db, 0.0)
            for n, (dref, dact, w_ref) in enumerate(zip((dq_ref, dk_ref, dv_ref), dacts, w_refs)):
                dx, dw = _conv_silu_bwd(x_scr.at[p, n], w_ref[:, cols], dact, dc_scr, tb)
                x_scr[p, n] = dx
                x_scr[p, n, tb:tb + HALO, :] += carry_scr[h, n]
                carry_scr[h, n] = x_scr[p, n, 0:HALO, :]
                dref[:, cols] = x_scr[p, n, HALO:HALO + tb, :].astype(BF16)
                lanes = pl.ds(pl.multiple_of((n * heads + h) * HEAD_DIM, HEAD_DIM), HEAD_DIM)
                for j in range(CONV_K):
                    dconv_ref[j:j + 1, lanes] += dw[j]
            dalog_ref[...] += jnp.where(sel_a, dalog, 0.0)
            ddtb_ref[...] += jnp.where(sel_a, ddtb, 0.0)

    out_col = pl.BlockSpec((tb, PAIR * HEAD_DIM), lambda i, h: (nb - 1 - i, h))
    dog_spec = out_col
    col_shape = jax.ShapeDtypeStruct((t, heads * HEAD_DIM), BF16)
    row_shape = jax.ShapeDtypeStruct((1, LANES), F32)
    n_x = len(exchange)
    grid = (nb, heads // PAIR)
    outs = pl.pallas_call(
        _with_exchange(body, 16, 9, False, n_x, grid),
        name=name,
        grid=grid,
        in_specs=[col(0), col(1), col(2), halo(0), halo(1), halo(2), col(3), abspec, convw(0), convw(1), convw(2), vec, vec, vec,
                  states_spec, dog_spec] + [_ANY] * n_x,
        out_specs=[out_col, out_col, out_col, out_col, abspec,
                   pl.BlockSpec((CONV_K, n_conv), lambda i, h: (0, 0)), vec, vec, vec] + [_ANY] * n_x,
        out_shape=[col_shape, col_shape, col_shape, col_shape, jax.ShapeDtypeStruct((t, LANES), F32),
                   jax.ShapeDtypeStruct((CONV_K, n_conv), F32), row_shape, row_shape, row_shape] + _chip_shapes(False, exchange),
        scratch_shapes=[pltpu.VMEM((heads, HEAD_DIM, HEAD_DIM), F32), pltpu.VMEM((PAIR, 3, HALO + tb, HEAD_DIM), F32),
                        pltpu.VMEM((heads, 3, HALO, HEAD_DIM), F32)] + _scan_scratch(cpb, BF16) * PAIR
        + _scan_scratch(cpb, F32) * PAIR + [pltpu.VMEM((cpb, HEAD_DIM, HEAD_DIM), F32)] * PAIR
        + [pltpu.VMEM((HALO + tb + HALO, HEAD_DIM), F32)]
        + (_chip_scratch(n_x) if n_x else []),
        compiler_params=_params("arbitrary", "arbitrary"),
    )(proj, proj, proj, proj, proj, proj, proj, ab, conv_w, conv_w, conv_w, alog_row, dtb_row, onw_row, states, dog, *exchange)
    return (*outs[:9], outs[9:])


BAND = (LEFT_CHUNKS + 1) * CHUNK
PAD = LEFT_CHUNKS * CHUNK
GROUP = 2
ROWS = GROUP * CHUNK
WIN = (LEFT_CHUNKS + GROUP) * CHUNK
DIAGS = WIN + ROWS - 1
NEAR = PAD + ROWS - 1 - REL_CLIP
assert 0 < NEAR < DIAGS and WIN - PAD - 1 <= REL_CLIP and WIN % LANES == 0
ATTN_BLOCK = 1024


def _band_bias(rel_bias):
    heads = rel_bias.shape[0]
    far = jnp.broadcast_to(rel_bias[:, 2 * REL_CLIP:], (heads, NEAR + 1))
    near = rel_bias[:, 2 * REL_CLIP + NEAR + 1 - DIAGS:2 * REL_CLIP][:, ::-1]
    diag = jnp.concatenate([far, near], axis=1)
    return jnp.stack([diag[:, ROWS - 1 - r:ROWS - 1 - r + WIN] for r in range(ROWS)], axis=1)


def _band_bias_grad(dbias):
    heads = dbias.shape[0]
    diag = sum(jnp.pad(dbias[:, r, :], ((0, 0), (ROWS - 1 - r, r))) for r in range(ROWS))
    far = jnp.sum(diag[:, :NEAR + 1], axis=1, keepdims=True)
    near = diag[:, NEAR + 1:][:, ::-1]
    unused = jnp.zeros((heads, 2 * REL_CLIP - near.shape[1]), F32)
    return jnp.concatenate([unused, near, far], axis=1)


def _masked_bias(bias, n):
    r = np.arange(ROWS)[:, None]
    key = np.arange(WIN)[None, :]
    band_start = (r // CHUNK) * CHUNK
    in_band = np.logical_and(key >= band_start, key < band_start + BAND)
    in_sequence = key[None] >= PAD - np.arange(n)[:, None, None] * ROWS
    first = jnp.where(np.logical_and(in_band[None], in_sequence)[None], bias[:, None], -1e30)
    return first, jnp.where(in_band[None, None], bias[:, None], -1e30)


def _attn_groups(q_pre, z, kn, v, bias, qnw):
    q = _rms(q_pre, qnw)
    s = _bdot(q, kn, _BNT) * (HEAD_DIM ** -0.5) + bias
    p = jnp.exp(s - jnp.max(s, axis=-1, keepdims=True))
    p = p / jnp.sum(p, axis=-1, keepdims=True)
    return _bdot(p, v, _BNN) * _silu(z)


def _attn_groups_bwd(q_pre, z, kn, v, bias, qnw, dog):
    scale = HEAD_DIM ** -0.5
    inv_rms = lax.rsqrt(jnp.mean(q_pre * q_pre, axis=-1, keepdims=True) + EPS)
    q_hat = q_pre * inv_rms
    q_b = (q_hat * qnw).astype(BF16)
    s = _dot(q_b, kn, _BNT) * scale + bias
    e = jnp.exp(s - jnp.max(s, axis=-1, keepdims=True))
    p = e * (1.0 / jnp.sum(e, axis=-1, keepdims=True))
    p_b = p.astype(BF16)
    o = _dot(p_b, v, _BNN)
    sig = jax.nn.sigmoid(z)
    do = dog * (z * sig)
    dz = dog * o * (sig * (1.0 + z * (1.0 - sig)))
    do_b = do.astype(BF16)
    dv = _dot(p_b, do_b, _BTN)
    dp = _dot(do_b, v, _BNT)
    ds = p * (dp - jnp.sum(do * o, axis=-1, keepdims=True))
    ds_b = (ds * scale).astype(BF16)
    dq = _dot(ds_b, kn, _BNN)
    dkn = _dot(ds_b, q_b, _BTN)
    dqnw = jnp.sum(jnp.sum(dq * q_hat, axis=0), axis=0, keepdims=True)
    dq_hat = dq * qnw
    dq_pre = inv_rms * (dq_hat - q_hat * jnp.mean(dq_hat * q_hat, axis=-1, keepdims=True))
    return dq_pre, dz, dkn, dv, jnp.sum(ds, axis=0), dqnw


def _attn_specs(heads, tb, t):
    def col(group):
        return pl.BlockSpec((tb, HEAD_DIM), lambda h, i: (i, group * heads + h))

    def full(group):
        return pl.BlockSpec((t, HEAD_DIM), lambda h, i: (0, group * heads + h))

    bias = [pl.BlockSpec((1, tb // ROWS, ROWS, WIN), lambda h, i: (h, 0, 0, 0)),
            pl.BlockSpec((1, 1, ROWS, WIN), lambda h, i: (h, 0, 0, 0))]
    vec = pl.BlockSpec((1, HEAD_DIM), lambda h, i: (0, 0))
    return col, full, bias, vec


def _attn_windows(scr, block_start, n):
    return jnp.stack([scr[pl.ds(pl.multiple_of(block_start + g * ROWS, ROWS), WIN), :] for g in range(n)])


def _attn_fill(k_ref, v_ref, knw_ref, kn_scr, v_scr, t):
    kn_scr[0:PAD, :] = jnp.zeros((PAD, HEAD_DIM), BF16)
    v_scr[0:PAD, :] = jnp.zeros((PAD, HEAD_DIM), BF16)
    step = min(512, t)

    def fill(j, _):
        rows = pl.ds(pl.multiple_of(j * step, step), step)
        prows = pl.ds(pl.multiple_of(PAD + j * step, CHUNK), step)
        kn_scr[prows, :] = _rms(k_ref[rows, :], knw_ref[...]).astype(BF16)
        v_scr[prows, :] = v_ref[rows, :].astype(BF16)
        return 0

    lax.fori_loop(0, t // step, fill, 0)


def _attn_fwd(proj, bias, qnw_row, knw_row, *, heads, name, tb=ATTN_BLOCK):
    t = proj.shape[0]
    tb = min(tb, t)
    nb, ng = t // tb, tb // ROWS
    col, full, bias_spec, vec = _attn_specs(heads, tb, t)

    def body(q_ref, k_ref, v_ref, z_ref, first_ref, rest_ref, qnw_ref, knw_ref, og_ref, kn_scr, v_scr):
        i = pl.program_id(1)

        @pl.when(i == 0)
        def _():
            _attn_fill(k_ref, v_ref, knw_ref, kn_scr, v_scr, t)

        def run(block_bias):
            start = i * tb
            og = _attn_groups(q_ref[...].reshape(ng, ROWS, HEAD_DIM), z_ref[...].reshape(ng, ROWS, HEAD_DIM),
                              _attn_windows(kn_scr, start, ng), _attn_windows(v_scr, start, ng), block_bias, qnw_ref[...])
            og_ref[...] = og.reshape(tb, HEAD_DIM).astype(BF16)

        pl.when(i == 0)(lambda: run(first_ref[0]))
        pl.when(i > 0)(lambda: run(rest_ref[0]))

    return pl.pallas_call(
        body,
        name=name,
        grid=(heads, nb),
        in_specs=[col(0), full(1), full(2), col(3), *bias_spec, vec, vec],
        out_specs=pl.BlockSpec((tb, HEAD_DIM), lambda h, i: (i, h)),
        out_shape=jax.ShapeDtypeStruct((t, heads * HEAD_DIM), BF16),
        scratch_shapes=[pltpu.VMEM((PAD + t, HEAD_DIM), BF16), pltpu.VMEM((PAD + t, HEAD_DIM), BF16)],
        compiler_params=_params("arbitrary", "arbitrary"),
    )(proj, proj, proj, proj, *bias, qnw_row, knw_row)


def _attn_bwd(proj, bias, qnw_row, knw_row, dog, *, heads, name, tb=ATTN_BLOCK, sub=4):
    t = proj.shape[0]
    tb = min(tb, t)
    nb, ng = t // tb, tb // ROWS
    sub = min(sub, ng)
    col, full, bias_spec, vec = _attn_specs(heads, tb, t)

    def body(q_ref, k_ref, v_ref, z_ref, first_ref, rest_ref, qnw_ref, knw_ref, dog_ref,
             dq_ref, dk_ref, dv_ref, dz_ref, dbias_ref, dqnw_ref, dknw_ref, kn_scr, v_scr, dkn_scr, dv_scr):
        i = pl.program_id(1)

        @pl.when(i == 0)
        def _():
            _attn_fill(k_ref, v_ref, knw_ref, kn_scr, v_scr, t)
            dkn_scr[...] = jnp.zeros_like(dkn_scr)
            dv_scr[...] = jnp.zeros_like(dv_scr)
            dbias_ref[...] = jnp.zeros_like(dbias_ref)
            dqnw_ref[...] = jnp.zeros_like(dqnw_ref)

        def run(block_bias):
            for g0 in range(0, ng, sub):
                rows = pl.ds(g0 * ROWS, sub * ROWS)
                at = i * tb + g0 * ROWS
                blocked = lambda ref: ref[rows, :].reshape(sub, ROWS, HEAD_DIM)
                dq, dz, dkn, dv, dbias, dqnw = _attn_groups_bwd(
                    blocked(q_ref), blocked(z_ref), _attn_windows(kn_scr, at, sub), _attn_windows(v_scr, at, sub),
                    block_bias(g0), qnw_ref[...], blocked(dog_ref))
                dq_ref[rows, :] = dq.reshape(sub * ROWS, HEAD_DIM).astype(BF16)
                dz_ref[rows, :] = dz.reshape(sub * ROWS, HEAD_DIM).astype(BF16)
                for g in range(sub):
                    window = pl.ds(pl.multiple_of(at + g * ROWS, ROWS), WIN)
                    dkn_scr[window, :] += dkn[g]
                    dv_scr[window, :] += dv[g]
                dbias_ref[0] += dbias
                dqnw_ref[0] += dqnw

        pl.when(i == 0)(lambda: run(lambda g0: first_ref[0, g0:g0 + sub]))
        pl.when(i > 0)(lambda: run(lambda g0: rest_ref[0]))

        @pl.when(i == nb - 1)
        def _():
            step = min(512, t)

            def finish(j, dknw):
                rows = pl.ds(pl.multiple_of(j * step, step), step)
                prows = pl.ds(pl.multiple_of(PAD + j * step, CHUNK), step)
                _, vjp = jax.vjp(_rms, k_ref[rows, :], knw_ref[...])
                dk, dw = vjp(dkn_scr[prows, :])
                dk_ref[rows, :] = dk.astype(BF16)
                dv_ref[rows, :] = dv_scr[prows, :].astype(BF16)
                return dknw + dw

            dknw_ref[0] = lax.fori_loop(0, t // step, finish, jnp.zeros((1, HEAD_DIM), F32))

    out_col = pl.BlockSpec((tb, HEAD_DIM), lambda h, i: (i, h))
    out_full = pl.BlockSpec((t, HEAD_DIM), lambda h, i: (0, h))
    head_vec = pl.BlockSpec((1, 1, HEAD_DIM), lambda h, i: (h, 0, 0))
    col_shape = jax.ShapeDtypeStruct((t, heads * HEAD_DIM), BF16)
    vec_shape = jax.ShapeDtypeStruct((heads, 1, HEAD_DIM), F32)
    return pl.pallas_call(
        body,
        name=name,
        grid=(heads, nb),
        in_specs=[col(0), full(1), full(2), col(3), *bias_spec, vec, vec, pl.BlockSpec((tb, HEAD_DIM), lambda h, i: (i, h))],
        out_specs=[out_col, out_full, out_full, out_col, pl.BlockSpec((1, ROWS, WIN), lambda h, i: (h, 0, 0)), head_vec,
                   head_vec],
        out_shape=[col_shape, col_shape, col_shape, col_shape, jax.ShapeDtypeStruct((heads, ROWS, WIN), F32),
                   vec_shape, vec_shape],
        scratch_shapes=[pltpu.VMEM((PAD + t, HEAD_DIM), BF16), pltpu.VMEM((PAD + t, HEAD_DIM), BF16),
                        pltpu.VMEM((PAD + t, HEAD_DIM), F32), pltpu.VMEM((PAD + t, HEAD_DIM), F32)],
        compiler_params=_params("arbitrary", "arbitrary"),
    )(proj, proj, proj, proj, *bias, qnw_row, knw_row, dog)


def _lane_row(v):
    v = v.reshape(1, -1)
    return jnp.pad(v, ((0, 0), (0, LANES - v.shape[1])))


def _local_step(x, target, norm_w, wa_in, conv_w, a_log, dt_bias, onw, wa_out, wb_in, qnw, knw, rel_bias, wb_out, *,
                sharded=False):
    ha, hb = a_log.shape[-1], rel_bias.shape[-2]
    na = 4 * ha * HEAD_DIM
    wa_main = wa_in[:, :na]
    wa_ab = jnp.pad(wa_in[:, na:], ((0, 0), (0, LANES - 2 * ha)))
    alog_row, dtb_row, onw_row = _lane_row(a_log), _lane_row(dt_bias), _lane_row(onw)
    qnw_row, knw_row = _lane_row(qnw), _lane_row(knw)
    bias = _masked_bias(_band_bias(rel_bias.reshape(hb, -1)), min(ATTN_BLOCK, x.shape[0]) // ROWS)

    hn0 = _rmsnorm_fwd(x, norm_w[0:1], name="norm0")
    proj_a = _matmul(hn0, wa_main, name="a_in")
    ab_a = _matmul(hn0, wa_ab, name="a_in_ab")
    og_a, states, got = _gdn_fwd(proj_a, ab_a, conv_w, alog_row, dtb_row, onw_row, heads=ha, name="gdn_fwd",
                                 gather=[wb_in, wa_out, wb_out] if sharded else [])
    if sharded:
        wb_in, wa_out, wb_out = _join_cols(got[0]), got[1].reshape(-1, got[1].shape[-1]), got[2].reshape(-1, got[2].shape[-1])
    h1 = _matmul(og_a, wa_out, residual=x, name="a_out")
    hn1 = _rmsnorm_fwd(h1, norm_w[1:2], name="norm1")
    proj_b = _matmul(hn1, wb_in, name="b_in")
    og_b = _attn_fwd(proj_b, bias, qnw_row, knw_row, heads=hb, name="attn_fwd")
    h2 = _matmul(og_b, wb_out, residual=h1, name="b_out")
    loss, dh2, dh2_b = _loss_head(h2, target, name="loss_head")

    dog_b = _matmul(dh2_b, wb_out, trans_b=True, name="d_b_out_x")
    dwb_out = _matmul(og_b.T, dh2_b, name="d_b_out_w")
    dq, dk, dv, dz, dbias, dqnw, dknw = _attn_bwd(proj_b, bias, qnw_row, knw_row, dog_b, heads=hb, name="attn_bwd")
    dproj_b = jnp.concatenate([dq, dk, dv, dz], axis=1)
    dhn1 = _matmul(dproj_b, wb_in, trans_b=True, name="d_b_in_x")
    dwb_in = _matmul(hn1.T, dproj_b, name="d_b_in_w")
    dh1, dh1_b, dnw1 = _rmsnorm_bwd(h1, norm_w[1:2], dhn1, dh2, name="d_norm1")

    dog_a = _matmul(dh1_b, wa_out, trans_b=True, name="d_a_out_x")
    dwa_out = _matmul(og_a.T, dh1_b, name="d_a_out_w")
    early = [_split_cols(dwb_in), _split_rows(dwa_out), _split_rows(dwb_out)] if sharded else []
    dq, dk, dv, dz, dab, dconv, dalog, ddtb, donw, landed = _gdn_bwd(
        proj_a, ab_a, conv_w, alog_row, dtb_row, onw_row, states, dog_a, heads=ha, name="gdn_bwd",
        exchange=[s.astype(BF16) for s in early])
    if sharded:
        dwb_in, dwa_out, dwb_out = landed
    dproj_a = jnp.concatenate([dq, dk, dv, dz], axis=1)
    dab_b = dab.astype(BF16)
    hn0_t = hn0.T
    dwa_in = jnp.concatenate(
        [_matmul(hn0_t, dproj_a, name="d_a_in_w"), _matmul(hn0_t, dab_b, name="d_a_in_ab_w")[:, :2 * ha]], axis=1)
    if sharded:
        dhn0, (dwa_in, dconv) = _matmul(dproj_a, wa_main, trans_b=True, name="d_a_in_x",
                                        exchange=[_split_cols(dwa_in).astype(BF16), _split_cols(dconv)])
    else:
        dhn0 = _matmul(dproj_a, wa_main, trans_b=True, name="d_a_in_x")
    dhn0 = _matmul(dab_b, wa_ab, trans_b=True, residual=dhn0, name="d_a_in_ab_x")
    dx, _, dnw0 = _rmsnorm_bwd(x, norm_w[0:1], dhn0, dh1, name="d_norm0")

    drel = _band_bias_grad(dbias)
    grads = dict(
        norm_w=jnp.concatenate([dnw0, dnw1], axis=0), a_w_in=dwa_in, a_conv_w=dconv, a_a_log=dalog[:, :ha],
        a_dt_bias=ddtb[:, :ha], a_out_norm_w=donw, a_w_out=dwa_out, b_w_in=dwb_in, b_q_norm_w=jnp.sum(dqnw, axis=0),
        b_k_norm_w=jnp.sum(dknw, axis=0), b_rel_bias=drel[None], b_w_out=dwb_out)
    return loss, dx, grads


_ANY = pl.BlockSpec(memory_space=pl.ANY)
_CHIP_FLIPS = ((1, 0), (0, 1), (1, 1))


def _place():
    x, y, c = lax.axis_index("x"), lax.axis_index("y"), lax.axis_index("c")
    return x, y, c


def _flip(v, bit):
    return 1 - v if bit else v


def _remote(src, dst, send_sem, recv_sem, peer):
    return pltpu.make_async_remote_copy(src_ref=src, dst_ref=dst, send_sem=send_sem, recv_sem=recv_sem, device_id=peer,
                                        device_id_type=MESH)


def _comm_call(body, arrays, out_shapes, n_remote, n_local, name):
    scratch = [pltpu.SemaphoreType.DMA((n_remote,)), pltpu.SemaphoreType.DMA((n_remote,))]
    if n_local:
        scratch.append(pltpu.SemaphoreType.DMA((n_local,)))
    return pl.pallas_call(
        body, name=name, in_specs=[_ANY] * len(arrays), out_specs=[_ANY] * len(out_shapes), out_shape=out_shapes,
        scratch_shapes=scratch)(*arrays)


def _chip_scratch(n):
    return [pltpu.SemaphoreType.DMA((3 * n,)), pltpu.SemaphoreType.DMA((3 * n,)), pltpu.SemaphoreType.DMA((n,))]


def _chip_shapes(gather, arrays):
    return [jax.ShapeDtypeStruct(((N_CHIPS,) + s.shape) if gather else s.shape, s.dtype) for s in arrays]


def _chip_traffic(gather, ins, outs, sems):
    send_sems, recv_sems, local_sems = sems
    x, y, c = _place()
    mine = 2 * x + y
    local, remote, landing = [], [], []
    for a in range(len(ins)):
        local.append(pltpu.make_async_copy(ins[a] if gather else ins[a].at[mine], outs[a].at[mine], local_sems.at[a]))
        for k, (fx, fy) in enumerate(_CHIP_FLIPS):
            peer = (_flip(x, fx), _flip(y, fy), c)
            theirs = 2 * peer[0] + peer[1]
            src = ins[a] if gather else ins[a].at[theirs]
            pair = send_sems.at[3 * a + k], recv_sems.at[3 * a + k]
            remote.append(_remote(src, outs[a].at[mine], *pair, peer))
            landing.append(_remote(src, outs[a].at[theirs], *pair, peer))
    return local + remote, (local, landing, remote)


def _start(traffic):
    for cp in traffic[0]:
        cp.start()


def _finish(traffic):
    local, landing, remote = traffic[1]
    for cp in local:
        cp.wait()
    for cp in landing:
        cp.wait_recv()
    for cp in remote:
        cp.wait_send()


def _with_exchange(compute, n_in, n_out, gather, n_x, grid):
    if not n_x:
        return compute

    def body(*refs):
        ins, x_in = refs[:n_in], refs[n_in:n_in + n_x]
        outs, x_out = refs[n_in + n_x:n_in + n_x + n_out], refs[n_in + n_x + n_out:n_in + 2 * n_x + n_out]
        scratch, sems = refs[n_in + 2 * n_x + n_out:-3], refs[-3:]
        traffic = _chip_traffic(gather, x_in, x_out, sems)
        first = functools.reduce(jnp.logical_and, [pl.program_id(d) == 0 for d in range(len(grid))])
        last = functools.reduce(jnp.logical_and, [pl.program_id(d) == grid[d] - 1 for d in range(len(grid))])

        @pl.when(first)
        def _():
            _start(traffic)

        compute(*ins, *outs, *scratch)

        @pl.when(last)
        def _():
            _finish(traffic)

    return body


def _chip_call(gather, arrays, *, name):
    n = len(arrays)

    def body(*refs):
        traffic = _chip_traffic(gather, refs[:n], refs[n:2 * n], refs[2 * n:])
        _start(traffic)
        _finish(traffic)

    return pl.pallas_call(
        body, name=name, in_specs=[_ANY] * n, out_specs=[_ANY] * n, out_shape=_chip_shapes(gather, arrays),
        scratch_shapes=_chip_scratch(n))(*arrays)


def _gather_shared(shard, small, *, name):
    rows = shard.shape[0]
    assert rows % 2 == 0
    half = rows // 2

    def body(shard_ref, small_ref, out_ref, small_out_ref, send_sems, recv_sems, local_sems):
        x, y, c = _place()
        mine = 2 * x + y
        sibling = (x, y, 1 - c)
        my_rows = pl.ds(pl.multiple_of(c * half, 8), half)
        local = [pltpu.make_async_copy(shard_ref, out_ref.at[mine], local_sems.at[0]),
                 pltpu.make_async_copy(small_ref, small_out_ref.at[mine], local_sems.at[1])]
        sent, landed, passed_on, handed = [], [], [], []
        for k, (fx, fy) in enumerate(_CHIP_FLIPS):
            peer = (_flip(x, fx), _flip(y, fy), c)
            theirs = 2 * peer[0] + peer[1]
            ici, d2d, tiny = [(send_sems.at[3 * n + k], recv_sems.at[3 * n + k]) for n in range(3)]
            sent.append(_remote(shard_ref.at[my_rows], out_ref.at[mine, my_rows], *ici, peer))
            landed.append(_remote(shard_ref.at[my_rows], out_ref.at[theirs, my_rows], *ici, peer))
            sent.append(_remote(small_ref, small_out_ref.at[mine], *tiny, peer))
            landed.append(_remote(small_ref, small_out_ref.at[theirs], *tiny, peer))
            passed_on.append(_remote(out_ref.at[theirs, my_rows], out_ref.at[theirs, my_rows], *d2d, sibling))
            other_rows = pl.ds(pl.multiple_of((1 - c) * half, 8), half)
            handed.append(_remote(out_ref.at[theirs, other_rows], out_ref.at[theirs, other_rows], *d2d, sibling))
        for cp in local + sent:
            cp.start()
        for k in range(3):
            landed[2 * k].wait_recv()
            passed_on[k].start()
        for k in range(3):
            landed[2 * k + 1].wait_recv()
            handed[k].wait_recv()
        for cp in local:
            cp.wait()
        for cp in sent + passed_on:
            cp.wait_send()

    return pl.pallas_call(
        body, name=name, in_specs=[_ANY] * 2, out_specs=[_ANY] * 2, out_shape=_chip_shapes(True, [shard, small]),
        scratch_shapes=[pltpu.SemaphoreType.DMA((9,)), pltpu.SemaphoreType.DMA((9,)), pltpu.SemaphoreType.DMA((2,))],
    )(shard, small)


def _swap_pair(arrays, *, name):
    n = len(arrays)

    def body(*refs):
        ins, outs, (send_sems, recv_sems) = refs[:n], refs[n:2 * n], refs[2 * n:]
        x, y, c = _place()
        copies = [_remote(ins[a], outs[a], send_sems.at[a], recv_sems.at[a], (x, y, 1 - c)) for a in range(n)]
        for cp in copies:
            cp.start()
        for cp in copies:
            cp.wait_recv()
        for cp in copies:
            cp.wait_send()

    shapes = [jax.ShapeDtypeStruct(s.shape, s.dtype) for s in arrays]
    return _comm_call(body, arrays, shapes, n, 0, name)


def _gather_all(tile, *, name):
    def body(in_ref, out_ref, send_sems, recv_sems, local_sems):
        x, y, c = _place()
        mine = 4 * x + 2 * y + c
        local = pltpu.make_async_copy(in_ref, out_ref.at[mine], local_sems.at[0])
        remote, landing = [], []
        for k in range(1, N_DEV):
            peer = (_flip(x, k & 4), _flip(y, k & 2), _flip(c, k & 1))
            sems = send_sems.at[k - 1], recv_sems.at[k - 1]
            remote.append(_remote(in_ref, out_ref.at[mine], *sems, peer))
            landing.append(_remote(in_ref, out_ref.at[4 * peer[0] + 2 * peer[1] + peer[2]], *sems, peer))
        for cp in [local] + remote:
            cp.start()
        local.wait()
        for cp in landing:
            cp.wait_recv()
        for cp in remote:
            cp.wait_send()

    return _comm_call(body, [tile], [jax.ShapeDtypeStruct((N_DEV,) + tile.shape, tile.dtype)], N_DEV - 1, 1, name)[0]


def _sum_slots(slabs, *, name, tr=128):
    s, r, c = slabs.shape
    tr = min(tr, r)

    def body(in_ref, o_ref):
        acc = in_ref[0].astype(F32)
        for j in range(1, s):
            acc = acc + in_ref[j].astype(F32)
        o_ref[...] = acc

    return pl.pallas_call(
        body, name=name, grid=(r // tr,),
        in_specs=[pl.BlockSpec((s, tr, c), lambda i: (0, i, 0))], out_specs=pl.BlockSpec((tr, c), lambda i: (i, 0)),
        out_shape=jax.ShapeDtypeStruct((r, c), F32), compiler_params=_params("parallel"))(slabs)


def _adamw_math(w, g, m, v):
    m = ADAM_B1 * m + (1.0 - ADAM_B1) * g
    v = ADAM_B2 * v + (1.0 - ADAM_B2) * (g * g)
    m_hat = m / (1.0 - ADAM_B1 ** ADAM_STEP)
    v_hat = v / (1.0 - ADAM_B2 ** ADAM_STEP)
    delta = -ADAM_LR * (m_hat / (jnp.sqrt(v_hat) + ADAM_EPS) + ADAM_WD * w)
    return delta, m, v


def _adamw(w, m, v, parts, *, name, tr=128):
    r, c = w.shape
    tr = min(tr, r)
    s = len(parts)

    def body(w_ref, m_ref, v_ref, *refs):
        g_ref, d_ref, nm_ref, nv_ref = refs[s:]
        g = refs[0][...]
        for p_ref in refs[1:s]:
            g = g + p_ref[...]
        g_ref[...] = g
        d_ref[...], nm_ref[...], nv_ref[...] = _adamw_math(w_ref[...], g, m_ref[...], v_ref[...])

    blk = pl.BlockSpec((tr, c), lambda i: (i, 0))
    shape = jax.ShapeDtypeStruct((r, c), F32)
    return pl.pallas_call(
        body, name=name, grid=(r // tr,), in_specs=[blk] * (3 + s), out_specs=[blk] * 4, out_shape=[shape] * 4,
        compiler_params=_params("parallel"))(w, m, v, *parts)


_BIG = ("a_w_in", "b_w_in", "a_w_out", "b_w_out", "a_conv_w")
_SMALL = ("norm_w", "a_a_log", "a_dt_bias", "a_out_norm_w", "b_q_norm_w", "b_k_norm_w", "b_rel_bias")
_ORDER = ("norm_w", "a_w_in", "a_conv_w", "a_a_log", "a_dt_bias", "a_out_norm_w", "a_w_out", "b_w_in", "b_q_norm_w",
          "b_k_norm_w", "b_rel_bias", "b_w_out")


def _join_cols(g):
    return jnp.transpose(g, (1, 0, 2)).reshape(g.shape[1], -1)


def _split_cols(g):
    return jnp.transpose(g.reshape(g.shape[0], N_CHIPS, -1), (1, 0, 2))


def _split_rows(g):
    return g.reshape(N_CHIPS, -1, g.shape[-1])


def _pack(d):
    flat = jnp.concatenate([d[n].reshape(-1) for n in _SMALL])
    return jnp.pad(flat, (0, -flat.shape[0] % LANES)).reshape(1, -1)


def _unpack(row, like):
    out, at = {}, 0
    for n in _SMALL:
        size = like[n].size
        out[n] = row[0, at:at + size].reshape(like[n].shape)
        at += size
    return out


def kernel(x, norm_w, a_w_in, a_conv_w, a_a_log, a_dt_bias, a_out_norm_w, a_w_out, b_w_in, b_q_norm_w, b_k_norm_w, b_rel_bias, b_w_out, loss_target, m_norm_w, m_a_w_in, m_a_conv_w, m_a_a_log, m_a_dt_bias, m_a_out_norm_w, m_a_w_out, m_b_w_in, m_b_q_norm_w, m_b_k_norm_w, m_b_rel_bias, m_b_w_out, v_norm_w, v_a_w_in, v_a_conv_w, v_a_a_log, v_a_dt_bias, v_a_out_norm_w, v_a_w_out, v_b_w_in, v_b_q_norm_w, v_b_k_norm_w, v_b_rel_bias, v_b_w_out):
    w = dict(norm_w=norm_w, a_w_in=a_w_in, a_conv_w=a_conv_w, a_a_log=a_a_log, a_dt_bias=a_dt_bias,
             a_out_norm_w=a_out_norm_w, a_w_out=a_w_out, b_w_in=b_w_in, b_q_norm_w=b_q_norm_w, b_k_norm_w=b_k_norm_w,
             b_rel_bias=b_rel_bias, b_w_out=b_w_out)
    m = dict(norm_w=m_norm_w, a_w_in=m_a_w_in, a_conv_w=m_a_conv_w, a_a_log=m_a_a_log, a_dt_bias=m_a_dt_bias,
             a_out_norm_w=m_a_out_norm_w, a_w_out=m_a_w_out, b_w_in=m_b_w_in, b_q_norm_w=m_b_q_norm_w,
             b_k_norm_w=m_b_k_norm_w, b_rel_bias=m_b_rel_bias, b_w_out=m_b_w_out)
    v = dict(norm_w=v_norm_w, a_w_in=v_a_w_in, a_conv_w=v_a_conv_w, a_a_log=v_a_a_log, a_dt_bias=v_a_dt_bias,
             a_out_norm_w=v_a_out_norm_w, a_w_out=v_a_w_out, b_w_in=v_b_w_in, b_q_norm_w=v_b_q_norm_w,
             b_k_norm_w=v_b_k_norm_w, b_rel_bias=v_b_rel_bias, b_w_out=v_b_w_out)

    wa_in, conv = _gather_shared(a_w_in[0].astype(BF16), a_conv_w[0], name="gather_a_in")
    loss, dx, grads = _local_step(
        x[0], loss_target[0], norm_w, _join_cols(wa_in), _join_cols(conv), a_a_log, a_dt_bias, a_out_norm_w,
        a_w_out[0].astype(BF16), b_w_in[0].astype(BF16), b_q_norm_w, b_k_norm_w, b_rel_bias, b_w_out[0].astype(BF16),
        sharded=True)
    loss = lax.psum(loss, ("x", "y", "c"))

    mine = [_sum_slots(grads[n], name=f"chip_sum_{n}") for n in _BIG]
    theirs = _swap_pair(mine, name="pair_grads")
    out = {}
    for n, p, q in zip(_BIG, mine, theirs):
        out[n] = [r[None] for r in _adamw(w[n][0], m[n][0], v[n][0], [p, q], name=f"adamw_{n}")]

    row = _pack(grads)
    tiles = _gather_all(jnp.broadcast_to(row, (8, row.shape[1])), name="gather_small_grads")
    res = _adamw(_pack(w), _pack(m), _pack(v), [tiles[d, 0:1, :] for d in range(N_DEV)], name="adamw_small")
    unpacked = [_unpack(r, w) for r in res]
    for n in _SMALL:
        out[n] = [u[n] for u in unpacked]

    return (loss, dx[None], *[out[n][0] for n in _ORDER], *[out[n][1] for n in _ORDER], *[out[n][2] for n in _ORDER],
            *[out[n][3] for n in _ORDER])
```

```python
import functools

import numpy as np
import jax
import jax.numpy as jnp
from jax import lax
from jax.experimental import pallas as pl
from jax.experimental.pallas import tpu as pltpu

F32 = jnp.float32
BF16 = jnp.bfloat16

CHUNK = 64
HEAD_DIM = 128
LEFT_CHUNKS = 8
REL_CLIP = 256
CONV_K = 4
EPS = 1e-6
HALO = 8

ADAM_LR = 0.001
ADAM_B1 = 0.9
ADAM_B2 = 0.999
ADAM_EPS = 1e-08
ADAM_WD = 0.01
ADAM_STEP = 10

LANES = 128
N_CHIPS = 4
N_DEV = 8
VMEM_LIMIT_BYTES = 56 * 1024 * 1024
MESH = pl.DeviceIdType.MESH
HIGHEST = lax.Precision.HIGHEST


def _params(*sem):
    return pltpu.CompilerParams(dimension_semantics=sem, vmem_limit_bytes=VMEM_LIMIT_BYTES)


def _dot(a, b, dims=(((1,), (0,)), ((), ())), precision=None):
    return lax.dot_general(a, b, dims, precision=precision, preferred_element_type=F32)


_NT = (((1,), (1,)), ((), ()))
_TN = (((0,), (0,)), ((), ()))


def _bdot(a, b, dims=(((1,), (0,)), ((), ()))):
    return _dot(a.astype(BF16), b.astype(BF16), dims)


def _fdot(a, b, dims=(((1,), (0,)), ((), ()))):
    return _dot(a, b, dims, precision=lax.Precision.HIGH)


def _silu(x):
    return x * jax.nn.sigmoid(x)


def _matmul(a, b, *, name, trans_a=False, trans_b=False, residual=None, out_dtype=F32, tm=1024, tn=1024, tk=2048,
            exchange=()):
    assert not (trans_a and trans_b)
    k, m = a.shape if trans_a else a.shape[::-1]
    n = b.shape[0] if trans_b else b.shape[1]
    tm, tn, tk = min(tm, m), min(tn, n), min(tk, k)
    assert m % tm == 0 and n % tn == 0 and k % tk == 0, (a.shape, b.shape, tm, tn, tk)
    nk = k // tk
    dims = _NT if trans_b else _TN if trans_a else (((1,), (0,)), ((), ()))

    def body(*refs):
        if residual is None:
            a_ref, b_ref, o_ref, acc_ref = refs
            r_ref = None
        else:
            a_ref, b_ref, r_ref, o_ref, acc_ref = refs
        kk = pl.program_id(2)

        @pl.when(kk == 0)
        def _():
            acc_ref[...] = jnp.zeros_like(acc_ref)

        acc_ref[...] += _dot(a_ref[...], b_ref[...], dims)

        @pl.when(kk == nk - 1)
        def _():
            r = acc_ref[...]
            if r_ref is not None:
                r = r + r_ref[...]
            o_ref[...] = r.astype(o_ref.dtype)

    in_specs = [
        pl.BlockSpec((tk, tm), lambda i, j, kk: (kk, i)) if trans_a else pl.BlockSpec((tm, tk), lambda i, j, kk: (i, kk)),
        pl.BlockSpec((tn, tk), lambda i, j, kk: (j, kk)) if trans_b else pl.BlockSpec((tk, tn), lambda i, j, kk: (kk, j)),
    ]
    args = [a, b]
    if residual is not None:
        in_specs.append(pl.BlockSpec((tm, tn), lambda i, j, kk: (i, j)))
        args.append(residual)
    grid = (m // tm, n // tn, nk)
    n_x = len(exchange)
    out, *landed = pl.pallas_call(
        _with_exchange(body, len(args), 1, False, n_x, grid),
        name=name,
        grid=grid,
        in_specs=in_specs + [_ANY] * n_x,
        out_specs=[pl.BlockSpec((tm, tn), lambda i, j, kk: (i, j))] + [_ANY] * n_x,
        out_shape=[jax.ShapeDtypeStruct((m, n), out_dtype)] + _chip_shapes(False, exchange),
        scratch_shapes=[pltpu.VMEM((tm, tn), F32)] + (_chip_scratch(n_x) if n_x else []),
        compiler_params=_params(*(("arbitrary",) * 3 if n_x else ("parallel", "parallel", "arbitrary"))),
    )(*args, *exchange)
    return (out, landed) if n_x else out


def _rms(x, w):
    return x * lax.rsqrt(jnp.mean(x * x, axis=-1, keepdims=True) + EPS) * w


def _rmsnorm_fwd(x, w_row, *, name, tr=512):
    t, d = x.shape
    tr = min(tr, t)

    def body(x_ref, w_ref, o_ref):
        o_ref[...] = _rms(x_ref[...], w_ref[...]).astype(BF16)

    return pl.pallas_call(
        body,
        name=name,
        grid=(t // tr,),
        in_specs=[pl.BlockSpec((tr, d), lambda i: (i, 0)), pl.BlockSpec((1, d), lambda i: (0, 0))],
        out_specs=pl.BlockSpec((tr, d), lambda i: (i, 0)),
        out_shape=jax.ShapeDtypeStruct((t, d), BF16),
        compiler_params=_params("parallel"),
    )(x, w_row)


def _rmsnorm_bwd(x, w_row, dy, dres, *, name, tr=256):
    t, d = x.shape
    tr = min(tr, t)

    def body(x_ref, w_ref, dy_ref, dres_ref, dx_ref, dxb_ref, dw_ref):
        @pl.when(pl.program_id(0) == 0)
        def _():
            dw_ref[...] = jnp.zeros_like(dw_ref)

        _, vjp = jax.vjp(_rms, x_ref[...], w_ref[...])
        dx, dw = vjp(dy_ref[...])
        dx = dx + dres_ref[...]
        dx_ref[...] = dx
        dxb_ref[...] = dx.astype(BF16)
        dw_ref[...] += dw

    row = pl.BlockSpec((tr, d), lambda i: (i, 0))
    vec = pl.BlockSpec((1, d), lambda i: (0, 0))
    return pl.pallas_call(
        body,
        name=name,
        grid=(t // tr,),
        in_specs=[row, vec, row, row],
        out_specs=[row, row, vec],
        out_shape=[jax.ShapeDtypeStruct((t, d), F32), jax.ShapeDtypeStruct((t, d), BF16), jax.ShapeDtypeStruct((1, d), F32)],
        compiler_params=_params("arbitrary"),
    )(x, w_row, dy, dres)


def _loss_head(h, target, *, name, tr=512):
    t, d = h.shape
    tr = min(tr, t)

    def body(h_ref, t_ref, dh_ref, dhb_ref, part_ref):
        @pl.when(pl.program_id(0) == 0)
        def _():
            part_ref[...] = jnp.zeros_like(part_ref)

        err = h_ref[...] - t_ref[...]
        dh = err * (1.0 / d)
        dh_ref[...] = dh
        dhb_ref[...] = dh.astype(BF16)
        part_ref[...] += jnp.sum(err * err, axis=0, keepdims=True)

    row = pl.BlockSpec((tr, d), lambda i: (i, 0))
    vec = pl.BlockSpec((1, d), lambda i: (0, 0))
    dh, dhb, part = pl.pallas_call(
        body,
        name=name,
        grid=(t // tr,),
        in_specs=[row, row],
        out_specs=[row, row, vec],
        out_shape=[jax.ShapeDtypeStruct((t, d), F32), jax.ShapeDtypeStruct((t, d), BF16), jax.ShapeDtypeStruct((1, d), F32)],
        compiler_params=_params("arbitrary"),
    )(h, target)
    return 0.5 / d * jnp.sum(part), dh, dhb


_BNN = (((2,), (1,)), ((0,), (0,)))
_BNT = (((2,), (2,)), ((0,), (0,)))
_BTN = (((1,), (1,)), ((0,), (0,)))


_TAP0 = HALO - (CONV_K - 1)


def _conv(x_ref, w, rows):
    c = w[0:1, :] * x_ref[_TAP0:_TAP0 + rows, :]
    for j in range(1, CONV_K):
        c = c + w[j:j + 1, :] * x_ref[_TAP0 + j:_TAP0 + j + rows, :]
    return c


def _conv_silu_bwd(x_ref, w, dact, dc_ref, rows):
    c = _conv(x_ref, w, rows)
    sig = jax.nn.sigmoid(c)
    dc = dact * (sig * (1.0 + c * (1.0 - sig)))
    dw = [jnp.sum(dc * x_ref[_TAP0 + j:_TAP0 + j + rows, :], axis=0, keepdims=True) for j in range(CONV_K)]
    dc_ref[0:HALO, :] = jnp.zeros((HALO, HEAD_DIM), F32)
    dc_ref[HALO:HALO + rows, :] = dc
    dc_ref[HALO + rows:HALO + rows + HALO, :] = jnp.zeros((HALO, HEAD_DIM), F32)
    first = HALO - _TAP0
    dx = w[0:1, :] * dc_ref[first:first + HALO + rows, :]
    for j in range(1, CONV_K):
        dx = dx + w[j:j + 1, :] * dc_ref[first - j:first - j + HALO + rows, :]
    return dx, dw


@jax.custom_vjp
def _unit_lower_inverse(neg_l):
    n = neg_l.shape[0]
    eye = (lax.broadcasted_iota(jnp.int32, (n, CHUNK, CHUNK), 1) == lax.broadcasted_iota(jnp.int32, (n, CHUNK, CHUNK), 2))
    inv = eye.astype(F32) + neg_l
    power = _bdot(neg_l, neg_l, _BNN)
    for _ in range(4):
        both = _bdot(jnp.concatenate([inv, power], axis=1), power, _BNN)
        inv, power = inv + both[:, :CHUNK], both[:, CHUNK:]
    return inv + _bdot(inv, power, _BNN)


def _unit_lower_inverse_fwd(neg_l):
    inv = _unit_lower_inverse(neg_l)
    return inv, inv


def _unit_lower_inverse_bwd(inv, dinv):
    return (_fdot(_fdot(inv, dinv, _BTN), inv, _BNT),)


_unit_lower_inverse.defvjp(_unit_lower_inverse_fwd, _unit_lower_inverse_bwd)


def _gdn_intra(qt, kt, v, a, b, alog, dtb):
    n = a.shape[0] // CHUNK
    q = qt * lax.rsqrt(jnp.sum(qt * qt, axis=-1, keepdims=True) + EPS) * (HEAD_DIM ** -0.5)
    k = kt * lax.rsqrt(jnp.sum(kt * kt, axis=-1, keepdims=True) + EPS)
    lanes = jnp.ones((1, HEAD_DIM), F32)
    beta = jax.nn.sigmoid(b) * lanes
    sp = a + dtb
    g = (-jnp.exp(alog) * (jnp.maximum(sp, 0.0) + jnp.log(1.0 + jnp.exp(-jnp.abs(sp))))) * lanes
    q, k, v, beta, g = (t.reshape(n, CHUNK, HEAD_DIM) for t in (q, k, v, beta, g))

    row = lax.broadcasted_iota(jnp.int32, (n, CHUNK, CHUNK), 1)
    col = lax.broadcasted_iota(jnp.int32, (n, CHUNK, CHUNK), 2)
    tri_incl = row >= col
    tri_strict = row > col
    gc = _fdot(tri_incl.astype(F32), g, _BNN)
    gc_row = _fdot(g[:, :, :CHUNK], (row <= col).astype(F32), _BTN)
    decay = jnp.exp(jnp.where(tri_incl, gc[:, :, :CHUNK] - gc_row, -1e30))
    kb = k * beta
    vb = v * beta
    with_k = _bdot(jnp.concatenate([kb, q], axis=1), k, _BNT)
    neg_l = jnp.where(tri_strict, -(with_k[:, :CHUNK] * decay), 0.0)
    qk = jnp.where(tri_incl, with_k[:, CHUNK:] * decay, 0.0)
    inv = _unit_lower_inverse(neg_l)
    e = jnp.exp(gc)
    solved = _bdot(inv, jnp.concatenate([kb * e, vb], axis=2), _BNN)
    g_last = gc[:, CHUNK - 1:CHUNK, :]
    k_dec = k * jnp.exp(g_last - gc)
    from_k = _bdot(k_dec, solved, _BTN)
    from_qk = _bdot(qk, solved, _BNN)
    step, add = -from_k[:, :, :HEAD_DIM], from_k[:, :, HEAD_DIM:]
    read, out = q * e - from_qk[:, :, :HEAD_DIM], from_qk[:, :, HEAD_DIM:]
    return step, add, jnp.exp(g_last), read, out


def _gdn_scan_step(state, step, add, decay_last):
    return state * decay_last + _bdot(step, state) + add


def _gdn_outputs(states, read, out, z, onw):
    return _rms(_bdot(read, states, _BNN) + out, onw) * _silu(z)


def _scan_scratch(n, dtype):
    return [pltpu.VMEM((n, HEAD_DIM, HEAD_DIM), dtype), pltpu.VMEM((n, HEAD_DIM, HEAD_DIM), F32), pltpu.VMEM((n, 1, HEAD_DIM), F32)]


def _head_lane(h, offset=0):
    return lax.broadcasted_iota(jnp.int32, (1, LANES), 1) == h + offset


def _pick(mask, x):
    return jnp.sum(jnp.where(mask, x, 0.0), axis=1, keepdims=True)


def _gdn_specs(heads, tb, rev, nb, PAIR):
    assert heads % PAIR == 0
    blk = (lambda i: nb - 1 - i) if rev else (lambda i: i)
    hb = tb // HALO
    width, pairs = PAIR * HEAD_DIM, heads // PAIR

    def col(group):
        return pl.BlockSpec((tb, width), lambda i, h: (blk(i), group * pairs + h))

    def halo(group):
        return pl.BlockSpec((HALO, width), lambda i, h: (jnp.maximum(blk(i) * hb - 1, 0), group * pairs + h))

    def convw(group):
        return pl.BlockSpec((CONV_K, width), lambda i, h: (0, group * pairs + h))

    vec = pl.BlockSpec((1, LANES), lambda i, h: (0, 0))
    ab = pl.BlockSpec((tb, LANES), lambda i, h: (blk(i), 0))
    states = pl.BlockSpec((PAIR, tb // CHUNK, HEAD_DIM, HEAD_DIM), lambda i, h: (h, blk(i), 0, 0))
    return blk, col, halo, convw, vec, ab, states


def _head_cols(p):
    return slice(p * HEAD_DIM, (p + 1) * HEAD_DIM)


def _gdn_fwd(proj, ab, conv_w, alog_row, dtb_row, onw_row, *, heads, name, tb=1024, pair=2, gather=()):
    t = proj.shape[0]
    tb = min(tb, t)
    nb, cpb = t // tb, tb // CHUNK
    PAIR = pair
    _, col, halo, convw, vec, abspec, states = _gdn_specs(heads, tb, False, nb, PAIR)

    def body(q_ref, k_ref, v_ref, qh_ref, kh_ref, vh_ref, z_ref, ab_ref, wq_ref, wk_ref, wv_ref, alog_ref, dtb_ref, onw_ref,
             og_ref, st_ref, state_scr, x_scr, *op_scr):
        i, pair = pl.program_id(0), pl.program_id(1)
        abv = ab_ref[...]
        heads_here, later = [pair * PAIR + p for p in range(PAIR)], []
        for p, h in enumerate(heads_here):
            cols = _head_cols(p)
            for n, (ref, href) in enumerate(((q_ref, qh_ref), (k_ref, kh_ref), (v_ref, vh_ref))):
                x_scr[p, n, 0:HALO, :] = jnp.where(i > 0, href[:, cols], 0.0)
                x_scr[p, n, HALO:HALO + tb, :] = ref[:, cols]
            sel_a, sel_b = _head_lane(h), _head_lane(h, heads)
            alog, dtb = _pick(sel_a, alog_ref[...]), _pick(sel_a, dtb_ref[...])
            acts = [_silu(_conv(x_scr.at[p, n], w_ref[:, cols], tb)) for n, w_ref in enumerate((wq_ref, wk_ref, wv_ref))]
            *scan, read, out = _gdn_intra(*acts, _pick(sel_a, abv), _pick(sel_b, abv), alog, dtb)
            for scr, val in zip(op_scr[3 * p:3 * p + 3], scan):
                scr[...] = val.astype(scr.dtype)
            later.append((read, out))

        def chunk(c, states):
            for p in range(PAIR):
                st_ref[p, c] = states[p]
            return tuple(_gdn_scan_step(states[p], *[scr[c] for scr in op_scr[3 * p:3 * p + 3]]) for p in range(PAIR))

        @pl.when(i == 0)
        def _():
            for h in heads_here:
                state_scr[h] = jnp.zeros((HEAD_DIM, HEAD_DIM), F32)

        last = lax.fori_loop(0, cpb, chunk, tuple(state_scr[h] for h in heads_here))
        for p, h in enumerate(heads_here):
            cols = _head_cols(p)
            state_scr[h] = last[p]
            og = _gdn_outputs(st_ref[p], *later[p], z_ref[:, cols].reshape(cpb, CHUNK, HEAD_DIM), onw_ref[...])
            og_ref[:, cols] = og.reshape(tb, HEAD_DIM).astype(BF16)

    n_x = len(gather)
    grid = (nb, heads // PAIR)
    og, st, *gathered = pl.pallas_call(
        _with_exchange(body, 14, 2, True, n_x, grid),
        name=name,
        grid=grid,
        in_specs=[col(0), col(1), col(2), halo(0), halo(1), halo(2), col(3), abspec, convw(0), convw(1), convw(2), vec, vec, vec]
        + [_ANY] * n_x,
        out_specs=[pl.BlockSpec((tb, PAIR * HEAD_DIM), lambda i, h: (i, h)), states] + [_ANY] * n_x,
        out_shape=[jax.ShapeDtypeStruct((t, heads * HEAD_DIM), BF16),
                   jax.ShapeDtypeStruct((heads, t // CHUNK, HEAD_DIM, HEAD_DIM), F32)] + _chip_shapes(True, gather),
        scratch_shapes=[pltpu.VMEM((heads, HEAD_DIM, HEAD_DIM), F32), pltpu.VMEM((PAIR, 3, HALO + tb, HEAD_DIM), F32)]
        + _scan_scratch(cpb, BF16) * PAIR + (_chip_scratch(n_x) if n_x else []),
        compiler_params=_params("arbitrary", "arbitrary"),
    )(proj, proj, proj, proj, proj, proj, proj, ab, conv_w, conv_w, conv_w, alog_row, dtb_row, onw_row, *gather)
    return og, st, gathered


def _gdn_bwd(proj, ab, conv_w, alog_row, dtb_row, onw_row, states, dog, *, heads, name, tb=1024, pair=1, exchange=()):
    t = proj.shape[0]
    tb = min(tb, t)
    nb, cpb = t // tb, tb // CHUNK
    PAIR = pair
    _, col, halo, convw, vec, abspec, states_spec = _gdn_specs(heads, tb, True, nb, PAIR)
    n_conv = conv_w.shape[1]

    def body(q_ref, k_ref, v_ref, qh_ref, kh_ref, vh_ref, z_ref, ab_ref, wq_ref, wk_ref, wv_ref, alog_ref, dtb_ref, onw_ref,
             st_ref, dog_ref, dq_ref, dk_ref, dv_ref, dz_ref, dab_ref, dconv_ref, dalog_ref, ddtb_ref, donw_ref,
             dstate_scr, x_scr, carry_scr, *scr):
        op_scr, dop_scr, dstates_scr, dc_scr = scr[:3 * PAIR], scr[3 * PAIR:6 * PAIR], scr[6 * PAIR:7 * PAIR], scr[7 * PAIR]
        i, pair = pl.program_id(0), pl.program_id(1)
        first_block = i == nb - 1
        heads_here, later = [pair * PAIR + p for p in range(PAIR)], []

        @pl.when(jnp.logical_and(i == 0, pair == 0))
        def _():
            dconv_ref[...] = jnp.zeros_like(dconv_ref)
            dalog_ref[...] = jnp.zeros_like(dalog_ref)
            ddtb_ref[...] = jnp.zeros_like(ddtb_ref)
            donw_ref[...] = jnp.zeros_like(donw_ref)

        @pl.when(pair == 0)
        def _():
            dab_ref[...] = jnp.zeros_like(dab_ref)

        @pl.when(i == 0)
        def _():
            for h in heads_here:
                dstate_scr[h] = jnp.zeros((HEAD_DIM, HEAD_DIM), F32)
                carry_scr[h] = jnp.zeros((3, HALO, HEAD_DIM), F32)

        abv = ab_ref[...]
        w_refs = (wq_ref, wk_ref, wv_ref)
        for p, h in enumerate(heads_here):
            cols = _head_cols(p)
            for n, (ref, href) in enumerate(((q_ref, qh_ref), (k_ref, kh_ref), (v_ref, vh_ref))):
                x_scr[p, n, 0:HALO, :] = jnp.where(first_block, 0.0, href[:, cols])
                x_scr[p, n, HALO:HALO + tb, :] = ref[:, cols]
            sel_a, sel_b = _head_lane(h), _head_lane(h, heads)
            alog, dtb = _pick(sel_a, alog_ref[...]), _pick(sel_a, dtb_ref[...])
            acts = [_silu(_conv(x_scr.at[p, n], w_ref[:, cols], tb)) for n, w_ref in enumerate(w_refs)]
            (*scan, read, out), vjp_intra = jax.vjp(_gdn_intra, *acts, _pick(sel_a, abv), _pick(sel_b, abv), alog, dtb)
            for s, val in zip(op_scr[3 * p:3 * p + 3], scan):
                s[...] = val.astype(s.dtype)
            blocked = lambda ref: ref[:, cols].reshape(cpb, CHUNK, HEAD_DIM)
            _, vjp_outputs = jax.vjp(_gdn_outputs, st_ref[p], read, out, blocked(z_ref), onw_ref[...])
            dstates_scr[p][...], dread, dout, dz, donw = vjp_outputs(blocked(dog_ref))
            dz_ref[:, cols] = dz.reshape(tb, HEAD_DIM).astype(BF16)
            donw_ref[...] += donw
            later.append((vjp_intra, dread, dout, sel_a, sel_b))

        def chunk(i_rev, dstates):
            c = cpb - 1 - i_rev
            new = []
            for p in range(PAIR):
                _, vjp = jax.vjp(_gdn_scan_step, st_ref[p, c], *[s[c].astype(F32) for s in op_scr[3 * p:3 * p + 3]])
                dstate, *grads = vjp(dstates[p])
                for s, val in zip(dop_scr[3 * p:3 * p + 3], grads):
                    s[c] = val
                new.append(dstate + dstates_scr[p][c])
            return tuple(new)

        last = lax.fori_loop(0, cpb, chunk, tuple(dstate_scr[h] for h in heads_here))
        for p, h in enumerate(heads_here):
            cols = _head_cols(p)
            vjp_intra, dread, dout, sel_a, sel_b = later[p]
            dstate_scr[h] = last[p]
            *dacts, da, db, dalog, ddtb = vjp_intra((*[s[...] for s in dop_scr[3 * p:3 * p + 3]], dread, dout))
            dab_ref[...] += jnp.where(sel_a, da, 0.0) + jnp.where(sel_b, db, 0.0)
            for n, (dref, dact, w_ref) in enumerate(zip((dq_ref, dk_ref, dv_ref), dacts, w_refs)):
                dx, dw = _conv_silu_bwd(x_scr.at[p, n], w_ref[:, cols], dact, dc_scr, tb)
                x_scr[p, n] = dx
                x_scr[p, n, tb:tb + HALO, :] += carry_scr[h, n]
                carry_scr[h, n] = x_scr[p, n, 0:HALO, :]
                dref[:, cols] = x_scr[p, n, HALO:HALO + tb, :].astype(BF16)
                lanes = pl.ds(pl.multiple_of((n * heads + h) * HEAD_DIM, HEAD_DIM), HEAD_DIM)
                for j in range(CONV_K):
                    dconv_ref[j:j + 1, lanes] += dw[j]
            dalog_ref[...] += jnp.where(sel_a, dalog, 0.0)
            ddtb_ref[...] += jnp.where(sel_a, ddtb, 0.0)

    out_col = pl.BlockSpec((tb, PAIR * HEAD_DIM), lambda i, h: (nb - 1 - i, h))
    dog_spec = out_col
    col_shape = jax.ShapeDtypeStruct((t, heads * HEAD_DIM), BF16)
    row_shape = jax.ShapeDtypeStruct((1, LANES), F32)
    n_x = len(exchange)
    grid = (nb, heads // PAIR)
    outs = pl.pallas_call(
        _with_exchange(body, 16, 9, False, n_x, grid),
        name=name,
        grid=grid,
        in_specs=[col(0), col(1), col(2), halo(0), halo(1), halo(2), col(3), abspec, convw(0), convw(1), convw(2), vec, vec, vec,
                  states_spec, dog_spec] + [_ANY] * n_x,
        out_specs=[out_col, out_col, out_col, out_col, abspec,
                   pl.BlockSpec((CONV_K, n_conv), lambda i, h: (0, 0)), vec, vec, vec] + [_ANY] * n_x,
        out_shape=[col_shape, col_shape, col_shape, col_shape, jax.ShapeDtypeStruct((t, LANES), F32),
                   jax.ShapeDtypeStruct((CONV_K, n_conv), F32), row_shape, row_shape, row_shape] + _chip_shapes(False, exchange),
        scratch_shapes=[pltpu.VMEM((heads, HEAD_DIM, HEAD_DIM), F32), pltpu.VMEM((PAIR, 3, HALO + tb, HEAD_DIM), F32),
                        pltpu.VMEM((heads, 3, HALO, HEAD_DIM), F32)] + _scan_scratch(cpb, BF16) * PAIR
        + _scan_scratch(cpb, F32) * PAIR + [pltpu.VMEM((cpb, HEAD_DIM, HEAD_DIM), F32)] * PAIR
        + [pltpu.VMEM((HALO + tb + HALO, HEAD_DIM), F32)]
        + (_chip_scratch(n_x) if n_x else []),
        compiler_params=_params("arbitrary", "arbitrary"),
    )(proj, proj, proj, proj, proj, proj, proj, ab, conv_w, conv_w, conv_w, alog_row, dtb_row, onw_row, states, dog, *exchange)
    return (*outs[:9], outs[9:])


BAND = (LEFT_CHUNKS + 1) * CHUNK
PAD = LEFT_CHUNKS * CHUNK
GROUP = 2
ROWS = GROUP * CHUNK
WIN = (LEFT_CHUNKS + GROUP) * CHUNK
DIAGS = WIN + ROWS - 1
NEAR = PAD + ROWS - 1 - REL_CLIP
assert 0 < NEAR < DIAGS and WIN - PAD - 1 <= REL_CLIP and WIN % LANES == 0
ATTN_BLOCK = 1024


def _band_bias(rel_bias):
    heads = rel_bias.shape[0]
    far = jnp.broadcast_to(rel_bias[:, 2 * REL_CLIP:], (heads, NEAR + 1))
    near = rel_bias[:, 2 * REL_CLIP + NEAR + 1 - DIAGS:2 * REL_CLIP][:, ::-1]
    diag = jnp.concatenate([far, near], axis=1)
    return jnp.stack([diag[:, ROWS - 1 - r:ROWS - 1 - r + WIN] for r in range(ROWS)], axis=1)


def _band_bias_grad(dbias):
    heads = dbias.shape[0]
    diag = sum(jnp.pad(dbias[:, r, :], ((0, 0), (ROWS - 1 - r, r))) for r in range(ROWS))
    far = jnp.sum(diag[:, :NEAR + 1], axis=1, keepdims=True)
    near = diag[:, NEAR + 1:][:, ::-1]
    unused = jnp.zeros((heads, 2 * REL_CLIP - near.shape[1]), F32)
    return jnp.concatenate([unused, near, far], axis=1)


def _masked_bias(bias, n):
    r = np.arange(ROWS)[:, None]
    key = np.arange(WIN)[None, :]
    band_start = (r // CHUNK) * CHUNK
    in_band = np.logical_and(key >= band_start, key < band_start + BAND)
    in_sequence = key[None] >= PAD - np.arange(n)[:, None, None] * ROWS
    first = jnp.where(np.logical_and(in_band[None], in_sequence)[None], bias[:, None], -1e30)
    return first, jnp.where(in_band[None, None], bias[:, None], -1e30)


def _attn_groups(q_pre, z, kn, v, bias, qnw):
    q = _rms(q_pre, qnw)
    s = _bdot(q, kn, _BNT) * (HEAD_DIM ** -0.5) + bias
    p = jnp.exp(s - jnp.max(s, axis=-1, keepdims=True))
    p = p / jnp.sum(p, axis=-1, keepdims=True)
    return _bdot(p, v, _BNN) * _silu(z)


def _attn_groups_bwd(q_pre, z, kn, v, bias, qnw, dog):
    scale = HEAD_DIM ** -0.5
    inv_rms = lax.rsqrt(jnp.mean(q_pre * q_pre, axis=-1, keepdims=True) + EPS)
    q_hat = q_pre * inv_rms
    q_b = (q_hat * qnw).astype(BF16)
    s = _dot(q_b, kn, _BNT) * scale + bias
    e = jnp.exp(s - jnp.max(s, axis=-1, keepdims=True))
    p = e * (1.0 / jnp.sum(e, axis=-1, keepdims=True))
    p_b = p.astype(BF16)
    o = _dot(p_b, v, _BNN)
    sig = jax.nn.sigmoid(z)
    do = dog * (z * sig)
    dz = dog * o * (sig * (1.0 + z * (1.0 - sig)))
    do_b = do.astype(BF16)
    dv = _dot(p_b, do_b, _BTN)
    dp = _dot(do_b, v, _BNT)
    ds = p * (dp - jnp.sum(do * o, axis=-1, keepdims=True))
    ds_b = (ds * scale).astype(BF16)
    dq = _dot(ds_b, kn, _BNN)
    dkn = _dot(ds_b, q_b, _BTN)
    dqnw = jnp.sum(jnp.sum(dq * q_hat, axis=0), axis=0, keepdims=True)
    dq_hat = dq * qnw
    dq_pre = inv_rms * (dq_hat - q_hat * jnp.mean(dq_hat * q_hat, axis=-1, keepdims=True))
    return dq_pre, dz, dkn, dv, jnp.sum(ds, axis=0), dqnw


def _attn_specs(heads, tb, t):
    def col(group):
        return pl.BlockSpec((tb, HEAD_DIM), lambda h, i: (i, group * heads + h))

    def full(group):
        return pl.BlockSpec((t, HEAD_DIM), lambda h, i: (0, group * heads + h))

    bias = [pl.BlockSpec((1, tb // ROWS, ROWS, WIN), lambda h, i: (h, 0, 0, 0)),
            pl.BlockSpec((1, 1, ROWS, WIN), lambda h, i: (h, 0, 0, 0))]
    vec = pl.BlockSpec((1, HEAD_DIM), lambda h, i: (0, 0))
    return col, full, bias, vec


def _attn_windows(scr, block_start, n):
    return jnp.stack([scr[pl.ds(pl.multiple_of(block_start + g * ROWS, ROWS), WIN), :] for g in range(n)])


def _attn_fill(k_ref, v_ref, knw_ref, kn_scr, v_scr, t):
    kn_scr[0:PAD, :] = jnp.zeros((PAD, HEAD_DIM), BF16)
    v_scr[0:PAD, :] = jnp.zeros((PAD, HEAD_DIM), BF16)
    step = min(512, t)

    def fill(j, _):
        rows = pl.ds(pl.multiple_of(j * step, step), step)
        prows = pl.ds(pl.multiple_of(PAD + j * step, CHUNK), step)
        kn_scr[prows, :] = _rms(k_ref[rows, :], knw_ref[...]).astype(BF16)
        v_scr[prows, :] = v_ref[rows, :].astype(BF16)
        return 0

    lax.fori_loop(0, t // step, fill, 0)


def _attn_fwd(proj, bias, qnw_row, knw_row, *, heads, name, tb=ATTN_BLOCK):
    t = proj.shape[0]
    tb = min(tb, t)
    nb, ng = t // tb, tb // ROWS
    col, full, bias_spec, vec = _attn_specs(heads, tb, t)

    def body(q_ref, k_ref, v_ref, z_ref, first_ref, rest_ref, qnw_ref, knw_ref, og_ref, kn_scr, v_scr):
        i = pl.program_id(1)

        @pl.when(i == 0)
        def _():
            _attn_fill(k_ref, v_ref, knw_ref, kn_scr, v_scr, t)

        def run(block_bias):
            start = i * tb
            og = _attn_groups(q_ref[...].reshape(ng, ROWS, HEAD_DIM), z_ref[...].reshape(ng, ROWS, HEAD_DIM),
                              _attn_windows(kn_scr, start, ng), _attn_windows(v_scr, start, ng), block_bias, qnw_ref[...])
            og_ref[...] = og.reshape(tb, HEAD_DIM).astype(BF16)

        pl.when(i == 0)(lambda: run(first_ref[0]))
        pl.when(i > 0)(lambda: run(rest_ref[0]))

    return pl.pallas_call(
        body,
        name=name,
        grid=(heads, nb),
        in_specs=[col(0), full(1), full(2), col(3), *bias_spec, vec, vec],
        out_specs=pl.BlockSpec((tb, HEAD_DIM), lambda h, i: (i, h)),
        out_shape=jax.ShapeDtypeStruct((t, heads * HEAD_DIM), BF16),
        scratch_shapes=[pltpu.VMEM((PAD + t, HEAD_DIM), BF16), pltpu.VMEM((PAD + t, HEAD_DIM), BF16)],
        compiler_params=_params("arbitrary", "arbitrary"),
    )(proj, proj, proj, proj, *bias, qnw_row, knw_row)


def _attn_bwd(proj, bias, qnw_row, knw_row, dog, *, heads, name, tb=ATTN_BLOCK, sub=4):
    t = proj.shape[0]
    tb = min(tb, t)
    nb, ng = t // tb, tb // ROWS
    sub = min(sub, ng)
    col, full, bias_spec, vec = _attn_specs(heads, tb, t)

    def body(q_ref, k_ref, v_ref, z_ref, first_ref, rest_ref, qnw_ref, knw_ref, dog_ref,
             dq_ref, dk_ref, dv_ref, dz_ref, dbias_ref, dqnw_ref, dknw_ref, kn_scr, v_scr, dkn_scr, dv_scr):
        i = pl.program_id(1)

        @pl.when(i == 0)
        def _():
            _attn_fill(k_ref, v_ref, knw_ref, kn_scr, v_scr, t)
            dkn_scr[...] = jnp.zeros_like(dkn_scr)
            dv_scr[...] = jnp.zeros_like(dv_scr)
            dbias_ref[...] = jnp.zeros_like(dbias_ref)
            dqnw_ref[...] = jnp.zeros_like(dqnw_ref)

        def run(block_bias):
            for g0 in range(0, ng, sub):
                rows = pl.ds(g0 * ROWS, sub * ROWS)
                at = i * tb + g0 * ROWS
                blocked = lambda ref: ref[rows, :].reshape(sub, ROWS, HEAD_DIM)
                dq, dz, dkn, dv, dbias, dqnw = _attn_groups_bwd(
                    blocked(q_ref), blocked(z_ref), _attn_windows(kn_scr, at, sub), _attn_windows(v_scr, at, sub),
                    block_bias(g0), qnw_ref[...], blocked(dog_ref))
                dq_ref[rows, :] = dq.reshape(sub * ROWS, HEAD_DIM).astype(BF16)
                dz_ref[rows, :] = dz.reshape(sub * ROWS, HEAD_DIM).astype(BF16)
                for g in range(sub):
                    window = pl.ds(pl.multiple_of(at + g * ROWS, ROWS), WIN)
                    dkn_scr[window, :] += dkn[g]
                    dv_scr[window, :] += dv[g]
                dbias_ref[0] += dbias
                dqnw_ref[0] += dqnw

        pl.when(i == 0)(lambda: run(lambda g0: first_ref[0, g0:g0 + sub]))
        pl.when(i > 0)(lambda: run(lambda g0: rest_ref[0]))

        @pl.when(i == nb - 1)
        def _():
            step = min(512, t)

            def finish(j, dknw):
                rows = pl.ds(pl.multiple_of(j * step, step), step)
                prows = pl.ds(pl.multiple_of(PAD + j * step, CHUNK), step)
                _, vjp = jax.vjp(_rms, k_ref[rows, :], knw_ref[...])
                dk, dw = vjp(dkn_scr[prows, :])
                dk_ref[rows, :] = dk.astype(BF16)
                dv_ref[rows, :] = dv_scr[prows, :].astype(BF16)
                return dknw + dw

            dknw_ref[0] = lax.fori_loop(0, t // step, finish, jnp.zeros((1, HEAD_DIM), F32))

    out_col = pl.BlockSpec((tb, HEAD_DIM), lambda h, i: (i, h))
    out_full = pl.BlockSpec((t, HEAD_DIM), lambda h, i: (0, h))
    head_vec = pl.BlockSpec((1, 1, HEAD_DIM), lambda h, i: (h, 0, 0))
    col_shape = jax.ShapeDtypeStruct((t, heads * HEAD_DIM), BF16)
    vec_shape = jax.ShapeDtypeStruct((heads, 1, HEAD_DIM), F32)
    return pl.pallas_call(
        body,
        name=name,
        grid=(heads, nb),
        in_specs=[col(0), full(1), full(2), col(3), *bias_spec, vec, vec, pl.BlockSpec((tb, HEAD_DIM), lambda h, i: (i, h))],
        out_specs=[out_col, out_full, out_full, out_col, pl.BlockSpec((1, ROWS, WIN), lambda h, i: (h, 0, 0)), head_vec,
                   head_vec],
        out_shape=[col_shape, col_shape, col_shape, col_shape, jax.ShapeDtypeStruct((heads, ROWS, WIN), F32),
                   vec_shape, vec_shape],
        scratch_shapes=[pltpu.VMEM((PAD + t, HEAD_DIM), BF16), pltpu.VMEM((PAD + t, HEAD_DIM), BF16),
                        pltpu.VMEM((PAD + t, HEAD_DIM), F32), pltpu.VMEM((PAD + t, HEAD_DIM), F32)],
        compiler_params=_params("arbitrary", "arbitrary"),
    )(proj, proj, proj, proj, *bias, qnw_row, knw_row, dog)


def _lane_row(v):
    v = v.reshape(1, -1)
    return jnp.pad(v, ((0, 0), (0, LANES - v.shape[1])))


def _local_step(x, target, norm_w, wa_in, conv_w, a_log, dt_bias, onw, wa_out, wb_in, qnw, knw, rel_bias, wb_out, *,
                sharded=False):
    ha, hb = a_log.shape[-1], rel_bias.shape[-2]
    na = 4 * ha * HEAD_DIM
    wa_main = wa_in[:, :na]
    wa_ab = jnp.pad(wa_in[:, na:], ((0, 0), (0, LANES - 2 * ha)))
    alog_row, dtb_row, onw_row = _lane_row(a_log), _lane_row(dt_bias), _lane_row(onw)
    qnw_row, knw_row = _lane_row(qnw), _lane_row(knw)
    bias = _masked_bias(_band_bias(rel_bias.reshape(hb, -1)), min(ATTN_BLOCK, x.shape[0]) // ROWS)

    hn0 = _rmsnorm_fwd(x, norm_w[0:1], name="norm0")
    proj_a = _matmul(hn0, wa_main, name="a_in")
    ab_a = _matmul(hn0, wa_ab, name="a_in_ab")
    og_a, states, got = _gdn_fwd(proj_a, ab_a, conv_w, alog_row, dtb_row, onw_row, heads=ha, name="gdn_fwd",
                                 gather=[wb_in, wa_out, wb_out] if sharded else [])
    if sharded:
        wb_in, wa_out, wb_out = _join_cols(got[0]), got[1].reshape(-1, got[1].shape[-1]), got[2].reshape(-1, got[2].shape[-1])
    h1 = _matmul(og_a, wa_out, residual=x, name="a_out")
    hn1 = _rmsnorm_fwd(h1, norm_w[1:2], name="norm1")
    proj_b = _matmul(hn1, wb_in, name="b_in")
    og_b = _attn_fwd(proj_b, bias, qnw_row, knw_row, heads=hb, name="attn_fwd")
    h2 = _matmul(og_b, wb_out, residual=h1, name="b_out")
    loss, dh2, dh2_b = _loss_head(h2, target, name="loss_head")

    grad_dtype = BF16 if sharded else F32
    dog_b = _matmul(dh2_b, wb_out, trans_b=True, name="d_b_out_x")
    dwb_out = _matmul(og_b, dh2_b, trans_a=True, out_dtype=grad_dtype, name="d_b_out_w")
    dq, dk, dv, dz, dbias, dqnw, dknw = _attn_bwd(proj_b, bias, qnw_row, knw_row, dog_b, heads=hb, name="attn_bwd")
    dproj_b = jnp.concatenate([dq, dk, dv, dz], axis=1)
    dhn1 = _matmul(dproj_b, wb_in, trans_b=True, name="d_b_in_x")
    dwb_in = _matmul(hn1, dproj_b, trans_a=True, out_dtype=grad_dtype, name="d_b_in_w")
    dh1, dh1_b, dnw1 = _rmsnorm_bwd(h1, norm_w[1:2], dhn1, dh2, name="d_norm1")

    dog_a = _matmul(dh1_b, wa_out, trans_b=True, name="d_a_out_x")
    dwa_out = _matmul(og_a, dh1_b, trans_a=True, out_dtype=grad_dtype, name="d_a_out_w")
    early = [_split_cols(dwb_in), _split_rows(dwa_out), _split_rows(dwb_out)] if sharded else []
    dq, dk, dv, dz, dab, dconv, dalog, ddtb, donw, landed = _gdn_bwd(
        proj_a, ab_a, conv_w, alog_row, dtb_row, onw_row, states, dog_a, heads=ha, name="gdn_bwd", exchange=early)
    if sharded:
        dwb_in, dwa_out, dwb_out = landed
    dproj_a = jnp.concatenate([dq, dk, dv, dz], axis=1)
    dab_b = dab.astype(BF16)
    dwa_in = jnp.concatenate(
        [_matmul(hn0, dproj_a, trans_a=True, out_dtype=grad_dtype, name="d_a_in_w"),
         _matmul(hn0, dab_b, trans_a=True, out_dtype=grad_dtype, name="d_a_in_ab_w")[:, :2 * ha]], axis=1)
    if sharded:
        dhn0, (dwa_in, dconv) = _matmul(dproj_a, wa_main, trans_b=True, name="d_a_in_x",
                                        exchange=[_split_cols(dwa_in), _split_cols(dconv)])
    else:
        dhn0 = _matmul(dproj_a, wa_main, trans_b=True, name="d_a_in_x")
    dhn0 = _matmul(dab_b, wa_ab, trans_b=True, residual=dhn0, name="d_a_in_ab_x")
    dx, _, dnw0 = _rmsnorm_bwd(x, norm_w[0:1], dhn0, dh1, name="d_norm0")

    drel = _band_bias_grad(dbias)
    grads = dict(
        norm_w=jnp.concatenate([dnw0, dnw1], axis=0), a_w_in=dwa_in, a_conv_w=dconv, a_a_log=dalog[:, :ha],
        a_dt_bias=ddtb[:, :ha], a_out_norm_w=donw, a_w_out=dwa_out, b_w_in=dwb_in, b_q_norm_w=jnp.sum(dqnw, axis=0),
        b_k_norm_w=jnp.sum(dknw, axis=0), b_rel_bias=drel[None], b_w_out=dwb_out)
    return loss, dx, grads


_ANY = pl.BlockSpec(memory_space=pl.ANY)
_CHIP_FLIPS = ((1, 0), (0, 1), (1, 1))


def _place():
    x, y, c = lax.axis_index("x"), lax.axis_index("y"), lax.axis_index("c")
    return x, y, c


def _flip(v, bit):
    return 1 - v if bit else v


def _remote(src, dst, send_sem, recv_sem, peer):
    return pltpu.make_async_remote_copy(src_ref=src, dst_ref=dst, send_sem=send_sem, recv_sem=recv_sem, device_id=peer,
                                        device_id_type=MESH)


def _comm_call(body, arrays, out_shapes, n_remote, n_local, name):
    scratch = [pltpu.SemaphoreType.DMA((n_remote,)), pltpu.SemaphoreType.DMA((n_remote,))]
    if n_local:
        scratch.append(pltpu.SemaphoreType.DMA((n_local,)))
    return pl.pallas_call(
        body, name=name, in_specs=[_ANY] * len(arrays), out_specs=[_ANY] * len(out_shapes), out_shape=out_shapes,
        scratch_shapes=scratch)(*arrays)


def _chip_scratch(n):
    return [pltpu.SemaphoreType.DMA((3 * n,)), pltpu.SemaphoreType.DMA((3 * n,)), pltpu.SemaphoreType.DMA((n,))]


def _chip_shapes(gather, arrays):
    return [jax.ShapeDtypeStruct(((N_CHIPS,) + s.shape) if gather else s.shape, s.dtype) for s in arrays]


def _chip_traffic(gather, ins, outs, sems):
    send_sems, recv_sems, local_sems = sems
    x, y, c = _place()
    mine = 2 * x + y
    local, remote, landing = [], [], []
    for a in range(len(ins)):
        local.append(pltpu.make_async_copy(ins[a] if gather else ins[a].at[mine], outs[a].at[mine], local_sems.at[a]))
        for k, (fx, fy) in enumerate(_CHIP_FLIPS):
            peer = (_flip(x, fx), _flip(y, fy), c)
            theirs = 2 * peer[0] + peer[1]
            src = ins[a] if gather else ins[a].at[theirs]
            pair = send_sems.at[3 * a + k], recv_sems.at[3 * a + k]
            remote.append(_remote(src, outs[a].at[mine], *pair, peer))
            landing.append(_remote(src, outs[a].at[theirs], *pair, peer))
    return local + remote, (local, landing, remote)


def _start(traffic):
    for cp in traffic[0]:
        cp.start()


def _finish(traffic):
    local, landing, remote = traffic[1]
    for cp in local:
        cp.wait()
    for cp in landing:
        cp.wait_recv()
    for cp in remote:
        cp.wait_send()


def _with_exchange(compute, n_in, n_out, gather, n_x, grid):
    if not n_x:
        return compute

    def body(*refs):
        ins, x_in = refs[:n_in], refs[n_in:n_in + n_x]
        outs, x_out = refs[n_in + n_x:n_in + n_x + n_out], refs[n_in + n_x + n_out:n_in + 2 * n_x + n_out]
        scratch, sems = refs[n_in + 2 * n_x + n_out:-3], refs[-3:]
        traffic = _chip_traffic(gather, x_in, x_out, sems)
        first = functools.reduce(jnp.logical_and, [pl.program_id(d) == 0 for d in range(len(grid))])
        last = functools.reduce(jnp.logical_and, [pl.program_id(d) == grid[d] - 1 for d in range(len(grid))])

        @pl.when(first)
        def _():
            _start(traffic)

        compute(*ins, *outs, *scratch)

        @pl.when(last)
        def _():
            _finish(traffic)

    return body


def _chip_call(gather, arrays, *, name):
    n = len(arrays)

    def body(*refs):
        traffic = _chip_traffic(gather, refs[:n], refs[n:2 * n], refs[2 * n:])
        _start(traffic)
        _finish(traffic)

    return pl.pallas_call(
        body, name=name, in_specs=[_ANY] * n, out_specs=[_ANY] * n, out_shape=_chip_shapes(gather, arrays),
        scratch_shapes=_chip_scratch(n))(*arrays)


def _gather_shared(shard, small, *, name):
    rows = shard.shape[0]
    assert rows % 2 == 0
    half = rows // 2

    def body(shard_ref, small_ref, out_ref, small_out_ref, send_sems, recv_sems, local_sems):
        x, y, c = _place()
        mine = 2 * x + y
        sibling = (x, y, 1 - c)
        my_rows = pl.ds(pl.multiple_of(c * half, 8), half)
        local = [pltpu.make_async_copy(shard_ref, out_ref.at[mine], local_sems.at[0]),
                 pltpu.make_async_copy(small_ref, small_out_ref.at[mine], local_sems.at[1])]
        sent, landed, passed_on, handed = [], [], [], []
        for k, (fx, fy) in enumerate(_CHIP_FLIPS):
            peer = (_flip(x, fx), _flip(y, fy), c)
            theirs = 2 * peer[0] + peer[1]
            ici, d2d, tiny = [(send_sems.at[3 * n + k], recv_sems.at[3 * n + k]) for n in range(3)]
            sent.append(_remote(shard_ref.at[my_rows], out_ref.at[mine, my_rows], *ici, peer))
            landed.append(_remote(shard_ref.at[my_rows], out_ref.at[theirs, my_rows], *ici, peer))
            sent.append(_remote(small_ref, small_out_ref.at[mine], *tiny, peer))
            landed.append(_remote(small_ref, small_out_ref.at[theirs], *tiny, peer))
            passed_on.append(_remote(out_ref.at[theirs, my_rows], out_ref.at[theirs, my_rows], *d2d, sibling))
            other_rows = pl.ds(pl.multiple_of((1 - c) * half, 8), half)
            handed.append(_remote(out_ref.at[theirs, other_rows], out_ref.at[theirs, other_rows], *d2d, sibling))
        for cp in local + sent:
            cp.start()
        for k in range(3):
            landed[2 * k].wait_recv()
            passed_on[k].start()
        for k in range(3):
            landed[2 * k + 1].wait_recv()
            handed[k].wait_recv()
        for cp in local:
            cp.wait()
        for cp in sent + passed_on:
            cp.wait_send()

    return pl.pallas_call(
        body, name=name, in_specs=[_ANY] * 2, out_specs=[_ANY] * 2, out_shape=_chip_shapes(True, [shard, small]),
        scratch_shapes=[pltpu.SemaphoreType.DMA((9,)), pltpu.SemaphoreType.DMA((9,)), pltpu.SemaphoreType.DMA((2,))],
    )(shard, small)


def _swap_pair(arrays, *, name):
    n = len(arrays)

    def body(*refs):
        ins, outs, (send_sems, recv_sems) = refs[:n], refs[n:2 * n], refs[2 * n:]
        x, y, c = _place()
        copies = [_remote(ins[a], outs[a], send_sems.at[a], recv_sems.at[a], (x, y, 1 - c)) for a in range(n)]
        for cp in copies:
            cp.start()
        for cp in copies:
            cp.wait_recv()
        for cp in copies:
            cp.wait_send()

    shapes = [jax.ShapeDtypeStruct(s.shape, s.dtype) for s in arrays]
    return _comm_call(body, arrays, shapes, n, 0, name)


def _gather_all(tile, *, name):
    def body(in_ref, out_ref, send_sems, recv_sems, local_sems):
        x, y, c = _place()
        mine = 4 * x + 2 * y + c
        local = pltpu.make_async_copy(in_ref, out_ref.at[mine], local_sems.at[0])
        remote, landing = [], []
        for k in range(1, N_DEV):
            peer = (_flip(x, k & 4), _flip(y, k & 2), _flip(c, k & 1))
            sems = send_sems.at[k - 1], recv_sems.at[k - 1]
            remote.append(_remote(in_ref, out_ref.at[mine], *sems, peer))
            landing.append(_remote(in_ref, out_ref.at[4 * peer[0] + 2 * peer[1] + peer[2]], *sems, peer))
        for cp in [local] + remote:
            cp.start()
        local.wait()
        for cp in landing:
            cp.wait_recv()
        for cp in remote:
            cp.wait_send()

    return _comm_call(body, [tile], [jax.ShapeDtypeStruct((N_DEV,) + tile.shape, tile.dtype)], N_DEV - 1, 1, name)[0]


def _sum_slots(slabs, *, name, tr=128):
    s, r, c = slabs.shape
    tr = min(tr, r)

    def body(in_ref, o_ref):
        acc = in_ref[0].astype(F32)
        for j in range(1, s):
            acc = acc + in_ref[j].astype(F32)
        o_ref[...] = acc

    return pl.pallas_call(
        body, name=name, grid=(r // tr,),
        in_specs=[pl.BlockSpec((s, tr, c), lambda i: (0, i, 0))], out_specs=pl.BlockSpec((tr, c), lambda i: (i, 0)),
        out_shape=jax.ShapeDtypeStruct((r, c), F32), compiler_params=_params("parallel"))(slabs)


def _adamw_math(w, g, m, v):
    m = ADAM_B1 * m + (1.0 - ADAM_B1) * g
    v = ADAM_B2 * v + (1.0 - ADAM_B2) * (g * g)
    m_hat = m / (1.0 - ADAM_B1 ** ADAM_STEP)
    v_hat = v / (1.0 - ADAM_B2 ** ADAM_STEP)
    delta = -ADAM_LR * (m_hat / (jnp.sqrt(v_hat) + ADAM_EPS) + ADAM_WD * w)
    return delta, m, v


def _adamw(w, m, v, parts, *, name, tr=128):
    r, c = w.shape
    tr = min(tr, r)
    s = len(parts)

    def body(w_ref, m_ref, v_ref, *refs):
        g_ref, d_ref, nm_ref, nv_ref = refs[s:]
        g = refs[0][...]
        for p_ref in refs[1:s]:
            g = g + p_ref[...]
        g_ref[...] = g
        d_ref[...], nm_ref[...], nv_ref[...] = _adamw_math(w_ref[...], g, m_ref[...], v_ref[...])

    blk = pl.BlockSpec((tr, c), lambda i: (i, 0))
    shape = jax.ShapeDtypeStruct((r, c), F32)
    return pl.pallas_call(
        body, name=name, grid=(r // tr,), in_specs=[blk] * (3 + s), out_specs=[blk] * 4, out_shape=[shape] * 4,
        compiler_params=_params("parallel"))(w, m, v, *parts)


_BIG = ("a_w_in", "b_w_in", "a_w_out", "b_w_out", "a_conv_w")
_SMALL = ("norm_w", "a_a_log", "a_dt_bias", "a_out_norm_w", "b_q_norm_w", "b_k_norm_w", "b_rel_bias")
_ORDER = ("norm_w", "a_w_in", "a_conv_w", "a_a_log", "a_dt_bias", "a_out_norm_w", "a_w_out", "b_w_in", "b_q_norm_w",
          "b_k_norm_w", "b_rel_bias", "b_w_out")


def _join_cols(g):
    return jnp.transpose(g, (1, 0, 2)).reshape(g.shape[1], -1)


def _split_cols(g):
    return jnp.transpose(g.reshape(g.shape[0], N_CHIPS, -1), (1, 0, 2))


def _split_rows(g):
    return g.reshape(N_CHIPS, -1, g.shape[-1])


def _pack(d):
    flat = jnp.concatenate([d[n].reshape(-1) for n in _SMALL])
    return jnp.pad(flat, (0, -flat.shape[0] % LANES)).reshape(1, -1)


def _unpack(row, like):
    out, at = {}, 0
    for n in _SMALL:
        size = like[n].size
        out[n] = row[0, at:at + size].reshape(like[n].shape)
        at += size
    return out


def kernel(x, norm_w, a_w_in, a_conv_w, a_a_log, a_dt_bias, a_out_norm_w, a_w_out, b_w_in, b_q_norm_w, b_k_norm_w, b_rel_bias, b_w_out, loss_target, m_norm_w, m_a_w_in, m_a_conv_w, m_a_a_log, m_a_dt_bias, m_a_out_norm_w, m_a_w_out, m_b_w_in, m_b_q_norm_w, m_b_k_norm_w, m_b_rel_bias, m_b_w_out, v_norm_w, v_a_w_in, v_a_conv_w, v_a_a_log, v_a_dt_bias, v_a_out_norm_w, v_a_w_out, v_b_w_in, v_b_q_norm_w, v_b_k_norm_w, v_b_rel_bias, v_b_w_out):
    w = dict(norm_w=norm_w, a_w_in=a_w_in, a_conv_w=a_conv_w, a_a_log=a_a_log, a_dt_bias=a_dt_bias,
             a_out_norm_w=a_out_norm_w, a_w_out=a_w_out, b_w_in=b_w_in, b_q_norm_w=b_q_norm_w, b_k_norm_w=b_k_norm_w,
             b_rel_bias=b_rel_bias, b_w_out=b_w_out)
    m = dict(norm_w=m_norm_w, a_w_in=m_a_w_in, a_conv_w=m_a_conv_w, a_a_log=m_a_a_log, a_dt_bias=m_a_dt_bias,
             a_out_norm_w=m_a_out_norm_w, a_w_out=m_a_w_out, b_w_in=m_b_w_in, b_q_norm_w=m_b_q_norm_w,
             b_k_norm_w=m_b_k_norm_w, b_rel_bias=m_b_rel_bias, b_w_out=m_b_w_out)
    v = dict(norm_w=v_norm_w, a_w_in=v_a_w_in, a_conv_w=v_a_conv_w, a_a_log=v_a_a_log, a_dt_bias=v_a_dt_bias,
             a_out_norm_w=v_a_out_norm_w, a_w_out=v_a_w_out, b_w_in=v_b_w_in, b_q_norm_w=v_b_q_norm_w,
             b_k_norm_w=v_b_k_norm_w, b_rel_bias=v_b_rel_bias, b_w_out=v_b_w_out)

    wa_in, conv = _gather_shared(a_w_in[0].astype(BF16), a_conv_w[0], name="gather_a_in")
    loss, dx, grads = _local_step(
        x[0], loss_target[0], norm_w, _join_cols(wa_in), _join_cols(conv), a_a_log, a_dt_bias, a_out_norm_w,
        a_w_out[0].astype(BF16), b_w_in[0].astype(BF16), b_q_norm_w, b_k_norm_w, b_rel_bias, b_w_out[0].astype(BF16),
        sharded=True)
    loss = lax.psum(loss, ("x", "y", "c"))

    mine = [_sum_slots(grads[n], name=f"chip_sum_{n}") for n in _BIG]
    theirs = _swap_pair(mine, name="pair_grads")
    out = {}
    for n, p, q in zip(_BIG, mine, theirs):
        out[n] = [r[None] for r in _adamw(w[n][0], m[n][0], v[n][0], [p, q], name=f"adamw_{n}")]

    row = _pack(grads)
    tiles = _gather_all(jnp.broadcast_to(row, (8, row.shape[1])), name="gather_small_grads")
    res = _adamw(_pack(w), _pack(m), _pack(v), [tiles[d, 0:1, :] for d in range(N_DEV)], name="adamw_small")
    unpacked = [_unpack(r, w) for r in res]
    for n in _SMALL:
        out[n] = [u[n] for u in unpacked]

    return (loss, dx[None], *[out[n][0] for n in _ORDER], *[out[n][1] for n in _ORDER], *[out[n][2] for n in _ORDER],
            *[out[n][3] for n in _ORDER])
```

```python
import functools

import numpy as np
import jax
import jax.numpy as jnp
from jax import lax
from jax.experimental import pallas as pl
from jax.experimental.pallas import tpu as pltpu

F32 = jnp.float32
BF16 = jnp.bfloat16

CHUNK = 64
HEAD_DIM = 128
LEFT_CHUNKS = 8
REL_CLIP = 256
CONV_K = 4
EPS = 1e-6
HALO = 8

ADAM_LR = 0.001
ADAM_B1 = 0.9
ADAM_B2 = 0.999
ADAM_EPS = 1e-08
ADAM_WD = 0.01
ADAM_STEP = 10

LANES = 128
N_CHIPS = 4
N_DEV = 8
VMEM_LIMIT_BYTES = 56 * 1024 * 1024
VMEM_LIMIT_WIDE_BYTES = 63 * 1024 * 1024
MESH = pl.DeviceIdType.MESH
HIGHEST = lax.Precision.HIGHEST


def _params(*sem, vmem=VMEM_LIMIT_BYTES):
    return pltpu.CompilerParams(dimension_semantics=sem, vmem_limit_bytes=vmem)


def _dot(a, b, dims=(((1,), (0,)), ((), ())), precision=None):
    return lax.dot_general(a, b, dims, precision=precision, preferred_element_type=F32)


_NT = (((1,), (1,)), ((), ()))
_TN = (((0,), (0,)), ((), ()))


def _bdot(a, b, dims=(((1,), (0,)), ((), ()))):
    return _dot(a.astype(BF16), b.astype(BF16), dims)


def _fdot(a, b, dims=(((1,), (0,)), ((), ()))):
    return _dot(a, b, dims, precision=lax.Precision.HIGH)


def _silu(x):
    return x * jax.nn.sigmoid(x)


def _matmul(a, b, *, name, trans_a=False, trans_b=False, residual=None, out_dtype=F32, tm=1024, tn=1024, tk=2048,
            col_slabs=0, exchange=()):
    assert not (trans_a and trans_b)
    k, m = a.shape if trans_a else a.shape[::-1]
    n = b.shape[0] if trans_b else b.shape[1]
    tm, tn, tk = min(tm, m), min(tn, n // max(col_slabs, 1)), min(tk, k)
    assert m % tm == 0 and n % tn == 0 and k % tk == 0, (a.shape, b.shape, tm, tn, tk)
    nk = k // tk
    dims = _NT if trans_b else _TN if trans_a else (((1,), (0,)), ((), ()))

    def body(*refs):
        if residual is None:
            a_ref, b_ref, o_ref, acc_ref = refs
            r_ref = None
        else:
            a_ref, b_ref, r_ref, o_ref, acc_ref = refs
        kk = pl.program_id(2)

        @pl.when(kk == 0)
        def _():
            acc_ref[...] = jnp.zeros_like(acc_ref)

        acc_ref[...] += _dot(a_ref[...], b_ref[...], dims)

        @pl.when(kk == nk - 1)
        def _():
            r = acc_ref[...]
            if r_ref is not None:
                r = r + r_ref[...]
            o_ref[...] = r.astype(o_ref.dtype)

    in_specs = [
        pl.BlockSpec((tk, tm), lambda i, j, kk: (kk, i)) if trans_a else pl.BlockSpec((tm, tk), lambda i, j, kk: (i, kk)),
        pl.BlockSpec((tn, tk), lambda i, j, kk: (j, kk)) if trans_b else pl.BlockSpec((tk, tn), lambda i, j, kk: (kk, j)),
    ]
    args = [a, b]
    if residual is not None:
        in_specs.append(pl.BlockSpec((tm, tn), lambda i, j, kk: (i, j)))
        args.append(residual)
    grid = (m // tm, n // tn, nk)
    n_x = len(exchange)
    if col_slabs:
        per = n // col_slabs // tn
        assert per * tn * col_slabs == n, (n, tn, col_slabs)
        out_spec = pl.BlockSpec((None, tm, tn), lambda i, j, kk: (j // per, i, j % per))
        out_shape = jax.ShapeDtypeStruct((col_slabs, m, n // col_slabs), out_dtype)
    else:
        out_spec = pl.BlockSpec((tm, tn), lambda i, j, kk: (i, j))
        out_shape = jax.ShapeDtypeStruct((m, n), out_dtype)
    out, *landed = pl.pallas_call(
        _with_exchange(body, len(args), 1, False, n_x, grid),
        name=name,
        grid=grid,
        in_specs=in_specs + [_ANY] * n_x,
        out_specs=[out_spec] + [_ANY] * n_x,
        out_shape=[out_shape] + _chip_shapes(False, exchange),
        scratch_shapes=[pltpu.VMEM((tm, tn), F32)] + (_chip_scratch(n_x) if n_x else []),
        compiler_params=_params(*(("arbitrary",) * 3 if n_x else ("parallel", "parallel", "arbitrary"))),
    )(*args, *exchange)
    return (out, landed) if n_x else out


def _rms(x, w):
    return x * lax.rsqrt(jnp.mean(x * x, axis=-1, keepdims=True) + EPS) * w


def _rmsnorm_fwd(x, w_row, *, name, tr=512):
    t, d = x.shape
    tr = min(tr, t)

    def body(x_ref, w_ref, o_ref):
        o_ref[...] = _rms(x_ref[...], w_ref[...]).astype(BF16)

    return pl.pallas_call(
        body,
        name=name,
        grid=(t // tr,),
        in_specs=[pl.BlockSpec((tr, d), lambda i: (i, 0)), pl.BlockSpec((1, d), lambda i: (0, 0))],
        out_specs=pl.BlockSpec((tr, d), lambda i: (i, 0)),
        out_shape=jax.ShapeDtypeStruct((t, d), BF16),
        compiler_params=_params("parallel"),
    )(x, w_row)


def _rmsnorm_bwd(x, w_row, dy, dres, *, name, tr=256):
    t, d = x.shape
    tr = min(tr, t)

    def body(x_ref, w_ref, dy_ref, dres_ref, dx_ref, dxb_ref, dw_ref):
        @pl.when(pl.program_id(0) == 0)
        def _():
            dw_ref[...] = jnp.zeros_like(dw_ref)

        _, vjp = jax.vjp(_rms, x_ref[...], w_ref[...])
        dx, dw = vjp(dy_ref[...])
        dx = dx + dres_ref[...]
        dx_ref[...] = dx
        dxb_ref[...] = dx.astype(BF16)
        dw_ref[...] += dw

    row = pl.BlockSpec((tr, d), lambda i: (i, 0))
    vec = pl.BlockSpec((1, d), lambda i: (0, 0))
    return pl.pallas_call(
        body,
        name=name,
        grid=(t // tr,),
        in_specs=[row, vec, row, row],
        out_specs=[row, row, vec],
        out_shape=[jax.ShapeDtypeStruct((t, d), F32), jax.ShapeDtypeStruct((t, d), BF16), jax.ShapeDtypeStruct((1, d), F32)],
        compiler_params=_params("arbitrary"),
    )(x, w_row, dy, dres)


def _loss_head(h, target, *, name, tr=512):
    t, d = h.shape
    tr = min(tr, t)

    def body(h_ref, t_ref, dh_ref, dhb_ref, part_ref):
        @pl.when(pl.program_id(0) == 0)
        def _():
            part_ref[...] = jnp.zeros_like(part_ref)

        err = h_ref[...] - t_ref[...]
        dh = err * (1.0 / d)
        dh_ref[...] = dh
        dhb_ref[...] = dh.astype(BF16)
        part_ref[...] += jnp.sum(err * err, axis=0, keepdims=True)

    row = pl.BlockSpec((tr, d), lambda i: (i, 0))
    vec = pl.BlockSpec((1, d), lambda i: (0, 0))
    dh, dhb, part = pl.pallas_call(
        body,
        name=name,
        grid=(t // tr,),
        in_specs=[row, row],
        out_specs=[row, row, vec],
        out_shape=[jax.ShapeDtypeStruct((t, d), F32), jax.ShapeDtypeStruct((t, d), BF16), jax.ShapeDtypeStruct((1, d), F32)],
        compiler_params=_params("arbitrary"),
    )(h, target)
    return 0.5 / d * jnp.sum(part), dh, dhb


_BNN = (((2,), (1,)), ((0,), (0,)))
_BNT = (((2,), (2,)), ((0,), (0,)))
_BTN = (((1,), (1,)), ((0,), (0,)))


_TAP0 = HALO - (CONV_K - 1)


def _conv(x_ref, w, rows):
    c = w[0:1, :] * x_ref[_TAP0:_TAP0 + rows, :]
    for j in range(1, CONV_K):
        c = c + w[j:j + 1, :] * x_ref[_TAP0 + j:_TAP0 + j + rows, :]
    return c


def _conv_silu_bwd(x_ref, w, dact, dc_ref, rows):
    c = _conv(x_ref, w, rows)
    sig = jax.nn.sigmoid(c)
    dc = dact * (sig * (1.0 + c * (1.0 - sig)))
    dw = [jnp.sum(dc * x_ref[_TAP0 + j:_TAP0 + j + rows, :], axis=0, keepdims=True) for j in range(CONV_K)]
    dc_ref[0:HALO, :] = jnp.zeros((HALO, HEAD_DIM), F32)
    dc_ref[HALO:HALO + rows, :] = dc
    dc_ref[HALO + rows:HALO + rows + HALO, :] = jnp.zeros((HALO, HEAD_DIM), F32)
    first = HALO - _TAP0
    dx = w[0:1, :] * dc_ref[first:first + HALO + rows, :]
    for j in range(1, CONV_K):
        dx = dx + w[j:j + 1, :] * dc_ref[first - j:first - j + HALO + rows, :]
    return dx, dw


@jax.custom_vjp
def _unit_lower_inverse(neg_l):
    n = neg_l.shape[0]
    eye = (lax.broadcasted_iota(jnp.int32, (n, CHUNK, CHUNK), 1) == lax.broadcasted_iota(jnp.int32, (n, CHUNK, CHUNK), 2))
    inv = eye.astype(F32) + neg_l
    power = _bdot(neg_l, neg_l, _BNN)
    for _ in range(4):
        both = _bdot(jnp.concatenate([inv, power], axis=1), power, _BNN)
        inv, power = inv + both[:, :CHUNK], both[:, CHUNK:]
    return inv + _bdot(inv, power, _BNN)


def _unit_lower_inverse_fwd(neg_l):
    inv = _unit_lower_inverse(neg_l)
    return inv, inv


def _unit_lower_inverse_bwd(inv, dinv):
    return (_fdot(_fdot(inv, dinv, _BTN), inv, _BNT),)


_unit_lower_inverse.defvjp(_unit_lower_inverse_fwd, _unit_lower_inverse_bwd)


def _gdn_intra(qt, kt, v, a, b, alog, dtb):
    n = a.shape[0] // CHUNK
    q = qt * lax.rsqrt(jnp.sum(qt * qt, axis=-1, keepdims=True) + EPS) * (HEAD_DIM ** -0.5)
    k = kt * lax.rsqrt(jnp.sum(kt * kt, axis=-1, keepdims=True) + EPS)
    lanes = jnp.ones((1, HEAD_DIM), F32)
    beta = jax.nn.sigmoid(b) * lanes
    sp = a + dtb
    g = (-jnp.exp(alog) * (jnp.maximum(sp, 0.0) + jnp.log(1.0 + jnp.exp(-jnp.abs(sp))))) * lanes
    q, k, v, beta, g = (t.reshape(n, CHUNK, HEAD_DIM) for t in (q, k, v, beta, g))

    row = lax.broadcasted_iota(jnp.int32, (n, CHUNK, CHUNK), 1)
    col = lax.broadcasted_iota(jnp.int32, (n, CHUNK, CHUNK), 2)
    tri_incl = row >= col
    tri_strict = row > col
    gc = _fdot(tri_incl.astype(F32), g, _BNN)
    gc_row = _fdot(g[:, :, :CHUNK], (row <= col).astype(F32), _BTN)
    decay = jnp.exp(jnp.where(tri_incl, gc[:, :, :CHUNK] - gc_row, -1e30))
    kb = k * beta
    vb = v * beta
    with_k = _bdot(jnp.concatenate([kb, q], axis=1), k, _BNT)
    neg_l = jnp.where(tri_strict, -(with_k[:, :CHUNK] * decay), 0.0)
    qk = jnp.where(tri_incl, with_k[:, CHUNK:] * decay, 0.0)
    inv = _unit_lower_inverse(neg_l)
    e = jnp.exp(gc)
    solved = _bdot(inv, jnp.concatenate([kb * e, vb], axis=2), _BNN)
    g_last = gc[:, CHUNK - 1:CHUNK, :]
    k_dec = k * jnp.exp(g_last - gc)
    from_k = _bdot(k_dec, solved, _BTN)
    from_qk = _bdot(qk, solved, _BNN)
    step, add = -from_k[:, :, :HEAD_DIM], from_k[:, :, HEAD_DIM:]
    read, out = q * e - from_qk[:, :, :HEAD_DIM], from_qk[:, :, HEAD_DIM:]
    return step, add, jnp.exp(g_last), read, out


def _gdn_scan_step(state, step, add, decay_last):
    return state * decay_last + _bdot(step, state) + add


def _gdn_outputs(states, read, out, z, onw):
    return _rms(_bdot(read, states, _BNN) + out, onw) * _silu(z)


def _scan_scratch(n, dtype):
    return [pltpu.VMEM((n, HEAD_DIM, HEAD_DIM), dtype), pltpu.VMEM((n, HEAD_DIM, HEAD_DIM), F32), pltpu.VMEM((n, 1, HEAD_DIM), F32)]


def _head_lane(h, offset=0):
    return lax.broadcasted_iota(jnp.int32, (1, LANES), 1) == h + offset


def _pick(mask, x):
    return jnp.sum(jnp.where(mask, x, 0.0), axis=1, keepdims=True)


def _gdn_specs(heads, tb, rev, nb, PAIR):
    assert heads % PAIR == 0
    blk = (lambda i: nb - 1 - i) if rev else (lambda i: i)
    hb = tb // HALO
    width, pairs = PAIR * HEAD_DIM, heads // PAIR

    def col(group):
        return pl.BlockSpec((tb, width), lambda i, h: (blk(i), group * pairs + h))

    def halo(group):
        return pl.BlockSpec((HALO, width), lambda i, h: (jnp.maximum(blk(i) * hb - 1, 0), group * pairs + h))

    def convw(group):
        return pl.BlockSpec((CONV_K, width), lambda i, h: (0, group * pairs + h))

    vec = pl.BlockSpec((1, LANES), lambda i, h: (0, 0))
    ab = pl.BlockSpec((tb, LANES), lambda i, h: (blk(i), 0))
    states = pl.BlockSpec((PAIR, tb // CHUNK, HEAD_DIM, HEAD_DIM), lambda i, h: (h, blk(i), 0, 0))
    return blk, col, halo, convw, vec, ab, states


def _head_cols(p):
    return slice(p * HEAD_DIM, (p + 1) * HEAD_DIM)


def _gdn_fwd(proj, ab, conv_w, alog_row, dtb_row, onw_row, *, heads, name, tb=1024, pair=2, gather=()):
    t = proj.shape[0]
    tb = min(tb, t)
    nb, cpb = t // tb, tb // CHUNK
    PAIR = pair
    _, col, halo, convw, vec, abspec, states = _gdn_specs(heads, tb, False, nb, PAIR)

    def body(q_ref, k_ref, v_ref, qh_ref, kh_ref, vh_ref, z_ref, ab_ref, wq_ref, wk_ref, wv_ref, alog_ref, dtb_ref, onw_ref,
             og_ref, st_ref, state_scr, x_scr, *op_scr):
        i, pair = pl.program_id(0), pl.program_id(1)
        abv = ab_ref[...]
        heads_here, later = [pair * PAIR + p for p in range(PAIR)], []
        for p, h in enumerate(heads_here):
            cols = _head_cols(p)
            for n, (ref, href) in enumerate(((q_ref, qh_ref), (k_ref, kh_ref), (v_ref, vh_ref))):
                x_scr[p, n, 0:HALO, :] = jnp.where(i > 0, href[:, cols], 0.0)
                x_scr[p, n, HALO:HALO + tb, :] = ref[:, cols]
            sel_a, sel_b = _head_lane(h), _head_lane(h, heads)
            alog, dtb = _pick(sel_a, alog_ref[...]), _pick(sel_a, dtb_ref[...])
            acts = [_silu(_conv(x_scr.at[p, n], w_ref[:, cols], tb)) for n, w_ref in enumerate((wq_ref, wk_ref, wv_ref))]
            *scan, read, out = _gdn_intra(*acts, _pick(sel_a, abv), _pick(sel_b, abv), alog, dtb)
            for scr, val in zip(op_scr[3 * p:3 * p + 3], scan):
                scr[...] = val.astype(scr.dtype)
            later.append((read, out))

        def chunk(c, states):
            for p in range(PAIR):
                st_ref[p, c] = states[p]
            return tuple(_gdn_scan_step(states[p], *[scr[c] for scr in op_scr[3 * p:3 * p + 3]]) for p in range(PAIR))

        @pl.when(i == 0)
        def _():
            for h in heads_here:
                state_scr[h] = jnp.zeros((HEAD_DIM, HEAD_DIM), F32)

        last = lax.fori_loop(0, cpb, chunk, tuple(state_scr[h] for h in heads_here))
        for p, h in enumerate(heads_here):
            cols = _head_cols(p)
            state_scr[h] = last[p]
            og = _gdn_outputs(st_ref[p], *later[p], z_ref[:, cols].reshape(cpb, CHUNK, HEAD_DIM), onw_ref[...])
            og_ref[:, cols] = og.reshape(tb, HEAD_DIM).astype(BF16)

    n_x = len(gather)
    grid = (nb, heads // PAIR)
    og, st, *gathered = pl.pallas_call(
        _with_exchange(body, 14, 2, True, n_x, grid),
        name=name,
        grid=grid,
        in_specs=[col(0), col(1), col(2), halo(0), halo(1), halo(2), col(3), abspec, convw(0), convw(1), convw(2), vec, vec, vec]
        + [_ANY] * n_x,
        out_specs=[pl.BlockSpec((tb, PAIR * HEAD_DIM), lambda i, h: (i, h)), states] + [_ANY] * n_x,
        out_shape=[jax.ShapeDtypeStruct((t, heads * HEAD_DIM), BF16),
                   jax.ShapeDtypeStruct((heads, t // CHUNK, HEAD_DIM, HEAD_DIM), F32)] + _chip_shapes(True, gather),
        scratch_shapes=[pltpu.VMEM((heads, HEAD_DIM, HEAD_DIM), F32), pltpu.VMEM((PAIR, 3, HALO + tb, HEAD_DIM), F32)]
        + _scan_scratch(cpb, BF16) * PAIR + (_chip_scratch(n_x) if n_x else []),
        compiler_params=_params("arbitrary", "arbitrary"),
    )(proj, proj, proj, proj, proj, proj, proj, ab, conv_w, conv_w, conv_w, alog_row, dtb_row, onw_row, *gather)
    return og, st, gathered


def _gdn_bwd(proj, ab, conv_w, alog_row, dtb_row, onw_row, states, dog, *, heads, name, tb=1024, pair=2, exchange=()):
    t = proj.shape[0]
    tb = min(tb, t)
    nb, cpb = t // tb, tb // CHUNK
    PAIR = pair
    _, col, halo, convw, vec, abspec, states_spec = _gdn_specs(heads, tb, True, nb, PAIR)
    n_conv = conv_w.shape[1]

    def body(q_ref, k_ref, v_ref, qh_ref, kh_ref, vh_ref, z_ref, ab_ref, wq_ref, wk_ref, wv_ref, alog_ref, dtb_ref, onw_ref,
             st_ref, dog_ref, dq_ref, dk_ref, dv_ref, dz_ref, dab_ref, dconv_ref, dalog_ref, ddtb_ref, donw_ref,
             dstate_scr, x_scr, carry_scr, *scr):
        op_scr, dop_scr, dstates_scr, dc_scr = scr[:3 * PAIR], scr[3 * PAIR:6 * PAIR], scr[6 * PAIR:7 * PAIR], scr[7 * PAIR]
        i, pair = pl.program_id(0), pl.program_id(1)
        first_block = i == nb - 1
        heads_here, later = [pair * PAIR + p for p in range(PAIR)], []

        @pl.when(jnp.logical_and(i == 0, pair == 0))
        def _():
            dconv_ref[...] = jnp.zeros_like(dconv_ref)
            dalog_ref[...] = jnp.zeros_like(dalog_ref)
            ddtb_ref[...] = jnp.zeros_like(ddtb_ref)
            donw_ref[...] = jnp.zeros_like(donw_ref)

        @pl.when(pair == 0)
        def _():
            dab_ref[...] = jnp.zeros_like(dab_ref)

        @pl.when(i == 0)
        def _():
            for h in heads_here:
                dstate_scr[h] = jnp.zeros((HEAD_DIM, HEAD_DIM), F32)
                carry_scr[h] = jnp.zeros((3, HALO, HEAD_DIM), F32)

        abv = ab_ref[...]
        w_refs = (wq_ref, wk_ref, wv_ref)
        for p, h in enumerate(heads_here):
            cols = _head_cols(p)
            for n, (ref, href) in enumerate(((q_ref, qh_ref), (k_ref, kh_ref), (v_ref, vh_ref))):
                x_scr[p, n, 0:HALO, :] = jnp.where(first_block, 0.0, href[:, cols])
                x_scr[p, n, HALO:HALO + tb, :] = ref[:, cols]
            sel_a, sel_b = _head_lane(h), _head_lane(h, heads)
            alog, dtb = _pick(sel_a, alog_ref[...]), _pick(sel_a, dtb_ref[...])
            acts = [_silu(_conv(x_scr.at[p, n], w_ref[:, cols], tb)) for n, w_ref in enumerate(w_refs)]
            (*scan, read, out), vjp_intra = jax.vjp(_gdn_intra, *acts, _pick(sel_a, abv), _pick(sel_b, abv), alog, dtb)
            for s, val in zip(op_scr[3 * p:3 * p + 3], scan):
                s[...] = val.astype(s.dtype)
            blocked = lambda ref: ref[:, cols].reshape(cpb, CHUNK, HEAD_DIM)
            _, vjp_outputs = jax.vjp(_gdn_outputs, st_ref[p], read, out, blocked(z_ref), onw_ref[...])
            dstates_scr[p][...], dread, dout, dz, donw = vjp_outputs(blocked(dog_ref))
            dz_ref[:, cols] = dz.reshape(tb, HEAD_DIM).astype(BF16)
            donw_ref[...] += donw
            later.append((vjp_intra, dread, dout, sel_a, sel_b))

        def chunk(i_rev, dstates):
            c = cpb - 1 - i_rev
            new = []
            for p in range(PAIR):
                _, vjp = jax.vjp(_gdn_scan_step, st_ref[p, c], *[s[c].astype(F32) for s in op_scr[3 * p:3 * p + 3]])
                dstate, *grads = vjp(dstates[p])
                for s, val in zip(dop_scr[3 * p:3 * p + 3], grads):
                    s[c] = val
                new.append(dstate + dstates_scr[p][c])
            return tuple(new)

        last = lax.fori_loop(0, cpb, chunk, tuple(dstate_scr[h] for h in heads_here))
        for p, h in enumerate(heads_here):
            cols = _head_cols(p)
            vjp_intra, dread, dout, sel_a, sel_b = later[p]
            dstate_scr[h] = last[p]
            *dacts, da, db, dalog, ddtb = vjp_intra((*[s[...] for s in dop_scr[3 * p:3 * p + 3]], dread, dout))
            dab_ref[...] += jnp.where(sel_a, da, 0.0) + jnp.where(sel_b, db, 0.0)
            for n, (dref, dact, w_ref) in enumerate(zip((dq_ref, dk_ref, dv_ref), dacts, w_refs)):
                dx, dw = _conv_silu_bwd(x_scr.at[p, n], w_ref[:, cols], dact, dc_scr, tb)
                x_scr[p, n] = dx
                x_scr[p, n, tb:tb + HALO, :] += carry_scr[h, n]
                carry_scr[h, n] = x_scr[p, n, 0:HALO, :]
                dref[:, cols] = x_scr[p, n, HALO:HALO + tb, :].astype(BF16)
                lanes = pl.ds(pl.multiple_of((n * heads + h) * HEAD_DIM, HEAD_DIM), HEAD_DIM)
                for j in range(CONV_K):
                    dconv_ref[j:j + 1, lanes] += dw[j]
            dalog_ref[...] += jnp.where(sel_a, dalog, 0.0)
            ddtb_ref[...] += jnp.where(sel_a, ddtb, 0.0)

    out_col = pl.BlockSpec((tb, PAIR * HEAD_DIM), lambda i, h: (nb - 1 - i, h))
    dog_spec = out_col
    col_shape = jax.ShapeDtypeStruct((t, heads * HEAD_DIM), BF16)
    row_shape = jax.ShapeDtypeStruct((1, LANES), F32)
    n_x = len(exchange)
    grid = (nb, heads // PAIR)
    outs = pl.pallas_call(
        _with_exchange(body, 16, 9, False, n_x, grid),
        name=name,
        grid=grid,
        in_specs=[col(0), col(1), col(2), halo(0), halo(1), halo(2), col(3), abspec, convw(0), convw(1), convw(2), vec, vec, vec,
                  states_spec, dog_spec] + [_ANY] * n_x,
        out_specs=[out_col, out_col, out_col, out_col, abspec,
                   pl.BlockSpec((CONV_K, n_conv), lambda i, h: (0, 0)), vec, vec, vec] + [_ANY] * n_x,
        out_shape=[col_shape, col_shape, col_shape, col_shape, jax.ShapeDtypeStruct((t, LANES), F32),
                   jax.ShapeDtypeStruct((CONV_K, n_conv), F32), row_shape, row_shape, row_shape] + _chip_shapes(False, exchange),
        scratch_shapes=[pltpu.VMEM((heads, HEAD_DIM, HEAD_DIM), F32), pltpu.VMEM((PAIR, 3, HALO + tb, HEAD_DIM), F32),
                        pltpu.VMEM((heads, 3, HALO, HEAD_DIM), F32)] + _scan_scratch(cpb, BF16) * PAIR
        + _scan_scratch(cpb, F32) * PAIR + [pltpu.VMEM((cpb, HEAD_DIM, HEAD_DIM), F32)] * PAIR
        + [pltpu.VMEM((HALO + tb + HALO, HEAD_DIM), F32)]
        + (_chip_scratch(n_x) if n_x else []),
        compiler_params=_params("arbitrary", "arbitrary", vmem=VMEM_LIMIT_WIDE_BYTES),
    )(proj, proj, proj, proj, proj, proj, proj, ab, conv_w, conv_w, conv_w, alog_row, dtb_row, onw_row, states, dog, *exchange)
    return (*outs[:9], outs[9:])


BAND = (LEFT_CHUNKS + 1) * CHUNK
PAD = LEFT_CHUNKS * CHUNK
GROUP = 2
ROWS = GROUP * CHUNK
WIN = (LEFT_CHUNKS + GROUP) * CHUNK
DIAGS = WIN + ROWS - 1
NEAR = PAD + ROWS - 1 - REL_CLIP
assert 0 < NEAR < DIAGS and WIN - PAD - 1 <= REL_CLIP and WIN % LANES == 0
ATTN_BLOCK = 1024


def _band_bias(rel_bias):
    heads = rel_bias.shape[0]
    far = jnp.broadcast_to(rel_bias[:, 2 * REL_CLIP:], (heads, NEAR + 1))
    near = rel_bias[:, 2 * REL_CLIP + NEAR + 1 - DIAGS:2 * REL_CLIP][:, ::-1]
    diag = jnp.concatenate([far, near], axis=1)
    return jnp.stack([diag[:, ROWS - 1 - r:ROWS - 1 - r + WIN] for r in range(ROWS)], axis=1)


def _band_bias_grad(dbias):
    heads = dbias.shape[0]
    diag = sum(jnp.pad(dbias[:, r, :], ((0, 0), (ROWS - 1 - r, r))) for r in range(ROWS))
    far = jnp.sum(diag[:, :NEAR + 1], axis=1, keepdims=True)
    near = diag[:, NEAR + 1:][:, ::-1]
    unused = jnp.zeros((heads, 2 * REL_CLIP - near.shape[1]), F32)
    return jnp.concatenate([unused, near, far], axis=1)


def _masked_bias(bias, n):
    r = np.arange(ROWS)[:, None]
    key = np.arange(WIN)[None, :]
    band_start = (r // CHUNK) * CHUNK
    in_band = np.logical_and(key >= band_start, key < band_start + BAND)
    in_sequence = key[None] >= PAD - np.arange(n)[:, None, None] * ROWS
    first = jnp.where(np.logical_and(in_band[None], in_sequence)[None], bias[:, None], -1e30)
    return first, jnp.where(in_band[None, None], bias[:, None], -1e30)


def _attn_groups(q_pre, z, kn, v, bias, qnw):
    q = _rms(q_pre, qnw)
    s = _bdot(q, kn, _BNT) * (HEAD_DIM ** -0.5) + bias
    p = jnp.exp(s - jnp.max(s, axis=-1, keepdims=True))
    p = p / jnp.sum(p, axis=-1, keepdims=True)
    return _bdot(p, v, _BNN) * _silu(z)


def _attn_groups_bwd(q_pre, z, kn, v, bias, qnw, dog):
    scale = HEAD_DIM ** -0.5
    inv_rms = lax.rsqrt(jnp.mean(q_pre * q_pre, axis=-1, keepdims=True) + EPS)
    q_hat = q_pre * inv_rms
    q_b = (q_hat * qnw).astype(BF16)
    s = _dot(q_b, kn, _BNT) * scale + bias
    e = jnp.exp(s - jnp.max(s, axis=-1, keepdims=True))
    p = e * (1.0 / jnp.sum(e, axis=-1, keepdims=True))
    p_b = p.astype(BF16)
    o = _dot(p_b, v, _BNN)
    sig = jax.nn.sigmoid(z)
    do = dog * (z * sig)
    dz = dog * o * (sig * (1.0 + z * (1.0 - sig)))
    do_b = do.astype(BF16)
    dv = _dot(p_b, do_b, _BTN)
    dp = _dot(do_b, v, _BNT)
    ds = p * (dp - jnp.sum(do * o, axis=-1, keepdims=True))
    ds_b = (ds * scale).astype(BF16)
    dq = _dot(ds_b, kn, _BNN)
    dkn = _dot(ds_b, q_b, _BTN)
    dqnw = jnp.sum(jnp.sum(dq * q_hat, axis=0), axis=0, keepdims=True)
    dq_hat = dq * qnw
    dq_pre = inv_rms * (dq_hat - q_hat * jnp.mean(dq_hat * q_hat, axis=-1, keepdims=True))
    return dq_pre, dz, dkn, dv, jnp.sum(ds, axis=0), dqnw


def _attn_specs(heads, tb, t):
    def col(group):
        return pl.BlockSpec((tb, HEAD_DIM), lambda h, i: (i, group * heads + h))

    def full(group):
        return pl.BlockSpec((t, HEAD_DIM), lambda h, i: (0, group * heads + h))

    bias = [pl.BlockSpec((1, tb // ROWS, ROWS, WIN), lambda h, i: (h, 0, 0, 0)),
            pl.BlockSpec((1, 1, ROWS, WIN), lambda h, i: (h, 0, 0, 0))]
    vec = pl.BlockSpec((1, HEAD_DIM), lambda h, i: (0, 0))
    return col, full, bias, vec


def _attn_windows(scr, block_start, n):
    return jnp.stack([scr[pl.ds(pl.multiple_of(block_start + g * ROWS, ROWS), WIN), :] for g in range(n)])


def _attn_fill(k_ref, v_ref, knw_ref, kn_scr, v_scr, t):
    kn_scr[0:PAD, :] = jnp.zeros((PAD, HEAD_DIM), BF16)
    v_scr[0:PAD, :] = jnp.zeros((PAD, HEAD_DIM), BF16)
    step = min(512, t)

    def fill(j, _):
        rows = pl.ds(pl.multiple_of(j * step, step), step)
        prows = pl.ds(pl.multiple_of(PAD + j * step, CHUNK), step)
        kn_scr[prows, :] = _rms(k_ref[rows, :], knw_ref[...]).astype(BF16)
        v_scr[prows, :] = v_ref[rows, :].astype(BF16)
        return 0

    lax.fori_loop(0, t // step, fill, 0)


def _attn_fwd(proj, bias, qnw_row, knw_row, *, heads, name, tb=ATTN_BLOCK):
    t = proj.shape[0]
    tb = min(tb, t)
    nb, ng = t // tb, tb // ROWS
    col, full, bias_spec, vec = _attn_specs(heads, tb, t)

    def body(q_ref, k_ref, v_ref, z_ref, first_ref, rest_ref, qnw_ref, knw_ref, og_ref, kn_scr, v_scr):
        i = pl.program_id(1)

        @pl.when(i == 0)
        def _():
            _attn_fill(k_ref, v_ref, knw_ref, kn_scr, v_scr, t)

        def run(block_bias):
            start = i * tb
            og = _attn_groups(q_ref[...].reshape(ng, ROWS, HEAD_DIM), z_ref[...].reshape(ng, ROWS, HEAD_DIM),
                              _attn_windows(kn_scr, start, ng), _attn_windows(v_scr, start, ng), block_bias, qnw_ref[...])
            og_ref[...] = og.reshape(tb, HEAD_DIM).astype(BF16)

        pl.when(i == 0)(lambda: run(first_ref[0]))
        pl.when(i > 0)(lambda: run(rest_ref[0]))

    return pl.pallas_call(
        body,
        name=name,
        grid=(heads, nb),
        in_specs=[col(0), full(1), full(2), col(3), *bias_spec, vec, vec],
        out_specs=pl.BlockSpec((tb, HEAD_DIM), lambda h, i: (i, h)),
        out_shape=jax.ShapeDtypeStruct((t, heads * HEAD_DIM), BF16),
        scratch_shapes=[pltpu.VMEM((PAD + t, HEAD_DIM), BF16), pltpu.VMEM((PAD + t, HEAD_DIM), BF16)],
        compiler_params=_params("arbitrary", "arbitrary"),
    )(proj, proj, proj, proj, *bias, qnw_row, knw_row)


def _attn_bwd(proj, bias, qnw_row, knw_row, dog, *, heads, name, tb=ATTN_BLOCK, sub=4):
    t = proj.shape[0]
    tb = min(tb, t)
    nb, ng = t // tb, tb // ROWS
    sub = min(sub, ng)
    col, full, bias_spec, vec = _attn_specs(heads, tb, t)

    def body(q_ref, k_ref, v_ref, z_ref, first_ref, rest_ref, qnw_ref, knw_ref, dog_ref,
             dq_ref, dk_ref, dv_ref, dz_ref, dbias_ref, dqnw_ref, dknw_ref, kn_scr, v_scr, dkn_scr, dv_scr):
        i = pl.program_id(1)

        @pl.when(i == 0)
        def _():
            _attn_fill(k_ref, v_ref, knw_ref, kn_scr, v_scr, t)
            dkn_scr[...] = jnp.zeros_like(dkn_scr)
            dv_scr[...] = jnp.zeros_like(dv_scr)
            dbias_ref[...] = jnp.zeros_like(dbias_ref)
            dqnw_ref[...] = jnp.zeros_like(dqnw_ref)

        def run(block_bias):
            for g0 in range(0, ng, sub):
                rows = pl.ds(g0 * ROWS, sub * ROWS)
                at = i * tb + g0 * ROWS
                blocked = lambda ref: ref[rows, :].reshape(sub, ROWS, HEAD_DIM)
                dq, dz, dkn, dv, dbias, dqnw = _attn_groups_bwd(
                    blocked(q_ref), blocked(z_ref), _attn_windows(kn_scr, at, sub), _attn_windows(v_scr, at, sub),
                    block_bias(g0), qnw_ref[...], blocked(dog_ref))
                dq_ref[rows, :] = dq.reshape(sub * ROWS, HEAD_DIM).astype(BF16)
                dz_ref[rows, :] = dz.reshape(sub * ROWS, HEAD_DIM).astype(BF16)
                for g in range(sub):
                    window = pl.ds(pl.multiple_of(at + g * ROWS, ROWS), WIN)
                    dkn_scr[window, :] += dkn[g]
                    dv_scr[window, :] += dv[g]
                dbias_ref[0] += dbias
                dqnw_ref[0] += dqnw

        pl.when(i == 0)(lambda: run(lambda g0: first_ref[0, g0:g0 + sub]))
        pl.when(i > 0)(lambda: run(lambda g0: rest_ref[0]))

        @pl.when(i == nb - 1)
        def _():
            step = min(512, t)

            def finish(j, dknw):
                rows = pl.ds(pl.multiple_of(j * step, step), step)
                prows = pl.ds(pl.multiple_of(PAD + j * step, CHUNK), step)
                _, vjp = jax.vjp(_rms, k_ref[rows, :], knw_ref[...])
                dk, dw = vjp(dkn_scr[prows, :])
                dk_ref[rows, :] = dk.astype(BF16)
                dv_ref[rows, :] = dv_scr[prows, :].astype(BF16)
                return dknw + dw

            dknw_ref[0] = lax.fori_loop(0, t // step, finish, jnp.zeros((1, HEAD_DIM), F32))

    out_col = pl.BlockSpec((tb, HEAD_DIM), lambda h, i: (i, h))
    out_full = pl.BlockSpec((t, HEAD_DIM), lambda h, i: (0, h))
    head_vec = pl.BlockSpec((1, 1, HEAD_DIM), lambda h, i: (h, 0, 0))
    col_shape = jax.ShapeDtypeStruct((t, heads * HEAD_DIM), BF16)
    vec_shape = jax.ShapeDtypeStruct((heads, 1, HEAD_DIM), F32)
    return pl.pallas_call(
        body,
        name=name,
        grid=(heads, nb),
        in_specs=[col(0), full(1), full(2), col(3), *bias_spec, vec, vec, pl.BlockSpec((tb, HEAD_DIM), lambda h, i: (i, h))],
        out_specs=[out_col, out_full, out_full, out_col, pl.BlockSpec((1, ROWS, WIN), lambda h, i: (h, 0, 0)), head_vec,
                   head_vec],
        out_shape=[col_shape, col_shape, col_shape, col_shape, jax.ShapeDtypeStruct((heads, ROWS, WIN), F32),
                   vec_shape, vec_shape],
        scratch_shapes=[pltpu.VMEM((PAD + t, HEAD_DIM), BF16), pltpu.VMEM((PAD + t, HEAD_DIM), BF16),
                        pltpu.VMEM((PAD + t, HEAD_DIM), F32), pltpu.VMEM((PAD + t, HEAD_DIM), F32)],
        compiler_params=_params("arbitrary", "arbitrary"),
    )(proj, proj, proj, proj, *bias, qnw_row, knw_row, dog)


def _lane_row(v):
    v = v.reshape(1, -1)
    return jnp.pad(v, ((0, 0), (0, LANES - v.shape[1])))


def _local_step(x, target, norm_w, wa_in, conv_w, a_log, dt_bias, onw, wa_out, wb_in, qnw, knw, rel_bias, wb_out, *,
                sharded=False):
    ha, hb = a_log.shape[-1], rel_bias.shape[-2]
    na = 4 * ha * HEAD_DIM
    wa_main = wa_in[:, :na]
    wa_ab = jnp.pad(wa_in[:, na:], ((0, 0), (0, LANES - 2 * ha)))
    alog_row, dtb_row, onw_row = _lane_row(a_log), _lane_row(dt_bias), _lane_row(onw)
    qnw_row, knw_row = _lane_row(qnw), _lane_row(knw)
    bias = _masked_bias(_band_bias(rel_bias.reshape(hb, -1)), min(ATTN_BLOCK, x.shape[0]) // ROWS)

    hn0 = _rmsnorm_fwd(x, norm_w[0:1], name="norm0")
    proj_a = _matmul(hn0, wa_main, name="a_in")
    ab_a = _matmul(hn0, wa_ab, name="a_in_ab")
    og_a, states, got = _gdn_fwd(proj_a, ab_a, conv_w, alog_row, dtb_row, onw_row, heads=ha, name="gdn_fwd",
                                 gather=[wb_in, wa_out, wb_out] if sharded else [])
    if sharded:
        wb_in, wa_out, wb_out = _join_cols(got[0]), got[1].reshape(-1, got[1].shape[-1]), got[2].reshape(-1, got[2].shape[-1])
    h1 = _matmul(og_a, wa_out, residual=x, name="a_out")
    hn1 = _rmsnorm_fwd(h1, norm_w[1:2], name="norm1")
    proj_b = _matmul(hn1, wb_in, name="b_in")
    og_b = _attn_fwd(proj_b, bias, qnw_row, knw_row, heads=hb, name="attn_fwd")
    h2 = _matmul(og_b, wb_out, residual=h1, name="b_out")
    loss, dh2, dh2_b = _loss_head(h2, target, name="loss_head")

    grad_dtype = BF16 if sharded else F32
    dog_b = _matmul(dh2_b, wb_out, trans_b=True, name="d_b_out_x")
    dwb_out = _matmul(og_b, dh2_b, trans_a=True, out_dtype=grad_dtype, name="d_b_out_w")
    dq, dk, dv, dz, dbias, dqnw, dknw = _attn_bwd(proj_b, bias, qnw_row, knw_row, dog_b, heads=hb, name="attn_bwd")
    dproj_b = jnp.concatenate([dq, dk, dv, dz], axis=1)
    dhn1 = _matmul(dproj_b, wb_in, trans_b=True, name="d_b_in_x")
    dwb_in = _matmul(hn1, dproj_b, trans_a=True, out_dtype=grad_dtype, col_slabs=N_CHIPS if sharded else 0, name="d_b_in_w")
    dh1, dh1_b, dnw1 = _rmsnorm_bwd(h1, norm_w[1:2], dhn1, dh2, name="d_norm1")

    dog_a = _matmul(dh1_b, wa_out, trans_b=True, name="d_a_out_x")
    dwa_out = _matmul(og_a, dh1_b, trans_a=True, out_dtype=grad_dtype, name="d_a_out_w")
    early = [dwb_in, _split_rows(dwa_out), _split_rows(dwb_out)] if sharded else []
    dq, dk, dv, dz, dab, dconv, dalog, ddtb, donw, landed = _gdn_bwd(
        proj_a, ab_a, conv_w, alog_row, dtb_row, onw_row, states, dog_a, heads=ha, name="gdn_bwd", exchange=early)
    if sharded:
        dwb_in, dwa_out, dwb_out = landed
    dproj_a = jnp.concatenate([dq, dk, dv, dz], axis=1)
    dab_b = dab.astype(BF16)
    dwa_in = jnp.concatenate(
        [_matmul(hn0, dproj_a, trans_a=True, out_dtype=grad_dtype, name="d_a_in_w"),
         _matmul(hn0, dab_b, trans_a=True, out_dtype=grad_dtype, name="d_a_in_ab_w")[:, :2 * ha]], axis=1)
    if sharded:
        dhn0, (dwa_in, dconv) = _matmul(dproj_a, wa_main, trans_b=True, name="d_a_in_x",
                                        exchange=[_split_cols(dwa_in), _split_cols(dconv)])
    else:
        dhn0 = _matmul(dproj_a, wa_main, trans_b=True, name="d_a_in_x")
    dhn0 = _matmul(dab_b, wa_ab, trans_b=True, residual=dhn0, name="d_a_in_ab_x")
    dx, _, dnw0 = _rmsnorm_bwd(x, norm_w[0:1], dhn0, dh1, name="d_norm0")

    drel = _band_bias_grad(dbias)
    grads = dict(
        norm_w=jnp.concatenate([dnw0, dnw1], axis=0), a_w_in=dwa_in, a_conv_w=dconv, a_a_log=dalog[:, :ha],
        a_dt_bias=ddtb[:, :ha], a_out_norm_w=donw, a_w_out=dwa_out, b_w_in=dwb_in, b_q_norm_w=jnp.sum(dqnw, axis=0),
        b_k_norm_w=jnp.sum(dknw, axis=0), b_rel_bias=drel[None], b_w_out=dwb_out)
    return loss, dx, grads


_ANY = pl.BlockSpec(memory_space=pl.ANY)
_CHIP_FLIPS = ((1, 0), (0, 1), (1, 1))


def _place():
    x, y, c = lax.axis_index("x"), lax.axis_index("y"), lax.axis_index("c")
    return x, y, c


def _flip(v, bit):
    return 1 - v if bit else v


def _remote(src, dst, send_sem, recv_sem, peer):
    return pltpu.make_async_remote_copy(src_ref=src, dst_ref=dst, send_sem=send_sem, recv_sem=recv_sem, device_id=peer,
                                        device_id_type=MESH)


def _comm_call(body, arrays, out_shapes, n_remote, n_local, name):
    scratch = [pltpu.SemaphoreType.DMA((n_remote,)), pltpu.SemaphoreType.DMA((n_remote,))]
    if n_local:
        scratch.append(pltpu.SemaphoreType.DMA((n_local,)))
    return pl.pallas_call(
        body, name=name, in_specs=[_ANY] * len(arrays), out_specs=[_ANY] * len(out_shapes), out_shape=out_shapes,
        scratch_shapes=scratch)(*arrays)


def _chip_scratch(n):
    return [pltpu.SemaphoreType.DMA((3 * n,)), pltpu.SemaphoreType.DMA((3 * n,)), pltpu.SemaphoreType.DMA((n,))]


def _chip_shapes(gather, arrays):
    return [jax.ShapeDtypeStruct(((N_CHIPS,) + s.shape) if gather else s.shape, s.dtype) for s in arrays]


def _chip_traffic(gather, ins, outs, sems):
    send_sems, recv_sems, local_sems = sems
    x, y, c = _place()
    mine = 2 * x + y
    local, remote, landing = [], [], []
    for a in range(len(ins)):
        local.append(pltpu.make_async_copy(ins[a] if gather else ins[a].at[mine], outs[a].at[mine], local_sems.at[a]))
        for k, (fx, fy) in enumerate(_CHIP_FLIPS):
            peer = (_flip(x, fx), _flip(y, fy), c)
            theirs = 2 * peer[0] + peer[1]
            src = ins[a] if gather else ins[a].at[theirs]
            pair = send_sems.at[3 * a + k], recv_sems.at[3 * a + k]
            remote.append(_remote(src, outs[a].at[mine], *pair, peer))
            landing.append(_remote(src, outs[a].at[theirs], *pair, peer))
    return local + remote, (local, landing, remote)


def _start(traffic):
    for cp in traffic[0]:
        cp.start()


def _finish(traffic):
    local, landing, remote = traffic[1]
    for cp in local:
        cp.wait()
    for cp in landing:
        cp.wait_recv()
    for cp in remote:
        cp.wait_send()


def _with_exchange(compute, n_in, n_out, gather, n_x, grid):
    if not n_x:
        return compute

    def body(*refs):
        ins, x_in = refs[:n_in], refs[n_in:n_in + n_x]
        outs, x_out = refs[n_in + n_x:n_in + n_x + n_out], refs[n_in + n_x + n_out:n_in + 2 * n_x + n_out]
        scratch, sems = refs[n_in + 2 * n_x + n_out:-3], refs[-3:]
        traffic = _chip_traffic(gather, x_in, x_out, sems)
        first = functools.reduce(jnp.logical_and, [pl.program_id(d) == 0 for d in range(len(grid))])
        last = functools.reduce(jnp.logical_and, [pl.program_id(d) == grid[d] - 1 for d in range(len(grid))])

        @pl.when(first)
        def _():
            _start(traffic)

        compute(*ins, *outs, *scratch)

        @pl.when(last)
        def _():
            _finish(traffic)

    return body


def _chip_call(gather, arrays, *, name):
    n = len(arrays)

    def body(*refs):
        traffic = _chip_traffic(gather, refs[:n], refs[n:2 * n], refs[2 * n:])
        _start(traffic)
        _finish(traffic)

    return pl.pallas_call(
        body, name=name, in_specs=[_ANY] * n, out_specs=[_ANY] * n, out_shape=_chip_shapes(gather, arrays),
        scratch_shapes=_chip_scratch(n))(*arrays)


def _gather_shared(shard, small, *, name):
    rows = shard.shape[0]
    assert rows % 2 == 0
    half = rows // 2

    def body(shard_ref, small_ref, out_ref, small_out_ref, send_sems, recv_sems, local_sems):
        x, y, c = _place()
        mine = 2 * x + y
        sibling = (x, y, 1 - c)
        my_rows = pl.ds(pl.multiple_of(c * half, 8), half)
        local = [pltpu.make_async_copy(shard_ref, out_ref.at[mine], local_sems.at[0]),
                 pltpu.make_async_copy(small_ref, small_out_ref.at[mine], local_sems.at[1])]
        sent, landed, passed_on, handed = [], [], [], []
        for k, (fx, fy) in enumerate(_CHIP_FLIPS):
            peer = (_flip(x, fx), _flip(y, fy), c)
            theirs = 2 * peer[0] + peer[1]
            ici, d2d, tiny = [(send_sems.at[3 * n + k], recv_sems.at[3 * n + k]) for n in range(3)]
            sent.append(_remote(shard_ref.at[my_rows], out_ref.at[mine, my_rows], *ici, peer))
            landed.append(_remote(shard_ref.at[my_rows], out_ref.at[theirs, my_rows], *ici, peer))
            sent.append(_remote(small_ref, small_out_ref.at[mine], *tiny, peer))
            landed.append(_remote(small_ref, small_out_ref.at[theirs], *tiny, peer))
            passed_on.append(_remote(out_ref.at[theirs, my_rows], out_ref.at[theirs, my_rows], *d2d, sibling))
            other_rows = pl.ds(pl.multiple_of((1 - c) * half, 8), half)
            handed.append(_remote(out_ref.at[theirs, other_rows], out_ref.at[theirs, other_rows], *d2d, sibling))
        for cp in local + sent:
            cp.start()
        for k in range(3):
            landed[2 * k].wait_recv()
            passed_on[k].start()
        for k in range(3):
            landed[2 * k + 1].wait_recv()
            handed[k].wait_recv()
        for cp in local:
            cp.wait()
        for cp in sent + passed_on:
            cp.wait_send()

    return pl.pallas_call(
        body, name=name, in_specs=[_ANY] * 2, out_specs=[_ANY] * 2, out_shape=_chip_shapes(True, [shard, small]),
        scratch_shapes=[pltpu.SemaphoreType.DMA((9,)), pltpu.SemaphoreType.DMA((9,)), pltpu.SemaphoreType.DMA((2,))],
    )(shard, small)


def _swap_pair(arrays, *, name):
    n = len(arrays)

    def body(*refs):
        ins, outs, (send_sems, recv_sems) = refs[:n], refs[n:2 * n], refs[2 * n:]
        x, y, c = _place()
        copies = [_remote(ins[a], outs[a], send_sems.at[a], recv_sems.at[a], (x, y, 1 - c)) for a in range(n)]
        for cp in copies:
            cp.start()
        for cp in copies:
            cp.wait_recv()
        for cp in copies:
            cp.wait_send()

    shapes = [jax.ShapeDtypeStruct(s.shape, s.dtype) for s in arrays]
    return _comm_call(body, arrays, shapes, n, 0, name)


def _gather_all(tile, *, name):
    def body(in_ref, out_ref, send_sems, recv_sems, local_sems):
        x, y, c = _place()
        mine = 4 * x + 2 * y + c
        local = pltpu.make_async_copy(in_ref, out_ref.at[mine], local_sems.at[0])
        remote, landing = [], []
        for k in range(1, N_DEV):
            peer = (_flip(x, k & 4), _flip(y, k & 2), _flip(c, k & 1))
            sems = send_sems.at[k - 1], recv_sems.at[k - 1]
            remote.append(_remote(in_ref, out_ref.at[mine], *sems, peer))
            landing.append(_remote(in_ref, out_ref.at[4 * peer[0] + 2 * peer[1] + peer[2]], *sems, peer))
        for cp in [local] + remote:
            cp.start()
        local.wait()
        for cp in landing:
            cp.wait_recv()
        for cp in remote:
            cp.wait_send()

    return _comm_call(body, [tile], [jax.ShapeDtypeStruct((N_DEV,) + tile.shape, tile.dtype)], N_DEV - 1, 1, name)[0]


def _sum_slots(slabs, *, name, tr=128):
    s, r, c = slabs.shape
    tr = min(tr, r)

    def body(in_ref, o_ref):
        acc = in_ref[0].astype(F32)
        for j in range(1, s):
            acc = acc + in_ref[j].astype(F32)
        o_ref[...] = acc

    return pl.pallas_call(
        body, name=name, grid=(r // tr,),
        in_specs=[pl.BlockSpec((s, tr, c), lambda i: (0, i, 0))], out_specs=pl.BlockSpec((tr, c), lambda i: (i, 0)),
        out_shape=jax.ShapeDtypeStruct((r, c), F32), compiler_params=_params("parallel"))(slabs)


def _adamw_math(w, g, m, v):
    m = ADAM_B1 * m + (1.0 - ADAM_B1) * g
    v = ADAM_B2 * v + (1.0 - ADAM_B2) * (g * g)
    m_hat = m / (1.0 - ADAM_B1 ** ADAM_STEP)
    v_hat = v / (1.0 - ADAM_B2 ** ADAM_STEP)
    delta = -ADAM_LR * (m_hat / (jnp.sqrt(v_hat) + ADAM_EPS) + ADAM_WD * w)
    return delta, m, v


def _adamw(w, m, v, parts, *, name, tr=128):
    r, c = w.shape
    tr = min(tr, r)
    s = len(parts)

    def body(w_ref, m_ref, v_ref, *refs):
        g_ref, d_ref, nm_ref, nv_ref = refs[s:]
        g = refs[0][...]
        for p_ref in refs[1:s]:
            g = g + p_ref[...]
        g_ref[...] = g
        d_ref[...], nm_ref[...], nv_ref[...] = _adamw_math(w_ref[...], g, m_ref[...], v_ref[...])

    blk = pl.BlockSpec((tr, c), lambda i: (i, 0))
    shape = jax.ShapeDtypeStruct((r, c), F32)
    return pl.pallas_call(
        body, name=name, grid=(r // tr,), in_specs=[blk] * (3 + s), out_specs=[blk] * 4, out_shape=[shape] * 4,
        compiler_params=_params("parallel"))(w, m, v, *parts)


_BIG = ("a_w_in", "b_w_in", "a_w_out", "b_w_out", "a_conv_w")
_SMALL = ("norm_w", "a_a_log", "a_dt_bias", "a_out_norm_w", "b_q_norm_w", "b_k_norm_w", "b_rel_bias")
_ORDER = ("norm_w", "a_w_in", "a_conv_w", "a_a_log", "a_dt_bias", "a_out_norm_w", "a_w_out", "b_w_in", "b_q_norm_w",
          "b_k_norm_w", "b_rel_bias", "b_w_out")


def _join_cols(g):
    return jnp.transpose(g, (1, 0, 2)).reshape(g.shape[1], -1)


def _split_cols(g):
    return jnp.transpose(g.reshape(g.shape[0], N_CHIPS, -1), (1, 0, 2))


def _split_rows(g):
    return g.reshape(N_CHIPS, -1, g.shape[-1])


def _pack(d):
    flat = jnp.concatenate([d[n].reshape(-1) for n in _SMALL])
    return jnp.pad(flat, (0, -flat.shape[0] % LANES)).reshape(1, -1)


def _unpack(row, like):
    out, at = {}, 0
    for n in _SMALL:
        size = like[n].size
        out[n] = row[0, at:at + size].reshape(like[n].shape)
        at += size
    return out


def kernel(x, norm_w, a_w_in, a_conv_w, a_a_log, a_dt_bias, a_out_norm_w, a_w_out, b_w_in, b_q_norm_w, b_k_norm_w, b_rel_bias, b_w_out, loss_target, m_norm_w, m_a_w_in, m_a_conv_w, m_a_a_log, m_a_dt_bias, m_a_out_norm_w, m_a_w_out, m_b_w_in, m_b_q_norm_w, m_b_k_norm_w, m_b_rel_bias, m_b_w_out, v_norm_w, v_a_w_in, v_a_conv_w, v_a_a_log, v_a_dt_bias, v_a_out_norm_w, v_a_w_out, v_b_w_in, v_b_q_norm_w, v_b_k_norm_w, v_b_rel_bias, v_b_w_out):
    w = dict(norm_w=norm_w, a_w_in=a_w_in, a_conv_w=a_conv_w, a_a_log=a_a_log, a_dt_bias=a_dt_bias,
             a_out_norm_w=a_out_norm_w, a_w_out=a_w_out, b_w_in=b_w_in, b_q_norm_w=b_q_norm_w, b_k_norm_w=b_k_norm_w,
             b_rel_bias=b_rel_bias, b_w_out=b_w_out)
    m = dict(norm_w=m_norm_w, a_w_in=m_a_w_in, a_conv_w=m_a_conv_w, a_a_log=m_a_a_log, a_dt_bias=m_a_dt_bias,
             a_out_norm_w=m_a_out_norm_w, a_w_out=m_a_w_out, b_w_in=m_b_w_in, b_q_norm_w=m_b_q_norm_w,
             b_k_norm_w=m_b_k_norm_w, b_rel_bias=m_b_rel_bias, b_w_out=m_b_w_out)
    v = dict(norm_w=v_norm_w, a_w_in=v_a_w_in, a_conv_w=v_a_conv_w, a_a_log=v_a_a_log, a_dt_bias=v_a_dt_bias,
             a_out_norm_w=v_a_out_norm_w, a_w_out=v_a_w_out, b_w_in=v_b_w_in, b_q_norm_w=v_b_q_norm_w,
             b_k_norm_w=v_b_k_norm_w, b_rel_bias=v_b_rel_bias, b_w_out=v_b_w_out)

    wa_in, conv = _gather_shared(a_w_in[0].astype(BF16), a_conv_w[0], name="gather_a_in")
    loss, dx, grads = _local_step(
        x[0], loss_target[0], norm_w, _join_cols(wa_in), _join_cols(conv), a_a_log, a_dt_bias, a_out_norm_w,
        a_w_out[0].astype(BF16), b_w_in[0].astype(BF16), b_q_norm_w, b_k_norm_w, b_rel_bias, b_w_out[0].astype(BF16),
        sharded=True)
    loss = lax.psum(loss, ("x", "y", "c"))

    mine = [_sum_slots(grads[n], name=f"chip_sum_{n}") for n in _BIG]
    theirs = _swap_pair(mine, name="pair_grads")
    out = {}
    for n, p, q in zip(_BIG, mine, theirs):
        out[n] = [r[None] for r in _adamw(w[n][0], m[n][0], v[n][0], [p, q], name=f"adamw_{n}")]

    row = _pack(grads)
    tiles = _gather_all(jnp.broadcast_to(row, (8, row.shape[1])), name="gather_small_grads")
    res = _adamw(_pack(w), _pack(m), _pack(v), [tiles[d, 0:1, :] for d in range(N_DEV)], name="adamw_small")
    unpacked = [_unpack(r, w) for r in res]
    for n in _SMALL:
        out[n] = [u[n] for u in unpacked]

    return (loss, dx[None], *[out[n][0] for n in _ORDER], *[out[n][1] for n in _ORDER], *[out[n][2] for n in _ORDER],
            *[out[n][3] for n in _ORDER])
```

```python
import functools

import numpy as np
import jax
import jax.numpy as jnp
from jax import lax
from jax.experimental import pallas as pl
from jax.experimental.pallas import tpu as pltpu

F32 = jnp.float32
BF16 = jnp.bfloat16

CHUNK = 64
HEAD_DIM = 128
LEFT_CHUNKS = 8
REL_CLIP = 256
CONV_K = 4
EPS = 1e-6
HALO = 8

ADAM_LR = 0.001
ADAM_B1 = 0.9
ADAM_B2 = 0.999
ADAM_EPS = 1e-08
ADAM_WD = 0.01
ADAM_STEP = 10

LANES = 128
N_CHIPS = 4
N_DEV = 8
VMEM_LIMIT_BYTES = 56 * 1024 * 1024
VMEM_LIMIT_WIDE_BYTES = 63 * 1024 * 1024
MESH = pl.DeviceIdType.MESH
HIGHEST = lax.Precision.HIGHEST


def _params(*sem, vmem=VMEM_LIMIT_BYTES):
    return pltpu.CompilerParams(dimension_semantics=sem, vmem_limit_bytes=vmem)


def _dot(a, b, dims=(((1,), (0,)), ((), ())), precision=None):
    return lax.dot_general(a, b, dims, precision=precision, preferred_element_type=F32)


_NT = (((1,), (1,)), ((), ()))
_TN = (((0,), (0,)), ((), ()))


def _bdot(a, b, dims=(((1,), (0,)), ((), ()))):
    return _dot(a.astype(BF16), b.astype(BF16), dims)


def _fdot(a, b, dims=(((1,), (0,)), ((), ()))):
    return _dot(a, b, dims, precision=lax.Precision.HIGH)


def _silu(x):
    return x * jax.nn.sigmoid(x)


def _stacks(x):
    if not isinstance(x, (list, tuple)) and x.ndim != 3:
        return None
    arrays = list(x) if isinstance(x, (list, tuple)) else [x]
    assert len({(v.shape[1], v.shape[2], v.dtype) for v in arrays}) == 1
    starts = [sum(v.shape[0] for v in arrays[:r]) for r in range(len(arrays))]
    return arrays, starts, starts[-1] + arrays[-1].shape[0]


def _static_pick(table, index):
    out = table[-1]
    for s in range(len(table) - 2, -1, -1):
        out = jnp.where(index == s, table[s], out)
    return out


def _matmul(a, b, *, name, trans_a=False, trans_b=False, residual=None, out_dtype=F32, tm=1024, tn=1024, tk=2048,
            col_slabs=0, order=None, exchange=()):
    assert not (trans_a and trans_b)
    a_stack, b_stack = _stacks(a), _stacks(b)
    assert not (a_stack and (trans_a or b_stack)) and not (b_stack and trans_b)
    a_list, b_list = (a_stack[0] if a_stack else [a]), (b_stack[0] if b_stack else [b])
    a0, b0 = a_list[0], b_list[0]
    k, m = (a_stack[2] * a0.shape[2], a0.shape[1]) if a_stack else a.shape if trans_a else a.shape[::-1]
    n = b_stack[2] * b0.shape[2] if b_stack else b.shape[0] if trans_b else b.shape[1]
    tm, tn, tk = min(tm, m), min(tn, n // max(col_slabs, 1)), min(tk, k)
    if a_stack:
        tk = min(tk, a0.shape[2])
        per_k = a0.shape[2] // tk
    if b_stack:
        tn = min(tn, b0.shape[2])
        per_n = b0.shape[2] // tn
    assert m % tm == 0 and n % tn == 0 and k % tk == 0, (a0.shape, b0.shape, tm, tn, tk)
    nk = k // tk
    dims = _NT if trans_b else _TN if trans_a else (((1,), (0,)), ((), ()))
    order = list(order) if order is not None else list(range(max(a_stack[2] if a_stack else 0, b_stack[2] if b_stack else 0)))
    na, nb = len(a_list), len(b_list)

    def group_of(r, stack, position):
        arrays, starts, _ = stack
        local = position - starts[r]
        return jnp.logical_and(local >= 0, local < arrays[r].shape[0]), jnp.clip(local, 0, arrays[r].shape[0] - 1)

    def body(*refs):
        a_refs, b_refs = refs[:na], refs[na:na + nb]
        r_ref = refs[na + nb] if residual is not None else None
        o_ref, acc_ref = refs[-2:]
        j, kk = pl.program_id(1), pl.program_id(2)

        @pl.when(kk == 0)
        def _():
            acc_ref[...] = jnp.zeros_like(acc_ref)

        for ra, a_ref in enumerate(a_refs):
            for rb, b_ref in enumerate(b_refs):
                def add(a_ref=a_ref, b_ref=b_ref):
                    acc_ref[...] += _dot(a_ref[...], b_ref[...], dims)

                if na > 1:
                    pl.when(group_of(ra, a_stack, kk // per_k)[0])(add)
                elif nb > 1:
                    pl.when(group_of(rb, b_stack, j // per_n)[0])(add)
                else:
                    add()

        @pl.when(kk == nk - 1)
        def _():
            r = acc_ref[...]
            if r_ref is not None:
                r = r + r_ref[...]
            o_ref[...] = r.astype(o_ref.dtype)

    if a_stack:
        a_specs = [pl.BlockSpec((None, tm, tk), lambda i, j, kk, r=r: (group_of(r, a_stack, kk // per_k)[1], i, kk % per_k))
                   for r in range(na)]
        b_k = lambda kk: _static_pick(order, kk // per_k) * per_k + kk % per_k
    else:
        a_specs = [pl.BlockSpec((tk, tm), lambda i, j, kk: (kk, i)) if trans_a else pl.BlockSpec((tm, tk), lambda i, j, kk: (i, kk))]
        b_k = lambda kk: kk
    if b_stack:
        b_specs = [pl.BlockSpec((None, tk, tn), lambda i, j, kk, r=r: (group_of(r, b_stack, j // per_n)[1], kk, j % per_n))
                   for r in range(nb)]
        out_col = lambda j: _static_pick(order, j // per_n) * per_n + j % per_n
    else:
        b_specs = [pl.BlockSpec((tn, tk), lambda i, j, kk: (j, b_k(kk))) if trans_b
                   else pl.BlockSpec((tk, tn), lambda i, j, kk: (b_k(kk), j))]
        out_col = lambda j: j
    in_specs = a_specs + b_specs
    args = a_list + b_list
    if residual is not None:
        in_specs.append(pl.BlockSpec((tm, tn), lambda i, j, kk: (i, j)))
        args.append(residual)
    grid = (m // tm, n // tn, nk)
    n_x = len(exchange)
    if col_slabs:
        per = n // col_slabs // tn
        assert per * tn * col_slabs == n, (n, tn, col_slabs)
        out_spec = pl.BlockSpec((None, tm, tn), lambda i, j, kk: (out_col(j) // per, i, out_col(j) % per))
        out_shape = jax.ShapeDtypeStruct((col_slabs, m, n // col_slabs), out_dtype)
    else:
        out_spec = pl.BlockSpec((tm, tn), lambda i, j, kk: (i, out_col(j)))
        out_shape = jax.ShapeDtypeStruct((m, n), out_dtype)
    out, *landed = pl.pallas_call(
        _with_exchange(body, len(args), 1, False, n_x, grid),
        name=name,
        grid=grid,
        in_specs=in_specs + [_ANY] * n_x,
        out_specs=[out_spec] + [_ANY] * n_x,
        out_shape=[out_shape] + _chip_shapes(False, exchange),
        scratch_shapes=[pltpu.VMEM((tm, tn), F32)] + (_chip_scratch(n_x) if n_x else []),
        compiler_params=_params(*(("arbitrary",) * 3 if n_x else ("parallel", "parallel", "arbitrary"))),
    )(*args, *exchange)
    return (out, landed) if n_x else out


def _rms(x, w):
    return x * lax.rsqrt(jnp.mean(x * x, axis=-1, keepdims=True) + EPS) * w


def _rmsnorm_fwd(x, w_row, *, name, tr=512):
    t, d = x.shape
    tr = min(tr, t)

    def body(x_ref, w_ref, o_ref):
        o_ref[...] = _rms(x_ref[...], w_ref[...]).astype(BF16)

    return pl.pallas_call(
        body,
        name=name,
        grid=(t // tr,),
        in_specs=[pl.BlockSpec((tr, d), lambda i: (i, 0)), pl.BlockSpec((1, d), lambda i: (0, 0))],
        out_specs=pl.BlockSpec((tr, d), lambda i: (i, 0)),
        out_shape=jax.ShapeDtypeStruct((t, d), BF16),
        compiler_params=_params("parallel"),
    )(x, w_row)


def _rmsnorm_bwd(x, w_row, dy, dres, *, name, tr=256):
    t, d = x.shape
    tr = min(tr, t)

    def body(x_ref, w_ref, dy_ref, dres_ref, dx_ref, dxb_ref, dw_ref):
        @pl.when(pl.program_id(0) == 0)
        def _():
            dw_ref[...] = jnp.zeros_like(dw_ref)

        _, vjp = jax.vjp(_rms, x_ref[...], w_ref[...])
        dx, dw = vjp(dy_ref[...])
        dx = dx + dres_ref[...]
        dx_ref[...] = dx
        dxb_ref[...] = dx.astype(BF16)
        dw_ref[...] += dw

    row = pl.BlockSpec((tr, d), lambda i: (i, 0))
    vec = pl.BlockSpec((1, d), lambda i: (0, 0))
    return pl.pallas_call(
        body,
        name=name,
        grid=(t // tr,),
        in_specs=[row, vec, row, row],
        out_specs=[row, row, vec],
        out_shape=[jax.ShapeDtypeStruct((t, d), F32), jax.ShapeDtypeStruct((t, d), BF16), jax.ShapeDtypeStruct((1, d), F32)],
        compiler_params=_params("arbitrary"),
    )(x, w_row, dy, dres)


def _loss_head(h, target, *, name, tr=512):
    t, d = h.shape
    tr = min(tr, t)

    def body(h_ref, t_ref, dh_ref, dhb_ref, part_ref):
        @pl.when(pl.program_id(0) == 0)
        def _():
            part_ref[...] = jnp.zeros_like(part_ref)

        err = h_ref[...] - t_ref[...]
        dh = err * (1.0 / d)
        dh_ref[...] = dh
        dhb_ref[...] = dh.astype(BF16)
        part_ref[...] += jnp.sum(err * err, axis=0, keepdims=True)

    row = pl.BlockSpec((tr, d), lambda i: (i, 0))
    vec = pl.BlockSpec((1, d), lambda i: (0, 0))
    dh, dhb, part = pl.pallas_call(
        body,
        name=name,
        grid=(t // tr,),
        in_specs=[row, row],
        out_specs=[row, row, vec],
        out_shape=[jax.ShapeDtypeStruct((t, d), F32), jax.ShapeDtypeStruct((t, d), BF16), jax.ShapeDtypeStruct((1, d), F32)],
        compiler_params=_params("arbitrary"),
    )(h, target)
    return 0.5 / d * jnp.sum(part), dh, dhb


_BNN = (((2,), (1,)), ((0,), (0,)))
_BNT = (((2,), (2,)), ((0,), (0,)))
_BTN = (((1,), (1,)), ((0,), (0,)))


_TAP0 = HALO - (CONV_K - 1)


def _conv(x_ref, w, rows):
    c = w[0:1, :] * x_ref[_TAP0:_TAP0 + rows, :]
    for j in range(1, CONV_K):
        c = c + w[j:j + 1, :] * x_ref[_TAP0 + j:_TAP0 + j + rows, :]
    return c


def _conv_silu_bwd(x_ref, w, dact, dc_ref, rows):
    c = _conv(x_ref, w, rows)
    sig = jax.nn.sigmoid(c)
    dc = dact * (sig * (1.0 + c * (1.0 - sig)))
    dw = [jnp.sum(dc * x_ref[_TAP0 + j:_TAP0 + j + rows, :], axis=0, keepdims=True) for j in range(CONV_K)]
    dc_ref[0:HALO, :] = jnp.zeros((HALO, HEAD_DIM), F32)
    dc_ref[HALO:HALO + rows, :] = dc
    dc_ref[HALO + rows:HALO + rows + HALO, :] = jnp.zeros((HALO, HEAD_DIM), F32)
    first = HALO - _TAP0
    dx = w[0:1, :] * dc_ref[first:first + HALO + rows, :]
    for j in range(1, CONV_K):
        dx = dx + w[j:j + 1, :] * dc_ref[first - j:first - j + HALO + rows, :]
    return dx, dw


@jax.custom_vjp
def _unit_lower_inverse(neg_l):
    n = neg_l.shape[0]
    eye = (lax.broadcasted_iota(jnp.int32, (n, CHUNK, CHUNK), 1) == lax.broadcasted_iota(jnp.int32, (n, CHUNK, CHUNK), 2))
    inv = eye.astype(F32) + neg_l
    power = _bdot(neg_l, neg_l, _BNN)
    for _ in range(4):
        both = _bdot(jnp.concatenate([inv, power], axis=1), power, _BNN)
        inv, power = inv + both[:, :CHUNK], both[:, CHUNK:]
    return inv + _bdot(inv, power, _BNN)


def _unit_lower_inverse_fwd(neg_l):
    inv = _unit_lower_inverse(neg_l)
    return inv, inv


def _unit_lower_inverse_bwd(inv, dinv):
    return (_fdot(_fdot(inv, dinv, _BTN), inv, _BNT),)


_unit_lower_inverse.defvjp(_unit_lower_inverse_fwd, _unit_lower_inverse_bwd)


def _gdn_intra(qt, kt, v, a, b, alog, dtb):
    n = a.shape[0] // CHUNK
    q = qt * lax.rsqrt(jnp.sum(qt * qt, axis=-1, keepdims=True) + EPS) * (HEAD_DIM ** -0.5)
    k = kt * lax.rsqrt(jnp.sum(kt * kt, axis=-1, keepdims=True) + EPS)
    lanes = jnp.ones((1, HEAD_DIM), F32)
    beta = jax.nn.sigmoid(b) * lanes
    sp = a + dtb
    g = (-jnp.exp(alog) * (jnp.maximum(sp, 0.0) + jnp.log(1.0 + jnp.exp(-jnp.abs(sp))))) * lanes
    q, k, v, beta, g = (t.reshape(n, CHUNK, HEAD_DIM) for t in (q, k, v, beta, g))

    row = lax.broadcasted_iota(jnp.int32, (n, CHUNK, CHUNK), 1)
    col = lax.broadcasted_iota(jnp.int32, (n, CHUNK, CHUNK), 2)
    tri_incl = row >= col
    tri_strict = row > col
    gc = _fdot(tri_incl.astype(F32), g, _BNN)
    gc_row = _fdot(g[:, :, :CHUNK], (row <= col).astype(F32), _BTN)
    decay = jnp.exp(jnp.where(tri_incl, gc[:, :, :CHUNK] - gc_row, -1e30))
    kb = k * beta
    vb = v * beta
    with_k = _bdot(jnp.concatenate([kb, q], axis=1), k, _BNT)
    neg_l = jnp.where(tri_strict, -(with_k[:, :CHUNK] * decay), 0.0)
    qk = jnp.where(tri_incl, with_k[:, CHUNK:] * decay, 0.0)
    inv = _unit_lower_inverse(neg_l)
    e = jnp.exp(gc)
    solved = _bdot(inv, jnp.concatenate([kb * e, vb], axis=2), _BNN)
    g_last = gc[:, CHUNK - 1:CHUNK, :]
    k_dec = k * jnp.exp(g_last - gc)
    from_k = _bdot(k_dec, solved, _BTN)
    from_qk = _bdot(qk, solved, _BNN)
    step, add = -from_k[:, :, :HEAD_DIM], from_k[:, :, HEAD_DIM:]
    read, out = q * e - from_qk[:, :, :HEAD_DIM], from_qk[:, :, HEAD_DIM:]
    return step, add, jnp.exp(g_last), read, out


def _gdn_scan_step(state, step, add, decay_last):
    return state * decay_last + _bdot(step, state) + add


def _gdn_outputs(states, read, out, z, onw):
    return _rms(_bdot(read, states, _BNN) + out, onw) * _silu(z)


def _scan_scratch(n, dtype):
    return [pltpu.VMEM((n, HEAD_DIM, HEAD_DIM), dtype), pltpu.VMEM((n, HEAD_DIM, HEAD_DIM), F32), pltpu.VMEM((n, 1, HEAD_DIM), F32)]


def _head_lane(h, offset=0):
    return lax.broadcasted_iota(jnp.int32, (1, LANES), 1) == h + offset


def _pick(mask, x):
    return jnp.sum(jnp.where(mask, x, 0.0), axis=1, keepdims=True)


def _gdn_specs(heads, tb, rev, nb, PAIR):
    assert heads % PAIR == 0
    blk = (lambda i: nb - 1 - i) if rev else (lambda i: i)
    hb = tb // HALO
    width, pairs = PAIR * HEAD_DIM, heads // PAIR

    def col(group):
        return pl.BlockSpec((tb, width), lambda i, h: (blk(i), group * pairs + h))

    def halo(group):
        return pl.BlockSpec((HALO, width), lambda i, h: (jnp.maximum(blk(i) * hb - 1, 0), group * pairs + h))

    def convw(group):
        return pl.BlockSpec((CONV_K, width), lambda i, h: (0, group * pairs + h))

    vec = pl.BlockSpec((1, LANES), lambda i, h: (0, 0))
    ab = pl.BlockSpec((tb, LANES), lambda i, h: (blk(i), 0))
    states = pl.BlockSpec((PAIR, tb // CHUNK, HEAD_DIM, HEAD_DIM), lambda i, h: (h, blk(i), 0, 0))
    return blk, col, halo, convw, vec, ab, states


def _head_cols(p):
    return slice(p * HEAD_DIM, (p + 1) * HEAD_DIM)


def _gdn_fwd(proj, ab, conv_w, alog_row, dtb_row, onw_row, *, heads, name, tb=1024, pair=2, gather=()):
    t = proj.shape[0]
    tb = min(tb, t)
    nb, cpb = t // tb, tb // CHUNK
    PAIR = pair
    _, col, halo, convw, vec, abspec, states = _gdn_specs(heads, tb, False, nb, PAIR)

    def body(q_ref, k_ref, v_ref, qh_ref, kh_ref, vh_ref, z_ref, ab_ref, wq_ref, wk_ref, wv_ref, alog_ref, dtb_ref, onw_ref,
             og_ref, st_ref, state_scr, x_scr, *op_scr):
        i, pair = pl.program_id(0), pl.program_id(1)
        abv = ab_ref[...]
        heads_here, later = [pair * PAIR + p for p in range(PAIR)], []
        for p, h in enumerate(heads_here):
            cols = _head_cols(p)
            for n, (ref, href) in enumerate(((q_ref, qh_ref), (k_ref, kh_ref), (v_ref, vh_ref))):
                x_scr[p, n, 0:HALO, :] = jnp.where(i > 0, href[:, cols], 0.0)
                x_scr[p, n, HALO:HALO + tb, :] = ref[:, cols]
            sel_a, sel_b = _head_lane(h), _head_lane(h, heads)
            alog, dtb = _pick(sel_a, alog_ref[...]), _pick(sel_a, dtb_ref[...])
            acts = [_silu(_conv(x_scr.at[p, n], w_ref[:, cols], tb)) for n, w_ref in enumerate((wq_ref, wk_ref, wv_ref))]
            *scan, read, out = _gdn_intra(*acts, _pick(sel_a, abv), _pick(sel_b, abv), alog, dtb)
            for scr, val in zip(op_scr[3 * p:3 * p + 3], scan):
                scr[...] = val.astype(scr.dtype)
            later.append((read, out))

        def chunk(c, states):
            for p in range(PAIR):
                st_ref[p, c] = states[p]
            return tuple(_gdn_scan_step(states[p], *[scr[c] for scr in op_scr[3 * p:3 * p + 3]]) for p in range(PAIR))

        @pl.when(i == 0)
        def _():
            for h in heads_here:
                state_scr[h] = jnp.zeros((HEAD_DIM, HEAD_DIM), F32)

        last = lax.fori_loop(0, cpb, chunk, tuple(state_scr[h] for h in heads_here))
        for p, h in enumerate(heads_here):
            cols = _head_cols(p)
            state_scr[h] = last[p]
            og = _gdn_outputs(st_ref[p], *later[p], z_ref[:, cols].reshape(cpb, CHUNK, HEAD_DIM), onw_ref[...])
            og_ref[:, cols] = og.reshape(tb, HEAD_DIM).astype(BF16)

    n_x = len(gather)
    grid = (nb, heads // PAIR)
    og, st, *gathered = pl.pallas_call(
        _with_exchange(body, 14, 2, True, n_x, grid),
        name=name,
        grid=grid,
        in_specs=[col(0), col(1), col(2), halo(0), halo(1), halo(2), col(3), abspec, convw(0), convw(1), convw(2), vec, vec, vec]
        + [_ANY] * n_x,
        out_specs=[pl.BlockSpec((tb, PAIR * HEAD_DIM), lambda i, h: (i, h)), states] + [_ANY] * n_x,
        out_shape=[jax.ShapeDtypeStruct((t, heads * HEAD_DIM), BF16),
                   jax.ShapeDtypeStruct((heads, t // CHUNK, HEAD_DIM, HEAD_DIM), F32)] + _chip_shapes(True, gather),
        scratch_shapes=[pltpu.VMEM((heads, HEAD_DIM, HEAD_DIM), F32), pltpu.VMEM((PAIR, 3, HALO + tb, HEAD_DIM), F32)]
        + _scan_scratch(cpb, BF16) * PAIR + (_chip_scratch(n_x) if n_x else []),
        compiler_params=_params("arbitrary", "arbitrary"),
    )(proj, proj, proj, proj, proj, proj, proj, ab, conv_w, conv_w, conv_w, alog_row, dtb_row, onw_row, *gather)
    return og, st, gathered


def _gdn_bwd(proj, ab, conv_w, alog_row, dtb_row, onw_row, states, dog, *, heads, name, tb=1024, pair=2, exchange=()):
    t = proj.shape[0]
    tb = min(tb, t)
    nb, cpb = t // tb, tb // CHUNK
    PAIR = pair
    _, col, halo, convw, vec, abspec, states_spec = _gdn_specs(heads, tb, True, nb, PAIR)
    n_conv = conv_w.shape[1]

    def body(q_ref, k_ref, v_ref, qh_ref, kh_ref, vh_ref, z_ref, ab_ref, wq_ref, wk_ref, wv_ref, alog_ref, dtb_ref, onw_ref,
             st_ref, dog_ref, dproj_ref, dab_ref, dconv_ref, dalog_ref, ddtb_ref, donw_ref,
             dstate_scr, x_scr, carry_scr, *scr):
        op_scr, dop_scr, dstates_scr, dc_scr = scr[:3 * PAIR], scr[3 * PAIR:6 * PAIR], scr[6 * PAIR:7 * PAIR], scr[7 * PAIR]
        i, pair = pl.program_id(0), pl.program_id(1)
        first_block = i == nb - 1
        heads_here, later = [pair * PAIR + p for p in range(PAIR)], []

        @pl.when(jnp.logical_and(i == 0, pair == 0))
        def _():
            dconv_ref[...] = jnp.zeros_like(dconv_ref)
            dalog_ref[...] = jnp.zeros_like(dalog_ref)
            ddtb_ref[...] = jnp.zeros_like(ddtb_ref)
            donw_ref[...] = jnp.zeros_like(donw_ref)

        @pl.when(pair == 0)
        def _():
            dab_ref[...] = jnp.zeros_like(dab_ref)

        @pl.when(i == 0)
        def _():
            for h in heads_here:
                dstate_scr[h] = jnp.zeros((HEAD_DIM, HEAD_DIM), F32)
                carry_scr[h] = jnp.zeros((3, HALO, HEAD_DIM), F32)

        abv = ab_ref[...]
        w_refs = (wq_ref, wk_ref, wv_ref)
        for p, h in enumerate(heads_here):
            cols = _head_cols(p)
            for n, (ref, href) in enumerate(((q_ref, qh_ref), (k_ref, kh_ref), (v_ref, vh_ref))):
                x_scr[p, n, 0:HALO, :] = jnp.where(first_block, 0.0, href[:, cols])
                x_scr[p, n, HALO:HALO + tb, :] = ref[:, cols]
            sel_a, sel_b = _head_lane(h), _head_lane(h, heads)
            alog, dtb = _pick(sel_a, alog_ref[...]), _pick(sel_a, dtb_ref[...])
            acts = [_silu(_conv(x_scr.at[p, n], w_ref[:, cols], tb)) for n, w_ref in enumerate(w_refs)]
            (*scan, read, out), vjp_intra = jax.vjp(_gdn_intra, *acts, _pick(sel_a, abv), _pick(sel_b, abv), alog, dtb)
            for s, val in zip(op_scr[3 * p:3 * p + 3], scan):
                s[...] = val.astype(s.dtype)
            blocked = lambda ref: ref[:, cols].reshape(cpb, CHUNK, HEAD_DIM)
            _, vjp_outputs = jax.vjp(_gdn_outputs, st_ref[p], read, out, blocked(z_ref), onw_ref[...])
            dstates_scr[p][...], dread, dout, dz, donw = vjp_outputs(blocked(dog_ref))
            dproj_ref[3, :, cols] = dz.reshape(tb, HEAD_DIM).astype(BF16)
            donw_ref[...] += donw
            later.append((vjp_intra, dread, dout, sel_a, sel_b))

        def chunk(i_rev, dstates):
            c = cpb - 1 - i_rev
            new = []
            for p in range(PAIR):
                _, vjp = jax.vjp(_gdn_scan_step, st_ref[p, c], *[s[c].astype(F32) for s in op_scr[3 * p:3 * p + 3]])
                dstate, *grads = vjp(dstates[p])
                for s, val in zip(dop_scr[3 * p:3 * p + 3], grads):
                    s[c] = val
                new.append(dstate + dstates_scr[p][c])
            return tuple(new)

        last = lax.fori_loop(0, cpb, chunk, tuple(dstate_scr[h] for h in heads_here))
        for p, h in enumerate(heads_here):
            cols = _head_cols(p)
            vjp_intra, dread, dout, sel_a, sel_b = later[p]
            dstate_scr[h] = last[p]
            *dacts, da, db, dalog, ddtb = vjp_intra((*[s[...] for s in dop_scr[3 * p:3 * p + 3]], dread, dout))
            dab_ref[...] += jnp.where(sel_a, da, 0.0) + jnp.where(sel_b, db, 0.0)
            for n, (dact, w_ref) in enumerate(zip(dacts, w_refs)):
                dx, dw = _conv_silu_bwd(x_scr.at[p, n], w_ref[:, cols], dact, dc_scr, tb)
                x_scr[p, n] = dx
                x_scr[p, n, tb:tb + HALO, :] += carry_scr[h, n]
                carry_scr[h, n] = x_scr[p, n, 0:HALO, :]
                dproj_ref[n, :, cols] = x_scr[p, n, HALO:HALO + tb, :].astype(BF16)
                lanes = pl.ds(pl.multiple_of((n * heads + h) * HEAD_DIM, HEAD_DIM), HEAD_DIM)
                for j in range(CONV_K):
                    dconv_ref[j:j + 1, lanes] += dw[j]
            dalog_ref[...] += jnp.where(sel_a, dalog, 0.0)
            ddtb_ref[...] += jnp.where(sel_a, ddtb, 0.0)

    dog_spec = pl.BlockSpec((tb, PAIR * HEAD_DIM), lambda i, h: (nb - 1 - i, h))
    dproj_spec = pl.BlockSpec((4, tb, PAIR * HEAD_DIM), lambda i, h: (0, nb - 1 - i, h))
    row_shape = jax.ShapeDtypeStruct((1, LANES), F32)
    n_x = len(exchange)
    grid = (nb, heads // PAIR)
    outs = pl.pallas_call(
        _with_exchange(body, 16, 6, False, n_x, grid),
        name=name,
        grid=grid,
        in_specs=[col(0), col(1), col(2), halo(0), halo(1), halo(2), col(3), abspec, convw(0), convw(1), convw(2), vec, vec, vec,
                  states_spec, dog_spec] + [_ANY] * n_x,
        out_specs=[dproj_spec, abspec, pl.BlockSpec((CONV_K, n_conv), lambda i, h: (0, 0)), vec, vec, vec] + [_ANY] * n_x,
        out_shape=[jax.ShapeDtypeStruct((4, t, heads * HEAD_DIM), BF16), jax.ShapeDtypeStruct((t, LANES), F32),
                   jax.ShapeDtypeStruct((CONV_K, n_conv), F32), row_shape, row_shape, row_shape] + _chip_shapes(False, exchange),
        scratch_shapes=[pltpu.VMEM((heads, HEAD_DIM, HEAD_DIM), F32), pltpu.VMEM((PAIR, 3, HALO + tb, HEAD_DIM), F32),
                        pltpu.VMEM((heads, 3, HALO, HEAD_DIM), F32)] + _scan_scratch(cpb, BF16) * PAIR
        + _scan_scratch(cpb, F32) * PAIR + [pltpu.VMEM((cpb, HEAD_DIM, HEAD_DIM), F32)] * PAIR
        + [pltpu.VMEM((HALO + tb + HALO, HEAD_DIM), F32)]
        + (_chip_scratch(n_x) if n_x else []),
        compiler_params=_params("arbitrary", "arbitrary", vmem=VMEM_LIMIT_WIDE_BYTES),
    )(proj, proj, proj, proj, proj, proj, proj, ab, conv_w, conv_w, conv_w, alog_row, dtb_row, onw_row, states, dog, *exchange)
    return (*outs[:6], outs[6:])


BAND = (LEFT_CHUNKS + 1) * CHUNK
PAD = LEFT_CHUNKS * CHUNK
GROUP = 2
ROWS = GROUP * CHUNK
WIN = (LEFT_CHUNKS + GROUP) * CHUNK
DIAGS = WIN + ROWS - 1
NEAR = PAD + ROWS - 1 - REL_CLIP
assert 0 < NEAR < DIAGS and WIN - PAD - 1 <= REL_CLIP and WIN % LANES == 0
ATTN_BLOCK = 1024


def _band_bias(rel_bias):
    heads = rel_bias.shape[0]
    far = jnp.broadcast_to(rel_bias[:, 2 * REL_CLIP:], (heads, NEAR + 1))
    near = rel_bias[:, 2 * REL_CLIP + NEAR + 1 - DIAGS:2 * REL_CLIP][:, ::-1]
    diag = jnp.concatenate([far, near], axis=1)
    return jnp.stack([diag[:, ROWS - 1 - r:ROWS - 1 - r + WIN] for r in range(ROWS)], axis=1)


def _band_bias_grad(dbias):
    heads = dbias.shape[0]
    diag = sum(jnp.pad(dbias[:, r, :], ((0, 0), (ROWS - 1 - r, r))) for r in range(ROWS))
    far = jnp.sum(diag[:, :NEAR + 1], axis=1, keepdims=True)
    near = diag[:, NEAR + 1:][:, ::-1]
    unused = jnp.zeros((heads, 2 * REL_CLIP - near.shape[1]), F32)
    return jnp.concatenate([unused, near, far], axis=1)


def _masked_bias(bias, n):
    r = np.arange(ROWS)[:, None]
    key = np.arange(WIN)[None, :]
    band_start = (r // CHUNK) * CHUNK
    in_band = np.logical_and(key >= band_start, key < band_start + BAND)
    in_sequence = key[None] >= PAD - np.arange(n)[:, None, None] * ROWS
    first = jnp.where(np.logical_and(in_band[None], in_sequence)[None], bias[:, None], -1e30)
    return first, jnp.where(in_band[None, None], bias[:, None], -1e30)


def _attn_groups(q_pre, z, kn, v, bias, qnw):
    q = _rms(q_pre, qnw)
    s = _bdot(q, kn, _BNT) * (HEAD_DIM ** -0.5) + bias
    p = jnp.exp(s - jnp.max(s, axis=-1, keepdims=True))
    p = p / jnp.sum(p, axis=-1, keepdims=True)
    return _bdot(p, v, _BNN) * _silu(z)


def _attn_groups_bwd(q_pre, z, kn, v, bias, qnw, dog):
    scale = HEAD_DIM ** -0.5
    inv_rms = lax.rsqrt(jnp.mean(q_pre * q_pre, axis=-1, keepdims=True) + EPS)
    q_hat = q_pre * inv_rms
    q_b = (q_hat * qnw).astype(BF16)
    s = _dot(q_b, kn, _BNT) * scale + bias
    e = jnp.exp(s - jnp.max(s, axis=-1, keepdims=True))
    p = e * (1.0 / jnp.sum(e, axis=-1, keepdims=True))
    p_b = p.astype(BF16)
    o = _dot(p_b, v, _BNN)
    sig = jax.nn.sigmoid(z)
    do = dog * (z * sig)
    dz = dog * o * (sig * (1.0 + z * (1.0 - sig)))
    do_b = do.astype(BF16)
    dv = _dot(p_b, do_b, _BTN)
    dp = _dot(do_b, v, _BNT)
    ds = p * (dp - jnp.sum(do * o, axis=-1, keepdims=True))
    ds_b = (ds * scale).astype(BF16)
    dq = _dot(ds_b, kn, _BNN)
    dkn = _dot(ds_b, q_b, _BTN)
    dqnw = jnp.sum(jnp.sum(dq * q_hat, axis=0), axis=0, keepdims=True)
    dq_hat = dq * qnw
    dq_pre = inv_rms * (dq_hat - q_hat * jnp.mean(dq_hat * q_hat, axis=-1, keepdims=True))
    return dq_pre, dz, dkn, dv, jnp.sum(ds, axis=0), dqnw


def _attn_specs(heads, tb, t):
    def col(group):
        return pl.BlockSpec((tb, HEAD_DIM), lambda h, i: (i, group * heads + h))

    def full(group):
        return pl.BlockSpec((t, HEAD_DIM), lambda h, i: (0, group * heads + h))

    bias = [pl.BlockSpec((1, tb // ROWS, ROWS, WIN), lambda h, i: (h, 0, 0, 0)),
            pl.BlockSpec((1, 1, ROWS, WIN), lambda h, i: (h, 0, 0, 0))]
    vec = pl.BlockSpec((1, HEAD_DIM), lambda h, i: (0, 0))
    return col, full, bias, vec


def _attn_windows(scr, block_start, n):
    return jnp.stack([scr[pl.ds(pl.multiple_of(block_start + g * ROWS, ROWS), WIN), :] for g in range(n)])


def _attn_fill(k_ref, v_ref, knw_ref, kn_scr, v_scr, t):
    kn_scr[0:PAD, :] = jnp.zeros((PAD, HEAD_DIM), BF16)
    v_scr[0:PAD, :] = jnp.zeros((PAD, HEAD_DIM), BF16)
    step = min(512, t)

    def fill(j, _):
        rows = pl.ds(pl.multiple_of(j * step, step), step)
        prows = pl.ds(pl.multiple_of(PAD + j * step, CHUNK), step)
        kn_scr[prows, :] = _rms(k_ref[rows, :], knw_ref[...]).astype(BF16)
        v_scr[prows, :] = v_ref[rows, :].astype(BF16)
        return 0

    lax.fori_loop(0, t // step, fill, 0)


def _attn_fwd(proj, bias, qnw_row, knw_row, *, heads, name, tb=ATTN_BLOCK):
    t = proj.shape[0]
    tb = min(tb, t)
    nb, ng = t // tb, tb // ROWS
    col, full, bias_spec, vec = _attn_specs(heads, tb, t)

    def body(q_ref, k_ref, v_ref, z_ref, first_ref, rest_ref, qnw_ref, knw_ref, og_ref, kn_scr, v_scr):
        i = pl.program_id(1)

        @pl.when(i == 0)
        def _():
            _attn_fill(k_ref, v_ref, knw_ref, kn_scr, v_scr, t)

        def run(block_bias):
            start = i * tb
            og = _attn_groups(q_ref[...].reshape(ng, ROWS, HEAD_DIM), z_ref[...].reshape(ng, ROWS, HEAD_DIM),
                              _attn_windows(kn_scr, start, ng), _attn_windows(v_scr, start, ng), block_bias, qnw_ref[...])
            og_ref[...] = og.reshape(tb, HEAD_DIM).astype(BF16)

        pl.when(i == 0)(lambda: run(first_ref[0]))
        pl.when(i > 0)(lambda: run(rest_ref[0]))

    return pl.pallas_call(
        body,
        name=name,
        grid=(heads, nb),
        in_specs=[col(0), full(1), full(2), col(3), *bias_spec, vec, vec],
        out_specs=pl.BlockSpec((tb, HEAD_DIM), lambda h, i: (i, h)),
        out_shape=jax.ShapeDtypeStruct((t, heads * HEAD_DIM), BF16),
        scratch_shapes=[pltpu.VMEM((PAD + t, HEAD_DIM), BF16), pltpu.VMEM((PAD + t, HEAD_DIM), BF16)],
        compiler_params=_params("arbitrary", "arbitrary"),
    )(proj, proj, proj, proj, *bias, qnw_row, knw_row)


def _attn_bwd(proj, bias, qnw_row, knw_row, dog, *, heads, name, tb=ATTN_BLOCK, sub=4):
    t = proj.shape[0]
    tb = min(tb, t)
    nb, ng = t // tb, tb // ROWS
    sub = min(sub, ng)
    col, full, bias_spec, vec = _attn_specs(heads, tb, t)

    def body(q_ref, k_ref, v_ref, z_ref, first_ref, rest_ref, qnw_ref, knw_ref, dog_ref,
             dqz_ref, dkv_ref, dbias_ref, dqnw_ref, dknw_ref, kn_scr, v_scr, dkn_scr, dv_scr):
        i = pl.program_id(1)

        @pl.when(i == 0)
        def _():
            _attn_fill(k_ref, v_ref, knw_ref, kn_scr, v_scr, t)
            dkn_scr[...] = jnp.zeros_like(dkn_scr)
            dv_scr[...] = jnp.zeros_like(dv_scr)
            dbias_ref[...] = jnp.zeros_like(dbias_ref)
            dqnw_ref[...] = jnp.zeros_like(dqnw_ref)

        def run(block_bias):
            for g0 in range(0, ng, sub):
                rows = pl.ds(g0 * ROWS, sub * ROWS)
                at = i * tb + g0 * ROWS
                blocked = lambda ref: ref[rows, :].reshape(sub, ROWS, HEAD_DIM)
                dq, dz, dkn, dv, dbias, dqnw = _attn_groups_bwd(
                    blocked(q_ref), blocked(z_ref), _attn_windows(kn_scr, at, sub), _attn_windows(v_scr, at, sub),
                    block_bias(g0), qnw_ref[...], blocked(dog_ref))
                dqz_ref[0, rows, :] = dq.reshape(sub * ROWS, HEAD_DIM).astype(BF16)
                dqz_ref[1, rows, :] = dz.reshape(sub * ROWS, HEAD_DIM).astype(BF16)
                for g in range(sub):
                    window = pl.ds(pl.multiple_of(at + g * ROWS, ROWS), WIN)
                    dkn_scr[window, :] += dkn[g]
                    dv_scr[window, :] += dv[g]
                dbias_ref[0] += dbias
                dqnw_ref[0] += dqnw

        pl.when(i == 0)(lambda: run(lambda g0: first_ref[0, g0:g0 + sub]))
        pl.when(i > 0)(lambda: run(lambda g0: rest_ref[0]))

        @pl.when(i == nb - 1)
        def _():
            step = min(512, t)

            def finish(j, dknw):
                rows = pl.ds(pl.multiple_of(j * step, step), step)
                prows = pl.ds(pl.multiple_of(PAD + j * step, CHUNK), step)
                _, vjp = jax.vjp(_rms, k_ref[rows, :], knw_ref[...])
                dk, dw = vjp(dkn_scr[prows, :])
                dkv_ref[0, rows, :] = dk.astype(BF16)
                dkv_ref[1, rows, :] = dv_scr[prows, :].astype(BF16)
                return dknw + dw

            dknw_ref[0] = lax.fori_loop(0, t // step, finish, jnp.zeros((1, HEAD_DIM), F32))

    pair_col = pl.BlockSpec((2, tb, HEAD_DIM), lambda h, i: (0, i, h))
    pair_full = pl.BlockSpec((2, t, HEAD_DIM), lambda h, i: (0, 0, h))
    head_vec = pl.BlockSpec((1, 1, HEAD_DIM), lambda h, i: (h, 0, 0))
    pair_shape = jax.ShapeDtypeStruct((2, t, heads * HEAD_DIM), BF16)
    vec_shape = jax.ShapeDtypeStruct((heads, 1, HEAD_DIM), F32)
    return pl.pallas_call(
        body,
        name=name,
        grid=(heads, nb),
        in_specs=[col(0), full(1), full(2), col(3), *bias_spec, vec, vec, pl.BlockSpec((tb, HEAD_DIM), lambda h, i: (i, h))],
        out_specs=[pair_col, pair_full, pl.BlockSpec((1, ROWS, WIN), lambda h, i: (h, 0, 0)), head_vec, head_vec],
        out_shape=[pair_shape, pair_shape, jax.ShapeDtypeStruct((heads, ROWS, WIN), F32), vec_shape, vec_shape],
        scratch_shapes=[pltpu.VMEM((PAD + t, HEAD_DIM), BF16), pltpu.VMEM((PAD + t, HEAD_DIM), BF16),
                        pltpu.VMEM((PAD + t, HEAD_DIM), F32), pltpu.VMEM((PAD + t, HEAD_DIM), F32)],
        compiler_params=_params("arbitrary", "arbitrary"),
    )(proj, proj, proj, proj, *bias, qnw_row, knw_row, dog)


def _lane_row(v):
    v = v.reshape(1, -1)
    return jnp.pad(v, ((0, 0), (0, LANES - v.shape[1])))


def _local_step(x, target, norm_w, wa_in, conv_w, a_log, dt_bias, onw, wa_out, wb_in, qnw, knw, rel_bias, wb_out, *,
                sharded=False):
    ha, hb = a_log.shape[-1], rel_bias.shape[-2]
    na = 4 * ha * HEAD_DIM
    wa_main = wa_in[:, :na]
    wa_ab = jnp.pad(wa_in[:, na:], ((0, 0), (0, LANES - 2 * ha)))
    alog_row, dtb_row, onw_row = _lane_row(a_log), _lane_row(dt_bias), _lane_row(onw)
    qnw_row, knw_row = _lane_row(qnw), _lane_row(knw)
    bias = _masked_bias(_band_bias(rel_bias.reshape(hb, -1)), min(ATTN_BLOCK, x.shape[0]) // ROWS)

    hn0 = _rmsnorm_fwd(x, norm_w[0:1], name="norm0")
    proj_a = _matmul(hn0, wa_main, name="a_in")
    ab_a = _matmul(hn0, wa_ab, name="a_in_ab")
    og_a, states, got = _gdn_fwd(proj_a, ab_a, conv_w, alog_row, dtb_row, onw_row, heads=ha, name="gdn_fwd",
                                 gather=[wb_in, wa_out, wb_out] if sharded else [])
    if sharded:
        wb_in, wa_out, wb_out = _join_cols(got[0]), got[1].reshape(-1, got[1].shape[-1]), got[2].reshape(-1, got[2].shape[-1])
    h1 = _matmul(og_a, wa_out, residual=x, name="a_out")
    hn1 = _rmsnorm_fwd(h1, norm_w[1:2], name="norm1")
    proj_b = _matmul(hn1, wb_in, name="b_in")
    og_b = _attn_fwd(proj_b, bias, qnw_row, knw_row, heads=hb, name="attn_fwd")
    h2 = _matmul(og_b, wb_out, residual=h1, name="b_out")
    loss, dh2, dh2_b = _loss_head(h2, target, name="loss_head")

    grad_dtype = BF16 if sharded else F32
    dog_b = _matmul(dh2_b, wb_out, trans_b=True, name="d_b_out_x")
    dwb_out = _matmul(og_b, dh2_b, trans_a=True, out_dtype=grad_dtype, name="d_b_out_w")
    dqz, dkv, dbias, dqnw, dknw = _attn_bwd(proj_b, bias, qnw_row, knw_row, dog_b, heads=hb, name="attn_bwd")
    dproj_b, qkvz = [dqz, dkv], (0, 3, 1, 2)
    dhn1 = _matmul(dproj_b, wb_in, trans_b=True, order=qkvz, name="d_b_in_x")
    dwb_in = _matmul(hn1, dproj_b, trans_a=True, order=qkvz, out_dtype=grad_dtype, col_slabs=N_CHIPS if sharded else 0,
                     name="d_b_in_w")
    dh1, dh1_b, dnw1 = _rmsnorm_bwd(h1, norm_w[1:2], dhn1, dh2, name="d_norm1")

    dog_a = _matmul(dh1_b, wa_out, trans_b=True, name="d_a_out_x")
    dwa_out = _matmul(og_a, dh1_b, trans_a=True, out_dtype=grad_dtype, name="d_a_out_w")
    early = [dwb_in, _split_rows(dwa_out), _split_rows(dwb_out)] if sharded else []
    dproj_a, dab, dconv, dalog, ddtb, donw, landed = _gdn_bwd(
        proj_a, ab_a, conv_w, alog_row, dtb_row, onw_row, states, dog_a, heads=ha, name="gdn_bwd", exchange=early)
    if sharded:
        dwb_in, dwa_out, dwb_out = landed
    dab_b = dab.astype(BF16)
    dwa_in = jnp.concatenate(
        [_matmul(hn0, dproj_a, trans_a=True, out_dtype=grad_dtype, name="d_a_in_w"),
         _matmul(hn0, dab_b, trans_a=True, out_dtype=grad_dtype, name="d_a_in_ab_w")[:, :2 * ha]], axis=1)
    if sharded:
        dhn0, (dwa_in, dconv) = _matmul(dproj_a, wa_main, trans_b=True, name="d_a_in_x",
                                        exchange=[_split_cols(dwa_in), _split_cols(dconv)])
    else:
        dhn0 = _matmul(dproj_a, wa_main, trans_b=True, name="d_a_in_x")
    dhn0 = _matmul(dab_b, wa_ab, trans_b=True, residual=dhn0, name="d_a_in_ab_x")
    dx, _, dnw0 = _rmsnorm_bwd(x, norm_w[0:1], dhn0, dh1, name="d_norm0")

    drel = _band_bias_grad(dbias)
    grads = dict(
        norm_w=jnp.concatenate([dnw0, dnw1], axis=0), a_w_in=dwa_in, a_conv_w=dconv, a_a_log=dalog[:, :ha],
        a_dt_bias=ddtb[:, :ha], a_out_norm_w=donw, a_w_out=dwa_out, b_w_in=dwb_in, b_q_norm_w=jnp.sum(dqnw, axis=0),
        b_k_norm_w=jnp.sum(dknw, axis=0), b_rel_bias=drel[None], b_w_out=dwb_out)
    return loss, dx, grads


_ANY = pl.BlockSpec(memory_space=pl.ANY)
_CHIP_FLIPS = ((1, 0), (0, 1), (1, 1))


def _place():
    x, y, c = lax.axis_index("x"), lax.axis_index("y"), lax.axis_index("c")
    return x, y, c


def _flip(v, bit):
    return 1 - v if bit else v


def _remote(src, dst, send_sem, recv_sem, peer):
    return pltpu.make_async_remote_copy(src_ref=src, dst_ref=dst, send_sem=send_sem, recv_sem=recv_sem, device_id=peer,
                                        device_id_type=MESH)


def _comm_call(body, arrays, out_shapes, n_remote, n_local, name):
    scratch = [pltpu.SemaphoreType.DMA((n_remote,)), pltpu.SemaphoreType.DMA((n_remote,))]
    if n_local:
        scratch.append(pltpu.SemaphoreType.DMA((n_local,)))
    return pl.pallas_call(
        body, name=name, in_specs=[_ANY] * len(arrays), out_specs=[_ANY] * len(out_shapes), out_shape=out_shapes,
        scratch_shapes=scratch)(*arrays)


def _chip_scratch(n):
    return [pltpu.SemaphoreType.DMA((3 * n,)), pltpu.SemaphoreType.DMA((3 * n,)), pltpu.SemaphoreType.DMA((n,))]


def _chip_shapes(gather, arrays):
    return [jax.ShapeDtypeStruct(((N_CHIPS,) + s.shape) if gather else s.shape, s.dtype) for s in arrays]


def _chip_traffic(gather, ins, outs, sems):
    send_sems, recv_sems, local_sems = sems
    x, y, c = _place()
    mine = 2 * x + y
    local, remote, landing = [], [], []
    for a in range(len(ins)):
        local.append(pltpu.make_async_copy(ins[a] if gather else ins[a].at[mine], outs[a].at[mine], local_sems.at[a]))
        for k, (fx, fy) in enumerate(_CHIP_FLIPS):
            peer = (_flip(x, fx), _flip(y, fy), c)
            theirs = 2 * peer[0] + peer[1]
            src = ins[a] if gather else ins[a].at[theirs]
            pair = send_sems.at[3 * a + k], recv_sems.at[3 * a + k]
            remote.append(_remote(src, outs[a].at[mine], *pair, peer))
            landing.append(_remote(src, outs[a].at[theirs], *pair, peer))
    return local + remote, (local, landing, remote)


def _start(traffic):
    for cp in traffic[0]:
        cp.start()


def _finish(traffic):
    local, landing, remote = traffic[1]
    for cp in local:
        cp.wait()
    for cp in landing:
        cp.wait_recv()
    for cp in remote:
        cp.wait_send()


def _with_exchange(compute, n_in, n_out, gather, n_x, grid):
    if not n_x:
        return compute

    def body(*refs):
        ins, x_in = refs[:n_in], refs[n_in:n_in + n_x]
        outs, x_out = refs[n_in + n_x:n_in + n_x + n_out], refs[n_in + n_x + n_out:n_in + 2 * n_x + n_out]
        scratch, sems = refs[n_in + 2 * n_x + n_out:-3], refs[-3:]
        traffic = _chip_traffic(gather, x_in, x_out, sems)
        first = functools.reduce(jnp.logical_and, [pl.program_id(d) == 0 for d in range(len(grid))])
        last = functools.reduce(jnp.logical_and, [pl.program_id(d) == grid[d] - 1 for d in range(len(grid))])

        @pl.when(first)
        def _():
            _start(traffic)

        compute(*ins, *outs, *scratch)

        @pl.when(last)
        def _():
            _finish(traffic)

    return body


def _chip_call(gather, arrays, *, name):
    n = len(arrays)

    def body(*refs):
        traffic = _chip_traffic(gather, refs[:n], refs[n:2 * n], refs[2 * n:])
        _start(traffic)
        _finish(traffic)

    return pl.pallas_call(
        body, name=name, in_specs=[_ANY] * n, out_specs=[_ANY] * n, out_shape=_chip_shapes(gather, arrays),
        scratch_shapes=_chip_scratch(n))(*arrays)


def _gather_shared(shard, small, *, name):
    rows = shard.shape[0]
    assert rows % 2 == 0
    half = rows // 2

    def body(shard_ref, small_ref, out_ref, small_out_ref, send_sems, recv_sems, local_sems):
        x, y, c = _place()
        mine = 2 * x + y
        sibling = (x, y, 1 - c)
        my_rows = pl.ds(pl.multiple_of(c * half, 8), half)
        local = [pltpu.make_async_copy(shard_ref, out_ref.at[mine], local_sems.at[0]),
                 pltpu.make_async_copy(small_ref, small_out_ref.at[mine], local_sems.at[1])]
        sent, landed, passed_on, handed = [], [], [], []
        for k, (fx, fy) in enumerate(_CHIP_FLIPS):
            peer = (_flip(x, fx), _flip(y, fy), c)
            theirs = 2 * peer[0] + peer[1]
            ici, d2d, tiny = [(send_sems.at[3 * n + k], recv_sems.at[3 * n + k]) for n in range(3)]
            sent.append(_remote(shard_ref.at[my_rows], out_ref.at[mine, my_rows], *ici, peer))
            landed.append(_remote(shard_ref.at[my_rows], out_ref.at[theirs, my_rows], *ici, peer))
            sent.append(_remote(small_ref, small_out_ref.at[mine], *tiny, peer))
            landed.append(_remote(small_ref, small_out_ref.at[theirs], *tiny, peer))
            passed_on.append(_remote(out_ref.at[theirs, my_rows], out_ref.at[theirs, my_rows], *d2d, sibling))
            other_rows = pl.ds(pl.multiple_of((1 - c) * half, 8), half)
            handed.append(_remote(out_ref.at[theirs, other_rows], out_ref.at[theirs, other_rows], *d2d, sibling))
        for cp in local + sent:
            cp.start()
        for k in range(3):
            landed[2 * k].wait_recv()
            passed_on[k].start()
        for k in range(3):
            landed[2 * k + 1].wait_recv()
            handed[k].wait_recv()
        for cp in local:
            cp.wait()
        for cp in sent + passed_on:
            cp.wait_send()

    return pl.pallas_call(
        body, name=name, in_specs=[_ANY] * 2, out_specs=[_ANY] * 2, out_shape=_chip_shapes(True, [shard, small]),
        scratch_shapes=[pltpu.SemaphoreType.DMA((9,)), pltpu.SemaphoreType.DMA((9,)), pltpu.SemaphoreType.DMA((2,))],
    )(shard, small)


def _swap_pair(arrays, *, name):
    n = len(arrays)

    def body(*refs):
        ins, outs, (send_sems, recv_sems) = refs[:n], refs[n:2 * n], refs[2 * n:]
        x, y, c = _place()
        copies = [_remote(ins[a], outs[a], send_sems.at[a], recv_sems.at[a], (x, y, 1 - c)) for a in range(n)]
        for cp in copies:
            cp.start()
        for cp in copies:
            cp.wait_recv()
        for cp in copies:
            cp.wait_send()

    shapes = [jax.ShapeDtypeStruct(s.shape, s.dtype) for s in arrays]
    return _comm_call(body, arrays, shapes, n, 0, name)


def _gather_all(tile, *, name):
    def body(in_ref, out_ref, send_sems, recv_sems, local_sems):
        x, y, c = _place()
        mine = 4 * x + 2 * y + c
        local = pltpu.make_async_copy(in_ref, out_ref.at[mine], local_sems.at[0])
        remote, landing = [], []
        for k in range(1, N_DEV):
            peer = (_flip(x, k & 4), _flip(y, k & 2), _flip(c, k & 1))
            sems = send_sems.at[k - 1], recv_sems.at[k - 1]
            remote.append(_remote(in_ref, out_ref.at[mine], *sems, peer))
            landing.append(_remote(in_ref, out_ref.at[4 * peer[0] + 2 * peer[1] + peer[2]], *sems, peer))
        for cp in [local] + remote:
            cp.start()
        local.wait()
        for cp in landing:
            cp.wait_recv()
        for cp in remote:
            cp.wait_send()

    return _comm_call(body, [tile], [jax.ShapeDtypeStruct((N_DEV,) + tile.shape, tile.dtype)], N_DEV - 1, 1, name)[0]


def _sum_slots(slabs, *, name, tr=128):
    s, r, c = slabs.shape
    tr = min(tr, r)

    def body(in_ref, o_ref):
        acc = in_ref[0].astype(F32)
        for j in range(1, s):
            acc = acc + in_ref[j].astype(F32)
        o_ref[...] = acc

    return pl.pallas_call(
        body, name=name, grid=(r // tr,),
        in_specs=[pl.BlockSpec((s, tr, c), lambda i: (0, i, 0))], out_specs=pl.BlockSpec((tr, c), lambda i: (i, 0)),
        out_shape=jax.ShapeDtypeStruct((r, c), F32), compiler_params=_params("parallel"))(slabs)


def _adamw_math(w, g, m, v):
    m = ADAM_B1 * m + (1.0 - ADAM_B1) * g
    v = ADAM_B2 * v + (1.0 - ADAM_B2) * (g * g)
    m_hat = m / (1.0 - ADAM_B1 ** ADAM_STEP)
    v_hat = v / (1.0 - ADAM_B2 ** ADAM_STEP)
    delta = -ADAM_LR * (m_hat / (jnp.sqrt(v_hat) + ADAM_EPS) + ADAM_WD * w)
    return delta, m, v


def _adamw(w, m, v, parts, *, name, tr=128):
    r, c = w.shape
    tr = min(tr, r)
    s = len(parts)

    def body(w_ref, m_ref, v_ref, *refs):
        g_ref, d_ref, nm_ref, nv_ref = refs[s:]
        g = refs[0][...]
        for p_ref in refs[1:s]:
            g = g + p_ref[...]
        g_ref[...] = g
        d_ref[...], nm_ref[...], nv_ref[...] = _adamw_math(w_ref[...], g, m_ref[...], v_ref[...])

    blk = pl.BlockSpec((tr, c), lambda i: (i, 0))
    shape = jax.ShapeDtypeStruct((r, c), F32)
    return pl.pallas_call(
        body, name=name, grid=(r // tr,), in_specs=[blk] * (3 + s), out_specs=[blk] * 4, out_shape=[shape] * 4,
        compiler_params=_params("parallel"))(w, m, v, *parts)


_BIG = ("a_w_in", "b_w_in", "a_w_out", "b_w_out", "a_conv_w")
_SMALL = ("norm_w", "a_a_log", "a_dt_bias", "a_out_norm_w", "b_q_norm_w", "b_k_norm_w", "b_rel_bias")
_ORDER = ("norm_w", "a_w_in", "a_conv_w", "a_a_log", "a_dt_bias", "a_out_norm_w", "a_w_out", "b_w_in", "b_q_norm_w",
          "b_k_norm_w", "b_rel_bias", "b_w_out")


def _join_cols(g):
    return jnp.transpose(g, (1, 0, 2)).reshape(g.shape[1], -1)


def _split_cols(g):
    return jnp.transpose(g.reshape(g.shape[0], N_CHIPS, -1), (1, 0, 2))


def _split_rows(g):
    return g.reshape(N_CHIPS, -1, g.shape[-1])


def _pack(d):
    flat = jnp.concatenate([d[n].reshape(-1) for n in _SMALL])
    return jnp.pad(flat, (0, -flat.shape[0] % LANES)).reshape(1, -1)


def _unpack(row, like):
    out, at = {}, 0
    for n in _SMALL:
        size = like[n].size
        out[n] = row[0, at:at + size].reshape(like[n].shape)
        at += size
    return out


def kernel(x, norm_w, a_w_in, a_conv_w, a_a_log, a_dt_bias, a_out_norm_w, a_w_out, b_w_in, b_q_norm_w, b_k_norm_w, b_rel_bias, b_w_out, loss_target, m_norm_w, m_a_w_in, m_a_conv_w, m_a_a_log, m_a_dt_bias, m_a_out_norm_w, m_a_w_out, m_b_w_in, m_b_q_norm_w, m_b_k_norm_w, m_b_rel_bias, m_b_w_out, v_norm_w, v_a_w_in, v_a_conv_w, v_a_a_log, v_a_dt_bias, v_a_out_norm_w, v_a_w_out, v_b_w_in, v_b_q_norm_w, v_b_k_norm_w, v_b_rel_bias, v_b_w_out):
    w = dict(norm_w=norm_w, a_w_in=a_w_in, a_conv_w=a_conv_w, a_a_log=a_a_log, a_dt_bias=a_dt_bias,
             a_out_norm_w=a_out_norm_w, a_w_out=a_w_out, b_w_in=b_w_in, b_q_norm_w=b_q_norm_w, b_k_norm_w=b_k_norm_w,
             b_rel_bias=b_rel_bias, b_w_out=b_w_out)
    m = dict(norm_w=m_norm_w, a_w_in=m_a_w_in, a_conv_w=m_a_conv_w, a_a_log=m_a_a_log, a_dt_bias=m_a_dt_bias,
             a_out_norm_w=m_a_out_norm_w, a_w_out=m_a_w_out, b_w_in=m_b_w_in, b_q_norm_w=m_b_q_norm_w,
             b_k_norm_w=m_b_k_norm_w, b_rel_bias=m_b_rel_bias, b_w_out=m_b_w_out)
    v = dict(norm_w=v_norm_w, a_w_in=v_a_w_in, a_conv_w=v_a_conv_w, a_a_log=v_a_a_log, a_dt_bias=v_a_dt_bias,
             a_out_norm_w=v_a_out_norm_w, a_w_out=v_a_w_out, b_w_in=v_b_w_in, b_q_norm_w=v_b_q_norm_w,
             b_k_norm_w=v_b_k_norm_w, b_rel_bias=v_b_rel_bias, b_w_out=v_b_w_out)

    wa_in, conv = _gather_shared(a_w_in[0].astype(BF16), a_conv_w[0], name="gather_a_in")
    loss, dx, grads = _local_step(
        x[0], loss_target[0], norm_w, _join_cols(wa_in), _join_cols(conv), a_a_log, a_dt_bias, a_out_norm_w,
        a_w_out[0].astype(BF16), b_w_in[0].astype(BF16), b_q_norm_w, b_k_norm_w, b_rel_bias, b_w_out[0].astype(BF16),
        sharded=True)
    loss = lax.psum(loss, ("x", "y", "c"))

    mine = [_sum_slots(grads[n], name=f"chip_sum_{n}") for n in _BIG]
    theirs = _swap_pair(mine, name="pair_grads")
    out = {}
    for n, p, q in zip(_BIG, mine, theirs):
        out[n] = [r[None] for r in _adamw(w[n][0], m[n][0], v[n][0], [p, q], name=f"adamw_{n}")]

    row = _pack(grads)
    tiles = _gather_all(jnp.broadcast_to(row, (8, row.shape[1])), name="gather_small_grads")
    res = _adamw(_pack(w), _pack(m), _pack(v), [tiles[d, 0:1, :] for d in range(N_DEV)], name="adamw_small")
    unpacked = [_unpack(r, w) for r in res]
    for n in _SMALL:
        out[n] = [u[n] for u in unpacked]

    return (loss, dx[None], *[out[n][0] for n in _ORDER], *[out[n][1] for n in _ORDER], *[out[n][2] for n in _ORDER],
            *[out[n][3] for n in _ORDER])
```

```python
import functools

import numpy as np
import jax
import jax.numpy as jnp
from jax import lax
from jax.experimental import pallas as pl
from jax.experimental.pallas import tpu as pltpu

F32 = jnp.float32
BF16 = jnp.bfloat16

CHUNK = 64
HEAD_DIM = 128
LEFT_CHUNKS = 8
REL_CLIP = 256
CONV_K = 4
EPS = 1e-6
HALO = 8

ADAM_LR = 0.001
ADAM_B1 = 0.9
ADAM_B2 = 0.999
ADAM_EPS = 1e-08
ADAM_WD = 0.01
ADAM_STEP = 10

LANES = 128
N_CHIPS = 4
N_DEV = 8
VMEM_LIMIT_BYTES = 56 * 1024 * 1024
VMEM_LIMIT_WIDE_BYTES = 63 * 1024 * 1024
MESH = pl.DeviceIdType.MESH


def _params(*sem, vmem=VMEM_LIMIT_BYTES):
    return pltpu.CompilerParams(dimension_semantics=sem, vmem_limit_bytes=vmem)


def _dot(a, b, dims=(((1,), (0,)), ((), ())), precision=None):
    return lax.dot_general(a, b, dims, precision=precision, preferred_element_type=F32)


_NT = (((1,), (1,)), ((), ()))
_TN = (((0,), (0,)), ((), ()))


def _bdot(a, b, dims=(((1,), (0,)), ((), ()))):
    return _dot(a.astype(BF16), b.astype(BF16), dims)


def _fdot(a, b, dims=(((1,), (0,)), ((), ()))):
    return _dot(a, b, dims, precision=lax.Precision.HIGH)


def _silu(x):
    return x * jax.nn.sigmoid(x)


def _stacks(x):
    if not isinstance(x, (list, tuple)) and x.ndim != 3:
        return None
    arrays = list(x) if isinstance(x, (list, tuple)) else [x]
    assert len({(v.shape[1], v.shape[2], v.dtype) for v in arrays}) == 1
    starts = [sum(v.shape[0] for v in arrays[:r]) for r in range(len(arrays))]
    return arrays, starts, starts[-1] + arrays[-1].shape[0]


def _static_pick(table, index):
    out = table[-1]
    for s in range(len(table) - 2, -1, -1):
        out = jnp.where(index == s, table[s], out)
    return out


def _matmul(a, b, *, name, trans_a=False, trans_b=False, residual=None, out_dtype=F32, tm=1024, tn=1024, tk=2048,
            col_slabs=0, order=None, exchange=()):
    assert not (trans_a and trans_b)
    a_stack, b_stack = _stacks(a), _stacks(b)
    assert not (a_stack and (trans_a or b_stack)) and not (b_stack and trans_b)
    a_list, b_list = (a_stack[0] if a_stack else [a]), (b_stack[0] if b_stack else [b])
    a0, b0 = a_list[0], b_list[0]
    k, m = (a_stack[2] * a0.shape[2], a0.shape[1]) if a_stack else a.shape if trans_a else a.shape[::-1]
    n = b_stack[2] * b0.shape[2] if b_stack else b.shape[0] if trans_b else b.shape[1]
    tm, tn, tk = min(tm, m), min(tn, n // max(col_slabs, 1)), min(tk, k)
    if a_stack:
        tk = min(tk, a0.shape[2])
        per_k = a0.shape[2] // tk
    if b_stack:
        tn = min(tn, b0.shape[2])
        per_n = b0.shape[2] // tn
    assert m % tm == 0 and n % tn == 0 and k % tk == 0, (a0.shape, b0.shape, tm, tn, tk)
    nk = k // tk
    dims = _NT if trans_b else _TN if trans_a else (((1,), (0,)), ((), ()))
    order = list(order) if order is not None else list(range(max(a_stack[2] if a_stack else 0, b_stack[2] if b_stack else 0)))
    na, nb = len(a_list), len(b_list)

    def group_of(r, stack, position):
        arrays, starts, _ = stack
        local = position - starts[r]
        return jnp.logical_and(local >= 0, local < arrays[r].shape[0]), jnp.clip(local, 0, arrays[r].shape[0] - 1)

    def body(*refs):
        a_refs, b_refs = refs[:na], refs[na:na + nb]
        r_ref = refs[na + nb] if residual is not None else None
        o_ref, acc_ref = refs[-2:]
        j, kk = pl.program_id(1), pl.program_id(2)

        @pl.when(kk == 0)
        def _():
            acc_ref[...] = jnp.zeros_like(acc_ref)

        for ra, a_ref in enumerate(a_refs):
            for rb, b_ref in enumerate(b_refs):
                def add(a_ref=a_ref, b_ref=b_ref):
                    acc_ref[...] += _dot(a_ref[...], b_ref[...], dims)

                if na > 1:
                    pl.when(group_of(ra, a_stack, kk // per_k)[0])(add)
                elif nb > 1:
                    pl.when(group_of(rb, b_stack, j // per_n)[0])(add)
                else:
                    add()

        @pl.when(kk == nk - 1)
        def _():
            r = acc_ref[...]
            if r_ref is not None:
                r = r + r_ref[...]
            o_ref[...] = r.astype(o_ref.dtype)

    if a_stack:
        a_specs = [pl.BlockSpec((None, tm, tk), lambda i, j, kk, r=r: (group_of(r, a_stack, kk // per_k)[1], i, kk % per_k))
                   for r in range(na)]
        b_k = lambda kk: _static_pick(order, kk // per_k) * per_k + kk % per_k
    else:
        a_specs = [pl.BlockSpec((tk, tm), lambda i, j, kk: (kk, i)) if trans_a else pl.BlockSpec((tm, tk), lambda i, j, kk: (i, kk))]
        b_k = lambda kk: kk
    if b_stack:
        b_specs = [pl.BlockSpec((None, tk, tn), lambda i, j, kk, r=r: (group_of(r, b_stack, j // per_n)[1], kk, j % per_n))
                   for r in range(nb)]
        out_col = lambda j: _static_pick(order, j // per_n) * per_n + j % per_n
    else:
        b_specs = [pl.BlockSpec((tn, tk), lambda i, j, kk: (j, b_k(kk))) if trans_b
                   else pl.BlockSpec((tk, tn), lambda i, j, kk: (b_k(kk), j))]
        out_col = lambda j: j
    in_specs = a_specs + b_specs
    args = a_list + b_list
    if residual is not None:
        in_specs.append(pl.BlockSpec((tm, tn), lambda i, j, kk: (i, j)))
        args.append(residual)
    grid = (m // tm, n // tn, nk)
    n_x = len(exchange)
    if col_slabs:
        per = n // col_slabs // tn
        assert per * tn * col_slabs == n, (n, tn, col_slabs)
        out_spec = pl.BlockSpec((None, tm, tn), lambda i, j, kk: (out_col(j) // per, i, out_col(j) % per))
        out_shape = jax.ShapeDtypeStruct((col_slabs, m, n // col_slabs), out_dtype)
    else:
        out_spec = pl.BlockSpec((tm, tn), lambda i, j, kk: (i, out_col(j)))
        out_shape = jax.ShapeDtypeStruct((m, n), out_dtype)
    out, *landed = pl.pallas_call(
        _with_exchange(body, len(args), 1, False, n_x, grid),
        name=name,
        grid=grid,
        in_specs=in_specs + [_ANY] * n_x,
        out_specs=[out_spec] + [_ANY] * n_x,
        out_shape=[out_shape] + _chip_shapes(False, exchange),
        scratch_shapes=[pltpu.VMEM((tm, tn), F32)] + (_chip_scratch(n_x) if n_x else []),
        compiler_params=_params(*(("arbitrary",) * 3 if n_x else ("parallel", "parallel", "arbitrary"))),
    )(*args, *exchange)
    return (out, landed) if n_x else out


def _rms(x, w):
    return x * lax.rsqrt(jnp.mean(x * x, axis=-1, keepdims=True) + EPS) * w


def _rmsnorm_fwd(x, w_row, *, name, tr=512):
    t, d = x.shape
    tr = min(tr, t)

    def body(x_ref, w_ref, o_ref):
        o_ref[...] = _rms(x_ref[...], w_ref[...]).astype(BF16)

    return pl.pallas_call(
        body,
        name=name,
        grid=(t // tr,),
        in_specs=[pl.BlockSpec((tr, d), lambda i: (i, 0)), pl.BlockSpec((1, d), lambda i: (0, 0))],
        out_specs=pl.BlockSpec((tr, d), lambda i: (i, 0)),
        out_shape=jax.ShapeDtypeStruct((t, d), BF16),
        compiler_params=_params("parallel"),
    )(x, w_row)


def _rmsnorm_bwd(x, w_row, dy, dres, *, name, tr=256):
    t, d = x.shape
    tr = min(tr, t)

    def body(x_ref, w_ref, dy_ref, dres_ref, dx_ref, dxb_ref, dw_ref):
        @pl.when(pl.program_id(0) == 0)
        def _():
            dw_ref[...] = jnp.zeros_like(dw_ref)

        _, vjp = jax.vjp(_rms, x_ref[...], w_ref[...])
        dx, dw = vjp(dy_ref[...])
        dx = dx + dres_ref[...]
        dx_ref[...] = dx
        dxb_ref[...] = dx.astype(BF16)
        dw_ref[...] += dw

    row = pl.BlockSpec((tr, d), lambda i: (i, 0))
    vec = pl.BlockSpec((1, d), lambda i: (0, 0))
    return pl.pallas_call(
        body,
        name=name,
        grid=(t // tr,),
        in_specs=[row, vec, row, row],
        out_specs=[row, row, vec],
        out_shape=[jax.ShapeDtypeStruct((t, d), F32), jax.ShapeDtypeStruct((t, d), BF16), jax.ShapeDtypeStruct((1, d), F32)],
        compiler_params=_params("arbitrary"),
    )(x, w_row, dy, dres)


def _loss_head(h, target, *, name, tr=512):
    t, d = h.shape
    tr = min(tr, t)

    def body(h_ref, t_ref, dh_ref, dhb_ref, part_ref):
        @pl.when(pl.program_id(0) == 0)
        def _():
            part_ref[...] = jnp.zeros_like(part_ref)

        err = h_ref[...] - t_ref[...]
        dh = err * (1.0 / d)
        dh_ref[...] = dh
        dhb_ref[...] = dh.astype(BF16)
        part_ref[...] += jnp.sum(err * err, axis=0, keepdims=True)

    row = pl.BlockSpec((tr, d), lambda i: (i, 0))
    vec = pl.BlockSpec((1, d), lambda i: (0, 0))
    dh, dhb, part = pl.pallas_call(
        body,
        name=name,
        grid=(t // tr,),
        in_specs=[row, row],
        out_specs=[row, row, vec],
        out_shape=[jax.ShapeDtypeStruct((t, d), F32), jax.ShapeDtypeStruct((t, d), BF16), jax.ShapeDtypeStruct((1, d), F32)],
        compiler_params=_params("arbitrary"),
    )(h, target)
    return 0.5 / d * jnp.sum(part), dh, dhb


_BNN = (((2,), (1,)), ((0,), (0,)))
_BNT = (((2,), (2,)), ((0,), (0,)))
_BTN = (((1,), (1,)), ((0,), (0,)))


_TAP0 = HALO - (CONV_K - 1)


def _conv(x_ref, w, rows):
    c = w[0:1, :] * x_ref[_TAP0:_TAP0 + rows, :]
    for j in range(1, CONV_K):
        c = c + w[j:j + 1, :] * x_ref[_TAP0 + j:_TAP0 + j + rows, :]
    return c


def _conv_silu_bwd(x_ref, w, dact, dc_ref, rows):
    c = _conv(x_ref, w, rows)
    sig = jax.nn.sigmoid(c)
    dc = dact * (sig * (1.0 + c * (1.0 - sig)))
    dw = [jnp.sum(dc * x_ref[_TAP0 + j:_TAP0 + j + rows, :], axis=0, keepdims=True) for j in range(CONV_K)]
    dc_ref[0:HALO, :] = jnp.zeros((HALO, HEAD_DIM), F32)
    dc_ref[HALO:HALO + rows, :] = dc
    dc_ref[HALO + rows:HALO + rows + HALO, :] = jnp.zeros((HALO, HEAD_DIM), F32)
    first = HALO - _TAP0
    dx = w[0:1, :] * dc_ref[first:first + HALO + rows, :]
    for j in range(1, CONV_K):
        dx = dx + w[j:j + 1, :] * dc_ref[first - j:first - j + HALO + rows, :]
    return dx, dw


@jax.custom_vjp
def _unit_lower_inverse(neg_l):
    n = neg_l.shape[0]
    eye = (lax.broadcasted_iota(jnp.int32, (n, CHUNK, CHUNK), 1) == lax.broadcasted_iota(jnp.int32, (n, CHUNK, CHUNK), 2))
    inv = eye.astype(F32) + neg_l
    power = _bdot(neg_l, neg_l, _BNN)
    for _ in range(4):
        both = _bdot(jnp.concatenate([inv, power], axis=1), power, _BNN)
        inv, power = inv + both[:, :CHUNK], both[:, CHUNK:]
    return inv + _bdot(inv, power, _BNN)


def _unit_lower_inverse_fwd(neg_l):
    inv = _unit_lower_inverse(neg_l)
    return inv, inv


def _unit_lower_inverse_bwd(inv, dinv):
    return (_fdot(_fdot(inv, dinv, _BTN), inv, _BNT),)


_unit_lower_inverse.defvjp(_unit_lower_inverse_fwd, _unit_lower_inverse_bwd)


def _gdn_intra(qt, kt, v, a, b, alog, dtb):
    n = a.shape[0] // CHUNK
    q = qt * lax.rsqrt(jnp.sum(qt * qt, axis=-1, keepdims=True) + EPS) * (HEAD_DIM ** -0.5)
    k = kt * lax.rsqrt(jnp.sum(kt * kt, axis=-1, keepdims=True) + EPS)
    lanes = jnp.ones((1, HEAD_DIM), F32)
    beta = jax.nn.sigmoid(b) * lanes
    sp = a + dtb
    g = (-jnp.exp(alog) * (jnp.maximum(sp, 0.0) + jnp.log(1.0 + jnp.exp(-jnp.abs(sp))))) * lanes
    q, k, v, beta, g = (t.reshape(n, CHUNK, HEAD_DIM) for t in (q, k, v, beta, g))

    row = lax.broadcasted_iota(jnp.int32, (n, CHUNK, CHUNK), 1)
    col = lax.broadcasted_iota(jnp.int32, (n, CHUNK, CHUNK), 2)
    tri_incl = row >= col
    tri_strict = row > col
    gc = _fdot(tri_incl.astype(F32), g, _BNN)
    gc_row = _fdot(g[:, :, :CHUNK], (row <= col).astype(F32), _BTN)
    decay = jnp.exp(jnp.where(tri_incl, gc[:, :, :CHUNK] - gc_row, -1e30))
    kb = k * beta
    vb = v * beta
    with_k = _bdot(jnp.concatenate([kb, q], axis=1), k, _BNT)
    neg_l = jnp.where(tri_strict, -(with_k[:, :CHUNK] * decay), 0.0)
    qk = jnp.where(tri_incl, with_k[:, CHUNK:] * decay, 0.0)
    inv = _unit_lower_inverse(neg_l)
    e = jnp.exp(gc)
    solved = _bdot(inv, jnp.concatenate([kb * e, vb], axis=2), _BNN)
    g_last = gc[:, CHUNK - 1:CHUNK, :]
    k_dec = k * jnp.exp(g_last - gc)
    from_k = _bdot(k_dec, solved, _BTN)
    from_qk = _bdot(qk, solved, _BNN)
    step, add = -from_k[:, :, :HEAD_DIM], from_k[:, :, HEAD_DIM:]
    read, out = q * e - from_qk[:, :, :HEAD_DIM], from_qk[:, :, HEAD_DIM:]
    return step, add, jnp.exp(g_last), read, out


def _gdn_scan_step(state, step, add, decay_last):
    return state * decay_last + _bdot(step, state) + add


def _gdn_outputs(states, read, out, z, onw):
    return _rms(_bdot(read, states, _BNN) + out, onw) * _silu(z)


def _scan_scratch(n, dtype):
    return [pltpu.VMEM((n, HEAD_DIM, HEAD_DIM), dtype), pltpu.VMEM((n, HEAD_DIM, HEAD_DIM), F32), pltpu.VMEM((n, 1, HEAD_DIM), F32)]


def _head_lane(h, offset=0):
    return lax.broadcasted_iota(jnp.int32, (1, LANES), 1) == h + offset


def _pick(mask, x):
    return jnp.sum(jnp.where(mask, x, 0.0), axis=1, keepdims=True)


def _gdn_specs(heads, tb, rev, nb, PAIR):
    assert heads % PAIR == 0
    blk = (lambda i: nb - 1 - i) if rev else (lambda i: i)
    hb = tb // HALO
    width, pairs = PAIR * HEAD_DIM, heads // PAIR

    def col(group):
        return pl.BlockSpec((tb, width), lambda i, h: (blk(i), group * pairs + h))

    def halo(group):
        return pl.BlockSpec((HALO, width), lambda i, h: (jnp.maximum(blk(i) * hb - 1, 0), group * pairs + h))

    def convw(group):
        return pl.BlockSpec((CONV_K, width), lambda i, h: (0, group * pairs + h))

    vec = pl.BlockSpec((1, LANES), lambda i, h: (0, 0))
    ab = pl.BlockSpec((tb, LANES), lambda i, h: (blk(i), 0))
    states = pl.BlockSpec((PAIR, tb // CHUNK, HEAD_DIM, HEAD_DIM), lambda i, h: (h, blk(i), 0, 0))
    return blk, col, halo, convw, vec, ab, states


def _head_cols(p):
    return slice(p * HEAD_DIM, (p + 1) * HEAD_DIM)


def _gdn_fwd(proj, ab, conv_w, alog_row, dtb_row, onw_row, *, heads, name, tb=1024, pair=4, gather=()):
    t = proj.shape[0]
    tb = min(tb, t)
    nb, cpb = t // tb, tb // CHUNK
    PAIR = min(pair, heads)
    _, col, halo, convw, vec, abspec, states = _gdn_specs(heads, tb, False, nb, PAIR)

    def body(q_ref, k_ref, v_ref, qh_ref, kh_ref, vh_ref, z_ref, ab_ref, wq_ref, wk_ref, wv_ref, alog_ref, dtb_ref, onw_ref,
             og_ref, st_ref, state_scr, x_scr, *op_scr):
        i, pair = pl.program_id(0), pl.program_id(1)
        abv = ab_ref[...]
        heads_here, later = [pair * PAIR + p for p in range(PAIR)], []
        for p, h in enumerate(heads_here):
            cols = _head_cols(p)
            for n, (ref, href) in enumerate(((q_ref, qh_ref), (k_ref, kh_ref), (v_ref, vh_ref))):
                x_scr[p, n, 0:HALO, :] = jnp.where(i > 0, href[:, cols], 0.0)
                x_scr[p, n, HALO:HALO + tb, :] = ref[:, cols]
            sel_a, sel_b = _head_lane(h), _head_lane(h, heads)
            alog, dtb = _pick(sel_a, alog_ref[...]), _pick(sel_a, dtb_ref[...])
            acts = [_silu(_conv(x_scr.at[p, n], w_ref[:, cols], tb)) for n, w_ref in enumerate((wq_ref, wk_ref, wv_ref))]
            *scan, read, out = _gdn_intra(*acts, _pick(sel_a, abv), _pick(sel_b, abv), alog, dtb)
            for scr, val in zip(op_scr[3 * p:3 * p + 3], scan):
                scr[...] = val.astype(scr.dtype)
            later.append((read, out))

        def chunk(c, states):
            for p in range(PAIR):
                st_ref[p, c] = states[p]
            return tuple(_gdn_scan_step(states[p], *[scr[c] for scr in op_scr[3 * p:3 * p + 3]]) for p in range(PAIR))

        @pl.when(i == 0)
        def _():
            for h in heads_here:
                state_scr[h] = jnp.zeros((HEAD_DIM, HEAD_DIM), F32)

        last = lax.fori_loop(0, cpb, chunk, tuple(state_scr[h] for h in heads_here))
        for p, h in enumerate(heads_here):
            cols = _head_cols(p)
            state_scr[h] = last[p]
            og = _gdn_outputs(st_ref[p], *later[p], z_ref[:, cols].reshape(cpb, CHUNK, HEAD_DIM), onw_ref[...])
            og_ref[:, cols] = og.reshape(tb, HEAD_DIM).astype(BF16)

    n_x = len(gather)
    grid = (nb, heads // PAIR)
    og, st, *gathered = pl.pallas_call(
        _with_exchange(body, 14, 2, True, n_x, grid),
        name=name,
        grid=grid,
        in_specs=[col(0), col(1), col(2), halo(0), halo(1), halo(2), col(3), abspec, convw(0), convw(1), convw(2), vec, vec, vec]
        + [_ANY] * n_x,
        out_specs=[pl.BlockSpec((tb, PAIR * HEAD_DIM), lambda i, h: (i, h)), states] + [_ANY] * n_x,
        out_shape=[jax.ShapeDtypeStruct((t, heads * HEAD_DIM), BF16),
                   jax.ShapeDtypeStruct((heads, t // CHUNK, HEAD_DIM, HEAD_DIM), F32)] + _chip_shapes(True, gather),
        scratch_shapes=[pltpu.VMEM((heads, HEAD_DIM, HEAD_DIM), F32), pltpu.VMEM((PAIR, 3, HALO + tb, HEAD_DIM), F32)]
        + _scan_scratch(cpb, BF16) * PAIR + (_chip_scratch(n_x) if n_x else []),
        compiler_params=_params("arbitrary", "arbitrary"),
    )(proj, proj, proj, proj, proj, proj, proj, ab, conv_w, conv_w, conv_w, alog_row, dtb_row, onw_row, *gather)
    return og, st, gathered


def _gdn_bwd(proj, ab, conv_w, alog_row, dtb_row, onw_row, states, dog, *, heads, name, tb=1024, pair=2, exchange=()):
    t = proj.shape[0]
    tb = min(tb, t)
    nb, cpb = t // tb, tb // CHUNK
    PAIR = min(pair, heads)
    _, col, halo, convw, vec, abspec, states_spec = _gdn_specs(heads, tb, True, nb, PAIR)
    n_conv = conv_w.shape[1]

    def body(q_ref, k_ref, v_ref, qh_ref, kh_ref, vh_ref, z_ref, ab_ref, wq_ref, wk_ref, wv_ref, alog_ref, dtb_ref, onw_ref,
             st_ref, dog_ref, dproj_ref, dab_ref, dconv_ref, dalog_ref, ddtb_ref, donw_ref,
             dstate_scr, x_scr, carry_scr, *scr):
        op_scr, dop_scr, dstates_scr, dc_scr = scr[:3 * PAIR], scr[3 * PAIR:6 * PAIR], scr[6 * PAIR:7 * PAIR], scr[7 * PAIR]
        i, pair = pl.program_id(0), pl.program_id(1)
        first_block = i == nb - 1
        heads_here, later = [pair * PAIR + p for p in range(PAIR)], []

        @pl.when(jnp.logical_and(i == 0, pair == 0))
        def _():
            dconv_ref[...] = jnp.zeros_like(dconv_ref)
            dalog_ref[...] = jnp.zeros_like(dalog_ref)
            ddtb_ref[...] = jnp.zeros_like(ddtb_ref)
            donw_ref[...] = jnp.zeros_like(donw_ref)

        @pl.when(pair == 0)
        def _():
            dab_ref[...] = jnp.zeros_like(dab_ref)

        @pl.when(i == 0)
        def _():
            for h in heads_here:
                dstate_scr[h] = jnp.zeros((HEAD_DIM, HEAD_DIM), F32)
                carry_scr[h] = jnp.zeros((3, HALO, HEAD_DIM), F32)

        abv = ab_ref[...]
        w_refs = (wq_ref, wk_ref, wv_ref)
        for p, h in enumerate(heads_here):
            cols = _head_cols(p)
            for n, (ref, href) in enumerate(((q_ref, qh_ref), (k_ref, kh_ref), (v_ref, vh_ref))):
                x_scr[p, n, 0:HALO, :] = jnp.where(first_block, 0.0, href[:, cols])
                x_scr[p, n, HALO:HALO + tb, :] = ref[:, cols]
            sel_a, sel_b = _head_lane(h), _head_lane(h, heads)
            alog, dtb = _pick(sel_a, alog_ref[...]), _pick(sel_a, dtb_ref[...])
            acts = [_silu(_conv(x_scr.at[p, n], w_ref[:, cols], tb)) for n, w_ref in enumerate(w_refs)]
            (*scan, read, out), vjp_intra = jax.vjp(_gdn_intra, *acts, _pick(sel_a, abv), _pick(sel_b, abv), alog, dtb)
            for s, val in zip(op_scr[3 * p:3 * p + 3], scan):
                s[...] = val.astype(s.dtype)
            blocked = lambda ref: ref[:, cols].reshape(cpb, CHUNK, HEAD_DIM)
            _, vjp_outputs = jax.vjp(_gdn_outputs, st_ref[p], read, out, blocked(z_ref), onw_ref[...])
            dstates_scr[p][...], dread, dout, dz, donw = vjp_outputs(blocked(dog_ref))
            dproj_ref[3, :, cols] = dz.reshape(tb, HEAD_DIM).astype(BF16)
            donw_ref[...] += donw
            later.append((vjp_intra, dread, dout, sel_a, sel_b))

        def chunk(i_rev, dstates):
            c = cpb - 1 - i_rev
            new = []
            for p in range(PAIR):
                _, vjp = jax.vjp(_gdn_scan_step, st_ref[p, c], *[s[c].astype(F32) for s in op_scr[3 * p:3 * p + 3]])
                dstate, *grads = vjp(dstates[p])
                for s, val in zip(dop_scr[3 * p:3 * p + 3], grads):
                    s[c] = val
                new.append(dstate + dstates_scr[p][c])
            return tuple(new)

        last = lax.fori_loop(0, cpb, chunk, tuple(dstate_scr[h] for h in heads_here))
        for p, h in enumerate(heads_here):
            cols = _head_cols(p)
            vjp_intra, dread, dout, sel_a, sel_b = later[p]
            dstate_scr[h] = last[p]
            *dacts, da, db, dalog, ddtb = vjp_intra((*[s[...] for s in dop_scr[3 * p:3 * p + 3]], dread, dout))
            dab_ref[...] += jnp.where(sel_a, da, 0.0) + jnp.where(sel_b, db, 0.0)
            for n, (dact, w_ref) in enumerate(zip(dacts, w_refs)):
                dx, dw = _conv_silu_bwd(x_scr.at[p, n], w_ref[:, cols], dact, dc_scr, tb)
                x_scr[p, n] = dx
                x_scr[p, n, tb:tb + HALO, :] += carry_scr[h, n]
                carry_scr[h, n] = x_scr[p, n, 0:HALO, :]
                dproj_ref[n, :, cols] = x_scr[p, n, HALO:HALO + tb, :].astype(BF16)
                lanes = pl.ds(pl.multiple_of((n * heads + h) * HEAD_DIM, HEAD_DIM), HEAD_DIM)
                for j in range(CONV_K):
                    dconv_ref[j:j + 1, lanes] += dw[j]
            dalog_ref[...] += jnp.where(sel_a, dalog, 0.0)
            ddtb_ref[...] += jnp.where(sel_a, ddtb, 0.0)

    dog_spec = pl.BlockSpec((tb, PAIR * HEAD_DIM), lambda i, h: (nb - 1 - i, h))
    dproj_spec = pl.BlockSpec((4, tb, PAIR * HEAD_DIM), lambda i, h: (0, nb - 1 - i, h))
    row_shape = jax.ShapeDtypeStruct((1, LANES), F32)
    n_x = len(exchange)
    grid = (nb, heads // PAIR)
    outs = pl.pallas_call(
        _with_exchange(body, 16, 6, False, n_x, grid),
        name=name,
        grid=grid,
        in_specs=[col(0), col(1), col(2), halo(0), halo(1), halo(2), col(3), abspec, convw(0), convw(1), convw(2), vec, vec, vec,
                  states_spec, dog_spec] + [_ANY] * n_x,
        out_specs=[dproj_spec, abspec, pl.BlockSpec((CONV_K, n_conv), lambda i, h: (0, 0)), vec, vec, vec] + [_ANY] * n_x,
        out_shape=[jax.ShapeDtypeStruct((4, t, heads * HEAD_DIM), BF16), jax.ShapeDtypeStruct((t, LANES), F32),
                   jax.ShapeDtypeStruct((CONV_K, n_conv), F32), row_shape, row_shape, row_shape] + _chip_shapes(False, exchange),
        scratch_shapes=[pltpu.VMEM((heads, HEAD_DIM, HEAD_DIM), F32), pltpu.VMEM((PAIR, 3, HALO + tb, HEAD_DIM), F32),
                        pltpu.VMEM((heads, 3, HALO, HEAD_DIM), F32)] + _scan_scratch(cpb, BF16) * PAIR
        + _scan_scratch(cpb, F32) * PAIR + [pltpu.VMEM((cpb, HEAD_DIM, HEAD_DIM), F32)] * PAIR
        + [pltpu.VMEM((HALO + tb + HALO, HEAD_DIM), F32)]
        + (_chip_scratch(n_x) if n_x else []),
        compiler_params=_params("arbitrary", "arbitrary", vmem=VMEM_LIMIT_WIDE_BYTES),
    )(proj, proj, proj, proj, proj, proj, proj, ab, conv_w, conv_w, conv_w, alog_row, dtb_row, onw_row, states, dog, *exchange)
    return (*outs[:6], outs[6:])


BAND = (LEFT_CHUNKS + 1) * CHUNK
PAD = LEFT_CHUNKS * CHUNK
GROUP = 2
ROWS = GROUP * CHUNK
WIN = (LEFT_CHUNKS + GROUP) * CHUNK
DIAGS = WIN + ROWS - 1
NEAR = PAD + ROWS - 1 - REL_CLIP
assert 0 < NEAR < DIAGS and WIN - PAD - 1 <= REL_CLIP and WIN % LANES == 0
ATTN_BLOCK = 1024


def _band_bias(rel_bias):
    heads = rel_bias.shape[0]
    far = jnp.broadcast_to(rel_bias[:, 2 * REL_CLIP:], (heads, NEAR + 1))
    near = rel_bias[:, 2 * REL_CLIP + NEAR + 1 - DIAGS:2 * REL_CLIP][:, ::-1]
    diag = jnp.concatenate([far, near], axis=1)
    return jnp.stack([diag[:, ROWS - 1 - r:ROWS - 1 - r + WIN] for r in range(ROWS)], axis=1)


def _band_bias_grad(dbias):
    heads = dbias.shape[0]
    diag = sum(jnp.pad(dbias[:, r, :], ((0, 0), (ROWS - 1 - r, r))) for r in range(ROWS))
    far = jnp.sum(diag[:, :NEAR + 1], axis=1, keepdims=True)
    near = diag[:, NEAR + 1:][:, ::-1]
    unused = jnp.zeros((heads, 2 * REL_CLIP - near.shape[1]), F32)
    return jnp.concatenate([unused, near, far], axis=1)


def _masked_bias(bias, n):
    r = np.arange(ROWS)[:, None]
    key = np.arange(WIN)[None, :]
    band_start = (r // CHUNK) * CHUNK
    in_band = np.logical_and(key >= band_start, key < band_start + BAND)
    in_sequence = key[None] >= PAD - np.arange(n)[:, None, None] * ROWS
    first = jnp.where(np.logical_and(in_band[None], in_sequence)[None], bias[:, None], -1e30)
    return first, jnp.where(in_band[None, None], bias[:, None], -1e30)


def _attn_groups(q_pre, z, kn, v, bias, qnw):
    q = _rms(q_pre, qnw)
    s = _bdot(q, kn, _BNT) * (HEAD_DIM ** -0.5) + bias
    p = jnp.exp(s - jnp.max(s, axis=-1, keepdims=True))
    p = p / jnp.sum(p, axis=-1, keepdims=True)
    return _bdot(p, v, _BNN) * _silu(z)


def _attn_groups_bwd(q_pre, z, kn, v, bias, qnw, dog):
    scale = HEAD_DIM ** -0.5
    inv_rms = lax.rsqrt(jnp.mean(q_pre * q_pre, axis=-1, keepdims=True) + EPS)
    q_hat = q_pre * inv_rms
    q_b = (q_hat * qnw).astype(BF16)
    s = _dot(q_b, kn, _BNT) * scale + bias
    e = jnp.exp(s - jnp.max(s, axis=-1, keepdims=True))
    p = e * (1.0 / jnp.sum(e, axis=-1, keepdims=True))
    p_b = p.astype(BF16)
    o = _dot(p_b, v, _BNN)
    sig = jax.nn.sigmoid(z)
    do = dog * (z * sig)
    dz = dog * o * (sig * (1.0 + z * (1.0 - sig)))
    do_b = do.astype(BF16)
    dv = _dot(p_b, do_b, _BTN)
    dp = _dot(do_b, v, _BNT)
    ds = p * (dp - jnp.sum(do * o, axis=-1, keepdims=True))
    ds_b = (ds * scale).astype(BF16)
    dq = _dot(ds_b, kn, _BNN)
    dkn = _dot(ds_b, q_b, _BTN)
    dqnw = jnp.sum(jnp.sum(dq * q_hat, axis=0), axis=0, keepdims=True)
    dq_hat = dq * qnw
    dq_pre = inv_rms * (dq_hat - q_hat * jnp.mean(dq_hat * q_hat, axis=-1, keepdims=True))
    return dq_pre, dz, dkn, dv, jnp.sum(ds, axis=0), dqnw


def _attn_specs(heads, tb, t):
    def col(group):
        return pl.BlockSpec((tb, HEAD_DIM), lambda h, i: (i, group * heads + h))

    def full(group):
        return pl.BlockSpec((t, HEAD_DIM), lambda h, i: (0, group * heads + h))

    bias = [pl.BlockSpec((1, tb // ROWS, ROWS, WIN), lambda h, i: (h, 0, 0, 0)),
            pl.BlockSpec((1, 1, ROWS, WIN), lambda h, i: (h, 0, 0, 0))]
    vec = pl.BlockSpec((1, HEAD_DIM), lambda h, i: (0, 0))
    return col, full, bias, vec


def _attn_windows(scr, block_start, n):
    return jnp.stack([scr[pl.ds(pl.multiple_of(block_start + g * ROWS, ROWS), WIN), :] for g in range(n)])


def _attn_fill(k_ref, v_ref, knw_ref, kn_scr, v_scr, t):
    kn_scr[0:PAD, :] = jnp.zeros((PAD, HEAD_DIM), BF16)
    v_scr[0:PAD, :] = jnp.zeros((PAD, HEAD_DIM), BF16)
    step = min(512, t)

    def fill(j, _):
        rows = pl.ds(pl.multiple_of(j * step, step), step)
        prows = pl.ds(pl.multiple_of(PAD + j * step, CHUNK), step)
        kn_scr[prows, :] = _rms(k_ref[rows, :], knw_ref[...]).astype(BF16)
        v_scr[prows, :] = v_ref[rows, :].astype(BF16)
        return 0

    lax.fori_loop(0, t // step, fill, 0)


def _attn_fwd(proj, bias, qnw_row, knw_row, *, heads, name, tb=ATTN_BLOCK):
    t = proj.shape[0]
    tb = min(tb, t)
    nb, ng = t // tb, tb // ROWS
    col, full, bias_spec, vec = _attn_specs(heads, tb, t)

    def body(q_ref, k_ref, v_ref, z_ref, first_ref, rest_ref, qnw_ref, knw_ref, og_ref, kn_scr, v_scr):
        i = pl.program_id(1)

        @pl.when(i == 0)
        def _():
            _attn_fill(k_ref, v_ref, knw_ref, kn_scr, v_scr, t)

        def run(block_bias):
            start = i * tb
            og = _attn_groups(q_ref[...].reshape(ng, ROWS, HEAD_DIM), z_ref[...].reshape(ng, ROWS, HEAD_DIM),
                              _attn_windows(kn_scr, start, ng), _attn_windows(v_scr, start, ng), block_bias, qnw_ref[...])
            og_ref[...] = og.reshape(tb, HEAD_DIM).astype(BF16)

        pl.when(i == 0)(lambda: run(first_ref[0]))
        pl.when(i > 0)(lambda: run(rest_ref[0]))

    return pl.pallas_call(
        body,
        name=name,
        grid=(heads, nb),
        in_specs=[col(0), full(1), full(2), col(3), *bias_spec, vec, vec],
        out_specs=pl.BlockSpec((tb, HEAD_DIM), lambda h, i: (i, h)),
        out_shape=jax.ShapeDtypeStruct((t, heads * HEAD_DIM), BF16),
        scratch_shapes=[pltpu.VMEM((PAD + t, HEAD_DIM), BF16), pltpu.VMEM((PAD + t, HEAD_DIM), BF16)],
        compiler_params=_params("arbitrary", "arbitrary"),
    )(proj, proj, proj, proj, *bias, qnw_row, knw_row)


def _attn_bwd(proj, bias, qnw_row, knw_row, dog, *, heads, name, tb=ATTN_BLOCK, sub=4):
    t = proj.shape[0]
    tb = min(tb, t)
    nb, ng = t // tb, tb // ROWS
    sub = min(sub, ng)
    col, full, bias_spec, vec = _attn_specs(heads, tb, t)

    def body(q_ref, k_ref, v_ref, z_ref, first_ref, rest_ref, qnw_ref, knw_ref, dog_ref,
             dqz_ref, dkv_ref, dbias_ref, dqnw_ref, dknw_ref, kn_scr, v_scr, dkn_scr, dv_scr):
        i = pl.program_id(1)

        @pl.when(i == 0)
        def _():
            _attn_fill(k_ref, v_ref, knw_ref, kn_scr, v_scr, t)
            dkn_scr[...] = jnp.zeros_like(dkn_scr)
            dv_scr[...] = jnp.zeros_like(dv_scr)
            dbias_ref[...] = jnp.zeros_like(dbias_ref)
            dqnw_ref[...] = jnp.zeros_like(dqnw_ref)

        def run(block_bias):
            for g0 in range(0, ng, sub):
                rows = pl.ds(g0 * ROWS, sub * ROWS)
                at = i * tb + g0 * ROWS
                blocked = lambda ref: ref[rows, :].reshape(sub, ROWS, HEAD_DIM)
                dq, dz, dkn, dv, dbias, dqnw = _attn_groups_bwd(
                    blocked(q_ref), blocked(z_ref), _attn_windows(kn_scr, at, sub), _attn_windows(v_scr, at, sub),
                    block_bias(g0), qnw_ref[...], blocked(dog_ref))
                dqz_ref[0, rows, :] = dq.reshape(sub * ROWS, HEAD_DIM).astype(BF16)
                dqz_ref[1, rows, :] = dz.reshape(sub * ROWS, HEAD_DIM).astype(BF16)
                for g in range(sub):
                    window = pl.ds(pl.multiple_of(at + g * ROWS, ROWS), WIN)
                    dkn_scr[window, :] += dkn[g]
                    dv_scr[window, :] += dv[g]
                dbias_ref[0] += dbias
                dqnw_ref[0] += dqnw

        pl.when(i == 0)(lambda: run(lambda g0: first_ref[0, g0:g0 + sub]))
        pl.when(i > 0)(lambda: run(lambda g0: rest_ref[0]))

        @pl.when(i == nb - 1)
        def _():
            step = min(512, t)

            def finish(j, dknw):
                rows = pl.ds(pl.multiple_of(j * step, step), step)
                prows = pl.ds(pl.multiple_of(PAD + j * step, CHUNK), step)
                _, vjp = jax.vjp(_rms, k_ref[rows, :], knw_ref[...])
                dk, dw = vjp(dkn_scr[prows, :])
                dkv_ref[0, rows, :] = dk.astype(BF16)
                dkv_ref[1, rows, :] = dv_scr[prows, :].astype(BF16)
                return dknw + dw

            dknw_ref[0] = lax.fori_loop(0, t // step, finish, jnp.zeros((1, HEAD_DIM), F32))

    pair_col = pl.BlockSpec((2, tb, HEAD_DIM), lambda h, i: (0, i, h))
    pair_full = pl.BlockSpec((2, t, HEAD_DIM), lambda h, i: (0, 0, h))
    head_vec = pl.BlockSpec((1, 1, HEAD_DIM), lambda h, i: (h, 0, 0))
    pair_shape = jax.ShapeDtypeStruct((2, t, heads * HEAD_DIM), BF16)
    vec_shape = jax.ShapeDtypeStruct((heads, 1, HEAD_DIM), F32)
    return pl.pallas_call(
        body,
        name=name,
        grid=(heads, nb),
        in_specs=[col(0), full(1), full(2), col(3), *bias_spec, vec, vec, pl.BlockSpec((tb, HEAD_DIM), lambda h, i: (i, h))],
        out_specs=[pair_col, pair_full, pl.BlockSpec((1, ROWS, WIN), lambda h, i: (h, 0, 0)), head_vec, head_vec],
        out_shape=[pair_shape, pair_shape, jax.ShapeDtypeStruct((heads, ROWS, WIN), F32), vec_shape, vec_shape],
        scratch_shapes=[pltpu.VMEM((PAD + t, HEAD_DIM), BF16), pltpu.VMEM((PAD + t, HEAD_DIM), BF16),
                        pltpu.VMEM((PAD + t, HEAD_DIM), F32), pltpu.VMEM((PAD + t, HEAD_DIM), F32)],
        compiler_params=_params("arbitrary", "arbitrary"),
    )(proj, proj, proj, proj, *bias, qnw_row, knw_row, dog)


def _lane_row(v):
    v = v.reshape(1, -1)
    return jnp.pad(v, ((0, 0), (0, LANES - v.shape[1])))


def _local_step(x, target, norm_w, wa_in, conv_w, a_log, dt_bias, onw, wa_out, wb_in, qnw, knw, rel_bias, wb_out, *,
                sharded=False):
    ha, hb = a_log.shape[-1], rel_bias.shape[-2]
    na = 4 * ha * HEAD_DIM
    wa_main = wa_in[:, :na]
    wa_ab = jnp.pad(wa_in[:, na:], ((0, 0), (0, LANES - 2 * ha)))
    alog_row, dtb_row, onw_row = _lane_row(a_log), _lane_row(dt_bias), _lane_row(onw)
    qnw_row, knw_row = _lane_row(qnw), _lane_row(knw)
    bias = _masked_bias(_band_bias(rel_bias.reshape(hb, -1)), min(ATTN_BLOCK, x.shape[0]) // ROWS)

    hn0 = _rmsnorm_fwd(x, norm_w[0:1], name="norm0")
    proj_a = _matmul(hn0, wa_main, name="a_in")
    ab_a = _matmul(hn0, wa_ab, name="a_in_ab")
    og_a, states, got = _gdn_fwd(proj_a, ab_a, conv_w, alog_row, dtb_row, onw_row, heads=ha, name="gdn_fwd",
                                 gather=[wb_in, wa_out, wb_out] if sharded else [])
    if sharded:
        wb_in, wa_out, wb_out = _join_cols(got[0]), got[1].reshape(-1, got[1].shape[-1]), got[2].reshape(-1, got[2].shape[-1])
    h1 = _matmul(og_a, wa_out, residual=x, name="a_out")
    hn1 = _rmsnorm_fwd(h1, norm_w[1:2], name="norm1")
    proj_b = _matmul(hn1, wb_in, name="b_in")
    og_b = _attn_fwd(proj_b, bias, qnw_row, knw_row, heads=hb, name="attn_fwd")
    h2 = _matmul(og_b, wb_out, residual=h1, name="b_out")
    loss, dh2, dh2_b = _loss_head(h2, target, name="loss_head")

    grad_dtype = BF16 if sharded else F32
    dog_b = _matmul(dh2_b, wb_out, trans_b=True, name="d_b_out_x")
    dwb_out = _matmul(og_b, dh2_b, trans_a=True, out_dtype=grad_dtype, name="d_b_out_w")
    dqz, dkv, dbias, dqnw, dknw = _attn_bwd(proj_b, bias, qnw_row, knw_row, dog_b, heads=hb, name="attn_bwd")
    dproj_b, qkvz = [dqz, dkv], (0, 3, 1, 2)
    dhn1 = _matmul(dproj_b, wb_in, trans_b=True, order=qkvz, name="d_b_in_x")
    dwb_in = _matmul(hn1, dproj_b, trans_a=True, order=qkvz, out_dtype=grad_dtype, col_slabs=N_CHIPS if sharded else 0,
                     name="d_b_in_w")
    dh1, dh1_b, dnw1 = _rmsnorm_bwd(h1, norm_w[1:2], dhn1, dh2, name="d_norm1")

    dog_a = _matmul(dh1_b, wa_out, trans_b=True, name="d_a_out_x")
    dwa_out = _matmul(og_a, dh1_b, trans_a=True, out_dtype=grad_dtype, name="d_a_out_w")
    early = [dwb_in, _split_rows(dwa_out), _split_rows(dwb_out)] if sharded else []
    dproj_a, dab, dconv, dalog, ddtb, donw, landed = _gdn_bwd(
        proj_a, ab_a, conv_w, alog_row, dtb_row, onw_row, states, dog_a, heads=ha, name="gdn_bwd", exchange=early)
    if sharded:
        dwb_in, dwa_out, dwb_out = landed
    dab_b = dab.astype(BF16)
    dwa_in = jnp.concatenate(
        [_matmul(hn0, dproj_a, trans_a=True, out_dtype=grad_dtype, name="d_a_in_w"),
         _matmul(hn0, dab_b, trans_a=True, out_dtype=grad_dtype, name="d_a_in_ab_w")[:, :2 * ha]], axis=1)
    if sharded:
        dhn0, (dwa_in, dconv) = _matmul(dproj_a, wa_main, trans_b=True, name="d_a_in_x",
                                        exchange=[_split_cols(dwa_in), _split_cols(dconv)])
    else:
        dhn0 = _matmul(dproj_a, wa_main, trans_b=True, name="d_a_in_x")
    dhn0 = _matmul(dab_b, wa_ab, trans_b=True, residual=dhn0, name="d_a_in_ab_x")
    dx, _, dnw0 = _rmsnorm_bwd(x, norm_w[0:1], dhn0, dh1, name="d_norm0")

    drel = _band_bias_grad(dbias)
    grads = dict(
        norm_w=jnp.concatenate([dnw0, dnw1], axis=0), a_w_in=dwa_in, a_conv_w=dconv, a_a_log=dalog[:, :ha],
        a_dt_bias=ddtb[:, :ha], a_out_norm_w=donw, a_w_out=dwa_out, b_w_in=dwb_in, b_q_norm_w=jnp.sum(dqnw, axis=0),
        b_k_norm_w=jnp.sum(dknw, axis=0), b_rel_bias=drel[None], b_w_out=dwb_out)
    return loss, dx, grads


_ANY = pl.BlockSpec(memory_space=pl.ANY)
_CHIP_FLIPS = ((1, 0), (0, 1), (1, 1))


def _place():
    x, y, c = lax.axis_index("x"), lax.axis_index("y"), lax.axis_index("c")
    return x, y, c


def _flip(v, bit):
    return 1 - v if bit else v


def _remote(src, dst, send_sem, recv_sem, peer):
    return pltpu.make_async_remote_copy(src_ref=src, dst_ref=dst, send_sem=send_sem, recv_sem=recv_sem, device_id=peer,
                                        device_id_type=MESH)


def _comm_call(body, arrays, out_shapes, n_remote, n_local, name):
    scratch = [pltpu.SemaphoreType.DMA((n_remote,)), pltpu.SemaphoreType.DMA((n_remote,))]
    if n_local:
        scratch.append(pltpu.SemaphoreType.DMA((n_local,)))
    return pl.pallas_call(
        body, name=name, in_specs=[_ANY] * len(arrays), out_specs=[_ANY] * len(out_shapes), out_shape=out_shapes,
        scratch_shapes=scratch)(*arrays)


def _chip_scratch(n):
    return [pltpu.SemaphoreType.DMA((3 * n,)), pltpu.SemaphoreType.DMA((3 * n,)), pltpu.SemaphoreType.DMA((n,))]


def _chip_shapes(gather, arrays):
    return [jax.ShapeDtypeStruct(((N_CHIPS,) + s.shape) if gather else s.shape, s.dtype) for s in arrays]


def _chip_traffic(gather, ins, outs, sems):
    send_sems, recv_sems, local_sems = sems
    x, y, c = _place()
    mine = 2 * x + y
    local, remote, landing = [], [], []
    for a in range(len(ins)):
        local.append(pltpu.make_async_copy(ins[a] if gather else ins[a].at[mine], outs[a].at[mine], local_sems.at[a]))
        for k, (fx, fy) in enumerate(_CHIP_FLIPS):
            peer = (_flip(x, fx), _flip(y, fy), c)
            theirs = 2 * peer[0] + peer[1]
            src = ins[a] if gather else ins[a].at[theirs]
            pair = send_sems.at[3 * a + k], recv_sems.at[3 * a + k]
            remote.append(_remote(src, outs[a].at[mine], *pair, peer))
            landing.append(_remote(src, outs[a].at[theirs], *pair, peer))
    return local + remote, (local, landing, remote)


def _start(traffic):
    for cp in traffic[0]:
        cp.start()


def _finish(traffic):
    local, landing, remote = traffic[1]
    for cp in local:
        cp.wait()
    for cp in landing:
        cp.wait_recv()
    for cp in remote:
        cp.wait_send()


def _with_exchange(compute, n_in, n_out, gather, n_x, grid):
    if not n_x:
        return compute

    def body(*refs):
        ins, x_in = refs[:n_in], refs[n_in:n_in + n_x]
        outs, x_out = refs[n_in + n_x:n_in + n_x + n_out], refs[n_in + n_x + n_out:n_in + 2 * n_x + n_out]
        scratch, sems = refs[n_in + 2 * n_x + n_out:-3], refs[-3:]
        traffic = _chip_traffic(gather, x_in, x_out, sems)
        first = functools.reduce(jnp.logical_and, [pl.program_id(d) == 0 for d in range(len(grid))])
        last = functools.reduce(jnp.logical_and, [pl.program_id(d) == grid[d] - 1 for d in range(len(grid))])

        @pl.when(first)
        def _():
            _start(traffic)

        compute(*ins, *outs, *scratch)

        @pl.when(last)
        def _():
            _finish(traffic)

    return body


def _gather_shared(shard, small, *, name):
    rows = shard.shape[0]
    assert rows % 2 == 0
    half = rows // 2

    def body(shard_ref, small_ref, out_ref, small_out_ref, send_sems, recv_sems, local_sems):
        x, y, c = _place()
        mine = 2 * x + y
        sibling = (x, y, 1 - c)
        my_rows = pl.ds(pl.multiple_of(c * half, 8), half)
        local = [pltpu.make_async_copy(shard_ref, out_ref.at[mine], local_sems.at[0]),
                 pltpu.make_async_copy(small_ref, small_out_ref.at[mine], local_sems.at[1])]
        sent, landed, passed_on, handed = [], [], [], []
        for k, (fx, fy) in enumerate(_CHIP_FLIPS):
            peer = (_flip(x, fx), _flip(y, fy), c)
            theirs = 2 * peer[0] + peer[1]
            ici, d2d, tiny = [(send_sems.at[3 * n + k], recv_sems.at[3 * n + k]) for n in range(3)]
            sent.append(_remote(shard_ref.at[my_rows], out_ref.at[mine, my_rows], *ici, peer))
            landed.append(_remote(shard_ref.at[my_rows], out_ref.at[theirs, my_rows], *ici, peer))
            sent.append(_remote(small_ref, small_out_ref.at[mine], *tiny, peer))
            landed.append(_remote(small_ref, small_out_ref.at[theirs], *tiny, peer))
            passed_on.append(_remote(out_ref.at[theirs, my_rows], out_ref.at[theirs, my_rows], *d2d, sibling))
            other_rows = pl.ds(pl.multiple_of((1 - c) * half, 8), half)
            handed.append(_remote(out_ref.at[theirs, other_rows], out_ref.at[theirs, other_rows], *d2d, sibling))
        for cp in local + sent:
            cp.start()
        for k in range(3):
            landed[2 * k].wait_recv()
            passed_on[k].start()
        for k in range(3):
            landed[2 * k + 1].wait_recv()
            handed[k].wait_recv()
        for cp in local:
            cp.wait()
        for cp in sent + passed_on:
            cp.wait_send()

    return pl.pallas_call(
        body, name=name, in_specs=[_ANY] * 2, out_specs=[_ANY] * 2, out_shape=_chip_shapes(True, [shard, small]),
        scratch_shapes=[pltpu.SemaphoreType.DMA((9,)), pltpu.SemaphoreType.DMA((9,)), pltpu.SemaphoreType.DMA((2,))],
    )(shard, small)


def _swap_pair(arrays, *, name):
    n = len(arrays)

    def body(*refs):
        ins, outs, (send_sems, recv_sems) = refs[:n], refs[n:2 * n], refs[2 * n:]
        x, y, c = _place()
        copies = [_remote(ins[a], outs[a], send_sems.at[a], recv_sems.at[a], (x, y, 1 - c)) for a in range(n)]
        for cp in copies:
            cp.start()
        for cp in copies:
            cp.wait_recv()
        for cp in copies:
            cp.wait_send()

    shapes = [jax.ShapeDtypeStruct(s.shape, s.dtype) for s in arrays]
    return _comm_call(body, arrays, shapes, n, 0, name)


def _gather_all(tile, *, name):
    def body(in_ref, out_ref, send_sems, recv_sems, local_sems):
        x, y, c = _place()
        mine = 4 * x + 2 * y + c
        local = pltpu.make_async_copy(in_ref, out_ref.at[mine], local_sems.at[0])
        remote, landing = [], []
        for k in range(1, N_DEV):
            peer = (_flip(x, k & 4), _flip(y, k & 2), _flip(c, k & 1))
            sems = send_sems.at[k - 1], recv_sems.at[k - 1]
            remote.append(_remote(in_ref, out_ref.at[mine], *sems, peer))
            landing.append(_remote(in_ref, out_ref.at[4 * peer[0] + 2 * peer[1] + peer[2]], *sems, peer))
        for cp in [local] + remote:
            cp.start()
        local.wait()
        for cp in landing:
            cp.wait_recv()
        for cp in remote:
            cp.wait_send()

    return _comm_call(body, [tile], [jax.ShapeDtypeStruct((N_DEV,) + tile.shape, tile.dtype)], N_DEV - 1, 1, name)[0]


def _sum_slots(slabs, *, name, tr=128):
    s, r, c = slabs.shape
    tr = min(tr, r)

    def body(in_ref, o_ref):
        acc = in_ref[0].astype(F32)
        for j in range(1, s):
            acc = acc + in_ref[j].astype(F32)
        o_ref[...] = acc

    return pl.pallas_call(
        body, name=name, grid=(r // tr,),
        in_specs=[pl.BlockSpec((s, tr, c), lambda i: (0, i, 0))], out_specs=pl.BlockSpec((tr, c), lambda i: (i, 0)),
        out_shape=jax.ShapeDtypeStruct((r, c), F32), compiler_params=_params("parallel"))(slabs)


def _adamw_math(w, g, m, v):
    m = ADAM_B1 * m + (1.0 - ADAM_B1) * g
    v = ADAM_B2 * v + (1.0 - ADAM_B2) * (g * g)
    m_hat = m / (1.0 - ADAM_B1 ** ADAM_STEP)
    v_hat = v / (1.0 - ADAM_B2 ** ADAM_STEP)
    delta = -ADAM_LR * (m_hat / (jnp.sqrt(v_hat) + ADAM_EPS) + ADAM_WD * w)
    return delta, m, v


def _adamw(w, m, v, parts, *, name, tr=128):
    r, c = w.shape
    tr = min(tr, r)
    s = len(parts)

    def body(w_ref, m_ref, v_ref, *refs):
        g_ref, d_ref, nm_ref, nv_ref = refs[s:]
        g = refs[0][...]
        for p_ref in refs[1:s]:
            g = g + p_ref[...]
        g_ref[...] = g
        d_ref[...], nm_ref[...], nv_ref[...] = _adamw_math(w_ref[...], g, m_ref[...], v_ref[...])

    blk = pl.BlockSpec((tr, c), lambda i: (i, 0))
    shape = jax.ShapeDtypeStruct((r, c), F32)
    return pl.pallas_call(
        body, name=name, grid=(r // tr,), in_specs=[blk] * (3 + s), out_specs=[blk] * 4, out_shape=[shape] * 4,
        compiler_params=_params("parallel"))(w, m, v, *parts)


_BIG = ("a_w_in", "b_w_in", "a_w_out", "b_w_out", "a_conv_w")
_SMALL = ("norm_w", "a_a_log", "a_dt_bias", "a_out_norm_w", "b_q_norm_w", "b_k_norm_w", "b_rel_bias")
_ORDER = ("norm_w", "a_w_in", "a_conv_w", "a_a_log", "a_dt_bias", "a_out_norm_w", "a_w_out", "b_w_in", "b_q_norm_w",
          "b_k_norm_w", "b_rel_bias", "b_w_out")


def _join_cols(g):
    return jnp.transpose(g, (1, 0, 2)).reshape(g.shape[1], -1)


def _split_cols(g):
    return jnp.transpose(g.reshape(g.shape[0], N_CHIPS, -1), (1, 0, 2))


def _split_rows(g):
    return g.reshape(N_CHIPS, -1, g.shape[-1])


def _pack(d):
    flat = jnp.concatenate([d[n].reshape(-1) for n in _SMALL])
    return jnp.pad(flat, (0, -flat.shape[0] % LANES)).reshape(1, -1)


def _unpack(row, like):
    out, at = {}, 0
    for n in _SMALL:
        size = like[n].size
        out[n] = row[0, at:at + size].reshape(like[n].shape)
        at += size
    return out


def kernel(x, norm_w, a_w_in, a_conv_w, a_a_log, a_dt_bias, a_out_norm_w, a_w_out, b_w_in, b_q_norm_w, b_k_norm_w, b_rel_bias, b_w_out, loss_target, m_norm_w, m_a_w_in, m_a_conv_w, m_a_a_log, m_a_dt_bias, m_a_out_norm_w, m_a_w_out, m_b_w_in, m_b_q_norm_w, m_b_k_norm_w, m_b_rel_bias, m_b_w_out, v_norm_w, v_a_w_in, v_a_conv_w, v_a_a_log, v_a_dt_bias, v_a_out_norm_w, v_a_w_out, v_b_w_in, v_b_q_norm_w, v_b_k_norm_w, v_b_rel_bias, v_b_w_out):
    w = dict(norm_w=norm_w, a_w_in=a_w_in, a_conv_w=a_conv_w, a_a_log=a_a_log, a_dt_bias=a_dt_bias,
             a_out_norm_w=a_out_norm_w, a_w_out=a_w_out, b_w_in=b_w_in, b_q_norm_w=b_q_norm_w, b_k_norm_w=b_k_norm_w,
             b_rel_bias=b_rel_bias, b_w_out=b_w_out)
    m = dict(norm_w=m_norm_w, a_w_in=m_a_w_in, a_conv_w=m_a_conv_w, a_a_log=m_a_a_log, a_dt_bias=m_a_dt_bias,
             a_out_norm_w=m_a_out_norm_w, a_w_out=m_a_w_out, b_w_in=m_b_w_in, b_q_norm_w=m_b_q_norm_w,
             b_k_norm_w=m_b_k_norm_w, b_rel_bias=m_b_rel_bias, b_w_out=m_b_w_out)
    v = dict(norm_w=v_norm_w, a_w_in=v_a_w_in, a_conv_w=v_a_conv_w, a_a_log=v_a_a_log, a_dt_bias=v_a_dt_bias,
             a_out_norm_w=v_a_out_norm_w, a_w_out=v_a_w_out, b_w_in=v_b_w_in, b_q_norm_w=v_b_q_norm_w,
             b_k_norm_w=v_b_k_norm_w, b_rel_bias=v_b_rel_bias, b_w_out=v_b_w_out)

    wa_in, conv = _gather_shared(a_w_in[0].astype(BF16), a_conv_w[0], name="gather_a_in")
    loss, dx, grads = _local_step(
        x[0], loss_target[0], norm_w, _join_cols(wa_in), _join_cols(conv), a_a_log, a_dt_bias, a_out_norm_w,
        a_w_out[0].astype(BF16), b_w_in[0].astype(BF16), b_q_norm_w, b_k_norm_w, b_rel_bias, b_w_out[0].astype(BF16),
        sharded=True)
    loss = lax.psum(loss, ("x", "y", "c"))

    mine = [_sum_slots(grads[n], name=f"chip_sum_{n}") for n in _BIG]
    theirs = _swap_pair(mine, name="pair_grads")
    out = {}
    for n, p, q in zip(_BIG, mine, theirs):
        out[n] = [r[None] for r in _adamw(w[n][0], m[n][0], v[n][0], [p, q], name=f"adamw_{n}")]

    row = _pack(grads)
    tiles = _gather_all(jnp.broadcast_to(row, (8, row.shape[1])), name="gather_small_grads")
    res = _adamw(_pack(w), _pack(m), _pack(v), [tiles[d, 0:1, :] for d in range(N_DEV)], name="adamw_small")
    unpacked = [_unpack(r, w) for r in res]
    for n in _SMALL:
        out[n] = [u[n] for u in unpacked]

    return (loss, dx[None], *[out[n][0] for n in _ORDER], *[out[n][1] for n in _ORDER], *[out[n][2] for n in _ORDER],
            *[out[n][3] for n in _ORDER])
```

```python
import functools

import numpy as np
import jax
import jax.numpy as jnp
from jax import lax
from jax.experimental import pallas as pl
from jax.experimental.pallas import tpu as pltpu

F32 = jnp.float32
BF16 = jnp.bfloat16

CHUNK = 64
HEAD_DIM = 128
LEFT_CHUNKS = 8
REL_CLIP = 256
CONV_K = 4
EPS = 1e-6
HALO = 8

ADAM_LR = 0.001
ADAM_B1 = 0.9
ADAM_B2 = 0.999
ADAM_EPS = 1e-08
ADAM_WD = 0.01
ADAM_STEP = 10

LANES = 128
N_CHIPS = 4
N_DEV = 8
VMEM_LIMIT_BYTES = 56 * 1024 * 1024
VMEM_LIMIT_WIDE_BYTES = 63 * 1024 * 1024
MESH = pl.DeviceIdType.MESH


def _params(*sem, vmem=VMEM_LIMIT_BYTES):
    return pltpu.CompilerParams(dimension_semantics=sem, vmem_limit_bytes=vmem)


def _dot(a, b, dims=(((1,), (0,)), ((), ())), precision=None):
    return lax.dot_general(a, b, dims, precision=precision, preferred_element_type=F32)


_NT = (((1,), (1,)), ((), ()))
_TN = (((0,), (0,)), ((), ()))


def _bdot(a, b, dims=(((1,), (0,)), ((), ()))):
    return _dot(a.astype(BF16), b.astype(BF16), dims)


def _fdot(a, b, dims=(((1,), (0,)), ((), ()))):
    return _dot(a, b, dims, precision=lax.Precision.HIGH)


def _silu(x):
    return x * jax.nn.sigmoid(x)


def _stacks(x):
    if not isinstance(x, (list, tuple)) and x.ndim != 3:
        return None
    arrays = list(x) if isinstance(x, (list, tuple)) else [x]
    assert len({(v.shape[1], v.shape[2], v.dtype) for v in arrays}) == 1
    starts = [sum(v.shape[0] for v in arrays[:r]) for r in range(len(arrays))]
    return arrays, starts, starts[-1] + arrays[-1].shape[0]


def _static_pick(table, index):
    out = table[-1]
    for s in range(len(table) - 2, -1, -1):
        out = jnp.where(index == s, table[s], out)
    return out


def _matmul(a, b, *, name, trans_a=False, trans_b=False, residual=None, out_dtype=F32, tm=1024, tn=1024, tk=2048,
            col_slabs=0, order=None, exchange=()):
    assert not (trans_a and trans_b)
    a_stack, b_stack = _stacks(a), _stacks(b)
    assert not (a_stack and (trans_a or b_stack)) and not (b_stack and trans_b)
    a_list, b_list = (a_stack[0] if a_stack else [a]), (b_stack[0] if b_stack else [b])
    a0, b0 = a_list[0], b_list[0]
    k, m = (a_stack[2] * a0.shape[2], a0.shape[1]) if a_stack else a.shape if trans_a else a.shape[::-1]
    n = b_stack[2] * b0.shape[2] if b_stack else b.shape[0] if trans_b else b.shape[1]
    tm, tn, tk = min(tm, m), min(tn, n // max(col_slabs, 1)), min(tk, k)
    if a_stack:
        tk = min(tk, a0.shape[2])
        per_k = a0.shape[2] // tk
    if b_stack:
        tn = min(tn, b0.shape[2])
        per_n = b0.shape[2] // tn
    assert m % tm == 0 and n % tn == 0 and k % tk == 0, (a0.shape, b0.shape, tm, tn, tk)
    nk = k // tk
    dims = _NT if trans_b else _TN if trans_a else (((1,), (0,)), ((), ()))
    order = list(order) if order is not None else list(range(max(a_stack[2] if a_stack else 0, b_stack[2] if b_stack else 0)))
    na, nb = len(a_list), len(b_list)

    def group_of(r, stack, position):
        arrays, starts, _ = stack
        local = position - starts[r]
        return jnp.logical_and(local >= 0, local < arrays[r].shape[0]), jnp.clip(local, 0, arrays[r].shape[0] - 1)

    def body(*refs):
        a_refs, b_refs = refs[:na], refs[na:na + nb]
        r_ref = refs[na + nb] if residual is not None else None
        o_ref, acc_ref = refs[-2:]
        j, kk = pl.program_id(1), pl.program_id(2)

        @pl.when(kk == 0)
        def _():
            acc_ref[...] = jnp.zeros_like(acc_ref)

        for ra, a_ref in enumerate(a_refs):
            for rb, b_ref in enumerate(b_refs):
                def add(a_ref=a_ref, b_ref=b_ref):
                    acc_ref[...] += _dot(a_ref[...], b_ref[...], dims)

                if na > 1:
                    pl.when(group_of(ra, a_stack, kk // per_k)[0])(add)
                elif nb > 1:
                    pl.when(group_of(rb, b_stack, j // per_n)[0])(add)
                else:
                    add()

        @pl.when(kk == nk - 1)
        def _():
            r = acc_ref[...]
            if r_ref is not None:
                r = r + r_ref[...]
            o_ref[...] = r.astype(o_ref.dtype)

    if a_stack:
        a_specs = [pl.BlockSpec((None, tm, tk), lambda i, j, kk, r=r: (group_of(r, a_stack, kk // per_k)[1], i, kk % per_k))
                   for r in range(na)]
        b_k = lambda kk: _static_pick(order, kk // per_k) * per_k + kk % per_k
    else:
        a_specs = [pl.BlockSpec((tk, tm), lambda i, j, kk: (kk, i)) if trans_a else pl.BlockSpec((tm, tk), lambda i, j, kk: (i, kk))]
        b_k = lambda kk: kk
    if b_stack:
        b_specs = [pl.BlockSpec((None, tk, tn), lambda i, j, kk, r=r: (group_of(r, b_stack, j // per_n)[1], kk, j % per_n))
                   for r in range(nb)]
        out_col = lambda j: _static_pick(order, j // per_n) * per_n + j % per_n
    else:
        b_specs = [pl.BlockSpec((tn, tk), lambda i, j, kk: (j, b_k(kk))) if trans_b
                   else pl.BlockSpec((tk, tn), lambda i, j, kk: (b_k(kk), j))]
        out_col = lambda j: j
    in_specs = a_specs + b_specs
    args = a_list + b_list
    if residual is not None:
        in_specs.append(pl.BlockSpec((tm, tn), lambda i, j, kk: (i, j)))
        args.append(residual)
    grid = (m // tm, n // tn, nk)
    n_x = len(exchange)
    if col_slabs:
        per = n // col_slabs // tn
        assert per * tn * col_slabs == n, (n, tn, col_slabs)
        out_spec = pl.BlockSpec((None, tm, tn), lambda i, j, kk: (out_col(j) // per, i, out_col(j) % per))
        out_shape = jax.ShapeDtypeStruct((col_slabs, m, n // col_slabs), out_dtype)
    else:
        out_spec = pl.BlockSpec((tm, tn), lambda i, j, kk: (i, out_col(j)))
        out_shape = jax.ShapeDtypeStruct((m, n), out_dtype)
    out, *landed = pl.pallas_call(
        _with_exchange(body, len(args), 1, False, n_x, grid),
        name=name,
        grid=grid,
        in_specs=in_specs + [_ANY] * n_x,
        out_specs=[out_spec] + [_ANY] * n_x,
        out_shape=[out_shape] + _chip_shapes(False, exchange),
        scratch_shapes=[pltpu.VMEM((tm, tn), F32)] + (_chip_scratch(n_x) if n_x else []),
        compiler_params=_params(*(("arbitrary",) * 3 if n_x else ("parallel", "parallel", "arbitrary"))),
    )(*args, *exchange)
    return (out, landed) if n_x else out


def _rms(x, w):
    return x * lax.rsqrt(jnp.mean(x * x, axis=-1, keepdims=True) + EPS) * w


def _rmsnorm_fwd(x, w_row, *, name, tr=512):
    t, d = x.shape
    tr = min(tr, t)

    def body(x_ref, w_ref, o_ref):
        o_ref[...] = _rms(x_ref[...], w_ref[...]).astype(BF16)

    return pl.pallas_call(
        body,
        name=name,
        grid=(t // tr,),
        in_specs=[pl.BlockSpec((tr, d), lambda i: (i, 0)), pl.BlockSpec((1, d), lambda i: (0, 0))],
        out_specs=pl.BlockSpec((tr, d), lambda i: (i, 0)),
        out_shape=jax.ShapeDtypeStruct((t, d), BF16),
        compiler_params=_params("parallel"),
    )(x, w_row)


def _rmsnorm_bwd(x, w_row, dy, dres, *, name, tr=256, narrow=None):
    t, d = x.shape
    tr = min(tr, t)
    extra = list(narrow) if narrow is not None else []

    def body(x_ref, w_ref, dy_ref, dres_ref, *refs):
        dx_ref, dxb_ref, dw_ref = refs[len(extra):]

        @pl.when(pl.program_id(0) == 0)
        def _():
            dw_ref[...] = jnp.zeros_like(dw_ref)

        dy = dy_ref[...]
        if extra:
            dy = dy + _dot(refs[0][...], refs[1][...], _NT)
        _, vjp = jax.vjp(_rms, x_ref[...], w_ref[...])
        dx, dw = vjp(dy)
        dx = dx + dres_ref[...]
        dx_ref[...] = dx
        dxb_ref[...] = dx.astype(BF16)
        dw_ref[...] += dw

    row = pl.BlockSpec((tr, d), lambda i: (i, 0))
    vec = pl.BlockSpec((1, d), lambda i: (0, 0))
    extra_specs = [pl.BlockSpec((tr, LANES), lambda i: (i, 0)), pl.BlockSpec((d, LANES), lambda i: (0, 0))] if extra else []
    return pl.pallas_call(
        body,
        name=name,
        grid=(t // tr,),
        in_specs=[row, vec, row, row] + extra_specs,
        out_specs=[row, row, vec],
        out_shape=[jax.ShapeDtypeStruct((t, d), F32), jax.ShapeDtypeStruct((t, d), BF16), jax.ShapeDtypeStruct((1, d), F32)],
        compiler_params=_params("arbitrary"),
    )(x, w_row, dy, dres, *extra)


def _matmul_loss(a, b, residual, target, *, name, tm=512, tn=1024):
    t, k = a.shape
    d = b.shape[1]
    tm, tn = min(tm, t), min(tn, d)

    def body(a_ref, b_ref, r_ref, t_ref, dh_ref, dhb_ref, part_ref):
        @pl.when(pl.program_id(1) == 0)
        def _():
            part_ref[...] = jnp.zeros_like(part_ref)

        err = _dot(a_ref[...], b_ref[...]) + r_ref[...] - t_ref[...]
        dh = err * (1.0 / d)
        dh_ref[...] = dh
        dhb_ref[...] = dh.astype(BF16)
        part_ref[...] += jnp.sum(err * err, axis=0, keepdims=True)

    tile = pl.BlockSpec((tm, tn), lambda j, i: (i, j))
    dh, dhb, part = pl.pallas_call(
        body,
        name=name,
        grid=(d // tn, t // tm),
        in_specs=[pl.BlockSpec((tm, k), lambda j, i: (i, 0)), pl.BlockSpec((k, tn), lambda j, i: (0, j)), tile, tile],
        out_specs=[tile, tile, pl.BlockSpec((1, tn), lambda j, i: (0, j))],
        out_shape=[jax.ShapeDtypeStruct((t, d), F32), jax.ShapeDtypeStruct((t, d), BF16), jax.ShapeDtypeStruct((1, d), F32)],
        compiler_params=_params("arbitrary", "arbitrary"),
    )(a, b, residual, target)
    return 0.5 / d * jnp.sum(part), dh, dhb


_BNN = (((2,), (1,)), ((0,), (0,)))
_BNT = (((2,), (2,)), ((0,), (0,)))
_BTN = (((1,), (1,)), ((0,), (0,)))


_TAP0 = HALO - (CONV_K - 1)


def _conv(x_ref, w, rows):
    c = w[0:1, :] * x_ref[_TAP0:_TAP0 + rows, :]
    for j in range(1, CONV_K):
        c = c + w[j:j + 1, :] * x_ref[_TAP0 + j:_TAP0 + j + rows, :]
    return c


def _conv_silu_bwd(x_ref, w, dact, dc_ref, rows):
    c = _conv(x_ref, w, rows)
    sig = jax.nn.sigmoid(c)
    dc = dact * (sig * (1.0 + c * (1.0 - sig)))
    dw = [jnp.sum(dc * x_ref[_TAP0 + j:_TAP0 + j + rows, :], axis=0, keepdims=True) for j in range(CONV_K)]
    dc_ref[0:HALO, :] = jnp.zeros((HALO, HEAD_DIM), F32)
    dc_ref[HALO:HALO + rows, :] = dc
    dc_ref[HALO + rows:HALO + rows + HALO, :] = jnp.zeros((HALO, HEAD_DIM), F32)
    first = HALO - _TAP0
    dx = w[0:1, :] * dc_ref[first:first + HALO + rows, :]
    for j in range(1, CONV_K):
        dx = dx + w[j:j + 1, :] * dc_ref[first - j:first - j + HALO + rows, :]
    return dx, dw


@jax.custom_vjp
def _unit_lower_inverse(neg_l):
    n = neg_l.shape[0]
    eye = (lax.broadcasted_iota(jnp.int32, (n, CHUNK, CHUNK), 1) == lax.broadcasted_iota(jnp.int32, (n, CHUNK, CHUNK), 2))
    inv = eye.astype(F32) + neg_l
    power = _bdot(neg_l, neg_l, _BNN)
    for _ in range(4):
        both = _bdot(jnp.concatenate([inv, power], axis=1), power, _BNN)
        inv, power = inv + both[:, :CHUNK], both[:, CHUNK:]
    return inv + _bdot(inv, power, _BNN)


def _unit_lower_inverse_fwd(neg_l):
    inv = _unit_lower_inverse(neg_l)
    return inv, inv


def _unit_lower_inverse_bwd(inv, dinv):
    return (_fdot(_fdot(inv, dinv, _BTN), inv, _BNT),)


_unit_lower_inverse.defvjp(_unit_lower_inverse_fwd, _unit_lower_inverse_bwd)


def _gdn_intra(qt, kt, v, a, b, alog, dtb):
    n = a.shape[0] // CHUNK
    q = qt * lax.rsqrt(jnp.sum(qt * qt, axis=-1, keepdims=True) + EPS) * (HEAD_DIM ** -0.5)
    k = kt * lax.rsqrt(jnp.sum(kt * kt, axis=-1, keepdims=True) + EPS)
    lanes = jnp.ones((1, HEAD_DIM), F32)
    beta = jax.nn.sigmoid(b) * lanes
    sp = a + dtb
    g = (-jnp.exp(alog) * (jnp.maximum(sp, 0.0) + jnp.log(1.0 + jnp.exp(-jnp.abs(sp))))) * lanes
    q, k, v, beta, g = (t.reshape(n, CHUNK, HEAD_DIM) for t in (q, k, v, beta, g))

    row = lax.broadcasted_iota(jnp.int32, (n, CHUNK, CHUNK), 1)
    col = lax.broadcasted_iota(jnp.int32, (n, CHUNK, CHUNK), 2)
    tri_incl = row >= col
    tri_strict = row > col
    gc = _fdot(tri_incl.astype(F32), g, _BNN)
    gc_row = _fdot(g[:, :, :CHUNK], (row <= col).astype(F32), _BTN)
    decay = jnp.exp(jnp.where(tri_incl, gc[:, :, :CHUNK] - gc_row, -1e30))
    kb = k * beta
    vb = v * beta
    with_k = _bdot(jnp.concatenate([kb, q], axis=1), k, _BNT)
    neg_l = jnp.where(tri_strict, -(with_k[:, :CHUNK] * decay), 0.0)
    qk = jnp.where(tri_incl, with_k[:, CHUNK:] * decay, 0.0)
    inv = _unit_lower_inverse(neg_l)
    e = jnp.exp(gc)
    solved = _bdot(inv, jnp.concatenate([kb * e, vb], axis=2), _BNN)
    g_last = gc[:, CHUNK - 1:CHUNK, :]
    k_dec = k * jnp.exp(g_last - gc)
    from_k = _bdot(k_dec, solved, _BTN)
    from_qk = _bdot(qk, solved, _BNN)
    step, add = -from_k[:, :, :HEAD_DIM], from_k[:, :, HEAD_DIM:]
    read, out = q * e - from_qk[:, :, :HEAD_DIM], from_qk[:, :, HEAD_DIM:]
    return step, add, jnp.exp(g_last), read, out


def _gdn_scan_step(state, step, add, decay_last):
    return state * decay_last + _bdot(step, state) + add


def _gdn_outputs(states, read, out, z, onw):
    return _rms(_bdot(read, states, _BNN) + out, onw) * _silu(z)


def _scan_scratch(n, dtype):
    return [pltpu.VMEM((n, HEAD_DIM, HEAD_DIM), dtype), pltpu.VMEM((n, HEAD_DIM, HEAD_DIM), F32), pltpu.VMEM((n, 1, HEAD_DIM), F32)]


def _head_lane(h, offset=0):
    return lax.broadcasted_iota(jnp.int32, (1, LANES), 1) == h + offset


def _pick(mask, x):
    return jnp.sum(jnp.where(mask, x, 0.0), axis=1, keepdims=True)


def _gdn_specs(heads, tb, rev, nb, PAIR):
    assert heads % PAIR == 0
    blk = (lambda i: nb - 1 - i) if rev else (lambda i: i)
    hb = tb // HALO
    width, pairs = PAIR * HEAD_DIM, heads // PAIR

    def col(group):
        return pl.BlockSpec((tb, width), lambda i, h: (blk(i), group * pairs + h))

    def halo(group):
        return pl.BlockSpec((HALO, width), lambda i, h: (jnp.maximum(blk(i) * hb - 1, 0), group * pairs + h))

    def convw(group):
        return pl.BlockSpec((CONV_K, width), lambda i, h: (0, group * pairs + h))

    vec = pl.BlockSpec((1, LANES), lambda i, h: (0, 0))
    ab = pl.BlockSpec((tb, LANES), lambda i, h: (blk(i), 0))
    states = pl.BlockSpec((PAIR, tb // CHUNK, HEAD_DIM, HEAD_DIM), lambda i, h: (h, blk(i), 0, 0))
    return blk, col, halo, convw, vec, ab, states


def _head_cols(p):
    return slice(p * HEAD_DIM, (p + 1) * HEAD_DIM)


def _gdn_fwd(proj, ab, conv_w, alog_row, dtb_row, onw_row, *, heads, name, tb=1024, pair=4, gather=()):
    t = proj.shape[0]
    tb = min(tb, t)
    nb, cpb = t // tb, tb // CHUNK
    PAIR = min(pair, heads)
    _, col, halo, convw, vec, abspec, states = _gdn_specs(heads, tb, False, nb, PAIR)

    def body(q_ref, k_ref, v_ref, qh_ref, kh_ref, vh_ref, z_ref, ab_ref, wq_ref, wk_ref, wv_ref, alog_ref, dtb_ref, onw_ref,
             og_ref, st_ref, state_scr, x_scr, *op_scr):
        i, pair = pl.program_id(0), pl.program_id(1)
        abv = ab_ref[...]
        heads_here, later = [pair * PAIR + p for p in range(PAIR)], []
        for p, h in enumerate(heads_here):
            cols = _head_cols(p)
            for n, (ref, href) in enumerate(((q_ref, qh_ref), (k_ref, kh_ref), (v_ref, vh_ref))):
                x_scr[p, n, 0:HALO, :] = jnp.where(i > 0, href[:, cols], 0.0)
                x_scr[p, n, HALO:HALO + tb, :] = ref[:, cols]
            sel_a, sel_b = _head_lane(h), _head_lane(h, heads)
            alog, dtb = _pick(sel_a, alog_ref[...]), _pick(sel_a, dtb_ref[...])
            acts = [_silu(_conv(x_scr.at[p, n], w_ref[:, cols], tb)) for n, w_ref in enumerate((wq_ref, wk_ref, wv_ref))]
            *scan, read, out = _gdn_intra(*acts, _pick(sel_a, abv), _pick(sel_b, abv), alog, dtb)
            for scr, val in zip(op_scr[3 * p:3 * p + 3], scan):
                scr[...] = val.astype(scr.dtype)
            later.append((read, out))

        def chunk(c, states):
            for p in range(PAIR):
                st_ref[p, c] = states[p]
            return tuple(_gdn_scan_step(states[p], *[scr[c] for scr in op_scr[3 * p:3 * p + 3]]) for p in range(PAIR))

        @pl.when(i == 0)
        def _():
            for h in heads_here:
                state_scr[h] = jnp.zeros((HEAD_DIM, HEAD_DIM), F32)

        last = lax.fori_loop(0, cpb, chunk, tuple(state_scr[h] for h in heads_here))
        for p, h in enumerate(heads_here):
            cols = _head_cols(p)
            state_scr[h] = last[p]
            og = _gdn_outputs(st_ref[p], *later[p], z_ref[:, cols].reshape(cpb, CHUNK, HEAD_DIM), onw_ref[...])
            og_ref[:, cols] = og.reshape(tb, HEAD_DIM).astype(BF16)

    n_x = len(gather)
    grid = (nb, heads // PAIR)
    og, st, *gathered = pl.pallas_call(
        _with_exchange(body, 14, 2, True, n_x, grid),
        name=name,
        grid=grid,
        in_specs=[col(0), col(1), col(2), halo(0), halo(1), halo(2), col(3), abspec, convw(0), convw(1), convw(2), vec, vec, vec]
        + [_ANY] * n_x,
        out_specs=[pl.BlockSpec((tb, PAIR * HEAD_DIM), lambda i, h: (i, h)), states] + [_ANY] * n_x,
        out_shape=[jax.ShapeDtypeStruct((t, heads * HEAD_DIM), BF16),
                   jax.ShapeDtypeStruct((heads, t // CHUNK, HEAD_DIM, HEAD_DIM), F32)] + _chip_shapes(True, gather),
        scratch_shapes=[pltpu.VMEM((heads, HEAD_DIM, HEAD_DIM), F32), pltpu.VMEM((PAIR, 3, HALO + tb, HEAD_DIM), F32)]
        + _scan_scratch(cpb, BF16) * PAIR + (_chip_scratch(n_x) if n_x else []),
        compiler_params=_params("arbitrary", "arbitrary"),
    )(proj, proj, proj, proj, proj, proj, proj, ab, conv_w, conv_w, conv_w, alog_row, dtb_row, onw_row, *gather)
    return og, st, gathered


def _gdn_bwd(proj, ab, conv_w, alog_row, dtb_row, onw_row, states, dog, *, heads, name, tb=1024, pair=2, exchange=()):
    t = proj.shape[0]
    tb = min(tb, t)
    nb, cpb = t // tb, tb // CHUNK
    PAIR = min(pair, heads)
    _, col, halo, convw, vec, abspec, states_spec = _gdn_specs(heads, tb, True, nb, PAIR)
    n_conv = conv_w.shape[1]

    def body(q_ref, k_ref, v_ref, qh_ref, kh_ref, vh_ref, z_ref, ab_ref, wq_ref, wk_ref, wv_ref, alog_ref, dtb_ref, onw_ref,
             st_ref, dog_ref, dproj_ref, dab_ref, dconv_ref, dalog_ref, ddtb_ref, donw_ref,
             dstate_scr, x_scr, carry_scr, *scr):
        op_scr, dop_scr, dstates_scr, dc_scr = scr[:3 * PAIR], scr[3 * PAIR:6 * PAIR], scr[6 * PAIR:7 * PAIR], scr[7 * PAIR]
        i, pair = pl.program_id(0), pl.program_id(1)
        first_block = i == nb - 1
        heads_here, later = [pair * PAIR + p for p in range(PAIR)], []

        @pl.when(jnp.logical_and(i == 0, pair == 0))
        def _():
            dconv_ref[...] = jnp.zeros_like(dconv_ref)
            dalog_ref[...] = jnp.zeros_like(dalog_ref)
            ddtb_ref[...] = jnp.zeros_like(ddtb_ref)
            donw_ref[...] = jnp.zeros_like(donw_ref)

        @pl.when(pair == 0)
        def _():
            dab_ref[...] = jnp.zeros_like(dab_ref)

        @pl.when(i == 0)
        def _():
            for h in heads_here:
                dstate_scr[h] = jnp.zeros((HEAD_DIM, HEAD_DIM), F32)
                carry_scr[h] = jnp.zeros((3, HALO, HEAD_DIM), F32)

        abv = ab_ref[...]
        w_refs = (wq_ref, wk_ref, wv_ref)
        for p, h in enumerate(heads_here):
            cols = _head_cols(p)
            for n, (ref, href) in enumerate(((q_ref, qh_ref), (k_ref, kh_ref), (v_ref, vh_ref))):
                x_scr[p, n, 0:HALO, :] = jnp.where(first_block, 0.0, href[:, cols])
                x_scr[p, n, HALO:HALO + tb, :] = ref[:, cols]
            sel_a, sel_b = _head_lane(h), _head_lane(h, heads)
            alog, dtb = _pick(sel_a, alog_ref[...]), _pick(sel_a, dtb_ref[...])
            acts = [_silu(_conv(x_scr.at[p, n], w_ref[:, cols], tb)) for n, w_ref in enumerate(w_refs)]
            (*scan, read, out), vjp_intra = jax.vjp(_gdn_intra, *acts, _pick(sel_a, abv), _pick(sel_b, abv), alog, dtb)
            for s, val in zip(op_scr[3 * p:3 * p + 3], scan):
                s[...] = val.astype(s.dtype)
            blocked = lambda ref: ref[:, cols].reshape(cpb, CHUNK, HEAD_DIM)
            _, vjp_outputs = jax.vjp(_gdn_outputs, st_ref[p], read, out, blocked(z_ref), onw_ref[...])
            dstates_scr[p][...], dread, dout, dz, donw = vjp_outputs(blocked(dog_ref))
            dproj_ref[3, :, cols] = dz.reshape(tb, HEAD_DIM).astype(BF16)
            donw_ref[...] += donw
            later.append((vjp_intra, dread, dout, sel_a, sel_b))

        def chunk(i_rev, dstates):
            c = cpb - 1 - i_rev
            new = []
            for p in range(PAIR):
                _, vjp = jax.vjp(_gdn_scan_step, st_ref[p, c], *[s[c].astype(F32) for s in op_scr[3 * p:3 * p + 3]])
                dstate, *grads = vjp(dstates[p])
                for s, val in zip(dop_scr[3 * p:3 * p + 3], grads):
                    s[c] = val
                new.append(dstate + dstates_scr[p][c])
            return tuple(new)

        last = lax.fori_loop(0, cpb, chunk, tuple(dstate_scr[h] for h in heads_here))
        for p, h in enumerate(heads_here):
            cols = _head_cols(p)
            vjp_intra, dread, dout, sel_a, sel_b = later[p]
            dstate_scr[h] = last[p]
            *dacts, da, db, dalog, ddtb = vjp_intra((*[s[...] for s in dop_scr[3 * p:3 * p + 3]], dread, dout))
            dab_ref[...] += jnp.where(sel_a, da, 0.0) + jnp.where(sel_b, db, 0.0)
            for n, (dact, w_ref) in enumerate(zip(dacts, w_refs)):
                dx, dw = _conv_silu_bwd(x_scr.at[p, n], w_ref[:, cols], dact, dc_scr, tb)
                x_scr[p, n] = dx
                x_scr[p, n, tb:tb + HALO, :] += carry_scr[h, n]
                carry_scr[h, n] = x_scr[p, n, 0:HALO, :]
                dproj_ref[n, :, cols] = x_scr[p, n, HALO:HALO + tb, :].astype(BF16)
                lanes = pl.ds(pl.multiple_of((n * heads + h) * HEAD_DIM, HEAD_DIM), HEAD_DIM)
                for j in range(CONV_K):
                    dconv_ref[j:j + 1, lanes] += dw[j]
            dalog_ref[...] += jnp.where(sel_a, dalog, 0.0)
            ddtb_ref[...] += jnp.where(sel_a, ddtb, 0.0)

    dog_spec = pl.BlockSpec((tb, PAIR * HEAD_DIM), lambda i, h: (nb - 1 - i, h))
    dproj_spec = pl.BlockSpec((4, tb, PAIR * HEAD_DIM), lambda i, h: (0, nb - 1 - i, h))
    row_shape = jax.ShapeDtypeStruct((1, LANES), F32)
    n_x = len(exchange)
    grid = (nb, heads // PAIR)
    outs = pl.pallas_call(
        _with_exchange(body, 16, 6, False, n_x, grid),
        name=name,
        grid=grid,
        in_specs=[col(0), col(1), col(2), halo(0), halo(1), halo(2), col(3), abspec, convw(0), convw(1), convw(2), vec, vec, vec,
                  states_spec, dog_spec] + [_ANY] * n_x,
        out_specs=[dproj_spec, abspec, pl.BlockSpec((CONV_K, n_conv), lambda i, h: (0, 0)), vec, vec, vec] + [_ANY] * n_x,
        out_shape=[jax.ShapeDtypeStruct((4, t, heads * HEAD_DIM), BF16), jax.ShapeDtypeStruct((t, LANES), F32),
                   jax.ShapeDtypeStruct((CONV_K, n_conv), F32), row_shape, row_shape, row_shape] + _chip_shapes(False, exchange),
        scratch_shapes=[pltpu.VMEM((heads, HEAD_DIM, HEAD_DIM), F32), pltpu.VMEM((PAIR, 3, HALO + tb, HEAD_DIM), F32),
                        pltpu.VMEM((heads, 3, HALO, HEAD_DIM), F32)] + _scan_scratch(cpb, BF16) * PAIR
        + _scan_scratch(cpb, F32) * PAIR + [pltpu.VMEM((cpb, HEAD_DIM, HEAD_DIM), F32)] * PAIR
        + [pltpu.VMEM((HALO + tb + HALO, HEAD_DIM), F32)]
        + (_chip_scratch(n_x) if n_x else []),
        compiler_params=_params("arbitrary", "arbitrary", vmem=VMEM_LIMIT_WIDE_BYTES),
    )(proj, proj, proj, proj, proj, proj, proj, ab, conv_w, conv_w, conv_w, alog_row, dtb_row, onw_row, states, dog, *exchange)
    return (*outs[:6], outs[6:])


BAND = (LEFT_CHUNKS + 1) * CHUNK
PAD = LEFT_CHUNKS * CHUNK
GROUP = 2
ROWS = GROUP * CHUNK
WIN = (LEFT_CHUNKS + GROUP) * CHUNK
DIAGS = WIN + ROWS - 1
NEAR = PAD + ROWS - 1 - REL_CLIP
assert 0 < NEAR < DIAGS and WIN - PAD - 1 <= REL_CLIP and WIN % LANES == 0
ATTN_BLOCK = 1024


def _band_bias(rel_bias):
    heads = rel_bias.shape[0]
    far = jnp.broadcast_to(rel_bias[:, 2 * REL_CLIP:], (heads, NEAR + 1))
    near = rel_bias[:, 2 * REL_CLIP + NEAR + 1 - DIAGS:2 * REL_CLIP][:, ::-1]
    diag = jnp.concatenate([far, near], axis=1)
    return jnp.stack([diag[:, ROWS - 1 - r:ROWS - 1 - r + WIN] for r in range(ROWS)], axis=1)


def _band_bias_grad(dbias):
    heads = dbias.shape[0]
    diag = sum(jnp.pad(dbias[:, r, :], ((0, 0), (ROWS - 1 - r, r))) for r in range(ROWS))
    far = jnp.sum(diag[:, :NEAR + 1], axis=1, keepdims=True)
    near = diag[:, NEAR + 1:][:, ::-1]
    unused = jnp.zeros((heads, 2 * REL_CLIP - near.shape[1]), F32)
    return jnp.concatenate([unused, near, far], axis=1)


def _masked_bias(bias, n):
    r = np.arange(ROWS)[:, None]
    key = np.arange(WIN)[None, :]
    band_start = (r // CHUNK) * CHUNK
    in_band = np.logical_and(key >= band_start, key < band_start + BAND)
    in_sequence = key[None] >= PAD - np.arange(n)[:, None, None] * ROWS
    first = jnp.where(np.logical_and(in_band[None], in_sequence)[None], bias[:, None], -1e30)
    return first, jnp.where(in_band[None, None], bias[:, None], -1e30)


def _attn_groups(q_pre, z, kn, v, bias, qnw):
    q = _rms(q_pre, qnw)
    s = _bdot(q, kn, _BNT) * (HEAD_DIM ** -0.5) + bias
    p = jnp.exp(s - jnp.max(s, axis=-1, keepdims=True))
    p = p / jnp.sum(p, axis=-1, keepdims=True)
    return _bdot(p, v, _BNN) * _silu(z)


def _attn_groups_bwd(q_pre, z, kn, v, bias, qnw, dog):
    scale = HEAD_DIM ** -0.5
    inv_rms = lax.rsqrt(jnp.mean(q_pre * q_pre, axis=-1, keepdims=True) + EPS)
    q_hat = q_pre * inv_rms
    q_b = (q_hat * qnw).astype(BF16)
    s = _dot(q_b, kn, _BNT) * scale + bias
    e = jnp.exp(s - jnp.max(s, axis=-1, keepdims=True))
    p = e * (1.0 / jnp.sum(e, axis=-1, keepdims=True))
    p_b = p.astype(BF16)
    o = _dot(p_b, v, _BNN)
    sig = jax.nn.sigmoid(z)
    do = dog * (z * sig)
    dz = dog * o * (sig * (1.0 + z * (1.0 - sig)))
    do_b = do.astype(BF16)
    dv = _dot(p_b, do_b, _BTN)
    dp = _dot(do_b, v, _BNT)
    ds = p * (dp - jnp.sum(do * o, axis=-1, keepdims=True))
    ds_b = (ds * scale).astype(BF16)
    dq = _dot(ds_b, kn, _BNN)
    dkn = _dot(ds_b, q_b, _BTN)
    dqnw = jnp.sum(jnp.sum(dq * q_hat, axis=0), axis=0, keepdims=True)
    dq_hat = dq * qnw
    dq_pre = inv_rms * (dq_hat - q_hat * jnp.mean(dq_hat * q_hat, axis=-1, keepdims=True))
    return dq_pre, dz, dkn, dv, jnp.sum(ds, axis=0), dqnw


def _attn_specs(heads, tb, t):
    def col(group):
        return pl.BlockSpec((tb, HEAD_DIM), lambda h, i: (i, group * heads + h))

    def full(group):
        return pl.BlockSpec((t, HEAD_DIM), lambda h, i: (0, group * heads + h))

    bias = [pl.BlockSpec((1, tb // ROWS, ROWS, WIN), lambda h, i: (h, 0, 0, 0)),
            pl.BlockSpec((1, 1, ROWS, WIN), lambda h, i: (h, 0, 0, 0))]
    vec = pl.BlockSpec((1, HEAD_DIM), lambda h, i: (0, 0))
    return col, full, bias, vec


def _attn_windows(scr, block_start, n):
    return jnp.stack([scr[pl.ds(pl.multiple_of(block_start + g * ROWS, ROWS), WIN), :] for g in range(n)])


def _attn_fill(k_ref, v_ref, knw_ref, kn_scr, v_scr, t):
    kn_scr[0:PAD, :] = jnp.zeros((PAD, HEAD_DIM), BF16)
    v_scr[0:PAD, :] = jnp.zeros((PAD, HEAD_DIM), BF16)
    step = min(512, t)

    def fill(j, _):
        rows = pl.ds(pl.multiple_of(j * step, step), step)
        prows = pl.ds(pl.multiple_of(PAD + j * step, CHUNK), step)
        kn_scr[prows, :] = _rms(k_ref[rows, :], knw_ref[...]).astype(BF16)
        v_scr[prows, :] = v_ref[rows, :].astype(BF16)
        return 0

    lax.fori_loop(0, t // step, fill, 0)


def _attn_fwd(proj, bias, qnw_row, knw_row, *, heads, name, tb=ATTN_BLOCK):
    t = proj.shape[0]
    tb = min(tb, t)
    nb, ng = t // tb, tb // ROWS
    col, full, bias_spec, vec = _attn_specs(heads, tb, t)

    def body(q_ref, k_ref, v_ref, z_ref, first_ref, rest_ref, qnw_ref, knw_ref, og_ref, kn_scr, v_scr):
        i = pl.program_id(1)

        @pl.when(i == 0)
        def _():
            _attn_fill(k_ref, v_ref, knw_ref, kn_scr, v_scr, t)

        def run(block_bias):
            start = i * tb
            og = _attn_groups(q_ref[...].reshape(ng, ROWS, HEAD_DIM), z_ref[...].reshape(ng, ROWS, HEAD_DIM),
                              _attn_windows(kn_scr, start, ng), _attn_windows(v_scr, start, ng), block_bias, qnw_ref[...])
            og_ref[...] = og.reshape(tb, HEAD_DIM).astype(BF16)

        pl.when(i == 0)(lambda: run(first_ref[0]))
        pl.when(i > 0)(lambda: run(rest_ref[0]))

    return pl.pallas_call(
        body,
        name=name,
        grid=(heads, nb),
        in_specs=[col(0), full(1), full(2), col(3), *bias_spec, vec, vec],
        out_specs=pl.BlockSpec((tb, HEAD_DIM), lambda h, i: (i, h)),
        out_shape=jax.ShapeDtypeStruct((t, heads * HEAD_DIM), BF16),
        scratch_shapes=[pltpu.VMEM((PAD + t, HEAD_DIM), BF16), pltpu.VMEM((PAD + t, HEAD_DIM), BF16)],
        compiler_params=_params("arbitrary", "arbitrary"),
    )(proj, proj, proj, proj, *bias, qnw_row, knw_row)


def _attn_bwd(proj, bias, qnw_row, knw_row, dog, *, heads, name, tb=ATTN_BLOCK, sub=4):
    t = proj.shape[0]
    tb = min(tb, t)
    nb, ng = t // tb, tb // ROWS
    sub = min(sub, ng)
    col, full, bias_spec, vec = _attn_specs(heads, tb, t)

    def body(q_ref, k_ref, v_ref, z_ref, first_ref, rest_ref, qnw_ref, knw_ref, dog_ref,
             dqz_ref, dkv_ref, dbias_ref, dqnw_ref, dknw_ref, kn_scr, v_scr, dkn_scr, dv_scr):
        i = pl.program_id(1)

        @pl.when(i == 0)
        def _():
            _attn_fill(k_ref, v_ref, knw_ref, kn_scr, v_scr, t)
            dkn_scr[...] = jnp.zeros_like(dkn_scr)
            dv_scr[...] = jnp.zeros_like(dv_scr)
            dbias_ref[...] = jnp.zeros_like(dbias_ref)
            dqnw_ref[...] = jnp.zeros_like(dqnw_ref)

        def run(block_bias):
            for g0 in range(0, ng, sub):
                rows = pl.ds(g0 * ROWS, sub * ROWS)
                at = i * tb + g0 * ROWS
                blocked = lambda ref: ref[rows, :].reshape(sub, ROWS, HEAD_DIM)
                dq, dz, dkn, dv, dbias, dqnw = _attn_groups_bwd(
                    blocked(q_ref), blocked(z_ref), _attn_windows(kn_scr, at, sub), _attn_windows(v_scr, at, sub),
                    block_bias(g0), qnw_ref[...], blocked(dog_ref))
                dqz_ref[0, rows, :] = dq.reshape(sub * ROWS, HEAD_DIM).astype(BF16)
                dqz_ref[1, rows, :] = dz.reshape(sub * ROWS, HEAD_DIM).astype(BF16)
                for g in range(sub):
                    window = pl.ds(pl.multiple_of(at + g * ROWS, ROWS), WIN)
                    dkn_scr[window, :] += dkn[g]
                    dv_scr[window, :] += dv[g]
                dbias_ref[0] += dbias
                dqnw_ref[0] += dqnw

        pl.when(i == 0)(lambda: run(lambda g0: first_ref[0, g0:g0 + sub]))
        pl.when(i > 0)(lambda: run(lambda g0: rest_ref[0]))

        @pl.when(i == nb - 1)
        def _():
            step = min(512, t)

            def finish(j, dknw):
                rows = pl.ds(pl.multiple_of(j * step, step), step)
                prows = pl.ds(pl.multiple_of(PAD + j * step, CHUNK), step)
                _, vjp = jax.vjp(_rms, k_ref[rows, :], knw_ref[...])
                dk, dw = vjp(dkn_scr[prows, :])
                dkv_ref[0, rows, :] = dk.astype(BF16)
                dkv_ref[1, rows, :] = dv_scr[prows, :].astype(BF16)
                return dknw + dw

            dknw_ref[0] = lax.fori_loop(0, t // step, finish, jnp.zeros((1, HEAD_DIM), F32))

    pair_col = pl.BlockSpec((2, tb, HEAD_DIM), lambda h, i: (0, i, h))
    pair_full = pl.BlockSpec((2, t, HEAD_DIM), lambda h, i: (0, 0, h))
    head_vec = pl.BlockSpec((1, 1, HEAD_DIM), lambda h, i: (h, 0, 0))
    pair_shape = jax.ShapeDtypeStruct((2, t, heads * HEAD_DIM), BF16)
    vec_shape = jax.ShapeDtypeStruct((heads, 1, HEAD_DIM), F32)
    return pl.pallas_call(
        body,
        name=name,
        grid=(heads, nb),
        in_specs=[col(0), full(1), full(2), col(3), *bias_spec, vec, vec, pl.BlockSpec((tb, HEAD_DIM), lambda h, i: (i, h))],
        out_specs=[pair_col, pair_full, pl.BlockSpec((1, ROWS, WIN), lambda h, i: (h, 0, 0)), head_vec, head_vec],
        out_shape=[pair_shape, pair_shape, jax.ShapeDtypeStruct((heads, ROWS, WIN), F32), vec_shape, vec_shape],
        scratch_shapes=[pltpu.VMEM((PAD + t, HEAD_DIM), BF16), pltpu.VMEM((PAD + t, HEAD_DIM), BF16),
                        pltpu.VMEM((PAD + t, HEAD_DIM), F32), pltpu.VMEM((PAD + t, HEAD_DIM), F32)],
        compiler_params=_params("arbitrary", "arbitrary"),
    )(proj, proj, proj, proj, *bias, qnw_row, knw_row, dog)


def _lane_row(v):
    v = v.reshape(1, -1)
    return jnp.pad(v, ((0, 0), (0, LANES - v.shape[1])))


def _local_step(x, target, norm_w, wa_in, conv_w, a_log, dt_bias, onw, wa_out, wb_in, qnw, knw, rel_bias, wb_out, *,
                sharded=False):
    ha, hb = a_log.shape[-1], rel_bias.shape[-2]
    na = 4 * ha * HEAD_DIM
    wa_main = wa_in[:, :na]
    wa_ab = jnp.pad(wa_in[:, na:], ((0, 0), (0, LANES - 2 * ha)))
    alog_row, dtb_row, onw_row = _lane_row(a_log), _lane_row(dt_bias), _lane_row(onw)
    qnw_row, knw_row = _lane_row(qnw), _lane_row(knw)
    bias = _masked_bias(_band_bias(rel_bias.reshape(hb, -1)), min(ATTN_BLOCK, x.shape[0]) // ROWS)

    hn0 = _rmsnorm_fwd(x, norm_w[0:1], name="norm0")
    proj_a = _matmul(hn0, wa_main, name="a_in")
    ab_a = _matmul(hn0, wa_ab, name="a_in_ab")
    og_a, states, got = _gdn_fwd(proj_a, ab_a, conv_w, alog_row, dtb_row, onw_row, heads=ha, name="gdn_fwd",
                                 gather=[wb_in, wa_out, wb_out] if sharded else [])
    if sharded:
        wb_in, wa_out, wb_out = _join_cols(got[0]), got[1].reshape(-1, got[1].shape[-1]), got[2].reshape(-1, got[2].shape[-1])
    h1 = _matmul(og_a, wa_out, residual=x, name="a_out")
    hn1 = _rmsnorm_fwd(h1, norm_w[1:2], name="norm1")
    proj_b = _matmul(hn1, wb_in, name="b_in")
    og_b = _attn_fwd(proj_b, bias, qnw_row, knw_row, heads=hb, name="attn_fwd")
    loss, dh2, dh2_b = _matmul_loss(og_b, wb_out, h1, target, name="b_out_loss")

    grad_dtype = BF16 if sharded else F32
    dog_b = _matmul(dh2_b, wb_out, trans_b=True, name="d_b_out_x")
    dwb_out = _matmul(og_b, dh2_b, trans_a=True, out_dtype=grad_dtype, name="d_b_out_w")
    dqz, dkv, dbias, dqnw, dknw = _attn_bwd(proj_b, bias, qnw_row, knw_row, dog_b, heads=hb, name="attn_bwd")
    dproj_b, qkvz = [dqz, dkv], (0, 3, 1, 2)
    dhn1 = _matmul(dproj_b, wb_in, trans_b=True, order=qkvz, name="d_b_in_x")
    dwb_in = _matmul(hn1, dproj_b, trans_a=True, order=qkvz, out_dtype=grad_dtype, col_slabs=N_CHIPS if sharded else 0,
                     name="d_b_in_w")
    dh1, dh1_b, dnw1 = _rmsnorm_bwd(h1, norm_w[1:2], dhn1, dh2, name="d_norm1")

    dog_a = _matmul(dh1_b, wa_out, trans_b=True, name="d_a_out_x")
    dwa_out = _matmul(og_a, dh1_b, trans_a=True, out_dtype=grad_dtype, name="d_a_out_w")
    early = [dwb_in, _split_rows(dwa_out), _split_rows(dwb_out)] if sharded else []
    dproj_a, dab, dconv, dalog, ddtb, donw, landed = _gdn_bwd(
        proj_a, ab_a, conv_w, alog_row, dtb_row, onw_row, states, dog_a, heads=ha, name="gdn_bwd", exchange=early)
    if sharded:
        dwb_in, dwa_out, dwb_out = landed
    dab_b = dab.astype(BF16)
    dwa_in = jnp.concatenate(
        [_matmul(hn0, dproj_a, trans_a=True, out_dtype=grad_dtype, name="d_a_in_w"),
         _matmul(hn0, dab_b, trans_a=True, out_dtype=grad_dtype, name="d_a_in_ab_w")[:, :2 * ha]], axis=1)
    if sharded:
        dhn0, (dwa_in, dconv) = _matmul(dproj_a, wa_main, trans_b=True, name="d_a_in_x",
                                        exchange=[_split_cols(dwa_in), _split_cols(dconv)])
    else:
        dhn0 = _matmul(dproj_a, wa_main, trans_b=True, name="d_a_in_x")
    dx, _, dnw0 = _rmsnorm_bwd(x, norm_w[0:1], dhn0, dh1, narrow=(dab_b, wa_ab), name="d_norm0")

    drel = _band_bias_grad(dbias)
    grads = dict(
        norm_w=jnp.concatenate([dnw0, dnw1], axis=0), a_w_in=dwa_in, a_conv_w=dconv, a_a_log=dalog[:, :ha],
        a_dt_bias=ddtb[:, :ha], a_out_norm_w=donw, a_w_out=dwa_out, b_w_in=dwb_in, b_q_norm_w=jnp.sum(dqnw, axis=0),
        b_k_norm_w=jnp.sum(dknw, axis=0), b_rel_bias=drel[None], b_w_out=dwb_out)
    return loss, dx, grads


_ANY = pl.BlockSpec(memory_space=pl.ANY)
_CHIP_FLIPS = ((1, 0), (0, 1), (1, 1))


def _place():
    x, y, c = lax.axis_index("x"), lax.axis_index("y"), lax.axis_index("c")
    return x, y, c


def _flip(v, bit):
    return 1 - v if bit else v


def _remote(src, dst, send_sem, recv_sem, peer):
    return pltpu.make_async_remote_copy(src_ref=src, dst_ref=dst, send_sem=send_sem, recv_sem=recv_sem, device_id=peer,
                                        device_id_type=MESH)


def _comm_call(body, arrays, out_shapes, n_remote, n_local, name):
    scratch = [pltpu.SemaphoreType.DMA((n_remote,)), pltpu.SemaphoreType.DMA((n_remote,))]
    if n_local:
        scratch.append(pltpu.SemaphoreType.DMA((n_local,)))
    return pl.pallas_call(
        body, name=name, in_specs=[_ANY] * len(arrays), out_specs=[_ANY] * len(out_shapes), out_shape=out_shapes,
        scratch_shapes=scratch)(*arrays)


def _chip_scratch(n):
    return [pltpu.SemaphoreType.DMA((3 * n,)), pltpu.SemaphoreType.DMA((3 * n,)), pltpu.SemaphoreType.DMA((n,))]


def _chip_shapes(gather, arrays):
    return [jax.ShapeDtypeStruct(((N_CHIPS,) + s.shape) if gather else s.shape, s.dtype) for s in arrays]


def _chip_traffic(gather, ins, outs, sems):
    send_sems, recv_sems, local_sems = sems
    x, y, c = _place()
    mine = 2 * x + y
    local, remote, landing = [], [], []
    for a in range(len(ins)):
        local.append(pltpu.make_async_copy(ins[a] if gather else ins[a].at[mine], outs[a].at[mine], local_sems.at[a]))
        for k, (fx, fy) in enumerate(_CHIP_FLIPS):
            peer = (_flip(x, fx), _flip(y, fy), c)
            theirs = 2 * peer[0] + peer[1]
            src = ins[a] if gather else ins[a].at[theirs]
            pair = send_sems.at[3 * a + k], recv_sems.at[3 * a + k]
            remote.append(_remote(src, outs[a].at[mine], *pair, peer))
            landing.append(_remote(src, outs[a].at[theirs], *pair, peer))
    return local + remote, (local, landing, remote)


def _start(traffic):
    for cp in traffic[0]:
        cp.start()


def _finish(traffic):
    local, landing, remote = traffic[1]
    for cp in local:
        cp.wait()
    for cp in landing:
        cp.wait_recv()
    for cp in remote:
        cp.wait_send()


def _with_exchange(compute, n_in, n_out, gather, n_x, grid):
    if not n_x:
        return compute

    def body(*refs):
        ins, x_in = refs[:n_in], refs[n_in:n_in + n_x]
        outs, x_out = refs[n_in + n_x:n_in + n_x + n_out], refs[n_in + n_x + n_out:n_in + 2 * n_x + n_out]
        scratch, sems = refs[n_in + 2 * n_x + n_out:-3], refs[-3:]
        traffic = _chip_traffic(gather, x_in, x_out, sems)
        first = functools.reduce(jnp.logical_and, [pl.program_id(d) == 0 for d in range(len(grid))])
        last = functools.reduce(jnp.logical_and, [pl.program_id(d) == grid[d] - 1 for d in range(len(grid))])

        @pl.when(first)
        def _():
            _start(traffic)

        compute(*ins, *outs, *scratch)

        @pl.when(last)
        def _():
            _finish(traffic)

    return body


def _gather_shared(shard, small, *, name):
    rows = shard.shape[0]
    assert rows % 2 == 0
    half = rows // 2

    def body(shard_ref, small_ref, out_ref, small_out_ref, send_sems, recv_sems, local_sems):
        x, y, c = _place()
        mine = 2 * x + y
        sibling = (x, y, 1 - c)
        my_rows = pl.ds(pl.multiple_of(c * half, 8), half)
        local = [pltpu.make_async_copy(shard_ref, out_ref.at[mine], local_sems.at[0]),
                 pltpu.make_async_copy(small_ref, small_out_ref.at[mine], local_sems.at[1])]
        sent, landed, passed_on, handed = [], [], [], []
        for k, (fx, fy) in enumerate(_CHIP_FLIPS):
            peer = (_flip(x, fx), _flip(y, fy), c)
            theirs = 2 * peer[0] + peer[1]
            ici, d2d, tiny = [(send_sems.at[3 * n + k], recv_sems.at[3 * n + k]) for n in range(3)]
            sent.append(_remote(shard_ref.at[my_rows], out_ref.at[mine, my_rows], *ici, peer))
            landed.append(_remote(shard_ref.at[my_rows], out_ref.at[theirs, my_rows], *ici, peer))
            sent.append(_remote(small_ref, small_out_ref.at[mine], *tiny, peer))
            landed.append(_remote(small_ref, small_out_ref.at[theirs], *tiny, peer))
            passed_on.append(_remote(out_ref.at[theirs, my_rows], out_ref.at[theirs, my_rows], *d2d, sibling))
            other_rows = pl.ds(pl.multiple_of((1 - c) * half, 8), half)
            handed.append(_remote(out_ref.at[theirs, other_rows], out_ref.at[theirs, other_rows], *d2d, sibling))
        for cp in local + sent:
            cp.start()
        for k in range(3):
            landed[2 * k].wait_recv()
            passed_on[k].start()
        for k in range(3):
            landed[2 * k + 1].wait_recv()
            handed[k].wait_recv()
        for cp in local:
            cp.wait()
        for cp in sent + passed_on:
            cp.wait_send()

    return pl.pallas_call(
        body, name=name, in_specs=[_ANY] * 2, out_specs=[_ANY] * 2, out_shape=_chip_shapes(True, [shard, small]),
        scratch_shapes=[pltpu.SemaphoreType.DMA((9,)), pltpu.SemaphoreType.DMA((9,)), pltpu.SemaphoreType.DMA((2,))],
    )(shard, small)


def _swap_pair(arrays, *, name):
    n = len(arrays)

    def body(*refs):
        ins, outs, (send_sems, recv_sems) = refs[:n], refs[n:2 * n], refs[2 * n:]
        x, y, c = _place()
        copies = [_remote(ins[a], outs[a], send_sems.at[a], recv_sems.at[a], (x, y, 1 - c)) for a in range(n)]
        for cp in copies:
            cp.start()
        for cp in copies:
            cp.wait_recv()
        for cp in copies:
            cp.wait_send()

    shapes = [jax.ShapeDtypeStruct(s.shape, s.dtype) for s in arrays]
    return _comm_call(body, arrays, shapes, n, 0, name)


def _gather_all(tile, *, name):
    def body(in_ref, out_ref, send_sems, recv_sems, local_sems):
        x, y, c = _place()
        mine = 4 * x + 2 * y + c
        local = pltpu.make_async_copy(in_ref, out_ref.at[mine], local_sems.at[0])
        remote, landing = [], []
        for k in range(1, N_DEV):
            peer = (_flip(x, k & 4), _flip(y, k & 2), _flip(c, k & 1))
            sems = send_sems.at[k - 1], recv_sems.at[k - 1]
            remote.append(_remote(in_ref, out_ref.at[mine], *sems, peer))
            landing.append(_remote(in_ref, out_ref.at[4 * peer[0] + 2 * peer[1] + peer[2]], *sems, peer))
        for cp in [local] + remote:
            cp.start()
        local.wait()
        for cp in landing:
            cp.wait_recv()
        for cp in remote:
            cp.wait_send()

    return _comm_call(body, [tile], [jax.ShapeDtypeStruct((N_DEV,) + tile.shape, tile.dtype)], N_DEV - 1, 1, name)[0]


def _sum_slots(slabs, *, name, tr=128):
    s, r, c = slabs.shape
    tr = min(tr, r)

    def body(in_ref, o_ref):
        acc = in_ref[0].astype(F32)
        for j in range(1, s):
            acc = acc + in_ref[j].astype(F32)
        o_ref[...] = acc

    return pl.pallas_call(
        body, name=name, grid=(r // tr,),
        in_specs=[pl.BlockSpec((s, tr, c), lambda i: (0, i, 0))], out_specs=pl.BlockSpec((tr, c), lambda i: (i, 0)),
        out_shape=jax.ShapeDtypeStruct((r, c), F32), compiler_params=_params("parallel"))(slabs)


def _adamw_math(w, g, m, v):
    m = ADAM_B1 * m + (1.0 - ADAM_B1) * g
    v = ADAM_B2 * v + (1.0 - ADAM_B2) * (g * g)
    m_hat = m / (1.0 - ADAM_B1 ** ADAM_STEP)
    v_hat = v / (1.0 - ADAM_B2 ** ADAM_STEP)
    delta = -ADAM_LR * (m_hat / (jnp.sqrt(v_hat) + ADAM_EPS) + ADAM_WD * w)
    return delta, m, v


def _adamw(w, m, v, parts, *, name, tr=128):
    r, c = w.shape
    tr = min(tr, r)
    s = len(parts)

    def body(w_ref, m_ref, v_ref, *refs):
        g_ref, d_ref, nm_ref, nv_ref = refs[s:]
        g = refs[0][...]
        for p_ref in refs[1:s]:
            g = g + p_ref[...]
        g_ref[...] = g
        d_ref[...], nm_ref[...], nv_ref[...] = _adamw_math(w_ref[...], g, m_ref[...], v_ref[...])

    blk = pl.BlockSpec((tr, c), lambda i: (i, 0))
    shape = jax.ShapeDtypeStruct((r, c), F32)
    return pl.pallas_call(
        body, name=name, grid=(r // tr,), in_specs=[blk] * (3 + s), out_specs=[blk] * 4, out_shape=[shape] * 4,
        compiler_params=_params("parallel"))(w, m, v, *parts)


_BIG = ("a_w_in", "b_w_in", "a_w_out", "b_w_out", "a_conv_w")
_SMALL = ("norm_w", "a_a_log", "a_dt_bias", "a_out_norm_w", "b_q_norm_w", "b_k_norm_w", "b_rel_bias")
_ORDER = ("norm_w", "a_w_in", "a_conv_w", "a_a_log", "a_dt_bias", "a_out_norm_w", "a_w_out", "b_w_in", "b_q_norm_w",
          "b_k_norm_w", "b_rel_bias", "b_w_out")


def _join_cols(g):
    return jnp.transpose(g, (1, 0, 2)).reshape(g.shape[1], -1)


def _split_cols(g):
    return jnp.transpose(g.reshape(g.shape[0], N_CHIPS, -1), (1, 0, 2))


def _split_rows(g):
    return g.reshape(N_CHIPS, -1, g.shape[-1])


def _pack(d):
    flat = jnp.concatenate([d[n].reshape(-1) for n in _SMALL])
    return jnp.pad(flat, (0, -flat.shape[0] % LANES)).reshape(1, -1)


def _unpack(row, like):
    out, at = {}, 0
    for n in _SMALL:
        size = like[n].size
        out[n] = row[0, at:at + size].reshape(like[n].shape)
        at += size
    return out


def kernel(x, norm_w, a_w_in, a_conv_w, a_a_log, a_dt_bias, a_out_norm_w, a_w_out, b_w_in, b_q_norm_w, b_k_norm_w, b_rel_bias, b_w_out, loss_target, m_norm_w, m_a_w_in, m_a_conv_w, m_a_a_log, m_a_dt_bias, m_a_out_norm_w, m_a_w_out, m_b_w_in, m_b_q_norm_w, m_b_k_norm_w, m_b_rel_bias, m_b_w_out, v_norm_w, v_a_w_in, v_a_conv_w, v_a_a_log, v_a_dt_bias, v_a_out_norm_w, v_a_w_out, v_b_w_in, v_b_q_norm_w, v_b_k_norm_w, v_b_rel_bias, v_b_w_out):
    w = dict(norm_w=norm_w, a_w_in=a_w_in, a_conv_w=a_conv_w, a_a_log=a_a_log, a_dt_bias=a_dt_bias,
             a_out_norm_w=a_out_norm_w, a_w_out=a_w_out, b_w_in=b_w_in, b_q_norm_w=b_q_norm_w, b_k_norm_w=b_k_norm_w,
             b_rel_bias=b_rel_bias, b_w_out=b_w_out)
    m = dict(norm_w=m_norm_w, a_w_in=m_a_w_in, a_conv_w=m_a_conv_w, a_a_log=m_a_a_log, a_dt_bias=m_a_dt_bias,
             a_out_norm_w=m_a_out_norm_w, a_w_out=m_a_w_out, b_w_in=m_b_w_in, b_q_norm_w=m_b_q_norm_w,
             b_k_norm_w=m_b_k_norm_w, b_rel_bias=m_b_rel_bias, b_w_out=m_b_w_out)
    v = dict(norm_w=v_norm_w, a_w_in=v_a_w_in, a_conv_w=v_a_conv_w, a_a_log=v_a_a_log, a_dt_bias=v_a_dt_bias,
             a_out_norm_w=v_a_out_norm_w, a_w_out=v_a_w_out, b_w_in=v_b_w_in, b_q_norm_w=v_b_q_norm_w,
             b_k_norm_w=v_b_k_norm_w, b_rel_bias=v_b_rel_bias, b_w_out=v_b_w_out)

    wa_in, conv = _gather_shared(a_w_in[0].astype(BF16), a_conv_w[0], name="gather_a_in")
    loss, dx, grads = _local_step(
        x[0], loss_target[0], norm_w, _join_cols(wa_in), _join_cols(conv), a_a_log, a_dt_bias, a_out_norm_w,
        a_w_out[0].astype(BF16), b_w_in[0].astype(BF16), b_q_norm_w, b_k_norm_w, b_rel_bias, b_w_out[0].astype(BF16),
        sharded=True)
    loss = lax.psum(loss, ("x", "y", "c"))

    mine = [_sum_slots(grads[n], name=f"chip_sum_{n}") for n in _BIG]
    theirs = _swap_pair(mine, name="pair_grads")
    out = {}
    for n, p, q in zip(_BIG, mine, theirs):
        out[n] = [r[None] for r in _adamw(w[n][0], m[n][0], v[n][0], [p, q], name=f"adamw_{n}")]

    row = _pack(grads)
    tiles = _gather_all(jnp.broadcast_to(row, (8, row.shape[1])), name="gather_small_grads")
    res = _adamw(_pack(w), _pack(m), _pack(v), [tiles[d, 0:1, :] for d in range(N_DEV)], name="adamw_small")
    unpacked = [_unpack(r, w) for r in res]
    for n in _SMALL:
        out[n] = [u[n] for u in unpacked]

    return (loss, dx[None], *[out[n][0] for n in _ORDER], *[out[n][1] for n in _ORDER], *[out[n][2] for n in _ORDER],
            *[out[n][3] for n in _ORDER])
```

```python
import functools

import numpy as np
import jax
import jax.numpy as jnp
from jax import lax
from jax.experimental import pallas as pl
from jax.experimental.pallas import tpu as pltpu

F32 = jnp.float32
BF16 = jnp.bfloat16

CHUNK = 64
HEAD_DIM = 128
LEFT_CHUNKS = 8
REL_CLIP = 256
CONV_K = 4
EPS = 1e-6
HALO = 8

ADAM_LR = 0.001
ADAM_B1 = 0.9
ADAM_B2 = 0.999
ADAM_EPS = 1e-08
ADAM_WD = 0.01
ADAM_STEP = 10

LANES = 128
N_CHIPS = 4
N_DEV = 8
VMEM_LIMIT_BYTES = 56 * 1024 * 1024
VMEM_LIMIT_WIDE_BYTES = 63 * 1024 * 1024
MESH = pl.DeviceIdType.MESH


def _params(*sem, vmem=VMEM_LIMIT_BYTES):
    return pltpu.CompilerParams(dimension_semantics=sem, vmem_limit_bytes=vmem)


def _dot(a, b, dims=(((1,), (0,)), ((), ())), precision=None):
    return lax.dot_general(a, b, dims, precision=precision, preferred_element_type=F32)


_NT = (((1,), (1,)), ((), ()))
_TN = (((0,), (0,)), ((), ()))


def _bdot(a, b, dims=(((1,), (0,)), ((), ()))):
    return _dot(a.astype(BF16), b.astype(BF16), dims)


def _fdot(a, b, dims=(((1,), (0,)), ((), ()))):
    return _dot(a, b, dims, precision=lax.Precision.HIGH)


def _silu(x):
    return x * jax.nn.sigmoid(x)


def _stacks(x):
    if not isinstance(x, (list, tuple)) and x.ndim != 3:
        return None
    arrays = list(x) if isinstance(x, (list, tuple)) else [x]
    assert len({(v.shape[1], v.shape[2], v.dtype) for v in arrays}) == 1
    starts = [sum(v.shape[0] for v in arrays[:r]) for r in range(len(arrays))]
    return arrays, starts, starts[-1] + arrays[-1].shape[0]


def _static_pick(table, index):
    out = table[-1]
    for s in range(len(table) - 2, -1, -1):
        out = jnp.where(index == s, table[s], out)
    return out


def _matmul(a, b, *, name, trans_a=False, trans_b=False, residual=None, out_dtype=F32, tm=1024, tn=1024, tk=2048,
            col_slabs=0, order=None, exchange=(), with_pair=False):
    assert not (trans_a and trans_b)
    a_stack, b_stack = _stacks(a), _stacks(b)
    assert not (a_stack and (trans_a or b_stack)) and not (b_stack and trans_b)
    a_list, b_list = (a_stack[0] if a_stack else [a]), (b_stack[0] if b_stack else [b])
    a0, b0 = a_list[0], b_list[0]
    k, m = (a_stack[2] * a0.shape[2], a0.shape[1]) if a_stack else a.shape if trans_a else a.shape[::-1]
    n = b_stack[2] * b0.shape[2] if b_stack else b.shape[0] if trans_b else b.shape[1]
    tm, tn, tk = min(tm, m), min(tn, n // max(col_slabs, 1)), min(tk, k)
    if a_stack:
        tk = min(tk, a0.shape[2])
        per_k = a0.shape[2] // tk
    if b_stack:
        tn = min(tn, b0.shape[2])
        per_n = b0.shape[2] // tn
    assert m % tm == 0 and n % tn == 0 and k % tk == 0, (a0.shape, b0.shape, tm, tn, tk)
    nk = k // tk
    dims = _NT if trans_b else _TN if trans_a else (((1,), (0,)), ((), ()))
    order = list(order) if order is not None else list(range(max(a_stack[2] if a_stack else 0, b_stack[2] if b_stack else 0)))
    na, nb = len(a_list), len(b_list)

    def group_of(r, stack, position):
        arrays, starts, _ = stack
        local = position - starts[r]
        return jnp.logical_and(local >= 0, local < arrays[r].shape[0]), jnp.clip(local, 0, arrays[r].shape[0] - 1)

    def body(*refs):
        a_refs, b_refs = refs[:na], refs[na:na + nb]
        r_ref = refs[na + nb] if residual is not None else None
        o_ref, acc_ref = refs[-2:]
        j, kk = pl.program_id(1), pl.program_id(2)

        @pl.when(kk == 0)
        def _():
            acc_ref[...] = jnp.zeros_like(acc_ref)

        for ra, a_ref in enumerate(a_refs):
            for rb, b_ref in enumerate(b_refs):
                def add(a_ref=a_ref, b_ref=b_ref):
                    acc_ref[...] += _dot(a_ref[...], b_ref[...], dims)

                if na > 1:
                    pl.when(group_of(ra, a_stack, kk // per_k)[0])(add)
                elif nb > 1:
                    pl.when(group_of(rb, b_stack, j // per_n)[0])(add)
                else:
                    add()

        @pl.when(kk == nk - 1)
        def _():
            r = acc_ref[...]
            if r_ref is not None:
                r = r + r_ref[...]
            o_ref[...] = r.astype(o_ref.dtype)

    if a_stack:
        a_specs = [pl.BlockSpec((None, tm, tk), lambda i, j, kk, r=r: (group_of(r, a_stack, kk // per_k)[1], i, kk % per_k))
                   for r in range(na)]
        b_k = lambda kk: _static_pick(order, kk // per_k) * per_k + kk % per_k
    else:
        a_specs = [pl.BlockSpec((tk, tm), lambda i, j, kk: (kk, i)) if trans_a else pl.BlockSpec((tm, tk), lambda i, j, kk: (i, kk))]
        b_k = lambda kk: kk
    if b_stack:
        b_specs = [pl.BlockSpec((None, tk, tn), lambda i, j, kk, r=r: (group_of(r, b_stack, j // per_n)[1], kk, j % per_n))
                   for r in range(nb)]
        out_col = lambda j: _static_pick(order, j // per_n) * per_n + j % per_n
    else:
        b_specs = [pl.BlockSpec((tn, tk), lambda i, j, kk: (j, b_k(kk))) if trans_b
                   else pl.BlockSpec((tk, tn), lambda i, j, kk: (b_k(kk), j))]
        out_col = lambda j: j
    in_specs = a_specs + b_specs
    args = a_list + b_list
    if residual is not None:
        in_specs.append(pl.BlockSpec((tm, tn), lambda i, j, kk: (i, j)))
        args.append(residual)
    grid = (m // tm, n // tn, nk)
    n_x = len(exchange)
    if col_slabs:
        per = n // col_slabs // tn
        assert per * tn * col_slabs == n, (n, tn, col_slabs)
        out_spec = pl.BlockSpec((None, tm, tn), lambda i, j, kk: (out_col(j) // per, i, out_col(j) % per))
        out_shape = jax.ShapeDtypeStruct((col_slabs, m, n // col_slabs), out_dtype)
    else:
        out_spec = pl.BlockSpec((tm, tn), lambda i, j, kk: (i, out_col(j)))
        out_shape = jax.ShapeDtypeStruct((m, n), out_dtype)
    out, *landed = pl.pallas_call(
        _with_exchange(body, len(args), 1, "pair" if with_pair else False, n_x, grid),
        name=name,
        grid=grid,
        in_specs=in_specs + [_ANY] * n_x,
        out_specs=[out_spec] + [_ANY] * n_x,
        out_shape=[out_shape] + _chip_shapes(False, exchange),
        scratch_shapes=[pltpu.VMEM((tm, tn), F32)] + ((_pair_scratch if with_pair else _chip_scratch)(n_x) if n_x else []),
        compiler_params=_params(*(("arbitrary",) * 3 if n_x else ("parallel", "parallel", "arbitrary"))),
    )(*args, *exchange)
    return (out, landed) if n_x else out


def _rms(x, w):
    return x * lax.rsqrt(jnp.mean(x * x, axis=-1, keepdims=True) + EPS) * w


def _rmsnorm_fwd(x, w_row, narrow_w, *, name, tr=512):
    t, d = x.shape
    tr = min(tr, t)

    def body(x_ref, w_ref, nw_ref, o_ref, narrow_ref):
        hn = _rms(x_ref[...], w_ref[...]).astype(BF16)
        o_ref[...] = hn
        narrow_ref[...] = _dot(hn, nw_ref[...])

    return pl.pallas_call(
        body,
        name=name,
        grid=(t // tr,),
        in_specs=[pl.BlockSpec((tr, d), lambda i: (i, 0)), pl.BlockSpec((1, d), lambda i: (0, 0)),
                  pl.BlockSpec((d, LANES), lambda i: (0, 0))],
        out_specs=[pl.BlockSpec((tr, d), lambda i: (i, 0)), pl.BlockSpec((tr, LANES), lambda i: (i, 0))],
        out_shape=[jax.ShapeDtypeStruct((t, d), BF16), jax.ShapeDtypeStruct((t, LANES), F32)],
        compiler_params=_params("parallel"),
    )(x, w_row, narrow_w)


def _matmul_norm(a, b, residual, w_row, *, name, tm=512):
    t, k = a.shape
    d = b.shape[1]
    tm = min(tm, t)

    def body(a_ref, b_ref, r_ref, w_ref, h_ref, hn_ref):
        h = _dot(a_ref[...], b_ref[...]) + r_ref[...]
        h_ref[...] = h
        hn_ref[...] = _rms(h, w_ref[...]).astype(BF16)

    row = pl.BlockSpec((tm, d), lambda i: (i, 0))
    return pl.pallas_call(
        body,
        name=name,
        grid=(t // tm,),
        in_specs=[pl.BlockSpec((tm, k), lambda i: (i, 0)), pl.BlockSpec((k, d), lambda i: (0, 0)), row,
                  pl.BlockSpec((1, d), lambda i: (0, 0))],
        out_specs=[row, row],
        out_shape=[jax.ShapeDtypeStruct((t, d), F32), jax.ShapeDtypeStruct((t, d), BF16)],
        compiler_params=_params("parallel"),
    )(a, b, residual, w_row)


def _rmsnorm_bwd(x, w_row, dy, dres, *, name, tr=256, narrow=None):
    t, d = x.shape
    tr = min(tr, t)
    extra = list(narrow) if narrow is not None else []

    def body(x_ref, w_ref, dy_ref, dres_ref, *refs):
        dx_ref, dxb_ref, dw_ref = refs[len(extra):]

        @pl.when(pl.program_id(0) == 0)
        def _():
            dw_ref[...] = jnp.zeros_like(dw_ref)

        dy = dy_ref[...]
        if extra:
            dy = dy + _dot(refs[0][...], refs[1][...], _NT)
        _, vjp = jax.vjp(_rms, x_ref[...], w_ref[...])
        dx, dw = vjp(dy)
        dx = dx + dres_ref[...]
        dx_ref[...] = dx
        dxb_ref[...] = dx.astype(BF16)
        dw_ref[...] += dw

    row = pl.BlockSpec((tr, d), lambda i: (i, 0))
    vec = pl.BlockSpec((1, d), lambda i: (0, 0))
    extra_specs = [pl.BlockSpec((tr, LANES), lambda i: (i, 0)), pl.BlockSpec((d, LANES), lambda i: (0, 0))] if extra else []
    return pl.pallas_call(
        body,
        name=name,
        grid=(t // tr,),
        in_specs=[row, vec, row, row] + extra_specs,
        out_specs=[row, row, vec],
        out_shape=[jax.ShapeDtypeStruct((t, d), F32), jax.ShapeDtypeStruct((t, d), BF16), jax.ShapeDtypeStruct((1, d), F32)],
        compiler_params=_params("arbitrary"),
    )(x, w_row, dy, dres, *extra)


def _matmul_loss(a, b, residual, target, *, name, tm=512, tn=1024):
    t, k = a.shape
    d = b.shape[1]
    tm, tn = min(tm, t), min(tn, d)

    def body(a_ref, b_ref, r_ref, t_ref, dh_ref, dhb_ref, part_ref):
        @pl.when(pl.program_id(1) == 0)
        def _():
            part_ref[...] = jnp.zeros_like(part_ref)

        err = _dot(a_ref[...], b_ref[...]) + r_ref[...] - t_ref[...]
        dh = err * (1.0 / d)
        dh_ref[...] = dh
        dhb_ref[...] = dh.astype(BF16)
        part_ref[...] += jnp.sum(err * err, axis=0, keepdims=True)

    tile = pl.BlockSpec((tm, tn), lambda j, i: (i, j))
    dh, dhb, part = pl.pallas_call(
        body,
        name=name,
        grid=(d // tn, t // tm),
        in_specs=[pl.BlockSpec((tm, k), lambda j, i: (i, 0)), pl.BlockSpec((k, tn), lambda j, i: (0, j)), tile, tile],
        out_specs=[tile, tile, pl.BlockSpec((1, tn), lambda j, i: (0, j))],
        out_shape=[jax.ShapeDtypeStruct((t, d), F32), jax.ShapeDtypeStruct((t, d), BF16), jax.ShapeDtypeStruct((1, d), F32)],
        compiler_params=_params("arbitrary", "arbitrary"),
    )(a, b, residual, target)
    return 0.5 / d * jnp.sum(part), dh, dhb


_BNN = (((2,), (1,)), ((0,), (0,)))
_BNT = (((2,), (2,)), ((0,), (0,)))
_BTN = (((1,), (1,)), ((0,), (0,)))


_TAP0 = HALO - (CONV_K - 1)


def _conv(x_ref, w, rows):
    c = w[0:1, :] * x_ref[_TAP0:_TAP0 + rows, :]
    for j in range(1, CONV_K):
        c = c + w[j:j + 1, :] * x_ref[_TAP0 + j:_TAP0 + j + rows, :]
    return c


def _conv_silu_bwd(x_ref, w, dact, dc_ref, rows):
    c = _conv(x_ref, w, rows)
    sig = jax.nn.sigmoid(c)
    dc = dact * (sig * (1.0 + c * (1.0 - sig)))
    dw = [jnp.sum(dc * x_ref[_TAP0 + j:_TAP0 + j + rows, :], axis=0, keepdims=True) for j in range(CONV_K)]
    dc_ref[0:HALO, :] = jnp.zeros((HALO, HEAD_DIM), F32)
    dc_ref[HALO:HALO + rows, :] = dc
    dc_ref[HALO + rows:HALO + rows + HALO, :] = jnp.zeros((HALO, HEAD_DIM), F32)
    first = HALO - _TAP0
    dx = w[0:1, :] * dc_ref[first:first + HALO + rows, :]
    for j in range(1, CONV_K):
        dx = dx + w[j:j + 1, :] * dc_ref[first - j:first - j + HALO + rows, :]
    return dx, dw


@jax.custom_vjp
def _unit_lower_inverse(neg_l):
    n = neg_l.shape[0]
    eye = (lax.broadcasted_iota(jnp.int32, (n, CHUNK, CHUNK), 1) == lax.broadcasted_iota(jnp.int32, (n, CHUNK, CHUNK), 2))
    inv = eye.astype(F32) + neg_l
    power = _bdot(neg_l, neg_l, _BNN)
    for _ in range(4):
        both = _bdot(jnp.concatenate([inv, power], axis=1), power, _BNN)
        inv, power = inv + both[:, :CHUNK], both[:, CHUNK:]
    return inv + _bdot(inv, power, _BNN)


def _unit_lower_inverse_fwd(neg_l):
    inv = _unit_lower_inverse(neg_l)
    return inv, inv


def _unit_lower_inverse_bwd(inv, dinv):
    return (_fdot(_fdot(inv, dinv, _BTN), inv, _BNT),)


_unit_lower_inverse.defvjp(_unit_lower_inverse_fwd, _unit_lower_inverse_bwd)


def _gdn_intra(qt, kt, v, a, b, alog, dtb):
    n = a.shape[0] // CHUNK
    q = qt * lax.rsqrt(jnp.sum(qt * qt, axis=-1, keepdims=True) + EPS) * (HEAD_DIM ** -0.5)
    k = kt * lax.rsqrt(jnp.sum(kt * kt, axis=-1, keepdims=True) + EPS)
    lanes = jnp.ones((1, HEAD_DIM), F32)
    beta = jax.nn.sigmoid(b) * lanes
    sp = a + dtb
    g = (-jnp.exp(alog) * (jnp.maximum(sp, 0.0) + jnp.log(1.0 + jnp.exp(-jnp.abs(sp))))) * lanes
    q, k, v, beta, g = (t.reshape(n, CHUNK, HEAD_DIM) for t in (q, k, v, beta, g))

    row = lax.broadcasted_iota(jnp.int32, (n, CHUNK, CHUNK), 1)
    col = lax.broadcasted_iota(jnp.int32, (n, CHUNK, CHUNK), 2)
    tri_incl = row >= col
    tri_strict = row > col
    gc = _fdot(tri_incl.astype(F32), g, _BNN)
    gc_row = _fdot(g[:, :, :CHUNK], (row <= col).astype(F32), _BTN)
    decay = jnp.exp(jnp.where(tri_incl, gc[:, :, :CHUNK] - gc_row, -1e30))
    kb = k * beta
    vb = v * beta
    with_k = _bdot(jnp.concatenate([kb, q], axis=1), k, _BNT)
    neg_l = jnp.where(tri_strict, -(with_k[:, :CHUNK] * decay), 0.0)
    qk = jnp.where(tri_incl, with_k[:, CHUNK:] * decay, 0.0)
    inv = _unit_lower_inverse(neg_l)
    e = jnp.exp(gc)
    solved = _bdot(inv, jnp.concatenate([kb * e, vb], axis=2), _BNN)
    g_last = gc[:, CHUNK - 1:CHUNK, :]
    k_dec = k * jnp.exp(g_last - gc)
    from_k = _bdot(k_dec, solved, _BTN)
    from_qk = _bdot(qk, solved, _BNN)
    step, add = -from_k[:, :, :HEAD_DIM], from_k[:, :, HEAD_DIM:]
    read, out = q * e - from_qk[:, :, :HEAD_DIM], from_qk[:, :, HEAD_DIM:]
    return step, add, jnp.exp(g_last), read, out


def _gdn_scan_step(state, step, add, decay_last):
    return state * decay_last + _bdot(step, state) + add


def _gdn_outputs(states, read, out, z, onw):
    return _rms(_bdot(read, states, _BNN) + out, onw) * _silu(z)


def _scan_scratch(n, dtype):
    return [pltpu.VMEM((n, HEAD_DIM, HEAD_DIM), dtype), pltpu.VMEM((n, HEAD_DIM, HEAD_DIM), F32), pltpu.VMEM((n, 1, HEAD_DIM), F32)]


def _head_lane(h, offset=0):
    return lax.broadcasted_iota(jnp.int32, (1, LANES), 1) == h + offset


def _pick(mask, x):
    return jnp.sum(jnp.where(mask, x, 0.0), axis=1, keepdims=True)


def _gdn_specs(heads, tb, rev, nb, PAIR):
    assert heads % PAIR == 0
    blk = (lambda i: nb - 1 - i) if rev else (lambda i: i)
    hb = tb // HALO
    width, pairs = PAIR * HEAD_DIM, heads // PAIR

    def col(group):
        return pl.BlockSpec((tb, width), lambda i, h: (blk(i), group * pairs + h))

    def halo(group):
        return pl.BlockSpec((HALO, width), lambda i, h: (jnp.maximum(blk(i) * hb - 1, 0), group * pairs + h))

    def convw(group):
        return pl.BlockSpec((CONV_K, width), lambda i, h: (0, group * pairs + h))

    vec = pl.BlockSpec((1, LANES), lambda i, h: (0, 0))
    ab = pl.BlockSpec((tb, LANES), lambda i, h: (blk(i), 0))
    states = pl.BlockSpec((PAIR, tb // CHUNK, HEAD_DIM, HEAD_DIM), lambda i, h: (h, blk(i), 0, 0))
    return blk, col, halo, convw, vec, ab, states


def _head_cols(p):
    return slice(p * HEAD_DIM, (p + 1) * HEAD_DIM)


def _gdn_fwd(proj, ab, conv_w, alog_row, dtb_row, onw_row, *, heads, name, tb=1024, pair=4, gather=()):
    t = proj.shape[0]
    tb = min(tb, t)
    nb, cpb = t // tb, tb // CHUNK
    PAIR = min(pair, heads)
    _, col, halo, convw, vec, abspec, states = _gdn_specs(heads, tb, False, nb, PAIR)

    def body(q_ref, k_ref, v_ref, qh_ref, kh_ref, vh_ref, z_ref, ab_ref, wq_ref, wk_ref, wv_ref, alog_ref, dtb_ref, onw_ref,
             og_ref, st_ref, state_scr, x_scr, *op_scr):
        i, pair = pl.program_id(0), pl.program_id(1)
        abv = ab_ref[...]
        heads_here, later = [pair * PAIR + p for p in range(PAIR)], []
        for p, h in enumerate(heads_here):
            cols = _head_cols(p)
            for n, (ref, href) in enumerate(((q_ref, qh_ref), (k_ref, kh_ref), (v_ref, vh_ref))):
                x_scr[p, n, 0:HALO, :] = jnp.where(i > 0, href[:, cols], 0.0)
                x_scr[p, n, HALO:HALO + tb, :] = ref[:, cols]
            sel_a, sel_b = _head_lane(h), _head_lane(h, heads)
            alog, dtb = _pick(sel_a, alog_ref[...]), _pick(sel_a, dtb_ref[...])
            acts = [_silu(_conv(x_scr.at[p, n], w_ref[:, cols], tb)) for n, w_ref in enumerate((wq_ref, wk_ref, wv_ref))]
            *scan, read, out = _gdn_intra(*acts, _pick(sel_a, abv), _pick(sel_b, abv), alog, dtb)
            for scr, val in zip(op_scr[3 * p:3 * p + 3], scan):
                scr[...] = val.astype(scr.dtype)
            later.append((read, out))

        def chunk(c, states):
            for p in range(PAIR):
                st_ref[p, c] = states[p]
            return tuple(_gdn_scan_step(states[p], *[scr[c] for scr in op_scr[3 * p:3 * p + 3]]) for p in range(PAIR))

        @pl.when(i == 0)
        def _():
            for h in heads_here:
                state_scr[h] = jnp.zeros((HEAD_DIM, HEAD_DIM), F32)

        last = lax.fori_loop(0, cpb, chunk, tuple(state_scr[h] for h in heads_here))
        for p, h in enumerate(heads_here):
            cols = _head_cols(p)
            state_scr[h] = last[p]
            og = _gdn_outputs(st_ref[p], *later[p], z_ref[:, cols].reshape(cpb, CHUNK, HEAD_DIM), onw_ref[...])
            og_ref[:, cols] = og.reshape(tb, HEAD_DIM).astype(BF16)

    n_x = len(gather)
    grid = (nb, heads // PAIR)
    og, st, *gathered = pl.pallas_call(
        _with_exchange(body, 14, 2, True, n_x, grid),
        name=name,
        grid=grid,
        in_specs=[col(0), col(1), col(2), halo(0), halo(1), halo(2), col(3), abspec, convw(0), convw(1), convw(2), vec, vec, vec]
        + [_ANY] * n_x,
        out_specs=[pl.BlockSpec((tb, PAIR * HEAD_DIM), lambda i, h: (i, h)), states] + [_ANY] * n_x,
        out_shape=[jax.ShapeDtypeStruct((t, heads * HEAD_DIM), BF16),
                   jax.ShapeDtypeStruct((heads, t // CHUNK, HEAD_DIM, HEAD_DIM), F32)] + _chip_shapes(True, gather),
        scratch_shapes=[pltpu.VMEM((heads, HEAD_DIM, HEAD_DIM), F32), pltpu.VMEM((PAIR, 3, HALO + tb, HEAD_DIM), F32)]
        + _scan_scratch(cpb, BF16) * PAIR + (_chip_scratch(n_x) if n_x else []),
        compiler_params=_params("arbitrary", "arbitrary"),
    )(proj, proj, proj, proj, proj, proj, proj, ab, conv_w, conv_w, conv_w, alog_row, dtb_row, onw_row, *gather)
    return og, st, gathered


def _gdn_bwd(proj, ab, conv_w, alog_row, dtb_row, onw_row, states, dog, *, heads, name, tb=1024, pair=2, exchange=()):
    t = proj.shape[0]
    tb = min(tb, t)
    nb, cpb = t // tb, tb // CHUNK
    PAIR = min(pair, heads)
    _, col, halo, convw, vec, abspec, states_spec = _gdn_specs(heads, tb, True, nb, PAIR)
    n_conv = conv_w.shape[1]

    def body(q_ref, k_ref, v_ref, qh_ref, kh_ref, vh_ref, z_ref, ab_ref, wq_ref, wk_ref, wv_ref, alog_ref, dtb_ref, onw_ref,
             st_ref, dog_ref, dproj_ref, dab_ref, dconv_ref, dalog_ref, ddtb_ref, donw_ref,
             dstate_scr, x_scr, carry_scr, *scr):
        op_scr, dop_scr, dstates_scr, dc_scr = scr[:3 * PAIR], scr[3 * PAIR:6 * PAIR], scr[6 * PAIR:7 * PAIR], scr[7 * PAIR]
        i, pair = pl.program_id(0), pl.program_id(1)
        first_block = i == nb - 1
        heads_here, later = [pair * PAIR + p for p in range(PAIR)], []

        @pl.when(jnp.logical_and(i == 0, pair == 0))
        def _():
            dconv_ref[...] = jnp.zeros_like(dconv_ref)
            dalog_ref[...] = jnp.zeros_like(dalog_ref)
            ddtb_ref[...] = jnp.zeros_like(ddtb_ref)
            donw_ref[...] = jnp.zeros_like(donw_ref)

        @pl.when(pair == 0)
        def _():
            dab_ref[...] = jnp.zeros_like(dab_ref)

        @pl.when(i == 0)
        def _():
            for h in heads_here:
                dstate_scr[h] = jnp.zeros((HEAD_DIM, HEAD_DIM), F32)
                carry_scr[h] = jnp.zeros((3, HALO, HEAD_DIM), F32)

        abv = ab_ref[...]
        w_refs = (wq_ref, wk_ref, wv_ref)
        for p, h in enumerate(heads_here):
            cols = _head_cols(p)
            for n, (ref, href) in enumerate(((q_ref, qh_ref), (k_ref, kh_ref), (v_ref, vh_ref))):
                x_scr[p, n, 0:HALO, :] = jnp.where(first_block, 0.0, href[:, cols])
                x_scr[p, n, HALO:HALO + tb, :] = ref[:, cols]
            sel_a, sel_b = _head_lane(h), _head_lane(h, heads)
            alog, dtb = _pick(sel_a, alog_ref[...]), _pick(sel_a, dtb_ref[...])
            acts = [_silu(_conv(x_scr.at[p, n], w_ref[:, cols], tb)) for n, w_ref in enumerate(w_refs)]
            (*scan, read, out), vjp_intra = jax.vjp(_gdn_intra, *acts, _pick(sel_a, abv), _pick(sel_b, abv), alog, dtb)
            for s, val in zip(op_scr[3 * p:3 * p + 3], scan):
                s[...] = val.astype(s.dtype)
            blocked = lambda ref: ref[:, cols].reshape(cpb, CHUNK, HEAD_DIM)
            _, vjp_outputs = jax.vjp(_gdn_outputs, st_ref[p], read, out, blocked(z_ref), onw_ref[...])
            dstates_scr[p][...], dread, dout, dz, donw = vjp_outputs(blocked(dog_ref))
            dproj_ref[3, :, cols] = dz.reshape(tb, HEAD_DIM).astype(BF16)
            donw_ref[...] += donw
            later.append((vjp_intra, dread, dout, sel_a, sel_b))

        def chunk(i_rev, dstates):
            c = cpb - 1 - i_rev
            new = []
            for p in range(PAIR):
                _, vjp = jax.vjp(_gdn_scan_step, st_ref[p, c], *[s[c].astype(F32) for s in op_scr[3 * p:3 * p + 3]])
                dstate, *grads = vjp(dstates[p])
                for s, val in zip(dop_scr[3 * p:3 * p + 3], grads):
                    s[c] = val
                new.append(dstate + dstates_scr[p][c])
            return tuple(new)

        last = lax.fori_loop(0, cpb, chunk, tuple(dstate_scr[h] for h in heads_here))
        for p, h in enumerate(heads_here):
            cols = _head_cols(p)
            vjp_intra, dread, dout, sel_a, sel_b = later[p]
            dstate_scr[h] = last[p]
            *dacts, da, db, dalog, ddtb = vjp_intra((*[s[...] for s in dop_scr[3 * p:3 * p + 3]], dread, dout))
            dab_ref[...] += jnp.where(sel_a, da, 0.0) + jnp.where(sel_b, db, 0.0)
            for n, (dact, w_ref) in enumerate(zip(dacts, w_refs)):
                dx, dw = _conv_silu_bwd(x_scr.at[p, n], w_ref[:, cols], dact, dc_scr, tb)
                x_scr[p, n] = dx
                x_scr[p, n, tb:tb + HALO, :] += carry_scr[h, n]
                carry_scr[h, n] = x_scr[p, n, 0:HALO, :]
                dproj_ref[n, :, cols] = x_scr[p, n, HALO:HALO + tb, :].astype(BF16)
                lanes = pl.ds(pl.multiple_of((n * heads + h) * HEAD_DIM, HEAD_DIM), HEAD_DIM)
                for j in range(CONV_K):
                    dconv_ref[j:j + 1, lanes] += dw[j]
            dalog_ref[...] += jnp.where(sel_a, dalog, 0.0)
            ddtb_ref[...] += jnp.where(sel_a, ddtb, 0.0)

    dog_spec = pl.BlockSpec((tb, PAIR * HEAD_DIM), lambda i, h: (nb - 1 - i, h))
    dproj_spec = pl.BlockSpec((4, tb, PAIR * HEAD_DIM), lambda i, h: (0, nb - 1 - i, h))
    row_shape = jax.ShapeDtypeStruct((1, LANES), F32)
    n_x = len(exchange)
    grid = (nb, heads // PAIR)
    outs = pl.pallas_call(
        _with_exchange(body, 16, 6, False, n_x, grid),
        name=name,
        grid=grid,
        in_specs=[col(0), col(1), col(2), halo(0), halo(1), halo(2), col(3), abspec, convw(0), convw(1), convw(2), vec, vec, vec,
                  states_spec, dog_spec] + [_ANY] * n_x,
        out_specs=[dproj_spec, abspec, pl.BlockSpec((CONV_K, n_conv), lambda i, h: (0, 0)), vec, vec, vec] + [_ANY] * n_x,
        out_shape=[jax.ShapeDtypeStruct((4, t, heads * HEAD_DIM), BF16), jax.ShapeDtypeStruct((t, LANES), F32),
                   jax.ShapeDtypeStruct((CONV_K, n_conv), F32), row_shape, row_shape, row_shape] + _chip_shapes(False, exchange),
        scratch_shapes=[pltpu.VMEM((heads, HEAD_DIM, HEAD_DIM), F32), pltpu.VMEM((PAIR, 3, HALO + tb, HEAD_DIM), F32),
                        pltpu.VMEM((heads, 3, HALO, HEAD_DIM), F32)] + _scan_scratch(cpb, BF16) * PAIR
        + _scan_scratch(cpb, F32) * PAIR + [pltpu.VMEM((cpb, HEAD_DIM, HEAD_DIM), F32)] * PAIR
        + [pltpu.VMEM((HALO + tb + HALO, HEAD_DIM), F32)]
        + (_chip_scratch(n_x) if n_x else []),
        compiler_params=_params("arbitrary", "arbitrary", vmem=VMEM_LIMIT_WIDE_BYTES),
    )(proj, proj, proj, proj, proj, proj, proj, ab, conv_w, conv_w, conv_w, alog_row, dtb_row, onw_row, states, dog, *exchange)
    return (*outs[:6], outs[6:])


BAND = (LEFT_CHUNKS + 1) * CHUNK
PAD = LEFT_CHUNKS * CHUNK
GROUP = 2
ROWS = GROUP * CHUNK
WIN = (LEFT_CHUNKS + GROUP) * CHUNK
DIAGS = WIN + ROWS - 1
NEAR = PAD + ROWS - 1 - REL_CLIP
assert 0 < NEAR < DIAGS and WIN - PAD - 1 <= REL_CLIP and WIN % LANES == 0
ATTN_BLOCK = 1024


def _band_bias(rel_bias):
    heads = rel_bias.shape[0]
    far = jnp.broadcast_to(rel_bias[:, 2 * REL_CLIP:], (heads, NEAR + 1))
    near = rel_bias[:, 2 * REL_CLIP + NEAR + 1 - DIAGS:2 * REL_CLIP][:, ::-1]
    diag = jnp.concatenate([far, near], axis=1)
    return jnp.stack([diag[:, ROWS - 1 - r:ROWS - 1 - r + WIN] for r in range(ROWS)], axis=1)


def _band_bias_grad(dbias):
    heads = dbias.shape[0]
    diag = sum(jnp.pad(dbias[:, r, :], ((0, 0), (ROWS - 1 - r, r))) for r in range(ROWS))
    far = jnp.sum(diag[:, :NEAR + 1], axis=1, keepdims=True)
    near = diag[:, NEAR + 1:][:, ::-1]
    unused = jnp.zeros((heads, 2 * REL_CLIP - near.shape[1]), F32)
    return jnp.concatenate([unused, near, far], axis=1)


def _masked_bias(bias, n):
    r = np.arange(ROWS)[:, None]
    key = np.arange(WIN)[None, :]
    band_start = (r // CHUNK) * CHUNK
    in_band = np.logical_and(key >= band_start, key < band_start + BAND)
    in_sequence = key[None] >= PAD - np.arange(n)[:, None, None] * ROWS
    first = jnp.where(np.logical_and(in_band[None], in_sequence)[None], bias[:, None], -1e30)
    return first, jnp.where(in_band[None, None], bias[:, None], -1e30)


def _attn_groups(q_pre, z, kn, v, bias, qnw):
    q = _rms(q_pre, qnw)
    s = _bdot(q, kn, _BNT) * (HEAD_DIM ** -0.5) + bias
    p = jnp.exp(s - jnp.max(s, axis=-1, keepdims=True))
    p = p / jnp.sum(p, axis=-1, keepdims=True)
    return _bdot(p, v, _BNN) * _silu(z)


def _attn_groups_bwd(q_pre, z, kn, v, bias, qnw, dog):
    scale = HEAD_DIM ** -0.5
    inv_rms = lax.rsqrt(jnp.mean(q_pre * q_pre, axis=-1, keepdims=True) + EPS)
    q_hat = q_pre * inv_rms
    q_b = (q_hat * qnw).astype(BF16)
    s = _dot(q_b, kn, _BNT) * scale + bias
    e = jnp.exp(s - jnp.max(s, axis=-1, keepdims=True))
    p = e * (1.0 / jnp.sum(e, axis=-1, keepdims=True))
    p_b = p.astype(BF16)
    o = _dot(p_b, v, _BNN)
    sig = jax.nn.sigmoid(z)
    do = dog * (z * sig)
    dz = dog * o * (sig * (1.0 + z * (1.0 - sig)))
    do_b = do.astype(BF16)
    dv = _dot(p_b, do_b, _BTN)
    dp = _dot(do_b, v, _BNT)
    ds = p * (dp - jnp.sum(do * o, axis=-1, keepdims=True))
    ds_b = (ds * scale).astype(BF16)
    dq = _dot(ds_b, kn, _BNN)
    dkn = _dot(ds_b, q_b, _BTN)
    dqnw = jnp.sum(jnp.sum(dq * q_hat, axis=0), axis=0, keepdims=True)
    dq_hat = dq * qnw
    dq_pre = inv_rms * (dq_hat - q_hat * jnp.mean(dq_hat * q_hat, axis=-1, keepdims=True))
    return dq_pre, dz, dkn, dv, jnp.sum(ds, axis=0), dqnw


def _attn_specs(heads, tb, t):
    def col(group):
        return pl.BlockSpec((tb, HEAD_DIM), lambda h, i: (i, group * heads + h))

    def full(group):
        return pl.BlockSpec((t, HEAD_DIM), lambda h, i: (0, group * heads + h))

    bias = [pl.BlockSpec((1, tb // ROWS, ROWS, WIN), lambda h, i: (h, 0, 0, 0)),
            pl.BlockSpec((1, 1, ROWS, WIN), lambda h, i: (h, 0, 0, 0))]
    vec = pl.BlockSpec((1, HEAD_DIM), lambda h, i: (0, 0))
    return col, full, bias, vec


def _attn_windows(scr, block_start, n):
    return jnp.stack([scr[pl.ds(pl.multiple_of(block_start + g * ROWS, ROWS), WIN), :] for g in range(n)])


def _attn_fill(k_ref, v_ref, knw_ref, kn_scr, v_scr, t):
    kn_scr[0:PAD, :] = jnp.zeros((PAD, HEAD_DIM), BF16)
    v_scr[0:PAD, :] = jnp.zeros((PAD, HEAD_DIM), BF16)
    step = min(512, t)

    def fill(j, _):
        rows = pl.ds(pl.multiple_of(j * step, step), step)
        prows = pl.ds(pl.multiple_of(PAD + j * step, CHUNK), step)
        kn_scr[prows, :] = _rms(k_ref[rows, :], knw_ref[...]).astype(BF16)
        v_scr[prows, :] = v_ref[rows, :].astype(BF16)
        return 0

    lax.fori_loop(0, t // step, fill, 0)


def _attn_fwd(proj, bias, qnw_row, knw_row, *, heads, name, tb=ATTN_BLOCK):
    t = proj.shape[0]
    tb = min(tb, t)
    nb, ng = t // tb, tb // ROWS
    col, full, bias_spec, vec = _attn_specs(heads, tb, t)

    def body(q_ref, k_ref, v_ref, z_ref, first_ref, rest_ref, qnw_ref, knw_ref, og_ref, kn_scr, v_scr):
        i = pl.program_id(1)

        @pl.when(i == 0)
        def _():
            _attn_fill(k_ref, v_ref, knw_ref, kn_scr, v_scr, t)

        def run(block_bias):
            start = i * tb
            og = _attn_groups(q_ref[...].reshape(ng, ROWS, HEAD_DIM), z_ref[...].reshape(ng, ROWS, HEAD_DIM),
                              _attn_windows(kn_scr, start, ng), _attn_windows(v_scr, start, ng), block_bias, qnw_ref[...])
            og_ref[...] = og.reshape(tb, HEAD_DIM).astype(BF16)

        pl.when(i == 0)(lambda: run(first_ref[0]))
        pl.when(i > 0)(lambda: run(rest_ref[0]))

    return pl.pallas_call(
        body,
        name=name,
        grid=(heads, nb),
        in_specs=[col(0), full(1), full(2), col(3), *bias_spec, vec, vec],
        out_specs=pl.BlockSpec((tb, HEAD_DIM), lambda h, i: (i, h)),
        out_shape=jax.ShapeDtypeStruct((t, heads * HEAD_DIM), BF16),
        scratch_shapes=[pltpu.VMEM((PAD + t, HEAD_DIM), BF16), pltpu.VMEM((PAD + t, HEAD_DIM), BF16)],
        compiler_params=_params("arbitrary", "arbitrary"),
    )(proj, proj, proj, proj, *bias, qnw_row, knw_row)


def _attn_bwd(proj, bias, qnw_row, knw_row, dog, *, heads, name, tb=ATTN_BLOCK, sub=4):
    t = proj.shape[0]
    tb = min(tb, t)
    nb, ng = t // tb, tb // ROWS
    sub = min(sub, ng)
    col, full, bias_spec, vec = _attn_specs(heads, tb, t)

    def body(q_ref, k_ref, v_ref, z_ref, first_ref, rest_ref, qnw_ref, knw_ref, dog_ref,
             dqz_ref, dkv_ref, dbias_ref, dqnw_ref, dknw_ref, kn_scr, v_scr, dkn_scr, dv_scr):
        i = pl.program_id(1)

        @pl.when(i == 0)
        def _():
            _attn_fill(k_ref, v_ref, knw_ref, kn_scr, v_scr, t)
            dkn_scr[...] = jnp.zeros_like(dkn_scr)
            dv_scr[...] = jnp.zeros_like(dv_scr)
            dbias_ref[...] = jnp.zeros_like(dbias_ref)
            dqnw_ref[...] = jnp.zeros_like(dqnw_ref)

        def run(block_bias):
            for g0 in range(0, ng, sub):
                rows = pl.ds(g0 * ROWS, sub * ROWS)
                at = i * tb + g0 * ROWS
                blocked = lambda ref: ref[rows, :].reshape(sub, ROWS, HEAD_DIM)
                dq, dz, dkn, dv, dbias, dqnw = _attn_groups_bwd(
                    blocked(q_ref), blocked(z_ref), _attn_windows(kn_scr, at, sub), _attn_windows(v_scr, at, sub),
                    block_bias(g0), qnw_ref[...], blocked(dog_ref))
                dqz_ref[0, rows, :] = dq.reshape(sub * ROWS, HEAD_DIM).astype(BF16)
                dqz_ref[1, rows, :] = dz.reshape(sub * ROWS, HEAD_DIM).astype(BF16)
                for g in range(sub):
                    window = pl.ds(pl.multiple_of(at + g * ROWS, ROWS), WIN)
                    dkn_scr[window, :] += dkn[g]
                    dv_scr[window, :] += dv[g]
                dbias_ref[0] += dbias
                dqnw_ref[0] += dqnw

        pl.when(i == 0)(lambda: run(lambda g0: first_ref[0, g0:g0 + sub]))
        pl.when(i > 0)(lambda: run(lambda g0: rest_ref[0]))

        @pl.when(i == nb - 1)
        def _():
            step = min(512, t)

            def finish(j, dknw):
                rows = pl.ds(pl.multiple_of(j * step, step), step)
                prows = pl.ds(pl.multiple_of(PAD + j * step, CHUNK), step)
                _, vjp = jax.vjp(_rms, k_ref[rows, :], knw_ref[...])
                dk, dw = vjp(dkn_scr[prows, :])
                dkv_ref[0, rows, :] = dk.astype(BF16)
                dkv_ref[1, rows, :] = dv_scr[prows, :].astype(BF16)
                return dknw + dw

            dknw_ref[0] = lax.fori_loop(0, t // step, finish, jnp.zeros((1, HEAD_DIM), F32))

    pair_col = pl.BlockSpec((2, tb, HEAD_DIM), lambda h, i: (0, i, h))
    pair_full = pl.BlockSpec((2, t, HEAD_DIM), lambda h, i: (0, 0, h))
    head_vec = pl.BlockSpec((1, 1, HEAD_DIM), lambda h, i: (h, 0, 0))
    pair_shape = jax.ShapeDtypeStruct((2, t, heads * HEAD_DIM), BF16)
    vec_shape = jax.ShapeDtypeStruct((heads, 1, HEAD_DIM), F32)
    return pl.pallas_call(
        body,
        name=name,
        grid=(heads, nb),
        in_specs=[col(0), full(1), full(2), col(3), *bias_spec, vec, vec, pl.BlockSpec((tb, HEAD_DIM), lambda h, i: (i, h))],
        out_specs=[pair_col, pair_full, pl.BlockSpec((1, ROWS, WIN), lambda h, i: (h, 0, 0)), head_vec, head_vec],
        out_shape=[pair_shape, pair_shape, jax.ShapeDtypeStruct((heads, ROWS, WIN), F32), vec_shape, vec_shape],
        scratch_shapes=[pltpu.VMEM((PAD + t, HEAD_DIM), BF16), pltpu.VMEM((PAD + t, HEAD_DIM), BF16),
                        pltpu.VMEM((PAD + t, HEAD_DIM), F32), pltpu.VMEM((PAD + t, HEAD_DIM), F32)],
        compiler_params=_params("arbitrary", "arbitrary"),
    )(proj, proj, proj, proj, *bias, qnw_row, knw_row, dog)


def _lane_row(v):
    v = v.reshape(1, -1)
    return jnp.pad(v, ((0, 0), (0, LANES - v.shape[1])))


def _local_step(x, target, norm_w, wa_in, conv_w, a_log, dt_bias, onw, wa_out, wb_in, qnw, knw, rel_bias, wb_out, *,
                sharded=False):
    ha, hb = a_log.shape[-1], rel_bias.shape[-2]
    na = 4 * ha * HEAD_DIM
    wa_main = wa_in[:, :na]
    wa_ab = jnp.pad(wa_in[:, na:], ((0, 0), (0, LANES - 2 * ha)))
    alog_row, dtb_row, onw_row = _lane_row(a_log), _lane_row(dt_bias), _lane_row(onw)
    qnw_row, knw_row = _lane_row(qnw), _lane_row(knw)
    bias = _masked_bias(_band_bias(rel_bias.reshape(hb, -1)), min(ATTN_BLOCK, x.shape[0]) // ROWS)

    hn0, ab_a = _rmsnorm_fwd(x, norm_w[0:1], wa_ab, name="norm0")
    proj_a = _matmul(hn0, wa_main, name="a_in")
    og_a, states, got = _gdn_fwd(proj_a, ab_a, conv_w, alog_row, dtb_row, onw_row, heads=ha, name="gdn_fwd",
                                 gather=[wb_in, wa_out, wb_out] if sharded else [])
    if sharded:
        wb_in, wa_out, wb_out = _join_cols(got[0]), got[1].reshape(-1, got[1].shape[-1]), got[2].reshape(-1, got[2].shape[-1])
    h1, hn1 = _matmul_norm(og_a, wa_out, x, norm_w[1:2], name="a_out_norm1")
    proj_b = _matmul(hn1, wb_in, name="b_in")
    og_b = _attn_fwd(proj_b, bias, qnw_row, knw_row, heads=hb, name="attn_fwd")
    loss, dh2, dh2_b = _matmul_loss(og_b, wb_out, h1, target, name="b_out_loss")

    grad_dtype = BF16 if sharded else F32
    dog_b = _matmul(dh2_b, wb_out, trans_b=True, name="d_b_out_x")
    dwb_out = _matmul(og_b, dh2_b, trans_a=True, out_dtype=grad_dtype, name="d_b_out_w")
    dqz, dkv, dbias, dqnw, dknw = _attn_bwd(proj_b, bias, qnw_row, knw_row, dog_b, heads=hb, name="attn_bwd")
    dproj_b, qkvz = [dqz, dkv], (0, 3, 1, 2)
    dhn1 = _matmul(dproj_b, wb_in, trans_b=True, order=qkvz, name="d_b_in_x")
    dwb_in = _matmul(hn1, dproj_b, trans_a=True, order=qkvz, out_dtype=grad_dtype, col_slabs=N_CHIPS if sharded else 0,
                     name="d_b_in_w")
    dh1, dh1_b, dnw1 = _rmsnorm_bwd(h1, norm_w[1:2], dhn1, dh2, name="d_norm1")

    dog_a = _matmul(dh1_b, wa_out, trans_b=True, name="d_a_out_x")
    dwa_out = _matmul(og_a, dh1_b, trans_a=True, out_dtype=grad_dtype, name="d_a_out_w")
    early = [dwb_in, _split_rows(dwa_out), _split_rows(dwb_out)] if sharded else []
    dproj_a, dab, dconv, dalog, ddtb, donw, landed = _gdn_bwd(
        proj_a, ab_a, conv_w, alog_row, dtb_row, onw_row, states, dog_a, heads=ha, name="gdn_bwd", exchange=early)
    dab_b = dab.astype(BF16)
    if sharded:
        mine = [_sum_slots(s, name=f"chip_sum_{n}") for n, s in zip(("b_w_in", "a_w_out", "b_w_out"), landed)]
        dwa_main, theirs = _matmul(hn0, dproj_a, trans_a=True, out_dtype=grad_dtype, name="d_a_in_w", exchange=mine,
                                   with_pair=True)
        dwb_in, dwa_out, dwb_out = zip(mine, theirs)
    else:
        dwa_main = _matmul(hn0, dproj_a, trans_a=True, out_dtype=grad_dtype, name="d_a_in_w")
    dwa_in = jnp.concatenate(
        [dwa_main, _matmul(hn0, dab_b, trans_a=True, out_dtype=grad_dtype, name="d_a_in_ab_w")[:, :2 * ha]], axis=1)
    if sharded:
        dhn0, (dwa_in, dconv) = _matmul(dproj_a, wa_main, trans_b=True, name="d_a_in_x",
                                        exchange=[_split_cols(dwa_in), _split_cols(dconv)])
    else:
        dhn0 = _matmul(dproj_a, wa_main, trans_b=True, name="d_a_in_x")
    dx, _, dnw0 = _rmsnorm_bwd(x, norm_w[0:1], dhn0, dh1, narrow=(dab_b, wa_ab), name="d_norm0")

    drel = _band_bias_grad(dbias)
    grads = dict(
        norm_w=jnp.concatenate([dnw0, dnw1], axis=0), a_w_in=dwa_in, a_conv_w=dconv, a_a_log=dalog[:, :ha],
        a_dt_bias=ddtb[:, :ha], a_out_norm_w=donw, a_w_out=dwa_out, b_w_in=dwb_in, b_q_norm_w=jnp.sum(dqnw, axis=0),
        b_k_norm_w=jnp.sum(dknw, axis=0), b_rel_bias=drel[None], b_w_out=dwb_out)
    return loss, dx, grads


_ANY = pl.BlockSpec(memory_space=pl.ANY)
_CHIP_FLIPS = ((1, 0), (0, 1), (1, 1))


def _place():
    x, y, c = lax.axis_index("x"), lax.axis_index("y"), lax.axis_index("c")
    return x, y, c


def _flip(v, bit):
    return 1 - v if bit else v


def _remote(src, dst, send_sem, recv_sem, peer):
    return pltpu.make_async_remote_copy(src_ref=src, dst_ref=dst, send_sem=send_sem, recv_sem=recv_sem, device_id=peer,
                                        device_id_type=MESH)


def _comm_call(body, arrays, out_shapes, n_remote, n_local, name):
    scratch = [pltpu.SemaphoreType.DMA((n_remote,)), pltpu.SemaphoreType.DMA((n_remote,))]
    if n_local:
        scratch.append(pltpu.SemaphoreType.DMA((n_local,)))
    return pl.pallas_call(
        body, name=name, in_specs=[_ANY] * len(arrays), out_specs=[_ANY] * len(out_shapes), out_shape=out_shapes,
        scratch_shapes=scratch)(*arrays)


def _chip_scratch(n):
    return [pltpu.SemaphoreType.DMA((3 * n,)), pltpu.SemaphoreType.DMA((3 * n,)), pltpu.SemaphoreType.DMA((n,))]


def _chip_shapes(gather, arrays):
    return [jax.ShapeDtypeStruct(((N_CHIPS,) + s.shape) if gather else s.shape, s.dtype) for s in arrays]


def _chip_traffic(gather, ins, outs, sems):
    send_sems, recv_sems, local_sems = sems
    x, y, c = _place()
    mine = 2 * x + y
    local, remote, landing = [], [], []
    for a in range(len(ins)):
        local.append(pltpu.make_async_copy(ins[a] if gather else ins[a].at[mine], outs[a].at[mine], local_sems.at[a]))
        for k, (fx, fy) in enumerate(_CHIP_FLIPS):
            peer = (_flip(x, fx), _flip(y, fy), c)
            theirs = 2 * peer[0] + peer[1]
            src = ins[a] if gather else ins[a].at[theirs]
            pair = send_sems.at[3 * a + k], recv_sems.at[3 * a + k]
            remote.append(_remote(src, outs[a].at[mine], *pair, peer))
            landing.append(_remote(src, outs[a].at[theirs], *pair, peer))
    return local + remote, (local, landing, remote)


def _start(traffic):
    for cp in traffic[0]:
        cp.start()


def _finish(traffic):
    local, landing, remote = traffic[1]
    for cp in local:
        cp.wait()
    for cp in landing:
        cp.wait_recv()
    for cp in remote:
        cp.wait_send()


def _pair_scratch(n):
    return [pltpu.SemaphoreType.DMA((n,)), pltpu.SemaphoreType.DMA((n,))]


def _pair_traffic(ins, outs, sems):
    send_sems, recv_sems = sems
    x, y, c = _place()
    copies = [_remote(ins[a], outs[a], send_sems.at[a], recv_sems.at[a], (x, y, 1 - c)) for a in range(len(ins))]
    return copies, ([], copies, copies)


def _with_exchange(compute, n_in, n_out, gather, n_x, grid):
    if not n_x:
        return compute

    def body(*refs):
        ins, x_in = refs[:n_in], refs[n_in:n_in + n_x]
        outs, x_out = refs[n_in + n_x:n_in + n_x + n_out], refs[n_in + n_x + n_out:n_in + 2 * n_x + n_out]
        n_sems = 2 if gather == "pair" else 3
        scratch, sems = refs[n_in + 2 * n_x + n_out:-n_sems], refs[-n_sems:]
        traffic = _pair_traffic(x_in, x_out, sems) if gather == "pair" else _chip_traffic(gather, x_in, x_out, sems)
        first = functools.reduce(jnp.logical_and, [pl.program_id(d) == 0 for d in range(len(grid))])
        last = functools.reduce(jnp.logical_and, [pl.program_id(d) == grid[d] - 1 for d in range(len(grid))])

        @pl.when(first)
        def _():
            _start(traffic)

        compute(*ins, *outs, *scratch)

        @pl.when(last)
        def _():
            _finish(traffic)

    return body


def _gather_shared(shard, small, *, name):
    rows = shard.shape[0]
    assert rows % 2 == 0
    half = rows // 2

    def body(shard_ref, small_ref, out_ref, small_out_ref, send_sems, recv_sems, local_sems):
        x, y, c = _place()
        mine = 2 * x + y
        sibling = (x, y, 1 - c)
        my_rows = pl.ds(pl.multiple_of(c * half, 8), half)
        local = [pltpu.make_async_copy(shard_ref, out_ref.at[mine], local_sems.at[0]),
                 pltpu.make_async_copy(small_ref, small_out_ref.at[mine], local_sems.at[1])]
        sent, landed, passed_on, handed = [], [], [], []
        for k, (fx, fy) in enumerate(_CHIP_FLIPS):
            peer = (_flip(x, fx), _flip(y, fy), c)
            theirs = 2 * peer[0] + peer[1]
            ici, d2d, tiny = [(send_sems.at[3 * n + k], recv_sems.at[3 * n + k]) for n in range(3)]
            sent.append(_remote(shard_ref.at[my_rows], out_ref.at[mine, my_rows], *ici, peer))
            landed.append(_remote(shard_ref.at[my_rows], out_ref.at[theirs, my_rows], *ici, peer))
            sent.append(_remote(small_ref, small_out_ref.at[mine], *tiny, peer))
            landed.append(_remote(small_ref, small_out_ref.at[theirs], *tiny, peer))
            passed_on.append(_remote(out_ref.at[theirs, my_rows], out_ref.at[theirs, my_rows], *d2d, sibling))
            other_rows = pl.ds(pl.multiple_of((1 - c) * half, 8), half)
            handed.append(_remote(out_ref.at[theirs, other_rows], out_ref.at[theirs, other_rows], *d2d, sibling))
        for cp in local + sent:
            cp.start()
        for k in range(3):
            landed[2 * k].wait_recv()
            passed_on[k].start()
        for k in range(3):
            landed[2 * k + 1].wait_recv()
            handed[k].wait_recv()
        for cp in local:
            cp.wait()
        for cp in sent + passed_on:
            cp.wait_send()

    return pl.pallas_call(
        body, name=name, in_specs=[_ANY] * 2, out_specs=[_ANY] * 2, out_shape=_chip_shapes(True, [shard, small]),
        scratch_shapes=[pltpu.SemaphoreType.DMA((9,)), pltpu.SemaphoreType.DMA((9,)), pltpu.SemaphoreType.DMA((2,))],
    )(shard, small)


def _swap_pair(arrays, *, name):
    n = len(arrays)

    def body(*refs):
        ins, outs, (send_sems, recv_sems) = refs[:n], refs[n:2 * n], refs[2 * n:]
        x, y, c = _place()
        copies = [_remote(ins[a], outs[a], send_sems.at[a], recv_sems.at[a], (x, y, 1 - c)) for a in range(n)]
        for cp in copies:
            cp.start()
        for cp in copies:
            cp.wait_recv()
        for cp in copies:
            cp.wait_send()

    shapes = [jax.ShapeDtypeStruct(s.shape, s.dtype) for s in arrays]
    return _comm_call(body, arrays, shapes, n, 0, name)


def _gather_all(tile, *, name):
    def body(in_ref, out_ref, send_sems, recv_sems, local_sems):
        x, y, c = _place()
        mine = 4 * x + 2 * y + c
        local = pltpu.make_async_copy(in_ref, out_ref.at[mine], local_sems.at[0])
        remote, landing = [], []
        for k in range(1, N_DEV):
            peer = (_flip(x, k & 4), _flip(y, k & 2), _flip(c, k & 1))
            sems = send_sems.at[k - 1], recv_sems.at[k - 1]
            remote.append(_remote(in_ref, out_ref.at[mine], *sems, peer))
            landing.append(_remote(in_ref, out_ref.at[4 * peer[0] + 2 * peer[1] + peer[2]], *sems, peer))
        for cp in [local] + remote:
            cp.start()
        local.wait()
        for cp in landing:
            cp.wait_recv()
        for cp in remote:
            cp.wait_send()

    return _comm_call(body, [tile], [jax.ShapeDtypeStruct((N_DEV,) + tile.shape, tile.dtype)], N_DEV - 1, 1, name)[0]


def _sum_slots(slabs, *, name, tr=128):
    s, r, c = slabs.shape
    tr = min(tr, r)

    def body(in_ref, o_ref):
        acc = in_ref[0].astype(F32)
        for j in range(1, s):
            acc = acc + in_ref[j].astype(F32)
        o_ref[...] = acc

    return pl.pallas_call(
        body, name=name, grid=(r // tr,),
        in_specs=[pl.BlockSpec((s, tr, c), lambda i: (0, i, 0))], out_specs=pl.BlockSpec((tr, c), lambda i: (i, 0)),
        out_shape=jax.ShapeDtypeStruct((r, c), F32), compiler_params=_params("parallel"))(slabs)


def _adamw_math(w, g, m, v):
    m = ADAM_B1 * m + (1.0 - ADAM_B1) * g
    v = ADAM_B2 * v + (1.0 - ADAM_B2) * (g * g)
    m_hat = m / (1.0 - ADAM_B1 ** ADAM_STEP)
    v_hat = v / (1.0 - ADAM_B2 ** ADAM_STEP)
    delta = -ADAM_LR * (m_hat / (jnp.sqrt(v_hat) + ADAM_EPS) + ADAM_WD * w)
    return delta, m, v


def _adamw(w, m, v, parts, *, name, tr=128):
    r, c = w.shape
    tr = min(tr, r)
    s = len(parts)

    def body(w_ref, m_ref, v_ref, *refs):
        g_ref, d_ref, nm_ref, nv_ref = refs[s:]
        g = refs[0][...]
        for p_ref in refs[1:s]:
            g = g + p_ref[...]
        g_ref[...] = g
        d_ref[...], nm_ref[...], nv_ref[...] = _adamw_math(w_ref[...], g, m_ref[...], v_ref[...])

    blk = pl.BlockSpec((tr, c), lambda i: (i, 0))
    shape = jax.ShapeDtypeStruct((r, c), F32)
    return pl.pallas_call(
        body, name=name, grid=(r // tr,), in_specs=[blk] * (3 + s), out_specs=[blk] * 4, out_shape=[shape] * 4,
        compiler_params=_params("parallel"))(w, m, v, *parts)


_BIG = ("a_w_in", "b_w_in", "a_w_out", "b_w_out", "a_conv_w")
_SMALL = ("norm_w", "a_a_log", "a_dt_bias", "a_out_norm_w", "b_q_norm_w", "b_k_norm_w", "b_rel_bias")
_ORDER = ("norm_w", "a_w_in", "a_conv_w", "a_a_log", "a_dt_bias", "a_out_norm_w", "a_w_out", "b_w_in", "b_q_norm_w",
          "b_k_norm_w", "b_rel_bias", "b_w_out")


def _join_cols(g):
    return jnp.transpose(g, (1, 0, 2)).reshape(g.shape[1], -1)


def _split_cols(g):
    return jnp.transpose(g.reshape(g.shape[0], N_CHIPS, -1), (1, 0, 2))


def _split_rows(g):
    return g.reshape(N_CHIPS, -1, g.shape[-1])


def _pack(d):
    flat = jnp.concatenate([d[n].reshape(-1) for n in _SMALL])
    return jnp.pad(flat, (0, -flat.shape[0] % LANES)).reshape(1, -1)


def _unpack(row, like):
    out, at = {}, 0
    for n in _SMALL:
        size = like[n].size
        out[n] = row[0, at:at + size].reshape(like[n].shape)
        at += size
    return out


def kernel(x, norm_w, a_w_in, a_conv_w, a_a_log, a_dt_bias, a_out_norm_w, a_w_out, b_w_in, b_q_norm_w, b_k_norm_w, b_rel_bias, b_w_out, loss_target, m_norm_w, m_a_w_in, m_a_conv_w, m_a_a_log, m_a_dt_bias, m_a_out_norm_w, m_a_w_out, m_b_w_in, m_b_q_norm_w, m_b_k_norm_w, m_b_rel_bias, m_b_w_out, v_norm_w, v_a_w_in, v_a_conv_w, v_a_a_log, v_a_dt_bias, v_a_out_norm_w, v_a_w_out, v_b_w_in, v_b_q_norm_w, v_b_k_norm_w, v_b_rel_bias, v_b_w_out):
    w = dict(norm_w=norm_w, a_w_in=a_w_in, a_conv_w=a_conv_w, a_a_log=a_a_log, a_dt_bias=a_dt_bias,
             a_out_norm_w=a_out_norm_w, a_w_out=a_w_out, b_w_in=b_w_in, b_q_norm_w=b_q_norm_w, b_k_norm_w=b_k_norm_w,
             b_rel_bias=b_rel_bias, b_w_out=b_w_out)
    m = dict(norm_w=m_norm_w, a_w_in=m_a_w_in, a_conv_w=m_a_conv_w, a_a_log=m_a_a_log, a_dt_bias=m_a_dt_bias,
             a_out_norm_w=m_a_out_norm_w, a_w_out=m_a_w_out, b_w_in=m_b_w_in, b_q_norm_w=m_b_q_norm_w,
             b_k_norm_w=m_b_k_norm_w, b_rel_bias=m_b_rel_bias, b_w_out=m_b_w_out)
    v = dict(norm_w=v_norm_w, a_w_in=v_a_w_in, a_conv_w=v_a_conv_w, a_a_log=v_a_a_log, a_dt_bias=v_a_dt_bias,
             a_out_norm_w=v_a_out_norm_w, a_w_out=v_a_w_out, b_w_in=v_b_w_in, b_q_norm_w=v_b_q_norm_w,
             b_k_norm_w=v_b_k_norm_w, b_rel_bias=v_b_rel_bias, b_w_out=v_b_w_out)

    wa_in, conv = _gather_shared(a_w_in[0].astype(BF16), a_conv_w[0], name="gather_a_in")
    loss, dx, grads = _local_step(
        x[0], loss_target[0], norm_w, _join_cols(wa_in), _join_cols(conv), a_a_log, a_dt_bias, a_out_norm_w,
        a_w_out[0].astype(BF16), b_w_in[0].astype(BF16), b_q_norm_w, b_k_norm_w, b_rel_bias, b_w_out[0].astype(BF16),
        sharded=True)
    loss = lax.psum(loss, ("x", "y", "c"))

    late = [n for n in _BIG if not isinstance(grads[n], tuple)]
    mine = [_sum_slots(grads[n], name=f"chip_sum_{n}") for n in late]
    sums = {n: grads[n] for n in _BIG if n not in late}
    sums.update(zip(late, zip(mine, _swap_pair(mine, name="pair_grads"))))
    out = {}
    for n in _BIG:
        out[n] = [r[None] for r in _adamw(w[n][0], m[n][0], v[n][0], list(sums[n]), name=f"adamw_{n}")]

    row = _pack(grads)
    tiles = _gather_all(jnp.broadcast_to(row, (8, row.shape[1])), name="gather_small_grads")
    res = _adamw(_pack(w), _pack(m), _pack(v), [tiles[d, 0:1, :] for d in range(N_DEV)], name="adamw_small")
    unpacked = [_unpack(r, w) for r in res]
    for n in _SMALL:
        out[n] = [u[n] for u in unpacked]

    return (loss, dx[None], *[out[n][0] for n in _ORDER], *[out[n][1] for n in _ORDER], *[out[n][2] for n in _ORDER],
            *[out[n][3] for n in _ORDER])
```

```python
import functools

import numpy as np
import jax
import jax.numpy as jnp
from jax import lax
from jax.experimental import pallas as pl
from jax.experimental.pallas import tpu as pltpu

F32 = jnp.float32
BF16 = jnp.bfloat16

CHUNK = 64
HEAD_DIM = 128
LEFT_CHUNKS = 8
REL_CLIP = 256
CONV_K = 4
EPS = 1e-6
HALO = 8

ADAM_LR = 0.001
ADAM_B1 = 0.9
ADAM_B2 = 0.999
ADAM_EPS = 1e-08
ADAM_WD = 0.01
ADAM_STEP = 10

LANES = 128
N_CHIPS = 4
N_DEV = 8
VMEM_LIMIT_BYTES = 56 * 1024 * 1024
VMEM_LIMIT_WIDE_BYTES = 63 * 1024 * 1024
MESH = pl.DeviceIdType.MESH


def _params(*sem, vmem=VMEM_LIMIT_BYTES):
    return pltpu.CompilerParams(dimension_semantics=sem, vmem_limit_bytes=vmem)


def _dot(a, b, dims=(((1,), (0,)), ((), ())), precision=None):
    return lax.dot_general(a, b, dims, precision=precision, preferred_element_type=F32)


_NT = (((1,), (1,)), ((), ()))
_TN = (((0,), (0,)), ((), ()))


def _bdot(a, b, dims=(((1,), (0,)), ((), ()))):
    return _dot(a.astype(BF16), b.astype(BF16), dims)


def _fdot(a, b, dims=(((1,), (0,)), ((), ()))):
    return _dot(a, b, dims, precision=lax.Precision.HIGH)


def _silu(x):
    return x * jax.nn.sigmoid(x)


def _stacks(x):
    if not isinstance(x, (list, tuple)) and x.ndim != 3:
        return None
    arrays = list(x) if isinstance(x, (list, tuple)) else [x]
    assert len({(v.shape[1], v.shape[2], v.dtype) for v in arrays}) == 1
    starts = [sum(v.shape[0] for v in arrays[:r]) for r in range(len(arrays))]
    return arrays, starts, starts[-1] + arrays[-1].shape[0]


def _static_pick(table, index):
    out = table[-1]
    for s in range(len(table) - 2, -1, -1):
        out = jnp.where(index == s, table[s], out)
    return out


def _matmul(a, b, *, name, trans_a=False, trans_b=False, residual=None, out_dtype=F32, tm=1024, tn=1024, tk=2048,
            col_slabs=0, order=None, exchange=(), with_pair=False):
    assert not (trans_a and trans_b)
    a_stack, b_stack = _stacks(a), _stacks(b)
    assert not (a_stack and (trans_a or b_stack)) and not (b_stack and trans_b)
    a_list, b_list = (a_stack[0] if a_stack else [a]), (b_stack[0] if b_stack else [b])
    a0, b0 = a_list[0], b_list[0]
    k, m = (a_stack[2] * a0.shape[2], a0.shape[1]) if a_stack else a.shape if trans_a else a.shape[::-1]
    n = b_stack[2] * b0.shape[2] if b_stack else b.shape[0] if trans_b else b.shape[1]
    tm, tn, tk = min(tm, m), min(tn, n // max(col_slabs, 1)), min(tk, k)
    if a_stack:
        tk = min(tk, a0.shape[2])
        per_k = a0.shape[2] // tk
    if b_stack:
        tn = min(tn, b0.shape[2])
        per_n = b0.shape[2] // tn
    assert m % tm == 0 and n % tn == 0 and k % tk == 0, (a0.shape, b0.shape, tm, tn, tk)
    nk = k // tk
    dims = _NT if trans_b else _TN if trans_a else (((1,), (0,)), ((), ()))
    order = list(order) if order is not None else list(range(max(a_stack[2] if a_stack else 0, b_stack[2] if b_stack else 0)))
    na, nb = len(a_list), len(b_list)

    def group_of(r, stack, position):
        arrays, starts, _ = stack
        local = position - starts[r]
        return jnp.logical_and(local >= 0, local < arrays[r].shape[0]), jnp.clip(local, 0, arrays[r].shape[0] - 1)

    def body(*refs):
        a_refs, b_refs = refs[:na], refs[na:na + nb]
        r_ref = refs[na + nb] if residual is not None else None
        o_ref, acc_ref = refs[-2:]
        j, kk = pl.program_id(1), pl.program_id(2)

        @pl.when(kk == 0)
        def _():
            acc_ref[...] = jnp.zeros_like(acc_ref)

        for ra, a_ref in enumerate(a_refs):
            for rb, b_ref in enumerate(b_refs):
                def add(a_ref=a_ref, b_ref=b_ref):
                    acc_ref[...] += _dot(a_ref[...], b_ref[...], dims)

                if na > 1:
                    pl.when(group_of(ra, a_stack, kk // per_k)[0])(add)
                elif nb > 1:
                    pl.when(group_of(rb, b_stack, j // per_n)[0])(add)
                else:
                    add()

        @pl.when(kk == nk - 1)
        def _():
            r = acc_ref[...]
            if r_ref is not None:
                r = r + r_ref[...]
            o_ref[...] = r.astype(o_ref.dtype)

    if a_stack:
        a_specs = [pl.BlockSpec((None, tm, tk), lambda i, j, kk, r=r: (group_of(r, a_stack, kk // per_k)[1], i, kk % per_k))
                   for r in range(na)]
        b_k = lambda kk: _static_pick(order, kk // per_k) * per_k + kk % per_k
    else:
        a_specs = [pl.BlockSpec((tk, tm), lambda i, j, kk: (kk, i)) if trans_a else pl.BlockSpec((tm, tk), lambda i, j, kk: (i, kk))]
        b_k = lambda kk: kk
    if b_stack:
        b_specs = [pl.BlockSpec((None, tk, tn), lambda i, j, kk, r=r: (group_of(r, b_stack, j // per_n)[1], kk, j % per_n))
                   for r in range(nb)]
        out_col = lambda j: _static_pick(order, j // per_n) * per_n + j % per_n
    else:
        b_specs = [pl.BlockSpec((tn, tk), lambda i, j, kk: (j, b_k(kk))) if trans_b
                   else pl.BlockSpec((tk, tn), lambda i, j, kk: (b_k(kk), j))]
        out_col = lambda j: j
    in_specs = a_specs + b_specs
    args = a_list + b_list
    if residual is not None:
        in_specs.append(pl.BlockSpec((tm, tn), lambda i, j, kk: (i, j)))
        args.append(residual)
    grid = (m // tm, n // tn, nk)
    n_x = len(exchange)
    if col_slabs:
        per = n // col_slabs // tn
        assert per * tn * col_slabs == n, (n, tn, col_slabs)
        out_spec = pl.BlockSpec((None, tm, tn), lambda i, j, kk: (out_col(j) // per, i, out_col(j) % per))
        out_shape = jax.ShapeDtypeStruct((col_slabs, m, n // col_slabs), out_dtype)
    else:
        out_spec = pl.BlockSpec((tm, tn), lambda i, j, kk: (i, out_col(j)))
        out_shape = jax.ShapeDtypeStruct((m, n), out_dtype)
    out, *landed = pl.pallas_call(
        _with_exchange(body, len(args), 1, "pair" if with_pair else False, n_x, grid),
        name=name,
        grid=grid,
        in_specs=in_specs + [_ANY] * n_x,
        out_specs=[out_spec] + [_ANY] * n_x,
        out_shape=[out_shape] + _chip_shapes(False, exchange),
        scratch_shapes=[pltpu.VMEM((tm, tn), F32)] + ((_pair_scratch if with_pair else _chip_scratch)(n_x) if n_x else []),
        compiler_params=_params(*(("arbitrary",) * 3 if n_x else ("parallel", "parallel", "arbitrary"))),
    )(*args, *exchange)
    return (out, landed) if n_x else out


def _rms(x, w):
    return x * lax.rsqrt(jnp.mean(x * x, axis=-1, keepdims=True) + EPS) * w


def _rmsnorm_fwd(x, w_row, narrow_w, *, name, tr=512):
    t, d = x.shape
    tr = min(tr, t)

    def body(x_ref, w_ref, nw_ref, o_ref, narrow_ref):
        hn = _rms(x_ref[...], w_ref[...]).astype(BF16)
        o_ref[...] = hn
        narrow_ref[...] = _dot(hn, nw_ref[...])

    return pl.pallas_call(
        body,
        name=name,
        grid=(t // tr,),
        in_specs=[pl.BlockSpec((tr, d), lambda i: (i, 0)), pl.BlockSpec((1, d), lambda i: (0, 0)),
                  pl.BlockSpec((d, LANES), lambda i: (0, 0))],
        out_specs=[pl.BlockSpec((tr, d), lambda i: (i, 0)), pl.BlockSpec((tr, LANES), lambda i: (i, 0))],
        out_shape=[jax.ShapeDtypeStruct((t, d), BF16), jax.ShapeDtypeStruct((t, LANES), F32)],
        compiler_params=_params("parallel"),
    )(x, w_row, narrow_w)


def _matmul_norm(a, b, residual, w_row, *, name, tm=512):
    t, k = a.shape
    d = b.shape[1]
    tm = min(tm, t)

    def body(a_ref, b_ref, r_ref, w_ref, h_ref, hn_ref):
        h = _dot(a_ref[...], b_ref[...]) + r_ref[...]
        h_ref[...] = h
        hn_ref[...] = _rms(h, w_ref[...]).astype(BF16)

    row = pl.BlockSpec((tm, d), lambda i: (i, 0))
    return pl.pallas_call(
        body,
        name=name,
        grid=(t // tm,),
        in_specs=[pl.BlockSpec((tm, k), lambda i: (i, 0)), pl.BlockSpec((k, d), lambda i: (0, 0)), row,
                  pl.BlockSpec((1, d), lambda i: (0, 0))],
        out_specs=[row, row],
        out_shape=[jax.ShapeDtypeStruct((t, d), F32), jax.ShapeDtypeStruct((t, d), BF16)],
        compiler_params=_params("parallel"),
    )(a, b, residual, w_row)


def _rmsnorm_bwd(x, w_row, dy, dres, *, name, tr=256, narrow=None):
    t, d = x.shape
    tr = min(tr, t)
    extra = list(narrow) if narrow is not None else []

    def body(x_ref, w_ref, dy_ref, dres_ref, *refs):
        dx_ref, dxb_ref, dw_ref = refs[len(extra):]

        @pl.when(pl.program_id(0) == 0)
        def _():
            dw_ref[...] = jnp.zeros_like(dw_ref)

        dy = dy_ref[...]
        if extra:
            dy = dy + _dot(refs[0][...], refs[1][...], _NT)
        _, vjp = jax.vjp(_rms, x_ref[...], w_ref[...])
        dx, dw = vjp(dy)
        dx = dx + dres_ref[...]
        dx_ref[...] = dx
        dxb_ref[...] = dx.astype(BF16)
        dw_ref[...] += dw

    row = pl.BlockSpec((tr, d), lambda i: (i, 0))
    vec = pl.BlockSpec((1, d), lambda i: (0, 0))
    extra_specs = [pl.BlockSpec((tr, LANES), lambda i: (i, 0)), pl.BlockSpec((d, LANES), lambda i: (0, 0))] if extra else []
    return pl.pallas_call(
        body,
        name=name,
        grid=(t // tr,),
        in_specs=[row, vec, row, row] + extra_specs,
        out_specs=[row, row, vec],
        out_shape=[jax.ShapeDtypeStruct((t, d), F32), jax.ShapeDtypeStruct((t, d), BF16), jax.ShapeDtypeStruct((1, d), F32)],
        compiler_params=_params("arbitrary"),
    )(x, w_row, dy, dres, *extra)


def _matmul_loss(a, b, residual, target, *, name, tm=512, tn=1024):
    t, k = a.shape
    d = b.shape[1]
    tm, tn = min(tm, t), min(tn, d)

    def body(a_ref, b_ref, r_ref, t_ref, dh_ref, dhb_ref, part_ref):
        @pl.when(pl.program_id(1) == 0)
        def _():
            part_ref[...] = jnp.zeros_like(part_ref)

        err = _dot(a_ref[...], b_ref[...]) + r_ref[...] - t_ref[...]
        dh = err * (1.0 / d)
        dh_ref[...] = dh
        dhb_ref[...] = dh.astype(BF16)
        part_ref[...] += jnp.sum(err * err, axis=0, keepdims=True)

    tile = pl.BlockSpec((tm, tn), lambda j, i: (i, j))
    dh, dhb, part = pl.pallas_call(
        body,
        name=name,
        grid=(d // tn, t // tm),
        in_specs=[pl.BlockSpec((tm, k), lambda j, i: (i, 0)), pl.BlockSpec((k, tn), lambda j, i: (0, j)), tile, tile],
        out_specs=[tile, tile, pl.BlockSpec((1, tn), lambda j, i: (0, j))],
        out_shape=[jax.ShapeDtypeStruct((t, d), F32), jax.ShapeDtypeStruct((t, d), BF16), jax.ShapeDtypeStruct((1, d), F32)],
        compiler_params=_params("arbitrary", "arbitrary"),
    )(a, b, residual, target)
    return 0.5 / d * jnp.sum(part), dh, dhb


_BNN = (((2,), (1,)), ((0,), (0,)))
_BNT = (((2,), (2,)), ((0,), (0,)))
_BTN = (((1,), (1,)), ((0,), (0,)))


_TAP0 = HALO - (CONV_K - 1)


def _conv(x_ref, w, rows):
    c = w[0:1, :] * x_ref[_TAP0:_TAP0 + rows, :]
    for j in range(1, CONV_K):
        c = c + w[j:j + 1, :] * x_ref[_TAP0 + j:_TAP0 + j + rows, :]
    return c


def _conv_silu_bwd(x_ref, w, dact, dc_ref, rows):
    c = _conv(x_ref, w, rows)
    sig = jax.nn.sigmoid(c)
    dc = dact * (sig * (1.0 + c * (1.0 - sig)))
    dw = [jnp.sum(dc * x_ref[_TAP0 + j:_TAP0 + j + rows, :], axis=0, keepdims=True) for j in range(CONV_K)]
    dc_ref[0:HALO, :] = jnp.zeros((HALO, HEAD_DIM), F32)
    dc_ref[HALO:HALO + rows, :] = dc
    dc_ref[HALO + rows:HALO + rows + HALO, :] = jnp.zeros((HALO, HEAD_DIM), F32)
    first = HALO - _TAP0
    dx = w[0:1, :] * dc_ref[first:first + HALO + rows, :]
    for j in range(1, CONV_K):
        dx = dx + w[j:j + 1, :] * dc_ref[first - j:first - j + HALO + rows, :]
    return dx, dw


@jax.custom_vjp
def _unit_lower_inverse(neg_l):
    n = neg_l.shape[0]
    eye = (lax.broadcasted_iota(jnp.int32, (n, CHUNK, CHUNK), 1) == lax.broadcasted_iota(jnp.int32, (n, CHUNK, CHUNK), 2))
    inv = eye.astype(F32) + neg_l
    power = _bdot(neg_l, neg_l, _BNN)
    for _ in range(4):
        both = _bdot(jnp.concatenate([inv, power], axis=1), power, _BNN)
        inv, power = inv + both[:, :CHUNK], both[:, CHUNK:]
    return inv + _bdot(inv, power, _BNN)


def _unit_lower_inverse_fwd(neg_l):
    inv = _unit_lower_inverse(neg_l)
    return inv, inv


def _unit_lower_inverse_bwd(inv, dinv):
    return (_fdot(_fdot(inv, dinv, _BTN), inv, _BNT),)


_unit_lower_inverse.defvjp(_unit_lower_inverse_fwd, _unit_lower_inverse_bwd)


def _gdn_intra(qt, kt, v, a, b, alog, dtb):
    n = a.shape[0] // CHUNK
    q = qt * lax.rsqrt(jnp.sum(qt * qt, axis=-1, keepdims=True) + EPS) * (HEAD_DIM ** -0.5)
    k = kt * lax.rsqrt(jnp.sum(kt * kt, axis=-1, keepdims=True) + EPS)
    lanes = jnp.ones((1, HEAD_DIM), F32)
    beta = jax.nn.sigmoid(b) * lanes
    sp = a + dtb
    g = (-jnp.exp(alog) * (jnp.maximum(sp, 0.0) + jnp.log(1.0 + jnp.exp(-jnp.abs(sp))))) * lanes
    q, k, v, beta, g = (t.reshape(n, CHUNK, HEAD_DIM) for t in (q, k, v, beta, g))

    row = lax.broadcasted_iota(jnp.int32, (n, CHUNK, CHUNK), 1)
    col = lax.broadcasted_iota(jnp.int32, (n, CHUNK, CHUNK), 2)
    tri_incl = row >= col
    tri_strict = row > col
    gc = _fdot(tri_incl.astype(F32), g, _BNN)
    gc_row = _fdot(g[:, :, :CHUNK], (row <= col).astype(F32), _BTN)
    decay = jnp.exp(jnp.where(tri_incl, gc[:, :, :CHUNK] - gc_row, -1e30))
    kb = k * beta
    vb = v * beta
    with_k = _bdot(jnp.concatenate([kb, q], axis=1), k, _BNT)
    neg_l = jnp.where(tri_strict, -(with_k[:, :CHUNK] * decay), 0.0)
    qk = jnp.where(tri_incl, with_k[:, CHUNK:] * decay, 0.0)
    inv = _unit_lower_inverse(neg_l)
    e = jnp.exp(gc)
    solved = _bdot(inv, jnp.concatenate([kb * e, vb], axis=2), _BNN)
    g_last = gc[:, CHUNK - 1:CHUNK, :]
    k_dec = k * jnp.exp(g_last - gc)
    from_k = _bdot(k_dec, solved, _BTN)
    from_qk = _bdot(qk, solved, _BNN)
    step, add = -from_k[:, :, :HEAD_DIM], from_k[:, :, HEAD_DIM:]
    read, out = q * e - from_qk[:, :, :HEAD_DIM], from_qk[:, :, HEAD_DIM:]
    return step, add, jnp.exp(g_last), read, out


def _gdn_scan_step(state, step, add, decay_last):
    return state * decay_last + _bdot(step, state) + add


def _gdn_outputs(states, read, out, z, onw):
    return _rms(_bdot(read, states, _BNN) + out, onw) * _silu(z)


def _scan_scratch(n, dtype):
    return [pltpu.VMEM((n, HEAD_DIM, HEAD_DIM), dtype), pltpu.VMEM((n, HEAD_DIM, HEAD_DIM), F32), pltpu.VMEM((n, 1, HEAD_DIM), F32)]


def _head_lane(h, offset=0):
    return lax.broadcasted_iota(jnp.int32, (1, LANES), 1) == h + offset


def _pick(mask, x):
    return jnp.sum(jnp.where(mask, x, 0.0), axis=1, keepdims=True)


def _gdn_specs(heads, tb, rev, nb, PAIR):
    assert heads % PAIR == 0
    blk = (lambda i: nb - 1 - i) if rev else (lambda i: i)
    hb = tb // HALO
    width, pairs = PAIR * HEAD_DIM, heads // PAIR

    def col(group):
        return pl.BlockSpec((tb, width), lambda i, h: (blk(i), group * pairs + h))

    def halo(group):
        return pl.BlockSpec((HALO, width), lambda i, h: (jnp.maximum(blk(i) * hb - 1, 0), group * pairs + h))

    def convw(group):
        return pl.BlockSpec((CONV_K, width), lambda i, h: (0, group * pairs + h))

    vec = pl.BlockSpec((1, LANES), lambda i, h: (0, 0))
    ab = pl.BlockSpec((tb, LANES), lambda i, h: (blk(i), 0))
    states = pl.BlockSpec((PAIR, tb // CHUNK, HEAD_DIM, HEAD_DIM), lambda i, h: (h, blk(i), 0, 0))
    return blk, col, halo, convw, vec, ab, states


def _head_cols(p):
    return slice(p * HEAD_DIM, (p + 1) * HEAD_DIM)


def _gdn_fwd(proj, ab, conv_w, alog_row, dtb_row, onw_row, *, heads, name, tb=1024, pair=4, gather=()):
    t = proj.shape[0]
    tb = min(tb, t)
    nb, cpb = t // tb, tb // CHUNK
    PAIR = min(pair, heads)
    _, col, halo, convw, vec, abspec, states = _gdn_specs(heads, tb, False, nb, PAIR)

    def body(q_ref, k_ref, v_ref, qh_ref, kh_ref, vh_ref, z_ref, ab_ref, wq_ref, wk_ref, wv_ref, alog_ref, dtb_ref, onw_ref,
             og_ref, st_ref, state_scr, x_scr, *op_scr):
        i, pair = pl.program_id(0), pl.program_id(1)
        abv = ab_ref[...]
        heads_here, later = [pair * PAIR + p for p in range(PAIR)], []
        for p, h in enumerate(heads_here):
            cols = _head_cols(p)
            for n, (ref, href) in enumerate(((q_ref, qh_ref), (k_ref, kh_ref), (v_ref, vh_ref))):
                x_scr[p, n, 0:HALO, :] = jnp.where(i > 0, href[:, cols], 0.0)
                x_scr[p, n, HALO:HALO + tb, :] = ref[:, cols]
            sel_a, sel_b = _head_lane(h), _head_lane(h, heads)
            alog, dtb = _pick(sel_a, alog_ref[...]), _pick(sel_a, dtb_ref[...])
            acts = [_silu(_conv(x_scr.at[p, n], w_ref[:, cols], tb)) for n, w_ref in enumerate((wq_ref, wk_ref, wv_ref))]
            *scan, read, out = _gdn_intra(*acts, _pick(sel_a, abv), _pick(sel_b, abv), alog, dtb)
            for scr, val in zip(op_scr[3 * p:3 * p + 3], scan):
                scr[...] = val.astype(scr.dtype)
            later.append((read, out))

        def chunk(c, states):
            for p in range(PAIR):
                st_ref[p, c] = states[p]
            return tuple(_gdn_scan_step(states[p], *[scr[c] for scr in op_scr[3 * p:3 * p + 3]]) for p in range(PAIR))

        @pl.when(i == 0)
        def _():
            for h in heads_here:
                state_scr[h] = jnp.zeros((HEAD_DIM, HEAD_DIM), F32)

        last = lax.fori_loop(0, cpb, chunk, tuple(state_scr[h] for h in heads_here))
        for p, h in enumerate(heads_here):
            cols = _head_cols(p)
            state_scr[h] = last[p]
            og = _gdn_outputs(st_ref[p], *later[p], z_ref[:, cols].reshape(cpb, CHUNK, HEAD_DIM), onw_ref[...])
            og_ref[:, cols] = og.reshape(tb, HEAD_DIM).astype(BF16)

    n_x = len(gather)
    grid = (nb, heads // PAIR)
    og, st, *gathered = pl.pallas_call(
        _with_exchange(body, 14, 2, True, n_x, grid),
        name=name,
        grid=grid,
        in_specs=[col(0), col(1), col(2), halo(0), halo(1), halo(2), col(3), abspec, convw(0), convw(1), convw(2), vec, vec, vec]
        + [_ANY] * n_x,
        out_specs=[pl.BlockSpec((tb, PAIR * HEAD_DIM), lambda i, h: (i, h)), states] + [_ANY] * n_x,
        out_shape=[jax.ShapeDtypeStruct((t, heads * HEAD_DIM), BF16),
                   jax.ShapeDtypeStruct((heads, t // CHUNK, HEAD_DIM, HEAD_DIM), F32)] + _chip_shapes(True, gather),
        scratch_shapes=[pltpu.VMEM((heads, HEAD_DIM, HEAD_DIM), F32), pltpu.VMEM((PAIR, 3, HALO + tb, HEAD_DIM), F32)]
        + _scan_scratch(cpb, BF16) * PAIR + (_chip_scratch(n_x) if n_x else []),
        compiler_params=_params("arbitrary", "arbitrary"),
    )(proj, proj, proj, proj, proj, proj, proj, ab, conv_w, conv_w, conv_w, alog_row, dtb_row, onw_row, *gather)
    return og, st, gathered


def _gdn_bwd(proj, ab, conv_w, alog_row, dtb_row, onw_row, states, dog, *, heads, name, tb=1024, pair=2, exchange=()):
    t = proj.shape[0]
    tb = min(tb, t)
    nb, cpb = t // tb, tb // CHUNK
    PAIR = min(pair, heads)
    _, col, halo, convw, vec, abspec, states_spec = _gdn_specs(heads, tb, True, nb, PAIR)
    n_conv = conv_w.shape[1]

    def body(q_ref, k_ref, v_ref, qh_ref, kh_ref, vh_ref, z_ref, ab_ref, wq_ref, wk_ref, wv_ref, alog_ref, dtb_ref, onw_ref,
             st_ref, dog_ref, dproj_ref, dab_ref, dconv_ref, dalog_ref, ddtb_ref, donw_ref,
             dstate_scr, x_scr, carry_scr, *scr):
        op_scr, dop_scr, dstates_scr, dc_scr = scr[:3 * PAIR], scr[3 * PAIR:6 * PAIR], scr[6 * PAIR:7 * PAIR], scr[7 * PAIR]
        i, pair = pl.program_id(0), pl.program_id(1)
        first_block = i == nb - 1
        heads_here, later = [pair * PAIR + p for p in range(PAIR)], []

        @pl.when(jnp.logical_and(i == 0, pair == 0))
        def _():
            dconv_ref[...] = jnp.zeros_like(dconv_ref)
            dalog_ref[...] = jnp.zeros_like(dalog_ref)
            ddtb_ref[...] = jnp.zeros_like(ddtb_ref)
            donw_ref[...] = jnp.zeros_like(donw_ref)

        @pl.when(pair == 0)
        def _():
            dab_ref[...] = jnp.zeros_like(dab_ref)

        @pl.when(i == 0)
        def _():
            for h in heads_here:
                dstate_scr[h] = jnp.zeros((HEAD_DIM, HEAD_DIM), F32)
                carry_scr[h] = jnp.zeros((3, HALO, HEAD_DIM), F32)

        abv = ab_ref[...]
        w_refs = (wq_ref, wk_ref, wv_ref)
        for p, h in enumerate(heads_here):
            cols = _head_cols(p)
            for n, (ref, href) in enumerate(((q_ref, qh_ref), (k_ref, kh_ref), (v_ref, vh_ref))):
                x_scr[p, n, 0:HALO, :] = jnp.where(first_block, 0.0, href[:, cols])
                x_scr[p, n, HALO:HALO + tb, :] = ref[:, cols]
            sel_a, sel_b = _head_lane(h), _head_lane(h, heads)
            alog, dtb = _pick(sel_a, alog_ref[...]), _pick(sel_a, dtb_ref[...])
            acts = [_silu(_conv(x_scr.at[p, n], w_ref[:, cols], tb)) for n, w_ref in enumerate(w_refs)]
            (*scan, read, out), vjp_intra = jax.vjp(_gdn_intra, *acts, _pick(sel_a, abv), _pick(sel_b, abv), alog, dtb)
            for s, val in zip(op_scr[3 * p:3 * p + 3], scan):
                s[...] = val.astype(s.dtype)
            blocked = lambda ref: ref[:, cols].reshape(cpb, CHUNK, HEAD_DIM)
            _, vjp_outputs = jax.vjp(_gdn_outputs, st_ref[p], read, out, blocked(z_ref), onw_ref[...])
            dstates_scr[p][...], dread, dout, dz, donw = vjp_outputs(blocked(dog_ref))
            dproj_ref[3, :, cols] = dz.reshape(tb, HEAD_DIM).astype(BF16)
            donw_ref[...] += donw
            later.append((vjp_intra, dread, dout, sel_a, sel_b))

        def chunk(i_rev, dstates):
            c = cpb - 1 - i_rev
            new = []
            for p in range(PAIR):
                _, vjp = jax.vjp(_gdn_scan_step, st_ref[p, c], *[s[c].astype(F32) for s in op_scr[3 * p:3 * p + 3]])
                dstate, *grads = vjp(dstates[p])
                for s, val in zip(dop_scr[3 * p:3 * p + 3], grads):
                    s[c] = val
                new.append(dstate + dstates_scr[p][c])
            return tuple(new)

        last = lax.fori_loop(0, cpb, chunk, tuple(dstate_scr[h] for h in heads_here))
        for p, h in enumerate(heads_here):
            cols = _head_cols(p)
            vjp_intra, dread, dout, sel_a, sel_b = later[p]
            dstate_scr[h] = last[p]
            *dacts, da, db, dalog, ddtb = vjp_intra((*[s[...] for s in dop_scr[3 * p:3 * p + 3]], dread, dout))
            dab_ref[...] += jnp.where(sel_a, da, 0.0) + jnp.where(sel_b, db, 0.0)
            for n, (dact, w_ref) in enumerate(zip(dacts, w_refs)):
                dx, dw = _conv_silu_bwd(x_scr.at[p, n], w_ref[:, cols], dact, dc_scr, tb)
                x_scr[p, n] = dx
                x_scr[p, n, tb:tb + HALO, :] += carry_scr[h, n]
                carry_scr[h, n] = x_scr[p, n, 0:HALO, :]
                dproj_ref[n, :, cols] = x_scr[p, n, HALO:HALO + tb, :].astype(BF16)
                lanes = pl.ds(pl.multiple_of((n * heads + h) * HEAD_DIM, HEAD_DIM), HEAD_DIM)
                for j in range(CONV_K):
                    dconv_ref[j:j + 1, lanes] += dw[j]
            dalog_ref[...] += jnp.where(sel_a, dalog, 0.0)
            ddtb_ref[...] += jnp.where(sel_a, ddtb, 0.0)

    dog_spec = pl.BlockSpec((tb, PAIR * HEAD_DIM), lambda i, h: (nb - 1 - i, h))
    dproj_spec = pl.BlockSpec((4, tb, PAIR * HEAD_DIM), lambda i, h: (0, nb - 1 - i, h))
    row_shape = jax.ShapeDtypeStruct((1, LANES), F32)
    n_x = len(exchange)
    grid = (nb, heads // PAIR)
    outs = pl.pallas_call(
        _with_exchange(body, 16, 6, False, n_x, grid),
        name=name,
        grid=grid,
        in_specs=[col(0), col(1), col(2), halo(0), halo(1), halo(2), col(3), abspec, convw(0), convw(1), convw(2), vec, vec, vec,
                  states_spec, dog_spec] + [_ANY] * n_x,
        out_specs=[dproj_spec, abspec, pl.BlockSpec((CONV_K, n_conv), lambda i, h: (0, 0)), vec, vec, vec] + [_ANY] * n_x,
        out_shape=[jax.ShapeDtypeStruct((4, t, heads * HEAD_DIM), BF16), jax.ShapeDtypeStruct((t, LANES), F32),
                   jax.ShapeDtypeStruct((CONV_K, n_conv), F32), row_shape, row_shape, row_shape] + _chip_shapes(False, exchange),
        scratch_shapes=[pltpu.VMEM((heads, HEAD_DIM, HEAD_DIM), F32), pltpu.VMEM((PAIR, 3, HALO + tb, HEAD_DIM), F32),
                        pltpu.VMEM((heads, 3, HALO, HEAD_DIM), F32)] + _scan_scratch(cpb, BF16) * PAIR
        + _scan_scratch(cpb, F32) * PAIR + [pltpu.VMEM((cpb, HEAD_DIM, HEAD_DIM), F32)] * PAIR
        + [pltpu.VMEM((HALO + tb + HALO, HEAD_DIM), F32)]
        + (_chip_scratch(n_x) if n_x else []),
        compiler_params=_params("arbitrary", "arbitrary", vmem=VMEM_LIMIT_WIDE_BYTES),
    )(proj, proj, proj, proj, proj, proj, proj, ab, conv_w, conv_w, conv_w, alog_row, dtb_row, onw_row, states, dog, *exchange)
    return (*outs[:6], outs[6:])


BAND = (LEFT_CHUNKS + 1) * CHUNK
PAD = LEFT_CHUNKS * CHUNK
GROUP = 2
ROWS = GROUP * CHUNK
WIN = (LEFT_CHUNKS + GROUP) * CHUNK
DIAGS = WIN + ROWS - 1
NEAR = PAD + ROWS - 1 - REL_CLIP
assert 0 < NEAR < DIAGS and WIN - PAD - 1 <= REL_CLIP and WIN % LANES == 0
ATTN_BLOCK = 1024


def _band_bias(rel_bias):
    heads = rel_bias.shape[0]
    far = jnp.broadcast_to(rel_bias[:, 2 * REL_CLIP:], (heads, NEAR + 1))
    near = rel_bias[:, 2 * REL_CLIP + NEAR + 1 - DIAGS:2 * REL_CLIP][:, ::-1]
    diag = jnp.concatenate([far, near], axis=1)
    return jnp.stack([diag[:, ROWS - 1 - r:ROWS - 1 - r + WIN] for r in range(ROWS)], axis=1)


def _band_bias_grad(dbias):
    heads = dbias.shape[0]
    diag = sum(jnp.pad(dbias[:, r, :], ((0, 0), (ROWS - 1 - r, r))) for r in range(ROWS))
    far = jnp.sum(diag[:, :NEAR + 1], axis=1, keepdims=True)
    near = diag[:, NEAR + 1:][:, ::-1]
    unused = jnp.zeros((heads, 2 * REL_CLIP - near.shape[1]), F32)
    return jnp.concatenate([unused, near, far], axis=1)


def _masked_bias(bias, n):
    r = np.arange(ROWS)[:, None]
    key = np.arange(WIN)[None, :]
    band_start = (r // CHUNK) * CHUNK
    in_band = np.logical_and(key >= band_start, key < band_start + BAND)
    in_sequence = key[None] >= PAD - np.arange(n)[:, None, None] * ROWS
    first = jnp.where(np.logical_and(in_band[None], in_sequence)[None], bias[:, None], -1e30)
    return first, jnp.where(in_band[None, None], bias[:, None], -1e30)


def _attn_groups(q_pre, z, kn, v, bias, qnw):
    q = _rms(q_pre, qnw)
    s = _bdot(q, kn, _BNT) * (HEAD_DIM ** -0.5) + bias
    p = jnp.exp(s - jnp.max(s, axis=-1, keepdims=True))
    p = p / jnp.sum(p, axis=-1, keepdims=True)
    return _bdot(p, v, _BNN) * _silu(z)


def _attn_groups_bwd(q_pre, z, kn, v, bias, qnw, dog):
    scale = HEAD_DIM ** -0.5
    inv_rms = lax.rsqrt(jnp.mean(q_pre * q_pre, axis=-1, keepdims=True) + EPS)
    q_hat = q_pre * inv_rms
    q_b = (q_hat * qnw).astype(BF16)
    s = _dot(q_b, kn, _BNT) * scale + bias
    e = jnp.exp(s - jnp.max(s, axis=-1, keepdims=True))
    p = e * (1.0 / jnp.sum(e, axis=-1, keepdims=True))
    p_b = p.astype(BF16)
    o = _dot(p_b, v, _BNN)
    sig = jax.nn.sigmoid(z)
    do = dog * (z * sig)
    dz = dog * o * (sig * (1.0 + z * (1.0 - sig)))
    do_b = do.astype(BF16)
    dv = _dot(p_b, do_b, _BTN)
    dp = _dot(do_b, v, _BNT)
    ds = p * (dp - jnp.sum(do * o, axis=-1, keepdims=True))
    ds_b = (ds * scale).astype(BF16)
    dq = _dot(ds_b, kn, _BNN)
    dkn = _dot(ds_b, q_b, _BTN)
    dqnw = jnp.sum(jnp.sum(dq * q_hat, axis=0), axis=0, keepdims=True)
    dq_hat = dq * qnw
    dq_pre = inv_rms * (dq_hat - q_hat * jnp.mean(dq_hat * q_hat, axis=-1, keepdims=True))
    return dq_pre, dz, dkn, dv, jnp.sum(ds, axis=0), dqnw


def _attn_specs(heads, tb, t):
    def col(group):
        return pl.BlockSpec((tb, HEAD_DIM), lambda h, i: (i, group * heads + h))

    def full(group):
        return pl.BlockSpec((t, HEAD_DIM), lambda h, i: (0, group * heads + h))

    bias = [pl.BlockSpec((1, tb // ROWS, ROWS, WIN), lambda h, i: (h, 0, 0, 0)),
            pl.BlockSpec((1, 1, ROWS, WIN), lambda h, i: (h, 0, 0, 0))]
    vec = pl.BlockSpec((1, HEAD_DIM), lambda h, i: (0, 0))
    return col, full, bias, vec


def _attn_windows(scr, block_start, n):
    return jnp.stack([scr[pl.ds(pl.multiple_of(block_start + g * ROWS, ROWS), WIN), :] for g in range(n)])


def _attn_fill(k_ref, v_ref, knw_ref, kn_scr, v_scr, t):
    kn_scr[0:PAD, :] = jnp.zeros((PAD, HEAD_DIM), BF16)
    v_scr[0:PAD, :] = jnp.zeros((PAD, HEAD_DIM), BF16)
    step = min(512, t)

    def fill(j, _):
        rows = pl.ds(pl.multiple_of(j * step, step), step)
        prows = pl.ds(pl.multiple_of(PAD + j * step, CHUNK), step)
        kn_scr[prows, :] = _rms(k_ref[rows, :], knw_ref[...]).astype(BF16)
        v_scr[prows, :] = v_ref[rows, :].astype(BF16)
        return 0

    lax.fori_loop(0, t // step, fill, 0)


def _attn_fwd(proj, bias, qnw_row, knw_row, *, heads, name, tb=ATTN_BLOCK):
    t = proj.shape[0]
    tb = min(tb, t)
    nb, ng = t // tb, tb // ROWS
    col, full, bias_spec, vec = _attn_specs(heads, tb, t)

    def body(q_ref, k_ref, v_ref, z_ref, first_ref, rest_ref, qnw_ref, knw_ref, og_ref, kn_scr, v_scr):
        i = pl.program_id(1)

        @pl.when(i == 0)
        def _():
            _attn_fill(k_ref, v_ref, knw_ref, kn_scr, v_scr, t)

        def run(block_bias):
            start = i * tb
            og = _attn_groups(q_ref[...].reshape(ng, ROWS, HEAD_DIM), z_ref[...].reshape(ng, ROWS, HEAD_DIM),
                              _attn_windows(kn_scr, start, ng), _attn_windows(v_scr, start, ng), block_bias, qnw_ref[...])
            og_ref[...] = og.reshape(tb, HEAD_DIM).astype(BF16)

        pl.when(i == 0)(lambda: run(first_ref[0]))
        pl.when(i > 0)(lambda: run(rest_ref[0]))

    return pl.pallas_call(
        body,
        name=name,
        grid=(heads, nb),
        in_specs=[col(0), full(1), full(2), col(3), *bias_spec, vec, vec],
        out_specs=pl.BlockSpec((tb, HEAD_DIM), lambda h, i: (i, h)),
        out_shape=jax.ShapeDtypeStruct((t, heads * HEAD_DIM), BF16),
        scratch_shapes=[pltpu.VMEM((PAD + t, HEAD_DIM), BF16), pltpu.VMEM((PAD + t, HEAD_DIM), BF16)],
        compiler_params=_params("arbitrary", "arbitrary"),
    )(proj, proj, proj, proj, *bias, qnw_row, knw_row)


def _attn_bwd(proj, bias, qnw_row, knw_row, dog, *, heads, name, tb=ATTN_BLOCK, sub=4):
    t = proj.shape[0]
    tb = min(tb, t)
    nb, ng = t // tb, tb // ROWS
    sub = min(sub, ng)
    col, full, bias_spec, vec = _attn_specs(heads, tb, t)

    def body(q_ref, k_ref, v_ref, z_ref, first_ref, rest_ref, qnw_ref, knw_ref, dog_ref,
             dqz_ref, dkv_ref, dbias_ref, dqnw_ref, dknw_ref, kn_scr, v_scr, dkn_scr, dv_scr):
        i = pl.program_id(1)

        @pl.when(i == 0)
        def _():
            _attn_fill(k_ref, v_ref, knw_ref, kn_scr, v_scr, t)
            dkn_scr[...] = jnp.zeros_like(dkn_scr)
            dv_scr[...] = jnp.zeros_like(dv_scr)
            dbias_ref[...] = jnp.zeros_like(dbias_ref)
            dqnw_ref[...] = jnp.zeros_like(dqnw_ref)

        def run(block_bias):
            for g0 in range(0, ng, sub):
                rows = pl.ds(g0 * ROWS, sub * ROWS)
                at = i * tb + g0 * ROWS
                blocked = lambda ref: ref[rows, :].reshape(sub, ROWS, HEAD_DIM)
                dq, dz, dkn, dv, dbias, dqnw = _attn_groups_bwd(
                    blocked(q_ref), blocked(z_ref), _attn_windows(kn_scr, at, sub), _attn_windows(v_scr, at, sub),
                    block_bias(g0), qnw_ref[...], blocked(dog_ref))
                dqz_ref[0, rows, :] = dq.reshape(sub * ROWS, HEAD_DIM).astype(BF16)
                dqz_ref[1, rows, :] = dz.reshape(sub * ROWS, HEAD_DIM).astype(BF16)
                for g in range(sub):
                    window = pl.ds(pl.multiple_of(at + g * ROWS, ROWS), WIN)
                    dkn_scr[window, :] += dkn[g]
                    dv_scr[window, :] += dv[g]
                dbias_ref[0] += dbias
                dqnw_ref[0] += dqnw

        pl.when(i == 0)(lambda: run(lambda g0: first_ref[0, g0:g0 + sub]))
        pl.when(i > 0)(lambda: run(lambda g0: rest_ref[0]))

        @pl.when(i == nb - 1)
        def _():
            step = min(512, t)

            def finish(j, dknw):
                rows = pl.ds(pl.multiple_of(j * step, step), step)
                prows = pl.ds(pl.multiple_of(PAD + j * step, CHUNK), step)
                _, vjp = jax.vjp(_rms, k_ref[rows, :], knw_ref[...])
                dk, dw = vjp(dkn_scr[prows, :])
                dkv_ref[0, rows, :] = dk.astype(BF16)
                dkv_ref[1, rows, :] = dv_scr[prows, :].astype(BF16)
                return dknw + dw

            dknw_ref[0] = lax.fori_loop(0, t // step, finish, jnp.zeros((1, HEAD_DIM), F32))

    pair_col = pl.BlockSpec((2, tb, HEAD_DIM), lambda h, i: (0, i, h))
    pair_full = pl.BlockSpec((2, t, HEAD_DIM), lambda h, i: (0, 0, h))
    head_vec = pl.BlockSpec((1, 1, HEAD_DIM), lambda h, i: (h, 0, 0))
    pair_shape = jax.ShapeDtypeStruct((2, t, heads * HEAD_DIM), BF16)
    vec_shape = jax.ShapeDtypeStruct((heads, 1, HEAD_DIM), F32)
    return pl.pallas_call(
        body,
        name=name,
        grid=(heads, nb),
        in_specs=[col(0), full(1), full(2), col(3), *bias_spec, vec, vec, pl.BlockSpec((tb, HEAD_DIM), lambda h, i: (i, h))],
        out_specs=[pair_col, pair_full, pl.BlockSpec((1, ROWS, WIN), lambda h, i: (h, 0, 0)), head_vec, head_vec],
        out_shape=[pair_shape, pair_shape, jax.ShapeDtypeStruct((heads, ROWS, WIN), F32), vec_shape, vec_shape],
        scratch_shapes=[pltpu.VMEM((PAD + t, HEAD_DIM), BF16), pltpu.VMEM((PAD + t, HEAD_DIM), BF16),
                        pltpu.VMEM((PAD + t, HEAD_DIM), F32), pltpu.VMEM((PAD + t, HEAD_DIM), F32)],
        compiler_params=_params("arbitrary", "arbitrary"),
    )(proj, proj, proj, proj, *bias, qnw_row, knw_row, dog)


def _lane_row(v):
    v = v.reshape(1, -1)
    return jnp.pad(v, ((0, 0), (0, LANES - v.shape[1])))


def _local_step(x, target, norm_w, wa_in, conv_w, a_log, dt_bias, onw, wa_out, wb_in, qnw, knw, rel_bias, wb_out, *,
                sharded=False):
    ha, hb = a_log.shape[-1], rel_bias.shape[-2]
    na = 4 * ha * HEAD_DIM
    wa_main = wa_in[:, :na]
    wa_ab = jnp.pad(wa_in[:, na:], ((0, 0), (0, LANES - 2 * ha)))
    alog_row, dtb_row, onw_row = _lane_row(a_log), _lane_row(dt_bias), _lane_row(onw)
    qnw_row, knw_row = _lane_row(qnw), _lane_row(knw)
    bias = _masked_bias(_band_bias(rel_bias.reshape(hb, -1)), min(ATTN_BLOCK, x.shape[0]) // ROWS)

    hn0, ab_a = _rmsnorm_fwd(x, norm_w[0:1], wa_ab, name="norm0")
    proj_a = _matmul(hn0, wa_main, name="a_in")
    og_a, states, got = _gdn_fwd(proj_a, ab_a, conv_w, alog_row, dtb_row, onw_row, heads=ha, name="gdn_fwd",
                                 gather=[wb_in, wa_out, wb_out] if sharded else [])
    if sharded:
        wb_in, wa_out, wb_out = _join_cols(got[0]), got[1].reshape(-1, got[1].shape[-1]), got[2].reshape(-1, got[2].shape[-1])
    h1, hn1 = _matmul_norm(og_a, wa_out, x, norm_w[1:2], name="a_out_norm1")
    proj_b = _matmul(hn1, wb_in, name="b_in")
    og_b = _attn_fwd(proj_b, bias, qnw_row, knw_row, heads=hb, name="attn_fwd")
    loss, dh2, dh2_b = _matmul_loss(og_b, wb_out, h1, target, name="b_out_loss")

    grad_dtype = BF16 if sharded else F32
    dog_b = _matmul(dh2_b, wb_out, trans_b=True, name="d_b_out_x")
    dwb_out = _matmul(og_b, dh2_b, trans_a=True, out_dtype=grad_dtype, name="d_b_out_w")
    dqz, dkv, dbias, dqnw, dknw = _attn_bwd(proj_b, bias, qnw_row, knw_row, dog_b, heads=hb, name="attn_bwd")
    dproj_b, qkvz = [dqz, dkv], (0, 3, 1, 2)
    dhn1 = _matmul(dproj_b, wb_in, trans_b=True, order=qkvz, name="d_b_in_x")
    dwb_in = _matmul(hn1, dproj_b, trans_a=True, order=qkvz, out_dtype=grad_dtype, col_slabs=N_CHIPS if sharded else 0,
                     name="d_b_in_w")
    dh1, dh1_b, dnw1 = _rmsnorm_bwd(h1, norm_w[1:2], dhn1, dh2, name="d_norm1")

    dog_a = _matmul(dh1_b, wa_out, trans_b=True, name="d_a_out_x")
    dwa_out = _matmul(og_a, dh1_b, trans_a=True, out_dtype=grad_dtype, name="d_a_out_w")
    early = [dwb_in, _split_rows(dwa_out), _split_rows(dwb_out)] if sharded else []
    dproj_a, dab, dconv, dalog, ddtb, donw, landed = _gdn_bwd(
        proj_a, ab_a, conv_w, alog_row, dtb_row, onw_row, states, dog_a, heads=ha, name="gdn_bwd", exchange=early)
    dab_b = dab.astype(BF16)
    if sharded:
        mine = [_sum_slots(s, name=f"chip_sum_{n}") for n, s in zip(("b_w_in", "a_w_out", "b_w_out"), landed)]
        dwa_main, theirs = _matmul(hn0, dproj_a, trans_a=True, out_dtype=grad_dtype, name="d_a_in_w", exchange=mine,
                                   with_pair=True)
        dwb_in, dwa_out, dwb_out = zip(mine, theirs)
    else:
        dwa_main = _matmul(hn0, dproj_a, trans_a=True, out_dtype=grad_dtype, name="d_a_in_w")
    dwa_in = jnp.concatenate(
        [dwa_main, _matmul(hn0, dab_b, trans_a=True, out_dtype=grad_dtype, name="d_a_in_ab_w")[:, :2 * ha]], axis=1)
    if sharded:
        dhn0, (dwa_in, dconv) = _matmul(dproj_a, wa_main, trans_b=True, name="d_a_in_x",
                                        exchange=[_split_cols(dwa_in), _split_cols(dconv)])
    else:
        dhn0 = _matmul(dproj_a, wa_main, trans_b=True, name="d_a_in_x")
    dx, _, dnw0 = _rmsnorm_bwd(x, norm_w[0:1], dhn0, dh1, narrow=(dab_b, wa_ab), name="d_norm0")

    drel = _band_bias_grad(dbias)
    grads = dict(
        norm_w=jnp.concatenate([dnw0, dnw1], axis=0), a_w_in=dwa_in, a_conv_w=dconv, a_a_log=dalog[:, :ha],
        a_dt_bias=ddtb[:, :ha], a_out_norm_w=donw, a_w_out=dwa_out, b_w_in=dwb_in, b_q_norm_w=jnp.sum(dqnw, axis=0),
        b_k_norm_w=jnp.sum(dknw, axis=0), b_rel_bias=drel[None], b_w_out=dwb_out)
    return loss, dx, grads


_ANY = pl.BlockSpec(memory_space=pl.ANY)
_CHIP_FLIPS = ((1, 0), (0, 1), (1, 1))


def _place():
    x, y, c = lax.axis_index("x"), lax.axis_index("y"), lax.axis_index("c")
    return x, y, c


def _flip(v, bit):
    return 1 - v if bit else v


def _remote(src, dst, send_sem, recv_sem, peer):
    return pltpu.make_async_remote_copy(src_ref=src, dst_ref=dst, send_sem=send_sem, recv_sem=recv_sem, device_id=peer,
                                        device_id_type=MESH)


def _chip_scratch(n):
    return [pltpu.SemaphoreType.DMA((3 * n,)), pltpu.SemaphoreType.DMA((3 * n,)), pltpu.SemaphoreType.DMA((n,))]


def _chip_shapes(gather, arrays):
    return [jax.ShapeDtypeStruct(((N_CHIPS,) + s.shape) if gather else s.shape, s.dtype) for s in arrays]


def _chip_traffic(gather, ins, outs, sems):
    send_sems, recv_sems, local_sems = sems
    x, y, c = _place()
    mine = 2 * x + y
    local, remote, landing = [], [], []
    for a in range(len(ins)):
        local.append(pltpu.make_async_copy(ins[a] if gather else ins[a].at[mine], outs[a].at[mine], local_sems.at[a]))
        for k, (fx, fy) in enumerate(_CHIP_FLIPS):
            peer = (_flip(x, fx), _flip(y, fy), c)
            theirs = 2 * peer[0] + peer[1]
            src = ins[a] if gather else ins[a].at[theirs]
            pair = send_sems.at[3 * a + k], recv_sems.at[3 * a + k]
            remote.append(_remote(src, outs[a].at[mine], *pair, peer))
            landing.append(_remote(src, outs[a].at[theirs], *pair, peer))
    return local + remote, (local, landing, remote)


def _start(traffic):
    for cp in traffic[0]:
        cp.start()


def _finish(traffic):
    local, landing, remote = traffic[1]
    for cp in local:
        cp.wait()
    for cp in landing:
        cp.wait_recv()
    for cp in remote:
        cp.wait_send()


def _pair_scratch(n):
    return [pltpu.SemaphoreType.DMA((n,)), pltpu.SemaphoreType.DMA((n,))]


def _pair_traffic(ins, outs, sems):
    send_sems, recv_sems = sems
    x, y, c = _place()
    copies = [_remote(ins[a], outs[a], send_sems.at[a], recv_sems.at[a], (x, y, 1 - c)) for a in range(len(ins))]
    return copies, ([], copies, copies)


def _exchange_plan(kind, arrays):
    if kind == "pair":
        return _chip_shapes(False, arrays), _pair_scratch(len(arrays))
    if kind == "all":
        return [jax.ShapeDtypeStruct((N_DEV,) + s.shape, s.dtype) for s in arrays], _all_scratch(len(arrays))
    return _chip_shapes(kind, arrays), _chip_scratch(len(arrays))


def _with_exchange(compute, n_in, n_out, gather, n_x, grid):
    if not n_x:
        return compute

    def body(*refs):
        ins, x_in = refs[:n_in], refs[n_in:n_in + n_x]
        outs, x_out = refs[n_in + n_x:n_in + n_x + n_out], refs[n_in + n_x + n_out:n_in + 2 * n_x + n_out]
        n_sems = 2 if gather == "pair" else 3
        scratch, sems = refs[n_in + 2 * n_x + n_out:-n_sems], refs[-n_sems:]
        if gather == "pair":
            traffic = _pair_traffic(x_in, x_out, sems)
        elif gather == "all":
            traffic = _all_traffic(x_in, x_out, sems)
        else:
            traffic = _chip_traffic(gather, x_in, x_out, sems)
        first = functools.reduce(jnp.logical_and, [pl.program_id(d) == 0 for d in range(len(grid))])
        last = functools.reduce(jnp.logical_and, [pl.program_id(d) == grid[d] - 1 for d in range(len(grid))])

        @pl.when(first)
        def _():
            _start(traffic)

        compute(*ins, *outs, *scratch)

        @pl.when(last)
        def _():
            _finish(traffic)

    return body


def _gather_shared(shard, small, *, name):
    rows = shard.shape[0]
    assert rows % 2 == 0
    half = rows // 2

    def body(shard_ref, small_ref, out_ref, small_out_ref, send_sems, recv_sems, local_sems):
        x, y, c = _place()
        mine = 2 * x + y
        sibling = (x, y, 1 - c)
        my_rows = pl.ds(pl.multiple_of(c * half, 8), half)
        local = [pltpu.make_async_copy(shard_ref, out_ref.at[mine], local_sems.at[0]),
                 pltpu.make_async_copy(small_ref, small_out_ref.at[mine], local_sems.at[1])]
        sent, landed, passed_on, handed = [], [], [], []
        for k, (fx, fy) in enumerate(_CHIP_FLIPS):
            peer = (_flip(x, fx), _flip(y, fy), c)
            theirs = 2 * peer[0] + peer[1]
            ici, d2d, tiny = [(send_sems.at[3 * n + k], recv_sems.at[3 * n + k]) for n in range(3)]
            sent.append(_remote(shard_ref.at[my_rows], out_ref.at[mine, my_rows], *ici, peer))
            landed.append(_remote(shard_ref.at[my_rows], out_ref.at[theirs, my_rows], *ici, peer))
            sent.append(_remote(small_ref, small_out_ref.at[mine], *tiny, peer))
            landed.append(_remote(small_ref, small_out_ref.at[theirs], *tiny, peer))
            passed_on.append(_remote(out_ref.at[theirs, my_rows], out_ref.at[theirs, my_rows], *d2d, sibling))
            other_rows = pl.ds(pl.multiple_of((1 - c) * half, 8), half)
            handed.append(_remote(out_ref.at[theirs, other_rows], out_ref.at[theirs, other_rows], *d2d, sibling))
        for cp in local + sent:
            cp.start()
        for k in range(3):
            landed[2 * k].wait_recv()
            passed_on[k].start()
        for k in range(3):
            landed[2 * k + 1].wait_recv()
            handed[k].wait_recv()
        for cp in local:
            cp.wait()
        for cp in sent + passed_on:
            cp.wait_send()

    return pl.pallas_call(
        body, name=name, in_specs=[_ANY] * 2, out_specs=[_ANY] * 2, out_shape=_chip_shapes(True, [shard, small]),
        scratch_shapes=[pltpu.SemaphoreType.DMA((9,)), pltpu.SemaphoreType.DMA((9,)), pltpu.SemaphoreType.DMA((2,))],
    )(shard, small)


def _all_scratch(n):
    assert n == 1
    return [pltpu.SemaphoreType.DMA((N_DEV - 1,)), pltpu.SemaphoreType.DMA((N_DEV - 1,)), pltpu.SemaphoreType.DMA((1,))]


def _all_traffic(ins, outs, sems):
    (in_ref,), (out_ref,), (send_sems, recv_sems, local_sems) = ins, outs, sems
    x, y, c = _place()
    mine = 4 * x + 2 * y + c
    local = [pltpu.make_async_copy(in_ref, out_ref.at[mine], local_sems.at[0])]
    remote, landing = [], []
    for k in range(1, N_DEV):
        peer = (_flip(x, k & 4), _flip(y, k & 2), _flip(c, k & 1))
        pair = send_sems.at[k - 1], recv_sems.at[k - 1]
        remote.append(_remote(in_ref, out_ref.at[mine], *pair, peer))
        landing.append(_remote(in_ref, out_ref.at[4 * peer[0] + 2 * peer[1] + peer[2]], *pair, peer))
    return local + remote, (local, landing, remote)


def _sum_slots(slabs, *, name, tr=128):
    s, r, c = slabs.shape
    tr = min(tr, r)

    def body(in_ref, o_ref):
        acc = in_ref[0].astype(F32)
        for j in range(1, s):
            acc = acc + in_ref[j].astype(F32)
        o_ref[...] = acc

    return pl.pallas_call(
        body, name=name, grid=(r // tr,),
        in_specs=[pl.BlockSpec((s, tr, c), lambda i: (0, i, 0))], out_specs=pl.BlockSpec((tr, c), lambda i: (i, 0)),
        out_shape=jax.ShapeDtypeStruct((r, c), F32), compiler_params=_params("parallel"))(slabs)


def _adamw_math(w, g, m, v):
    m = ADAM_B1 * m + (1.0 - ADAM_B1) * g
    v = ADAM_B2 * v + (1.0 - ADAM_B2) * (g * g)
    m_hat = m / (1.0 - ADAM_B1 ** ADAM_STEP)
    v_hat = v / (1.0 - ADAM_B2 ** ADAM_STEP)
    delta = -ADAM_LR * (m_hat / (jnp.sqrt(v_hat) + ADAM_EPS) + ADAM_WD * w)
    return delta, m, v


def _adamw(w, m, v, parts, *, name, tr=128, exchange=(), kind=None):
    r, c = w.shape
    tr = min(tr, r)
    s = len(parts)

    def body(w_ref, m_ref, v_ref, *refs):
        g_ref, d_ref, nm_ref, nv_ref = refs[s:]
        g = refs[0][...]
        for p_ref in refs[1:s]:
            g = g + p_ref[...]
        g_ref[...] = g
        d_ref[...], nm_ref[...], nv_ref[...] = _adamw_math(w_ref[...], g, m_ref[...], v_ref[...])

    blk = pl.BlockSpec((tr, c), lambda i: (i, 0))
    shape = jax.ShapeDtypeStruct((r, c), F32)
    n_x = len(exchange)
    x_shapes, x_scratch = _exchange_plan(kind, exchange) if n_x else ([], [])
    grid = (r // tr,)
    outs = pl.pallas_call(
        _with_exchange(body, 3 + s, 4, kind, n_x, grid), name=name, grid=grid, in_specs=[blk] * (3 + s) + [_ANY] * n_x,
        out_specs=[blk] * 4 + [_ANY] * n_x, out_shape=[shape] * 4 + x_shapes, scratch_shapes=x_scratch,
        compiler_params=_params("arbitrary" if n_x else "parallel"))(w, m, v, *parts, *exchange)
    return (outs[:4], outs[4:]) if n_x else outs


_BIG = ("a_w_in", "b_w_in", "a_w_out", "b_w_out", "a_conv_w")
_SMALL = ("norm_w", "a_a_log", "a_dt_bias", "a_out_norm_w", "b_q_norm_w", "b_k_norm_w", "b_rel_bias")
_ORDER = ("norm_w", "a_w_in", "a_conv_w", "a_a_log", "a_dt_bias", "a_out_norm_w", "a_w_out", "b_w_in", "b_q_norm_w",
          "b_k_norm_w", "b_rel_bias", "b_w_out")


def _join_cols(g):
    return jnp.transpose(g, (1, 0, 2)).reshape(g.shape[1], -1)


def _split_cols(g):
    return jnp.transpose(g.reshape(g.shape[0], N_CHIPS, -1), (1, 0, 2))


def _split_rows(g):
    return g.reshape(N_CHIPS, -1, g.shape[-1])


def _pack(d):
    flat = jnp.concatenate([d[n].reshape(-1) for n in _SMALL])
    return jnp.pad(flat, (0, -flat.shape[0] % LANES)).reshape(1, -1)


def _unpack(row, like):
    out, at = {}, 0
    for n in _SMALL:
        size = like[n].size
        out[n] = row[0, at:at + size].reshape(like[n].shape)
        at += size
    return out


def kernel(x, norm_w, a_w_in, a_conv_w, a_a_log, a_dt_bias, a_out_norm_w, a_w_out, b_w_in, b_q_norm_w, b_k_norm_w, b_rel_bias, b_w_out, loss_target, m_norm_w, m_a_w_in, m_a_conv_w, m_a_a_log, m_a_dt_bias, m_a_out_norm_w, m_a_w_out, m_b_w_in, m_b_q_norm_w, m_b_k_norm_w, m_b_rel_bias, m_b_w_out, v_norm_w, v_a_w_in, v_a_conv_w, v_a_a_log, v_a_dt_bias, v_a_out_norm_w, v_a_w_out, v_b_w_in, v_b_q_norm_w, v_b_k_norm_w, v_b_rel_bias, v_b_w_out):
    w = dict(norm_w=norm_w, a_w_in=a_w_in, a_conv_w=a_conv_w, a_a_log=a_a_log, a_dt_bias=a_dt_bias,
             a_out_norm_w=a_out_norm_w, a_w_out=a_w_out, b_w_in=b_w_in, b_q_norm_w=b_q_norm_w, b_k_norm_w=b_k_norm_w,
             b_rel_bias=b_rel_bias, b_w_out=b_w_out)
    m = dict(norm_w=m_norm_w, a_w_in=m_a_w_in, a_conv_w=m_a_conv_w, a_a_log=m_a_a_log, a_dt_bias=m_a_dt_bias,
             a_out_norm_w=m_a_out_norm_w, a_w_out=m_a_w_out, b_w_in=m_b_w_in, b_q_norm_w=m_b_q_norm_w,
             b_k_norm_w=m_b_k_norm_w, b_rel_bias=m_b_rel_bias, b_w_out=m_b_w_out)
    v = dict(norm_w=v_norm_w, a_w_in=v_a_w_in, a_conv_w=v_a_conv_w, a_a_log=v_a_a_log, a_dt_bias=v_a_dt_bias,
             a_out_norm_w=v_a_out_norm_w, a_w_out=v_a_w_out, b_w_in=v_b_w_in, b_q_norm_w=v_b_q_norm_w,
             b_k_norm_w=v_b_k_norm_w, b_rel_bias=v_b_rel_bias, b_w_out=v_b_w_out)

    wa_in, conv = _gather_shared(a_w_in[0].astype(BF16), a_conv_w[0], name="gather_a_in")
    loss, dx, grads = _local_step(
        x[0], loss_target[0], norm_w, _join_cols(wa_in), _join_cols(conv), a_a_log, a_dt_bias, a_out_norm_w,
        a_w_out[0].astype(BF16), b_w_in[0].astype(BF16), b_q_norm_w, b_k_norm_w, b_rel_bias, b_w_out[0].astype(BF16),
        sharded=True)
    loss = lax.psum(loss, ("x", "y", "c"))

    update = lambda n, **kw: _adamw(w[n][0], m[n][0], v[n][0], list(sums[n]), name=f"adamw_{n}", **kw)
    late = [n for n in _BIG if not isinstance(grads[n], tuple)]
    mine = [_sum_slots(grads[n], name=f"chip_sum_{n}") for n in late]
    sums = {n: grads[n] for n in _BIG if n not in late}
    out = {}
    out["b_w_in"], theirs = update("b_w_in", exchange=mine, kind="pair")
    sums.update(zip(late, zip(mine, theirs)))
    row = _pack(grads)
    out["b_w_out"], (tiles,) = update("b_w_out", exchange=[jnp.broadcast_to(row, (8, row.shape[1]))], kind="all")
    for n in _BIG:
        out[n] = [r[None] for r in (out[n] if n in out else update(n))]

    res = _adamw(_pack(w), _pack(m), _pack(v), [tiles[d, 0:1, :] for d in range(N_DEV)], name="adamw_small")
    unpacked = [_unpack(r, w) for r in res]
    for n in _SMALL:
        out[n] = [u[n] for u in unpacked]

    return (loss, dx[None], *[out[n][0] for n in _ORDER], *[out[n][1] for n in _ORDER], *[out[n][2] for n in _ORDER],
            *[out[n][3] for n in _ORDER])
```

```python
import functools

import numpy as np
import jax
import jax.numpy as jnp
from jax import lax
from jax.experimental import pallas as pl
from jax.experimental.pallas import tpu as pltpu

F32 = jnp.float32
BF16 = jnp.bfloat16

CHUNK = 64
HEAD_DIM = 128
LEFT_CHUNKS = 8
REL_CLIP = 256
CONV_K = 4
EPS = 1e-6
HALO = 8

ADAM_LR = 0.001
ADAM_B1 = 0.9
ADAM_B2 = 0.999
ADAM_EPS = 1e-08
ADAM_WD = 0.01
ADAM_STEP = 10

LANES = 128
N_CHIPS = 4
N_DEV = 8
VMEM_LIMIT_BYTES = 56 * 1024 * 1024
VMEM_LIMIT_WIDE_BYTES = 63 * 1024 * 1024
MESH = pl.DeviceIdType.MESH


def _params(*sem, vmem=VMEM_LIMIT_BYTES):
    return pltpu.CompilerParams(dimension_semantics=sem, vmem_limit_bytes=vmem)


def _dot(a, b, dims=(((1,), (0,)), ((), ())), precision=None):
    return lax.dot_general(a, b, dims, precision=precision, preferred_element_type=F32)


_NT = (((1,), (1,)), ((), ()))
_TN = (((0,), (0,)), ((), ()))


def _bdot(a, b, dims=(((1,), (0,)), ((), ()))):
    return _dot(a.astype(BF16), b.astype(BF16), dims)


def _fdot(a, b, dims=(((1,), (0,)), ((), ()))):
    return _dot(a, b, dims, precision=lax.Precision.HIGH)


def _silu(x):
    return x * jax.nn.sigmoid(x)


def _stacks(x):
    if not isinstance(x, (list, tuple)) and x.ndim != 3:
        return None
    arrays = list(x) if isinstance(x, (list, tuple)) else [x]
    assert len({(v.shape[1], v.shape[2], v.dtype) for v in arrays}) == 1
    starts = [sum(v.shape[0] for v in arrays[:r]) for r in range(len(arrays))]
    return arrays, starts, starts[-1] + arrays[-1].shape[0]


def _static_pick(table, index):
    out = table[-1]
    for s in range(len(table) - 2, -1, -1):
        out = jnp.where(index == s, table[s], out)
    return out


def _matmul(a, b, *, name, trans_a=False, trans_b=False, residual=None, out_dtype=F32, tm=1024, tn=1024, tk=2048,
            col_slabs=0, order=None, exchange=(), with_pair=False):
    assert not (trans_a and trans_b)
    a_stack, b_stack = _stacks(a), _stacks(b)
    assert not (a_stack and (trans_a or b_stack)) and not (b_stack and trans_b)
    a_list, b_list = (a_stack[0] if a_stack else [a]), (b_stack[0] if b_stack else [b])
    a0, b0 = a_list[0], b_list[0]
    k, m = (a_stack[2] * a0.shape[2], a0.shape[1]) if a_stack else a.shape if trans_a else a.shape[::-1]
    n = b_stack[2] * b0.shape[2] if b_stack else b.shape[0] if trans_b else b.shape[1]
    tm, tn, tk = min(tm, m), min(tn, n // max(col_slabs, 1)), min(tk, k)
    if a_stack:
        tk = min(tk, a0.shape[2])
        per_k = a0.shape[2] // tk
    if b_stack:
        tn = min(tn, b0.shape[2])
        per_n = b0.shape[2] // tn
    assert m % tm == 0 and n % tn == 0 and k % tk == 0, (a0.shape, b0.shape, tm, tn, tk)
    nk = k // tk
    dims = _NT if trans_b else _TN if trans_a else (((1,), (0,)), ((), ()))
    order = list(order) if order is not None else list(range(max(a_stack[2] if a_stack else 0, b_stack[2] if b_stack else 0)))
    na, nb = len(a_list), len(b_list)

    def group_of(r, stack, position):
        arrays, starts, _ = stack
        local = position - starts[r]
        return jnp.logical_and(local >= 0, local < arrays[r].shape[0]), jnp.clip(local, 0, arrays[r].shape[0] - 1)

    def body(*refs):
        a_refs, b_refs = refs[:na], refs[na:na + nb]
        r_ref = refs[na + nb] if residual is not None else None
        o_ref, acc_ref = refs[-2:]
        j, kk = pl.program_id(1), pl.program_id(2)

        @pl.when(kk == 0)
        def _():
            acc_ref[...] = jnp.zeros_like(acc_ref)

        for ra, a_ref in enumerate(a_refs):
            for rb, b_ref in enumerate(b_refs):
                def add(a_ref=a_ref, b_ref=b_ref):
                    acc_ref[...] += _dot(a_ref[...], b_ref[...], dims)

                if na > 1:
                    pl.when(group_of(ra, a_stack, kk // per_k)[0])(add)
                elif nb > 1:
                    pl.when(group_of(rb, b_stack, j // per_n)[0])(add)
                else:
                    add()

        @pl.when(kk == nk - 1)
        def _():
            r = acc_ref[...]
            if r_ref is not None:
                r = r + r_ref[...]
            o_ref[...] = r.astype(o_ref.dtype)

    if a_stack:
        a_specs = [pl.BlockSpec((None, tm, tk), lambda i, j, kk, r=r: (group_of(r, a_stack, kk // per_k)[1], i, kk % per_k))
                   for r in range(na)]
        b_k = lambda kk: _static_pick(order, kk // per_k) * per_k + kk % per_k
    else:
        a_specs = [pl.BlockSpec((tk, tm), lambda i, j, kk: (kk, i)) if trans_a else pl.BlockSpec((tm, tk), lambda i, j, kk: (i, kk))]
        b_k = lambda kk: kk
    if b_stack:
        b_specs = [pl.BlockSpec((None, tk, tn), lambda i, j, kk, r=r: (group_of(r, b_stack, j // per_n)[1], kk, j % per_n))
                   for r in range(nb)]
        out_col = lambda j: _static_pick(order, j // per_n) * per_n + j % per_n
    else:
        b_specs = [pl.BlockSpec((tn, tk), lambda i, j, kk: (j, b_k(kk))) if trans_b
                   else pl.BlockSpec((tk, tn), lambda i, j, kk: (b_k(kk), j))]
        out_col = lambda j: j
    in_specs = a_specs + b_specs
    args = a_list + b_list
    if residual is not None:
        in_specs.append(pl.BlockSpec((tm, tn), lambda i, j, kk: (i, j)))
        args.append(residual)
    grid = (m // tm, n // tn, nk)
    n_x = len(exchange)
    if col_slabs:
        per = n // col_slabs // tn
        assert per * tn * col_slabs == n, (n, tn, col_slabs)
        out_spec = pl.BlockSpec((None, tm, tn), lambda i, j, kk: (out_col(j) // per, i, out_col(j) % per))
        out_shape = jax.ShapeDtypeStruct((col_slabs, m, n // col_slabs), out_dtype)
    else:
        out_spec = pl.BlockSpec((tm, tn), lambda i, j, kk: (i, out_col(j)))
        out_shape = jax.ShapeDtypeStruct((m, n), out_dtype)
    out, *landed = pl.pallas_call(
        _with_exchange(body, len(args), 1, "pair" if with_pair else False, n_x, grid),
        name=name,
        grid=grid,
        in_specs=in_specs + [_ANY] * n_x,
        out_specs=[out_spec] + [_ANY] * n_x,
        out_shape=[out_shape] + _chip_shapes(False, exchange),
        scratch_shapes=[pltpu.VMEM((tm, tn), F32)] + ((_pair_scratch if with_pair else _chip_scratch)(n_x) if n_x else []),
        compiler_params=_params(*(("arbitrary",) * 3 if n_x else ("parallel", "parallel", "arbitrary"))),
    )(*args, *exchange)
    return (out, landed) if n_x else out


def _rms(x, w):
    return x * lax.rsqrt(jnp.mean(x * x, axis=-1, keepdims=True) + EPS) * w


def _rmsnorm_fwd(x, w_row, narrow_w, *, name, tr=512):
    t, d = x.shape
    tr = min(tr, t)

    def body(x_ref, w_ref, nw_ref, o_ref, narrow_ref):
        hn = _rms(x_ref[...], w_ref[...]).astype(BF16)
        o_ref[...] = hn
        narrow_ref[...] = _dot(hn, nw_ref[...])

    return pl.pallas_call(
        body,
        name=name,
        grid=(t // tr,),
        in_specs=[pl.BlockSpec((tr, d), lambda i: (i, 0)), pl.BlockSpec((1, d), lambda i: (0, 0)),
                  pl.BlockSpec((d, LANES), lambda i: (0, 0))],
        out_specs=[pl.BlockSpec((tr, d), lambda i: (i, 0)), pl.BlockSpec((tr, LANES), lambda i: (i, 0))],
        out_shape=[jax.ShapeDtypeStruct((t, d), BF16), jax.ShapeDtypeStruct((t, LANES), F32)],
        compiler_params=_params("parallel"),
    )(x, w_row, narrow_w)


def _matmul_norm(a, b, residual, w_row, *, name, tm=512):
    t, k = a.shape
    d = b.shape[1]
    tm = min(tm, t)

    def body(a_ref, b_ref, r_ref, w_ref, h_ref, hn_ref):
        h = _dot(a_ref[...], b_ref[...]) + r_ref[...]
        h_ref[...] = h
        hn_ref[...] = _rms(h, w_ref[...]).astype(BF16)

    row = pl.BlockSpec((tm, d), lambda i: (i, 0))
    return pl.pallas_call(
        body,
        name=name,
        grid=(t // tm,),
        in_specs=[pl.BlockSpec((tm, k), lambda i: (i, 0)), pl.BlockSpec((k, d), lambda i: (0, 0)), row,
                  pl.BlockSpec((1, d), lambda i: (0, 0))],
        out_specs=[row, row],
        out_shape=[jax.ShapeDtypeStruct((t, d), F32), jax.ShapeDtypeStruct((t, d), BF16)],
        compiler_params=_params("parallel"),
    )(a, b, residual, w_row)


def _rmsnorm_bwd(x, w_row, dy, dres, *, name, tr=256, narrow=None):
    t, d = x.shape
    tr = min(tr, t)
    extra = list(narrow) if narrow is not None else []

    def body(x_ref, w_ref, dy_ref, dres_ref, *refs):
        dx_ref, dxb_ref, dw_ref = refs[len(extra):]

        @pl.when(pl.program_id(0) == 0)
        def _():
            dw_ref[...] = jnp.zeros_like(dw_ref)

        dy = dy_ref[...]
        if extra:
            dy = dy + _dot(refs[0][...], refs[1][...], _NT)
        _, vjp = jax.vjp(_rms, x_ref[...], w_ref[...])
        dx, dw = vjp(dy)
        dx = dx + dres_ref[...]
        dx_ref[...] = dx
        dxb_ref[...] = dx.astype(BF16)
        dw_ref[...] += dw

    row = pl.BlockSpec((tr, d), lambda i: (i, 0))
    vec = pl.BlockSpec((1, d), lambda i: (0, 0))
    extra_specs = [pl.BlockSpec((tr, LANES), lambda i: (i, 0)), pl.BlockSpec((d, LANES), lambda i: (0, 0))] if extra else []
    return pl.pallas_call(
        body,
        name=name,
        grid=(t // tr,),
        in_specs=[row, vec, row, row] + extra_specs,
        out_specs=[row, row, vec],
        out_shape=[jax.ShapeDtypeStruct((t, d), F32), jax.ShapeDtypeStruct((t, d), BF16), jax.ShapeDtypeStruct((1, d), F32)],
        compiler_params=_params("arbitrary"),
    )(x, w_row, dy, dres, *extra)


def _matmul_loss(a, b, residual, target, *, name, tm=512, tn=1024):
    t, k = a.shape
    d = b.shape[1]
    tm, tn = min(tm, t), min(tn, d)

    def body(a_ref, b_ref, r_ref, t_ref, dh_ref, dhb_ref, part_ref):
        @pl.when(pl.program_id(1) == 0)
        def _():
            part_ref[...] = jnp.zeros_like(part_ref)

        err = _dot(a_ref[...], b_ref[...]) + r_ref[...] - t_ref[...]
        dh = err * (1.0 / d)
        dh_ref[...] = dh
        dhb_ref[...] = dh.astype(BF16)
        part_ref[...] += jnp.sum(err * err, axis=0, keepdims=True)

    tile = pl.BlockSpec((tm, tn), lambda j, i: (i, j))
    dh, dhb, part = pl.pallas_call(
        body,
        name=name,
        grid=(d // tn, t // tm),
        in_specs=[pl.BlockSpec((tm, k), lambda j, i: (i, 0)), pl.BlockSpec((k, tn), lambda j, i: (0, j)), tile, tile],
        out_specs=[tile, tile, pl.BlockSpec((1, tn), lambda j, i: (0, j))],
        out_shape=[jax.ShapeDtypeStruct((t, d), F32), jax.ShapeDtypeStruct((t, d), BF16), jax.ShapeDtypeStruct((1, d), F32)],
        compiler_params=_params("arbitrary", "arbitrary"),
    )(a, b, residual, target)
    return 0.5 / d * jnp.sum(part), dh, dhb


_BNN = (((2,), (1,)), ((0,), (0,)))
_BNT = (((2,), (2,)), ((0,), (0,)))
_BTN = (((1,), (1,)), ((0,), (0,)))


_TAP0 = HALO - (CONV_K - 1)


def _conv(x_ref, w, rows):
    c = w[0:1, :] * x_ref[_TAP0:_TAP0 + rows, :]
    for j in range(1, CONV_K):
        c = c + w[j:j + 1, :] * x_ref[_TAP0 + j:_TAP0 + j + rows, :]
    return c


def _conv_silu_bwd(x_ref, w, dact, dc_ref, rows):
    c = _conv(x_ref, w, rows)
    sig = jax.nn.sigmoid(c)
    dc = dact * (sig * (1.0 + c * (1.0 - sig)))
    dw = [jnp.sum(dc * x_ref[_TAP0 + j:_TAP0 + j + rows, :], axis=0, keepdims=True) for j in range(CONV_K)]
    dc_ref[0:HALO, :] = jnp.zeros((HALO, HEAD_DIM), F32)
    dc_ref[HALO:HALO + rows, :] = dc
    dc_ref[HALO + rows:HALO + rows + HALO, :] = jnp.zeros((HALO, HEAD_DIM), F32)
    first = HALO - _TAP0
    dx = w[0:1, :] * dc_ref[first:first + HALO + rows, :]
    for j in range(1, CONV_K):
        dx = dx + w[j:j + 1, :] * dc_ref[first - j:first - j + HALO + rows, :]
    return dx, dw


@jax.custom_vjp
def _unit_lower_inverse(neg_l):
    n = neg_l.shape[0]
    eye = (lax.broadcasted_iota(jnp.int32, (n, CHUNK, CHUNK), 1) == lax.broadcasted_iota(jnp.int32, (n, CHUNK, CHUNK), 2))
    inv = eye.astype(F32) + neg_l
    power = _bdot(neg_l, neg_l, _BNN)
    for _ in range(4):
        both = _bdot(jnp.concatenate([inv, power], axis=1), power, _BNN)
        inv, power = inv + both[:, :CHUNK], both[:, CHUNK:]
    return inv + _bdot(inv, power, _BNN)


def _unit_lower_inverse_fwd(neg_l):
    inv = _unit_lower_inverse(neg_l)
    return inv, inv


def _unit_lower_inverse_bwd(inv, dinv):
    return (_fdot(_fdot(inv, dinv, _BTN), inv, _BNT),)


_unit_lower_inverse.defvjp(_unit_lower_inverse_fwd, _unit_lower_inverse_bwd)


def _gdn_intra(qt, kt, v, a, b, alog, dtb):
    n = a.shape[0] // CHUNK
    q = qt * lax.rsqrt(jnp.sum(qt * qt, axis=-1, keepdims=True) + EPS) * (HEAD_DIM ** -0.5)
    k = kt * lax.rsqrt(jnp.sum(kt * kt, axis=-1, keepdims=True) + EPS)
    lanes = jnp.ones((1, HEAD_DIM), F32)
    beta = jax.nn.sigmoid(b) * lanes
    sp = a + dtb
    g = (-jnp.exp(alog) * (jnp.maximum(sp, 0.0) + jnp.log(1.0 + jnp.exp(-jnp.abs(sp))))) * lanes
    q, k, v, beta, g = (t.reshape(n, CHUNK, HEAD_DIM) for t in (q, k, v, beta, g))

    row = lax.broadcasted_iota(jnp.int32, (n, CHUNK, CHUNK), 1)
    col = lax.broadcasted_iota(jnp.int32, (n, CHUNK, CHUNK), 2)
    tri_incl = row >= col
    tri_strict = row > col
    gc = _fdot(tri_incl.astype(F32), g, _BNN)
    gc_row = _fdot(g[:, :, :CHUNK], (row <= col).astype(F32), _BTN)
    decay = jnp.exp(jnp.where(tri_incl, gc[:, :, :CHUNK] - gc_row, -1e30))
    kb = k * beta
    vb = v * beta
    with_k = _bdot(jnp.concatenate([kb, q], axis=1), k, _BNT)
    neg_l = jnp.where(tri_strict, -(with_k[:, :CHUNK] * decay), 0.0)
    qk = jnp.where(tri_incl, with_k[:, CHUNK:] * decay, 0.0)
    inv = _unit_lower_inverse(neg_l)
    e = jnp.exp(gc)
    solved = _bdot(inv, jnp.concatenate([kb * e, vb], axis=2), _BNN)
    g_last = gc[:, CHUNK - 1:CHUNK, :]
    k_dec = k * jnp.exp(g_last - gc)
    from_k = _bdot(k_dec, solved, _BTN)
    from_qk = _bdot(qk, solved, _BNN)
    step, add = -from_k[:, :, :HEAD_DIM], from_k[:, :, HEAD_DIM:]
    read, out = q * e - from_qk[:, :, :HEAD_DIM], from_qk[:, :, HEAD_DIM:]
    return step, add, jnp.exp(g_last), read, out


def _gdn_scan_step(state, step, add, decay_last):
    return state * decay_last + _bdot(step, state) + add


def _gdn_outputs(states, read, out, z, onw):
    return _rms(_bdot(read, states, _BNN) + out, onw) * _silu(z)


def _scan_scratch(n, dtype):
    return [pltpu.VMEM((n, HEAD_DIM, HEAD_DIM), dtype), pltpu.VMEM((n, HEAD_DIM, HEAD_DIM), F32), pltpu.VMEM((n, 1, HEAD_DIM), F32)]


def _head_lane(h, offset=0):
    return lax.broadcasted_iota(jnp.int32, (1, LANES), 1) == h + offset


def _pick(mask, x):
    return jnp.sum(jnp.where(mask, x, 0.0), axis=1, keepdims=True)


def _gdn_specs(heads, tb, rev, nb, PAIR):
    assert heads % PAIR == 0
    blk = (lambda i: nb - 1 - i) if rev else (lambda i: i)
    hb = tb // HALO
    width, pairs = PAIR * HEAD_DIM, heads // PAIR

    def col(group):
        return pl.BlockSpec((tb, width), lambda i, h: (blk(i), group * pairs + h))

    def halo(group):
        return pl.BlockSpec((HALO, width), lambda i, h: (jnp.maximum(blk(i) * hb - 1, 0), group * pairs + h))

    def convw(group):
        return pl.BlockSpec((CONV_K, width), lambda i, h: (0, group * pairs + h))

    vec = pl.BlockSpec((1, LANES), lambda i, h: (0, 0))
    ab = pl.BlockSpec((tb, LANES), lambda i, h: (blk(i), 0))
    states = pl.BlockSpec((PAIR, tb // CHUNK, HEAD_DIM, HEAD_DIM), lambda i, h: (h, blk(i), 0, 0))
    return blk, col, halo, convw, vec, ab, states


def _head_cols(p):
    return slice(p * HEAD_DIM, (p + 1) * HEAD_DIM)


def _gdn_fwd(proj, ab, conv_w, alog_row, dtb_row, onw_row, *, heads, name, tb=1024, pair=4, gather=()):
    t = proj.shape[0]
    tb = min(tb, t)
    nb, cpb = t // tb, tb // CHUNK
    PAIR = min(pair, heads)
    _, col, halo, convw, vec, abspec, states = _gdn_specs(heads, tb, False, nb, PAIR)

    def body(q_ref, k_ref, v_ref, qh_ref, kh_ref, vh_ref, z_ref, ab_ref, wq_ref, wk_ref, wv_ref, alog_ref, dtb_ref, onw_ref,
             og_ref, st_ref, state_scr, x_scr, *op_scr):
        i, pair = pl.program_id(0), pl.program_id(1)
        abv = ab_ref[...]
        heads_here, later = [pair * PAIR + p for p in range(PAIR)], []
        for p, h in enumerate(heads_here):
            cols = _head_cols(p)
            for n, (ref, href) in enumerate(((q_ref, qh_ref), (k_ref, kh_ref), (v_ref, vh_ref))):
                x_scr[p, n, 0:HALO, :] = jnp.where(i > 0, href[:, cols], 0.0)
                x_scr[p, n, HALO:HALO + tb, :] = ref[:, cols]
            sel_a, sel_b = _head_lane(h), _head_lane(h, heads)
            alog, dtb = _pick(sel_a, alog_ref[...]), _pick(sel_a, dtb_ref[...])
            acts = [_silu(_conv(x_scr.at[p, n], w_ref[:, cols], tb)) for n, w_ref in enumerate((wq_ref, wk_ref, wv_ref))]
            *scan, read, out = _gdn_intra(*acts, _pick(sel_a, abv), _pick(sel_b, abv), alog, dtb)
            for scr, val in zip(op_scr[3 * p:3 * p + 3], scan):
                scr[...] = val.astype(scr.dtype)
            later.append((read, out))

        def chunk(c, states):
            for p in range(PAIR):
                st_ref[p, c] = states[p]
            return tuple(_gdn_scan_step(states[p], *[scr[c] for scr in op_scr[3 * p:3 * p + 3]]) for p in range(PAIR))

        @pl.when(i == 0)
        def _():
            for h in heads_here:
                state_scr[h] = jnp.zeros((HEAD_DIM, HEAD_DIM), F32)

        last = lax.fori_loop(0, cpb, chunk, tuple(state_scr[h] for h in heads_here))
        for p, h in enumerate(heads_here):
            cols = _head_cols(p)
            state_scr[h] = last[p]
            og = _gdn_outputs(st_ref[p], *later[p], z_ref[:, cols].reshape(cpb, CHUNK, HEAD_DIM), onw_ref[...])
            og_ref[:, cols] = og.reshape(tb, HEAD_DIM).astype(BF16)

    n_x = len(gather)
    grid = (nb, heads // PAIR)
    og, st, *gathered = pl.pallas_call(
        _with_exchange(body, 14, 2, True, n_x, grid),
        name=name,
        grid=grid,
        in_specs=[col(0), col(1), col(2), halo(0), halo(1), halo(2), col(3), abspec, convw(0), convw(1), convw(2), vec, vec, vec]
        + [_ANY] * n_x,
        out_specs=[pl.BlockSpec((tb, PAIR * HEAD_DIM), lambda i, h: (i, h)), states] + [_ANY] * n_x,
        out_shape=[jax.ShapeDtypeStruct((t, heads * HEAD_DIM), BF16),
                   jax.ShapeDtypeStruct((heads, t // CHUNK, HEAD_DIM, HEAD_DIM), F32)] + _chip_shapes(True, gather),
        scratch_shapes=[pltpu.VMEM((heads, HEAD_DIM, HEAD_DIM), F32), pltpu.VMEM((PAIR, 3, HALO + tb, HEAD_DIM), F32)]
        + _scan_scratch(cpb, BF16) * PAIR + (_chip_scratch(n_x) if n_x else []),
        compiler_params=_params("arbitrary", "arbitrary"),
    )(proj, proj, proj, proj, proj, proj, proj, ab, conv_w, conv_w, conv_w, alog_row, dtb_row, onw_row, *gather)
    return og, st, gathered


def _gdn_bwd(proj, ab, conv_w, alog_row, dtb_row, onw_row, states, dog, *, heads, name, tb=1024, pair=2, exchange=()):
    t = proj.shape[0]
    tb = min(tb, t)
    nb, cpb = t // tb, tb // CHUNK
    PAIR = min(pair, heads)
    _, col, halo, convw, vec, abspec, states_spec = _gdn_specs(heads, tb, True, nb, PAIR)
    n_conv = conv_w.shape[1]

    def body(q_ref, k_ref, v_ref, qh_ref, kh_ref, vh_ref, z_ref, ab_ref, wq_ref, wk_ref, wv_ref, alog_ref, dtb_ref, onw_ref,
             st_ref, dog_ref, dproj_ref, dab_ref, dconv_ref, dalog_ref, ddtb_ref, donw_ref,
             dstate_scr, x_scr, carry_scr, *scr):
        op_scr, dop_scr, dstates_scr, dc_scr = scr[:3 * PAIR], scr[3 * PAIR:6 * PAIR], scr[6 * PAIR:7 * PAIR], scr[7 * PAIR]
        i, pair = pl.program_id(0), pl.program_id(1)
        first_block = i == nb - 1
        heads_here, later = [pair * PAIR + p for p in range(PAIR)], []

        @pl.when(jnp.logical_and(i == 0, pair == 0))
        def _():
            dconv_ref[...] = jnp.zeros_like(dconv_ref)
            dalog_ref[...] = jnp.zeros_like(dalog_ref)
            ddtb_ref[...] = jnp.zeros_like(ddtb_ref)
            donw_ref[...] = jnp.zeros_like(donw_ref)

        @pl.when(pair == 0)
        def _():
            dab_ref[...] = jnp.zeros_like(dab_ref)

        @pl.when(i == 0)
        def _():
            for h in heads_here:
                dstate_scr[h] = jnp.zeros((HEAD_DIM, HEAD_DIM), F32)
                carry_scr[h] = jnp.zeros((3, HALO, HEAD_DIM), F32)

        abv = ab_ref[...]
        w_refs = (wq_ref, wk_ref, wv_ref)
        for p, h in enumerate(heads_here):
            cols = _head_cols(p)
            for n, (ref, href) in enumerate(((q_ref, qh_ref), (k_ref, kh_ref), (v_ref, vh_ref))):
                x_scr[p, n, 0:HALO, :] = jnp.where(first_block, 0.0, href[:, cols])
                x_scr[p, n, HALO:HALO + tb, :] = ref[:, cols]
            sel_a, sel_b = _head_lane(h), _head_lane(h, heads)
            alog, dtb = _pick(sel_a, alog_ref[...]), _pick(sel_a, dtb_ref[...])
            acts = [_silu(_conv(x_scr.at[p, n], w_ref[:, cols], tb)) for n, w_ref in enumerate(w_refs)]
            (*scan, read, out), vjp_intra = jax.vjp(_gdn_intra, *acts, _pick(sel_a, abv), _pick(sel_b, abv), alog, dtb)
            for s, val in zip(op_scr[3 * p:3 * p + 3], scan):
                s[...] = val.astype(s.dtype)
            blocked = lambda ref: ref[:, cols].reshape(cpb, CHUNK, HEAD_DIM)
            _, vjp_outputs = jax.vjp(_gdn_outputs, st_ref[p], read, out, blocked(z_ref), onw_ref[...])
            dstates_scr[p][...], dread, dout, dz, donw = vjp_outputs(blocked(dog_ref))
            dproj_ref[3, :, cols] = dz.reshape(tb, HEAD_DIM).astype(BF16)
            donw_ref[...] += donw
            later.append((vjp_intra, dread, dout, sel_a, sel_b))

        def chunk(i_rev, dstates):
            c = cpb - 1 - i_rev
            new = []
            for p in range(PAIR):
                _, vjp = jax.vjp(_gdn_scan_step, st_ref[p, c], *[s[c].astype(F32) for s in op_scr[3 * p:3 * p + 3]])
                dstate, *grads = vjp(dstates[p])
                for s, val in zip(dop_scr[3 * p:3 * p + 3], grads):
                    s[c] = val
                new.append(dstate + dstates_scr[p][c])
            return tuple(new)

        last = lax.fori_loop(0, cpb, chunk, tuple(dstate_scr[h] for h in heads_here))
        for p, h in enumerate(heads_here):
            cols = _head_cols(p)
            vjp_intra, dread, dout, sel_a, sel_b = later[p]
            dstate_scr[h] = last[p]
            *dacts, da, db, dalog, ddtb = vjp_intra((*[s[...] for s in dop_scr[3 * p:3 * p + 3]], dread, dout))
            dab_ref[...] += jnp.where(sel_a, da, 0.0) + jnp.where(sel_b, db, 0.0)
            for n, (dact, w_ref) in enumerate(zip(dacts, w_refs)):
                dx, dw = _conv_silu_bwd(x_scr.at[p, n], w_ref[:, cols], dact, dc_scr, tb)
                x_scr[p, n] = dx
                x_scr[p, n, tb:tb + HALO, :] += carry_scr[h, n]
                carry_scr[h, n] = x_scr[p, n, 0:HALO, :]
                dproj_ref[n, :, cols] = x_scr[p, n, HALO:HALO + tb, :].astype(BF16)
                lanes = pl.ds(pl.multiple_of((n * heads + h) * HEAD_DIM, HEAD_DIM), HEAD_DIM)
                for j in range(CONV_K):
                    dconv_ref[j:j + 1, lanes] += dw[j]
            dalog_ref[...] += jnp.where(sel_a, dalog, 0.0)
            ddtb_ref[...] += jnp.where(sel_a, ddtb, 0.0)

    dog_spec = pl.BlockSpec((tb, PAIR * HEAD_DIM), lambda i, h: (nb - 1 - i, h))
    dproj_spec = pl.BlockSpec((4, tb, PAIR * HEAD_DIM), lambda i, h: (0, nb - 1 - i, h))
    row_shape = jax.ShapeDtypeStruct((1, LANES), F32)
    n_x = len(exchange)
    grid = (nb, heads // PAIR)
    outs = pl.pallas_call(
        _with_exchange(body, 16, 6, False, n_x, grid),
        name=name,
        grid=grid,
        in_specs=[col(0), col(1), col(2), halo(0), halo(1), halo(2), col(3), abspec, convw(0), convw(1), convw(2), vec, vec, vec,
                  states_spec, dog_spec] + [_ANY] * n_x,
        out_specs=[dproj_spec, abspec, pl.BlockSpec((CONV_K, n_conv), lambda i, h: (0, 0)), vec, vec, vec] + [_ANY] * n_x,
        out_shape=[jax.ShapeDtypeStruct((4, t, heads * HEAD_DIM), BF16), jax.ShapeDtypeStruct((t, LANES), F32),
                   jax.ShapeDtypeStruct((CONV_K, n_conv), F32), row_shape, row_shape, row_shape] + _chip_shapes(False, exchange),
        scratch_shapes=[pltpu.VMEM((heads, HEAD_DIM, HEAD_DIM), F32), pltpu.VMEM((PAIR, 3, HALO + tb, HEAD_DIM), F32),
                        pltpu.VMEM((heads, 3, HALO, HEAD_DIM), F32)] + _scan_scratch(cpb, BF16) * PAIR
        + _scan_scratch(cpb, F32) * PAIR + [pltpu.VMEM((cpb, HEAD_DIM, HEAD_DIM), F32)] * PAIR
        + [pltpu.VMEM((HALO + tb + HALO, HEAD_DIM), F32)]
        + (_chip_scratch(n_x) if n_x else []),
        compiler_params=_params("arbitrary", "arbitrary", vmem=VMEM_LIMIT_WIDE_BYTES),
    )(proj, proj, proj, proj, proj, proj, proj, ab, conv_w, conv_w, conv_w, alog_row, dtb_row, onw_row, states, dog, *exchange)
    return (*outs[:6], outs[6:])


BAND = (LEFT_CHUNKS + 1) * CHUNK
PAD = LEFT_CHUNKS * CHUNK
GROUP = 2
ROWS = GROUP * CHUNK
WIN = (LEFT_CHUNKS + GROUP) * CHUNK
DIAGS = WIN + ROWS - 1
NEAR = PAD + ROWS - 1 - REL_CLIP
assert 0 < NEAR < DIAGS and WIN - PAD - 1 <= REL_CLIP and WIN % LANES == 0
ATTN_BLOCK = 1024
N_EDGE = PAD // ROWS


def _band_bias(rel_bias):
    heads = rel_bias.shape[0]
    far = jnp.broadcast_to(rel_bias[:, 2 * REL_CLIP:], (heads, NEAR + 1))
    near = rel_bias[:, 2 * REL_CLIP + NEAR + 1 - DIAGS:2 * REL_CLIP][:, ::-1]
    diag = jnp.concatenate([far, near], axis=1)
    return jnp.stack([diag[:, ROWS - 1 - r:ROWS - 1 - r + WIN] for r in range(ROWS)], axis=1)


def _band_bias_grad(dbias):
    heads = dbias.shape[0]
    diag = sum(jnp.pad(dbias[:, r, :], ((0, 0), (ROWS - 1 - r, r))) for r in range(ROWS))
    far = jnp.sum(diag[:, :NEAR + 1], axis=1, keepdims=True)
    near = diag[:, NEAR + 1:][:, ::-1]
    unused = jnp.zeros((heads, 2 * REL_CLIP - near.shape[1]), F32)
    return jnp.concatenate([unused, near, far], axis=1)


def _masked_bias(bias, n):
    r = np.arange(ROWS)[:, None]
    key = np.arange(WIN)[None, :]
    band_start = (r // CHUNK) * CHUNK
    in_band = np.logical_and(key >= band_start, key < band_start + BAND)
    in_sequence = key[None] >= PAD - np.arange(n)[:, None, None] * ROWS
    first = jnp.where(np.logical_and(in_band[None], in_sequence)[None], bias[:, None], -1e30)
    return first, jnp.where(in_band[None, None], bias[:, None], -1e30)


def _attn_groups(q_pre, z, kn, v, bias, qnw):
    q = _rms(q_pre, qnw)
    s = _bdot(q, kn, _BNT) * (HEAD_DIM ** -0.5) + bias
    p = jnp.exp(s - jnp.max(s, axis=-1, keepdims=True))
    p = p / jnp.sum(p, axis=-1, keepdims=True)
    return _bdot(p, v, _BNN) * _silu(z)


def _attn_groups_bwd(q_pre, z, kn, v, bias, qnw, dog):
    scale = HEAD_DIM ** -0.5
    inv_rms = lax.rsqrt(jnp.mean(q_pre * q_pre, axis=-1, keepdims=True) + EPS)
    q_hat = q_pre * inv_rms
    q_b = (q_hat * qnw).astype(BF16)
    s = _dot(q_b, kn, _BNT) * scale + bias
    e = jnp.exp(s - jnp.max(s, axis=-1, keepdims=True))
    p = e * (1.0 / jnp.sum(e, axis=-1, keepdims=True))
    p_b = p.astype(BF16)
    o = _dot(p_b, v, _BNN)
    sig = jax.nn.sigmoid(z)
    do = dog * (z * sig)
    dz = dog * o * (sig * (1.0 + z * (1.0 - sig)))
    do_b = do.astype(BF16)
    dv = _dot(p_b, do_b, _BTN)
    dp = _dot(do_b, v, _BNT)
    ds = p * (dp - jnp.sum(do * o, axis=-1, keepdims=True))
    ds_b = (ds * scale).astype(BF16)
    dq = _dot(ds_b, kn, _BNN)
    dkn = _dot(ds_b, q_b, _BTN)
    dqnw = jnp.sum(jnp.sum(dq * q_hat, axis=0), axis=0, keepdims=True)
    dq_hat = dq * qnw
    dq_pre = inv_rms * (dq_hat - q_hat * jnp.mean(dq_hat * q_hat, axis=-1, keepdims=True))
    return dq_pre, dz, dkn, dv, jnp.sum(ds, axis=0), dqnw


def _attn_specs(heads, tb, t):
    def col(group):
        return pl.BlockSpec((tb, HEAD_DIM), lambda h, i: (i, group * heads + h))

    def full(group):
        return pl.BlockSpec((t, HEAD_DIM), lambda h, i: (0, group * heads + h))

    bias = [pl.BlockSpec((1, min(tb // ROWS, N_EDGE), ROWS, WIN), lambda h, i: (h, 0, 0, 0)),
            pl.BlockSpec((1, 1, ROWS, WIN), lambda h, i: (h, 0, 0, 0))]
    vec = pl.BlockSpec((1, HEAD_DIM), lambda h, i: (0, 0))
    return col, full, bias, vec


def _attn_windows(scr, block_start, n):
    return jnp.stack([scr[pl.ds(pl.multiple_of(block_start + g * ROWS, ROWS), WIN), :] for g in range(n)])


def _attn_fill(k_ref, v_ref, knw_ref, kn_scr, v_scr, t):
    kn_scr[0:PAD, :] = jnp.zeros((PAD, HEAD_DIM), BF16)
    v_scr[0:PAD, :] = jnp.zeros((PAD, HEAD_DIM), BF16)
    step = min(512, t)

    def fill(j, _):
        rows = pl.ds(pl.multiple_of(j * step, step), step)
        prows = pl.ds(pl.multiple_of(PAD + j * step, CHUNK), step)
        kn_scr[prows, :] = _rms(k_ref[rows, :], knw_ref[...]).astype(BF16)
        v_scr[prows, :] = v_ref[rows, :].astype(BF16)
        return 0

    lax.fori_loop(0, t // step, fill, 0)


def _attn_fwd(proj, bias, qnw_row, knw_row, *, heads, name, tb=ATTN_BLOCK):
    t = proj.shape[0]
    tb = min(tb, t)
    nb, ng = t // tb, tb // ROWS
    col, full, bias_spec, vec = _attn_specs(heads, tb, t)

    def body(q_ref, k_ref, v_ref, z_ref, first_ref, rest_ref, qnw_ref, knw_ref, og_ref, kn_scr, v_scr):
        i = pl.program_id(1)

        @pl.when(i == 0)
        def _():
            _attn_fill(k_ref, v_ref, knw_ref, kn_scr, v_scr, t)

        def run(block_bias):
            start = i * tb
            og = _attn_groups(q_ref[...].reshape(ng, ROWS, HEAD_DIM), z_ref[...].reshape(ng, ROWS, HEAD_DIM),
                              _attn_windows(kn_scr, start, ng), _attn_windows(v_scr, start, ng), block_bias, qnw_ref[...])
            og_ref[...] = og.reshape(tb, HEAD_DIM).astype(BF16)

        def first_bias():
            edge = first_ref[0]
            more = ng - edge.shape[0]
            return edge if more == 0 else jnp.concatenate([edge, jnp.broadcast_to(rest_ref[0], (more, ROWS, WIN))])

        pl.when(i == 0)(lambda: run(first_bias()))
        pl.when(i > 0)(lambda: run(rest_ref[0]))

    return pl.pallas_call(
        body,
        name=name,
        grid=(heads, nb),
        in_specs=[col(0), full(1), full(2), col(3), *bias_spec, vec, vec],
        out_specs=pl.BlockSpec((tb, HEAD_DIM), lambda h, i: (i, h)),
        out_shape=jax.ShapeDtypeStruct((t, heads * HEAD_DIM), BF16),
        scratch_shapes=[pltpu.VMEM((PAD + t, HEAD_DIM), BF16), pltpu.VMEM((PAD + t, HEAD_DIM), BF16)],
        compiler_params=_params("arbitrary", "arbitrary"),
    )(proj, proj, proj, proj, *bias, qnw_row, knw_row)


def _attn_bwd(proj, bias, qnw_row, knw_row, dog, *, heads, name, tb=ATTN_BLOCK, sub=4):
    t = proj.shape[0]
    tb = min(tb, t)
    nb, ng = t // tb, tb // ROWS
    sub = min(sub, ng)
    n_edge = min(ng, N_EDGE)
    assert n_edge % sub == 0
    col, full, bias_spec, vec = _attn_specs(heads, tb, t)

    def body(q_ref, k_ref, v_ref, z_ref, first_ref, rest_ref, qnw_ref, knw_ref, dog_ref,
             dqz_ref, dkv_ref, dbias_ref, dqnw_ref, dknw_ref, kn_scr, v_scr, dkn_scr, dv_scr):
        i = pl.program_id(1)

        @pl.when(i == 0)
        def _():
            _attn_fill(k_ref, v_ref, knw_ref, kn_scr, v_scr, t)
            dkn_scr[...] = jnp.zeros_like(dkn_scr)
            dv_scr[...] = jnp.zeros_like(dv_scr)
            dbias_ref[...] = jnp.zeros_like(dbias_ref)
            dqnw_ref[...] = jnp.zeros_like(dqnw_ref)

        def run(block_bias):
            for g0 in range(0, ng, sub):
                rows = pl.ds(g0 * ROWS, sub * ROWS)
                at = i * tb + g0 * ROWS
                blocked = lambda ref: ref[rows, :].reshape(sub, ROWS, HEAD_DIM)
                dq, dz, dkn, dv, dbias, dqnw = _attn_groups_bwd(
                    blocked(q_ref), blocked(z_ref), _attn_windows(kn_scr, at, sub), _attn_windows(v_scr, at, sub),
                    block_bias(g0), qnw_ref[...], blocked(dog_ref))
                dqz_ref[0, rows, :] = dq.reshape(sub * ROWS, HEAD_DIM).astype(BF16)
                dqz_ref[1, rows, :] = dz.reshape(sub * ROWS, HEAD_DIM).astype(BF16)
                for g in range(sub):
                    window = pl.ds(pl.multiple_of(at + g * ROWS, ROWS), WIN)
                    dkn_scr[window, :] += dkn[g]
                    dv_scr[window, :] += dv[g]
                dbias_ref[0] += dbias
                dqnw_ref[0] += dqnw

        pl.when(i == 0)(lambda: run(lambda g0: first_ref[0, g0:g0 + sub] if g0 + sub <= n_edge else rest_ref[0]))
        pl.when(i > 0)(lambda: run(lambda g0: rest_ref[0]))

        @pl.when(i == nb - 1)
        def _():
            step = min(512, t)

            def finish(j, dknw):
                rows = pl.ds(pl.multiple_of(j * step, step), step)
                prows = pl.ds(pl.multiple_of(PAD + j * step, CHUNK), step)
                _, vjp = jax.vjp(_rms, k_ref[rows, :], knw_ref[...])
                dk, dw = vjp(dkn_scr[prows, :])
                dkv_ref[0, rows, :] = dk.astype(BF16)
                dkv_ref[1, rows, :] = dv_scr[prows, :].astype(BF16)
                return dknw + dw

            dknw_ref[0] = lax.fori_loop(0, t // step, finish, jnp.zeros((1, HEAD_DIM), F32))

    pair_col = pl.BlockSpec((2, tb, HEAD_DIM), lambda h, i: (0, i, h))
    pair_full = pl.BlockSpec((2, t, HEAD_DIM), lambda h, i: (0, 0, h))
    head_vec = pl.BlockSpec((1, 1, HEAD_DIM), lambda h, i: (h, 0, 0))
    pair_shape = jax.ShapeDtypeStruct((2, t, heads * HEAD_DIM), BF16)
    vec_shape = jax.ShapeDtypeStruct((heads, 1, HEAD_DIM), F32)
    return pl.pallas_call(
        body,
        name=name,
        grid=(heads, nb),
        in_specs=[col(0), full(1), full(2), col(3), *bias_spec, vec, vec, pl.BlockSpec((tb, HEAD_DIM), lambda h, i: (i, h))],
        out_specs=[pair_col, pair_full, pl.BlockSpec((1, ROWS, WIN), lambda h, i: (h, 0, 0)), head_vec, head_vec],
        out_shape=[pair_shape, pair_shape, jax.ShapeDtypeStruct((heads, ROWS, WIN), F32), vec_shape, vec_shape],
        scratch_shapes=[pltpu.VMEM((PAD + t, HEAD_DIM), BF16), pltpu.VMEM((PAD + t, HEAD_DIM), BF16),
                        pltpu.VMEM((PAD + t, HEAD_DIM), F32), pltpu.VMEM((PAD + t, HEAD_DIM), F32)],
        compiler_params=_params("arbitrary", "arbitrary"),
    )(proj, proj, proj, proj, *bias, qnw_row, knw_row, dog)


def _lane_row(v):
    v = v.reshape(1, -1)
    return jnp.pad(v, ((0, 0), (0, LANES - v.shape[1])))


def _local_step(x, target, norm_w, wa_in, conv_w, a_log, dt_bias, onw, wa_out, wb_in, qnw, knw, rel_bias, wb_out, *,
                sharded=False):
    ha, hb = a_log.shape[-1], rel_bias.shape[-2]
    na = 4 * ha * HEAD_DIM
    wa_main = wa_in[:, :na]
    wa_ab = jnp.pad(wa_in[:, na:], ((0, 0), (0, LANES - 2 * ha)))
    alog_row, dtb_row, onw_row = _lane_row(a_log), _lane_row(dt_bias), _lane_row(onw)
    qnw_row, knw_row = _lane_row(qnw), _lane_row(knw)
    bias = _masked_bias(_band_bias(rel_bias.reshape(hb, -1)), min(min(ATTN_BLOCK, x.shape[0]) // ROWS, N_EDGE))

    hn0, ab_a = _rmsnorm_fwd(x, norm_w[0:1], wa_ab, name="norm0")
    proj_a = _matmul(hn0, wa_main, name="a_in")
    og_a, states, got = _gdn_fwd(proj_a, ab_a, conv_w, alog_row, dtb_row, onw_row, heads=ha, name="gdn_fwd",
                                 gather=[wb_in, wa_out, wb_out] if sharded else [])
    if sharded:
        wb_in, wa_out, wb_out = _join_cols(got[0]), got[1].reshape(-1, got[1].shape[-1]), got[2].reshape(-1, got[2].shape[-1])
    h1, hn1 = _matmul_norm(og_a, wa_out, x, norm_w[1:2], name="a_out_norm1")
    proj_b = _matmul(hn1, wb_in, name="b_in")
    og_b = _attn_fwd(proj_b, bias, qnw_row, knw_row, heads=hb, name="attn_fwd")
    loss, dh2, dh2_b = _matmul_loss(og_b, wb_out, h1, target, name="b_out_loss")

    grad_dtype = BF16 if sharded else F32
    dog_b = _matmul(dh2_b, wb_out, trans_b=True, name="d_b_out_x")
    dwb_out = _matmul(og_b, dh2_b, trans_a=True, out_dtype=grad_dtype, name="d_b_out_w")
    dqz, dkv, dbias, dqnw, dknw = _attn_bwd(proj_b, bias, qnw_row, knw_row, dog_b, heads=hb, name="attn_bwd")
    dproj_b, qkvz = [dqz, dkv], (0, 3, 1, 2)
    dhn1 = _matmul(dproj_b, wb_in, trans_b=True, order=qkvz, name="d_b_in_x")
    dwb_in = _matmul(hn1, dproj_b, trans_a=True, order=qkvz, out_dtype=grad_dtype, col_slabs=N_CHIPS if sharded else 0,
                     name="d_b_in_w")
    dh1, dh1_b, dnw1 = _rmsnorm_bwd(h1, norm_w[1:2], dhn1, dh2, name="d_norm1")

    dog_a = _matmul(dh1_b, wa_out, trans_b=True, name="d_a_out_x")
    dwa_out = _matmul(og_a, dh1_b, trans_a=True, out_dtype=grad_dtype, name="d_a_out_w")
    early = [dwb_in, _split_rows(dwa_out), _split_rows(dwb_out)] if sharded else []
    dproj_a, dab, dconv, dalog, ddtb, donw, landed = _gdn_bwd(
        proj_a, ab_a, conv_w, alog_row, dtb_row, onw_row, states, dog_a, heads=ha, name="gdn_bwd", exchange=early)
    dab_b = dab.astype(BF16)
    if sharded:
        mine = [_sum_slots(s, name=f"chip_sum_{n}") for n, s in zip(("b_w_in", "a_w_out", "b_w_out"), landed)]
        dwa_main, theirs = _matmul(hn0, dproj_a, trans_a=True, out_dtype=grad_dtype, name="d_a_in_w", exchange=mine,
                                   with_pair=True)
        dwb_in, dwa_out, dwb_out = zip(mine, theirs)
    else:
        dwa_main = _matmul(hn0, dproj_a, trans_a=True, out_dtype=grad_dtype, name="d_a_in_w")
    dwa_in = jnp.concatenate(
        [dwa_main, _matmul(hn0, dab_b, trans_a=True, out_dtype=grad_dtype, name="d_a_in_ab_w")[:, :2 * ha]], axis=1)
    if sharded:
        dhn0, (dwa_in, dconv) = _matmul(dproj_a, wa_main, trans_b=True, name="d_a_in_x",
                                        exchange=[_split_cols(dwa_in), _split_cols(dconv)])
    else:
        dhn0 = _matmul(dproj_a, wa_main, trans_b=True, name="d_a_in_x")
    dx, _, dnw0 = _rmsnorm_bwd(x, norm_w[0:1], dhn0, dh1, narrow=(dab_b, wa_ab), name="d_norm0")

    drel = _band_bias_grad(dbias)
    grads = dict(
        norm_w=jnp.concatenate([dnw0, dnw1], axis=0), a_w_in=dwa_in, a_conv_w=dconv, a_a_log=dalog[:, :ha],
        a_dt_bias=ddtb[:, :ha], a_out_norm_w=donw, a_w_out=dwa_out, b_w_in=dwb_in, b_q_norm_w=jnp.sum(dqnw, axis=0),
        b_k_norm_w=jnp.sum(dknw, axis=0), b_rel_bias=drel[None], b_w_out=dwb_out)
    return loss, dx, grads


_ANY = pl.BlockSpec(memory_space=pl.ANY)
_CHIP_FLIPS = ((1, 0), (0, 1), (1, 1))


def _place():
    x, y, c = lax.axis_index("x"), lax.axis_index("y"), lax.axis_index("c")
    return x, y, c


def _flip(v, bit):
    return 1 - v if bit else v


def _remote(src, dst, send_sem, recv_sem, peer):
    return pltpu.make_async_remote_copy(src_ref=src, dst_ref=dst, send_sem=send_sem, recv_sem=recv_sem, device_id=peer,
                                        device_id_type=MESH)


def _comm_call(body, arrays, out_shapes, n_remote, n_local, name):
    scratch = [pltpu.SemaphoreType.DMA((n_remote,)), pltpu.SemaphoreType.DMA((n_remote,))]
    if n_local:
        scratch.append(pltpu.SemaphoreType.DMA((n_local,)))
    return pl.pallas_call(
        body, name=name, in_specs=[_ANY] * len(arrays), out_specs=[_ANY] * len(out_shapes), out_shape=out_shapes,
        scratch_shapes=scratch)(*arrays)


def _chip_scratch(n):
    return [pltpu.SemaphoreType.DMA((3 * n,)), pltpu.SemaphoreType.DMA((3 * n,)), pltpu.SemaphoreType.DMA((n,))]


def _chip_shapes(gather, arrays):
    return [jax.ShapeDtypeStruct(((N_CHIPS,) + s.shape) if gather else s.shape, s.dtype) for s in arrays]


def _chip_traffic(gather, ins, outs, sems):
    send_sems, recv_sems, local_sems = sems
    x, y, c = _place()
    mine = 2 * x + y
    local, remote, landing = [], [], []
    for a in range(len(ins)):
        local.append(pltpu.make_async_copy(ins[a] if gather else ins[a].at[mine], outs[a].at[mine], local_sems.at[a]))
        for k, (fx, fy) in enumerate(_CHIP_FLIPS):
            peer = (_flip(x, fx), _flip(y, fy), c)
            theirs = 2 * peer[0] + peer[1]
            src = ins[a] if gather else ins[a].at[theirs]
            pair = send_sems.at[3 * a + k], recv_sems.at[3 * a + k]
            remote.append(_remote(src, outs[a].at[mine], *pair, peer))
            landing.append(_remote(src, outs[a].at[theirs], *pair, peer))
    return local + remote, (local, landing, remote)


def _start(traffic):
    for cp in traffic[0]:
        cp.start()


def _finish(traffic):
    local, landing, remote = traffic[1]
    for cp in local:
        cp.wait()
    for cp in landing:
        cp.wait_recv()
    for cp in remote:
        cp.wait_send()


def _pair_scratch(n):
    return [pltpu.SemaphoreType.DMA((n,)), pltpu.SemaphoreType.DMA((n,))]


def _pair_traffic(ins, outs, sems):
    send_sems, recv_sems = sems
    x, y, c = _place()
    copies = [_remote(ins[a], outs[a], send_sems.at[a], recv_sems.at[a], (x, y, 1 - c)) for a in range(len(ins))]
    return copies, ([], copies, copies)


def _with_exchange(compute, n_in, n_out, gather, n_x, grid):
    if not n_x:
        return compute

    def body(*refs):
        ins, x_in = refs[:n_in], refs[n_in:n_in + n_x]
        outs, x_out = refs[n_in + n_x:n_in + n_x + n_out], refs[n_in + n_x + n_out:n_in + 2 * n_x + n_out]
        n_sems = 2 if gather == "pair" else 3
        scratch, sems = refs[n_in + 2 * n_x + n_out:-n_sems], refs[-n_sems:]
        traffic = _pair_traffic(x_in, x_out, sems) if gather == "pair" else _chip_traffic(gather, x_in, x_out, sems)
        first = functools.reduce(jnp.logical_and, [pl.program_id(d) == 0 for d in range(len(grid))])
        last = functools.reduce(jnp.logical_and, [pl.program_id(d) == grid[d] - 1 for d in range(len(grid))])

        @pl.when(first)
        def _():
            _start(traffic)

        compute(*ins, *outs, *scratch)

        @pl.when(last)
        def _():
            _finish(traffic)

    return body


def _gather_shared(shard, small, *, name):
    rows = shard.shape[0]
    assert rows % 2 == 0
    half = rows // 2

    def body(shard_ref, small_ref, out_ref, small_out_ref, send_sems, recv_sems, local_sems):
        x, y, c = _place()
        mine = 2 * x + y
        sibling = (x, y, 1 - c)
        my_rows = pl.ds(pl.multiple_of(c * half, 8), half)
        local = [pltpu.make_async_copy(shard_ref, out_ref.at[mine], local_sems.at[0]),
                 pltpu.make_async_copy(small_ref, small_out_ref.at[mine], local_sems.at[1])]
        sent, landed, passed_on, handed = [], [], [], []
        for k, (fx, fy) in enumerate(_CHIP_FLIPS):
            peer = (_flip(x, fx), _flip(y, fy), c)
            theirs = 2 * peer[0] + peer[1]
            ici, d2d, tiny = [(send_sems.at[3 * n + k], recv_sems.at[3 * n + k]) for n in range(3)]
            sent.append(_remote(shard_ref.at[my_rows], out_ref.at[mine, my_rows], *ici, peer))
            landed.append(_remote(shard_ref.at[my_rows], out_ref.at[theirs, my_rows], *ici, peer))
            sent.append(_remote(small_ref, small_out_ref.at[mine], *tiny, peer))
            landed.append(_remote(small_ref, small_out_ref.at[theirs], *tiny, peer))
            passed_on.append(_remote(out_ref.at[theirs, my_rows], out_ref.at[theirs, my_rows], *d2d, sibling))
            other_rows = pl.ds(pl.multiple_of((1 - c) * half, 8), half)
            handed.append(_remote(out_ref.at[theirs, other_rows], out_ref.at[theirs, other_rows], *d2d, sibling))
        for cp in local + sent:
            cp.start()
        for k in range(3):
            landed[2 * k].wait_recv()
            passed_on[k].start()
        for k in range(3):
            landed[2 * k + 1].wait_recv()
            handed[k].wait_recv()
        for cp in local:
            cp.wait()
        for cp in sent + passed_on:
            cp.wait_send()

    return pl.pallas_call(
        body, name=name, in_specs=[_ANY] * 2, out_specs=[_ANY] * 2, out_shape=_chip_shapes(True, [shard, small]),
        scratch_shapes=[pltpu.SemaphoreType.DMA((9,)), pltpu.SemaphoreType.DMA((9,)), pltpu.SemaphoreType.DMA((2,))],
    )(shard, small)


def _swap_pair(arrays, *, name):
    n = len(arrays)

    def body(*refs):
        ins, outs, (send_sems, recv_sems) = refs[:n], refs[n:2 * n], refs[2 * n:]
        x, y, c = _place()
        copies = [_remote(ins[a], outs[a], send_sems.at[a], recv_sems.at[a], (x, y, 1 - c)) for a in range(n)]
        for cp in copies:
            cp.start()
        for cp in copies:
            cp.wait_recv()
        for cp in copies:
            cp.wait_send()

    shapes = [jax.ShapeDtypeStruct(s.shape, s.dtype) for s in arrays]
    return _comm_call(body, arrays, shapes, n, 0, name)


def _gather_all(tile, *, name):
    def body(in_ref, out_ref, send_sems, recv_sems, local_sems):
        x, y, c = _place()
        mine = 4 * x + 2 * y + c
        local = pltpu.make_async_copy(in_ref, out_ref.at[mine], local_sems.at[0])
        remote, landing = [], []
        for k in range(1, N_DEV):
            peer = (_flip(x, k & 4), _flip(y, k & 2), _flip(c, k & 1))
            sems = send_sems.at[k - 1], recv_sems.at[k - 1]
            remote.append(_remote(in_ref, out_ref.at[mine], *sems, peer))
            landing.append(_remote(in_ref, out_ref.at[4 * peer[0] + 2 * peer[1] + peer[2]], *sems, peer))
        for cp in [local] + remote:
            cp.start()
        local.wait()
        for cp in landing:
            cp.wait_recv()
        for cp in remote:
            cp.wait_send()

    return _comm_call(body, [tile], [jax.ShapeDtypeStruct((N_DEV,) + tile.shape, tile.dtype)], N_DEV - 1, 1, name)[0]


def _sum_slots(slabs, *, name, tr=128):
    s, r, c = slabs.shape
    tr = min(tr, r)

    def body(in_ref, o_ref):
        acc = in_ref[0].astype(F32)
        for j in range(1, s):
            acc = acc + in_ref[j].astype(F32)
        o_ref[...] = acc

    return pl.pallas_call(
        body, name=name, grid=(r // tr,),
        in_specs=[pl.BlockSpec((s, tr, c), lambda i: (0, i, 0))], out_specs=pl.BlockSpec((tr, c), lambda i: (i, 0)),
        out_shape=jax.ShapeDtypeStruct((r, c), F32), compiler_params=_params("parallel"))(slabs)


def _adamw_math(w, g, m, v):
    m = ADAM_B1 * m + (1.0 - ADAM_B1) * g
    v = ADAM_B2 * v + (1.0 - ADAM_B2) * (g * g)
    m_hat = m / (1.0 - ADAM_B1 ** ADAM_STEP)
    v_hat = v / (1.0 - ADAM_B2 ** ADAM_STEP)
    delta = -ADAM_LR * (m_hat / (jnp.sqrt(v_hat) + ADAM_EPS) + ADAM_WD * w)
    return delta, m, v


def _adamw(w, m, v, parts, *, name, tr=128):
    r, c = w.shape
    tr = min(tr, r)
    s = len(parts)

    def body(w_ref, m_ref, v_ref, *refs):
        g_ref, d_ref, nm_ref, nv_ref = refs[s:]
        g = refs[0][...]
        for p_ref in refs[1:s]:
            g = g + p_ref[...]
        g_ref[...] = g
        d_ref[...], nm_ref[...], nv_ref[...] = _adamw_math(w_ref[...], g, m_ref[...], v_ref[...])

    blk = pl.BlockSpec((tr, c), lambda i: (i, 0))
    shape = jax.ShapeDtypeStruct((r, c), F32)
    return pl.pallas_call(
        body, name=name, grid=(r // tr,), in_specs=[blk] * (3 + s), out_specs=[blk] * 4, out_shape=[shape] * 4,
        compiler_params=_params("parallel"))(w, m, v, *parts)


_BIG = ("a_w_in", "b_w_in", "a_w_out", "b_w_out", "a_conv_w")
_SMALL = ("norm_w", "a_a_log", "a_dt_bias", "a_out_norm_w", "b_q_norm_w", "b_k_norm_w", "b_rel_bias")
_ORDER = ("norm_w", "a_w_in", "a_conv_w", "a_a_log", "a_dt_bias", "a_out_norm_w", "a_w_out", "b_w_in", "b_q_norm_w",
          "b_k_norm_w", "b_rel_bias", "b_w_out")


def _join_cols(g):
    return jnp.transpose(g, (1, 0, 2)).reshape(g.shape[1], -1)


def _split_cols(g):
    return jnp.transpose(g.reshape(g.shape[0], N_CHIPS, -1), (1, 0, 2))


def _split_rows(g):
    return g.reshape(N_CHIPS, -1, g.shape[-1])


def _pack(d):
    flat = jnp.concatenate([d[n].reshape(-1) for n in _SMALL])
    return jnp.pad(flat, (0, -flat.shape[0] % LANES)).reshape(1, -1)


def _unpack(row, like):
    out, at = {}, 0
    for n in _SMALL:
        size = like[n].size
        out[n] = row[0, at:at + size].reshape(like[n].shape)
        at += size
    return out


def kernel(x, norm_w, a_w_in, a_conv_w, a_a_log, a_dt_bias, a_out_norm_w, a_w_out, b_w_in, b_q_norm_w, b_k_norm_w, b_rel_bias, b_w_out, loss_target, m_norm_w, m_a_w_in, m_a_conv_w, m_a_a_log, m_a_dt_bias, m_a_out_norm_w, m_a_w_out, m_b_w_in, m_b_q_norm_w, m_b_k_norm_w, m_b_rel_bias, m_b_w_out, v_norm_w, v_a_w_in, v_a_conv_w, v_a_a_log, v_a_dt_bias, v_a_out_norm_w, v_a_w_out, v_b_w_in, v_b_q_norm_w, v_b_k_norm_w, v_b_rel_bias, v_b_w_out):
    w = dict(norm_w=norm_w, a_w_in=a_w_in, a_conv_w=a_conv_w, a_a_log=a_a_log, a_dt_bias=a_dt_bias,
             a_out_norm_w=a_out_norm_w, a_w_out=a_w_out, b_w_in=b_w_in, b_q_norm_w=b_q_norm_w, b_k_norm_w=b_k_norm_w,
             b_rel_bias=b_rel_bias, b_w_out=b_w_out)
    m = dict(norm_w=m_norm_w, a_w_in=m_a_w_in, a_conv_w=m_a_conv_w, a_a_log=m_a_a_log, a_dt_bias=m_a_dt_bias,
             a_out_norm_w=m_a_out_norm_w, a_w_out=m_a_w_out, b_w_in=m_b_w_in, b_q_norm_w=m_b_q_norm_w,
             b_k_norm_w=m_b_k_norm_w, b_rel_bias=m_b_rel_bias, b_w_out=m_b_w_out)
    v = dict(norm_w=v_norm_w, a_w_in=v_a_w_in, a_conv_w=v_a_conv_w, a_a_log=v_a_a_log, a_dt_bias=v_a_dt_bias,
             a_out_norm_w=v_a_out_norm_w, a_w_out=v_a_w_out, b_w_in=v_b_w_in, b_q_norm_w=v_b_q_norm_w,
             b_k_norm_w=v_b_k_norm_w, b_rel_bias=v_b_rel_bias, b_w_out=v_b_w_out)

    wa_in, conv = _gather_shared(a_w_in[0].astype(BF16), a_conv_w[0], name="gather_a_in")
    loss, dx, grads = _local_step(
        x[0], loss_target[0], norm_w, _join_cols(wa_in), _join_cols(conv), a_a_log, a_dt_bias, a_out_norm_w,
        a_w_out[0].astype(BF16), b_w_in[0].astype(BF16), b_q_norm_w, b_k_norm_w, b_rel_bias, b_w_out[0].astype(BF16),
        sharded=True)
    loss = lax.psum(loss, ("x", "y", "c"))

    late = [n for n in _BIG if not isinstance(grads[n], tuple)]
    mine = [_sum_slots(grads[n], name=f"chip_sum_{n}") for n in late]
    sums = {n: grads[n] for n in _BIG if n not in late}
    sums.update(zip(late, zip(mine, _swap_pair(mine, name="pair_grads"))))
    out = {}
    for n in _BIG:
        out[n] = [r[None] for r in _adamw(w[n][0], m[n][0], v[n][0], list(sums[n]), name=f"adamw_{n}")]

    row = _pack(grads)
    tiles = _gather_all(jnp.broadcast_to(row, (8, row.shape[1])), name="gather_small_grads")
    res = _adamw(_pack(w), _pack(m), _pack(v), [tiles[d, 0:1, :] for d in range(N_DEV)], name="adamw_small")
    unpacked = [_unpack(r, w) for r in res]
    for n in _SMALL:
        out[n] = [u[n] for u in unpacked]

    return (loss, dx[None], *[out[n][0] for n in _ORDER], *[out[n][1] for n in _ORDER], *[out[n][2] for n in _ORDER],
            *[out[n][3] for n in _ORDER])
```

```python
import functools

import numpy as np
import jax
import jax.numpy as jnp
from jax import lax
from jax.experimental import pallas as pl
from jax.experimental.pallas import tpu as pltpu

F32 = jnp.float32
BF16 = jnp.bfloat16

CHUNK = 64
HEAD_DIM = 128
LEFT_CHUNKS = 8
REL_CLIP = 256
CONV_K = 4
EPS = 1e-6
HALO = 8

ADAM_LR = 0.001
ADAM_B1 = 0.9
ADAM_B2 = 0.999
ADAM_EPS = 1e-08
ADAM_WD = 0.01
ADAM_STEP = 10

LANES = 128
N_CHIPS = 4
N_DEV = 8
VMEM_LIMIT_BYTES = 56 * 1024 * 1024
VMEM_LIMIT_WIDE_BYTES = 63 * 1024 * 1024
MESH = pl.DeviceIdType.MESH


def _params(*sem, vmem=VMEM_LIMIT_BYTES):
    return pltpu.CompilerParams(dimension_semantics=sem, vmem_limit_bytes=vmem)


def _dot(a, b, dims=(((1,), (0,)), ((), ())), precision=None):
    return lax.dot_general(a, b, dims, precision=precision, preferred_element_type=F32)


_NT = (((1,), (1,)), ((), ()))
_TN = (((0,), (0,)), ((), ()))


def _bdot(a, b, dims=(((1,), (0,)), ((), ()))):
    return _dot(a.astype(BF16), b.astype(BF16), dims)


def _fdot(a, b, dims=(((1,), (0,)), ((), ()))):
    return _dot(a, b, dims, precision=lax.Precision.HIGH)


def _silu(x):
    return x * jax.nn.sigmoid(x)


def _stacks(x):
    if not isinstance(x, (list, tuple)) and x.ndim != 3:
        return None
    arrays = list(x) if isinstance(x, (list, tuple)) else [x]
    assert len({(v.shape[1], v.shape[2], v.dtype) for v in arrays}) == 1
    starts = [sum(v.shape[0] for v in arrays[:r]) for r in range(len(arrays))]
    return arrays, starts, starts[-1] + arrays[-1].shape[0]


def _static_pick(table, index):
    out = table[-1]
    for s in range(len(table) - 2, -1, -1):
        out = jnp.where(index == s, table[s], out)
    return out


def _matmul(a, b, *, name, trans_a=False, trans_b=False, residual=None, out_dtype=F32, tm=1024, tn=1024, tk=2048,
            col_slabs=0, order=None, exchange=(), with_pair=False):
    assert not (trans_a and trans_b)
    a_stack, b_stack = _stacks(a), _stacks(b)
    assert not (a_stack and (trans_a or b_stack)) and not (b_stack and trans_b)
    a_list, b_list = (a_stack[0] if a_stack else [a]), (b_stack[0] if b_stack else [b])
    a0, b0 = a_list[0], b_list[0]
    k, m = (a_stack[2] * a0.shape[2], a0.shape[1]) if a_stack else a.shape if trans_a else a.shape[::-1]
    n = b_stack[2] * b0.shape[2] if b_stack else b.shape[0] if trans_b else b.shape[1]
    tm, tn, tk = min(tm, m), min(tn, n // max(col_slabs, 1)), min(tk, k)
    if a_stack:
        tk = min(tk, a0.shape[2])
        per_k = a0.shape[2] // tk
    if b_stack:
        tn = min(tn, b0.shape[2])
        per_n = b0.shape[2] // tn
    assert m % tm == 0 and n % tn == 0 and k % tk == 0, (a0.shape, b0.shape, tm, tn, tk)
    nk = k // tk
    dims = _NT if trans_b else _TN if trans_a else (((1,), (0,)), ((), ()))
    order = list(order) if order is not None else list(range(max(a_stack[2] if a_stack else 0, b_stack[2] if b_stack else 0)))
    na, nb = len(a_list), len(b_list)

    def group_of(r, stack, position):
        arrays, starts, _ = stack
        local = position - starts[r]
        return jnp.logical_and(local >= 0, local < arrays[r].shape[0]), jnp.clip(local, 0, arrays[r].shape[0] - 1)

    def body(*refs):
        a_refs, b_refs = refs[:na], refs[na:na + nb]
        r_ref = refs[na + nb] if residual is not None else None
        o_ref, acc_ref = refs[-2:]
        j, kk = pl.program_id(1), pl.program_id(2)

        @pl.when(kk == 0)
        def _():
            acc_ref[...] = jnp.zeros_like(acc_ref)

        for ra, a_ref in enumerate(a_refs):
            for rb, b_ref in enumerate(b_refs):
                def add(a_ref=a_ref, b_ref=b_ref):
                    acc_ref[...] += _dot(a_ref[...], b_ref[...], dims)

                if na > 1:
                    pl.when(group_of(ra, a_stack, kk // per_k)[0])(add)
                elif nb > 1:
                    pl.when(group_of(rb, b_stack, j // per_n)[0])(add)
                else:
                    add()

        @pl.when(kk == nk - 1)
        def _():
            r = acc_ref[...]
            if r_ref is not None:
                r = r + r_ref[...]
            o_ref[...] = r.astype(o_ref.dtype)

    if a_stack:
        a_specs = [pl.BlockSpec((None, tm, tk), lambda i, j, kk, r=r: (group_of(r, a_stack, kk // per_k)[1], i, kk % per_k))
                   for r in range(na)]
        b_k = lambda kk: _static_pick(order, kk // per_k) * per_k + kk % per_k
    else:
        a_specs = [pl.BlockSpec((tk, tm), lambda i, j, kk: (kk, i)) if trans_a else pl.BlockSpec((tm, tk), lambda i, j, kk: (i, kk))]
        b_k = lambda kk: kk
    if b_stack:
        b_specs = [pl.BlockSpec((None, tk, tn), lambda i, j, kk, r=r: (group_of(r, b_stack, j // per_n)[1], kk, j % per_n))
                   for r in range(nb)]
        out_col = lambda j: _static_pick(order, j // per_n) * per_n + j % per_n
    else:
        b_specs = [pl.BlockSpec((tn, tk), lambda i, j, kk: (j, b_k(kk))) if trans_b
                   else pl.BlockSpec((tk, tn), lambda i, j, kk: (b_k(kk), j))]
        out_col = lambda j: j
    in_specs = a_specs + b_specs
    args = a_list + b_list
    if residual is not None:
        in_specs.append(pl.BlockSpec((tm, tn), lambda i, j, kk: (i, j)))
        args.append(residual)
    grid = (m // tm, n // tn, nk)
    n_x = len(exchange)
    if col_slabs:
        per = n // col_slabs // tn
        assert per * tn * col_slabs == n, (n, tn, col_slabs)
        out_spec = pl.BlockSpec((None, tm, tn), lambda i, j, kk: (out_col(j) // per, i, out_col(j) % per))
        out_shape = jax.ShapeDtypeStruct((col_slabs, m, n // col_slabs), out_dtype)
    else:
        out_spec = pl.BlockSpec((tm, tn), lambda i, j, kk: (i, out_col(j)))
        out_shape = jax.ShapeDtypeStruct((m, n), out_dtype)
    out, *landed = pl.pallas_call(
        _with_exchange(body, len(args), 1, "pair" if with_pair else False, n_x, grid),
        name=name,
        grid=grid,
        in_specs=in_specs + [_ANY] * n_x,
        out_specs=[out_spec] + [_ANY] * n_x,
        out_shape=[out_shape] + _chip_shapes(False, exchange),
        scratch_shapes=[pltpu.VMEM((tm, tn), F32)] + ((_pair_scratch if with_pair else _chip_scratch)(n_x) if n_x else []),
        compiler_params=_params(*(("arbitrary",) * 3 if n_x else ("parallel", "parallel", "arbitrary"))),
    )(*args, *exchange)
    return (out, landed) if n_x else out


def _rms(x, w):
    return x * lax.rsqrt(jnp.mean(x * x, axis=-1, keepdims=True) + EPS) * w


def _rmsnorm_fwd(x, w_row, narrow_w, *, name, tr=512):
    t, d = x.shape
    tr = min(tr, t)

    def body(x_ref, w_ref, nw_ref, o_ref, narrow_ref):
        hn = _rms(x_ref[...], w_ref[...]).astype(BF16)
        o_ref[...] = hn
        narrow_ref[...] = _dot(hn, nw_ref[...])

    return pl.pallas_call(
        body,
        name=name,
        grid=(t // tr,),
        in_specs=[pl.BlockSpec((tr, d), lambda i: (i, 0)), pl.BlockSpec((1, d), lambda i: (0, 0)),
                  pl.BlockSpec((d, LANES), lambda i: (0, 0))],
        out_specs=[pl.BlockSpec((tr, d), lambda i: (i, 0)), pl.BlockSpec((tr, LANES), lambda i: (i, 0))],
        out_shape=[jax.ShapeDtypeStruct((t, d), BF16), jax.ShapeDtypeStruct((t, LANES), F32)],
        compiler_params=_params("parallel"),
    )(x, w_row, narrow_w)


def _matmul_norm(a, b, residual, w_row, *, name, tm=512):
    t, k = a.shape
    d = b.shape[1]
    tm = min(tm, t)

    def body(a_ref, b_ref, r_ref, w_ref, h_ref, hn_ref):
        h = _dot(a_ref[...], b_ref[...]) + r_ref[...]
        h_ref[...] = h
        hn_ref[...] = _rms(h, w_ref[...]).astype(BF16)

    row = pl.BlockSpec((tm, d), lambda i: (i, 0))
    return pl.pallas_call(
        body,
        name=name,
        grid=(t // tm,),
        in_specs=[pl.BlockSpec((tm, k), lambda i: (i, 0)), pl.BlockSpec((k, d), lambda i: (0, 0)), row,
                  pl.BlockSpec((1, d), lambda i: (0, 0))],
        out_specs=[row, row],
        out_shape=[jax.ShapeDtypeStruct((t, d), F32), jax.ShapeDtypeStruct((t, d), BF16)],
        compiler_params=_params("parallel"),
    )(a, b, residual, w_row)


def _rmsnorm_bwd(x, w_row, dy, dres, *, name, tr=256, narrow=None):
    t, d = x.shape
    tr = min(tr, t)
    extra = list(narrow) if narrow is not None else []

    def body(x_ref, w_ref, dy_ref, dres_ref, *refs):
        dx_ref, dxb_ref, dw_ref = refs[len(extra):]

        @pl.when(pl.program_id(0) == 0)
        def _():
            dw_ref[...] = jnp.zeros_like(dw_ref)

        dy = dy_ref[...]
        if extra:
            dy = dy + _dot(refs[0][...], refs[1][...], _NT)
        _, vjp = jax.vjp(_rms, x_ref[...], w_ref[...])
        dx, dw = vjp(dy)
        dx = dx + dres_ref[...]
        dx_ref[...] = dx
        dxb_ref[...] = dx.astype(BF16)
        dw_ref[...] += dw

    row = pl.BlockSpec((tr, d), lambda i: (i, 0))
    vec = pl.BlockSpec((1, d), lambda i: (0, 0))
    extra_specs = [pl.BlockSpec((tr, LANES), lambda i: (i, 0)), pl.BlockSpec((d, LANES), lambda i: (0, 0))] if extra else []
    return pl.pallas_call(
        body,
        name=name,
        grid=(t // tr,),
        in_specs=[row, vec, row, row] + extra_specs,
        out_specs=[row, row, vec],
        out_shape=[jax.ShapeDtypeStruct((t, d), F32), jax.ShapeDtypeStruct((t, d), BF16), jax.ShapeDtypeStruct((1, d), F32)],
        compiler_params=_params("arbitrary"),
    )(x, w_row, dy, dres, *extra)


def _matmul_loss(a, b, residual, target, *, name, tm=512, tn=1024):
    t, k = a.shape
    d = b.shape[1]
    tm, tn = min(tm, t), min(tn, d)

    def body(a_ref, b_ref, r_ref, t_ref, dh_ref, dhb_ref, part_ref):
        @pl.when(pl.program_id(1) == 0)
        def _():
            part_ref[...] = jnp.zeros_like(part_ref)

        err = _dot(a_ref[...], b_ref[...]) + r_ref[...] - t_ref[...]
        dh = err * (1.0 / d)
        dh_ref[...] = dh
        dhb_ref[...] = dh.astype(BF16)
        part_ref[...] += jnp.sum(err * err, axis=0, keepdims=True)

    tile = pl.BlockSpec((tm, tn), lambda j, i: (i, j))
    dh, dhb, part = pl.pallas_call(
        body,
        name=name,
        grid=(d // tn, t // tm),
        in_specs=[pl.BlockSpec((tm, k), lambda j, i: (i, 0)), pl.BlockSpec((k, tn), lambda j, i: (0, j)), tile, tile],
        out_specs=[tile, tile, pl.BlockSpec((1, tn), lambda j, i: (0, j))],
        out_shape=[jax.ShapeDtypeStruct((t, d), F32), jax.ShapeDtypeStruct((t, d), BF16), jax.ShapeDtypeStruct((1, d), F32)],
        compiler_params=_params("arbitrary", "arbitrary"),
    )(a, b, residual, target)
    return 0.5 / d * jnp.sum(part), dh, dhb


_BNN = (((2,), (1,)), ((0,), (0,)))
_BNT = (((2,), (2,)), ((0,), (0,)))
_BTN = (((1,), (1,)), ((0,), (0,)))


_TAP0 = HALO - (CONV_K - 1)


def _conv(x_ref, w, rows):
    c = w[0:1, :] * x_ref[_TAP0:_TAP0 + rows, :]
    for j in range(1, CONV_K):
        c = c + w[j:j + 1, :] * x_ref[_TAP0 + j:_TAP0 + j + rows, :]
    return c


def _conv_silu_bwd(x_ref, w, dact, dc_ref, rows):
    c = _conv(x_ref, w, rows)
    sig = jax.nn.sigmoid(c)
    dc = dact * (sig * (1.0 + c * (1.0 - sig)))
    dw = [jnp.sum(dc * x_ref[_TAP0 + j:_TAP0 + j + rows, :], axis=0, keepdims=True) for j in range(CONV_K)]
    dc_ref[0:HALO, :] = jnp.zeros((HALO, HEAD_DIM), F32)
    dc_ref[HALO:HALO + rows, :] = dc
    dc_ref[HALO + rows:HALO + rows + HALO, :] = jnp.zeros((HALO, HEAD_DIM), F32)
    first = HALO - _TAP0
    dx = w[0:1, :] * dc_ref[first:first + HALO + rows, :]
    for j in range(1, CONV_K):
        dx = dx + w[j:j + 1, :] * dc_ref[first - j:first - j + HALO + rows, :]
    return dx, dw


@jax.custom_vjp
def _unit_lower_inverse(neg_l):
    n = neg_l.shape[0]
    eye = (lax.broadcasted_iota(jnp.int32, (n, CHUNK, CHUNK), 1) == lax.broadcasted_iota(jnp.int32, (n, CHUNK, CHUNK), 2))
    inv = eye.astype(F32) + neg_l
    power = _bdot(neg_l, neg_l, _BNN)
    for _ in range(4):
        both = _bdot(jnp.concatenate([inv, power], axis=1), power, _BNN)
        inv, power = inv + both[:, :CHUNK], both[:, CHUNK:]
    return inv + _bdot(inv, power, _BNN)


def _unit_lower_inverse_fwd(neg_l):
    inv = _unit_lower_inverse(neg_l)
    return inv, inv


def _unit_lower_inverse_bwd(inv, dinv):
    return (_fdot(_fdot(inv, dinv, _BTN), inv, _BNT),)


_unit_lower_inverse.defvjp(_unit_lower_inverse_fwd, _unit_lower_inverse_bwd)


def _gdn_intra(qt, kt, v, a, b, alog, dtb):
    n = a.shape[0] // CHUNK
    q = qt * lax.rsqrt(jnp.sum(qt * qt, axis=-1, keepdims=True) + EPS) * (HEAD_DIM ** -0.5)
    k = kt * lax.rsqrt(jnp.sum(kt * kt, axis=-1, keepdims=True) + EPS)
    lanes = jnp.ones((1, HEAD_DIM), F32)
    beta = jax.nn.sigmoid(b) * lanes
    sp = a + dtb
    g = (-jnp.exp(alog) * (jnp.maximum(sp, 0.0) + jnp.log(1.0 + jnp.exp(-jnp.abs(sp))))) * lanes
    q, k, v, beta, g = (t.reshape(n, CHUNK, HEAD_DIM) for t in (q, k, v, beta, g))

    row = lax.broadcasted_iota(jnp.int32, (n, CHUNK, CHUNK), 1)
    col = lax.broadcasted_iota(jnp.int32, (n, CHUNK, CHUNK), 2)
    tri_incl = row >= col
    tri_strict = row > col
    gc = _fdot(tri_incl.astype(F32), g, _BNN)
    gc_row = _fdot(g[:, :, :CHUNK], (row <= col).astype(F32), _BTN)
    decay = jnp.exp(jnp.where(tri_incl, gc[:, :, :CHUNK] - gc_row, -1e30))
    kb = k * beta
    vb = v * beta
    with_k = _bdot(jnp.concatenate([kb, q], axis=1), k, _BNT)
    neg_l = jnp.where(tri_strict, -(with_k[:, :CHUNK] * decay), 0.0)
    qk = jnp.where(tri_incl, with_k[:, CHUNK:] * decay, 0.0)
    inv = _unit_lower_inverse(neg_l)
    e = jnp.exp(gc)
    solved = _bdot(inv, jnp.concatenate([kb * e, vb], axis=2), _BNN)
    g_last = gc[:, CHUNK - 1:CHUNK, :]
    k_dec = k * jnp.exp(g_last - gc)
    from_k = _bdot(k_dec, solved, _BTN)
    from_qk = _bdot(qk, solved, _BNN)
    step, add = -from_k[:, :, :HEAD_DIM], from_k[:, :, HEAD_DIM:]
    read, out = q * e - from_qk[:, :, :HEAD_DIM], from_qk[:, :, HEAD_DIM:]
    return step, add, jnp.exp(g_last), read, out


def _gdn_scan_step(state, step, add, decay_last):
    return state * decay_last + _bdot(step, state) + add


def _gdn_outputs(states, read, out, z, onw):
    return _rms(_bdot(read, states, _BNN) + out, onw) * _silu(z)


def _scan_scratch(n, dtype):
    return [pltpu.VMEM((n, HEAD_DIM, HEAD_DIM), dtype), pltpu.VMEM((n, HEAD_DIM, HEAD_DIM), F32), pltpu.VMEM((n, 1, HEAD_DIM), F32)]


def _head_lane(h, offset=0):
    return lax.broadcasted_iota(jnp.int32, (1, LANES), 1) == h + offset


def _pick(mask, x):
    return jnp.sum(jnp.where(mask, x, 0.0), axis=1, keepdims=True)


def _gdn_specs(heads, tb, rev, nb, PAIR):
    assert heads % PAIR == 0
    blk = (lambda i: nb - 1 - i) if rev else (lambda i: i)
    hb = tb // HALO
    width, pairs = PAIR * HEAD_DIM, heads // PAIR

    def col(group):
        return pl.BlockSpec((tb, width), lambda i, h: (blk(i), group * pairs + h))

    def halo(group):
        return pl.BlockSpec((HALO, width), lambda i, h: (jnp.maximum(blk(i) * hb - 1, 0), group * pairs + h))

    def convw(group):
        return pl.BlockSpec((CONV_K, width), lambda i, h: (0, group * pairs + h))

    vec = pl.BlockSpec((1, LANES), lambda i, h: (0, 0))
    ab = pl.BlockSpec((tb, LANES), lambda i, h: (blk(i), 0))
    states = pl.BlockSpec((PAIR, tb // CHUNK, HEAD_DIM, HEAD_DIM), lambda i, h: (h, blk(i), 0, 0))
    return blk, col, halo, convw, vec, ab, states


def _head_cols(p):
    return slice(p * HEAD_DIM, (p + 1) * HEAD_DIM)


def _gdn_fwd(proj, ab, conv_w, alog_row, dtb_row, onw_row, *, heads, name, tb=1024, pair=4, gather=()):
    t = proj.shape[0]
    tb = min(tb, t)
    nb, cpb = t // tb, tb // CHUNK
    PAIR = min(pair, heads)
    _, col, halo, convw, vec, abspec, states = _gdn_specs(heads, tb, False, nb, PAIR)

    def body(q_ref, k_ref, v_ref, qh_ref, kh_ref, vh_ref, z_ref, ab_ref, wq_ref, wk_ref, wv_ref, alog_ref, dtb_ref, onw_ref,
             og_ref, st_ref, state_scr, x_scr, *op_scr):
        i, pair = pl.program_id(0), pl.program_id(1)
        abv = ab_ref[...]
        heads_here, later = [pair * PAIR + p for p in range(PAIR)], []
        for p, h in enumerate(heads_here):
            cols = _head_cols(p)
            for n, (ref, href) in enumerate(((q_ref, qh_ref), (k_ref, kh_ref), (v_ref, vh_ref))):
                x_scr[p, n, 0:HALO, :] = jnp.where(i > 0, href[:, cols], 0.0)
                x_scr[p, n, HALO:HALO + tb, :] = ref[:, cols]
            sel_a, sel_b = _head_lane(h), _head_lane(h, heads)
            alog, dtb = _pick(sel_a, alog_ref[...]), _pick(sel_a, dtb_ref[...])
            acts = [_silu(_conv(x_scr.at[p, n], w_ref[:, cols], tb)) for n, w_ref in enumerate((wq_ref, wk_ref, wv_ref))]
            *scan, read, out = _gdn_intra(*acts, _pick(sel_a, abv), _pick(sel_b, abv), alog, dtb)
            for scr, val in zip(op_scr[3 * p:3 * p + 3], scan):
                scr[...] = val.astype(scr.dtype)
            later.append((read, out))

        def chunk(c, states):
            for p in range(PAIR):
                st_ref[p, c] = states[p]
            return tuple(_gdn_scan_step(states[p], *[scr[c] for scr in op_scr[3 * p:3 * p + 3]]) for p in range(PAIR))

        @pl.when(i == 0)
        def _():
            for h in heads_here:
                state_scr[h] = jnp.zeros((HEAD_DIM, HEAD_DIM), F32)

        last = lax.fori_loop(0, cpb, chunk, tuple(state_scr[h] for h in heads_here))
        for p, h in enumerate(heads_here):
            cols = _head_cols(p)
            state_scr[h] = last[p]
            og = _gdn_outputs(st_ref[p], *later[p], z_ref[:, cols].reshape(cpb, CHUNK, HEAD_DIM), onw_ref[...])
            og_ref[:, cols] = og.reshape(tb, HEAD_DIM).astype(BF16)

    n_x = len(gather)
    grid = (nb, heads // PAIR)
    og, st, *gathered = pl.pallas_call(
        _with_exchange(body, 14, 2, True, n_x, grid),
        name=name,
        grid=grid,
        in_specs=[col(0), col(1), col(2), halo(0), halo(1), halo(2), col(3), abspec, convw(0), convw(1), convw(2), vec, vec, vec]
        + [_ANY] * n_x,
        out_specs=[pl.BlockSpec((tb, PAIR * HEAD_DIM), lambda i, h: (i, h)), states] + [_ANY] * n_x,
        out_shape=[jax.ShapeDtypeStruct((t, heads * HEAD_DIM), BF16),
                   jax.ShapeDtypeStruct((heads, t // CHUNK, HEAD_DIM, HEAD_DIM), F32)] + _chip_shapes(True, gather),
        scratch_shapes=[pltpu.VMEM((heads, HEAD_DIM, HEAD_DIM), F32), pltpu.VMEM((PAIR, 3, HALO + tb, HEAD_DIM), F32)]
        + _scan_scratch(cpb, BF16) * PAIR + (_chip_scratch(n_x) if n_x else []),
        compiler_params=_params("arbitrary", "arbitrary"),
    )(proj, proj, proj, proj, proj, proj, proj, ab, conv_w, conv_w, conv_w, alog_row, dtb_row, onw_row, *gather)
    return og, st, gathered


def _gdn_bwd(proj, ab, conv_w, alog_row, dtb_row, onw_row, states, dog, *, heads, name, tb=1024, pair=2, exchange=()):
    t = proj.shape[0]
    tb = min(tb, t)
    nb, cpb = t // tb, tb // CHUNK
    PAIR = min(pair, heads)
    _, col, halo, convw, vec, abspec, states_spec = _gdn_specs(heads, tb, True, nb, PAIR)
    n_conv = conv_w.shape[1]

    def body(q_ref, k_ref, v_ref, qh_ref, kh_ref, vh_ref, z_ref, ab_ref, wq_ref, wk_ref, wv_ref, alog_ref, dtb_ref, onw_ref,
             st_ref, dog_ref, dproj_ref, dab_ref, dconv_ref, dalog_ref, ddtb_ref, donw_ref,
             dstate_scr, x_scr, carry_scr, *scr):
        op_scr, dop_scr, dstates_scr, dc_scr = scr[:3 * PAIR], scr[3 * PAIR:6 * PAIR], scr[6 * PAIR:7 * PAIR], scr[7 * PAIR]
        i, pair = pl.program_id(0), pl.program_id(1)
        first_block = i == nb - 1
        heads_here, later = [pair * PAIR + p for p in range(PAIR)], []

        @pl.when(jnp.logical_and(i == 0, pair == 0))
        def _():
            dconv_ref[...] = jnp.zeros_like(dconv_ref)
            dalog_ref[...] = jnp.zeros_like(dalog_ref)
            ddtb_ref[...] = jnp.zeros_like(ddtb_ref)
            donw_ref[...] = jnp.zeros_like(donw_ref)

        @pl.when(pair == 0)
        def _():
            dab_ref[...] = jnp.zeros_like(dab_ref)

        @pl.when(i == 0)
        def _():
            for h in heads_here:
                dstate_scr[h] = jnp.zeros((HEAD_DIM, HEAD_DIM), F32)
                carry_scr[h] = jnp.zeros((3, HALO, HEAD_DIM), F32)

        abv = ab_ref[...]
        w_refs = (wq_ref, wk_ref, wv_ref)
        for p, h in enumerate(heads_here):
            cols = _head_cols(p)
            for n, (ref, href) in enumerate(((q_ref, qh_ref), (k_ref, kh_ref), (v_ref, vh_ref))):
                x_scr[p, n, 0:HALO, :] = jnp.where(first_block, 0.0, href[:, cols])
                x_scr[p, n, HALO:HALO + tb, :] = ref[:, cols]
            sel_a, sel_b = _head_lane(h), _head_lane(h, heads)
            alog, dtb = _pick(sel_a, alog_ref[...]), _pick(sel_a, dtb_ref[...])
            acts = [_silu(_conv(x_scr.at[p, n], w_ref[:, cols], tb)) for n, w_ref in enumerate(w_refs)]
            (*scan, read, out), vjp_intra = jax.vjp(_gdn_intra, *acts, _pick(sel_a, abv), _pick(sel_b, abv), alog, dtb)
            for s, val in zip(op_scr[3 * p:3 * p + 3], scan):
                s[...] = val.astype(s.dtype)
            blocked = lambda ref: ref[:, cols].reshape(cpb, CHUNK, HEAD_DIM)
            _, vjp_outputs = jax.vjp(_gdn_outputs, st_ref[p], read, out, blocked(z_ref), onw_ref[...])
            dstates_scr[p][...], dread, dout, dz, donw = vjp_outputs(blocked(dog_ref))
            dproj_ref[3, :, cols] = dz.reshape(tb, HEAD_DIM).astype(BF16)
            donw_ref[...] += donw
            later.append((vjp_intra, dread, dout, sel_a, sel_b))

        def chunk(i_rev, dstates):
            c = cpb - 1 - i_rev
            new = []
            for p in range(PAIR):
                _, vjp = jax.vjp(_gdn_scan_step, st_ref[p, c], *[s[c].astype(F32) for s in op_scr[3 * p:3 * p + 3]])
                dstate, *grads = vjp(dstates[p])
                for s, val in zip(dop_scr[3 * p:3 * p + 3], grads):
                    s[c] = val
                new.append(dstate + dstates_scr[p][c])
            return tuple(new)

        last = lax.fori_loop(0, cpb, chunk, tuple(dstate_scr[h] for h in heads_here))
        for p, h in enumerate(heads_here):
            cols = _head_cols(p)
            vjp_intra, dread, dout, sel_a, sel_b = later[p]
            dstate_scr[h] = last[p]
            *dacts, da, db, dalog, ddtb = vjp_intra((*[s[...] for s in dop_scr[3 * p:3 * p + 3]], dread, dout))
            dab_ref[...] += jnp.where(sel_a, da, 0.0) + jnp.where(sel_b, db, 0.0)
            for n, (dact, w_ref) in enumerate(zip(dacts, w_refs)):
                dx, dw = _conv_silu_bwd(x_scr.at[p, n], w_ref[:, cols], dact, dc_scr, tb)
                x_scr[p, n] = dx
                x_scr[p, n, tb:tb + HALO, :] += carry_scr[h, n]
                carry_scr[h, n] = x_scr[p, n, 0:HALO, :]
                dproj_ref[n, :, cols] = x_scr[p, n, HALO:HALO + tb, :].astype(BF16)
                lanes = pl.ds(pl.multiple_of((n * heads + h) * HEAD_DIM, HEAD_DIM), HEAD_DIM)
                for j in range(CONV_K):
                    dconv_ref[j:j + 1, lanes] += dw[j]
            dalog_ref[...] += jnp.where(sel_a, dalog, 0.0)
            ddtb_ref[...] += jnp.where(sel_a, ddtb, 0.0)

    dog_spec = pl.BlockSpec((tb, PAIR * HEAD_DIM), lambda i, h: (nb - 1 - i, h))
    dproj_spec = pl.BlockSpec((4, tb, PAIR * HEAD_DIM), lambda i, h: (0, nb - 1 - i, h))
    row_shape = jax.ShapeDtypeStruct((1, LANES), F32)
    n_x = len(exchange)
    grid = (nb, heads // PAIR)
    outs = pl.pallas_call(
        _with_exchange(body, 16, 6, False, n_x, grid),
        name=name,
        grid=grid,
        in_specs=[col(0), col(1), col(2), halo(0), halo(1), halo(2), col(3), abspec, convw(0), convw(1), convw(2), vec, vec, vec,
                  states_spec, dog_spec] + [_ANY] * n_x,
        out_specs=[dproj_spec, abspec, pl.BlockSpec((CONV_K, n_conv), lambda i, h: (0, 0)), vec, vec, vec] + [_ANY] * n_x,
        out_shape=[jax.ShapeDtypeStruct((4, t, heads * HEAD_DIM), BF16), jax.ShapeDtypeStruct((t, LANES), F32),
                   jax.ShapeDtypeStruct((CONV_K, n_conv), F32), row_shape, row_shape, row_shape] + _chip_shapes(False, exchange),
        scratch_shapes=[pltpu.VMEM((heads, HEAD_DIM, HEAD_DIM), F32), pltpu.VMEM((PAIR, 3, HALO + tb, HEAD_DIM), F32),
                        pltpu.VMEM((heads, 3, HALO, HEAD_DIM), F32)] + _scan_scratch(cpb, BF16) * PAIR
        + _scan_scratch(cpb, F32) * PAIR + [pltpu.VMEM((cpb, HEAD_DIM, HEAD_DIM), F32)] * PAIR
        + [pltpu.VMEM((HALO + tb + HALO, HEAD_DIM), F32)]
        + (_chip_scratch(n_x) if n_x else []),
        compiler_params=_params("arbitrary", "arbitrary", vmem=VMEM_LIMIT_WIDE_BYTES),
    )(proj, proj, proj, proj, proj, proj, proj, ab, conv_w, conv_w, conv_w, alog_row, dtb_row, onw_row, states, dog, *exchange)
    return (*outs[:6], outs[6:])


BAND = (LEFT_CHUNKS + 1) * CHUNK
PAD = LEFT_CHUNKS * CHUNK
GROUP = 2
ROWS = GROUP * CHUNK
WIN = (LEFT_CHUNKS + GROUP) * CHUNK
DIAGS = WIN + ROWS - 1
NEAR = PAD + ROWS - 1 - REL_CLIP
assert 0 < NEAR < DIAGS and WIN - PAD - 1 <= REL_CLIP and WIN % LANES == 0
ATTN_BLOCK = 1024
N_EDGE = PAD // ROWS


def _band_bias(rel_bias):
    heads = rel_bias.shape[0]
    far = jnp.broadcast_to(rel_bias[:, 2 * REL_CLIP:], (heads, NEAR + 1))
    near = rel_bias[:, 2 * REL_CLIP + NEAR + 1 - DIAGS:2 * REL_CLIP][:, ::-1]
    diag = jnp.concatenate([far, near], axis=1)
    return jnp.stack([diag[:, ROWS - 1 - r:ROWS - 1 - r + WIN] for r in range(ROWS)], axis=1)


def _band_bias_grad(dbias):
    heads = dbias.shape[0]
    diag = sum(jnp.pad(dbias[:, r, :], ((0, 0), (ROWS - 1 - r, r))) for r in range(ROWS))
    far = jnp.sum(diag[:, :NEAR + 1], axis=1, keepdims=True)
    near = diag[:, NEAR + 1:][:, ::-1]
    unused = jnp.zeros((heads, 2 * REL_CLIP - near.shape[1]), F32)
    return jnp.concatenate([unused, near, far], axis=1)


def _masked_bias(bias, n):
    r = np.arange(ROWS)[:, None]
    key = np.arange(WIN)[None, :]
    band_start = (r // CHUNK) * CHUNK
    in_band = np.logical_and(key >= band_start, key < band_start + BAND)
    in_sequence = key[None] >= PAD - np.arange(n)[:, None, None] * ROWS
    first = jnp.where(np.logical_and(in_band[None], in_sequence)[None], bias[:, None], -1e30)
    return first, jnp.where(in_band[None, None], bias[:, None], -1e30)


def _attn_groups(q_pre, z, kn, v, bias, qnw):
    q = _rms(q_pre, qnw)
    s = _bdot(q, kn, _BNT) * (HEAD_DIM ** -0.5) + bias
    p = jnp.exp(s - jnp.max(s, axis=-1, keepdims=True))
    p = p / jnp.sum(p, axis=-1, keepdims=True)
    return _bdot(p, v, _BNN) * _silu(z)


def _attn_groups_bwd(q_pre, z, kn, v, bias, qnw, dog):
    scale = HEAD_DIM ** -0.5
    inv_rms = lax.rsqrt(jnp.mean(q_pre * q_pre, axis=-1, keepdims=True) + EPS)
    q_hat = q_pre * inv_rms
    q_b = (q_hat * qnw).astype(BF16)
    s = _dot(q_b, kn, _BNT) * scale + bias
    e = jnp.exp(s - jnp.max(s, axis=-1, keepdims=True))
    p = e * (1.0 / jnp.sum(e, axis=-1, keepdims=True))
    p_b = p.astype(BF16)
    o = _dot(p_b, v, _BNN)
    sig = jax.nn.sigmoid(z)
    do = dog * (z * sig)
    dz = dog * o * (sig * (1.0 + z * (1.0 - sig)))
    do_b = do.astype(BF16)
    dv = _dot(p_b, do_b, _BTN)
    dp = _dot(do_b, v, _BNT)
    ds = p * (dp - jnp.sum(do * o, axis=-1, keepdims=True))
    ds_b = (ds * scale).astype(BF16)
    dq = _dot(ds_b, kn, _BNN)
    dkn = _dot(ds_b, q_b, _BTN)
    dqnw = jnp.sum(jnp.sum(dq * q_hat, axis=0), axis=0, keepdims=True)
    dq_hat = dq * qnw
    dq_pre = inv_rms * (dq_hat - q_hat * jnp.mean(dq_hat * q_hat, axis=-1, keepdims=True))
    return dq_pre, dz, dkn, dv, jnp.sum(ds, axis=0), dqnw


def _attn_specs(heads, tb, t):
    def col(group):
        return pl.BlockSpec((tb, HEAD_DIM), lambda h, i: (i, group * heads + h))

    def full(group):
        return pl.BlockSpec((t, HEAD_DIM), lambda h, i: (0, group * heads + h))

    bias = [pl.BlockSpec((1, min(tb // ROWS, N_EDGE), ROWS, WIN), lambda h, i: (h, 0, 0, 0)),
            pl.BlockSpec((1, 1, ROWS, WIN), lambda h, i: (h, 0, 0, 0))]
    vec = pl.BlockSpec((1, HEAD_DIM), lambda h, i: (0, 0))
    return col, full, bias, vec


def _attn_windows(scr, block_start, n):
    return jnp.stack([scr[pl.ds(pl.multiple_of(block_start + g * ROWS, ROWS), WIN), :] for g in range(n)])


def _attn_fill(k_ref, v_ref, knw_ref, kn_scr, v_scr, t):
    kn_scr[0:PAD, :] = jnp.zeros((PAD, HEAD_DIM), BF16)
    v_scr[0:PAD, :] = jnp.zeros((PAD, HEAD_DIM), BF16)
    step = min(512, t)

    def fill(j, _):
        rows = pl.ds(pl.multiple_of(j * step, step), step)
        prows = pl.ds(pl.multiple_of(PAD + j * step, CHUNK), step)
        kn_scr[prows, :] = _rms(k_ref[rows, :], knw_ref[...]).astype(BF16)
        v_scr[prows, :] = v_ref[rows, :].astype(BF16)
        return 0

    lax.fori_loop(0, t // step, fill, 0)


def _attn_fwd(proj, bias, qnw_row, knw_row, *, heads, name, tb=2 * ATTN_BLOCK):
    t = proj.shape[0]
    tb = min(tb, t)
    nb, ng = t // tb, tb // ROWS
    col, full, bias_spec, vec = _attn_specs(heads, tb, t)

    def body(q_ref, k_ref, v_ref, z_ref, first_ref, rest_ref, qnw_ref, knw_ref, og_ref, kn_scr, v_scr):
        i = pl.program_id(1)

        @pl.when(i == 0)
        def _():
            _attn_fill(k_ref, v_ref, knw_ref, kn_scr, v_scr, t)

        def run(block_bias):
            start = i * tb
            og = _attn_groups(q_ref[...].reshape(ng, ROWS, HEAD_DIM), z_ref[...].reshape(ng, ROWS, HEAD_DIM),
                              _attn_windows(kn_scr, start, ng), _attn_windows(v_scr, start, ng), block_bias, qnw_ref[...])
            og_ref[...] = og.reshape(tb, HEAD_DIM).astype(BF16)

        def first_bias():
            edge = first_ref[0]
            more = ng - edge.shape[0]
            return edge if more == 0 else jnp.concatenate([edge, jnp.broadcast_to(rest_ref[0], (more, ROWS, WIN))])

        pl.when(i == 0)(lambda: run(first_bias()))
        pl.when(i > 0)(lambda: run(rest_ref[0]))

    return pl.pallas_call(
        body,
        name=name,
        grid=(heads, nb),
        in_specs=[col(0), full(1), full(2), col(3), *bias_spec, vec, vec],
        out_specs=pl.BlockSpec((tb, HEAD_DIM), lambda h, i: (i, h)),
        out_shape=jax.ShapeDtypeStruct((t, heads * HEAD_DIM), BF16),
        scratch_shapes=[pltpu.VMEM((PAD + t, HEAD_DIM), BF16), pltpu.VMEM((PAD + t, HEAD_DIM), BF16)],
        compiler_params=_params("arbitrary", "arbitrary"),
    )(proj, proj, proj, proj, *bias, qnw_row, knw_row)


def _attn_bwd(proj, bias, qnw_row, knw_row, dog, *, heads, name, tb=ATTN_BLOCK, sub=4):
    t = proj.shape[0]
    tb = min(tb, t)
    nb, ng = t // tb, tb // ROWS
    sub = min(sub, ng)
    n_edge = min(ng, N_EDGE)
    assert n_edge % sub == 0
    col, full, bias_spec, vec = _attn_specs(heads, tb, t)

    def body(q_ref, k_ref, v_ref, z_ref, first_ref, rest_ref, qnw_ref, knw_ref, dog_ref,
             dqz_ref, dkv_ref, dbias_ref, dqnw_ref, dknw_ref, kn_scr, v_scr, dkn_scr, dv_scr):
        i = pl.program_id(1)

        @pl.when(i == 0)
        def _():
            _attn_fill(k_ref, v_ref, knw_ref, kn_scr, v_scr, t)
            dkn_scr[...] = jnp.zeros_like(dkn_scr)
            dv_scr[...] = jnp.zeros_like(dv_scr)
            dbias_ref[...] = jnp.zeros_like(dbias_ref)
            dqnw_ref[...] = jnp.zeros_like(dqnw_ref)

        def run(block_bias):
            for g0 in range(0, ng, sub):
                rows = pl.ds(g0 * ROWS, sub * ROWS)
                at = i * tb + g0 * ROWS
                blocked = lambda ref: ref[rows, :].reshape(sub, ROWS, HEAD_DIM)
                dq, dz, dkn, dv, dbias, dqnw = _attn_groups_bwd(
                    blocked(q_ref), blocked(z_ref), _attn_windows(kn_scr, at, sub), _attn_windows(v_scr, at, sub),
                    block_bias(g0), qnw_ref[...], blocked(dog_ref))
                dqz_ref[0, rows, :] = dq.reshape(sub * ROWS, HEAD_DIM).astype(BF16)
                dqz_ref[1, rows, :] = dz.reshape(sub * ROWS, HEAD_DIM).astype(BF16)
                for g in range(sub):
                    window = pl.ds(pl.multiple_of(at + g * ROWS, ROWS), WIN)
                    dkn_scr[window, :] += dkn[g]
                    dv_scr[window, :] += dv[g]
                dbias_ref[0] += dbias
                dqnw_ref[0] += dqnw

        pl.when(i == 0)(lambda: run(lambda g0: first_ref[0, g0:g0 + sub] if g0 + sub <= n_edge else rest_ref[0]))
        pl.when(i > 0)(lambda: run(lambda g0: rest_ref[0]))

        @pl.when(i == nb - 1)
        def _():
            step = min(512, t)

            def finish(j, dknw):
                rows = pl.ds(pl.multiple_of(j * step, step), step)
                prows = pl.ds(pl.multiple_of(PAD + j * step, CHUNK), step)
                _, vjp = jax.vjp(_rms, k_ref[rows, :], knw_ref[...])
                dk, dw = vjp(dkn_scr[prows, :])
                dkv_ref[0, rows, :] = dk.astype(BF16)
                dkv_ref[1, rows, :] = dv_scr[prows, :].astype(BF16)
                return dknw + dw

            dknw_ref[0] = lax.fori_loop(0, t // step, finish, jnp.zeros((1, HEAD_DIM), F32))

    pair_col = pl.BlockSpec((2, tb, HEAD_DIM), lambda h, i: (0, i, h))
    pair_full = pl.BlockSpec((2, t, HEAD_DIM), lambda h, i: (0, 0, h))
    head_vec = pl.BlockSpec((1, 1, HEAD_DIM), lambda h, i: (h, 0, 0))
    pair_shape = jax.ShapeDtypeStruct((2, t, heads * HEAD_DIM), BF16)
    vec_shape = jax.ShapeDtypeStruct((heads, 1, HEAD_DIM), F32)
    return pl.pallas_call(
        body,
        name=name,
        grid=(heads, nb),
        in_specs=[col(0), full(1), full(2), col(3), *bias_spec, vec, vec, pl.BlockSpec((tb, HEAD_DIM), lambda h, i: (i, h))],
        out_specs=[pair_col, pair_full, pl.BlockSpec((1, ROWS, WIN), lambda h, i: (h, 0, 0)), head_vec, head_vec],
        out_shape=[pair_shape, pair_shape, jax.ShapeDtypeStruct((heads, ROWS, WIN), F32), vec_shape, vec_shape],
        scratch_shapes=[pltpu.VMEM((PAD + t, HEAD_DIM), BF16), pltpu.VMEM((PAD + t, HEAD_DIM), BF16),
                        pltpu.VMEM((PAD + t, HEAD_DIM), F32), pltpu.VMEM((PAD + t, HEAD_DIM), F32)],
        compiler_params=_params("arbitrary", "arbitrary"),
    )(proj, proj, proj, proj, *bias, qnw_row, knw_row, dog)


def _lane_row(v):
    v = v.reshape(1, -1)
    return jnp.pad(v, ((0, 0), (0, LANES - v.shape[1])))


def _local_step(x, target, norm_w, wa_in, conv_w, a_log, dt_bias, onw, wa_out, wb_in, qnw, knw, rel_bias, wb_out, *,
                sharded=False):
    ha, hb = a_log.shape[-1], rel_bias.shape[-2]
    na = 4 * ha * HEAD_DIM
    wa_main = wa_in[:, :na]
    wa_ab = jnp.pad(wa_in[:, na:], ((0, 0), (0, LANES - 2 * ha)))
    alog_row, dtb_row, onw_row = _lane_row(a_log), _lane_row(dt_bias), _lane_row(onw)
    qnw_row, knw_row = _lane_row(qnw), _lane_row(knw)
    bias = _masked_bias(_band_bias(rel_bias.reshape(hb, -1)), min(min(ATTN_BLOCK, x.shape[0]) // ROWS, N_EDGE))

    hn0, ab_a = _rmsnorm_fwd(x, norm_w[0:1], wa_ab, name="norm0")
    proj_a = _matmul(hn0, wa_main, name="a_in")
    og_a, states, got = _gdn_fwd(proj_a, ab_a, conv_w, alog_row, dtb_row, onw_row, heads=ha, name="gdn_fwd",
                                 gather=[wb_in, wa_out, wb_out] if sharded else [])
    if sharded:
        wb_in, wa_out, wb_out = _join_cols(got[0]), got[1].reshape(-1, got[1].shape[-1]), got[2].reshape(-1, got[2].shape[-1])
    h1, hn1 = _matmul_norm(og_a, wa_out, x, norm_w[1:2], name="a_out_norm1")
    proj_b = _matmul(hn1, wb_in, name="b_in")
    og_b = _attn_fwd(proj_b, bias, qnw_row, knw_row, heads=hb, name="attn_fwd")
    loss, dh2, dh2_b = _matmul_loss(og_b, wb_out, h1, target, name="b_out_loss")

    grad_dtype = BF16 if sharded else F32
    dog_b = _matmul(dh2_b, wb_out, trans_b=True, name="d_b_out_x")
    dwb_out = _matmul(og_b, dh2_b, trans_a=True, out_dtype=grad_dtype, name="d_b_out_w")
    dqz, dkv, dbias, dqnw, dknw = _attn_bwd(proj_b, bias, qnw_row, knw_row, dog_b, heads=hb, name="attn_bwd")
    dproj_b, qkvz = [dqz, dkv], (0, 3, 1, 2)
    dhn1 = _matmul(dproj_b, wb_in, trans_b=True, order=qkvz, name="d_b_in_x")
    dwb_in = _matmul(hn1, dproj_b, trans_a=True, order=qkvz, out_dtype=grad_dtype, col_slabs=N_CHIPS if sharded else 0,
                     name="d_b_in_w")
    dh1, dh1_b, dnw1 = _rmsnorm_bwd(h1, norm_w[1:2], dhn1, dh2, name="d_norm1")

    dog_a = _matmul(dh1_b, wa_out, trans_b=True, name="d_a_out_x")
    dwa_out = _matmul(og_a, dh1_b, trans_a=True, out_dtype=grad_dtype, name="d_a_out_w")
    early = [dwb_in, _split_rows(dwa_out), _split_rows(dwb_out)] if sharded else []
    dproj_a, dab, dconv, dalog, ddtb, donw, landed = _gdn_bwd(
        proj_a, ab_a, conv_w, alog_row, dtb_row, onw_row, states, dog_a, heads=ha, name="gdn_bwd", exchange=early)
    dab_b = dab.astype(BF16)
    if sharded:
        mine = [_sum_slots(s, name=f"chip_sum_{n}") for n, s in zip(("b_w_in", "a_w_out", "b_w_out"), landed)]
        dwa_main, theirs = _matmul(hn0, dproj_a, trans_a=True, out_dtype=grad_dtype, name="d_a_in_w", exchange=mine,
                                   with_pair=True)
        dwb_in, dwa_out, dwb_out = zip(mine, theirs)
    else:
        dwa_main = _matmul(hn0, dproj_a, trans_a=True, out_dtype=grad_dtype, name="d_a_in_w")
    dwa_in = jnp.concatenate(
        [dwa_main, _matmul(hn0, dab_b, trans_a=True, out_dtype=grad_dtype, name="d_a_in_ab_w")[:, :2 * ha]], axis=1)
    if sharded:
        dhn0, (dwa_in, dconv) = _matmul(dproj_a, wa_main, trans_b=True, name="d_a_in_x",
                                        exchange=[_split_cols(dwa_in), _split_cols(dconv)])
    else:
        dhn0 = _matmul(dproj_a, wa_main, trans_b=True, name="d_a_in_x")
    dx, _, dnw0 = _rmsnorm_bwd(x, norm_w[0:1], dhn0, dh1, narrow=(dab_b, wa_ab), name="d_norm0")

    drel = _band_bias_grad(dbias)
    grads = dict(
        norm_w=jnp.concatenate([dnw0, dnw1], axis=0), a_w_in=dwa_in, a_conv_w=dconv, a_a_log=dalog[:, :ha],
        a_dt_bias=ddtb[:, :ha], a_out_norm_w=donw, a_w_out=dwa_out, b_w_in=dwb_in, b_q_norm_w=jnp.sum(dqnw, axis=0),
        b_k_norm_w=jnp.sum(dknw, axis=0), b_rel_bias=drel[None], b_w_out=dwb_out)
    return loss, dx, grads


_ANY = pl.BlockSpec(memory_space=pl.ANY)
_CHIP_FLIPS = ((1, 0), (0, 1), (1, 1))


def _place():
    x, y, c = lax.axis_index("x"), lax.axis_index("y"), lax.axis_index("c")
    return x, y, c


def _flip(v, bit):
    return 1 - v if bit else v


def _remote(src, dst, send_sem, recv_sem, peer):
    return pltpu.make_async_remote_copy(src_ref=src, dst_ref=dst, send_sem=send_sem, recv_sem=recv_sem, device_id=peer,
                                        device_id_type=MESH)


def _comm_call(body, arrays, out_shapes, n_remote, n_local, name):
    scratch = [pltpu.SemaphoreType.DMA((n_remote,)), pltpu.SemaphoreType.DMA((n_remote,))]
    if n_local:
        scratch.append(pltpu.SemaphoreType.DMA((n_local,)))
    return pl.pallas_call(
        body, name=name, in_specs=[_ANY] * len(arrays), out_specs=[_ANY] * len(out_shapes), out_shape=out_shapes,
        scratch_shapes=scratch)(*arrays)


def _chip_scratch(n):
    return [pltpu.SemaphoreType.DMA((3 * n,)), pltpu.SemaphoreType.DMA((3 * n,)), pltpu.SemaphoreType.DMA((n,))]


def _chip_shapes(gather, arrays):
    return [jax.ShapeDtypeStruct(((N_CHIPS,) + s.shape) if gather else s.shape, s.dtype) for s in arrays]


def _chip_traffic(gather, ins, outs, sems):
    send_sems, recv_sems, local_sems = sems
    x, y, c = _place()
    mine = 2 * x + y
    local, remote, landing = [], [], []
    for a in range(len(ins)):
        local.append(pltpu.make_async_copy(ins[a] if gather else ins[a].at[mine], outs[a].at[mine], local_sems.at[a]))
        for k, (fx, fy) in enumerate(_CHIP_FLIPS):
            peer = (_flip(x, fx), _flip(y, fy), c)
            theirs = 2 * peer[0] + peer[1]
            src = ins[a] if gather else ins[a].at[theirs]
            pair = send_sems.at[3 * a + k], recv_sems.at[3 * a + k]
            remote.append(_remote(src, outs[a].at[mine], *pair, peer))
            landing.append(_remote(src, outs[a].at[theirs], *pair, peer))
    return local + remote, (local, landing, remote)


def _start(traffic):
    for cp in traffic[0]:
        cp.start()


def _finish(traffic):
    local, landing, remote = traffic[1]
    for cp in local:
        cp.wait()
    for cp in landing:
        cp.wait_recv()
    for cp in remote:
        cp.wait_send()


def _pair_scratch(n):
    return [pltpu.SemaphoreType.DMA((n,)), pltpu.SemaphoreType.DMA((n,))]


def _pair_traffic(ins, outs, sems):
    send_sems, recv_sems = sems
    x, y, c = _place()
    copies = [_remote(ins[a], outs[a], send_sems.at[a], recv_sems.at[a], (x, y, 1 - c)) for a in range(len(ins))]
    return copies, ([], copies, copies)


def _with_exchange(compute, n_in, n_out, gather, n_x, grid):
    if not n_x:
        return compute

    def body(*refs):
        ins, x_in = refs[:n_in], refs[n_in:n_in + n_x]
        outs, x_out = refs[n_in + n_x:n_in + n_x + n_out], refs[n_in + n_x + n_out:n_in + 2 * n_x + n_out]
        n_sems = 2 if gather == "pair" else 3
        scratch, sems = refs[n_in + 2 * n_x + n_out:-n_sems], refs[-n_sems:]
        traffic = _pair_traffic(x_in, x_out, sems) if gather == "pair" else _chip_traffic(gather, x_in, x_out, sems)
        first = functools.reduce(jnp.logical_and, [pl.program_id(d) == 0 for d in range(len(grid))])
        last = functools.reduce(jnp.logical_and, [pl.program_id(d) == grid[d] - 1 for d in range(len(grid))])

        @pl.when(first)
        def _():
            _start(traffic)

        compute(*ins, *outs, *scratch)

        @pl.when(last)
        def _():
            _finish(traffic)

    return body


def _gather_shared(shard, small, *, name):
    rows = shard.shape[0]
    assert rows % 2 == 0
    half = rows // 2

    def body(shard_ref, small_ref, out_ref, small_out_ref, send_sems, recv_sems, local_sems):
        x, y, c = _place()
        mine = 2 * x + y
        sibling = (x, y, 1 - c)
        my_rows = pl.ds(pl.multiple_of(c * half, 8), half)
        local = [pltpu.make_async_copy(shard_ref, out_ref.at[mine], local_sems.at[0]),
                 pltpu.make_async_copy(small_ref, small_out_ref.at[mine], local_sems.at[1])]
        sent, landed, passed_on, handed = [], [], [], []
        for k, (fx, fy) in enumerate(_CHIP_FLIPS):
            peer = (_flip(x, fx), _flip(y, fy), c)
            theirs = 2 * peer[0] + peer[1]
            ici, d2d, tiny = [(send_sems.at[3 * n + k], recv_sems.at[3 * n + k]) for n in range(3)]
            sent.append(_remote(shard_ref.at[my_rows], out_ref.at[mine, my_rows], *ici, peer))
            landed.append(_remote(shard_ref.at[my_rows], out_ref.at[theirs, my_rows], *ici, peer))
            sent.append(_remote(small_ref, small_out_ref.at[mine], *tiny, peer))
            landed.append(_remote(small_ref, small_out_ref.at[theirs], *tiny, peer))
            passed_on.append(_remote(out_ref.at[theirs, my_rows], out_ref.at[theirs, my_rows], *d2d, sibling))
            other_rows = pl.ds(pl.multiple_of((1 - c) * half, 8), half)
            handed.append(_remote(out_ref.at[theirs, other_rows], out_ref.at[theirs, other_rows], *d2d, sibling))
        for cp in local + sent:
            cp.start()
        for k in range(3):
            landed[2 * k].wait_recv()
            passed_on[k].start()
        for k in range(3):
            landed[2 * k + 1].wait_recv()
            handed[k].wait_recv()
        for cp in local:
            cp.wait()
        for cp in sent + passed_on:
            cp.wait_send()

    return pl.pallas_call(
        body, name=name, in_specs=[_ANY] * 2, out_specs=[_ANY] * 2, out_shape=_chip_shapes(True, [shard, small]),
        scratch_shapes=[pltpu.SemaphoreType.DMA((9,)), pltpu.SemaphoreType.DMA((9,)), pltpu.SemaphoreType.DMA((2,))],
    )(shard, small)


def _swap_pair(arrays, *, name):
    n = len(arrays)

    def body(*refs):
        ins, outs, (send_sems, recv_sems) = refs[:n], refs[n:2 * n], refs[2 * n:]
        x, y, c = _place()
        copies = [_remote(ins[a], outs[a], send_sems.at[a], recv_sems.at[a], (x, y, 1 - c)) for a in range(n)]
        for cp in copies:
            cp.start()
        for cp in copies:
            cp.wait_recv()
        for cp in copies:
            cp.wait_send()

    shapes = [jax.ShapeDtypeStruct(s.shape, s.dtype) for s in arrays]
    return _comm_call(body, arrays, shapes, n, 0, name)


def _gather_all(tile, *, name):
    def body(in_ref, out_ref, send_sems, recv_sems, local_sems):
        x, y, c = _place()
        mine = 4 * x + 2 * y + c
        local = pltpu.make_async_copy(in_ref, out_ref.at[mine], local_sems.at[0])
        remote, landing = [], []
        for k in range(1, N_DEV):
            peer = (_flip(x, k & 4), _flip(y, k & 2), _flip(c, k & 1))
            sems = send_sems.at[k - 1], recv_sems.at[k - 1]
            remote.append(_remote(in_ref, out_ref.at[mine], *sems, peer))
            landing.append(_remote(in_ref, out_ref.at[4 * peer[0] + 2 * peer[1] + peer[2]], *sems, peer))
        for cp in [local] + remote:
            cp.start()
        local.wait()
        for cp in landing:
            cp.wait_recv()
        for cp in remote:
            cp.wait_send()

    return _comm_call(body, [tile], [jax.ShapeDtypeStruct((N_DEV,) + tile.shape, tile.dtype)], N_DEV - 1, 1, name)[0]


def _sum_slots(slabs, *, name, tr=128):
    s, r, c = slabs.shape
    tr = min(tr, r)

    def body(in_ref, o_ref):
        acc = in_ref[0].astype(F32)
        for j in range(1, s):
            acc = acc + in_ref[j].astype(F32)
        o_ref[...] = acc

    return pl.pallas_call(
        body, name=name, grid=(r // tr,),
        in_specs=[pl.BlockSpec((s, tr, c), lambda i: (0, i, 0))], out_specs=pl.BlockSpec((tr, c), lambda i: (i, 0)),
        out_shape=jax.ShapeDtypeStruct((r, c), F32), compiler_params=_params("parallel"))(slabs)


def _adamw_math(w, g, m, v):
    m = ADAM_B1 * m + (1.0 - ADAM_B1) * g
    v = ADAM_B2 * v + (1.0 - ADAM_B2) * (g * g)
    m_hat = m / (1.0 - ADAM_B1 ** ADAM_STEP)
    v_hat = v / (1.0 - ADAM_B2 ** ADAM_STEP)
    delta = -ADAM_LR * (m_hat / (jnp.sqrt(v_hat) + ADAM_EPS) + ADAM_WD * w)
    return delta, m, v


def _adamw(w, m, v, parts, *, name, tr=128):
    r, c = w.shape
    tr = min(tr, r)
    s = len(parts)

    def body(w_ref, m_ref, v_ref, *refs):
        g_ref, d_ref, nm_ref, nv_ref = refs[s:]
        g = refs[0][...]
        for p_ref in refs[1:s]:
            g = g + p_ref[...]
        g_ref[...] = g
        d_ref[...], nm_ref[...], nv_ref[...] = _adamw_math(w_ref[...], g, m_ref[...], v_ref[...])

    blk = pl.BlockSpec((tr, c), lambda i: (i, 0))
    shape = jax.ShapeDtypeStruct((r, c), F32)
    return pl.pallas_call(
        body, name=name, grid=(r // tr,), in_specs=[blk] * (3 + s), out_specs=[blk] * 4, out_shape=[shape] * 4,
        compiler_params=_params("parallel"))(w, m, v, *parts)


_BIG = ("a_w_in", "b_w_in", "a_w_out", "b_w_out", "a_conv_w")
_SMALL = ("norm_w", "a_a_log", "a_dt_bias", "a_out_norm_w", "b_q_norm_w", "b_k_norm_w", "b_rel_bias")
_ORDER = ("norm_w", "a_w_in", "a_conv_w", "a_a_log", "a_dt_bias", "a_out_norm_w", "a_w_out", "b_w_in", "b_q_norm_w",
          "b_k_norm_w", "b_rel_bias", "b_w_out")


def _join_cols(g):
    return jnp.transpose(g, (1, 0, 2)).reshape(g.shape[1], -1)


def _split_cols(g):
    return jnp.transpose(g.reshape(g.shape[0], N_CHIPS, -1), (1, 0, 2))


def _split_rows(g):
    return g.reshape(N_CHIPS, -1, g.shape[-1])


def _pack(d):
    flat = jnp.concatenate([d[n].reshape(-1) for n in _SMALL])
    return jnp.pad(flat, (0, -flat.shape[0] % LANES)).reshape(1, -1)


def _unpack(row, like):
    out, at = {}, 0
    for n in _SMALL:
        size = like[n].size
        out[n] = row[0, at:at + size].reshape(like[n].shape)
        at += size
    return out


def kernel(x, norm_w, a_w_in, a_conv_w, a_a_log, a_dt_bias, a_out_norm_w, a_w_out, b_w_in, b_q_norm_w, b_k_norm_w, b_rel_bias, b_w_out, loss_target, m_norm_w, m_a_w_in, m_a_conv_w, m_a_a_log, m_a_dt_bias, m_a_out_norm_w, m_a_w_out, m_b_w_in, m_b_q_norm_w, m_b_k_norm_w, m_b_rel_bias, m_b_w_out, v_norm_w, v_a_w_in, v_a_conv_w, v_a_a_log, v_a_dt_bias, v_a_out_norm_w, v_a_w_out, v_b_w_in, v_b_q_norm_w, v_b_k_norm_w, v_b_rel_bias, v_b_w_out):
    w = dict(norm_w=norm_w, a_w_in=a_w_in, a_conv_w=a_conv_w, a_a_log=a_a_log, a_dt_bias=a_dt_bias,
             a_out_norm_w=a_out_norm_w, a_w_out=a_w_out, b_w_in=b_w_in, b_q_norm_w=b_q_norm_w, b_k_norm_w=b_k_norm_w,
             b_rel_bias=b_rel_bias, b_w_out=b_w_out)
    m = dict(norm_w=m_norm_w, a_w_in=m_a_w_in, a_conv_w=m_a_conv_w, a_a_log=m_a_a_log, a_dt_bias=m_a_dt_bias,
             a_out_norm_w=m_a_out_norm_w, a_w_out=m_a_w_out, b_w_in=m_b_w_in, b_q_norm_w=m_b_q_norm_w,
             b_k_norm_w=m_b_k_norm_w, b_rel_bias=m_b_rel_bias, b_w_out=m_b_w_out)
    v = dict(norm_w=v_norm_w, a_w_in=v_a_w_in, a_conv_w=v_a_conv_w, a_a_log=v_a_a_log, a_dt_bias=v_a_dt_bias,
             a_out_norm_w=v_a_out_norm_w, a_w_out=v_a_w_out, b_w_in=v_b_w_in, b_q_norm_w=v_b_q_norm_w,
             b_k_norm_w=v_b_k_norm_w, b_rel_bias=v_b_rel_bias, b_w_out=v_b_w_out)

    wa_in, conv = _gather_shared(a_w_in[0].astype(BF16), a_conv_w[0], name="gather_a_in")
    loss, dx, grads = _local_step(
        x[0], loss_target[0], norm_w, _join_cols(wa_in), _join_cols(conv), a_a_log, a_dt_bias, a_out_norm_w,
        a_w_out[0].astype(BF16), b_w_in[0].astype(BF16), b_q_norm_w, b_k_norm_w, b_rel_bias, b_w_out[0].astype(BF16),
        sharded=True)
    loss = lax.psum(loss, ("x", "y", "c"))

    late = [n for n in _BIG if not isinstance(grads[n], tuple)]
    mine = [_sum_slots(grads[n], name=f"chip_sum_{n}") for n in late]
    sums = {n: grads[n] for n in _BIG if n not in late}
    sums.update(zip(late, zip(mine, _swap_pair(mine, name="pair_grads"))))
    out = {}
    for n in _BIG:
        out[n] = [r[None] for r in _adamw(w[n][0], m[n][0], v[n][0], list(sums[n]), name=f"adamw_{n}")]

    row = _pack(grads)
    tiles = _gather_all(jnp.broadcast_to(row, (8, row.shape[1])), name="gather_small_grads")
    res = _adamw(_pack(w), _pack(m), _pack(v), [tiles[d, 0:1, :] for d in range(N_DEV)], name="adamw_small")
    unpacked = [_unpack(r, w) for r in res]
    for n in _SMALL:
        out[n] = [u[n] for u in unpacked]

    return (loss, dx[None], *[out[n][0] for n in _ORDER], *[out[n][1] for n in _ORDER], *[out[n][2] for n in _ORDER],
            *[out[n][3] for n in _ORDER])
```

```python
import functools

import numpy as np
import jax
import jax.numpy as jnp
from jax import lax
from jax.experimental import pallas as pl
from jax.experimental.pallas import tpu as pltpu

F32 = jnp.float32
BF16 = jnp.bfloat16

CHUNK = 64
HEAD_DIM = 128
LEFT_CHUNKS = 8
REL_CLIP = 256
CONV_K = 4
EPS = 1e-6
HALO = 8

ADAM_LR = 0.001
ADAM_B1 = 0.9
ADAM_B2 = 0.999
ADAM_EPS = 1e-08
ADAM_WD = 0.01
ADAM_STEP = 10

LANES = 128
N_CHIPS = 4
N_DEV = 8
VMEM_LIMIT_BYTES = 56 * 1024 * 1024
VMEM_LIMIT_WIDE_BYTES = 63 * 1024 * 1024
MESH = pl.DeviceIdType.MESH


def _params(*sem, vmem=VMEM_LIMIT_BYTES):
    return pltpu.CompilerParams(dimension_semantics=sem, vmem_limit_bytes=vmem)


def _dot(a, b, dims=(((1,), (0,)), ((), ())), precision=None):
    return lax.dot_general(a, b, dims, precision=precision, preferred_element_type=F32)


_NT = (((1,), (1,)), ((), ()))
_TN = (((0,), (0,)), ((), ()))


def _bdot(a, b, dims=(((1,), (0,)), ((), ()))):
    return _dot(a.astype(BF16), b.astype(BF16), dims)


def _fdot(a, b, dims=(((1,), (0,)), ((), ()))):
    return _dot(a, b, dims, precision=lax.Precision.HIGH)


def _silu(x):
    return x * jax.nn.sigmoid(x)


def _stacks(x):
    if not isinstance(x, (list, tuple)) and x.ndim != 3:
        return None
    arrays = list(x) if isinstance(x, (list, tuple)) else [x]
    assert len({(v.shape[1], v.shape[2], v.dtype) for v in arrays}) == 1
    starts = [sum(v.shape[0] for v in arrays[:r]) for r in range(len(arrays))]
    return arrays, starts, starts[-1] + arrays[-1].shape[0]


def _static_pick(table, index):
    out = table[-1]
    for s in range(len(table) - 2, -1, -1):
        out = jnp.where(index == s, table[s], out)
    return out


def _matmul(a, b, *, name, trans_a=False, trans_b=False, residual=None, out_dtype=F32, tm=1024, tn=1024, tk=2048,
            col_slabs=0, order=None, exchange=(), with_pair=False):
    assert not (trans_a and trans_b)
    a_stack, b_stack = _stacks(a), _stacks(b)
    assert not (a_stack and (trans_a or b_stack)) and not (b_stack and trans_b)
    a_list, b_list = (a_stack[0] if a_stack else [a]), (b_stack[0] if b_stack else [b])
    a0, b0 = a_list[0], b_list[0]
    k, m = (a_stack[2] * a0.shape[2], a0.shape[1]) if a_stack else a.shape if trans_a else a.shape[::-1]
    n = b_stack[2] * b0.shape[2] if b_stack else b.shape[0] if trans_b else b.shape[1]
    tm, tn, tk = min(tm, m), min(tn, n // max(col_slabs, 1)), min(tk, k)
    if a_stack:
        tk = min(tk, a0.shape[2])
        per_k = a0.shape[2] // tk
    if b_stack:
        tn = min(tn, b0.shape[2])
        per_n = b0.shape[2] // tn
    assert m % tm == 0 and n % tn == 0 and k % tk == 0, (a0.shape, b0.shape, tm, tn, tk)
    nk = k // tk
    dims = _NT if trans_b else _TN if trans_a else (((1,), (0,)), ((), ()))
    order = list(order) if order is not None else list(range(max(a_stack[2] if a_stack else 0, b_stack[2] if b_stack else 0)))
    na, nb = len(a_list), len(b_list)

    def group_of(r, stack, position):
        arrays, starts, _ = stack
        local = position - starts[r]
        return jnp.logical_and(local >= 0, local < arrays[r].shape[0]), jnp.clip(local, 0, arrays[r].shape[0] - 1)

    def body(*refs):
        a_refs, b_refs = refs[:na], refs[na:na + nb]
        r_ref = refs[na + nb] if residual is not None else None
        o_ref, acc_ref = refs[-2:]
        j, kk = pl.program_id(1), pl.program_id(2)

        @pl.when(kk == 0)
        def _():
            acc_ref[...] = jnp.zeros_like(acc_ref)

        for ra, a_ref in enumerate(a_refs):
            for rb, b_ref in enumerate(b_refs):
                def add(a_ref=a_ref, b_ref=b_ref):
                    acc_ref[...] += _dot(a_ref[...], b_ref[...], dims)

                if na > 1:
                    pl.when(group_of(ra, a_stack, kk // per_k)[0])(add)
                elif nb > 1:
                    pl.when(group_of(rb, b_stack, j // per_n)[0])(add)
                else:
                    add()

        @pl.when(kk == nk - 1)
        def _():
            r = acc_ref[...]
            if r_ref is not None:
                r = r + r_ref[...]
            o_ref[...] = r.astype(o_ref.dtype)

    if a_stack:
        a_specs = [pl.BlockSpec((None, tm, tk), lambda i, j, kk, r=r: (group_of(r, a_stack, kk // per_k)[1], i, kk % per_k))
                   for r in range(na)]
        b_k = lambda kk: _static_pick(order, kk // per_k) * per_k + kk % per_k
    else:
        a_specs = [pl.BlockSpec((tk, tm), lambda i, j, kk: (kk, i)) if trans_a else pl.BlockSpec((tm, tk), lambda i, j, kk: (i, kk))]
        b_k = lambda kk: kk
    if b_stack:
        b_specs = [pl.BlockSpec((None, tk, tn), lambda i, j, kk, r=r: (group_of(r, b_stack, j // per_n)[1], kk, j % per_n))
                   for r in range(nb)]
        out_col = lambda j: _static_pick(order, j // per_n) * per_n + j % per_n
    else:
        b_specs = [pl.BlockSpec((tn, tk), lambda i, j, kk: (j, b_k(kk))) if trans_b
                   else pl.BlockSpec((tk, tn), lambda i, j, kk: (b_k(kk), j))]
        out_col = lambda j: j
    in_specs = a_specs + b_specs
    args = a_list + b_list
    if residual is not None:
        in_specs.append(pl.BlockSpec((tm, tn), lambda i, j, kk: (i, j)))
        args.append(residual)
    grid = (m // tm, n // tn, nk)
    n_x = len(exchange)
    if col_slabs:
        per = n // col_slabs // tn
        assert per * tn * col_slabs == n, (n, tn, col_slabs)
        out_spec = pl.BlockSpec((None, tm, tn), lambda i, j, kk: (out_col(j) // per, i, out_col(j) % per))
        out_shape = jax.ShapeDtypeStruct((col_slabs, m, n // col_slabs), out_dtype)
    else:
        out_spec = pl.BlockSpec((tm, tn), lambda i, j, kk: (i, out_col(j)))
        out_shape = jax.ShapeDtypeStruct((m, n), out_dtype)
    out, *landed = pl.pallas_call(
        _with_exchange(body, len(args), 1, "pair" if with_pair else False, n_x, grid),
        name=name,
        grid=grid,
        in_specs=in_specs + [_ANY] * n_x,
        out_specs=[out_spec] + [_ANY] * n_x,
        out_shape=[out_shape] + _chip_shapes(False, exchange),
        scratch_shapes=[pltpu.VMEM((tm, tn), F32)] + ((_pair_scratch if with_pair else _chip_scratch)(n_x) if n_x else []),
        compiler_params=_params(*(("arbitrary",) * 3 if n_x else ("parallel", "parallel", "arbitrary"))),
    )(*args, *exchange)
    return (out, landed) if n_x else out


def _rms(x, w):
    return x * lax.rsqrt(jnp.mean(x * x, axis=-1, keepdims=True) + EPS) * w


def _rmsnorm_fwd(x, w_row, narrow_w, *, name, tr=512):
    t, d = x.shape
    tr = min(tr, t)

    def body(x_ref, w_ref, nw_ref, o_ref, narrow_ref):
        hn = _rms(x_ref[...], w_ref[...]).astype(BF16)
        o_ref[...] = hn
        narrow_ref[...] = _dot(hn, nw_ref[...])

    return pl.pallas_call(
        body,
        name=name,
        grid=(t // tr,),
        in_specs=[pl.BlockSpec((tr, d), lambda i: (i, 0)), pl.BlockSpec((1, d), lambda i: (0, 0)),
                  pl.BlockSpec((d, LANES), lambda i: (0, 0))],
        out_specs=[pl.BlockSpec((tr, d), lambda i: (i, 0)), pl.BlockSpec((tr, LANES), lambda i: (i, 0))],
        out_shape=[jax.ShapeDtypeStruct((t, d), BF16), jax.ShapeDtypeStruct((t, LANES), F32)],
        compiler_params=_params("parallel"),
    )(x, w_row, narrow_w)


def _matmul_norm(a, b, residual, w_row, *, name, tm=512):
    t, k = a.shape
    d = b.shape[1]
    tm = min(tm, t)

    def body(a_ref, b_ref, r_ref, w_ref, h_ref, hn_ref):
        h = _dot(a_ref[...], b_ref[...]) + r_ref[...]
        h_ref[...] = h
        hn_ref[...] = _rms(h, w_ref[...]).astype(BF16)

    row = pl.BlockSpec((tm, d), lambda i: (i, 0))
    return pl.pallas_call(
        body,
        name=name,
        grid=(t // tm,),
        in_specs=[pl.BlockSpec((tm, k), lambda i: (i, 0)), pl.BlockSpec((k, d), lambda i: (0, 0)), row,
                  pl.BlockSpec((1, d), lambda i: (0, 0))],
        out_specs=[row, row],
        out_shape=[jax.ShapeDtypeStruct((t, d), F32), jax.ShapeDtypeStruct((t, d), BF16)],
        compiler_params=_params("parallel"),
    )(a, b, residual, w_row)


def _rmsnorm_bwd(x, w_row, dy, dres, *, name, tr=256, narrow=None):
    t, d = x.shape
    tr = min(tr, t)
    extra = list(narrow) if narrow is not None else []

    def body(x_ref, w_ref, dy_ref, dres_ref, *refs):
        dx_ref, dxb_ref, dw_ref = refs[len(extra):]

        @pl.when(pl.program_id(0) == 0)
        def _():
            dw_ref[...] = jnp.zeros_like(dw_ref)

        dy = dy_ref[...]
        if extra:
            dy = dy + _dot(refs[0][...], refs[1][...], _NT)
        _, vjp = jax.vjp(_rms, x_ref[...], w_ref[...])
        dx, dw = vjp(dy)
        dx = dx + dres_ref[...]
        dx_ref[...] = dx
        dxb_ref[...] = dx.astype(BF16)
        dw_ref[...] += dw

    row = pl.BlockSpec((tr, d), lambda i: (i, 0))
    vec = pl.BlockSpec((1, d), lambda i: (0, 0))
    extra_specs = [pl.BlockSpec((tr, LANES), lambda i: (i, 0)), pl.BlockSpec((d, LANES), lambda i: (0, 0))] if extra else []
    return pl.pallas_call(
        body,
        name=name,
        grid=(t // tr,),
        in_specs=[row, vec, row, row] + extra_specs,
        out_specs=[row, row, vec],
        out_shape=[jax.ShapeDtypeStruct((t, d), F32), jax.ShapeDtypeStruct((t, d), BF16), jax.ShapeDtypeStruct((1, d), F32)],
        compiler_params=_params("arbitrary"),
    )(x, w_row, dy, dres, *extra)


def _matmul_loss(a, b, residual, target, *, name, tm=512, tn=1024):
    t, k = a.shape
    d = b.shape[1]
    tm, tn = min(tm, t), min(tn, d)

    def body(a_ref, b_ref, r_ref, t_ref, dh_ref, dhb_ref, part_ref):
        @pl.when(pl.program_id(1) == 0)
        def _():
            part_ref[...] = jnp.zeros_like(part_ref)

        err = _dot(a_ref[...], b_ref[...]) + r_ref[...] - t_ref[...]
        dh = err * (1.0 / d)
        dh_ref[...] = dh
        dhb_ref[...] = dh.astype(BF16)
        part_ref[...] += jnp.sum(err * err, axis=0, keepdims=True)

    tile = pl.BlockSpec((tm, tn), lambda j, i: (i, j))
    dh, dhb, part = pl.pallas_call(
        body,
        name=name,
        grid=(d // tn, t // tm),
        in_specs=[pl.BlockSpec((tm, k), lambda j, i: (i, 0)), pl.BlockSpec((k, tn), lambda j, i: (0, j)), tile, tile],
        out_specs=[tile, tile, pl.BlockSpec((1, tn), lambda j, i: (0, j))],
        out_shape=[jax.ShapeDtypeStruct((t, d), F32), jax.ShapeDtypeStruct((t, d), BF16), jax.ShapeDtypeStruct((1, d), F32)],
        compiler_params=_params("arbitrary", "arbitrary"),
    )(a, b, residual, target)
    return 0.5 / d * jnp.sum(part), dh, dhb


_BNN = (((2,), (1,)), ((0,), (0,)))
_BNT = (((2,), (2,)), ((0,), (0,)))
_BTN = (((1,), (1,)), ((0,), (0,)))


_TAP0 = HALO - (CONV_K - 1)


def _conv(x_ref, w, rows):
    c = w[0:1, :] * x_ref[_TAP0:_TAP0 + rows, :]
    for j in range(1, CONV_K):
        c = c + w[j:j + 1, :] * x_ref[_TAP0 + j:_TAP0 + j + rows, :]
    return c


def _conv_silu_bwd(x_ref, w, dact, dc_ref, rows):
    c = _conv(x_ref, w, rows)
    sig = jax.nn.sigmoid(c)
    dc = dact * (sig * (1.0 + c * (1.0 - sig)))
    dw = [jnp.sum(dc * x_ref[_TAP0 + j:_TAP0 + j + rows, :], axis=0, keepdims=True) for j in range(CONV_K)]
    dc_ref[0:HALO, :] = jnp.zeros((HALO, HEAD_DIM), F32)
    dc_ref[HALO:HALO + rows, :] = dc
    dc_ref[HALO + rows:HALO + rows + HALO, :] = jnp.zeros((HALO, HEAD_DIM), F32)
    first = HALO - _TAP0
    dx = w[0:1, :] * dc_ref[first:first + HALO + rows, :]
    for j in range(1, CONV_K):
        dx = dx + w[j:j + 1, :] * dc_ref[first - j:first - j + HALO + rows, :]
    return dx, dw


@jax.custom_vjp
def _unit_lower_inverse(neg_l):
    n = neg_l.shape[0]
    eye = (lax.broadcasted_iota(jnp.int32, (n, CHUNK, CHUNK), 1) == lax.broadcasted_iota(jnp.int32, (n, CHUNK, CHUNK), 2))
    inv = eye.astype(F32) + neg_l
    power = _bdot(neg_l, neg_l, _BNN)
    for _ in range(4):
        both = _bdot(jnp.concatenate([inv, power], axis=1), power, _BNN)
        inv, power = inv + both[:, :CHUNK], both[:, CHUNK:]
    return inv + _bdot(inv, power, _BNN)


def _unit_lower_inverse_fwd(neg_l):
    inv = _unit_lower_inverse(neg_l)
    return inv, inv


def _unit_lower_inverse_bwd(inv, dinv):
    return (_fdot(_fdot(inv, dinv, _BTN), inv, _BNT),)


_unit_lower_inverse.defvjp(_unit_lower_inverse_fwd, _unit_lower_inverse_bwd)


def _gdn_intra(qt, kt, v, a, b, alog, dtb):
    n = a.shape[0] // CHUNK
    q = qt * lax.rsqrt(jnp.sum(qt * qt, axis=-1, keepdims=True) + EPS) * (HEAD_DIM ** -0.5)
    k = kt * lax.rsqrt(jnp.sum(kt * kt, axis=-1, keepdims=True) + EPS)
    lanes = jnp.ones((1, HEAD_DIM), F32)
    beta = jax.nn.sigmoid(b) * lanes
    sp = a + dtb
    g = (-jnp.exp(alog) * (jnp.maximum(sp, 0.0) + jnp.log(1.0 + jnp.exp(-jnp.abs(sp))))) * lanes
    q, k, v, beta, g = (t.reshape(n, CHUNK, HEAD_DIM) for t in (q, k, v, beta, g))

    row = lax.broadcasted_iota(jnp.int32, (n, CHUNK, CHUNK), 1)
    col = lax.broadcasted_iota(jnp.int32, (n, CHUNK, CHUNK), 2)
    tri_incl = row >= col
    tri_strict = row > col
    gc = _fdot(tri_incl.astype(F32), g, _BNN)
    gc_row = _fdot(g[:, :, :CHUNK], (row <= col).astype(F32), _BTN)
    decay = jnp.exp(jnp.where(tri_incl, gc[:, :, :CHUNK] - gc_row, -1e30))
    kb = k * beta
    vb = v * beta
    with_k = _bdot(jnp.concatenate([kb, q], axis=1), k, _BNT)
    neg_l = jnp.where(tri_strict, -(with_k[:, :CHUNK] * decay), 0.0)
    qk = jnp.where(tri_incl, with_k[:, CHUNK:] * decay, 0.0)
    inv = _unit_lower_inverse(neg_l)
    e = jnp.exp(gc)
    solved = _bdot(inv, jnp.concatenate([kb * e, vb], axis=2), _BNN)
    g_last = gc[:, CHUNK - 1:CHUNK, :]
    k_dec = k * jnp.exp(g_last - gc)
    from_k = _bdot(k_dec, solved, _BTN)
    from_qk = _bdot(qk, solved, _BNN)
    step, add = -from_k[:, :, :HEAD_DIM], from_k[:, :, HEAD_DIM:]
    read, out = q * e - from_qk[:, :, :HEAD_DIM], from_qk[:, :, HEAD_DIM:]
    return step, add, jnp.exp(g_last), read, out


def _gdn_scan_step(state, step, add, decay_last):
    return state * decay_last + _bdot(step, state) + add


def _gdn_outputs(states, read, out, z, onw):
    return _rms(_bdot(read, states, _BNN) + out, onw) * _silu(z)


def _scan_scratch(n, dtype):
    return [pltpu.VMEM((n, HEAD_DIM, HEAD_DIM), dtype), pltpu.VMEM((n, HEAD_DIM, HEAD_DIM), F32), pltpu.VMEM((n, 1, HEAD_DIM), F32)]


def _head_lane(h, offset=0):
    return lax.broadcasted_iota(jnp.int32, (1, LANES), 1) == h + offset


def _pick(mask, x):
    return jnp.sum(jnp.where(mask, x, 0.0), axis=1, keepdims=True)


def _gdn_specs(heads, tb, rev, nb, PAIR):
    assert heads % PAIR == 0
    blk = (lambda i: nb - 1 - i) if rev else (lambda i: i)
    hb = tb // HALO
    width, pairs = PAIR * HEAD_DIM, heads // PAIR

    def col(group):
        return pl.BlockSpec((tb, width), lambda i, h: (blk(i), group * pairs + h))

    def halo(group):
        return pl.BlockSpec((HALO, width), lambda i, h: (jnp.maximum(blk(i) * hb - 1, 0), group * pairs + h))

    def convw(group):
        return pl.BlockSpec((CONV_K, width), lambda i, h: (0, group * pairs + h))

    vec = pl.BlockSpec((1, LANES), lambda i, h: (0, 0))
    ab = pl.BlockSpec((tb, LANES), lambda i, h: (blk(i), 0))
    states = pl.BlockSpec((PAIR, tb // CHUNK, HEAD_DIM, HEAD_DIM), lambda i, h: (h, blk(i), 0, 0))
    return blk, col, halo, convw, vec, ab, states


def _head_cols(p):
    return slice(p * HEAD_DIM, (p + 1) * HEAD_DIM)


def _gdn_fwd(proj, ab, conv_w, alog_row, dtb_row, onw_row, *, heads, name, tb=1024, pair=4, gather=()):
    t = proj.shape[0]
    tb = min(tb, t)
    nb, cpb = t // tb, tb // CHUNK
    PAIR = min(pair, heads)
    _, col, halo, convw, vec, abspec, states = _gdn_specs(heads, tb, False, nb, PAIR)

    def body(q_ref, k_ref, v_ref, qh_ref, kh_ref, vh_ref, z_ref, ab_ref, wq_ref, wk_ref, wv_ref, alog_ref, dtb_ref, onw_ref,
             og_ref, st_ref, state_scr, x_scr, *op_scr):
        i, pair = pl.program_id(0), pl.program_id(1)
        abv = ab_ref[...]
        heads_here, later = [pair * PAIR + p for p in range(PAIR)], []
        for p, h in enumerate(heads_here):
            cols = _head_cols(p)
            for n, (ref, href) in enumerate(((q_ref, qh_ref), (k_ref, kh_ref), (v_ref, vh_ref))):
                x_scr[p, n, 0:HALO, :] = jnp.where(i > 0, href[:, cols], 0.0)
                x_scr[p, n, HALO:HALO + tb, :] = ref[:, cols]
            sel_a, sel_b = _head_lane(h), _head_lane(h, heads)
            alog, dtb = _pick(sel_a, alog_ref[...]), _pick(sel_a, dtb_ref[...])
            acts = [_silu(_conv(x_scr.at[p, n], w_ref[:, cols], tb)) for n, w_ref in enumerate((wq_ref, wk_ref, wv_ref))]
            *scan, read, out = _gdn_intra(*acts, _pick(sel_a, abv), _pick(sel_b, abv), alog, dtb)
            for scr, val in zip(op_scr[3 * p:3 * p + 3], scan):
                scr[...] = val.astype(scr.dtype)
            later.append((read, out))

        def chunk(c, states):
            for p in range(PAIR):
                st_ref[p, c] = states[p]
            return tuple(_gdn_scan_step(states[p], *[scr[c] for scr in op_scr[3 * p:3 * p + 3]]) for p in range(PAIR))

        @pl.when(i == 0)
        def _():
            for h in heads_here:
                state_scr[h] = jnp.zeros((HEAD_DIM, HEAD_DIM), F32)

        last = lax.fori_loop(0, cpb, chunk, tuple(state_scr[h] for h in heads_here))
        for p, h in enumerate(heads_here):
            cols = _head_cols(p)
            state_scr[h] = last[p]
            og = _gdn_outputs(st_ref[p], *later[p], z_ref[:, cols].reshape(cpb, CHUNK, HEAD_DIM), onw_ref[...])
            og_ref[:, cols] = og.reshape(tb, HEAD_DIM).astype(BF16)

    n_x = len(gather)
    grid = (nb, heads // PAIR)
    og, st, *gathered = pl.pallas_call(
        _with_exchange(body, 14, 2, True, n_x, grid),
        name=name,
        grid=grid,
        in_specs=[col(0), col(1), col(2), halo(0), halo(1), halo(2), col(3), abspec, convw(0), convw(1), convw(2), vec, vec, vec]
        + [_ANY] * n_x,
        out_specs=[pl.BlockSpec((tb, PAIR * HEAD_DIM), lambda i, h: (i, h)), states] + [_ANY] * n_x,
        out_shape=[jax.ShapeDtypeStruct((t, heads * HEAD_DIM), BF16),
                   jax.ShapeDtypeStruct((heads, t // CHUNK, HEAD_DIM, HEAD_DIM), F32)] + _chip_shapes(True, gather),
        scratch_shapes=[pltpu.VMEM((heads, HEAD_DIM, HEAD_DIM), F32), pltpu.VMEM((PAIR, 3, HALO + tb, HEAD_DIM), F32)]
        + _scan_scratch(cpb, BF16) * PAIR + (_chip_scratch(n_x) if n_x else []),
        compiler_params=_params("arbitrary", "arbitrary"),
    )(proj, proj, proj, proj, proj, proj, proj, ab, conv_w, conv_w, conv_w, alog_row, dtb_row, onw_row, *gather)
    return og, st, gathered


def _gdn_bwd(proj, ab, conv_w, alog_row, dtb_row, onw_row, states, dog, *, heads, name, tb=1024, pair=2, exchange=()):
    t = proj.shape[0]
    tb = min(tb, t)
    nb, cpb = t // tb, tb // CHUNK
    PAIR = min(pair, heads)
    _, col, halo, convw, vec, abspec, states_spec = _gdn_specs(heads, tb, True, nb, PAIR)
    n_conv = conv_w.shape[1]

    def body(q_ref, k_ref, v_ref, qh_ref, kh_ref, vh_ref, z_ref, ab_ref, wq_ref, wk_ref, wv_ref, alog_ref, dtb_ref, onw_ref,
             st_ref, dog_ref, dproj_ref, dab_ref, dconv_ref, dalog_ref, ddtb_ref, donw_ref,
             dstate_scr, x_scr, carry_scr, *scr):
        op_scr, dop_scr, dstates_scr, dc_scr = scr[:3 * PAIR], scr[3 * PAIR:6 * PAIR], scr[6 * PAIR:7 * PAIR], scr[7 * PAIR]
        i, pair = pl.program_id(0), pl.program_id(1)
        first_block = i == nb - 1
        heads_here, later = [pair * PAIR + p for p in range(PAIR)], []

        @pl.when(jnp.logical_and(i == 0, pair == 0))
        def _():
            dconv_ref[...] = jnp.zeros_like(dconv_ref)
            dalog_ref[...] = jnp.zeros_like(dalog_ref)
            ddtb_ref[...] = jnp.zeros_like(ddtb_ref)
            donw_ref[...] = jnp.zeros_like(donw_ref)

        @pl.when(pair == 0)
        def _():
            dab_ref[...] = jnp.zeros_like(dab_ref)

        @pl.when(i == 0)
        def _():
            for h in heads_here:
                dstate_scr[h] = jnp.zeros((HEAD_DIM, HEAD_DIM), F32)
                carry_scr[h] = jnp.zeros((3, HALO, HEAD_DIM), F32)

        abv = ab_ref[...]
        w_refs = (wq_ref, wk_ref, wv_ref)
        for p, h in enumerate(heads_here):
            cols = _head_cols(p)
            for n, (ref, href) in enumerate(((q_ref, qh_ref), (k_ref, kh_ref), (v_ref, vh_ref))):
                x_scr[p, n, 0:HALO, :] = jnp.where(first_block, 0.0, href[:, cols])
                x_scr[p, n, HALO:HALO + tb, :] = ref[:, cols]
            sel_a, sel_b = _head_lane(h), _head_lane(h, heads)
            alog, dtb = _pick(sel_a, alog_ref[...]), _pick(sel_a, dtb_ref[...])
            acts = [_silu(_conv(x_scr.at[p, n], w_ref[:, cols], tb)) for n, w_ref in enumerate(w_refs)]
            (*scan, read, out), vjp_intra = jax.vjp(_gdn_intra, *acts, _pick(sel_a, abv), _pick(sel_b, abv), alog, dtb)
            for s, val in zip(op_scr[3 * p:3 * p + 3], scan):
                s[...] = val.astype(s.dtype)
            blocked = lambda ref: ref[:, cols].reshape(cpb, CHUNK, HEAD_DIM)
            _, vjp_outputs = jax.vjp(_gdn_outputs, st_ref[p], read, out, blocked(z_ref), onw_ref[...])
            dstates_scr[p][...], dread, dout, dz, donw = vjp_outputs(blocked(dog_ref))
            dproj_ref[3, :, cols] = dz.reshape(tb, HEAD_DIM).astype(BF16)
            donw_ref[...] += donw
            later.append((vjp_intra, dread, dout, sel_a, sel_b))

        def chunk(i_rev, dstates):
            c = cpb - 1 - i_rev
            new = []
            for p in range(PAIR):
                _, vjp = jax.vjp(_gdn_scan_step, st_ref[p, c], *[s[c].astype(F32) for s in op_scr[3 * p:3 * p + 3]])
                dstate, *grads = vjp(dstates[p])
                for s, val in zip(dop_scr[3 * p:3 * p + 3], grads):
                    s[c] = val
                new.append(dstate + dstates_scr[p][c])
            return tuple(new)

        last = lax.fori_loop(0, cpb, chunk, tuple(dstate_scr[h] for h in heads_here))
        for p, h in enumerate(heads_here):
            cols = _head_cols(p)
            vjp_intra, dread, dout, sel_a, sel_b = later[p]
            dstate_scr[h] = last[p]
            *dacts, da, db, dalog, ddtb = vjp_intra((*[s[...] for s in dop_scr[3 * p:3 * p + 3]], dread, dout))
            dab_ref[...] += jnp.where(sel_a, da, 0.0) + jnp.where(sel_b, db, 0.0)
            for n, (dact, w_ref) in enumerate(zip(dacts, w_refs)):
                dx, dw = _conv_silu_bwd(x_scr.at[p, n], w_ref[:, cols], dact, dc_scr, tb)
                x_scr[p, n] = dx
                x_scr[p, n, tb:tb + HALO, :] += carry_scr[h, n]
                carry_scr[h, n] = x_scr[p, n, 0:HALO, :]
                dproj_ref[n, :, cols] = x_scr[p, n, HALO:HALO + tb, :].astype(BF16)
                lanes = pl.ds(pl.multiple_of((n * heads + h) * HEAD_DIM, HEAD_DIM), HEAD_DIM)
                for j in range(CONV_K):
                    dconv_ref[j:j + 1, lanes] += dw[j]
            dalog_ref[...] += jnp.where(sel_a, dalog, 0.0)
            ddtb_ref[...] += jnp.where(sel_a, ddtb, 0.0)

    dog_spec = pl.BlockSpec((tb, PAIR * HEAD_DIM), lambda i, h: (nb - 1 - i, h))
    dproj_spec = pl.BlockSpec((4, tb, PAIR * HEAD_DIM), lambda i, h: (0, nb - 1 - i, h))
    row_shape = jax.ShapeDtypeStruct((1, LANES), F32)
    n_x = len(exchange)
    grid = (nb, heads // PAIR)
    outs = pl.pallas_call(
        _with_exchange(body, 16, 6, False, n_x, grid),
        name=name,
        grid=grid,
        in_specs=[col(0), col(1), col(2), halo(0), halo(1), halo(2), col(3), abspec, convw(0), convw(1), convw(2), vec, vec, vec,
                  states_spec, dog_spec] + [_ANY] * n_x,
        out_specs=[dproj_spec, abspec, pl.BlockSpec((CONV_K, n_conv), lambda i, h: (0, 0)), vec, vec, vec] + [_ANY] * n_x,
        out_shape=[jax.ShapeDtypeStruct((4, t, heads * HEAD_DIM), BF16), jax.ShapeDtypeStruct((t, LANES), F32),
                   jax.ShapeDtypeStruct((CONV_K, n_conv), F32), row_shape, row_shape, row_shape] + _chip_shapes(False, exchange),
        scratch_shapes=[pltpu.VMEM((heads, HEAD_DIM, HEAD_DIM), F32), pltpu.VMEM((PAIR, 3, HALO + tb, HEAD_DIM), F32),
                        pltpu.VMEM((heads, 3, HALO, HEAD_DIM), F32)] + _scan_scratch(cpb, BF16) * PAIR
        + _scan_scratch(cpb, F32) * PAIR + [pltpu.VMEM((cpb, HEAD_DIM, HEAD_DIM), F32)] * PAIR
        + [pltpu.VMEM((HALO + tb + HALO, HEAD_DIM), F32)]
        + (_chip_scratch(n_x) if n_x else []),
        compiler_params=_params("arbitrary", "arbitrary", vmem=VMEM_LIMIT_WIDE_BYTES),
    )(proj, proj, proj, proj, proj, proj, proj, ab, conv_w, conv_w, conv_w, alog_row, dtb_row, onw_row, states, dog, *exchange)
    return (*outs[:6], outs[6:])


BAND = (LEFT_CHUNKS + 1) * CHUNK
PAD = LEFT_CHUNKS * CHUNK
GROUP = 2
ROWS = GROUP * CHUNK
WIN = (LEFT_CHUNKS + GROUP) * CHUNK
DIAGS = WIN + ROWS - 1
NEAR = PAD + ROWS - 1 - REL_CLIP
assert 0 < NEAR < DIAGS and WIN - PAD - 1 <= REL_CLIP and WIN % LANES == 0
ATTN_BLOCK = 1024
N_EDGE = PAD // ROWS


def _band_bias(rel_bias):
    heads = rel_bias.shape[0]
    far = jnp.broadcast_to(rel_bias[:, 2 * REL_CLIP:], (heads, NEAR + 1))
    near = rel_bias[:, 2 * REL_CLIP + NEAR + 1 - DIAGS:2 * REL_CLIP][:, ::-1]
    diag = jnp.concatenate([far, near], axis=1)
    return jnp.stack([diag[:, ROWS - 1 - r:ROWS - 1 - r + WIN] for r in range(ROWS)], axis=1)


def _band_bias_grad(dbias):
    heads = dbias.shape[0]
    diag = sum(jnp.pad(dbias[:, r, :], ((0, 0), (ROWS - 1 - r, r))) for r in range(ROWS))
    far = jnp.sum(diag[:, :NEAR + 1], axis=1, keepdims=True)
    near = diag[:, NEAR + 1:][:, ::-1]
    unused = jnp.zeros((heads, 2 * REL_CLIP - near.shape[1]), F32)
    return jnp.concatenate([unused, near, far], axis=1)


def _masked_bias(bias, n):
    r = np.arange(ROWS)[:, None]
    key = np.arange(WIN)[None, :]
    band_start = (r // CHUNK) * CHUNK
    in_band = np.logical_and(key >= band_start, key < band_start + BAND)
    in_sequence = key[None] >= PAD - np.arange(n)[:, None, None] * ROWS
    first = jnp.where(np.logical_and(in_band[None], in_sequence)[None], bias[:, None], -1e30)
    return first, jnp.where(in_band[None, None], bias[:, None], -1e30)


def _attn_groups(q_pre, z, kn, v, bias, qnw):
    q = _rms(q_pre, qnw)
    s = _bdot(q, kn, _BNT) * (HEAD_DIM ** -0.5) + bias
    p = jnp.exp(s - jnp.max(s, axis=-1, keepdims=True))
    p = p / jnp.sum(p, axis=-1, keepdims=True)
    return _bdot(p, v, _BNN) * _silu(z)


def _attn_groups_bwd(q_pre, z, kn, v, bias, qnw, dog):
    scale = HEAD_DIM ** -0.5
    inv_rms = lax.rsqrt(jnp.mean(q_pre * q_pre, axis=-1, keepdims=True) + EPS)
    q_hat = q_pre * inv_rms
    q_b = (q_hat * qnw).astype(BF16)
    s = _dot(q_b, kn, _BNT) * scale + bias
    e = jnp.exp(s - jnp.max(s, axis=-1, keepdims=True))
    p = e * (1.0 / jnp.sum(e, axis=-1, keepdims=True))
    p_b = p.astype(BF16)
    o = _dot(p_b, v, _BNN)
    sig = jax.nn.sigmoid(z)
    do = dog * (z * sig)
    dz = dog * o * (sig * (1.0 + z * (1.0 - sig)))
    do_b = do.astype(BF16)
    dv = _dot(p_b, do_b, _BTN)
    dp = _dot(do_b, v, _BNT)
    ds = p * (dp - jnp.sum(do * o, axis=-1, keepdims=True))
    ds_b = (ds * scale).astype(BF16)
    dq = _dot(ds_b, kn, _BNN)
    dkn = _dot(ds_b, q_b, _BTN)
    dqnw = jnp.sum(jnp.sum(dq * q_hat, axis=0), axis=0, keepdims=True)
    dq_hat = dq * qnw
    dq_pre = inv_rms * (dq_hat - q_hat * jnp.mean(dq_hat * q_hat, axis=-1, keepdims=True))
    return dq_pre, dz, dkn, dv, jnp.sum(ds, axis=0), dqnw


def _attn_specs(heads, tb, t):
    def col(group):
        return pl.BlockSpec((tb, HEAD_DIM), lambda h, i: (i, group * heads + h))

    def full(group):
        return pl.BlockSpec((t, HEAD_DIM), lambda h, i: (0, group * heads + h))

    bias = [pl.BlockSpec((1, min(tb // ROWS, N_EDGE), ROWS, WIN), lambda h, i: (h, 0, 0, 0)),
            pl.BlockSpec((1, 1, ROWS, WIN), lambda h, i: (h, 0, 0, 0))]
    vec = pl.BlockSpec((1, HEAD_DIM), lambda h, i: (0, 0))
    return col, full, bias, vec


def _attn_windows(scr, block_start, n):
    return jnp.stack([scr[pl.ds(pl.multiple_of(block_start + g * ROWS, ROWS), WIN), :] for g in range(n)])


def _attn_fill(k_ref, v_ref, knw_ref, kn_scr, v_scr, t):
    kn_scr[0:PAD, :] = jnp.zeros((PAD, HEAD_DIM), BF16)
    v_scr[0:PAD, :] = jnp.zeros((PAD, HEAD_DIM), BF16)
    step = min(512, t)

    def fill(j, _):
        rows = pl.ds(pl.multiple_of(j * step, step), step)
        prows = pl.ds(pl.multiple_of(PAD + j * step, CHUNK), step)
        kn_scr[prows, :] = _rms(k_ref[rows, :], knw_ref[...]).astype(BF16)
        v_scr[prows, :] = v_ref[rows, :].astype(BF16)
        return 0

    lax.fori_loop(0, t // step, fill, 0)


def _attn_fwd(proj, bias, qnw_row, knw_row, *, heads, name, tb=2 * ATTN_BLOCK):
    t = proj.shape[0]
    tb = min(tb, t)
    nb, ng = t // tb, tb // ROWS
    col, full, bias_spec, vec = _attn_specs(heads, tb, t)

    def body(q_ref, k_ref, v_ref, z_ref, first_ref, rest_ref, qnw_ref, knw_ref, og_ref, kn_scr, v_scr):
        i = pl.program_id(1)

        @pl.when(i == 0)
        def _():
            _attn_fill(k_ref, v_ref, knw_ref, kn_scr, v_scr, t)

        def run(block_bias):
            start = i * tb
            og = _attn_groups(q_ref[...].reshape(ng, ROWS, HEAD_DIM), z_ref[...].reshape(ng, ROWS, HEAD_DIM),
                              _attn_windows(kn_scr, start, ng), _attn_windows(v_scr, start, ng), block_bias, qnw_ref[...])
            og_ref[...] = og.reshape(tb, HEAD_DIM).astype(BF16)

        def first_bias():
            edge = first_ref[0]
            more = ng - edge.shape[0]
            return edge if more == 0 else jnp.concatenate([edge, jnp.broadcast_to(rest_ref[0], (more, ROWS, WIN))])

        pl.when(i == 0)(lambda: run(first_bias()))
        pl.when(i > 0)(lambda: run(rest_ref[0]))

    return pl.pallas_call(
        body,
        name=name,
        grid=(heads, nb),
        in_specs=[col(0), full(1), full(2), col(3), *bias_spec, vec, vec],
        out_specs=pl.BlockSpec((tb, HEAD_DIM), lambda h, i: (i, h)),
        out_shape=jax.ShapeDtypeStruct((t, heads * HEAD_DIM), BF16),
        scratch_shapes=[pltpu.VMEM((PAD + t, HEAD_DIM), BF16), pltpu.VMEM((PAD + t, HEAD_DIM), BF16)],
        compiler_params=_params("arbitrary", "arbitrary"),
    )(proj, proj, proj, proj, *bias, qnw_row, knw_row)


def _attn_bwd(proj, bias, qnw_row, knw_row, dog, *, heads, name, tb=ATTN_BLOCK, sub=4):
    t = proj.shape[0]
    tb = min(tb, t)
    nb, ng = t // tb, tb // ROWS
    sub = min(sub, ng)
    n_edge = min(ng, N_EDGE)
    assert n_edge % sub == 0
    col, full, bias_spec, vec = _attn_specs(heads, tb, t)

    def body(q_ref, k_ref, v_ref, z_ref, first_ref, rest_ref, qnw_ref, knw_ref, dog_ref,
             dqz_ref, dkv_ref, dbias_ref, dqnw_ref, dknw_ref, kn_scr, v_scr, dkn_scr, dv_scr):
        i = pl.program_id(1)

        @pl.when(i == 0)
        def _():
            _attn_fill(k_ref, v_ref, knw_ref, kn_scr, v_scr, t)
            dkn_scr[...] = jnp.zeros_like(dkn_scr)
            dv_scr[...] = jnp.zeros_like(dv_scr)
            dbias_ref[...] = jnp.zeros_like(dbias_ref)
            dqnw_ref[...] = jnp.zeros_like(dqnw_ref)

        def run(block_bias):
            for g0 in range(0, ng, sub):
                rows = pl.ds(g0 * ROWS, sub * ROWS)
                at = i * tb + g0 * ROWS
                blocked = lambda ref: ref[rows, :].reshape(sub, ROWS, HEAD_DIM)
                dq, dz, dkn, dv, dbias, dqnw = _attn_groups_bwd(
                    blocked(q_ref), blocked(z_ref), _attn_windows(kn_scr, at, sub), _attn_windows(v_scr, at, sub),
                    block_bias(g0), qnw_ref[...], blocked(dog_ref))
                dqz_ref[0, rows, :] = dq.reshape(sub * ROWS, HEAD_DIM).astype(BF16)
                dqz_ref[1, rows, :] = dz.reshape(sub * ROWS, HEAD_DIM).astype(BF16)
                for g in range(sub):
                    window = pl.ds(pl.multiple_of(at + g * ROWS, ROWS), WIN)
                    dkn_scr[window, :] += dkn[g]
                    dv_scr[window, :] += dv[g]
                dbias_ref[0] += dbias
                dqnw_ref[0] += dqnw

        pl.when(i == 0)(lambda: run(lambda g0: first_ref[0, g0:g0 + sub] if g0 + sub <= n_edge else rest_ref[0]))
        pl.when(i > 0)(lambda: run(lambda g0: rest_ref[0]))

        @pl.when(i == nb - 1)
        def _():
            step = min(512, t)

            def finish(j, dknw):
                rows = pl.ds(pl.multiple_of(j * step, step), step)
                prows = pl.ds(pl.multiple_of(PAD + j * step, CHUNK), step)
                _, vjp = jax.vjp(_rms, k_ref[rows, :], knw_ref[...])
                dk, dw = vjp(dkn_scr[prows, :])
                dkv_ref[0, rows, :] = dk.astype(BF16)
                dkv_ref[1, rows, :] = dv_scr[prows, :].astype(BF16)
                return dknw + dw

            dknw_ref[0] = lax.fori_loop(0, t // step, finish, jnp.zeros((1, HEAD_DIM), F32))

    pair_col = pl.BlockSpec((2, tb, HEAD_DIM), lambda h, i: (0, i, h))
    pair_full = pl.BlockSpec((2, t, HEAD_DIM), lambda h, i: (0, 0, h))
    head_vec = pl.BlockSpec((1, 1, HEAD_DIM), lambda h, i: (h, 0, 0))
    pair_shape = jax.ShapeDtypeStruct((2, t, heads * HEAD_DIM), BF16)
    vec_shape = jax.ShapeDtypeStruct((heads, 1, HEAD_DIM), F32)
    return pl.pallas_call(
        body,
        name=name,
        grid=(heads, nb),
        in_specs=[col(0), full(1), full(2), col(3), *bias_spec, vec, vec, pl.BlockSpec((tb, HEAD_DIM), lambda h, i: (i, h))],
        out_specs=[pair_col, pair_full, pl.BlockSpec((1, ROWS, WIN), lambda h, i: (h, 0, 0)), head_vec, head_vec],
        out_shape=[pair_shape, pair_shape, jax.ShapeDtypeStruct((heads, ROWS, WIN), F32), vec_shape, vec_shape],
        scratch_shapes=[pltpu.VMEM((PAD + t, HEAD_DIM), BF16), pltpu.VMEM((PAD + t, HEAD_DIM), BF16),
                        pltpu.VMEM((PAD + t, HEAD_DIM), F32), pltpu.VMEM((PAD + t, HEAD_DIM), F32)],
        compiler_params=_params("arbitrary", "arbitrary"),
    )(proj, proj, proj, proj, *bias, qnw_row, knw_row, dog)


def _lane_row(v):
    v = v.reshape(1, -1)
    return jnp.pad(v, ((0, 0), (0, LANES - v.shape[1])))


def _local_step(x, target, norm_w, wa_in, conv_w, a_log, dt_bias, onw, wa_out, wb_in, qnw, knw, rel_bias, wb_out, *,
                sharded=False):
    ha, hb = a_log.shape[-1], rel_bias.shape[-2]
    na = 4 * ha * HEAD_DIM
    wa_main = wa_in[:, :na]
    wa_ab = jnp.pad(wa_in[:, na:], ((0, 0), (0, LANES - 2 * ha)))
    alog_row, dtb_row, onw_row = _lane_row(a_log), _lane_row(dt_bias), _lane_row(onw)
    qnw_row, knw_row = _lane_row(qnw), _lane_row(knw)
    bias = _masked_bias(_band_bias(rel_bias.reshape(hb, -1)), min(min(ATTN_BLOCK, x.shape[0]) // ROWS, N_EDGE))

    hn0, ab_a = _rmsnorm_fwd(x, norm_w[0:1], wa_ab, name="norm0")
    proj_a = _matmul(hn0, wa_main, name="a_in")
    og_a, states, got = _gdn_fwd(proj_a, ab_a, conv_w, alog_row, dtb_row, onw_row, heads=ha, name="gdn_fwd",
                                 gather=[wb_in, wa_out, wb_out] if sharded else [])
    if sharded:
        wb_in, wa_out, wb_out = _join_cols(got[0]), got[1].reshape(-1, got[1].shape[-1]), got[2].reshape(-1, got[2].shape[-1])
    h1, hn1 = _matmul_norm(og_a, wa_out, x, norm_w[1:2], name="a_out_norm1")
    proj_b = _matmul(hn1, wb_in, name="b_in")
    og_b = _attn_fwd(proj_b, bias, qnw_row, knw_row, heads=hb, name="attn_fwd")
    loss, dh2, dh2_b = _matmul_loss(og_b, wb_out, h1, target, name="b_out_loss")

    grad_dtype = BF16 if sharded else F32
    dog_b = _matmul(dh2_b, wb_out, trans_b=True, name="d_b_out_x")
    dwb_out = _matmul(og_b, dh2_b, trans_a=True, out_dtype=grad_dtype, name="d_b_out_w")
    dqz, dkv, dbias, dqnw, dknw = _attn_bwd(proj_b, bias, qnw_row, knw_row, dog_b, heads=hb, name="attn_bwd")
    dproj_b, qkvz = [dqz, dkv], (0, 3, 1, 2)
    dhn1 = _matmul(dproj_b, wb_in, trans_b=True, order=qkvz, name="d_b_in_x")
    dwb_in = _matmul(hn1, dproj_b, trans_a=True, order=qkvz, out_dtype=grad_dtype, col_slabs=N_CHIPS if sharded else 0,
                     name="d_b_in_w")
    dh1, dh1_b, dnw1 = _rmsnorm_bwd(h1, norm_w[1:2], dhn1, dh2, name="d_norm1")

    dog_a = _matmul(dh1_b, wa_out, trans_b=True, name="d_a_out_x")
    dwa_out = _matmul(og_a, dh1_b, trans_a=True, out_dtype=grad_dtype, name="d_a_out_w")
    early = [dwb_in, _split_rows(dwa_out), _split_rows(dwb_out)] if sharded else []
    dproj_a, dab, dconv, dalog, ddtb, donw, landed = _gdn_bwd(
        proj_a, ab_a, conv_w, alog_row, dtb_row, onw_row, states, dog_a, heads=ha, name="gdn_bwd", exchange=early)
    dab_b = dab.astype(BF16)
    if sharded:
        mine = [_sum_slots(s, name=f"chip_sum_{n}") for n, s in zip(("b_w_in", "a_w_out", "b_w_out"), landed)]
        dwa_main, theirs = _matmul(hn0, dproj_a, trans_a=True, out_dtype=grad_dtype, name="d_a_in_w", exchange=mine,
                                   with_pair=True)
        dwb_in, dwa_out, dwb_out = zip(mine, theirs)
    else:
        dwa_main = _matmul(hn0, dproj_a, trans_a=True, out_dtype=grad_dtype, name="d_a_in_w")
    dwa_in = jnp.concatenate(
        [dwa_main, _matmul(hn0, dab_b, trans_a=True, out_dtype=grad_dtype, name="d_a_in_ab_w")[:, :2 * ha]], axis=1)
    if sharded:
        dhn0, (dwa_in, dconv) = _matmul(dproj_a, wa_main, trans_b=True, name="d_a_in_x",
                                        exchange=[_split_cols(dwa_in), _split_cols(dconv)])
    else:
        dhn0 = _matmul(dproj_a, wa_main, trans_b=True, name="d_a_in_x")
    dx, _, dnw0 = _rmsnorm_bwd(x, norm_w[0:1], dhn0, dh1, narrow=(dab_b, wa_ab), name="d_norm0")

    drel = _band_bias_grad(dbias)
    grads = dict(
        norm_w=jnp.concatenate([dnw0, dnw1], axis=0), a_w_in=dwa_in, a_conv_w=dconv, a_a_log=dalog[:, :ha],
        a_dt_bias=ddtb[:, :ha], a_out_norm_w=donw, a_w_out=dwa_out, b_w_in=dwb_in, b_q_norm_w=jnp.sum(dqnw, axis=0),
        b_k_norm_w=jnp.sum(dknw, axis=0), b_rel_bias=drel[None], b_w_out=dwb_out)
    return loss, dx, grads


_ANY = pl.BlockSpec(memory_space=pl.ANY)
_CHIP_FLIPS = ((1, 0), (0, 1), (1, 1))


def _place():
    x, y, c = lax.axis_index("x"), lax.axis_index("y"), lax.axis_index("c")
    return x, y, c


def _flip(v, bit):
    return 1 - v if bit else v


def _remote(src, dst, send_sem, recv_sem, peer):
    return pltpu.make_async_remote_copy(src_ref=src, dst_ref=dst, send_sem=send_sem, recv_sem=recv_sem, device_id=peer,
                                        device_id_type=MESH)


def _comm_call(body, arrays, out_shapes, n_remote, n_local, name):
    scratch = [pltpu.SemaphoreType.DMA((n_remote,)), pltpu.SemaphoreType.DMA((n_remote,))]
    if n_local:
        scratch.append(pltpu.SemaphoreType.DMA((n_local,)))
    return pl.pallas_call(
        body, name=name, in_specs=[_ANY] * len(arrays), out_specs=[_ANY] * len(out_shapes), out_shape=out_shapes,
        scratch_shapes=scratch)(*arrays)


def _chip_scratch(n):
    return [pltpu.SemaphoreType.DMA((3 * n,)), pltpu.SemaphoreType.DMA((3 * n,)), pltpu.SemaphoreType.DMA((n,))]


def _chip_shapes(gather, arrays):
    return [jax.ShapeDtypeStruct(((N_CHIPS,) + s.shape) if gather else s.shape, s.dtype) for s in arrays]


def _chip_traffic(gather, ins, outs, sems):
    send_sems, recv_sems, local_sems = sems
    x, y, c = _place()
    mine = 2 * x + y
    local, remote, landing = [], [], []
    for a in range(len(ins)):
        local.append(pltpu.make_async_copy(ins[a] if gather else ins[a].at[mine], outs[a].at[mine], local_sems.at[a]))
        for k, (fx, fy) in enumerate(_CHIP_FLIPS):
            peer = (_flip(x, fx), _flip(y, fy), c)
            theirs = 2 * peer[0] + peer[1]
            src = ins[a] if gather else ins[a].at[theirs]
            pair = send_sems.at[3 * a + k], recv_sems.at[3 * a + k]
            remote.append(_remote(src, outs[a].at[mine], *pair, peer))
            landing.append(_remote(src, outs[a].at[theirs], *pair, peer))
    return local + remote, (local, landing, remote)


def _start(traffic):
    for cp in traffic[0]:
        cp.start()


def _finish(traffic):
    local, landing, remote = traffic[1]
    for cp in local:
        cp.wait()
    for cp in landing:
        cp.wait_recv()
    for cp in remote:
        cp.wait_send()


def _pair_scratch(n):
    return [pltpu.SemaphoreType.DMA((n,)), pltpu.SemaphoreType.DMA((n,))]


def _pair_traffic(ins, outs, sems):
    send_sems, recv_sems = sems
    x, y, c = _place()
    copies = [_remote(ins[a], outs[a], send_sems.at[a], recv_sems.at[a], (x, y, 1 - c)) for a in range(len(ins))]
    return copies, ([], copies, copies)


def _with_exchange(compute, n_in, n_out, gather, n_x, grid):
    if not n_x:
        return compute

    def body(*refs):
        ins, x_in = refs[:n_in], refs[n_in:n_in + n_x]
        outs, x_out = refs[n_in + n_x:n_in + n_x + n_out], refs[n_in + n_x + n_out:n_in + 2 * n_x + n_out]
        n_sems = 2 if gather == "pair" else 3
        scratch, sems = refs[n_in + 2 * n_x + n_out:-n_sems], refs[-n_sems:]
        traffic = _pair_traffic(x_in, x_out, sems) if gather == "pair" else _chip_traffic(gather, x_in, x_out, sems)
        first = functools.reduce(jnp.logical_and, [pl.program_id(d) == 0 for d in range(len(grid))])
        last = functools.reduce(jnp.logical_and, [pl.program_id(d) == grid[d] - 1 for d in range(len(grid))])

        @pl.when(first)
        def _():
            _start(traffic)

        compute(*ins, *outs, *scratch)

        @pl.when(last)
        def _():
            _finish(traffic)

    return body


def _gather_shared(shard, small, *, name):
    rows = shard.shape[0]
    assert rows % 2 == 0
    half = rows // 2

    def body(shard_ref, small_ref, out_ref, small_out_ref, send_sems, recv_sems, local_sems):
        x, y, c = _place()
        mine = 2 * x + y
        sibling = (x, y, 1 - c)
        my_rows = pl.ds(pl.multiple_of(c * half, 8), half)
        local = [pltpu.make_async_copy(shard_ref, out_ref.at[mine], local_sems.at[0]),
                 pltpu.make_async_copy(small_ref, small_out_ref.at[mine], local_sems.at[1])]
        sent, landed, passed_on, handed = [], [], [], []
        for k, (fx, fy) in enumerate(_CHIP_FLIPS):
            peer = (_flip(x, fx), _flip(y, fy), c)
            theirs = 2 * peer[0] + peer[1]
            ici, d2d, tiny = [(send_sems.at[3 * n + k], recv_sems.at[3 * n + k]) for n in range(3)]
            sent.append(_remote(shard_ref.at[my_rows], out_ref.at[mine, my_rows], *ici, peer))
            landed.append(_remote(shard_ref.at[my_rows], out_ref.at[theirs, my_rows], *ici, peer))
            sent.append(_remote(small_ref, small_out_ref.at[mine], *tiny, peer))
            landed.append(_remote(small_ref, small_out_ref.at[theirs], *tiny, peer))
            passed_on.append(_remote(out_ref.at[theirs, my_rows], out_ref.at[theirs, my_rows], *d2d, sibling))
            other_rows = pl.ds(pl.multiple_of((1 - c) * half, 8), half)
            handed.append(_remote(out_ref.at[theirs, other_rows], out_ref.at[theirs, other_rows], *d2d, sibling))
        for cp in local + sent:
            cp.start()
        for k in range(3):
            landed[2 * k].wait_recv()
            passed_on[k].start()
        for k in range(3):
            landed[2 * k + 1].wait_recv()
            handed[k].wait_recv()
        for cp in local:
            cp.wait()
        for cp in sent + passed_on:
            cp.wait_send()

    return pl.pallas_call(
        body, name=name, in_specs=[_ANY] * 2, out_specs=[_ANY] * 2, out_shape=_chip_shapes(True, [shard, small]),
        scratch_shapes=[pltpu.SemaphoreType.DMA((9,)), pltpu.SemaphoreType.DMA((9,)), pltpu.SemaphoreType.DMA((2,))],
    )(shard, small)


def _swap_pair(arrays, *, name):
    n = len(arrays)

    def body(*refs):
        ins, outs, (send_sems, recv_sems) = refs[:n], refs[n:2 * n], refs[2 * n:]
        x, y, c = _place()
        copies = [_remote(ins[a], outs[a], send_sems.at[a], recv_sems.at[a], (x, y, 1 - c)) for a in range(n)]
        for cp in copies:
            cp.start()
        for cp in copies:
            cp.wait_recv()
        for cp in copies:
            cp.wait_send()

    shapes = [jax.ShapeDtypeStruct(s.shape, s.dtype) for s in arrays]
    return _comm_call(body, arrays, shapes, n, 0, name)


def _gather_all(tile, *, name):
    def body(in_ref, out_ref, send_sems, recv_sems, local_sems):
        x, y, c = _place()
        mine = 4 * x + 2 * y + c
        local = pltpu.make_async_copy(in_ref, out_ref.at[mine], local_sems.at[0])
        remote, landing = [], []
        for k in range(1, N_DEV):
            peer = (_flip(x, k & 4), _flip(y, k & 2), _flip(c, k & 1))
            sems = send_sems.at[k - 1], recv_sems.at[k - 1]
            remote.append(_remote(in_ref, out_ref.at[mine], *sems, peer))
            landing.append(_remote(in_ref, out_ref.at[4 * peer[0] + 2 * peer[1] + peer[2]], *sems, peer))
        for cp in [local] + remote:
            cp.start()
        local.wait()
        for cp in landing:
            cp.wait_recv()
        for cp in remote:
            cp.wait_send()

    return _comm_call(body, [tile], [jax.ShapeDtypeStruct((N_DEV,) + tile.shape, tile.dtype)], N_DEV - 1, 1, name)[0]


def _sum_slots(slabs, *, name, tr=128):
    s, r, c = slabs.shape
    tr = min(tr, r)

    def body(in_ref, o_ref):
        acc = in_ref[0].astype(F32)
        for j in range(1, s):
            acc = acc + in_ref[j].astype(F32)
        o_ref[...] = acc

    return pl.pallas_call(
        body, name=name, grid=(r // tr,),
        in_specs=[pl.BlockSpec((s, tr, c), lambda i: (0, i, 0))], out_specs=pl.BlockSpec((tr, c), lambda i: (i, 0)),
        out_shape=jax.ShapeDtypeStruct((r, c), F32), compiler_params=_params("parallel"))(slabs)


def _adamw_math(w, g, m, v):
    m = ADAM_B1 * m + (1.0 - ADAM_B1) * g
    v = ADAM_B2 * v + (1.0 - ADAM_B2) * (g * g)
    m_hat = m / (1.0 - ADAM_B1 ** ADAM_STEP)
    v_hat = v / (1.0 - ADAM_B2 ** ADAM_STEP)
    delta = -ADAM_LR * (m_hat / (jnp.sqrt(v_hat) + ADAM_EPS) + ADAM_WD * w)
    return delta, m, v


def _adamw(w, m, v, parts, *, name, tr=128):
    r, c = w.shape
    tr = min(tr, r)
    s = len(parts)

    def body(w_ref, m_ref, v_ref, *refs):
        g_ref, d_ref, nm_ref, nv_ref = refs[s:]
        g = refs[0][...]
        for p_ref in refs[1:s]:
            g = g + p_ref[...]
        g_ref[...] = g
        d_ref[...], nm_ref[...], nv_ref[...] = _adamw_math(w_ref[...], g, m_ref[...], v_ref[...])

    blk = pl.BlockSpec((tr, c), lambda i: (i, 0))
    shape = jax.ShapeDtypeStruct((r, c), F32)
    return pl.pallas_call(
        body, name=name, grid=(r // tr,), in_specs=[blk] * (3 + s), out_specs=[blk] * 4, out_shape=[shape] * 4,
        compiler_params=_params("parallel"))(w, m, v, *parts)


_BIG = ("a_w_in", "b_w_in", "a_w_out", "b_w_out", "a_conv_w")
_SMALL = ("norm_w", "a_a_log", "a_dt_bias", "a_out_norm_w", "b_q_norm_w", "b_k_norm_w", "b_rel_bias")
_ORDER = ("norm_w", "a_w_in", "a_conv_w", "a_a_log", "a_dt_bias", "a_out_norm_w", "a_w_out", "b_w_in", "b_q_norm_w",
          "b_k_norm_w", "b_rel_bias", "b_w_out")


def _join_cols(g):
    return jnp.transpose(g, (1, 0, 2)).reshape(g.shape[1], -1)


def _split_cols(g):
    return jnp.transpose(g.reshape(g.shape[0], N_CHIPS, -1), (1, 0, 2))


def _split_rows(g):
    return g.reshape(N_CHIPS, -1, g.shape[-1])


PACK_ROWS = 8


def _pack(d):
    flat = jnp.concatenate([d[n].reshape(-1) for n in _SMALL])
    return jnp.pad(flat, (0, -flat.shape[0] % (PACK_ROWS * LANES))).reshape(PACK_ROWS, -1)


def _unpack(tile, like):
    flat, out, at = tile.reshape(-1), {}, 0
    for n in _SMALL:
        size = like[n].size
        out[n] = flat[at:at + size].reshape(like[n].shape)
        at += size
    return out


def kernel(x, norm_w, a_w_in, a_conv_w, a_a_log, a_dt_bias, a_out_norm_w, a_w_out, b_w_in, b_q_norm_w, b_k_norm_w, b_rel_bias, b_w_out, loss_target, m_norm_w, m_a_w_in, m_a_conv_w, m_a_a_log, m_a_dt_bias, m_a_out_norm_w, m_a_w_out, m_b_w_in, m_b_q_norm_w, m_b_k_norm_w, m_b_rel_bias, m_b_w_out, v_norm_w, v_a_w_in, v_a_conv_w, v_a_a_log, v_a_dt_bias, v_a_out_norm_w, v_a_w_out, v_b_w_in, v_b_q_norm_w, v_b_k_norm_w, v_b_rel_bias, v_b_w_out):
    w = dict(norm_w=norm_w, a_w_in=a_w_in, a_conv_w=a_conv_w, a_a_log=a_a_log, a_dt_bias=a_dt_bias,
             a_out_norm_w=a_out_norm_w, a_w_out=a_w_out, b_w_in=b_w_in, b_q_norm_w=b_q_norm_w, b_k_norm_w=b_k_norm_w,
             b_rel_bias=b_rel_bias, b_w_out=b_w_out)
    m = dict(norm_w=m_norm_w, a_w_in=m_a_w_in, a_conv_w=m_a_conv_w, a_a_log=m_a_a_log, a_dt_bias=m_a_dt_bias,
             a_out_norm_w=m_a_out_norm_w, a_w_out=m_a_w_out, b_w_in=m_b_w_in, b_q_norm_w=m_b_q_norm_w,
             b_k_norm_w=m_b_k_norm_w, b_rel_bias=m_b_rel_bias, b_w_out=m_b_w_out)
    v = dict(norm_w=v_norm_w, a_w_in=v_a_w_in, a_conv_w=v_a_conv_w, a_a_log=v_a_a_log, a_dt_bias=v_a_dt_bias,
             a_out_norm_w=v_a_out_norm_w, a_w_out=v_a_w_out, b_w_in=v_b_w_in, b_q_norm_w=v_b_q_norm_w,
             b_k_norm_w=v_b_k_norm_w, b_rel_bias=v_b_rel_bias, b_w_out=v_b_w_out)

    wa_in, conv = _gather_shared(a_w_in[0].astype(BF16), a_conv_w[0], name="gather_a_in")
    loss, dx, grads = _local_step(
        x[0], loss_target[0], norm_w, _join_cols(wa_in), _join_cols(conv), a_a_log, a_dt_bias, a_out_norm_w,
        a_w_out[0].astype(BF16), b_w_in[0].astype(BF16), b_q_norm_w, b_k_norm_w, b_rel_bias, b_w_out[0].astype(BF16),
        sharded=True)
    loss = lax.psum(loss, ("x", "y", "c"))

    late = [n for n in _BIG if not isinstance(grads[n], tuple)]
    mine = [_sum_slots(grads[n], name=f"chip_sum_{n}") for n in late]
    sums = {n: grads[n] for n in _BIG if n not in late}
    sums.update(zip(late, zip(mine, _swap_pair(mine, name="pair_grads"))))
    out = {}
    for n in _BIG:
        out[n] = [r[None] for r in _adamw(w[n][0], m[n][0], v[n][0], list(sums[n]), name=f"adamw_{n}")]

    tiles = _gather_all(_pack(grads), name="gather_small_grads")
    res = _adamw(_pack(w), _pack(m), _pack(v), [tiles[d] for d in range(N_DEV)], name="adamw_small")
    unpacked = [_unpack(r, w) for r in res]
    for n in _SMALL:
        out[n] = [u[n] for u in unpacked]

    return (loss, dx[None], *[out[n][0] for n in _ORDER], *[out[n][1] for n in _ORDER], *[out[n][2] for n in _ORDER],
            *[out[n][3] for n in _ORDER])
```

```python
import functools

import numpy as np
import jax
import jax.numpy as jnp
from jax import lax
from jax.experimental import pallas as pl
from jax.experimental.pallas import tpu as pltpu

F32 = jnp.float32
BF16 = jnp.bfloat16

CHUNK = 64
HEAD_DIM = 128
LEFT_CHUNKS = 8
REL_CLIP = 256
CONV_K = 4
EPS = 1e-6
HALO = 8

ADAM_LR = 0.001
ADAM_B1 = 0.9
ADAM_B2 = 0.999
ADAM_EPS = 1e-08
ADAM_WD = 0.01
ADAM_STEP = 10

LANES = 128
N_CHIPS = 4
N_DEV = 8
VMEM_LIMIT_BYTES = 56 * 1024 * 1024
VMEM_LIMIT_WIDE_BYTES = 63 * 1024 * 1024
MESH = pl.DeviceIdType.MESH


def _params(*sem, vmem=VMEM_LIMIT_BYTES):
    return pltpu.CompilerParams(dimension_semantics=sem, vmem_limit_bytes=vmem)


def _dot(a, b, dims=(((1,), (0,)), ((), ())), precision=None):
    return lax.dot_general(a, b, dims, precision=precision, preferred_element_type=F32)


_NT = (((1,), (1,)), ((), ()))
_TN = (((0,), (0,)), ((), ()))


def _bdot(a, b, dims=(((1,), (0,)), ((), ()))):
    return _dot(a.astype(BF16), b.astype(BF16), dims)


def _fdot(a, b, dims=(((1,), (0,)), ((), ()))):
    return _dot(a, b, dims, precision=lax.Precision.HIGH)


def _silu(x):
    return x * jax.nn.sigmoid(x)


def _stacks(x):
    if not isinstance(x, (list, tuple)) and x.ndim != 3:
        return None
    arrays = list(x) if isinstance(x, (list, tuple)) else [x]
    assert len({(v.shape[1], v.shape[2], v.dtype) for v in arrays}) == 1
    starts = [sum(v.shape[0] for v in arrays[:r]) for r in range(len(arrays))]
    return arrays, starts, starts[-1] + arrays[-1].shape[0]


def _static_pick(table, index):
    out = table[-1]
    for s in range(len(table) - 2, -1, -1):
        out = jnp.where(index == s, table[s], out)
    return out


def _matmul(a, b, *, name, trans_a=False, trans_b=False, residual=None, out_dtype=F32, tm=1024, tn=1024, tk=2048,
            col_slabs=0, order=None, n_cols=0, exchange=(), with_pair=False):
    assert not (trans_a and trans_b)
    a_stack, b_stack = _stacks(a), _stacks(b)
    assert not (a_stack and (trans_a or b_stack)) and not (b_stack and trans_b)
    a_list, b_list = (a_stack[0] if a_stack else [a]), (b_stack[0] if b_stack else [b])
    a0, b0 = a_list[0], b_list[0]
    k, m = (a_stack[2] * a0.shape[2], a0.shape[1]) if a_stack else a.shape if trans_a else a.shape[::-1]
    n = n_cols or (b_stack[2] * b0.shape[2] if b_stack else b.shape[0] if trans_b else b.shape[1])
    tm, tn, tk = min(tm, m), min(tn, n // max(col_slabs, 1)), min(tk, k)
    if a_stack:
        tk = min(tk, a0.shape[2])
        per_k = a0.shape[2] // tk
    if b_stack:
        tn = min(tn, b0.shape[2])
        per_n = b0.shape[2] // tn
    assert m % tm == 0 and n % tn == 0 and k % tk == 0, (a0.shape, b0.shape, tm, tn, tk)
    nk = k // tk
    dims = _NT if trans_b else _TN if trans_a else (((1,), (0,)), ((), ()))
    order = list(order) if order is not None else list(range(max(a_stack[2] if a_stack else 0, b_stack[2] if b_stack else 0)))
    na, nb = len(a_list), len(b_list)

    def group_of(r, stack, position):
        arrays, starts, _ = stack
        local = position - starts[r]
        return jnp.logical_and(local >= 0, local < arrays[r].shape[0]), jnp.clip(local, 0, arrays[r].shape[0] - 1)

    def body(*refs):
        a_refs, b_refs = refs[:na], refs[na:na + nb]
        r_ref = refs[na + nb] if residual is not None else None
        o_ref, acc_ref = refs[-2:]
        j, kk = pl.program_id(1), pl.program_id(2)

        @pl.when(kk == 0)
        def _():
            acc_ref[...] = jnp.zeros_like(acc_ref)

        for ra, a_ref in enumerate(a_refs):
            for rb, b_ref in enumerate(b_refs):
                def add(a_ref=a_ref, b_ref=b_ref):
                    acc_ref[...] += _dot(a_ref[...], b_ref[...], dims)

                if na > 1:
                    pl.when(group_of(ra, a_stack, kk // per_k)[0])(add)
                elif nb > 1:
                    pl.when(group_of(rb, b_stack, j // per_n)[0])(add)
                else:
                    add()

        @pl.when(kk == nk - 1)
        def _():
            r = acc_ref[...]
            if r_ref is not None:
                r = r + r_ref[...]
            o_ref[...] = r.astype(o_ref.dtype)

    if a_stack:
        a_specs = [pl.BlockSpec((None, tm, tk), lambda i, j, kk, r=r: (group_of(r, a_stack, kk // per_k)[1], i, kk % per_k))
                   for r in range(na)]
        b_k = lambda kk: _static_pick(order, kk // per_k) * per_k + kk % per_k
    else:
        a_specs = [pl.BlockSpec((tk, tm), lambda i, j, kk: (kk, i)) if trans_a else pl.BlockSpec((tm, tk), lambda i, j, kk: (i, kk))]
        b_k = lambda kk: kk
    if b_stack:
        b_specs = [pl.BlockSpec((None, tk, tn), lambda i, j, kk, r=r: (group_of(r, b_stack, j // per_n)[1], kk, j % per_n))
                   for r in range(nb)]
        out_col = lambda j: _static_pick(order, j // per_n) * per_n + j % per_n
    else:
        b_specs = [pl.BlockSpec((tn, tk), lambda i, j, kk: (j, b_k(kk))) if trans_b
                   else pl.BlockSpec((tk, tn), lambda i, j, kk: (b_k(kk), j))]
        out_col = lambda j: j
    in_specs = a_specs + b_specs
    args = a_list + b_list
    if residual is not None:
        in_specs.append(pl.BlockSpec((tm, tn), lambda i, j, kk: (i, j)))
        args.append(residual)
    grid = (m // tm, n // tn, nk)
    n_x = len(exchange)
    if col_slabs:
        per = n // col_slabs // tn
        assert per * tn * col_slabs == n, (n, tn, col_slabs)
        out_spec = pl.BlockSpec((None, tm, tn), lambda i, j, kk: (out_col(j) // per, i, out_col(j) % per))
        out_shape = jax.ShapeDtypeStruct((col_slabs, m, n // col_slabs), out_dtype)
    else:
        out_spec = pl.BlockSpec((tm, tn), lambda i, j, kk: (i, out_col(j)))
        out_shape = jax.ShapeDtypeStruct((m, n), out_dtype)
    out, *landed = pl.pallas_call(
        _with_exchange(body, len(args), 1, "pair" if with_pair else False, n_x, grid),
        name=name,
        grid=grid,
        in_specs=in_specs + [_ANY] * n_x,
        out_specs=[out_spec] + [_ANY] * n_x,
        out_shape=[out_shape] + _chip_shapes(False, exchange),
        scratch_shapes=[pltpu.VMEM((tm, tn), F32)] + ((_pair_scratch if with_pair else _chip_scratch)(n_x) if n_x else []),
        compiler_params=_params(*(("arbitrary",) * 3 if n_x else ("parallel", "parallel", "arbitrary"))),
    )(*args, *exchange)
    return (out, landed) if n_x else out


def _rms(x, w):
    return x * lax.rsqrt(jnp.mean(x * x, axis=-1, keepdims=True) + EPS) * w


def _rmsnorm_fwd(x, w_row, narrow_w, *, name, tr=512):
    t, d = x.shape
    tr = min(tr, t)

    def body(x_ref, w_ref, nw_ref, o_ref, narrow_ref):
        hn = _rms(x_ref[...], w_ref[...]).astype(BF16)
        o_ref[...] = hn
        narrow_ref[...] = _dot(hn, nw_ref[...])

    return pl.pallas_call(
        body,
        name=name,
        grid=(t // tr,),
        in_specs=[pl.BlockSpec((tr, d), lambda i: (i, 0)), pl.BlockSpec((1, d), lambda i: (0, 0)),
                  pl.BlockSpec((d, LANES), lambda i: (0, 0))],
        out_specs=[pl.BlockSpec((tr, d), lambda i: (i, 0)), pl.BlockSpec((tr, LANES), lambda i: (i, 0))],
        out_shape=[jax.ShapeDtypeStruct((t, d), BF16), jax.ShapeDtypeStruct((t, LANES), F32)],
        compiler_params=_params("parallel"),
    )(x, w_row, narrow_w)


def _matmul_norm(a, b, residual, w_row, *, name, tm=512):
    t, k = a.shape
    d = b.shape[1]
    tm = min(tm, t)

    def body(a_ref, b_ref, r_ref, w_ref, h_ref, hn_ref):
        h = _dot(a_ref[...], b_ref[...]) + r_ref[...]
        h_ref[...] = h
        hn_ref[...] = _rms(h, w_ref[...]).astype(BF16)

    row = pl.BlockSpec((tm, d), lambda i: (i, 0))
    return pl.pallas_call(
        body,
        name=name,
        grid=(t // tm,),
        in_specs=[pl.BlockSpec((tm, k), lambda i: (i, 0)), pl.BlockSpec((k, d), lambda i: (0, 0)), row,
                  pl.BlockSpec((1, d), lambda i: (0, 0))],
        out_specs=[row, row],
        out_shape=[jax.ShapeDtypeStruct((t, d), F32), jax.ShapeDtypeStruct((t, d), BF16)],
        compiler_params=_params("parallel"),
    )(a, b, residual, w_row)


def _rmsnorm_bwd(x, w_row, dy, dres, *, name, tr=256, narrow=None):
    t, d = x.shape
    tr = min(tr, t)
    extra = list(narrow) if narrow is not None else []

    def body(x_ref, w_ref, dy_ref, dres_ref, *refs):
        dx_ref, dxb_ref, dw_ref = refs[len(extra):]

        @pl.when(pl.program_id(0) == 0)
        def _():
            dw_ref[...] = jnp.zeros_like(dw_ref)

        dy = dy_ref[...]
        if extra:
            dy = dy + _dot(refs[0][...], refs[1][...], _NT)
        _, vjp = jax.vjp(_rms, x_ref[...], w_ref[...])
        dx, dw = vjp(dy)
        dx = dx + dres_ref[...]
        dx_ref[...] = dx
        dxb_ref[...] = dx.astype(BF16)
        dw_ref[...] += dw

    row = pl.BlockSpec((tr, d), lambda i: (i, 0))
    vec = pl.BlockSpec((1, d), lambda i: (0, 0))
    extra_specs = [pl.BlockSpec((tr, LANES), lambda i: (i, 0)), pl.BlockSpec((d, LANES), lambda i: (0, 0))] if extra else []
    return pl.pallas_call(
        body,
        name=name,
        grid=(t // tr,),
        in_specs=[row, vec, row, row] + extra_specs,
        out_specs=[row, row, vec],
        out_shape=[jax.ShapeDtypeStruct((t, d), F32), jax.ShapeDtypeStruct((t, d), BF16), jax.ShapeDtypeStruct((1, d), F32)],
        compiler_params=_params("arbitrary"),
    )(x, w_row, dy, dres, *extra)


def _matmul_loss(a, b, residual, target, *, name, tm=512, tn=1024):
    t, k = a.shape
    d = b.shape[1]
    tm, tn = min(tm, t), min(tn, d)

    def body(a_ref, b_ref, r_ref, t_ref, dh_ref, dhb_ref, part_ref):
        @pl.when(pl.program_id(1) == 0)
        def _():
            part_ref[...] = jnp.zeros_like(part_ref)

        err = _dot(a_ref[...], b_ref[...]) + r_ref[...] - t_ref[...]
        dh = err * (1.0 / d)
        dh_ref[...] = dh
        dhb_ref[...] = dh.astype(BF16)
        part_ref[...] += jnp.sum(err * err, axis=0, keepdims=True)

    tile = pl.BlockSpec((tm, tn), lambda j, i: (i, j))
    dh, dhb, part = pl.pallas_call(
        body,
        name=name,
        grid=(d // tn, t // tm),
        in_specs=[pl.BlockSpec((tm, k), lambda j, i: (i, 0)), pl.BlockSpec((k, tn), lambda j, i: (0, j)), tile, tile],
        out_specs=[tile, tile, pl.BlockSpec((1, tn), lambda j, i: (0, j))],
        out_shape=[jax.ShapeDtypeStruct((t, d), F32), jax.ShapeDtypeStruct((t, d), BF16), jax.ShapeDtypeStruct((1, d), F32)],
        compiler_params=_params("arbitrary", "arbitrary"),
    )(a, b, residual, target)
    return 0.5 / d * jnp.sum(part), dh, dhb


_BNN = (((2,), (1,)), ((0,), (0,)))
_BNT = (((2,), (2,)), ((0,), (0,)))
_BTN = (((1,), (1,)), ((0,), (0,)))


_TAP0 = HALO - (CONV_K - 1)


def _conv(x_ref, w, rows):
    c = w[0:1, :] * x_ref[_TAP0:_TAP0 + rows, :]
    for j in range(1, CONV_K):
        c = c + w[j:j + 1, :] * x_ref[_TAP0 + j:_TAP0 + j + rows, :]
    return c


def _conv_silu_bwd(x_ref, w, dact, dc_ref, rows):
    c = _conv(x_ref, w, rows)
    sig = jax.nn.sigmoid(c)
    dc = dact * (sig * (1.0 + c * (1.0 - sig)))
    dw = [jnp.sum(dc * x_ref[_TAP0 + j:_TAP0 + j + rows, :], axis=0, keepdims=True) for j in range(CONV_K)]
    dc_ref[0:HALO, :] = jnp.zeros((HALO, HEAD_DIM), F32)
    dc_ref[HALO:HALO + rows, :] = dc
    dc_ref[HALO + rows:HALO + rows + HALO, :] = jnp.zeros((HALO, HEAD_DIM), F32)
    first = HALO - _TAP0
    dx = w[0:1, :] * dc_ref[first:first + HALO + rows, :]
    for j in range(1, CONV_K):
        dx = dx + w[j:j + 1, :] * dc_ref[first - j:first - j + HALO + rows, :]
    return dx, dw


@jax.custom_vjp
def _unit_lower_inverse(neg_l):
    n = neg_l.shape[0]
    eye = (lax.broadcasted_iota(jnp.int32, (n, CHUNK, CHUNK), 1) == lax.broadcasted_iota(jnp.int32, (n, CHUNK, CHUNK), 2))
    inv = eye.astype(F32) + neg_l
    power = _bdot(neg_l, neg_l, _BNN)
    for _ in range(4):
        both = _bdot(jnp.concatenate([inv, power], axis=1), power, _BNN)
        inv, power = inv + both[:, :CHUNK], both[:, CHUNK:]
    return inv + _bdot(inv, power, _BNN)


def _unit_lower_inverse_fwd(neg_l):
    inv = _unit_lower_inverse(neg_l)
    return inv, inv


def _unit_lower_inverse_bwd(inv, dinv):
    return (_fdot(_fdot(inv, dinv, _BTN), inv, _BNT),)


_unit_lower_inverse.defvjp(_unit_lower_inverse_fwd, _unit_lower_inverse_bwd)


def _gdn_intra(qt, kt, v, a, b, alog, dtb):
    n = a.shape[0] // CHUNK
    q = qt * lax.rsqrt(jnp.sum(qt * qt, axis=-1, keepdims=True) + EPS) * (HEAD_DIM ** -0.5)
    k = kt * lax.rsqrt(jnp.sum(kt * kt, axis=-1, keepdims=True) + EPS)
    lanes = jnp.ones((1, HEAD_DIM), F32)
    beta = jax.nn.sigmoid(b) * lanes
    sp = a + dtb
    g = (-jnp.exp(alog) * (jnp.maximum(sp, 0.0) + jnp.log(1.0 + jnp.exp(-jnp.abs(sp))))) * lanes
    q, k, v, beta, g = (t.reshape(n, CHUNK, HEAD_DIM) for t in (q, k, v, beta, g))

    row = lax.broadcasted_iota(jnp.int32, (n, CHUNK, CHUNK), 1)
    col = lax.broadcasted_iota(jnp.int32, (n, CHUNK, CHUNK), 2)
    tri_incl = row >= col
    tri_strict = row > col
    gc = _fdot(tri_incl.astype(F32), g, _BNN)
    gc_row = _fdot(g[:, :, :CHUNK], (row <= col).astype(F32), _BTN)
    decay = jnp.exp(jnp.where(tri_incl, gc[:, :, :CHUNK] - gc_row, -1e30))
    kb = k * beta
    vb = v * beta
    with_k = _bdot(jnp.concatenate([kb, q], axis=1), k, _BNT)
    neg_l = jnp.where(tri_strict, -(with_k[:, :CHUNK] * decay), 0.0)
    qk = jnp.where(tri_incl, with_k[:, CHUNK:] * decay, 0.0)
    inv = _unit_lower_inverse(neg_l)
    e = jnp.exp(gc)
    solved = _bdot(inv, jnp.concatenate([kb * e, vb], axis=2), _BNN)
    g_last = gc[:, CHUNK - 1:CHUNK, :]
    k_dec = k * jnp.exp(g_last - gc)
    from_k = _bdot(k_dec, solved, _BTN)
    from_qk = _bdot(qk, solved, _BNN)
    step, add = -from_k[:, :, :HEAD_DIM], from_k[:, :, HEAD_DIM:]
    read, out = q * e - from_qk[:, :, :HEAD_DIM], from_qk[:, :, HEAD_DIM:]
    return step, add, jnp.exp(g_last), read, out


def _gdn_scan_step(state, step, add, decay_last):
    return state * decay_last + _bdot(step, state) + add


def _gdn_outputs(states, read, out, z, onw):
    return _rms(_bdot(read, states, _BNN) + out, onw) * _silu(z)


def _scan_scratch(n, dtype):
    return [pltpu.VMEM((n, HEAD_DIM, HEAD_DIM), dtype), pltpu.VMEM((n, HEAD_DIM, HEAD_DIM), F32), pltpu.VMEM((n, 1, HEAD_DIM), F32)]


def _head_lane(h, offset=0):
    return lax.broadcasted_iota(jnp.int32, (1, LANES), 1) == h + offset


def _pick(mask, x):
    return jnp.sum(jnp.where(mask, x, 0.0), axis=1, keepdims=True)


def _gdn_specs(heads, tb, rev, nb, PAIR):
    assert heads % PAIR == 0
    blk = (lambda i: nb - 1 - i) if rev else (lambda i: i)
    hb = tb // HALO
    width, pairs = PAIR * HEAD_DIM, heads // PAIR

    def col(group):
        return pl.BlockSpec((tb, width), lambda i, h: (blk(i), group * pairs + h))

    def halo(group):
        return pl.BlockSpec((HALO, width), lambda i, h: (jnp.maximum(blk(i) * hb - 1, 0), group * pairs + h))

    def convw(group):
        return pl.BlockSpec((CONV_K, width), lambda i, h: (0, group * pairs + h))

    vec = pl.BlockSpec((1, LANES), lambda i, h: (0, 0))
    ab = pl.BlockSpec((tb, LANES), lambda i, h: (blk(i), 0))
    states = pl.BlockSpec((PAIR, tb // CHUNK, HEAD_DIM, HEAD_DIM), lambda i, h: (h, blk(i), 0, 0))
    return blk, col, halo, convw, vec, ab, states


def _head_cols(p):
    return slice(p * HEAD_DIM, (p + 1) * HEAD_DIM)


def _gdn_fwd(proj, ab, conv_w, alog_row, dtb_row, onw_row, *, heads, name, tb=1024, pair=4, gather=()):
    t = proj.shape[0]
    tb = min(tb, t)
    nb, cpb = t // tb, tb // CHUNK
    PAIR = min(pair, heads)
    _, col, halo, convw, vec, abspec, states = _gdn_specs(heads, tb, False, nb, PAIR)

    def body(q_ref, k_ref, v_ref, qh_ref, kh_ref, vh_ref, z_ref, ab_ref, wq_ref, wk_ref, wv_ref, alog_ref, dtb_ref, onw_ref,
             og_ref, st_ref, state_scr, x_scr, *op_scr):
        i, pair = pl.program_id(0), pl.program_id(1)
        abv = ab_ref[...]
        heads_here, later = [pair * PAIR + p for p in range(PAIR)], []
        for p, h in enumerate(heads_here):
            cols = _head_cols(p)
            for n, (ref, href) in enumerate(((q_ref, qh_ref), (k_ref, kh_ref), (v_ref, vh_ref))):
                x_scr[p, n, 0:HALO, :] = jnp.where(i > 0, href[:, cols], 0.0)
                x_scr[p, n, HALO:HALO + tb, :] = ref[:, cols]
            sel_a, sel_b = _head_lane(h), _head_lane(h, heads)
            alog, dtb = _pick(sel_a, alog_ref[...]), _pick(sel_a, dtb_ref[...])
            acts = [_silu(_conv(x_scr.at[p, n], w_ref[:, cols], tb)) for n, w_ref in enumerate((wq_ref, wk_ref, wv_ref))]
            *scan, read, out = _gdn_intra(*acts, _pick(sel_a, abv), _pick(sel_b, abv), alog, dtb)
            for scr, val in zip(op_scr[3 * p:3 * p + 3], scan):
                scr[...] = val.astype(scr.dtype)
            later.append((read, out))

        def chunk(c, states):
            for p in range(PAIR):
                st_ref[p, c] = states[p]
            return tuple(_gdn_scan_step(states[p], *[scr[c] for scr in op_scr[3 * p:3 * p + 3]]) for p in range(PAIR))

        @pl.when(i == 0)
        def _():
            for h in heads_here:
                state_scr[h] = jnp.zeros((HEAD_DIM, HEAD_DIM), F32)

        last = lax.fori_loop(0, cpb, chunk, tuple(state_scr[h] for h in heads_here))
        for p, h in enumerate(heads_here):
            cols = _head_cols(p)
            state_scr[h] = last[p]
            og = _gdn_outputs(st_ref[p], *later[p], z_ref[:, cols].reshape(cpb, CHUNK, HEAD_DIM), onw_ref[...])
            og_ref[:, cols] = og.reshape(tb, HEAD_DIM).astype(BF16)

    n_x = len(gather)
    grid = (nb, heads // PAIR)
    og, st, *gathered = pl.pallas_call(
        _with_exchange(body, 14, 2, True, n_x, grid),
        name=name,
        grid=grid,
        in_specs=[col(0), col(1), col(2), halo(0), halo(1), halo(2), col(3), abspec, convw(0), convw(1), convw(2), vec, vec, vec]
        + [_ANY] * n_x,
        out_specs=[pl.BlockSpec((tb, PAIR * HEAD_DIM), lambda i, h: (i, h)), states] + [_ANY] * n_x,
        out_shape=[jax.ShapeDtypeStruct((t, heads * HEAD_DIM), BF16),
                   jax.ShapeDtypeStruct((heads, t // CHUNK, HEAD_DIM, HEAD_DIM), F32)] + _chip_shapes(True, gather),
        scratch_shapes=[pltpu.VMEM((heads, HEAD_DIM, HEAD_DIM), F32), pltpu.VMEM((PAIR, 3, HALO + tb, HEAD_DIM), F32)]
        + _scan_scratch(cpb, BF16) * PAIR + (_chip_scratch(n_x) if n_x else []),
        compiler_params=_params("arbitrary", "arbitrary"),
    )(proj, proj, proj, proj, proj, proj, proj, ab, conv_w, conv_w, conv_w, alog_row, dtb_row, onw_row, *gather)
    return og, st, gathered


def _gdn_bwd(proj, ab, conv_w, alog_row, dtb_row, onw_row, states, dog, *, heads, name, tb=1024, pair=2, exchange=()):
    t = proj.shape[0]
    tb = min(tb, t)
    nb, cpb = t // tb, tb // CHUNK
    PAIR = min(pair, heads)
    _, col, halo, convw, vec, abspec, states_spec = _gdn_specs(heads, tb, True, nb, PAIR)
    n_conv = conv_w.shape[1]

    def body(q_ref, k_ref, v_ref, qh_ref, kh_ref, vh_ref, z_ref, ab_ref, wq_ref, wk_ref, wv_ref, alog_ref, dtb_ref, onw_ref,
             st_ref, dog_ref, dproj_ref, dab_ref, dconv_ref, dalog_ref, ddtb_ref, donw_ref,
             dstate_scr, x_scr, carry_scr, *scr):
        op_scr, dop_scr, dstates_scr, dc_scr = scr[:3 * PAIR], scr[3 * PAIR:6 * PAIR], scr[6 * PAIR:7 * PAIR], scr[7 * PAIR]
        i, pair = pl.program_id(0), pl.program_id(1)
        first_block = i == nb - 1
        heads_here, later = [pair * PAIR + p for p in range(PAIR)], []

        @pl.when(jnp.logical_and(i == 0, pair == 0))
        def _():
            dconv_ref[...] = jnp.zeros_like(dconv_ref)
            dalog_ref[...] = jnp.zeros_like(dalog_ref)
            ddtb_ref[...] = jnp.zeros_like(ddtb_ref)
            donw_ref[...] = jnp.zeros_like(donw_ref)

        @pl.when(pair == 0)
        def _():
            dab_ref[...] = jnp.zeros_like(dab_ref)

        @pl.when(i == 0)
        def _():
            for h in heads_here:
                dstate_scr[h] = jnp.zeros((HEAD_DIM, HEAD_DIM), F32)
                carry_scr[h] = jnp.zeros((3, HALO, HEAD_DIM), F32)

        abv = ab_ref[...]
        w_refs = (wq_ref, wk_ref, wv_ref)
        for p, h in enumerate(heads_here):
            cols = _head_cols(p)
            for n, (ref, href) in enumerate(((q_ref, qh_ref), (k_ref, kh_ref), (v_ref, vh_ref))):
                x_scr[p, n, 0:HALO, :] = jnp.where(first_block, 0.0, href[:, cols])
                x_scr[p, n, HALO:HALO + tb, :] = ref[:, cols]
            sel_a, sel_b = _head_lane(h), _head_lane(h, heads)
            alog, dtb = _pick(sel_a, alog_ref[...]), _pick(sel_a, dtb_ref[...])
            acts = [_silu(_conv(x_scr.at[p, n], w_ref[:, cols], tb)) for n, w_ref in enumerate(w_refs)]
            (*scan, read, out), vjp_intra = jax.vjp(_gdn_intra, *acts, _pick(sel_a, abv), _pick(sel_b, abv), alog, dtb)
            for s, val in zip(op_scr[3 * p:3 * p + 3], scan):
                s[...] = val.astype(s.dtype)
            blocked = lambda ref: ref[:, cols].reshape(cpb, CHUNK, HEAD_DIM)
            _, vjp_outputs = jax.vjp(_gdn_outputs, st_ref[p], read, out, blocked(z_ref), onw_ref[...])
            dstates_scr[p][...], dread, dout, dz, donw = vjp_outputs(blocked(dog_ref))
            dproj_ref[3, :, cols] = dz.reshape(tb, HEAD_DIM).astype(BF16)
            donw_ref[...] += donw
            later.append((vjp_intra, dread, dout, sel_a, sel_b))

        def chunk(i_rev, dstates):
            c = cpb - 1 - i_rev
            new = []
            for p in range(PAIR):
                _, vjp = jax.vjp(_gdn_scan_step, st_ref[p, c], *[s[c].astype(F32) for s in op_scr[3 * p:3 * p + 3]])
                dstate, *grads = vjp(dstates[p])
                for s, val in zip(dop_scr[3 * p:3 * p + 3], grads):
                    s[c] = val
                new.append(dstate + dstates_scr[p][c])
            return tuple(new)

        last = lax.fori_loop(0, cpb, chunk, tuple(dstate_scr[h] for h in heads_here))
        for p, h in enumerate(heads_here):
            cols = _head_cols(p)
            vjp_intra, dread, dout, sel_a, sel_b = later[p]
            dstate_scr[h] = last[p]
            *dacts, da, db, dalog, ddtb = vjp_intra((*[s[...] for s in dop_scr[3 * p:3 * p + 3]], dread, dout))
            dab_ref[...] += jnp.where(sel_a, da, 0.0) + jnp.where(sel_b, db, 0.0)
            for n, (dact, w_ref) in enumerate(zip(dacts, w_refs)):
                dx, dw = _conv_silu_bwd(x_scr.at[p, n], w_ref[:, cols], dact, dc_scr, tb)
                x_scr[p, n] = dx
                x_scr[p, n, tb:tb + HALO, :] += carry_scr[h, n]
                carry_scr[h, n] = x_scr[p, n, 0:HALO, :]
                dproj_ref[n, :, cols] = x_scr[p, n, HALO:HALO + tb, :].astype(BF16)
                lanes = pl.ds(pl.multiple_of((n * heads + h) * HEAD_DIM, HEAD_DIM), HEAD_DIM)
                for j in range(CONV_K):
                    dconv_ref[j:j + 1, lanes] += dw[j]
            dalog_ref[...] += jnp.where(sel_a, dalog, 0.0)
            ddtb_ref[...] += jnp.where(sel_a, ddtb, 0.0)

    dog_spec = pl.BlockSpec((tb, PAIR * HEAD_DIM), lambda i, h: (nb - 1 - i, h))
    dproj_spec = pl.BlockSpec((4, tb, PAIR * HEAD_DIM), lambda i, h: (0, nb - 1 - i, h))
    row_shape = jax.ShapeDtypeStruct((1, LANES), F32)
    n_x = len(exchange)
    grid = (nb, heads // PAIR)
    outs = pl.pallas_call(
        _with_exchange(body, 16, 6, False, n_x, grid),
        name=name,
        grid=grid,
        in_specs=[col(0), col(1), col(2), halo(0), halo(1), halo(2), col(3), abspec, convw(0), convw(1), convw(2), vec, vec, vec,
                  states_spec, dog_spec] + [_ANY] * n_x,
        out_specs=[dproj_spec, abspec, pl.BlockSpec((CONV_K, n_conv), lambda i, h: (0, 0)), vec, vec, vec] + [_ANY] * n_x,
        out_shape=[jax.ShapeDtypeStruct((4, t, heads * HEAD_DIM), BF16), jax.ShapeDtypeStruct((t, LANES), F32),
                   jax.ShapeDtypeStruct((CONV_K, n_conv), F32), row_shape, row_shape, row_shape] + _chip_shapes(False, exchange),
        scratch_shapes=[pltpu.VMEM((heads, HEAD_DIM, HEAD_DIM), F32), pltpu.VMEM((PAIR, 3, HALO + tb, HEAD_DIM), F32),
                        pltpu.VMEM((heads, 3, HALO, HEAD_DIM), F32)] + _scan_scratch(cpb, BF16) * PAIR
        + _scan_scratch(cpb, F32) * PAIR + [pltpu.VMEM((cpb, HEAD_DIM, HEAD_DIM), F32)] * PAIR
        + [pltpu.VMEM((HALO + tb + HALO, HEAD_DIM), F32)]
        + (_chip_scratch(n_x) if n_x else []),
        compiler_params=_params("arbitrary", "arbitrary", vmem=VMEM_LIMIT_WIDE_BYTES),
    )(proj, proj, proj, proj, proj, proj, proj, ab, conv_w, conv_w, conv_w, alog_row, dtb_row, onw_row, states, dog, *exchange)
    return (*outs[:6], outs[6:])


BAND = (LEFT_CHUNKS + 1) * CHUNK
PAD = LEFT_CHUNKS * CHUNK
GROUP = 2
ROWS = GROUP * CHUNK
WIN = (LEFT_CHUNKS + GROUP) * CHUNK
DIAGS = WIN + ROWS - 1
NEAR = PAD + ROWS - 1 - REL_CLIP
assert 0 < NEAR < DIAGS and WIN - PAD - 1 <= REL_CLIP and WIN % LANES == 0
ATTN_BLOCK = 1024
N_EDGE = PAD // ROWS


def _band_bias(rel_bias):
    heads = rel_bias.shape[0]
    far = jnp.broadcast_to(rel_bias[:, 2 * REL_CLIP:], (heads, NEAR + 1))
    near = rel_bias[:, 2 * REL_CLIP + NEAR + 1 - DIAGS:2 * REL_CLIP][:, ::-1]
    diag = jnp.concatenate([far, near], axis=1)
    return jnp.stack([diag[:, ROWS - 1 - r:ROWS - 1 - r + WIN] for r in range(ROWS)], axis=1)


def _band_bias_grad(dbias):
    heads = dbias.shape[0]
    diag = sum(jnp.pad(dbias[:, r, :], ((0, 0), (ROWS - 1 - r, r))) for r in range(ROWS))
    far = jnp.sum(diag[:, :NEAR + 1], axis=1, keepdims=True)
    near = diag[:, NEAR + 1:][:, ::-1]
    unused = jnp.zeros((heads, 2 * REL_CLIP - near.shape[1]), F32)
    return jnp.concatenate([unused, near, far], axis=1)


def _masked_bias(bias, n):
    r = np.arange(ROWS)[:, None]
    key = np.arange(WIN)[None, :]
    band_start = (r // CHUNK) * CHUNK
    in_band = np.logical_and(key >= band_start, key < band_start + BAND)
    in_sequence = key[None] >= PAD - np.arange(n)[:, None, None] * ROWS
    first = jnp.where(np.logical_and(in_band[None], in_sequence)[None], bias[:, None], -1e30)
    return first, jnp.where(in_band[None, None], bias[:, None], -1e30)


def _attn_groups(q_pre, z, kn, v, bias, qnw):
    q = _rms(q_pre, qnw)
    s = _bdot(q, kn, _BNT) * (HEAD_DIM ** -0.5) + bias
    p = jnp.exp(s - jnp.max(s, axis=-1, keepdims=True))
    p = p / jnp.sum(p, axis=-1, keepdims=True)
    return _bdot(p, v, _BNN) * _silu(z)


def _attn_groups_bwd(q_pre, z, kn, v, bias, qnw, dog):
    scale = HEAD_DIM ** -0.5
    inv_rms = lax.rsqrt(jnp.mean(q_pre * q_pre, axis=-1, keepdims=True) + EPS)
    q_hat = q_pre * inv_rms
    q_b = (q_hat * qnw).astype(BF16)
    s = _dot(q_b, kn, _BNT) * scale + bias
    e = jnp.exp(s - jnp.max(s, axis=-1, keepdims=True))
    p = e * (1.0 / jnp.sum(e, axis=-1, keepdims=True))
    p_b = p.astype(BF16)
    o = _dot(p_b, v, _BNN)
    sig = jax.nn.sigmoid(z)
    do = dog * (z * sig)
    dz = dog * o * (sig * (1.0 + z * (1.0 - sig)))
    do_b = do.astype(BF16)
    dv = _dot(p_b, do_b, _BTN)
    dp = _dot(do_b, v, _BNT)
    ds = p * (dp - jnp.sum(do * o, axis=-1, keepdims=True))
    ds_b = (ds * scale).astype(BF16)
    dq = _dot(ds_b, kn, _BNN)
    dkn = _dot(ds_b, q_b, _BTN)
    dqnw = jnp.sum(jnp.sum(dq * q_hat, axis=0), axis=0, keepdims=True)
    dq_hat = dq * qnw
    dq_pre = inv_rms * (dq_hat - q_hat * jnp.mean(dq_hat * q_hat, axis=-1, keepdims=True))
    return dq_pre, dz, dkn, dv, jnp.sum(ds, axis=0), dqnw


def _attn_specs(heads, tb, t):
    def col(group):
        return pl.BlockSpec((tb, HEAD_DIM), lambda h, i: (i, group * heads + h))

    def full(group):
        return pl.BlockSpec((t, HEAD_DIM), lambda h, i: (0, group * heads + h))

    bias = [pl.BlockSpec((1, min(tb // ROWS, N_EDGE), ROWS, WIN), lambda h, i: (h, 0, 0, 0)),
            pl.BlockSpec((1, 1, ROWS, WIN), lambda h, i: (h, 0, 0, 0))]
    vec = pl.BlockSpec((1, HEAD_DIM), lambda h, i: (0, 0))
    return col, full, bias, vec


def _attn_windows(scr, block_start, n):
    return jnp.stack([scr[pl.ds(pl.multiple_of(block_start + g * ROWS, ROWS), WIN), :] for g in range(n)])


def _attn_fill(k_ref, v_ref, knw_ref, kn_scr, v_scr, t):
    kn_scr[0:PAD, :] = jnp.zeros((PAD, HEAD_DIM), BF16)
    v_scr[0:PAD, :] = jnp.zeros((PAD, HEAD_DIM), BF16)
    step = min(512, t)

    def fill(j, _):
        rows = pl.ds(pl.multiple_of(j * step, step), step)
        prows = pl.ds(pl.multiple_of(PAD + j * step, CHUNK), step)
        kn_scr[prows, :] = _rms(k_ref[rows, :], knw_ref[...]).astype(BF16)
        v_scr[prows, :] = v_ref[rows, :].astype(BF16)
        return 0

    lax.fori_loop(0, t // step, fill, 0)


def _attn_fwd(proj, bias, qnw_row, knw_row, *, heads, name, tb=2 * ATTN_BLOCK):
    t = proj.shape[0]
    tb = min(tb, t)
    nb, ng = t // tb, tb // ROWS
    col, full, bias_spec, vec = _attn_specs(heads, tb, t)

    def body(q_ref, k_ref, v_ref, z_ref, first_ref, rest_ref, qnw_ref, knw_ref, og_ref, kn_scr, v_scr):
        i = pl.program_id(1)

        @pl.when(i == 0)
        def _():
            _attn_fill(k_ref, v_ref, knw_ref, kn_scr, v_scr, t)

        def run(block_bias):
            start = i * tb
            og = _attn_groups(q_ref[...].reshape(ng, ROWS, HEAD_DIM), z_ref[...].reshape(ng, ROWS, HEAD_DIM),
                              _attn_windows(kn_scr, start, ng), _attn_windows(v_scr, start, ng), block_bias, qnw_ref[...])
            og_ref[...] = og.reshape(tb, HEAD_DIM).astype(BF16)

        def first_bias():
            edge = first_ref[0]
            more = ng - edge.shape[0]
            return edge if more == 0 else jnp.concatenate([edge, jnp.broadcast_to(rest_ref[0], (more, ROWS, WIN))])

        pl.when(i == 0)(lambda: run(first_bias()))
        pl.when(i > 0)(lambda: run(rest_ref[0]))

    return pl.pallas_call(
        body,
        name=name,
        grid=(heads, nb),
        in_specs=[col(0), full(1), full(2), col(3), *bias_spec, vec, vec],
        out_specs=pl.BlockSpec((tb, HEAD_DIM), lambda h, i: (i, h)),
        out_shape=jax.ShapeDtypeStruct((t, heads * HEAD_DIM), BF16),
        scratch_shapes=[pltpu.VMEM((PAD + t, HEAD_DIM), BF16), pltpu.VMEM((PAD + t, HEAD_DIM), BF16)],
        compiler_params=_params("arbitrary", "arbitrary"),
    )(proj, proj, proj, proj, *bias, qnw_row, knw_row)


def _attn_bwd(proj, bias, qnw_row, knw_row, dog, *, heads, name, tb=ATTN_BLOCK, sub=4):
    t = proj.shape[0]
    tb = min(tb, t)
    nb, ng = t // tb, tb // ROWS
    sub = min(sub, ng)
    n_edge = min(ng, N_EDGE)
    assert n_edge % sub == 0
    col, full, bias_spec, vec = _attn_specs(heads, tb, t)

    def body(q_ref, k_ref, v_ref, z_ref, first_ref, rest_ref, qnw_ref, knw_ref, dog_ref,
             dqz_ref, dkv_ref, dbias_ref, dqnw_ref, dknw_ref, kn_scr, v_scr, dkn_scr, dv_scr):
        i = pl.program_id(1)

        @pl.when(i == 0)
        def _():
            _attn_fill(k_ref, v_ref, knw_ref, kn_scr, v_scr, t)
            dkn_scr[...] = jnp.zeros_like(dkn_scr)
            dv_scr[...] = jnp.zeros_like(dv_scr)
            dbias_ref[...] = jnp.zeros_like(dbias_ref)
            dqnw_ref[...] = jnp.zeros_like(dqnw_ref)

        def run(block_bias):
            for g0 in range(0, ng, sub):
                rows = pl.ds(g0 * ROWS, sub * ROWS)
                at = i * tb + g0 * ROWS
                blocked = lambda ref: ref[rows, :].reshape(sub, ROWS, HEAD_DIM)
                dq, dz, dkn, dv, dbias, dqnw = _attn_groups_bwd(
                    blocked(q_ref), blocked(z_ref), _attn_windows(kn_scr, at, sub), _attn_windows(v_scr, at, sub),
                    block_bias(g0), qnw_ref[...], blocked(dog_ref))
                dqz_ref[0, rows, :] = dq.reshape(sub * ROWS, HEAD_DIM).astype(BF16)
                dqz_ref[1, rows, :] = dz.reshape(sub * ROWS, HEAD_DIM).astype(BF16)
                for g in range(sub):
                    window = pl.ds(pl.multiple_of(at + g * ROWS, ROWS), WIN)
                    dkn_scr[window, :] += dkn[g]
                    dv_scr[window, :] += dv[g]
                dbias_ref[0] += dbias
                dqnw_ref[0] += dqnw

        pl.when(i == 0)(lambda: run(lambda g0: first_ref[0, g0:g0 + sub] if g0 + sub <= n_edge else rest_ref[0]))
        pl.when(i > 0)(lambda: run(lambda g0: rest_ref[0]))

        @pl.when(i == nb - 1)
        def _():
            step = min(512, t)

            def finish(j, dknw):
                rows = pl.ds(pl.multiple_of(j * step, step), step)
                prows = pl.ds(pl.multiple_of(PAD + j * step, CHUNK), step)
                _, vjp = jax.vjp(_rms, k_ref[rows, :], knw_ref[...])
                dk, dw = vjp(dkn_scr[prows, :])
                dkv_ref[0, rows, :] = dk.astype(BF16)
                dkv_ref[1, rows, :] = dv_scr[prows, :].astype(BF16)
                return dknw + dw

            dknw_ref[0] = lax.fori_loop(0, t // step, finish, jnp.zeros((1, HEAD_DIM), F32))

    pair_col = pl.BlockSpec((2, tb, HEAD_DIM), lambda h, i: (0, i, h))
    pair_full = pl.BlockSpec((2, t, HEAD_DIM), lambda h, i: (0, 0, h))
    head_vec = pl.BlockSpec((1, 1, HEAD_DIM), lambda h, i: (h, 0, 0))
    pair_shape = jax.ShapeDtypeStruct((2, t, heads * HEAD_DIM), BF16)
    vec_shape = jax.ShapeDtypeStruct((heads, 1, HEAD_DIM), F32)
    return pl.pallas_call(
        body,
        name=name,
        grid=(heads, nb),
        in_specs=[col(0), full(1), full(2), col(3), *bias_spec, vec, vec, pl.BlockSpec((tb, HEAD_DIM), lambda h, i: (i, h))],
        out_specs=[pair_col, pair_full, pl.BlockSpec((1, ROWS, WIN), lambda h, i: (h, 0, 0)), head_vec, head_vec],
        out_shape=[pair_shape, pair_shape, jax.ShapeDtypeStruct((heads, ROWS, WIN), F32), vec_shape, vec_shape],
        scratch_shapes=[pltpu.VMEM((PAD + t, HEAD_DIM), BF16), pltpu.VMEM((PAD + t, HEAD_DIM), BF16),
                        pltpu.VMEM((PAD + t, HEAD_DIM), F32), pltpu.VMEM((PAD + t, HEAD_DIM), F32)],
        compiler_params=_params("arbitrary", "arbitrary"),
    )(proj, proj, proj, proj, *bias, qnw_row, knw_row, dog)


def _lane_row(v):
    v = v.reshape(1, -1)
    return jnp.pad(v, ((0, 0), (0, LANES - v.shape[1])))


def _local_step(x, target, norm_w, wa_in, conv_w, a_log, dt_bias, onw, wa_out, wb_in, qnw, knw, rel_bias, wb_out, *,
                sharded=False):
    ha, hb = a_log.shape[-1], rel_bias.shape[-2]
    na = 4 * ha * HEAD_DIM
    wa_ab = jnp.pad(wa_in[:, na:], ((0, 0), (0, LANES - 2 * ha)))
    alog_row, dtb_row, onw_row = _lane_row(a_log), _lane_row(dt_bias), _lane_row(onw)
    qnw_row, knw_row = _lane_row(qnw), _lane_row(knw)
    bias = _masked_bias(_band_bias(rel_bias.reshape(hb, -1)), min(min(ATTN_BLOCK, x.shape[0]) // ROWS, N_EDGE))

    hn0, ab_a = _rmsnorm_fwd(x, norm_w[0:1], wa_ab, name="norm0")
    proj_a = _matmul(hn0, wa_in, n_cols=na, name="a_in")
    og_a, states, got = _gdn_fwd(proj_a, ab_a, conv_w, alog_row, dtb_row, onw_row, heads=ha, name="gdn_fwd",
                                 gather=[wb_in, wa_out, wb_out] if sharded else [])
    if sharded:
        wb_in, wa_out, wb_out = _join_cols(got[0]), got[1].reshape(-1, got[1].shape[-1]), got[2].reshape(-1, got[2].shape[-1])
    h1, hn1 = _matmul_norm(og_a, wa_out, x, norm_w[1:2], name="a_out_norm1")
    proj_b = _matmul(hn1, wb_in, name="b_in")
    og_b = _attn_fwd(proj_b, bias, qnw_row, knw_row, heads=hb, name="attn_fwd")
    loss, dh2, dh2_b = _matmul_loss(og_b, wb_out, h1, target, name="b_out_loss")

    grad_dtype = BF16 if sharded else F32
    dog_b = _matmul(dh2_b, wb_out, trans_b=True, name="d_b_out_x")
    dwb_out = _matmul(og_b, dh2_b, trans_a=True, out_dtype=grad_dtype, name="d_b_out_w")
    dqz, dkv, dbias, dqnw, dknw = _attn_bwd(proj_b, bias, qnw_row, knw_row, dog_b, heads=hb, name="attn_bwd")
    dproj_b, qkvz = [dqz, dkv], (0, 3, 1, 2)
    dhn1 = _matmul(dproj_b, wb_in, trans_b=True, order=qkvz, name="d_b_in_x")
    dwb_in = _matmul(hn1, dproj_b, trans_a=True, order=qkvz, out_dtype=grad_dtype, col_slabs=N_CHIPS if sharded else 0,
                     name="d_b_in_w")
    dh1, dh1_b, dnw1 = _rmsnorm_bwd(h1, norm_w[1:2], dhn1, dh2, name="d_norm1")

    dog_a = _matmul(dh1_b, wa_out, trans_b=True, name="d_a_out_x")
    dwa_out = _matmul(og_a, dh1_b, trans_a=True, out_dtype=grad_dtype, name="d_a_out_w")
    early = [dwb_in, _split_rows(dwa_out), _split_rows(dwb_out)] if sharded else []
    dproj_a, dab, dconv, dalog, ddtb, donw, landed = _gdn_bwd(
        proj_a, ab_a, conv_w, alog_row, dtb_row, onw_row, states, dog_a, heads=ha, name="gdn_bwd", exchange=early)
    dab_b = dab.astype(BF16)
    if sharded:
        mine = [_sum_slots(s, name=f"chip_sum_{n}") for n, s in zip(("b_w_in", "a_w_out", "b_w_out"), landed)]
        dwa_main, theirs = _matmul(hn0, dproj_a, trans_a=True, out_dtype=grad_dtype, name="d_a_in_w", exchange=mine,
                                   with_pair=True)
        dwb_in, dwa_out, dwb_out = zip(mine, theirs)
    else:
        dwa_main = _matmul(hn0, dproj_a, trans_a=True, out_dtype=grad_dtype, name="d_a_in_w")
    dwa_in = jnp.concatenate(
        [dwa_main, _matmul(hn0, dab_b, trans_a=True, out_dtype=grad_dtype, name="d_a_in_ab_w")[:, :2 * ha]], axis=1)
    if sharded:
        dhn0, (dwa_in, dconv) = _matmul(dproj_a, wa_in, trans_b=True, name="d_a_in_x",
                                        exchange=[_split_cols(dwa_in), _split_cols(dconv)])
    else:
        dhn0 = _matmul(dproj_a, wa_in, trans_b=True, name="d_a_in_x")
    dx, _, dnw0 = _rmsnorm_bwd(x, norm_w[0:1], dhn0, dh1, narrow=(dab_b, wa_ab), name="d_norm0")

    drel = _band_bias_grad(dbias)
    grads = dict(
        norm_w=jnp.concatenate([dnw0, dnw1], axis=0), a_w_in=dwa_in, a_conv_w=dconv, a_a_log=dalog[:, :ha],
        a_dt_bias=ddtb[:, :ha], a_out_norm_w=donw, a_w_out=dwa_out, b_w_in=dwb_in, b_q_norm_w=jnp.sum(dqnw, axis=0),
        b_k_norm_w=jnp.sum(dknw, axis=0), b_rel_bias=drel[None], b_w_out=dwb_out)
    return loss, dx, grads


_ANY = pl.BlockSpec(memory_space=pl.ANY)
_CHIP_FLIPS = ((1, 0), (0, 1), (1, 1))


def _place():
    x, y, c = lax.axis_index("x"), lax.axis_index("y"), lax.axis_index("c")
    return x, y, c


def _flip(v, bit):
    return 1 - v if bit else v


def _remote(src, dst, send_sem, recv_sem, peer):
    return pltpu.make_async_remote_copy(src_ref=src, dst_ref=dst, send_sem=send_sem, recv_sem=recv_sem, device_id=peer,
                                        device_id_type=MESH)


def _comm_call(body, arrays, out_shapes, n_remote, n_local, name):
    scratch = [pltpu.SemaphoreType.DMA((n_remote,)), pltpu.SemaphoreType.DMA((n_remote,))]
    if n_local:
        scratch.append(pltpu.SemaphoreType.DMA((n_local,)))
    return pl.pallas_call(
        body, name=name, in_specs=[_ANY] * len(arrays), out_specs=[_ANY] * len(out_shapes), out_shape=out_shapes,
        scratch_shapes=scratch)(*arrays)


def _chip_scratch(n):
    return [pltpu.SemaphoreType.DMA((3 * n,)), pltpu.SemaphoreType.DMA((3 * n,)), pltpu.SemaphoreType.DMA((n,))]


def _chip_shapes(gather, arrays):
    return [jax.ShapeDtypeStruct(((N_CHIPS,) + s.shape) if gather else s.shape, s.dtype) for s in arrays]


def _chip_traffic(gather, ins, outs, sems):
    send_sems, recv_sems, local_sems = sems
    x, y, c = _place()
    mine = 2 * x + y
    local, remote, landing = [], [], []
    for a in range(len(ins)):
        local.append(pltpu.make_async_copy(ins[a] if gather else ins[a].at[mine], outs[a].at[mine], local_sems.at[a]))
        for k, (fx, fy) in enumerate(_CHIP_FLIPS):
            peer = (_flip(x, fx), _flip(y, fy), c)
            theirs = 2 * peer[0] + peer[1]
            src = ins[a] if gather else ins[a].at[theirs]
            pair = send_sems.at[3 * a + k], recv_sems.at[3 * a + k]
            remote.append(_remote(src, outs[a].at[mine], *pair, peer))
            landing.append(_remote(src, outs[a].at[theirs], *pair, peer))
    return local + remote, (local, landing, remote)


def _start(traffic):
    for cp in traffic[0]:
        cp.start()


def _finish(traffic):
    local, landing, remote = traffic[1]
    for cp in local:
        cp.wait()
    for cp in landing:
        cp.wait_recv()
    for cp in remote:
        cp.wait_send()


def _pair_scratch(n):
    return [pltpu.SemaphoreType.DMA((n,)), pltpu.SemaphoreType.DMA((n,))]


def _pair_traffic(ins, outs, sems):
    send_sems, recv_sems = sems
    x, y, c = _place()
    copies = [_remote(ins[a], outs[a], send_sems.at[a], recv_sems.at[a], (x, y, 1 - c)) for a in range(len(ins))]
    return copies, ([], copies, copies)


def _with_exchange(compute, n_in, n_out, gather, n_x, grid):
    if not n_x:
        return compute

    def body(*refs):
        ins, x_in = refs[:n_in], refs[n_in:n_in + n_x]
        outs, x_out = refs[n_in + n_x:n_in + n_x + n_out], refs[n_in + n_x + n_out:n_in + 2 * n_x + n_out]
        n_sems = 2 if gather == "pair" else 3
        scratch, sems = refs[n_in + 2 * n_x + n_out:-n_sems], refs[-n_sems:]
        traffic = _pair_traffic(x_in, x_out, sems) if gather == "pair" else _chip_traffic(gather, x_in, x_out, sems)
        first = functools.reduce(jnp.logical_and, [pl.program_id(d) == 0 for d in range(len(grid))])
        last = functools.reduce(jnp.logical_and, [pl.program_id(d) == grid[d] - 1 for d in range(len(grid))])

        @pl.when(first)
        def _():
            _start(traffic)

        compute(*ins, *outs, *scratch)

        @pl.when(last)
        def _():
            _finish(traffic)

    return body


def _gather_shared(shard, small, *, name):
    rows = shard.shape[0]
    assert rows % 2 == 0
    half = rows // 2

    def body(shard_ref, small_ref, out_ref, small_out_ref, send_sems, recv_sems, local_sems):
        x, y, c = _place()
        mine = 2 * x + y
        sibling = (x, y, 1 - c)
        my_rows = pl.ds(pl.multiple_of(c * half, 8), half)
        local = [pltpu.make_async_copy(shard_ref, out_ref.at[mine], local_sems.at[0]),
                 pltpu.make_async_copy(small_ref, small_out_ref.at[mine], local_sems.at[1])]
        sent, landed, passed_on, handed = [], [], [], []
        for k, (fx, fy) in enumerate(_CHIP_FLIPS):
            peer = (_flip(x, fx), _flip(y, fy), c)
            theirs = 2 * peer[0] + peer[1]
            ici, d2d, tiny = [(send_sems.at[3 * n + k], recv_sems.at[3 * n + k]) for n in range(3)]
            sent.append(_remote(shard_ref.at[my_rows], out_ref.at[mine, my_rows], *ici, peer))
            landed.append(_remote(shard_ref.at[my_rows], out_ref.at[theirs, my_rows], *ici, peer))
            sent.append(_remote(small_ref, small_out_ref.at[mine], *tiny, peer))
            landed.append(_remote(small_ref, small_out_ref.at[theirs], *tiny, peer))
            passed_on.append(_remote(out_ref.at[theirs, my_rows], out_ref.at[theirs, my_rows], *d2d, sibling))
            other_rows = pl.ds(pl.multiple_of((1 - c) * half, 8), half)
            handed.append(_remote(out_ref.at[theirs, other_rows], out_ref.at[theirs, other_rows], *d2d, sibling))
        for cp in local + sent:
            cp.start()
        for k in range(3):
            landed[2 * k].wait_recv()
            passed_on[k].start()
        for k in range(3):
            landed[2 * k + 1].wait_recv()
            handed[k].wait_recv()
        for cp in local:
            cp.wait()
        for cp in sent + passed_on:
            cp.wait_send()

    return pl.pallas_call(
        body, name=name, in_specs=[_ANY] * 2, out_specs=[_ANY] * 2, out_shape=_chip_shapes(True, [shard, small]),
        scratch_shapes=[pltpu.SemaphoreType.DMA((9,)), pltpu.SemaphoreType.DMA((9,)), pltpu.SemaphoreType.DMA((2,))],
    )(shard, small)


def _swap_pair(arrays, *, name):
    n = len(arrays)

    def body(*refs):
        ins, outs, (send_sems, recv_sems) = refs[:n], refs[n:2 * n], refs[2 * n:]
        x, y, c = _place()
        copies = [_remote(ins[a], outs[a], send_sems.at[a], recv_sems.at[a], (x, y, 1 - c)) for a in range(n)]
        for cp in copies:
            cp.start()
        for cp in copies:
            cp.wait_recv()
        for cp in copies:
            cp.wait_send()

    shapes = [jax.ShapeDtypeStruct(s.shape, s.dtype) for s in arrays]
    return _comm_call(body, arrays, shapes, n, 0, name)


def _gather_all(tile, *, name):
    def body(in_ref, out_ref, send_sems, recv_sems, local_sems):
        x, y, c = _place()
        mine = 4 * x + 2 * y + c
        local = pltpu.make_async_copy(in_ref, out_ref.at[mine], local_sems.at[0])
        remote, landing = [], []
        for k in range(1, N_DEV):
            peer = (_flip(x, k & 4), _flip(y, k & 2), _flip(c, k & 1))
            sems = send_sems.at[k - 1], recv_sems.at[k - 1]
            remote.append(_remote(in_ref, out_ref.at[mine], *sems, peer))
            landing.append(_remote(in_ref, out_ref.at[4 * peer[0] + 2 * peer[1] + peer[2]], *sems, peer))
        for cp in [local] + remote:
            cp.start()
        local.wait()
        for cp in landing:
            cp.wait_recv()
        for cp in remote:
            cp.wait_send()

    return _comm_call(body, [tile], [jax.ShapeDtypeStruct((N_DEV,) + tile.shape, tile.dtype)], N_DEV - 1, 1, name)[0]


def _sum_slots(slabs, *, name, tr=128):
    s, r, c = slabs.shape
    tr = min(tr, r)

    def body(in_ref, o_ref):
        acc = in_ref[0].astype(F32)
        for j in range(1, s):
            acc = acc + in_ref[j].astype(F32)
        o_ref[...] = acc

    return pl.pallas_call(
        body, name=name, grid=(r // tr,),
        in_specs=[pl.BlockSpec((s, tr, c), lambda i: (0, i, 0))], out_specs=pl.BlockSpec((tr, c), lambda i: (i, 0)),
        out_shape=jax.ShapeDtypeStruct((r, c), F32), compiler_params=_params("parallel"))(slabs)


def _adamw_math(w, g, m, v):
    m = ADAM_B1 * m + (1.0 - ADAM_B1) * g
    v = ADAM_B2 * v + (1.0 - ADAM_B2) * (g * g)
    m_hat = m / (1.0 - ADAM_B1 ** ADAM_STEP)
    v_hat = v / (1.0 - ADAM_B2 ** ADAM_STEP)
    delta = -ADAM_LR * (m_hat / (jnp.sqrt(v_hat) + ADAM_EPS) + ADAM_WD * w)
    return delta, m, v


def _adamw(w, m, v, parts, *, name, tr=128):
    r, c = w.shape
    tr = min(tr, r)
    s = len(parts)

    def body(w_ref, m_ref, v_ref, *refs):
        g_ref, d_ref, nm_ref, nv_ref = refs[s:]
        g = refs[0][...]
        for p_ref in refs[1:s]:
            g = g + p_ref[...]
        g_ref[...] = g
        d_ref[...], nm_ref[...], nv_ref[...] = _adamw_math(w_ref[...], g, m_ref[...], v_ref[...])

    blk = pl.BlockSpec((tr, c), lambda i: (i, 0))
    shape = jax.ShapeDtypeStruct((r, c), F32)
    return pl.pallas_call(
        body, name=name, grid=(r // tr,), in_specs=[blk] * (3 + s), out_specs=[blk] * 4, out_shape=[shape] * 4,
        compiler_params=_params("parallel"))(w, m, v, *parts)


_BIG = ("a_w_in", "b_w_in", "a_w_out", "b_w_out", "a_conv_w")
_SMALL = ("norm_w", "a_a_log", "a_dt_bias", "a_out_norm_w", "b_q_norm_w", "b_k_norm_w", "b_rel_bias")
_ORDER = ("norm_w", "a_w_in", "a_conv_w", "a_a_log", "a_dt_bias", "a_out_norm_w", "a_w_out", "b_w_in", "b_q_norm_w",
          "b_k_norm_w", "b_rel_bias", "b_w_out")


def _join_cols(g):
    return jnp.transpose(g, (1, 0, 2)).reshape(g.shape[1], -1)


def _split_cols(g):
    return jnp.transpose(g.reshape(g.shape[0], N_CHIPS, -1), (1, 0, 2))


def _split_rows(g):
    return g.reshape(N_CHIPS, -1, g.shape[-1])


PACK_ROWS = 8


def _pack(d):
    flat = jnp.concatenate([d[n].reshape(-1) for n in _SMALL])
    return jnp.pad(flat, (0, -flat.shape[0] % (PACK_ROWS * LANES))).reshape(PACK_ROWS, -1)


def _unpack(tile, like):
    flat, out, at = tile.reshape(-1), {}, 0
    for n in _SMALL:
        size = like[n].size
        out[n] = flat[at:at + size].reshape(like[n].shape)
        at += size
    return out


def kernel(x, norm_w, a_w_in, a_conv_w, a_a_log, a_dt_bias, a_out_norm_w, a_w_out, b_w_in, b_q_norm_w, b_k_norm_w, b_rel_bias, b_w_out, loss_target, m_norm_w, m_a_w_in, m_a_conv_w, m_a_a_log, m_a_dt_bias, m_a_out_norm_w, m_a_w_out, m_b_w_in, m_b_q_norm_w, m_b_k_norm_w, m_b_rel_bias, m_b_w_out, v_norm_w, v_a_w_in, v_a_conv_w, v_a_a_log, v_a_dt_bias, v_a_out_norm_w, v_a_w_out, v_b_w_in, v_b_q_norm_w, v_b_k_norm_w, v_b_rel_bias, v_b_w_out):
    w = dict(norm_w=norm_w, a_w_in=a_w_in, a_conv_w=a_conv_w, a_a_log=a_a_log, a_dt_bias=a_dt_bias,
             a_out_norm_w=a_out_norm_w, a_w_out=a_w_out, b_w_in=b_w_in, b_q_norm_w=b_q_norm_w, b_k_norm_w=b_k_norm_w,
             b_rel_bias=b_rel_bias, b_w_out=b_w_out)
    m = dict(norm_w=m_norm_w, a_w_in=m_a_w_in, a_conv_w=m_a_conv_w, a_a_log=m_a_a_log, a_dt_bias=m_a_dt_bias,
             a_out_norm_w=m_a_out_norm_w, a_w_out=m_a_w_out, b_w_in=m_b_w_in, b_q_norm_w=m_b_q_norm_w,
             b_k_norm_w=m_b_k_norm_w, b_rel_bias=m_b_rel_bias, b_w_out=m_b_w_out)
    v = dict(norm_w=v_norm_w, a_w_in=v_a_w_in, a_conv_w=v_a_conv_w, a_a_log=v_a_a_log, a_dt_bias=v_a_dt_bias,
             a_out_norm_w=v_a_out_norm_w, a_w_out=v_a_w_out, b_w_in=v_b_w_in, b_q_norm_w=v_b_q_norm_w,
             b_k_norm_w=v_b_k_norm_w, b_rel_bias=v_b_rel_bias, b_w_out=v_b_w_out)

    wa_in, conv = _gather_shared(a_w_in[0].astype(BF16), a_conv_w[0], name="gather_a_in")
    loss, dx, grads = _local_step(
        x[0], loss_target[0], norm_w, _join_cols(wa_in), _join_cols(conv), a_a_log, a_dt_bias, a_out_norm_w,
        a_w_out[0].astype(BF16), b_w_in[0].astype(BF16), b_q_norm_w, b_k_norm_w, b_rel_bias, b_w_out[0].astype(BF16),
        sharded=True)
    loss = lax.psum(loss, ("x", "y", "c"))

    late = [n for n in _BIG if not isinstance(grads[n], tuple)]
    mine = [_sum_slots(grads[n], name=f"chip_sum_{n}") for n in late]
    sums = {n: grads[n] for n in _BIG if n not in late}
    sums.update(zip(late, zip(mine, _swap_pair(mine, name="pair_grads"))))
    out = {}
    for n in _BIG:
        out[n] = [r[None] for r in _adamw(w[n][0], m[n][0], v[n][0], list(sums[n]), name=f"adamw_{n}")]

    tiles = _gather_all(_pack(grads), name="gather_small_grads")
    res = _adamw(_pack(w), _pack(m), _pack(v), [tiles[d] for d in range(N_DEV)], name="adamw_small")
    unpacked = [_unpack(r, w) for r in res]
    for n in _SMALL:
        out[n] = [u[n] for u in unpacked]

    return (loss, dx[None], *[out[n][0] for n in _ORDER], *[out[n][1] for n in _ORDER], *[out[n][2] for n in _ORDER],
            *[out[n][3] for n in _ORDER])
```

```python
import functools

import numpy as np
import jax
import jax.numpy as jnp
from jax import lax
from jax.experimental import pallas as pl
from jax.experimental.pallas import tpu as pltpu

F32 = jnp.float32
BF16 = jnp.bfloat16

CHUNK = 64
HEAD_DIM = 128
LEFT_CHUNKS = 8
REL_CLIP = 256
CONV_K = 4
EPS = 1e-6
HALO = 8

ADAM_LR = 0.001
ADAM_B1 = 0.9
ADAM_B2 = 0.999
ADAM_EPS = 1e-08
ADAM_WD = 0.01
ADAM_STEP = 10

LANES = 128
N_CHIPS = 4
N_DEV = 8
VMEM_LIMIT_BYTES = 56 * 1024 * 1024
VMEM_LIMIT_WIDE_BYTES = 63 * 1024 * 1024
MESH = pl.DeviceIdType.MESH


def _params(*sem, vmem=VMEM_LIMIT_BYTES):
    return pltpu.CompilerParams(dimension_semantics=sem, vmem_limit_bytes=vmem)


def _dot(a, b, dims=(((1,), (0,)), ((), ())), precision=None):
    return lax.dot_general(a, b, dims, precision=precision, preferred_element_type=F32)


_NT = (((1,), (1,)), ((), ()))
_TN = (((0,), (0,)), ((), ()))


def _bdot(a, b, dims=(((1,), (0,)), ((), ()))):
    return _dot(a.astype(BF16), b.astype(BF16), dims)


def _fdot(a, b, dims=(((1,), (0,)), ((), ()))):
    return _dot(a, b, dims, precision=lax.Precision.HIGH)


def _silu(x):
    return x * jax.nn.sigmoid(x)


def _stacks(x):
    if not isinstance(x, (list, tuple)) and x.ndim != 3:
        return None
    arrays = list(x) if isinstance(x, (list, tuple)) else [x]
    assert len({(v.shape[1], v.shape[2], v.dtype) for v in arrays}) == 1
    starts = [sum(v.shape[0] for v in arrays[:r]) for r in range(len(arrays))]
    return arrays, starts, starts[-1] + arrays[-1].shape[0]


def _static_pick(table, index):
    out = table[-1]
    for s in range(len(table) - 2, -1, -1):
        out = jnp.where(index == s, table[s], out)
    return out


def _matmul(a, b, *, name, trans_a=False, trans_b=False, residual=None, out_dtype=F32, tm=1024, tn=1024, tk=2048,
            col_slabs=0, order=None, n_cols=0, exchange=(), with_pair=False):
    assert not (trans_a and trans_b)
    a_stack, b_stack = _stacks(a), _stacks(b)
    assert not (a_stack and (trans_a or b_stack)) and not (b_stack and trans_b)
    a_list, b_list = (a_stack[0] if a_stack else [a]), (b_stack[0] if b_stack else [b])
    a0, b0 = a_list[0], b_list[0]
    k, m = (a_stack[2] * a0.shape[2], a0.shape[1]) if a_stack else a.shape if trans_a else a.shape[::-1]
    n = n_cols or (b_stack[2] * b0.shape[2] if b_stack else b.shape[0] if trans_b else b.shape[1])
    tm, tn, tk = min(tm, m), min(tn, n // max(col_slabs, 1)), min(tk, k)
    if a_stack:
        tk = min(tk, a0.shape[2])
        per_k = a0.shape[2] // tk
    if b_stack:
        tn = min(tn, b0.shape[2])
        per_n = b0.shape[2] // tn
    assert m % tm == 0 and n % tn == 0 and k % tk == 0, (a0.shape, b0.shape, tm, tn, tk)
    nk = k // tk
    dims = _NT if trans_b else _TN if trans_a else (((1,), (0,)), ((), ()))
    order = list(order) if order is not None else list(range(max(a_stack[2] if a_stack else 0, b_stack[2] if b_stack else 0)))
    na, nb = len(a_list), len(b_list)

    def group_of(r, stack, position):
        arrays, starts, _ = stack
        local = position - starts[r]
        return jnp.logical_and(local >= 0, local < arrays[r].shape[0]), jnp.clip(local, 0, arrays[r].shape[0] - 1)

    def body(*refs):
        a_refs, b_refs = refs[:na], refs[na:na + nb]
        r_ref = refs[na + nb] if residual is not None else None
        o_ref, acc_ref = refs[-2:]
        j, kk = pl.program_id(1), pl.program_id(2)

        @pl.when(kk == 0)
        def _():
            acc_ref[...] = jnp.zeros_like(acc_ref)

        for ra, a_ref in enumerate(a_refs):
            for rb, b_ref in enumerate(b_refs):
                def add(a_ref=a_ref, b_ref=b_ref):
                    acc_ref[...] += _dot(a_ref[...], b_ref[...], dims)

                if na > 1:
                    pl.when(group_of(ra, a_stack, kk // per_k)[0])(add)
                elif nb > 1:
                    pl.when(group_of(rb, b_stack, j // per_n)[0])(add)
                else:
                    add()

        @pl.when(kk == nk - 1)
        def _():
            r = acc_ref[...]
            if r_ref is not None:
                r = r + r_ref[...]
            o_ref[...] = r.astype(o_ref.dtype)

    if a_stack:
        a_specs = [pl.BlockSpec((None, tm, tk), lambda i, j, kk, r=r: (group_of(r, a_stack, kk // per_k)[1], i, kk % per_k))
                   for r in range(na)]
        b_k = lambda kk: _static_pick(order, kk // per_k) * per_k + kk % per_k
    else:
        a_specs = [pl.BlockSpec((tk, tm), lambda i, j, kk: (kk, i)) if trans_a else pl.BlockSpec((tm, tk), lambda i, j, kk: (i, kk))]
        b_k = lambda kk: kk
    if b_stack:
        b_specs = [pl.BlockSpec((None, tk, tn), lambda i, j, kk, r=r: (group_of(r, b_stack, j // per_n)[1], kk, j % per_n))
                   for r in range(nb)]
        out_col = lambda j: _static_pick(order, j // per_n) * per_n + j % per_n
    else:
        b_specs = [pl.BlockSpec((tn, tk), lambda i, j, kk: (j, b_k(kk))) if trans_b
                   else pl.BlockSpec((tk, tn), lambda i, j, kk: (b_k(kk), j))]
        out_col = lambda j: j
    in_specs = a_specs + b_specs
    args = a_list + b_list
    if residual is not None:
        in_specs.append(pl.BlockSpec((tm, tn), lambda i, j, kk: (i, j)))
        args.append(residual)
    grid = (m // tm, n // tn, nk)
    n_x = len(exchange)
    if col_slabs:
        per = n // col_slabs // tn
        assert per * tn * col_slabs == n, (n, tn, col_slabs)
        out_spec = pl.BlockSpec((None, tm, tn), lambda i, j, kk: (out_col(j) // per, i, out_col(j) % per))
        out_shape = jax.ShapeDtypeStruct((col_slabs, m, n // col_slabs), out_dtype)
    else:
        out_spec = pl.BlockSpec((tm, tn), lambda i, j, kk: (i, out_col(j)))
        out_shape = jax.ShapeDtypeStruct((m, n), out_dtype)
    out, *landed = pl.pallas_call(
        _with_exchange(body, len(args), 1, "pair" if with_pair else False, n_x, grid),
        name=name,
        grid=grid,
        in_specs=in_specs + [_ANY] * n_x,
        out_specs=[out_spec] + [_ANY] * n_x,
        out_shape=[out_shape] + _chip_shapes(False, exchange),
        scratch_shapes=[pltpu.VMEM((tm, tn), F32)] + ((_pair_scratch if with_pair else _chip_scratch)(n_x) if n_x else []),
        compiler_params=_params(*(("arbitrary",) * 3 if n_x else ("parallel", "parallel", "arbitrary"))),
    )(*args, *exchange)
    return (out, landed) if n_x else out


def _rms(x, w):
    return x * lax.rsqrt(jnp.mean(x * x, axis=-1, keepdims=True) + EPS) * w


def _rmsnorm_fwd(x, w_row, narrow_w, *, name, tr=512):
    t, d = x.shape
    tr = min(tr, t)

    def body(x_ref, w_ref, nw_ref, o_ref, narrow_ref):
        hn = _rms(x_ref[...], w_ref[...]).astype(BF16)
        o_ref[...] = hn
        narrow_ref[...] = _dot(hn, nw_ref[...])

    return pl.pallas_call(
        body,
        name=name,
        grid=(t // tr,),
        in_specs=[pl.BlockSpec((tr, d), lambda i: (i, 0)), pl.BlockSpec((1, d), lambda i: (0, 0)),
                  pl.BlockSpec((d, LANES), lambda i: (0, 0))],
        out_specs=[pl.BlockSpec((tr, d), lambda i: (i, 0)), pl.BlockSpec((tr, LANES), lambda i: (i, 0))],
        out_shape=[jax.ShapeDtypeStruct((t, d), BF16), jax.ShapeDtypeStruct((t, LANES), F32)],
        compiler_params=_params("parallel"),
    )(x, w_row, narrow_w)


def _matmul_norm(a, b, residual, w_row, *, name, tm=512):
    t, k = a.shape
    d = b.shape[1]
    tm = min(tm, t)

    def body(a_ref, b_ref, r_ref, w_ref, h_ref, hn_ref):
        h = _dot(a_ref[...], b_ref[...]) + r_ref[...]
        h_ref[...] = h
        hn_ref[...] = _rms(h, w_ref[...]).astype(BF16)

    row = pl.BlockSpec((tm, d), lambda i: (i, 0))
    return pl.pallas_call(
        body,
        name=name,
        grid=(t // tm,),
        in_specs=[pl.BlockSpec((tm, k), lambda i: (i, 0)), pl.BlockSpec((k, d), lambda i: (0, 0)), row,
                  pl.BlockSpec((1, d), lambda i: (0, 0))],
        out_specs=[row, row],
        out_shape=[jax.ShapeDtypeStruct((t, d), F32), jax.ShapeDtypeStruct((t, d), BF16)],
        compiler_params=_params("parallel"),
    )(a, b, residual, w_row)


def _rmsnorm_bwd(x, w_row, dy, dres, *, name, tr=256, narrow=None):
    t, d = x.shape
    tr = min(tr, t)
    extra = list(narrow) if narrow is not None else []

    def body(x_ref, w_ref, dy_ref, dres_ref, *refs):
        dx_ref, dxb_ref, dw_ref = refs[len(extra):]

        @pl.when(pl.program_id(0) == 0)
        def _():
            dw_ref[...] = jnp.zeros_like(dw_ref)

        dy = dy_ref[...]
        if extra:
            dy = dy + _dot(refs[0][...], refs[1][...], _NT)
        _, vjp = jax.vjp(_rms, x_ref[...], w_ref[...])
        dx, dw = vjp(dy)
        dx = dx + dres_ref[...]
        dx_ref[...] = dx
        dxb_ref[...] = dx.astype(BF16)
        dw_ref[...] += dw

    row = pl.BlockSpec((tr, d), lambda i: (i, 0))
    vec = pl.BlockSpec((1, d), lambda i: (0, 0))
    extra_specs = [pl.BlockSpec((tr, LANES), lambda i: (i, 0)), pl.BlockSpec((d, LANES), lambda i: (0, 0))] if extra else []
    return pl.pallas_call(
        body,
        name=name,
        grid=(t // tr,),
        in_specs=[row, vec, row, row] + extra_specs,
        out_specs=[row, row, vec],
        out_shape=[jax.ShapeDtypeStruct((t, d), F32), jax.ShapeDtypeStruct((t, d), BF16), jax.ShapeDtypeStruct((1, d), F32)],
        compiler_params=_params("arbitrary"),
    )(x, w_row, dy, dres, *extra)


def _matmul_loss(a, b, residual, target, *, name, tm=512, tn=1024):
    t, k = a.shape
    d = b.shape[1]
    tm, tn = min(tm, t), min(tn, d)

    def body(a_ref, b_ref, r_ref, t_ref, dh_ref, dhb_ref, part_ref):
        @pl.when(pl.program_id(1) == 0)
        def _():
            part_ref[...] = jnp.zeros_like(part_ref)

        err = _dot(a_ref[...], b_ref[...]) + r_ref[...] - t_ref[...]
        dh = err * (1.0 / d)
        dh_ref[...] = dh
        dhb_ref[...] = dh.astype(BF16)
        part_ref[...] += jnp.sum(err * err, axis=0, keepdims=True)

    tile = pl.BlockSpec((tm, tn), lambda j, i: (i, j))
    dh, dhb, part = pl.pallas_call(
        body,
        name=name,
        grid=(d // tn, t // tm),
        in_specs=[pl.BlockSpec((tm, k), lambda j, i: (i, 0)), pl.BlockSpec((k, tn), lambda j, i: (0, j)), tile, tile],
        out_specs=[tile, tile, pl.BlockSpec((1, tn), lambda j, i: (0, j))],
        out_shape=[jax.ShapeDtypeStruct((t, d), F32), jax.ShapeDtypeStruct((t, d), BF16), jax.ShapeDtypeStruct((1, d), F32)],
        compiler_params=_params("arbitrary", "arbitrary"),
    )(a, b, residual, target)
    return 0.5 / d * jnp.sum(part), dh, dhb


_BNN = (((2,), (1,)), ((0,), (0,)))
_BNT = (((2,), (2,)), ((0,), (0,)))
_BTN = (((1,), (1,)), ((0,), (0,)))


_TAP0 = HALO - (CONV_K - 1)


def _conv(x_ref, w, rows):
    c = w[0:1, :] * x_ref[_TAP0:_TAP0 + rows, :]
    for j in range(1, CONV_K):
        c = c + w[j:j + 1, :] * x_ref[_TAP0 + j:_TAP0 + j + rows, :]
    return c


def _conv_silu_bwd(x_ref, w, dact, dc_ref, rows):
    c = _conv(x_ref, w, rows)
    sig = jax.nn.sigmoid(c)
    dc = dact * (sig * (1.0 + c * (1.0 - sig)))
    dw = [jnp.sum(dc * x_ref[_TAP0 + j:_TAP0 + j + rows, :], axis=0, keepdims=True) for j in range(CONV_K)]
    dc_ref[0:HALO, :] = jnp.zeros((HALO, HEAD_DIM), F32)
    dc_ref[HALO:HALO + rows, :] = dc
    dc_ref[HALO + rows:HALO + rows + HALO, :] = jnp.zeros((HALO, HEAD_DIM), F32)
    first = HALO - _TAP0
    dx = w[0:1, :] * dc_ref[first:first + HALO + rows, :]
    for j in range(1, CONV_K):
        dx = dx + w[j:j + 1, :] * dc_ref[first - j:first - j + HALO + rows, :]
    return dx, dw


@jax.custom_vjp
def _unit_lower_inverse(neg_l):
    n = neg_l.shape[0]
    eye = (lax.broadcasted_iota(jnp.int32, (n, CHUNK, CHUNK), 1) == lax.broadcasted_iota(jnp.int32, (n, CHUNK, CHUNK), 2))
    inv = eye.astype(F32) + neg_l
    power = _bdot(neg_l, neg_l, _BNN)
    for _ in range(4):
        both = _bdot(jnp.concatenate([inv, power], axis=1), power, _BNN)
        inv, power = inv + both[:, :CHUNK], both[:, CHUNK:]
    return inv + _bdot(inv, power, _BNN)


def _unit_lower_inverse_fwd(neg_l):
    inv = _unit_lower_inverse(neg_l)
    return inv, inv


def _unit_lower_inverse_bwd(inv, dinv):
    return (_fdot(_fdot(inv, dinv, _BTN), inv, _BNT),)


_unit_lower_inverse.defvjp(_unit_lower_inverse_fwd, _unit_lower_inverse_bwd)


def _gdn_intra(qt, kt, v, a, b, alog, dtb):
    n = a.shape[0] // CHUNK
    q = qt * lax.rsqrt(jnp.sum(qt * qt, axis=-1, keepdims=True) + EPS) * (HEAD_DIM ** -0.5)
    k = kt * lax.rsqrt(jnp.sum(kt * kt, axis=-1, keepdims=True) + EPS)
    lanes = jnp.ones((1, HEAD_DIM), F32)
    beta = jax.nn.sigmoid(b) * lanes
    sp = a + dtb
    g = (-jnp.exp(alog) * (jnp.maximum(sp, 0.0) + jnp.log(1.0 + jnp.exp(-jnp.abs(sp))))) * lanes
    q, k, v, beta, g = (t.reshape(n, CHUNK, HEAD_DIM) for t in (q, k, v, beta, g))

    row = lax.broadcasted_iota(jnp.int32, (n, CHUNK, CHUNK), 1)
    col = lax.broadcasted_iota(jnp.int32, (n, CHUNK, CHUNK), 2)
    tri_incl = row >= col
    tri_strict = row > col
    gc = _fdot(tri_incl.astype(F32), g, _BNN)
    gc_row = _fdot(g[:, :, :CHUNK], (row <= col).astype(F32), _BTN)
    decay = jnp.exp(jnp.where(tri_incl, gc[:, :, :CHUNK] - gc_row, -1e30))
    kb = k * beta
    vb = v * beta
    with_k = _bdot(jnp.concatenate([kb, q], axis=1), k, _BNT)
    neg_l = jnp.where(tri_strict, -(with_k[:, :CHUNK] * decay), 0.0)
    qk = jnp.where(tri_incl, with_k[:, CHUNK:] * decay, 0.0)
    inv = _unit_lower_inverse(neg_l)
    e = jnp.exp(gc)
    solved = _bdot(inv, jnp.concatenate([kb * e, vb], axis=2), _BNN)
    g_last = gc[:, CHUNK - 1:CHUNK, :]
    k_dec = k * jnp.exp(g_last - gc)
    from_k = _bdot(k_dec, solved, _BTN)
    from_qk = _bdot(qk, solved, _BNN)
    step, add = -from_k[:, :, :HEAD_DIM], from_k[:, :, HEAD_DIM:]
    read, out = q * e - from_qk[:, :, :HEAD_DIM], from_qk[:, :, HEAD_DIM:]
    return step, add, jnp.exp(g_last), read, out


def _gdn_scan_step(state, step, add, decay_last):
    return state * decay_last + _bdot(step, state) + add


def _gdn_outputs(states, read, out, z, onw):
    return _rms(_bdot(read, states, _BNN) + out, onw) * _silu(z)


def _scan_scratch(n, dtype):
    return [pltpu.VMEM((n, HEAD_DIM, HEAD_DIM), dtype), pltpu.VMEM((n, HEAD_DIM, HEAD_DIM), F32), pltpu.VMEM((n, 1, HEAD_DIM), F32)]


def _head_lane(h, offset=0):
    return lax.broadcasted_iota(jnp.int32, (1, LANES), 1) == h + offset


def _pick(mask, x):
    return jnp.sum(jnp.where(mask, x, 0.0), axis=1, keepdims=True)


def _gdn_specs(heads, tb, rev, nb, PAIR):
    assert heads % PAIR == 0
    blk = (lambda i: nb - 1 - i) if rev else (lambda i: i)
    hb = tb // HALO
    width, pairs = PAIR * HEAD_DIM, heads // PAIR

    def col(group):
        return pl.BlockSpec((tb, width), lambda i, h: (blk(i), group * pairs + h))

    def halo(group):
        return pl.BlockSpec((HALO, width), lambda i, h: (jnp.maximum(blk(i) * hb - 1, 0), group * pairs + h))

    def convw(group):
        return pl.BlockSpec((CONV_K, width), lambda i, h: (0, group * pairs + h))

    vec = pl.BlockSpec((1, LANES), lambda i, h: (0, 0))
    ab = pl.BlockSpec((tb, LANES), lambda i, h: (blk(i), 0))
    states = pl.BlockSpec((PAIR, tb // CHUNK, HEAD_DIM, HEAD_DIM), lambda i, h: (h, blk(i), 0, 0))
    return blk, col, halo, convw, vec, ab, states


def _head_cols(p):
    return slice(p * HEAD_DIM, (p + 1) * HEAD_DIM)


def _gdn_fwd(proj, ab, conv_w, alog_row, dtb_row, onw_row, *, heads, name, tb=1024, pair=4, gather=()):
    t = proj.shape[0]
    tb = min(tb, t)
    nb, cpb = t // tb, tb // CHUNK
    PAIR = min(pair, heads)
    _, col, halo, convw, vec, abspec, states = _gdn_specs(heads, tb, False, nb, PAIR)

    def body(q_ref, k_ref, v_ref, qh_ref, kh_ref, vh_ref, z_ref, ab_ref, wq_ref, wk_ref, wv_ref, alog_ref, dtb_ref, onw_ref,
             og_ref, st_ref, state_scr, x_scr, *op_scr):
        i, pair = pl.program_id(0), pl.program_id(1)
        abv = ab_ref[...]
        heads_here, later = [pair * PAIR + p for p in range(PAIR)], []
        for p, h in enumerate(heads_here):
            cols = _head_cols(p)
            for n, (ref, href) in enumerate(((q_ref, qh_ref), (k_ref, kh_ref), (v_ref, vh_ref))):
                x_scr[p, n, 0:HALO, :] = jnp.where(i > 0, href[:, cols], 0.0)
                x_scr[p, n, HALO:HALO + tb, :] = ref[:, cols]
            sel_a, sel_b = _head_lane(h), _head_lane(h, heads)
            alog, dtb = _pick(sel_a, alog_ref[...]), _pick(sel_a, dtb_ref[...])
            acts = [_silu(_conv(x_scr.at[p, n], w_ref[:, cols], tb)) for n, w_ref in enumerate((wq_ref, wk_ref, wv_ref))]
            *scan, read, out = _gdn_intra(*acts, _pick(sel_a, abv), _pick(sel_b, abv), alog, dtb)
            for scr, val in zip(op_scr[3 * p:3 * p + 3], scan):
                scr[...] = val.astype(scr.dtype)
            later.append((read, out))

        def chunk(c, states):
            for p in range(PAIR):
                st_ref[p, c] = states[p]
            return tuple(_gdn_scan_step(states[p], *[scr[c] for scr in op_scr[3 * p:3 * p + 3]]) for p in range(PAIR))

        @pl.when(i == 0)
        def _():
            for h in heads_here:
                state_scr[h] = jnp.zeros((HEAD_DIM, HEAD_DIM), F32)

        last = lax.fori_loop(0, cpb, chunk, tuple(state_scr[h] for h in heads_here))
        for p, h in enumerate(heads_here):
            cols = _head_cols(p)
            state_scr[h] = last[p]
            og = _gdn_outputs(st_ref[p], *later[p], z_ref[:, cols].reshape(cpb, CHUNK, HEAD_DIM), onw_ref[...])
            og_ref[:, cols] = og.reshape(tb, HEAD_DIM).astype(BF16)

    n_x = len(gather)
    grid = (nb, heads // PAIR)
    og, st, *gathered = pl.pallas_call(
        _with_exchange(body, 14, 2, True, n_x, grid),
        name=name,
        grid=grid,
        in_specs=[col(0), col(1), col(2), halo(0), halo(1), halo(2), col(3), abspec, convw(0), convw(1), convw(2), vec, vec, vec]
        + [_ANY] * n_x,
        out_specs=[pl.BlockSpec((tb, PAIR * HEAD_DIM), lambda i, h: (i, h)), states] + [_ANY] * n_x,
        out_shape=[jax.ShapeDtypeStruct((t, heads * HEAD_DIM), BF16),
                   jax.ShapeDtypeStruct((heads, t // CHUNK, HEAD_DIM, HEAD_DIM), F32)] + _chip_shapes(True, gather),
        scratch_shapes=[pltpu.VMEM((heads, HEAD_DIM, HEAD_DIM), F32), pltpu.VMEM((PAIR, 3, HALO + tb, HEAD_DIM), F32)]
        + _scan_scratch(cpb, BF16) * PAIR + (_chip_scratch(n_x) if n_x else []),
        compiler_params=_params("arbitrary", "arbitrary"),
    )(proj, proj, proj, proj, proj, proj, proj, ab, conv_w, conv_w, conv_w, alog_row, dtb_row, onw_row, *gather)
    return og, st, gathered


def _gdn_bwd(proj, ab, conv_w, alog_row, dtb_row, onw_row, states, dog, *, heads, name, tb=1024, pair=2, exchange=()):
    t = proj.shape[0]
    tb = min(tb, t)
    nb, cpb = t // tb, tb // CHUNK
    PAIR = min(pair, heads)
    _, col, halo, convw, vec, abspec, states_spec = _gdn_specs(heads, tb, True, nb, PAIR)
    n_conv = conv_w.shape[1]

    def body(q_ref, k_ref, v_ref, qh_ref, kh_ref, vh_ref, z_ref, ab_ref, wq_ref, wk_ref, wv_ref, alog_ref, dtb_ref, onw_ref,
             st_ref, dog_ref, dproj_ref, dab_ref, dconv_ref, dalog_ref, ddtb_ref, donw_ref,
             dstate_scr, x_scr, carry_scr, *scr):
        op_scr, dop_scr, dstates_scr, dc_scr = scr[:3 * PAIR], scr[3 * PAIR:6 * PAIR], scr[6 * PAIR:7 * PAIR], scr[7 * PAIR]
        i, pair = pl.program_id(0), pl.program_id(1)
        first_block = i == nb - 1
        heads_here, later = [pair * PAIR + p for p in range(PAIR)], []

        @pl.when(jnp.logical_and(i == 0, pair == 0))
        def _():
            dconv_ref[...] = jnp.zeros_like(dconv_ref)
            dalog_ref[...] = jnp.zeros_like(dalog_ref)
            ddtb_ref[...] = jnp.zeros_like(ddtb_ref)
            donw_ref[...] = jnp.zeros_like(donw_ref)

        @pl.when(pair == 0)
        def _():
            dab_ref[...] = jnp.zeros_like(dab_ref)

        @pl.when(i == 0)
        def _():
            for h in heads_here:
                dstate_scr[h] = jnp.zeros((HEAD_DIM, HEAD_DIM), F32)
                carry_scr[h] = jnp.zeros((3, HALO, HEAD_DIM), F32)

        abv = ab_ref[...]
        w_refs = (wq_ref, wk_ref, wv_ref)
        for p, h in enumerate(heads_here):
            cols = _head_cols(p)
            for n, (ref, href) in enumerate(((q_ref, qh_ref), (k_ref, kh_ref), (v_ref, vh_ref))):
                x_scr[p, n, 0:HALO, :] = jnp.where(first_block, 0.0, href[:, cols])
                x_scr[p, n, HALO:HALO + tb, :] = ref[:, cols]
            sel_a, sel_b = _head_lane(h), _head_lane(h, heads)
            alog, dtb = _pick(sel_a, alog_ref[...]), _pick(sel_a, dtb_ref[...])
            acts = [_silu(_conv(x_scr.at[p, n], w_ref[:, cols], tb)) for n, w_ref in enumerate(w_refs)]
            (*scan, read, out), vjp_intra = jax.vjp(_gdn_intra, *acts, _pick(sel_a, abv), _pick(sel_b, abv), alog, dtb)
            for s, val in zip(op_scr[3 * p:3 * p + 3], scan):
                s[...] = val.astype(s.dtype)
            blocked = lambda ref: ref[:, cols].reshape(cpb, CHUNK, HEAD_DIM)
            _, vjp_outputs = jax.vjp(_gdn_outputs, st_ref[p], read, out, blocked(z_ref), onw_ref[...])
            dstates_scr[p][...], dread, dout, dz, donw = vjp_outputs(blocked(dog_ref))
            dproj_ref[3, :, cols] = dz.reshape(tb, HEAD_DIM).astype(BF16)
            donw_ref[...] += donw
            later.append((vjp_intra, dread, dout, sel_a, sel_b))

        def chunk(i_rev, dstates):
            c = cpb - 1 - i_rev
            new = []
            for p in range(PAIR):
                _, vjp = jax.vjp(_gdn_scan_step, st_ref[p, c], *[s[c].astype(F32) for s in op_scr[3 * p:3 * p + 3]])
                dstate, *grads = vjp(dstates[p])
                for s, val in zip(dop_scr[3 * p:3 * p + 3], grads):
                    s[c] = val
                new.append(dstate + dstates_scr[p][c])
            return tuple(new)

        last = lax.fori_loop(0, cpb, chunk, tuple(dstate_scr[h] for h in heads_here))
        for p, h in enumerate(heads_here):
            cols = _head_cols(p)
            vjp_intra, dread, dout, sel_a, sel_b = later[p]
            dstate_scr[h] = last[p]
            *dacts, da, db, dalog, ddtb = vjp_intra((*[s[...] for s in dop_scr[3 * p:3 * p + 3]], dread, dout))
            dab_ref[...] += jnp.where(sel_a, da, 0.0) + jnp.where(sel_b, db, 0.0)
            for n, (dact, w_ref) in enumerate(zip(dacts, w_refs)):
                dx, dw = _conv_silu_bwd(x_scr.at[p, n], w_ref[:, cols], dact, dc_scr, tb)
                x_scr[p, n] = dx
                x_scr[p, n, tb:tb + HALO, :] += carry_scr[h, n]
                carry_scr[h, n] = x_scr[p, n, 0:HALO, :]
                dproj_ref[n, :, cols] = x_scr[p, n, HALO:HALO + tb, :].astype(BF16)
                lanes = pl.ds(pl.multiple_of((n * heads + h) * HEAD_DIM, HEAD_DIM), HEAD_DIM)
                for j in range(CONV_K):
                    dconv_ref[j:j + 1, lanes] += dw[j]
            dalog_ref[...] += jnp.where(sel_a, dalog, 0.0)
            ddtb_ref[...] += jnp.where(sel_a, ddtb, 0.0)

    dog_spec = pl.BlockSpec((tb, PAIR * HEAD_DIM), lambda i, h: (nb - 1 - i, h))
    dproj_spec = pl.BlockSpec((4, tb, PAIR * HEAD_DIM), lambda i, h: (0, nb - 1 - i, h))
    row_shape = jax.ShapeDtypeStruct((1, LANES), F32)
    n_x = len(exchange)
    grid = (nb, heads // PAIR)
    outs = pl.pallas_call(
        _with_exchange(body, 16, 6, False, n_x, grid),
        name=name,
        grid=grid,
        in_specs=[col(0), col(1), col(2), halo(0), halo(1), halo(2), col(3), abspec, convw(0), convw(1), convw(2), vec, vec, vec,
                  states_spec, dog_spec] + [_ANY] * n_x,
        out_specs=[dproj_spec, abspec, pl.BlockSpec((CONV_K, n_conv), lambda i, h: (0, 0)), vec, vec, vec] + [_ANY] * n_x,
        out_shape=[jax.ShapeDtypeStruct((4, t, heads * HEAD_DIM), BF16), jax.ShapeDtypeStruct((t, LANES), F32),
                   jax.ShapeDtypeStruct((CONV_K, n_conv), F32), row_shape, row_shape, row_shape] + _chip_shapes(False, exchange),
        scratch_shapes=[pltpu.VMEM((heads, HEAD_DIM, HEAD_DIM), F32), pltpu.VMEM((PAIR, 3, HALO + tb, HEAD_DIM), F32),
                        pltpu.VMEM((heads, 3, HALO, HEAD_DIM), F32)] + _scan_scratch(cpb, BF16) * PAIR
        + _scan_scratch(cpb, F32) * PAIR + [pltpu.VMEM((cpb, HEAD_DIM, HEAD_DIM), F32)] * PAIR
        + [pltpu.VMEM((HALO + tb + HALO, HEAD_DIM), F32)]
        + (_chip_scratch(n_x) if n_x else []),
        compiler_params=_params("arbitrary", "arbitrary", vmem=VMEM_LIMIT_WIDE_BYTES),
    )(proj, proj, proj, proj, proj, proj, proj, ab, conv_w, conv_w, conv_w, alog_row, dtb_row, onw_row, states, dog, *exchange)
    return (*outs[:6], outs[6:])


BAND = (LEFT_CHUNKS + 1) * CHUNK
PAD = LEFT_CHUNKS * CHUNK
GROUP = 2
ROWS = GROUP * CHUNK
WIN = (LEFT_CHUNKS + GROUP) * CHUNK
DIAGS = WIN + ROWS - 1
NEAR = PAD + ROWS - 1 - REL_CLIP
assert 0 < NEAR < DIAGS and WIN - PAD - 1 <= REL_CLIP and WIN % LANES == 0
ATTN_BLOCK = 2048
N_EDGE = PAD // ROWS


def _band_bias(rel_bias):
    heads = rel_bias.shape[0]
    far = jnp.broadcast_to(rel_bias[:, 2 * REL_CLIP:], (heads, NEAR + 1))
    near = rel_bias[:, 2 * REL_CLIP + NEAR + 1 - DIAGS:2 * REL_CLIP][:, ::-1]
    diag = jnp.concatenate([far, near], axis=1)
    return jnp.stack([diag[:, ROWS - 1 - r:ROWS - 1 - r + WIN] for r in range(ROWS)], axis=1)


def _band_bias_grad(dbias):
    heads = dbias.shape[0]
    diag = sum(jnp.pad(dbias[:, r, :], ((0, 0), (ROWS - 1 - r, r))) for r in range(ROWS))
    far = jnp.sum(diag[:, :NEAR + 1], axis=1, keepdims=True)
    near = diag[:, NEAR + 1:][:, ::-1]
    unused = jnp.zeros((heads, 2 * REL_CLIP - near.shape[1]), F32)
    return jnp.concatenate([unused, near, far], axis=1)


def _masked_bias(bias, n):
    r = np.arange(ROWS)[:, None]
    key = np.arange(WIN)[None, :]
    band_start = (r // CHUNK) * CHUNK
    in_band = np.logical_and(key >= band_start, key < band_start + BAND)
    in_sequence = key[None] >= PAD - np.arange(n)[:, None, None] * ROWS
    first = jnp.where(np.logical_and(in_band[None], in_sequence)[None], bias[:, None], -1e30)
    return first, jnp.where(in_band[None, None], bias[:, None], -1e30)


def _attn_groups(q_pre, z, kn, v, bias, qnw):
    q = _rms(q_pre, qnw)
    s = _bdot(q, kn, _BNT) * (HEAD_DIM ** -0.5) + bias
    p = jnp.exp(s - jnp.max(s, axis=-1, keepdims=True))
    p = p / jnp.sum(p, axis=-1, keepdims=True)
    return _bdot(p, v, _BNN) * _silu(z)


def _attn_groups_bwd(q_pre, z, kn, v, bias, qnw, dog):
    scale = HEAD_DIM ** -0.5
    inv_rms = lax.rsqrt(jnp.mean(q_pre * q_pre, axis=-1, keepdims=True) + EPS)
    q_hat = q_pre * inv_rms
    q_b = (q_hat * qnw).astype(BF16)
    s = _dot(q_b, kn, _BNT) * scale + bias
    e = jnp.exp(s - jnp.max(s, axis=-1, keepdims=True))
    p = e * (1.0 / jnp.sum(e, axis=-1, keepdims=True))
    p_b = p.astype(BF16)
    o = _dot(p_b, v, _BNN)
    sig = jax.nn.sigmoid(z)
    do = dog * (z * sig)
    dz = dog * o * (sig * (1.0 + z * (1.0 - sig)))
    do_b = do.astype(BF16)
    dv = _dot(p_b, do_b, _BTN)
    dp = _dot(do_b, v, _BNT)
    ds = p * (dp - jnp.sum(do * o, axis=-1, keepdims=True))
    ds_b = (ds * scale).astype(BF16)
    dq = _dot(ds_b, kn, _BNN)
    dkn = _dot(ds_b, q_b, _BTN)
    dqnw = jnp.sum(jnp.sum(dq * q_hat, axis=0), axis=0, keepdims=True)
    dq_hat = dq * qnw
    dq_pre = inv_rms * (dq_hat - q_hat * jnp.mean(dq_hat * q_hat, axis=-1, keepdims=True))
    return dq_pre, dz, dkn, dv, jnp.sum(ds, axis=0), dqnw


def _attn_specs(heads, tb, t, single_kv=False):
    def col(group):
        return pl.BlockSpec((tb, HEAD_DIM), lambda h, i: (i, group * heads + h))

    def full(group):
        mode = dict(pipeline_mode=pl.Buffered(1)) if single_kv else {}
        return pl.BlockSpec((t, HEAD_DIM), lambda h, i: (0, group * heads + h), **mode)

    bias = [pl.BlockSpec((1, min(tb // ROWS, N_EDGE), ROWS, WIN), lambda h, i: (h, 0, 0, 0)),
            pl.BlockSpec((1, 1, ROWS, WIN), lambda h, i: (h, 0, 0, 0))]
    vec = pl.BlockSpec((1, HEAD_DIM), lambda h, i: (0, 0))
    return col, full, bias, vec


def _attn_windows(scr, block_start, n):
    return jnp.stack([scr[pl.ds(pl.multiple_of(block_start + g * ROWS, ROWS), WIN), :] for g in range(n)])


def _attn_fill(k_ref, v_ref, knw_ref, kn_scr, v_scr, t):
    kn_scr[0:PAD, :] = jnp.zeros((PAD, HEAD_DIM), BF16)
    v_scr[0:PAD, :] = jnp.zeros((PAD, HEAD_DIM), BF16)
    step = min(512, t)

    def fill(j, _):
        rows = pl.ds(pl.multiple_of(j * step, step), step)
        prows = pl.ds(pl.multiple_of(PAD + j * step, CHUNK), step)
        kn_scr[prows, :] = _rms(k_ref[rows, :], knw_ref[...]).astype(BF16)
        v_scr[prows, :] = v_ref[rows, :].astype(BF16)
        return 0

    lax.fori_loop(0, t // step, fill, 0)


def _attn_fwd(proj, bias, qnw_row, knw_row, *, heads, name, tb=ATTN_BLOCK):
    t = proj.shape[0]
    tb = min(tb, t)
    nb, ng = t // tb, tb // ROWS
    col, full, bias_spec, vec = _attn_specs(heads, tb, t)

    def body(q_ref, k_ref, v_ref, z_ref, first_ref, rest_ref, qnw_ref, knw_ref, og_ref, kn_scr, v_scr):
        i = pl.program_id(1)

        @pl.when(i == 0)
        def _():
            _attn_fill(k_ref, v_ref, knw_ref, kn_scr, v_scr, t)

        def run(block_bias):
            start = i * tb
            og = _attn_groups(q_ref[...].reshape(ng, ROWS, HEAD_DIM), z_ref[...].reshape(ng, ROWS, HEAD_DIM),
                              _attn_windows(kn_scr, start, ng), _attn_windows(v_scr, start, ng), block_bias, qnw_ref[...])
            og_ref[...] = og.reshape(tb, HEAD_DIM).astype(BF16)

        def first_bias():
            edge = first_ref[0]
            more = ng - edge.shape[0]
            return edge if more == 0 else jnp.concatenate([edge, jnp.broadcast_to(rest_ref[0], (more, ROWS, WIN))])

        pl.when(i == 0)(lambda: run(first_bias()))
        pl.when(i > 0)(lambda: run(rest_ref[0]))

    return pl.pallas_call(
        body,
        name=name,
        grid=(heads, nb),
        in_specs=[col(0), full(1), full(2), col(3), *bias_spec, vec, vec],
        out_specs=pl.BlockSpec((tb, HEAD_DIM), lambda h, i: (i, h)),
        out_shape=jax.ShapeDtypeStruct((t, heads * HEAD_DIM), BF16),
        scratch_shapes=[pltpu.VMEM((PAD + t, HEAD_DIM), BF16), pltpu.VMEM((PAD + t, HEAD_DIM), BF16)],
        compiler_params=_params("arbitrary", "arbitrary"),
    )(proj, proj, proj, proj, *bias, qnw_row, knw_row)


def _attn_bwd(proj, bias, qnw_row, knw_row, dog, *, heads, name, tb=ATTN_BLOCK, sub=4):
    t = proj.shape[0]
    tb = min(tb, t)
    nb, ng = t // tb, tb // ROWS
    sub = min(sub, ng)
    n_edge = min(ng, N_EDGE)
    assert n_edge % sub == 0
    col, full, bias_spec, vec = _attn_specs(heads, tb, t, single_kv=True)

    def body(q_ref, k_ref, v_ref, z_ref, first_ref, rest_ref, qnw_ref, knw_ref, dog_ref,
             dqz_ref, dkv_ref, dbias_ref, dqnw_ref, dknw_ref, kn_scr, v_scr, dkn_scr, dv_scr):
        i = pl.program_id(1)

        @pl.when(i == 0)
        def _():
            _attn_fill(k_ref, v_ref, knw_ref, kn_scr, v_scr, t)
            dkn_scr[...] = jnp.zeros_like(dkn_scr)
            dv_scr[...] = jnp.zeros_like(dv_scr)
            dbias_ref[...] = jnp.zeros_like(dbias_ref)
            dqnw_ref[...] = jnp.zeros_like(dqnw_ref)

        def run(block_bias):
            for g0 in range(0, ng, sub):
                rows = pl.ds(g0 * ROWS, sub * ROWS)
                at = i * tb + g0 * ROWS
                blocked = lambda ref: ref[rows, :].reshape(sub, ROWS, HEAD_DIM)
                dq, dz, dkn, dv, dbias, dqnw = _attn_groups_bwd(
                    blocked(q_ref), blocked(z_ref), _attn_windows(kn_scr, at, sub), _attn_windows(v_scr, at, sub),
                    block_bias(g0), qnw_ref[...], blocked(dog_ref))
                dqz_ref[0, rows, :] = dq.reshape(sub * ROWS, HEAD_DIM).astype(BF16)
                dqz_ref[1, rows, :] = dz.reshape(sub * ROWS, HEAD_DIM).astype(BF16)
                for g in range(sub):
                    window = pl.ds(pl.multiple_of(at + g * ROWS, ROWS), WIN)
                    dkn_scr[window, :] += dkn[g]
                    dv_scr[window, :] += dv[g]
                dbias_ref[0] += dbias
                dqnw_ref[0] += dqnw

        pl.when(i == 0)(lambda: run(lambda g0: first_ref[0, g0:g0 + sub] if g0 + sub <= n_edge else rest_ref[0]))
        pl.when(i > 0)(lambda: run(lambda g0: rest_ref[0]))

        @pl.when(i == nb - 1)
        def _():
            step = min(512, t)

            def finish(j, dknw):
                rows = pl.ds(pl.multiple_of(j * step, step), step)
                prows = pl.ds(pl.multiple_of(PAD + j * step, CHUNK), step)
                _, vjp = jax.vjp(_rms, k_ref[rows, :], knw_ref[...])
                dk, dw = vjp(dkn_scr[prows, :])
                dkv_ref[0, rows, :] = dk.astype(BF16)
                dkv_ref[1, rows, :] = dv_scr[prows, :].astype(BF16)
                return dknw + dw

            dknw_ref[0] = lax.fori_loop(0, t // step, finish, jnp.zeros((1, HEAD_DIM), F32))

    pair_col = pl.BlockSpec((2, tb, HEAD_DIM), lambda h, i: (0, i, h))
    pair_full = pl.BlockSpec((2, t, HEAD_DIM), lambda h, i: (0, 0, h))
    head_vec = pl.BlockSpec((1, 1, HEAD_DIM), lambda h, i: (h, 0, 0))
    pair_shape = jax.ShapeDtypeStruct((2, t, heads * HEAD_DIM), BF16)
    vec_shape = jax.ShapeDtypeStruct((heads, 1, HEAD_DIM), F32)
    return pl.pallas_call(
        body,
        name=name,
        grid=(heads, nb),
        in_specs=[col(0), full(1), full(2), col(3), *bias_spec, vec, vec, pl.BlockSpec((tb, HEAD_DIM), lambda h, i: (i, h))],
        out_specs=[pair_col, pair_full, pl.BlockSpec((1, ROWS, WIN), lambda h, i: (h, 0, 0)), head_vec, head_vec],
        out_shape=[pair_shape, pair_shape, jax.ShapeDtypeStruct((heads, ROWS, WIN), F32), vec_shape, vec_shape],
        scratch_shapes=[pltpu.VMEM((PAD + t, HEAD_DIM), BF16), pltpu.VMEM((PAD + t, HEAD_DIM), BF16),
                        pltpu.VMEM((PAD + t, HEAD_DIM), F32), pltpu.VMEM((PAD + t, HEAD_DIM), F32)],
        compiler_params=_params("arbitrary", "arbitrary"),
    )(proj, proj, proj, proj, *bias, qnw_row, knw_row, dog)


def _lane_row(v):
    v = v.reshape(1, -1)
    return jnp.pad(v, ((0, 0), (0, LANES - v.shape[1])))


def _local_step(x, target, norm_w, wa_in, conv_w, a_log, dt_bias, onw, wa_out, wb_in, qnw, knw, rel_bias, wb_out, *,
                sharded=False):
    ha, hb = a_log.shape[-1], rel_bias.shape[-2]
    na = 4 * ha * HEAD_DIM
    wa_ab = jnp.pad(wa_in[:, na:], ((0, 0), (0, LANES - 2 * ha)))
    alog_row, dtb_row, onw_row = _lane_row(a_log), _lane_row(dt_bias), _lane_row(onw)
    qnw_row, knw_row = _lane_row(qnw), _lane_row(knw)
    bias = _masked_bias(_band_bias(rel_bias.reshape(hb, -1)), min(min(ATTN_BLOCK, x.shape[0]) // ROWS, N_EDGE))

    hn0, ab_a = _rmsnorm_fwd(x, norm_w[0:1], wa_ab, name="norm0")
    proj_a = _matmul(hn0, wa_in, n_cols=na, name="a_in")
    og_a, states, got = _gdn_fwd(proj_a, ab_a, conv_w, alog_row, dtb_row, onw_row, heads=ha, name="gdn_fwd",
                                 gather=[wb_in, wa_out, wb_out] if sharded else [])
    if sharded:
        wb_in, wa_out, wb_out = _join_cols(got[0]), got[1].reshape(-1, got[1].shape[-1]), got[2].reshape(-1, got[2].shape[-1])
    h1, hn1 = _matmul_norm(og_a, wa_out, x, norm_w[1:2], name="a_out_norm1")
    proj_b = _matmul(hn1, wb_in, name="b_in")
    og_b = _attn_fwd(proj_b, bias, qnw_row, knw_row, heads=hb, name="attn_fwd")
    loss, dh2, dh2_b = _matmul_loss(og_b, wb_out, h1, target, name="b_out_loss")

    grad_dtype = BF16 if sharded else F32
    dog_b = _matmul(dh2_b, wb_out, trans_b=True, name="d_b_out_x")
    dwb_out = _matmul(og_b, dh2_b, trans_a=True, out_dtype=grad_dtype, name="d_b_out_w")
    dqz, dkv, dbias, dqnw, dknw = _attn_bwd(proj_b, bias, qnw_row, knw_row, dog_b, heads=hb, name="attn_bwd")
    dproj_b, qkvz = [dqz, dkv], (0, 3, 1, 2)
    dhn1 = _matmul(dproj_b, wb_in, trans_b=True, order=qkvz, name="d_b_in_x")
    dwb_in = _matmul(hn1, dproj_b, trans_a=True, order=qkvz, out_dtype=grad_dtype, col_slabs=N_CHIPS if sharded else 0,
                     name="d_b_in_w")
    dh1, dh1_b, dnw1 = _rmsnorm_bwd(h1, norm_w[1:2], dhn1, dh2, name="d_norm1")

    dog_a = _matmul(dh1_b, wa_out, trans_b=True, name="d_a_out_x")
    dwa_out = _matmul(og_a, dh1_b, trans_a=True, out_dtype=grad_dtype, name="d_a_out_w")
    early = [dwb_in, _split_rows(dwa_out), _split_rows(dwb_out)] if sharded else []
    dproj_a, dab, dconv, dalog, ddtb, donw, landed = _gdn_bwd(
        proj_a, ab_a, conv_w, alog_row, dtb_row, onw_row, states, dog_a, heads=ha, name="gdn_bwd", exchange=early)
    dab_b = dab.astype(BF16)
    if sharded:
        mine = [_sum_slots(s, name=f"chip_sum_{n}") for n, s in zip(("b_w_in", "a_w_out", "b_w_out"), landed)]
        dwa_main, theirs = _matmul(hn0, dproj_a, trans_a=True, out_dtype=grad_dtype, name="d_a_in_w", exchange=mine,
                                   with_pair=True)
        dwb_in, dwa_out, dwb_out = zip(mine, theirs)
    else:
        dwa_main = _matmul(hn0, dproj_a, trans_a=True, out_dtype=grad_dtype, name="d_a_in_w")
    dwa_in = jnp.concatenate(
        [dwa_main, _matmul(hn0, dab_b, trans_a=True, out_dtype=grad_dtype, name="d_a_in_ab_w")[:, :2 * ha]], axis=1)
    if sharded:
        dhn0, (dwa_in, dconv) = _matmul(dproj_a, wa_in, trans_b=True, name="d_a_in_x",
                                        exchange=[_split_cols(dwa_in), _split_cols(dconv)])
    else:
        dhn0 = _matmul(dproj_a, wa_in, trans_b=True, name="d_a_in_x")
    dx, _, dnw0 = _rmsnorm_bwd(x, norm_w[0:1], dhn0, dh1, narrow=(dab_b, wa_ab), name="d_norm0")

    drel = _band_bias_grad(dbias)
    grads = dict(
        norm_w=jnp.concatenate([dnw0, dnw1], axis=0), a_w_in=dwa_in, a_conv_w=dconv, a_a_log=dalog[:, :ha],
        a_dt_bias=ddtb[:, :ha], a_out_norm_w=donw, a_w_out=dwa_out, b_w_in=dwb_in, b_q_norm_w=jnp.sum(dqnw, axis=0),
        b_k_norm_w=jnp.sum(dknw, axis=0), b_rel_bias=drel[None], b_w_out=dwb_out)
    return loss, dx, grads


_ANY = pl.BlockSpec(memory_space=pl.ANY)
_CHIP_FLIPS = ((1, 0), (0, 1), (1, 1))


def _place():
    x, y, c = lax.axis_index("x"), lax.axis_index("y"), lax.axis_index("c")
    return x, y, c


def _flip(v, bit):
    return 1 - v if bit else v


def _remote(src, dst, send_sem, recv_sem, peer):
    return pltpu.make_async_remote_copy(src_ref=src, dst_ref=dst, send_sem=send_sem, recv_sem=recv_sem, device_id=peer,
                                        device_id_type=MESH)


def _comm_call(body, arrays, out_shapes, n_remote, n_local, name):
    scratch = [pltpu.SemaphoreType.DMA((n_remote,)), pltpu.SemaphoreType.DMA((n_remote,))]
    if n_local:
        scratch.append(pltpu.SemaphoreType.DMA((n_local,)))
    return pl.pallas_call(
        body, name=name, in_specs=[_ANY] * len(arrays), out_specs=[_ANY] * len(out_shapes), out_shape=out_shapes,
        scratch_shapes=scratch)(*arrays)


def _chip_scratch(n):
    return [pltpu.SemaphoreType.DMA((3 * n,)), pltpu.SemaphoreType.DMA((3 * n,)), pltpu.SemaphoreType.DMA((n,))]


def _chip_shapes(gather, arrays):
    return [jax.ShapeDtypeStruct(((N_CHIPS,) + s.shape) if gather else s.shape, s.dtype) for s in arrays]


def _chip_traffic(gather, ins, outs, sems):
    send_sems, recv_sems, local_sems = sems
    x, y, c = _place()
    mine = 2 * x + y
    local, remote, landing = [], [], []
    for a in range(len(ins)):
        local.append(pltpu.make_async_copy(ins[a] if gather else ins[a].at[mine], outs[a].at[mine], local_sems.at[a]))
        for k, (fx, fy) in enumerate(_CHIP_FLIPS):
            peer = (_flip(x, fx), _flip(y, fy), c)
            theirs = 2 * peer[0] + peer[1]
            src = ins[a] if gather else ins[a].at[theirs]
            pair = send_sems.at[3 * a + k], recv_sems.at[3 * a + k]
            remote.append(_remote(src, outs[a].at[mine], *pair, peer))
            landing.append(_remote(src, outs[a].at[theirs], *pair, peer))
    return local + remote, (local, landing, remote)


def _start(traffic):
    for cp in traffic[0]:
        cp.start()


def _finish(traffic):
    local, landing, remote = traffic[1]
    for cp in local:
        cp.wait()
    for cp in landing:
        cp.wait_recv()
    for cp in remote:
        cp.wait_send()


def _pair_scratch(n):
    return [pltpu.SemaphoreType.DMA((n,)), pltpu.SemaphoreType.DMA((n,))]


def _pair_traffic(ins, outs, sems):
    send_sems, recv_sems = sems
    x, y, c = _place()
    copies = [_remote(ins[a], outs[a], send_sems.at[a], recv_sems.at[a], (x, y, 1 - c)) for a in range(len(ins))]
    return copies, ([], copies, copies)


def _with_exchange(compute, n_in, n_out, gather, n_x, grid):
    if not n_x:
        return compute

    def body(*refs):
        ins, x_in = refs[:n_in], refs[n_in:n_in + n_x]
        outs, x_out = refs[n_in + n_x:n_in + n_x + n_out], refs[n_in + n_x + n_out:n_in + 2 * n_x + n_out]
        n_sems = 2 if gather == "pair" else 3
        scratch, sems = refs[n_in + 2 * n_x + n_out:-n_sems], refs[-n_sems:]
        traffic = _pair_traffic(x_in, x_out, sems) if gather == "pair" else _chip_traffic(gather, x_in, x_out, sems)
        first = functools.reduce(jnp.logical_and, [pl.program_id(d) == 0 for d in range(len(grid))])
        last = functools.reduce(jnp.logical_and, [pl.program_id(d) == grid[d] - 1 for d in range(len(grid))])

        @pl.when(first)
        def _():
            _start(traffic)

        compute(*ins, *outs, *scratch)

        @pl.when(last)
        def _():
            _finish(traffic)

    return body


def _gather_shared(shard, small, *, name):
    rows = shard.shape[0]
    assert rows % 2 == 0
    half = rows // 2

    def body(shard_ref, small_ref, out_ref, small_out_ref, send_sems, recv_sems, local_sems):
        x, y, c = _place()
        mine = 2 * x + y
        sibling = (x, y, 1 - c)
        my_rows = pl.ds(pl.multiple_of(c * half, 8), half)
        local = [pltpu.make_async_copy(shard_ref, out_ref.at[mine], local_sems.at[0]),
                 pltpu.make_async_copy(small_ref, small_out_ref.at[mine], local_sems.at[1])]
        sent, landed, passed_on, handed = [], [], [], []
        for k, (fx, fy) in enumerate(_CHIP_FLIPS):
            peer = (_flip(x, fx), _flip(y, fy), c)
            theirs = 2 * peer[0] + peer[1]
            ici, d2d, tiny = [(send_sems.at[3 * n + k], recv_sems.at[3 * n + k]) for n in range(3)]
            sent.append(_remote(shard_ref.at[my_rows], out_ref.at[mine, my_rows], *ici, peer))
            landed.append(_remote(shard_ref.at[my_rows], out_ref.at[theirs, my_rows], *ici, peer))
            sent.append(_remote(small_ref, small_out_ref.at[mine], *tiny, peer))
            landed.append(_remote(small_ref, small_out_ref.at[theirs], *tiny, peer))
            passed_on.append(_remote(out_ref.at[theirs, my_rows], out_ref.at[theirs, my_rows], *d2d, sibling))
            other_rows = pl.ds(pl.multiple_of((1 - c) * half, 8), half)
            handed.append(_remote(out_ref.at[theirs, other_rows], out_ref.at[theirs, other_rows], *d2d, sibling))
        for cp in local + sent:
            cp.start()
        for k in range(3):
            landed[2 * k].wait_recv()
            passed_on[k].start()
        for k in range(3):
            landed[2 * k + 1].wait_recv()
            handed[k].wait_recv()
        for cp in local:
            cp.wait()
        for cp in sent + passed_on:
            cp.wait_send()

    return pl.pallas_call(
        body, name=name, in_specs=[_ANY] * 2, out_specs=[_ANY] * 2, out_shape=_chip_shapes(True, [shard, small]),
        scratch_shapes=[pltpu.SemaphoreType.DMA((9,)), pltpu.SemaphoreType.DMA((9,)), pltpu.SemaphoreType.DMA((2,))],
    )(shard, small)


def _swap_pair(arrays, *, name):
    n = len(arrays)

    def body(*refs):
        ins, outs, (send_sems, recv_sems) = refs[:n], refs[n:2 * n], refs[2 * n:]
        x, y, c = _place()
        copies = [_remote(ins[a], outs[a], send_sems.at[a], recv_sems.at[a], (x, y, 1 - c)) for a in range(n)]
        for cp in copies:
            cp.start()
        for cp in copies:
            cp.wait_recv()
        for cp in copies:
            cp.wait_send()

    shapes = [jax.ShapeDtypeStruct(s.shape, s.dtype) for s in arrays]
    return _comm_call(body, arrays, shapes, n, 0, name)


def _gather_all(tile, *, name):
    def body(in_ref, out_ref, send_sems, recv_sems, local_sems):
        x, y, c = _place()
        mine = 4 * x + 2 * y + c
        local = pltpu.make_async_copy(in_ref, out_ref.at[mine], local_sems.at[0])
        remote, landing = [], []
        for k in range(1, N_DEV):
            peer = (_flip(x, k & 4), _flip(y, k & 2), _flip(c, k & 1))
            sems = send_sems.at[k - 1], recv_sems.at[k - 1]
            remote.append(_remote(in_ref, out_ref.at[mine], *sems, peer))
            landing.append(_remote(in_ref, out_ref.at[4 * peer[0] + 2 * peer[1] + peer[2]], *sems, peer))
        for cp in [local] + remote:
            cp.start()
        local.wait()
        for cp in landing:
            cp.wait_recv()
        for cp in remote:
            cp.wait_send()

    return _comm_call(body, [tile], [jax.ShapeDtypeStruct((N_DEV,) + tile.shape, tile.dtype)], N_DEV - 1, 1, name)[0]


def _sum_slots(slabs, *, name, tr=128):
    s, r, c = slabs.shape
    tr = min(tr, r)

    def body(in_ref, o_ref):
        acc = in_ref[0].astype(F32)
        for j in range(1, s):
            acc = acc + in_ref[j].astype(F32)
        o_ref[...] = acc

    return pl.pallas_call(
        body, name=name, grid=(r // tr,),
        in_specs=[pl.BlockSpec((s, tr, c), lambda i: (0, i, 0))], out_specs=pl.BlockSpec((tr, c), lambda i: (i, 0)),
        out_shape=jax.ShapeDtypeStruct((r, c), F32), compiler_params=_params("parallel"))(slabs)


def _adamw_math(w, g, m, v):
    m = ADAM_B1 * m + (1.0 - ADAM_B1) * g
    v = ADAM_B2 * v + (1.0 - ADAM_B2) * (g * g)
    m_hat = m / (1.0 - ADAM_B1 ** ADAM_STEP)
    v_hat = v / (1.0 - ADAM_B2 ** ADAM_STEP)
    delta = -ADAM_LR * (m_hat / (jnp.sqrt(v_hat) + ADAM_EPS) + ADAM_WD * w)
    return delta, m, v


def _adamw(w, m, v, parts, *, name, tr=128):
    r, c = w.shape
    tr = min(tr, r)
    s = len(parts)

    def body(w_ref, m_ref, v_ref, *refs):
        g_ref, d_ref, nm_ref, nv_ref = refs[s:]
        g = refs[0][...]
        for p_ref in refs[1:s]:
            g = g + p_ref[...]
        g_ref[...] = g
        d_ref[...], nm_ref[...], nv_ref[...] = _adamw_math(w_ref[...], g, m_ref[...], v_ref[...])

    blk = pl.BlockSpec((tr, c), lambda i: (i, 0))
    shape = jax.ShapeDtypeStruct((r, c), F32)
    return pl.pallas_call(
        body, name=name, grid=(r // tr,), in_specs=[blk] * (3 + s), out_specs=[blk] * 4, out_shape=[shape] * 4,
        compiler_params=_params("parallel"))(w, m, v, *parts)


_BIG = ("a_w_in", "b_w_in", "a_w_out", "b_w_out", "a_conv_w")
_SMALL = ("norm_w", "a_a_log", "a_dt_bias", "a_out_norm_w", "b_q_norm_w", "b_k_norm_w", "b_rel_bias")
_ORDER = ("norm_w", "a_w_in", "a_conv_w", "a_a_log", "a_dt_bias", "a_out_norm_w", "a_w_out", "b_w_in", "b_q_norm_w",
          "b_k_norm_w", "b_rel_bias", "b_w_out")


def _join_cols(g):
    return jnp.transpose(g, (1, 0, 2)).reshape(g.shape[1], -1)


def _split_cols(g):
    return jnp.transpose(g.reshape(g.shape[0], N_CHIPS, -1), (1, 0, 2))


def _split_rows(g):
    return g.reshape(N_CHIPS, -1, g.shape[-1])


PACK_ROWS = 8


def _pack(d):
    flat = jnp.concatenate([d[n].reshape(-1) for n in _SMALL])
    return jnp.pad(flat, (0, -flat.shape[0] % (PACK_ROWS * LANES))).reshape(PACK_ROWS, -1)


def _unpack(tile, like):
    flat, out, at = tile.reshape(-1), {}, 0
    for n in _SMALL:
        size = like[n].size
        out[n] = flat[at:at + size].reshape(like[n].shape)
        at += size
    return out


def kernel(x, norm_w, a_w_in, a_conv_w, a_a_log, a_dt_bias, a_out_norm_w, a_w_out, b_w_in, b_q_norm_w, b_k_norm_w, b_rel_bias, b_w_out, loss_target, m_norm_w, m_a_w_in, m_a_conv_w, m_a_a_log, m_a_dt_bias, m_a_out_norm_w, m_a_w_out, m_b_w_in, m_b_q_norm_w, m_b_k_norm_w, m_b_rel_bias, m_b_w_out, v_norm_w, v_a_w_in, v_a_conv_w, v_a_a_log, v_a_dt_bias, v_a_out_norm_w, v_a_w_out, v_b_w_in, v_b_q_norm_w, v_b_k_norm_w, v_b_rel_bias, v_b_w_out):
    w = dict(norm_w=norm_w, a_w_in=a_w_in, a_conv_w=a_conv_w, a_a_log=a_a_log, a_dt_bias=a_dt_bias,
             a_out_norm_w=a_out_norm_w, a_w_out=a_w_out, b_w_in=b_w_in, b_q_norm_w=b_q_norm_w, b_k_norm_w=b_k_norm_w,
             b_rel_bias=b_rel_bias, b_w_out=b_w_out)
    m = dict(norm_w=m_norm_w, a_w_in=m_a_w_in, a_conv_w=m_a_conv_w, a_a_log=m_a_a_log, a_dt_bias=m_a_dt_bias,
             a_out_norm_w=m_a_out_norm_w, a_w_out=m_a_w_out, b_w_in=m_b_w_in, b_q_norm_w=m_b_q_norm_w,
             b_k_norm_w=m_b_k_norm_w, b_rel_bias=m_b_rel_bias, b_w_out=m_b_w_out)
    v = dict(norm_w=v_norm_w, a_w_in=v_a_w_in, a_conv_w=v_a_conv_w, a_a_log=v_a_a_log, a_dt_bias=v_a_dt_bias,
             a_out_norm_w=v_a_out_norm_w, a_w_out=v_a_w_out, b_w_in=v_b_w_in, b_q_norm_w=v_b_q_norm_w,
             b_k_norm_w=v_b_k_norm_w, b_rel_bias=v_b_rel_bias, b_w_out=v_b_w_out)

    wa_in, conv = _gather_shared(a_w_in[0].astype(BF16), a_conv_w[0], name="gather_a_in")
    loss, dx, grads = _local_step(
        x[0], loss_target[0], norm_w, _join_cols(wa_in), _join_cols(conv), a_a_log, a_dt_bias, a_out_norm_w,
        a_w_out[0].astype(BF16), b_w_in[0].astype(BF16), b_q_norm_w, b_k_norm_w, b_rel_bias, b_w_out[0].astype(BF16),
        sharded=True)
    loss = lax.psum(loss, ("x", "y", "c"))

    late = [n for n in _BIG if not isinstance(grads[n], tuple)]
    mine = [_sum_slots(grads[n], name=f"chip_sum_{n}") for n in late]
    sums = {n: grads[n] for n in _BIG if n not in late}
    sums.update(zip(late, zip(mine, _swap_pair(mine, name="pair_grads"))))
    out = {}
    for n in _BIG:
        out[n] = [r[None] for r in _adamw(w[n][0], m[n][0], v[n][0], list(sums[n]), name=f"adamw_{n}")]

    tiles = _gather_all(_pack(grads), name="gather_small_grads")
    res = _adamw(_pack(w), _pack(m), _pack(v), [tiles[d] for d in range(N_DEV)], name="adamw_small")
    unpacked = [_unpack(r, w) for r in res]
    for n in _SMALL:
        out[n] = [u[n] for u in unpacked]

    return (loss, dx[None], *[out[n][0] for n in _ORDER], *[out[n][1] for n in _ORDER], *[out[n][2] for n in _ORDER],
            *[out[n][3] for n in _ORDER])
```

```python
import functools

import numpy as np
import jax
import jax.numpy as jnp
from jax import lax
from jax.experimental import pallas as pl
from jax.experimental.pallas import tpu as pltpu

F32 = jnp.float32
BF16 = jnp.bfloat16

CHUNK = 64
HEAD_DIM = 128
LEFT_CHUNKS = 8
REL_CLIP = 256
CONV_K = 4
EPS = 1e-6
HALO = 8

ADAM_LR = 0.001
ADAM_B1 = 0.9
ADAM_B2 = 0.999
ADAM_EPS = 1e-08
ADAM_WD = 0.01
ADAM_STEP = 10

LANES = 128
N_CHIPS = 4
N_DEV = 8
VMEM_LIMIT_BYTES = 56 * 1024 * 1024
VMEM_LIMIT_WIDE_BYTES = 63 * 1024 * 1024
MESH = pl.DeviceIdType.MESH


def _params(*sem, vmem=VMEM_LIMIT_BYTES):
    return pltpu.CompilerParams(dimension_semantics=sem, vmem_limit_bytes=vmem)


def _dot(a, b, dims=(((1,), (0,)), ((), ())), precision=None):
    return lax.dot_general(a, b, dims, precision=precision, preferred_element_type=F32)


_NT = (((1,), (1,)), ((), ()))
_TN = (((0,), (0,)), ((), ()))


def _bdot(a, b, dims=(((1,), (0,)), ((), ()))):
    return _dot(a.astype(BF16), b.astype(BF16), dims)


def _pdot(a, b, dims=(((1,), (0,)), ((), ()))):
    return _dot(a, b, dims)


def _fdot(a, b, dims=(((1,), (0,)), ((), ()))):
    return _dot(a, b, dims, precision=lax.Precision.HIGH)


def _silu(x):
    return x * jax.nn.sigmoid(x)


def _stacks(x):
    if not isinstance(x, (list, tuple)) and x.ndim != 3:
        return None
    arrays = list(x) if isinstance(x, (list, tuple)) else [x]
    assert len({(v.shape[1], v.shape[2], v.dtype) for v in arrays}) == 1
    starts = [sum(v.shape[0] for v in arrays[:r]) for r in range(len(arrays))]
    return arrays, starts, starts[-1] + arrays[-1].shape[0]


def _static_pick(table, index):
    out = table[-1]
    for s in range(len(table) - 2, -1, -1):
        out = jnp.where(index == s, table[s], out)
    return out


def _matmul(a, b, *, name, trans_a=False, trans_b=False, residual=None, out_dtype=F32, tm=1024, tn=1024, tk=2048,
            col_slabs=0, order=None, n_cols=0, exchange=(), with_pair=False):
    assert not (trans_a and trans_b)
    a_stack, b_stack = _stacks(a), _stacks(b)
    assert not (a_stack and (trans_a or b_stack)) and not (b_stack and trans_b)
    a_list, b_list = (a_stack[0] if a_stack else [a]), (b_stack[0] if b_stack else [b])
    a0, b0 = a_list[0], b_list[0]
    k, m = (a_stack[2] * a0.shape[2], a0.shape[1]) if a_stack else a.shape if trans_a else a.shape[::-1]
    n = n_cols or (b_stack[2] * b0.shape[2] if b_stack else b.shape[0] if trans_b else b.shape[1])
    tm, tn, tk = min(tm, m), min(tn, n // max(col_slabs, 1)), min(tk, k)
    if a_stack:
        tk = min(tk, a0.shape[2])
        per_k = a0.shape[2] // tk
    if b_stack:
        tn = min(tn, b0.shape[2])
        per_n = b0.shape[2] // tn
    assert m % tm == 0 and n % tn == 0 and k % tk == 0, (a0.shape, b0.shape, tm, tn, tk)
    nk = k // tk
    dims = _NT if trans_b else _TN if trans_a else (((1,), (0,)), ((), ()))
    order = list(order) if order is not None else list(range(max(a_stack[2] if a_stack else 0, b_stack[2] if b_stack else 0)))
    na, nb = len(a_list), len(b_list)

    def group_of(r, stack, position):
        arrays, starts, _ = stack
        local = position - starts[r]
        return jnp.logical_and(local >= 0, local < arrays[r].shape[0]), jnp.clip(local, 0, arrays[r].shape[0] - 1)

    def body(*refs):
        a_refs, b_refs = refs[:na], refs[na:na + nb]
        r_ref = refs[na + nb] if residual is not None else None
        o_ref, acc_ref = refs[-2:]
        j, kk = pl.program_id(1), pl.program_id(2)

        @pl.when(kk == 0)
        def _():
            acc_ref[...] = jnp.zeros_like(acc_ref)

        for ra, a_ref in enumerate(a_refs):
            for rb, b_ref in enumerate(b_refs):
                def add(a_ref=a_ref, b_ref=b_ref):
                    acc_ref[...] += _dot(a_ref[...], b_ref[...], dims)

                if na > 1:
                    pl.when(group_of(ra, a_stack, kk // per_k)[0])(add)
                elif nb > 1:
                    pl.when(group_of(rb, b_stack, j // per_n)[0])(add)
                else:
                    add()

        @pl.when(kk == nk - 1)
        def _():
            r = acc_ref[...]
            if r_ref is not None:
                r = r + r_ref[...]
            o_ref[...] = r.astype(o_ref.dtype)

    if a_stack:
        a_specs = [pl.BlockSpec((None, tm, tk), lambda i, j, kk, r=r: (group_of(r, a_stack, kk // per_k)[1], i, kk % per_k))
                   for r in range(na)]
        b_k = lambda kk: _static_pick(order, kk // per_k) * per_k + kk % per_k
    else:
        a_specs = [pl.BlockSpec((tk, tm), lambda i, j, kk: (kk, i)) if trans_a else pl.BlockSpec((tm, tk), lambda i, j, kk: (i, kk))]
        b_k = lambda kk: kk
    if b_stack:
        b_specs = [pl.BlockSpec((None, tk, tn), lambda i, j, kk, r=r: (group_of(r, b_stack, j // per_n)[1], kk, j % per_n))
                   for r in range(nb)]
        out_col = lambda j: _static_pick(order, j // per_n) * per_n + j % per_n
    else:
        b_specs = [pl.BlockSpec((tn, tk), lambda i, j, kk: (j, b_k(kk))) if trans_b
                   else pl.BlockSpec((tk, tn), lambda i, j, kk: (b_k(kk), j))]
        out_col = lambda j: j
    in_specs = a_specs + b_specs
    args = a_list + b_list
    if residual is not None:
        in_specs.append(pl.BlockSpec((tm, tn), lambda i, j, kk: (i, j)))
        args.append(residual)
    grid = (m // tm, n // tn, nk)
    n_x = len(exchange)
    if col_slabs:
        per = n // col_slabs // tn
        assert per * tn * col_slabs == n, (n, tn, col_slabs)
        out_spec = pl.BlockSpec((None, tm, tn), lambda i, j, kk: (out_col(j) // per, i, out_col(j) % per))
        out_shape = jax.ShapeDtypeStruct((col_slabs, m, n // col_slabs), out_dtype)
    else:
        out_spec = pl.BlockSpec((tm, tn), lambda i, j, kk: (i, out_col(j)))
        out_shape = jax.ShapeDtypeStruct((m, n), out_dtype)
    out, *landed = pl.pallas_call(
        _with_exchange(body, len(args), 1, "pair" if with_pair else False, n_x, grid),
        name=name,
        grid=grid,
        in_specs=in_specs + [_ANY] * n_x,
        out_specs=[out_spec] + [_ANY] * n_x,
        out_shape=[out_shape] + _chip_shapes(False, exchange),
        scratch_shapes=[pltpu.VMEM((tm, tn), F32)] + ((_pair_scratch if with_pair else _chip_scratch)(n_x) if n_x else []),
        compiler_params=_params(*(("arbitrary",) * 3 if n_x else ("parallel", "parallel", "arbitrary"))),
    )(*args, *exchange)
    return (out, landed) if n_x else out


def _rms(x, w):
    return x * lax.rsqrt(jnp.mean(x * x, axis=-1, keepdims=True) + EPS) * w


def _rmsnorm_fwd(x, w_row, narrow_w, *, name, tr=512):
    t, d = x.shape
    tr = min(tr, t)

    def body(x_ref, w_ref, nw_ref, o_ref, narrow_ref):
        hn = _rms(x_ref[...], w_ref[...]).astype(BF16)
        o_ref[...] = hn
        narrow_ref[...] = _dot(hn, nw_ref[...])

    return pl.pallas_call(
        body,
        name=name,
        grid=(t // tr,),
        in_specs=[pl.BlockSpec((tr, d), lambda i: (i, 0)), pl.BlockSpec((1, d), lambda i: (0, 0)),
                  pl.BlockSpec((d, LANES), lambda i: (0, 0))],
        out_specs=[pl.BlockSpec((tr, d), lambda i: (i, 0)), pl.BlockSpec((tr, LANES), lambda i: (i, 0))],
        out_shape=[jax.ShapeDtypeStruct((t, d), BF16), jax.ShapeDtypeStruct((t, LANES), F32)],
        compiler_params=_params("parallel"),
    )(x, w_row, narrow_w)


def _matmul_norm(a, b, residual, w_row, *, name, tm=512):
    t, k = a.shape
    d = b.shape[1]
    tm = min(tm, t)

    def body(a_ref, b_ref, r_ref, w_ref, h_ref, hn_ref):
        h = _dot(a_ref[...], b_ref[...]) + r_ref[...]
        h_ref[...] = h
        hn_ref[...] = _rms(h, w_ref[...]).astype(BF16)

    row = pl.BlockSpec((tm, d), lambda i: (i, 0))
    return pl.pallas_call(
        body,
        name=name,
        grid=(t // tm,),
        in_specs=[pl.BlockSpec((tm, k), lambda i: (i, 0)), pl.BlockSpec((k, d), lambda i: (0, 0)), row,
                  pl.BlockSpec((1, d), lambda i: (0, 0))],
        out_specs=[row, row],
        out_shape=[jax.ShapeDtypeStruct((t, d), F32), jax.ShapeDtypeStruct((t, d), BF16)],
        compiler_params=_params("parallel"),
    )(a, b, residual, w_row)


def _rmsnorm_bwd(x, w_row, dy, dres, *, name, tr=256, narrow=None):
    t, d = x.shape
    tr = min(tr, t)
    extra = list(narrow) if narrow is not None else []

    def body(x_ref, w_ref, dy_ref, dres_ref, *refs):
        dx_ref, dxb_ref, dw_ref = refs[len(extra):]

        @pl.when(pl.program_id(0) == 0)
        def _():
            dw_ref[...] = jnp.zeros_like(dw_ref)

        dy = dy_ref[...]
        if extra:
            dy = dy + _dot(refs[0][...], refs[1][...], _NT)
        _, vjp = jax.vjp(_rms, x_ref[...], w_ref[...])
        dx, dw = vjp(dy)
        dx = dx + dres_ref[...]
        dx_ref[...] = dx
        dxb_ref[...] = dx.astype(BF16)
        dw_ref[...] += dw

    row = pl.BlockSpec((tr, d), lambda i: (i, 0))
    vec = pl.BlockSpec((1, d), lambda i: (0, 0))
    extra_specs = [pl.BlockSpec((tr, LANES), lambda i: (i, 0)), pl.BlockSpec((d, LANES), lambda i: (0, 0))] if extra else []
    return pl.pallas_call(
        body,
        name=name,
        grid=(t // tr,),
        in_specs=[row, vec, row, row] + extra_specs,
        out_specs=[row, row, vec],
        out_shape=[jax.ShapeDtypeStruct((t, d), F32), jax.ShapeDtypeStruct((t, d), BF16), jax.ShapeDtypeStruct((1, d), F32)],
        compiler_params=_params("arbitrary"),
    )(x, w_row, dy, dres, *extra)


def _matmul_loss(a, b, residual, target, *, name, tm=512, tn=1024):
    t, k = a.shape
    d = b.shape[1]
    tm, tn = min(tm, t), min(tn, d)

    def body(a_ref, b_ref, r_ref, t_ref, dh_ref, dhb_ref, part_ref):
        @pl.when(pl.program_id(1) == 0)
        def _():
            part_ref[...] = jnp.zeros_like(part_ref)

        err = _dot(a_ref[...], b_ref[...]) + r_ref[...] - t_ref[...]
        dh = err * (1.0 / d)
        dh_ref[...] = dh
        dhb_ref[...] = dh.astype(BF16)
        part_ref[...] += jnp.sum(err * err, axis=0, keepdims=True)

    tile = pl.BlockSpec((tm, tn), lambda j, i: (i, j))
    dh, dhb, part = pl.pallas_call(
        body,
        name=name,
        grid=(d // tn, t // tm),
        in_specs=[pl.BlockSpec((tm, k), lambda j, i: (i, 0)), pl.BlockSpec((k, tn), lambda j, i: (0, j)), tile, tile],
        out_specs=[tile, tile, pl.BlockSpec((1, tn), lambda j, i: (0, j))],
        out_shape=[jax.ShapeDtypeStruct((t, d), F32), jax.ShapeDtypeStruct((t, d), BF16), jax.ShapeDtypeStruct((1, d), F32)],
        compiler_params=_params("arbitrary", "arbitrary"),
    )(a, b, residual, target)
    return 0.5 / d * jnp.sum(part), dh, dhb


_BNN = (((2,), (1,)), ((0,), (0,)))
_BNT = (((2,), (2,)), ((0,), (0,)))
_BTN = (((1,), (1,)), ((0,), (0,)))


_TAP0 = HALO - (CONV_K - 1)


def _conv(x_ref, w, rows):
    c = w[0:1, :] * x_ref[_TAP0:_TAP0 + rows, :]
    for j in range(1, CONV_K):
        c = c + w[j:j + 1, :] * x_ref[_TAP0 + j:_TAP0 + j + rows, :]
    return c


def _conv_silu_bwd(x_ref, w, dact, dc_ref, rows):
    c = _conv(x_ref, w, rows)
    sig = jax.nn.sigmoid(c)
    dc = dact * (sig * (1.0 + c * (1.0 - sig)))
    dw = [jnp.sum(dc * x_ref[_TAP0 + j:_TAP0 + j + rows, :], axis=0, keepdims=True) for j in range(CONV_K)]
    dc_ref[0:HALO, :] = jnp.zeros((HALO, HEAD_DIM), F32)
    dc_ref[HALO:HALO + rows, :] = dc
    dc_ref[HALO + rows:HALO + rows + HALO, :] = jnp.zeros((HALO, HEAD_DIM), F32)
    first = HALO - _TAP0
    dx = w[0:1, :] * dc_ref[first:first + HALO + rows, :]
    for j in range(1, CONV_K):
        dx = dx + w[j:j + 1, :] * dc_ref[first - j:first - j + HALO + rows, :]
    return dx, dw


@jax.custom_vjp
def _unit_lower_inverse(neg_l):
    n = neg_l.shape[0]
    eye = (lax.broadcasted_iota(jnp.int32, (n, CHUNK, CHUNK), 1) == lax.broadcasted_iota(jnp.int32, (n, CHUNK, CHUNK), 2))
    inv = eye.astype(F32) + neg_l
    power = _bdot(neg_l, neg_l, _BNN)
    for _ in range(4):
        both = _bdot(jnp.concatenate([inv, power], axis=1), power, _BNN)
        inv, power = inv + both[:, :CHUNK], both[:, CHUNK:]
    return inv + _bdot(inv, power, _BNN)


def _unit_lower_inverse_fwd(neg_l):
    inv = _unit_lower_inverse(neg_l)
    return inv, inv


def _unit_lower_inverse_bwd(inv, dinv):
    return (_fdot(_fdot(inv, dinv, _BTN), inv, _BNT),)


_unit_lower_inverse.defvjp(_unit_lower_inverse_fwd, _unit_lower_inverse_bwd)


def _gdn_intra(qt, kt, v, a, b, alog, dtb):
    n = a.shape[0] // CHUNK
    q = qt * lax.rsqrt(jnp.sum(qt * qt, axis=-1, keepdims=True) + EPS) * (HEAD_DIM ** -0.5)
    k = kt * lax.rsqrt(jnp.sum(kt * kt, axis=-1, keepdims=True) + EPS)
    lanes = jnp.ones((1, HEAD_DIM), F32)
    beta = jax.nn.sigmoid(b) * lanes
    sp = a + dtb
    g = (-jnp.exp(alog) * (jnp.maximum(sp, 0.0) + jnp.log(1.0 + jnp.exp(-jnp.abs(sp))))) * lanes
    q, k, v, beta, g = (t.reshape(n, CHUNK, HEAD_DIM) for t in (q, k, v, beta, g))

    row = lax.broadcasted_iota(jnp.int32, (n, CHUNK, CHUNK), 1)
    col = lax.broadcasted_iota(jnp.int32, (n, CHUNK, CHUNK), 2)
    tri_incl = row >= col
    tri_strict = row > col
    gc = _fdot(tri_incl.astype(F32), g, _BNN)
    gc_row = _fdot(g[:, :, :CHUNK], (row <= col).astype(F32), _BTN)
    decay = jnp.exp(jnp.where(tri_incl, gc[:, :, :CHUNK] - gc_row, -1e30))
    kb = k * beta
    vb = v * beta
    with_k = _pdot(jnp.concatenate([kb, q], axis=1), k, _BNT)
    neg_l = jnp.where(tri_strict, -(with_k[:, :CHUNK] * decay), 0.0)
    qk = jnp.where(tri_incl, with_k[:, CHUNK:] * decay, 0.0)
    inv = _unit_lower_inverse(neg_l)
    e = jnp.exp(gc)
    solved = _bdot(inv, jnp.concatenate([kb * e, vb], axis=2), _BNN)
    g_last = gc[:, CHUNK - 1:CHUNK, :]
    k_dec = k * jnp.exp(g_last - gc)
    from_k = _bdot(k_dec, solved, _BTN)
    from_qk = _pdot(qk, solved, _BNN)
    step, add = -from_k[:, :, :HEAD_DIM], from_k[:, :, HEAD_DIM:]
    read, out = q * e - from_qk[:, :, :HEAD_DIM], from_qk[:, :, HEAD_DIM:]
    return step, add, jnp.exp(g_last), read, out


def _gdn_scan_step(state, step, add, decay_last):
    return state * decay_last + _bdot(step, state) + add


def _gdn_outputs(states, read, out, z, onw):
    return _rms(_bdot(read, states, _BNN) + out, onw) * _silu(z)


def _scan_scratch(n, dtype):
    return [pltpu.VMEM((n, HEAD_DIM, HEAD_DIM), dtype), pltpu.VMEM((n, HEAD_DIM, HEAD_DIM), F32), pltpu.VMEM((n, 1, HEAD_DIM), F32)]


def _head_lane(h, offset=0):
    return lax.broadcasted_iota(jnp.int32, (1, LANES), 1) == h + offset


def _pick(mask, x):
    return jnp.sum(jnp.where(mask, x, 0.0), axis=1, keepdims=True)


def _gdn_specs(heads, tb, rev, nb, PAIR):
    assert heads % PAIR == 0
    blk = (lambda i: nb - 1 - i) if rev else (lambda i: i)
    hb = tb // HALO
    width, pairs = PAIR * HEAD_DIM, heads // PAIR

    def col(group):
        return pl.BlockSpec((tb, width), lambda i, h: (blk(i), group * pairs + h))

    def halo(group):
        return pl.BlockSpec((HALO, width), lambda i, h: (jnp.maximum(blk(i) * hb - 1, 0), group * pairs + h))

    def convw(group):
        return pl.BlockSpec((CONV_K, width), lambda i, h: (0, group * pairs + h))

    vec = pl.BlockSpec((1, LANES), lambda i, h: (0, 0))
    ab = pl.BlockSpec((tb, LANES), lambda i, h: (blk(i), 0))
    states = pl.BlockSpec((PAIR, tb // CHUNK, HEAD_DIM, HEAD_DIM), lambda i, h: (h, blk(i), 0, 0))
    return blk, col, halo, convw, vec, ab, states


def _head_cols(p):
    return slice(p * HEAD_DIM, (p + 1) * HEAD_DIM)


def _gdn_fwd(proj, ab, conv_w, alog_row, dtb_row, onw_row, *, heads, name, tb=1024, pair=4, gather=()):
    t = proj.shape[0]
    tb = min(tb, t)
    nb, cpb = t // tb, tb // CHUNK
    PAIR = min(pair, heads)
    _, col, halo, convw, vec, abspec, states = _gdn_specs(heads, tb, False, nb, PAIR)

    def body(q_ref, k_ref, v_ref, qh_ref, kh_ref, vh_ref, z_ref, ab_ref, wq_ref, wk_ref, wv_ref, alog_ref, dtb_ref, onw_ref,
             og_ref, st_ref, state_scr, x_scr, *op_scr):
        i, pair = pl.program_id(0), pl.program_id(1)
        abv = ab_ref[...]
        heads_here, later = [pair * PAIR + p for p in range(PAIR)], []
        for p, h in enumerate(heads_here):
            cols = _head_cols(p)
            for n, (ref, href) in enumerate(((q_ref, qh_ref), (k_ref, kh_ref), (v_ref, vh_ref))):
                x_scr[p, n, 0:HALO, :] = jnp.where(i > 0, href[:, cols], 0.0)
                x_scr[p, n, HALO:HALO + tb, :] = ref[:, cols]
            sel_a, sel_b = _head_lane(h), _head_lane(h, heads)
            alog, dtb = _pick(sel_a, alog_ref[...]), _pick(sel_a, dtb_ref[...])
            acts = [_silu(_conv(x_scr.at[p, n], w_ref[:, cols], tb)) for n, w_ref in enumerate((wq_ref, wk_ref, wv_ref))]
            *scan, read, out = _gdn_intra(*acts, _pick(sel_a, abv), _pick(sel_b, abv), alog, dtb)
            for scr, val in zip(op_scr[3 * p:3 * p + 3], scan):
                scr[...] = val.astype(scr.dtype)
            later.append((read, out))

        def chunk(c, states):
            for p in range(PAIR):
                st_ref[p, c] = states[p]
            return tuple(_gdn_scan_step(states[p], *[scr[c] for scr in op_scr[3 * p:3 * p + 3]]) for p in range(PAIR))

        @pl.when(i == 0)
        def _():
            for h in heads_here:
                state_scr[h] = jnp.zeros((HEAD_DIM, HEAD_DIM), F32)

        last = lax.fori_loop(0, cpb, chunk, tuple(state_scr[h] for h in heads_here))
        for p, h in enumerate(heads_here):
            cols = _head_cols(p)
            state_scr[h] = last[p]
            og = _gdn_outputs(st_ref[p], *later[p], z_ref[:, cols].reshape(cpb, CHUNK, HEAD_DIM), onw_ref[...])
            og_ref[:, cols] = og.reshape(tb, HEAD_DIM).astype(BF16)

    n_x = len(gather)
    grid = (nb, heads // PAIR)
    og, st, *gathered = pl.pallas_call(
        _with_exchange(body, 14, 2, True, n_x, grid),
        name=name,
        grid=grid,
        in_specs=[col(0), col(1), col(2), halo(0), halo(1), halo(2), col(3), abspec, convw(0), convw(1), convw(2), vec, vec, vec]
        + [_ANY] * n_x,
        out_specs=[pl.BlockSpec((tb, PAIR * HEAD_DIM), lambda i, h: (i, h)), states] + [_ANY] * n_x,
        out_shape=[jax.ShapeDtypeStruct((t, heads * HEAD_DIM), BF16),
                   jax.ShapeDtypeStruct((heads, t // CHUNK, HEAD_DIM, HEAD_DIM), F32)] + _chip_shapes(True, gather),
        scratch_shapes=[pltpu.VMEM((heads, HEAD_DIM, HEAD_DIM), F32), pltpu.VMEM((PAIR, 3, HALO + tb, HEAD_DIM), F32)]
        + _scan_scratch(cpb, BF16) * PAIR + (_chip_scratch(n_x) if n_x else []),
        compiler_params=_params("arbitrary", "arbitrary"),
    )(proj, proj, proj, proj, proj, proj, proj, ab, conv_w, conv_w, conv_w, alog_row, dtb_row, onw_row, *gather)
    return og, st, gathered


def _gdn_bwd(proj, ab, conv_w, alog_row, dtb_row, onw_row, states, dog, *, heads, name, tb=1024, pair=2, exchange=()):
    t = proj.shape[0]
    tb = min(tb, t)
    nb, cpb = t // tb, tb // CHUNK
    PAIR = min(pair, heads)
    _, col, halo, convw, vec, abspec, states_spec = _gdn_specs(heads, tb, True, nb, PAIR)
    n_conv = conv_w.shape[1]

    def body(q_ref, k_ref, v_ref, qh_ref, kh_ref, vh_ref, z_ref, ab_ref, wq_ref, wk_ref, wv_ref, alog_ref, dtb_ref, onw_ref,
             st_ref, dog_ref, dproj_ref, dab_ref, dconv_ref, dalog_ref, ddtb_ref, donw_ref,
             dstate_scr, x_scr, carry_scr, *scr):
        op_scr, dop_scr, dstates_scr, dc_scr = scr[:3 * PAIR], scr[3 * PAIR:6 * PAIR], scr[6 * PAIR:7 * PAIR], scr[7 * PAIR]
        i, pair = pl.program_id(0), pl.program_id(1)
        first_block = i == nb - 1
        heads_here, later = [pair * PAIR + p for p in range(PAIR)], []

        @pl.when(jnp.logical_and(i == 0, pair == 0))
        def _():
            dconv_ref[...] = jnp.zeros_like(dconv_ref)
            dalog_ref[...] = jnp.zeros_like(dalog_ref)
            ddtb_ref[...] = jnp.zeros_like(ddtb_ref)
            donw_ref[...] = jnp.zeros_like(donw_ref)

        @pl.when(pair == 0)
        def _():
            dab_ref[...] = jnp.zeros_like(dab_ref)

        @pl.when(i == 0)
        def _():
            for h in heads_here:
                dstate_scr[h] = jnp.zeros((HEAD_DIM, HEAD_DIM), F32)
                carry_scr[h] = jnp.zeros((3, HALO, HEAD_DIM), F32)

        abv = ab_ref[...]
        w_refs = (wq_ref, wk_ref, wv_ref)
        for p, h in enumerate(heads_here):
            cols = _head_cols(p)
            for n, (ref, href) in enumerate(((q_ref, qh_ref), (k_ref, kh_ref), (v_ref, vh_ref))):
                x_scr[p, n, 0:HALO, :] = jnp.where(first_block, 0.0, href[:, cols])
                x_scr[p, n, HALO:HALO + tb, :] = ref[:, cols]
            sel_a, sel_b = _head_lane(h), _head_lane(h, heads)
            alog, dtb = _pick(sel_a, alog_ref[...]), _pick(sel_a, dtb_ref[...])
            acts = [_silu(_conv(x_scr.at[p, n], w_ref[:, cols], tb)) for n, w_ref in enumerate(w_refs)]
            (*scan, read, out), vjp_intra = jax.vjp(_gdn_intra, *acts, _pick(sel_a, abv), _pick(sel_b, abv), alog, dtb)
            for s, val in zip(op_scr[3 * p:3 * p + 3], scan):
                s[...] = val.astype(s.dtype)
            blocked = lambda ref: ref[:, cols].reshape(cpb, CHUNK, HEAD_DIM)
            _, vjp_outputs = jax.vjp(_gdn_outputs, st_ref[p], read, out, blocked(z_ref), onw_ref[...])
            dstates_scr[p][...], dread, dout, dz, donw = vjp_outputs(blocked(dog_ref))
            dproj_ref[3, :, cols] = dz.reshape(tb, HEAD_DIM).astype(BF16)
            donw_ref[...] += donw
            later.append((vjp_intra, dread, dout, sel_a, sel_b))

        def chunk(i_rev, dstates):
            c = cpb - 1 - i_rev
            new = []
            for p in range(PAIR):
                _, vjp = jax.vjp(_gdn_scan_step, st_ref[p, c], *[s[c].astype(F32) for s in op_scr[3 * p:3 * p + 3]])
                dstate, *grads = vjp(dstates[p])
                for s, val in zip(dop_scr[3 * p:3 * p + 3], grads):
                    s[c] = val
                new.append(dstate + dstates_scr[p][c])
            return tuple(new)

        last = lax.fori_loop(0, cpb, chunk, tuple(dstate_scr[h] for h in heads_here))
        for p, h in enumerate(heads_here):
            cols = _head_cols(p)
            vjp_intra, dread, dout, sel_a, sel_b = later[p]
            dstate_scr[h] = last[p]
            *dacts, da, db, dalog, ddtb = vjp_intra((*[s[...] for s in dop_scr[3 * p:3 * p + 3]], dread, dout))
            dab_ref[...] += jnp.where(sel_a, da, 0.0) + jnp.where(sel_b, db, 0.0)
            for n, (dact, w_ref) in enumerate(zip(dacts, w_refs)):
                dx, dw = _conv_silu_bwd(x_scr.at[p, n], w_ref[:, cols], dact, dc_scr, tb)
                x_scr[p, n] = dx
                x_scr[p, n, tb:tb + HALO, :] += carry_scr[h, n]
                carry_scr[h, n] = x_scr[p, n, 0:HALO, :]
                dproj_ref[n, :, cols] = x_scr[p, n, HALO:HALO + tb, :].astype(BF16)
                lanes = pl.ds(pl.multiple_of((n * heads + h) * HEAD_DIM, HEAD_DIM), HEAD_DIM)
                for j in range(CONV_K):
                    dconv_ref[j:j + 1, lanes] += dw[j]
            dalog_ref[...] += jnp.where(sel_a, dalog, 0.0)
            ddtb_ref[...] += jnp.where(sel_a, ddtb, 0.0)

    dog_spec = pl.BlockSpec((tb, PAIR * HEAD_DIM), lambda i, h: (nb - 1 - i, h))
    dproj_spec = pl.BlockSpec((4, tb, PAIR * HEAD_DIM), lambda i, h: (0, nb - 1 - i, h))
    row_shape = jax.ShapeDtypeStruct((1, LANES), F32)
    n_x = len(exchange)
    grid = (nb, heads // PAIR)
    outs = pl.pallas_call(
        _with_exchange(body, 16, 6, False, n_x, grid),
        name=name,
        grid=grid,
        in_specs=[col(0), col(1), col(2), halo(0), halo(1), halo(2), col(3), abspec, convw(0), convw(1), convw(2), vec, vec, vec,
                  states_spec, dog_spec] + [_ANY] * n_x,
        out_specs=[dproj_spec, abspec, pl.BlockSpec((CONV_K, n_conv), lambda i, h: (0, 0)), vec, vec, vec] + [_ANY] * n_x,
        out_shape=[jax.ShapeDtypeStruct((4, t, heads * HEAD_DIM), BF16), jax.ShapeDtypeStruct((t, LANES), F32),
                   jax.ShapeDtypeStruct((CONV_K, n_conv), F32), row_shape, row_shape, row_shape] + _chip_shapes(False, exchange),
        scratch_shapes=[pltpu.VMEM((heads, HEAD_DIM, HEAD_DIM), F32), pltpu.VMEM((PAIR, 3, HALO + tb, HEAD_DIM), F32),
                        pltpu.VMEM((heads, 3, HALO, HEAD_DIM), F32)] + _scan_scratch(cpb, BF16) * PAIR
        + _scan_scratch(cpb, F32) * PAIR + [pltpu.VMEM((cpb, HEAD_DIM, HEAD_DIM), F32)] * PAIR
        + [pltpu.VMEM((HALO + tb + HALO, HEAD_DIM), F32)]
        + (_chip_scratch(n_x) if n_x else []),
        compiler_params=_params("arbitrary", "arbitrary", vmem=VMEM_LIMIT_WIDE_BYTES),
    )(proj, proj, proj, proj, proj, proj, proj, ab, conv_w, conv_w, conv_w, alog_row, dtb_row, onw_row, states, dog, *exchange)
    return (*outs[:6], outs[6:])


BAND = (LEFT_CHUNKS + 1) * CHUNK
PAD = LEFT_CHUNKS * CHUNK
GROUP = 2
ROWS = GROUP * CHUNK
WIN = (LEFT_CHUNKS + GROUP) * CHUNK
DIAGS = WIN + ROWS - 1
NEAR = PAD + ROWS - 1 - REL_CLIP
assert 0 < NEAR < DIAGS and WIN - PAD - 1 <= REL_CLIP and WIN % LANES == 0
ATTN_BLOCK = 1024
N_EDGE = PAD // ROWS


def _band_bias(rel_bias):
    heads = rel_bias.shape[0]
    far = jnp.broadcast_to(rel_bias[:, 2 * REL_CLIP:], (heads, NEAR + 1))
    near = rel_bias[:, 2 * REL_CLIP + NEAR + 1 - DIAGS:2 * REL_CLIP][:, ::-1]
    diag = jnp.concatenate([far, near], axis=1)
    return jnp.stack([diag[:, ROWS - 1 - r:ROWS - 1 - r + WIN] for r in range(ROWS)], axis=1)


def _band_bias_grad(dbias):
    heads = dbias.shape[0]
    diag = sum(jnp.pad(dbias[:, r, :], ((0, 0), (ROWS - 1 - r, r))) for r in range(ROWS))
    far = jnp.sum(diag[:, :NEAR + 1], axis=1, keepdims=True)
    near = diag[:, NEAR + 1:][:, ::-1]
    unused = jnp.zeros((heads, 2 * REL_CLIP - near.shape[1]), F32)
    return jnp.concatenate([unused, near, far], axis=1)


def _masked_bias(bias, n):
    r = np.arange(ROWS)[:, None]
    key = np.arange(WIN)[None, :]
    band_start = (r // CHUNK) * CHUNK
    in_band = np.logical_and(key >= band_start, key < band_start + BAND)
    in_sequence = key[None] >= PAD - np.arange(n)[:, None, None] * ROWS
    first = jnp.where(np.logical_and(in_band[None], in_sequence)[None], bias[:, None], -1e30)
    return first, jnp.where(in_band[None, None], bias[:, None], -1e30)


def _attn_groups(q_pre, z, kn, v, bias, qnw):
    q = _rms(q_pre, qnw)
    s = _bdot(q, kn, _BNT) * (HEAD_DIM ** -0.5) + bias
    p = jnp.exp(s - jnp.max(s, axis=-1, keepdims=True))
    p = p / jnp.sum(p, axis=-1, keepdims=True)
    return _bdot(p, v, _BNN) * _silu(z)


def _attn_groups_bwd(q_pre, z, kn, v, bias, qnw, dog):
    scale = HEAD_DIM ** -0.5
    inv_rms = lax.rsqrt(jnp.mean(q_pre * q_pre, axis=-1, keepdims=True) + EPS)
    q_hat = q_pre * inv_rms
    q_b = (q_hat * qnw).astype(BF16)
    s = _dot(q_b, kn, _BNT) * scale + bias
    e = jnp.exp(s - jnp.max(s, axis=-1, keepdims=True))
    p = e * (1.0 / jnp.sum(e, axis=-1, keepdims=True))
    p_b = p.astype(BF16)
    o = _dot(p_b, v, _BNN)
    sig = jax.nn.sigmoid(z)
    do = dog * (z * sig)
    dz = dog * o * (sig * (1.0 + z * (1.0 - sig)))
    do_b = do.astype(BF16)
    dv = _dot(p_b, do_b, _BTN)
    dp = _dot(do_b, v, _BNT)
    ds = p * (dp - jnp.sum(do * o, axis=-1, keepdims=True))
    ds_b = (ds * scale).astype(BF16)
    dq = _dot(ds_b, kn, _BNN)
    dkn = _dot(ds_b, q_b, _BTN)
    dqnw = jnp.sum(jnp.sum(dq * q_hat, axis=0), axis=0, keepdims=True)
    dq_hat = dq * qnw
    dq_pre = inv_rms * (dq_hat - q_hat * jnp.mean(dq_hat * q_hat, axis=-1, keepdims=True))
    return dq_pre, dz, dkn, dv, jnp.sum(ds, axis=0), dqnw


def _attn_specs(heads, tb, t):
    def col(group):
        return pl.BlockSpec((tb, HEAD_DIM), lambda h, i: (i, group * heads + h))

    def full(group):
        return pl.BlockSpec((t, HEAD_DIM), lambda h, i: (0, group * heads + h))

    bias = [pl.BlockSpec((1, min(tb // ROWS, N_EDGE), ROWS, WIN), lambda h, i: (h, 0, 0, 0)),
            pl.BlockSpec((1, 1, ROWS, WIN), lambda h, i: (h, 0, 0, 0))]
    vec = pl.BlockSpec((1, HEAD_DIM), lambda h, i: (0, 0))
    return col, full, bias, vec


def _attn_windows(scr, block_start, n):
    return jnp.stack([scr[pl.ds(pl.multiple_of(block_start + g * ROWS, ROWS), WIN), :] for g in range(n)])


def _attn_fill(k_ref, v_ref, knw_ref, kn_scr, v_scr, t):
    kn_scr[0:PAD, :] = jnp.zeros((PAD, HEAD_DIM), BF16)
    v_scr[0:PAD, :] = jnp.zeros((PAD, HEAD_DIM), BF16)
    step = min(512, t)

    def fill(j, _):
        rows = pl.ds(pl.multiple_of(j * step, step), step)
        prows = pl.ds(pl.multiple_of(PAD + j * step, CHUNK), step)
        kn_scr[prows, :] = _rms(k_ref[rows, :], knw_ref[...]).astype(BF16)
        v_scr[prows, :] = v_ref[rows, :].astype(BF16)
        return 0

    lax.fori_loop(0, t // step, fill, 0)


def _attn_fwd(proj, bias, qnw_row, knw_row, *, heads, name, tb=2 * ATTN_BLOCK):
    t = proj.shape[0]
    tb = min(tb, t)
    nb, ng = t // tb, tb // ROWS
    col, full, bias_spec, vec = _attn_specs(heads, tb, t)

    def body(q_ref, k_ref, v_ref, z_ref, first_ref, rest_ref, qnw_ref, knw_ref, og_ref, kn_scr, v_scr):
        i = pl.program_id(1)

        @pl.when(i == 0)
        def _():
            _attn_fill(k_ref, v_ref, knw_ref, kn_scr, v_scr, t)

        def run(block_bias):
            start = i * tb
            og = _attn_groups(q_ref[...].reshape(ng, ROWS, HEAD_DIM), z_ref[...].reshape(ng, ROWS, HEAD_DIM),
                              _attn_windows(kn_scr, start, ng), _attn_windows(v_scr, start, ng), block_bias, qnw_ref[...])
            og_ref[...] = og.reshape(tb, HEAD_DIM).astype(BF16)

        def first_bias():
            edge = first_ref[0]
            more = ng - edge.shape[0]
            return edge if more == 0 else jnp.concatenate([edge, jnp.broadcast_to(rest_ref[0], (more, ROWS, WIN))])

        pl.when(i == 0)(lambda: run(first_bias()))
        pl.when(i > 0)(lambda: run(rest_ref[0]))

    return pl.pallas_call(
        body,
        name=name,
        grid=(heads, nb),
        in_specs=[col(0), full(1), full(2), col(3), *bias_spec, vec, vec],
        out_specs=pl.BlockSpec((tb, HEAD_DIM), lambda h, i: (i, h)),
        out_shape=jax.ShapeDtypeStruct((t, heads * HEAD_DIM), BF16),
        scratch_shapes=[pltpu.VMEM((PAD + t, HEAD_DIM), BF16), pltpu.VMEM((PAD + t, HEAD_DIM), BF16)],
        compiler_params=_params("arbitrary", "arbitrary"),
    )(proj, proj, proj, proj, *bias, qnw_row, knw_row)


def _attn_bwd(proj, bias, qnw_row, knw_row, dog, *, heads, name, tb=ATTN_BLOCK, sub=4):
    t = proj.shape[0]
    tb = min(tb, t)
    nb, ng = t // tb, tb // ROWS
    sub = min(sub, ng)
    n_edge = min(ng, N_EDGE)
    assert n_edge % sub == 0
    col, full, bias_spec, vec = _attn_specs(heads, tb, t)

    def body(q_ref, k_ref, v_ref, z_ref, first_ref, rest_ref, qnw_ref, knw_ref, dog_ref,
             dqz_ref, dkv_ref, dbias_ref, dqnw_ref, dknw_ref, kn_scr, v_scr, dkn_scr, dv_scr):
        i = pl.program_id(1)

        @pl.when(i == 0)
        def _():
            _attn_fill(k_ref, v_ref, knw_ref, kn_scr, v_scr, t)
            dkn_scr[...] = jnp.zeros_like(dkn_scr)
            dv_scr[...] = jnp.zeros_like(dv_scr)
            dbias_ref[...] = jnp.zeros_like(dbias_ref)
            dqnw_ref[...] = jnp.zeros_like(dqnw_ref)

        def run(block_bias):
            for g0 in range(0, ng, sub):
                rows = pl.ds(g0 * ROWS, sub * ROWS)
                at = i * tb + g0 * ROWS
                blocked = lambda ref: ref[rows, :].reshape(sub, ROWS, HEAD_DIM)
                dq, dz, dkn, dv, dbias, dqnw = _attn_groups_bwd(
                    blocked(q_ref), blocked(z_ref), _attn_windows(kn_scr, at, sub), _attn_windows(v_scr, at, sub),
                    block_bias(g0), qnw_ref[...], blocked(dog_ref))
                dqz_ref[0, rows, :] = dq.reshape(sub * ROWS, HEAD_DIM).astype(BF16)
                dqz_ref[1, rows, :] = dz.reshape(sub * ROWS, HEAD_DIM).astype(BF16)
                for g in range(sub):
                    window = pl.ds(pl.multiple_of(at + g * ROWS, ROWS), WIN)
                    dkn_scr[window, :] += dkn[g]
                    dv_scr[window, :] += dv[g]
                dbias_ref[0] += dbias
                dqnw_ref[0] += dqnw

        pl.when(i == 0)(lambda: run(lambda g0: first_ref[0, g0:g0 + sub] if g0 + sub <= n_edge else rest_ref[0]))
        pl.when(i > 0)(lambda: run(lambda g0: rest_ref[0]))

        @pl.when(i == nb - 1)
        def _():
            step = min(512, t)

            def finish(j, dknw):
                rows = pl.ds(pl.multiple_of(j * step, step), step)
                prows = pl.ds(pl.multiple_of(PAD + j * step, CHUNK), step)
                _, vjp = jax.vjp(_rms, k_ref[rows, :], knw_ref[...])
                dk, dw = vjp(dkn_scr[prows, :])
                dkv_ref[0, rows, :] = dk.astype(BF16)
                dkv_ref[1, rows, :] = dv_scr[prows, :].astype(BF16)
                return dknw + dw

            dknw_ref[0] = lax.fori_loop(0, t // step, finish, jnp.zeros((1, HEAD_DIM), F32))

    pair_col = pl.BlockSpec((2, tb, HEAD_DIM), lambda h, i: (0, i, h))
    pair_full = pl.BlockSpec((2, t, HEAD_DIM), lambda h, i: (0, 0, h))
    head_vec = pl.BlockSpec((1, 1, HEAD_DIM), lambda h, i: (h, 0, 0))
    pair_shape = jax.ShapeDtypeStruct((2, t, heads * HEAD_DIM), BF16)
    vec_shape = jax.ShapeDtypeStruct((heads, 1, HEAD_DIM), F32)
    return pl.pallas_call(
        body,
        name=name,
        grid=(heads, nb),
        in_specs=[col(0), full(1), full(2), col(3), *bias_spec, vec, vec, pl.BlockSpec((tb, HEAD_DIM), lambda h, i: (i, h))],
        out_specs=[pair_col, pair_full, pl.BlockSpec((1, ROWS, WIN), lambda h, i: (h, 0, 0)), head_vec, head_vec],
        out_shape=[pair_shape, pair_shape, jax.ShapeDtypeStruct((heads, ROWS, WIN), F32), vec_shape, vec_shape],
        scratch_shapes=[pltpu.VMEM((PAD + t, HEAD_DIM), BF16), pltpu.VMEM((PAD + t, HEAD_DIM), BF16),
                        pltpu.VMEM((PAD + t, HEAD_DIM), F32), pltpu.VMEM((PAD + t, HEAD_DIM), F32)],
        compiler_params=_params("arbitrary", "arbitrary"),
    )(proj, proj, proj, proj, *bias, qnw_row, knw_row, dog)


def _lane_row(v):
    v = v.reshape(1, -1)
    return jnp.pad(v, ((0, 0), (0, LANES - v.shape[1])))


def _local_step(x, target, norm_w, wa_in, conv_w, a_log, dt_bias, onw, wa_out, wb_in, qnw, knw, rel_bias, wb_out, *,
                sharded=False):
    ha, hb = a_log.shape[-1], rel_bias.shape[-2]
    na = 4 * ha * HEAD_DIM
    wa_ab = jnp.pad(wa_in[:, na:], ((0, 0), (0, LANES - 2 * ha)))
    alog_row, dtb_row, onw_row = _lane_row(a_log), _lane_row(dt_bias), _lane_row(onw)
    qnw_row, knw_row = _lane_row(qnw), _lane_row(knw)
    bias = _masked_bias(_band_bias(rel_bias.reshape(hb, -1)), min(min(ATTN_BLOCK, x.shape[0]) // ROWS, N_EDGE))

    hn0, ab_a = _rmsnorm_fwd(x, norm_w[0:1], wa_ab, name="norm0")
    proj_a = _matmul(hn0, wa_in, n_cols=na, name="a_in")
    og_a, states, got = _gdn_fwd(proj_a, ab_a, conv_w, alog_row, dtb_row, onw_row, heads=ha, name="gdn_fwd",
                                 gather=[wb_in, wa_out, wb_out] if sharded else [])
    if sharded:
        wb_in, wa_out, wb_out = _join_cols(got[0]), got[1].reshape(-1, got[1].shape[-1]), got[2].reshape(-1, got[2].shape[-1])
    h1, hn1 = _matmul_norm(og_a, wa_out, x, norm_w[1:2], name="a_out_norm1")
    proj_b = _matmul(hn1, wb_in, name="b_in")
    og_b = _attn_fwd(proj_b, bias, qnw_row, knw_row, heads=hb, name="attn_fwd")
    loss, dh2, dh2_b = _matmul_loss(og_b, wb_out, h1, target, name="b_out_loss")

    grad_dtype = BF16 if sharded else F32
    dog_b = _matmul(dh2_b, wb_out, trans_b=True, name="d_b_out_x")
    dwb_out = _matmul(og_b, dh2_b, trans_a=True, out_dtype=grad_dtype, name="d_b_out_w")
    dqz, dkv, dbias, dqnw, dknw = _attn_bwd(proj_b, bias, qnw_row, knw_row, dog_b, heads=hb, name="attn_bwd")
    dproj_b, qkvz = [dqz, dkv], (0, 3, 1, 2)
    dhn1 = _matmul(dproj_b, wb_in, trans_b=True, order=qkvz, name="d_b_in_x")
    dwb_in = _matmul(hn1, dproj_b, trans_a=True, order=qkvz, out_dtype=grad_dtype, col_slabs=N_CHIPS if sharded else 0,
                     name="d_b_in_w")
    dh1, dh1_b, dnw1 = _rmsnorm_bwd(h1, norm_w[1:2], dhn1, dh2, name="d_norm1")

    dog_a = _matmul(dh1_b, wa_out, trans_b=True, name="d_a_out_x")
    dwa_out = _matmul(og_a, dh1_b, trans_a=True, out_dtype=grad_dtype, name="d_a_out_w")
    early = [dwb_in, _split_rows(dwa_out), _split_rows(dwb_out)] if sharded else []
    dproj_a, dab, dconv, dalog, ddtb, donw, landed = _gdn_bwd(
        proj_a, ab_a, conv_w, alog_row, dtb_row, onw_row, states, dog_a, heads=ha, name="gdn_bwd", exchange=early)
    dab_b = dab.astype(BF16)
    if sharded:
        mine = [_sum_slots(s, name=f"chip_sum_{n}") for n, s in zip(("b_w_in", "a_w_out", "b_w_out"), landed)]
        dwa_main, theirs = _matmul(hn0, dproj_a, trans_a=True, out_dtype=grad_dtype, name="d_a_in_w", exchange=mine,
                                   with_pair=True)
        dwb_in, dwa_out, dwb_out = zip(mine, theirs)
    else:
        dwa_main = _matmul(hn0, dproj_a, trans_a=True, out_dtype=grad_dtype, name="d_a_in_w")
    dwa_in = jnp.concatenate(
        [dwa_main, _matmul(hn0, dab_b, trans_a=True, out_dtype=grad_dtype, name="d_a_in_ab_w")[:, :2 * ha]], axis=1)
    if sharded:
        dhn0, (dwa_in, dconv) = _matmul(dproj_a, wa_in, trans_b=True, name="d_a_in_x",
                                        exchange=[_split_cols(dwa_in), _split_cols(dconv)])
    else:
        dhn0 = _matmul(dproj_a, wa_in, trans_b=True, name="d_a_in_x")
    dx, _, dnw0 = _rmsnorm_bwd(x, norm_w[0:1], dhn0, dh1, narrow=(dab_b, wa_ab), name="d_norm0")

    drel = _band_bias_grad(dbias)
    grads = dict(
        norm_w=jnp.concatenate([dnw0, dnw1], axis=0), a_w_in=dwa_in, a_conv_w=dconv, a_a_log=dalog[:, :ha],
        a_dt_bias=ddtb[:, :ha], a_out_norm_w=donw, a_w_out=dwa_out, b_w_in=dwb_in, b_q_norm_w=jnp.sum(dqnw, axis=0),
        b_k_norm_w=jnp.sum(dknw, axis=0), b_rel_bias=drel[None], b_w_out=dwb_out)
    return loss, dx, grads


_ANY = pl.BlockSpec(memory_space=pl.ANY)
_CHIP_FLIPS = ((1, 0), (0, 1), (1, 1))


def _place():
    x, y, c = lax.axis_index("x"), lax.axis_index("y"), lax.axis_index("c")
    return x, y, c


def _flip(v, bit):
    return 1 - v if bit else v


def _remote(src, dst, send_sem, recv_sem, peer):
    return pltpu.make_async_remote_copy(src_ref=src, dst_ref=dst, send_sem=send_sem, recv_sem=recv_sem, device_id=peer,
                                        device_id_type=MESH)


def _comm_call(body, arrays, out_shapes, n_remote, n_local, name):
    scratch = [pltpu.SemaphoreType.DMA((n_remote,)), pltpu.SemaphoreType.DMA((n_remote,))]
    if n_local:
        scratch.append(pltpu.SemaphoreType.DMA((n_local,)))
    return pl.pallas_call(
        body, name=name, in_specs=[_ANY] * len(arrays), out_specs=[_ANY] * len(out_shapes), out_shape=out_shapes,
        scratch_shapes=scratch)(*arrays)


def _chip_scratch(n):
    return [pltpu.SemaphoreType.DMA((3 * n,)), pltpu.SemaphoreType.DMA((3 * n,)), pltpu.SemaphoreType.DMA((n,))]


def _chip_shapes(gather, arrays):
    return [jax.ShapeDtypeStruct(((N_CHIPS,) + s.shape) if gather else s.shape, s.dtype) for s in arrays]


def _chip_traffic(gather, ins, outs, sems):
    send_sems, recv_sems, local_sems = sems
    x, y, c = _place()
    mine = 2 * x + y
    local, remote, landing = [], [], []
    for a in range(len(ins)):
        local.append(pltpu.make_async_copy(ins[a] if gather else ins[a].at[mine], outs[a].at[mine], local_sems.at[a]))
        for k, (fx, fy) in enumerate(_CHIP_FLIPS):
            peer = (_flip(x, fx), _flip(y, fy), c)
            theirs = 2 * peer[0] + peer[1]
            src = ins[a] if gather else ins[a].at[theirs]
            pair = send_sems.at[3 * a + k], recv_sems.at[3 * a + k]
            remote.append(_remote(src, outs[a].at[mine], *pair, peer))
            landing.append(_remote(src, outs[a].at[theirs], *pair, peer))
    return local + remote, (local, landing, remote)


def _start(traffic):
    for cp in traffic[0]:
        cp.start()


def _finish(traffic):
    local, landing, remote = traffic[1]
    for cp in local:
        cp.wait()
    for cp in landing:
        cp.wait_recv()
    for cp in remote:
        cp.wait_send()


def _pair_scratch(n):
    return [pltpu.SemaphoreType.DMA((n,)), pltpu.SemaphoreType.DMA((n,))]


def _pair_traffic(ins, outs, sems):
    send_sems, recv_sems = sems
    x, y, c = _place()
    copies = [_remote(ins[a], outs[a], send_sems.at[a], recv_sems.at[a], (x, y, 1 - c)) for a in range(len(ins))]
    return copies, ([], copies, copies)


def _with_exchange(compute, n_in, n_out, gather, n_x, grid):
    if not n_x:
        return compute

    def body(*refs):
        ins, x_in = refs[:n_in], refs[n_in:n_in + n_x]
        outs, x_out = refs[n_in + n_x:n_in + n_x + n_out], refs[n_in + n_x + n_out:n_in + 2 * n_x + n_out]
        n_sems = 2 if gather == "pair" else 3
        scratch, sems = refs[n_in + 2 * n_x + n_out:-n_sems], refs[-n_sems:]
        traffic = _pair_traffic(x_in, x_out, sems) if gather == "pair" else _chip_traffic(gather, x_in, x_out, sems)
        first = functools.reduce(jnp.logical_and, [pl.program_id(d) == 0 for d in range(len(grid))])
        last = functools.reduce(jnp.logical_and, [pl.program_id(d) == grid[d] - 1 for d in range(len(grid))])

        @pl.when(first)
        def _():
            _start(traffic)

        compute(*ins, *outs, *scratch)

        @pl.when(last)
        def _():
            _finish(traffic)

    return body


def _gather_shared(shard, small, *, name):
    rows = shard.shape[0]
    assert rows % 2 == 0
    half = rows // 2

    def body(shard_ref, small_ref, out_ref, small_out_ref, send_sems, recv_sems, local_sems):
        x, y, c = _place()
        mine = 2 * x + y
        sibling = (x, y, 1 - c)
        my_rows = pl.ds(pl.multiple_of(c * half, 8), half)
        local = [pltpu.make_async_copy(shard_ref, out_ref.at[mine], local_sems.at[0]),
                 pltpu.make_async_copy(small_ref, small_out_ref.at[mine], local_sems.at[1])]
        sent, landed, passed_on, handed = [], [], [], []
        for k, (fx, fy) in enumerate(_CHIP_FLIPS):
            peer = (_flip(x, fx), _flip(y, fy), c)
            theirs = 2 * peer[0] + peer[1]
            ici, d2d, tiny = [(send_sems.at[3 * n + k], recv_sems.at[3 * n + k]) for n in range(3)]
            sent.append(_remote(shard_ref.at[my_rows], out_ref.at[mine, my_rows], *ici, peer))
            landed.append(_remote(shard_ref.at[my_rows], out_ref.at[theirs, my_rows], *ici, peer))
            sent.append(_remote(small_ref, small_out_ref.at[mine], *tiny, peer))
            landed.append(_remote(small_ref, small_out_ref.at[theirs], *tiny, peer))
            passed_on.append(_remote(out_ref.at[theirs, my_rows], out_ref.at[theirs, my_rows], *d2d, sibling))
            other_rows = pl.ds(pl.multiple_of((1 - c) * half, 8), half)
            handed.append(_remote(out_ref.at[theirs, other_rows], out_ref.at[theirs, other_rows], *d2d, sibling))
        for cp in local + sent:
            cp.start()
        for k in range(3):
            landed[2 * k].wait_recv()
            passed_on[k].start()
        for k in range(3):
            landed[2 * k + 1].wait_recv()
            handed[k].wait_recv()
        for cp in local:
            cp.wait()
        for cp in sent + passed_on:
            cp.wait_send()

    return pl.pallas_call(
        body, name=name, in_specs=[_ANY] * 2, out_specs=[_ANY] * 2, out_shape=_chip_shapes(True, [shard, small]),
        scratch_shapes=[pltpu.SemaphoreType.DMA((9,)), pltpu.SemaphoreType.DMA((9,)), pltpu.SemaphoreType.DMA((2,))],
    )(shard, small)


def _swap_pair(arrays, *, name):
    n = len(arrays)

    def body(*refs):
        ins, outs, (send_sems, recv_sems) = refs[:n], refs[n:2 * n], refs[2 * n:]
        x, y, c = _place()
        copies = [_remote(ins[a], outs[a], send_sems.at[a], recv_sems.at[a], (x, y, 1 - c)) for a in range(n)]
        for cp in copies:
            cp.start()
        for cp in copies:
            cp.wait_recv()
        for cp in copies:
            cp.wait_send()

    shapes = [jax.ShapeDtypeStruct(s.shape, s.dtype) for s in arrays]
    return _comm_call(body, arrays, shapes, n, 0, name)


def _gather_all(tile, *, name):
    def body(in_ref, out_ref, send_sems, recv_sems, local_sems):
        x, y, c = _place()
        mine = 4 * x + 2 * y + c
        local = pltpu.make_async_copy(in_ref, out_ref.at[mine], local_sems.at[0])
        remote, landing = [], []
        for k in range(1, N_DEV):
            peer = (_flip(x, k & 4), _flip(y, k & 2), _flip(c, k & 1))
            sems = send_sems.at[k - 1], recv_sems.at[k - 1]
            remote.append(_remote(in_ref, out_ref.at[mine], *sems, peer))
            landing.append(_remote(in_ref, out_ref.at[4 * peer[0] + 2 * peer[1] + peer[2]], *sems, peer))
        for cp in [local] + remote:
            cp.start()
        local.wait()
        for cp in landing:
            cp.wait_recv()
        for cp in remote:
            cp.wait_send()

    return _comm_call(body, [tile], [jax.ShapeDtypeStruct((N_DEV,) + tile.shape, tile.dtype)], N_DEV - 1, 1, name)[0]


def _sum_slots(slabs, *, name, tr=128):
    s, r, c = slabs.shape
    tr = min(tr, r)

    def body(in_ref, o_ref):
        acc = in_ref[0].astype(F32)
        for j in range(1, s):
            acc = acc + in_ref[j].astype(F32)
        o_ref[...] = acc

    return pl.pallas_call(
        body, name=name, grid=(r // tr,),
        in_specs=[pl.BlockSpec((s, tr, c), lambda i: (0, i, 0))], out_specs=pl.BlockSpec((tr, c), lambda i: (i, 0)),
        out_shape=jax.ShapeDtypeStruct((r, c), F32), compiler_params=_params("parallel"))(slabs)


def _adamw_math(w, g, m, v):
    m = ADAM_B1 * m + (1.0 - ADAM_B1) * g
    v = ADAM_B2 * v + (1.0 - ADAM_B2) * (g * g)
    m_hat = m / (1.0 - ADAM_B1 ** ADAM_STEP)
    v_hat = v / (1.0 - ADAM_B2 ** ADAM_STEP)
    delta = -ADAM_LR * (m_hat / (jnp.sqrt(v_hat) + ADAM_EPS) + ADAM_WD * w)
    return delta, m, v


def _adamw(w, m, v, parts, *, name, tr=128):
    r, c = w.shape
    tr = min(tr, r)
    s = len(parts)

    def body(w_ref, m_ref, v_ref, *refs):
        g_ref, d_ref, nm_ref, nv_ref = refs[s:]
        g = refs[0][...]
        for p_ref in refs[1:s]:
            g = g + p_ref[...]
        g_ref[...] = g
        d_ref[...], nm_ref[...], nv_ref[...] = _adamw_math(w_ref[...], g, m_ref[...], v_ref[...])

    blk = pl.BlockSpec((tr, c), lambda i: (i, 0))
    shape = jax.ShapeDtypeStruct((r, c), F32)
    return pl.pallas_call(
        body, name=name, grid=(r // tr,), in_specs=[blk] * (3 + s), out_specs=[blk] * 4, out_shape=[shape] * 4,
        compiler_params=_params("parallel"))(w, m, v, *parts)


_BIG = ("a_w_in", "b_w_in", "a_w_out", "b_w_out", "a_conv_w")
_SMALL = ("norm_w", "a_a_log", "a_dt_bias", "a_out_norm_w", "b_q_norm_w", "b_k_norm_w", "b_rel_bias")
_ORDER = ("norm_w", "a_w_in", "a_conv_w", "a_a_log", "a_dt_bias", "a_out_norm_w", "a_w_out", "b_w_in", "b_q_norm_w",
          "b_k_norm_w", "b_rel_bias", "b_w_out")


def _join_cols(g):
    return jnp.transpose(g, (1, 0, 2)).reshape(g.shape[1], -1)


def _split_cols(g):
    return jnp.transpose(g.reshape(g.shape[0], N_CHIPS, -1), (1, 0, 2))


def _split_rows(g):
    return g.reshape(N_CHIPS, -1, g.shape[-1])


PACK_ROWS = 8


def _pack(d):
    flat = jnp.concatenate([d[n].reshape(-1) for n in _SMALL])
    return jnp.pad(flat, (0, -flat.shape[0] % (PACK_ROWS * LANES))).reshape(PACK_ROWS, -1)


def _unpack(tile, like):
    flat, out, at = tile.reshape(-1), {}, 0
    for n in _SMALL:
        size = like[n].size
        out[n] = flat[at:at + size].reshape(like[n].shape)
        at += size
    return out


def kernel(x, norm_w, a_w_in, a_conv_w, a_a_log, a_dt_bias, a_out_norm_w, a_w_out, b_w_in, b_q_norm_w, b_k_norm_w, b_rel_bias, b_w_out, loss_target, m_norm_w, m_a_w_in, m_a_conv_w, m_a_a_log, m_a_dt_bias, m_a_out_norm_w, m_a_w_out, m_b_w_in, m_b_q_norm_w, m_b_k_norm_w, m_b_rel_bias, m_b_w_out, v_norm_w, v_a_w_in, v_a_conv_w, v_a_a_log, v_a_dt_bias, v_a_out_norm_w, v_a_w_out, v_b_w_in, v_b_q_norm_w, v_b_k_norm_w, v_b_rel_bias, v_b_w_out):
    w = dict(norm_w=norm_w, a_w_in=a_w_in, a_conv_w=a_conv_w, a_a_log=a_a_log, a_dt_bias=a_dt_bias,
             a_out_norm_w=a_out_norm_w, a_w_out=a_w_out, b_w_in=b_w_in, b_q_norm_w=b_q_norm_w, b_k_norm_w=b_k_norm_w,
             b_rel_bias=b_rel_bias, b_w_out=b_w_out)
    m = dict(norm_w=m_norm_w, a_w_in=m_a_w_in, a_conv_w=m_a_conv_w, a_a_log=m_a_a_log, a_dt_bias=m_a_dt_bias,
             a_out_norm_w=m_a_out_norm_w, a_w_out=m_a_w_out, b_w_in=m_b_w_in, b_q_norm_w=m_b_q_norm_w,
             b_k_norm_w=m_b_k_norm_w, b_rel_bias=m_b_rel_bias, b_w_out=m_b_w_out)
    v = dict(norm_w=v_norm_w, a_w_in=v_a_w_in, a_conv_w=v_a_conv_w, a_a_log=v_a_a_log, a_dt_bias=v_a_dt_bias,
             a_out_norm_w=v_a_out_norm_w, a_w_out=v_a_w_out, b_w_in=v_b_w_in, b_q_norm_w=v_b_q_norm_w,
             b_k_norm_w=v_b_k_norm_w, b_rel_bias=v_b_rel_bias, b_w_out=v_b_w_out)

    wa_in, conv = _gather_shared(a_w_in[0].astype(BF16), a_conv_w[0], name="gather_a_in")
    loss, dx, grads = _local_step(
        x[0], loss_target[0], norm_w, _join_cols(wa_in), _join_cols(conv), a_a_log, a_dt_bias, a_out_norm_w,
        a_w_out[0].astype(BF16), b_w_in[0].astype(BF16), b_q_norm_w, b_k_norm_w, b_rel_bias, b_w_out[0].astype(BF16),
        sharded=True)
    loss = lax.psum(loss, ("x", "y", "c"))

    late = [n for n in _BIG if not isinstance(grads[n], tuple)]
    mine = [_sum_slots(grads[n], name=f"chip_sum_{n}") for n in late]
    sums = {n: grads[n] for n in _BIG if n not in late}
    sums.update(zip(late, zip(mine, _swap_pair(mine, name="pair_grads"))))
    out = {}
    for n in _BIG:
        out[n] = [r[None] for r in _adamw(w[n][0], m[n][0], v[n][0], list(sums[n]), name=f"adamw_{n}")]

    tiles = _gather_all(_pack(grads), name="gather_small_grads")
    res = _adamw(_pack(w), _pack(m), _pack(v), [tiles[d] for d in range(N_DEV)], name="adamw_small")
    unpacked = [_unpack(r, w) for r in res]
    for n in _SMALL:
        out[n] = [u[n] for u in unpacked]

    return (loss, dx[None], *[out[n][0] for n in _ORDER], *[out[n][1] for n in _ORDER], *[out[n][2] for n in _ORDER],
            *[out[n][3] for n in _ORDER])
```
